```python
import math
import jax, jax.numpy as jnp
from jax import lax
import numpy as np

D_MODEL = 2048
BATCH = 8
SEQ = 2048
DEPTH = 4

CHUNK = 64
N_MIXERS = 2
EPS = 1e-6

GLA_HEADS = 4
GLA_KEY_WIDTH = D_MODEL // 2
GLA_VAL_WIDTH = D_MODEL
GLA_DK = GLA_KEY_WIDTH // GLA_HEADS
GLA_DV = GLA_VAL_WIDTH // GLA_HEADS
GLA_GATE_RANK = 16
GLA_GATE_TEMP = 16.0
GLA_IN_WIDTH = 2 * GLA_KEY_WIDTH + 2 * GLA_VAL_WIDTH + GLA_GATE_RANK

S5_WIDTH = D_MODEL // 2
S5_GROUP = 16
S5_GROUPS = S5_WIDTH // S5_GROUP
S5_STATE = 64
S5_DT_MIN = 1e-3
S5_DT_MAX = 1e-1
S5_EIG_CLIP = -1e-4

MLP_HIDDEN = 4 * D_MODEL

kernel_name = "hybrid_gla_s5_stream_block"


def _rmsnorm(x, g):
    xf = x.astype(jnp.float32)
    y = xf * lax.rsqrt(jnp.mean(xf * xf, axis=-1, keepdims=True) + EPS)
    return (y * g.astype(jnp.float32)).astype(x.dtype)


def _gla_mixer(h, w_in, w_gate_up, b_gate, o_norm, w_out):
    bsz, seq, _ = h.shape
    nc = seq // CHUNK
    proj = h @ w_in
    q, k, v, r, g_low = jnp.split(
        proj,
        [GLA_KEY_WIDTH, 2 * GLA_KEY_WIDTH, 2 * GLA_KEY_WIDTH + GLA_VAL_WIDTH,
         2 * GLA_KEY_WIDTH + 2 * GLA_VAL_WIDTH], axis=-1)
    log_a = jax.nn.log_sigmoid((g_low @ w_gate_up + b_gate).astype(jnp.float32)) / GLA_GATE_TEMP

    def to_chunks(t, dh):
        return t.reshape(bsz, nc, CHUNK, GLA_HEADS, dh).transpose(1, 0, 3, 2, 4).astype(jnp.float32)

    qc = to_chunks(q, GLA_DK) * (GLA_DK ** -0.5)
    kc = to_chunks(k, GLA_DK)
    vc = to_chunks(v, GLA_DV)
    ac = to_chunks(log_a, GLA_DK)
    cum = jnp.cumsum(ac, axis=3)
    total = cum[:, :, :, -1:, :]
    k_dec = kc * jnp.exp(total - cum)
    chunk_decay = jnp.exp(total[:, :, :, 0, :])

    def step(state, xs):
        q_c, k_c, v_c, d_c = xs
        state = d_c[..., None] * state + jnp.einsum('bhck,bhcv->bhkv', k_c, v_c)
        return state, jnp.einsum('bhck,bhkv->bhcv', q_c, state)

    s0 = jnp.zeros((bsz, GLA_HEADS, GLA_DK, GLA_DV), jnp.float32)
    _, o = lax.scan(step, s0, (qc, k_dec, vc, chunk_decay))
    o = o * lax.rsqrt(jnp.mean(o * o, axis=-1, keepdims=True) + EPS) * o_norm.astype(jnp.float32)
    o = o.transpose(1, 0, 3, 2, 4).reshape(bsz, seq, GLA_VAL_WIDTH).astype(h.dtype)
    return (o * jax.nn.silu(r)) @ w_out


def _s5_mixer(h, w_in, lam_re, lam_im, log_dt, b_re, b_im, c_re, c_im, d_skip, w_out):
    bsz, seq, _ = h.shape
    u = (h @ w_in).astype(jnp.float32)
    ug = u.reshape(bsz, seq, S5_GROUPS, S5_GROUP)
    lr = jnp.minimum(lam_re.astype(jnp.float32), S5_EIG_CLIP)
    li = lam_im.astype(jnp.float32)
    dt = jnp.exp(log_dt.astype(jnp.float32))[:, None]
    mag = jnp.exp(lr * dt)
    ang = li * dt
    ab_re = mag * jnp.cos(ang)
    ab_im = mag * jnp.sin(ang)
    den = lr * lr + li * li
    nr = ab_re - 1.0
    f_re = (nr * lr + ab_im * li) / den
    f_im = (ab_im * lr - nr * li) / den
    br = b_re.astype(jnp.float32)
    bi = b_im.astype(jnp.float32)
    bb_re = f_re[..., None] * br - f_im[..., None] * bi
    bb_im = f_re[..., None] * bi + f_im[..., None] * br
    bu_re = jnp.einsum('gnc,blgc->blgn', bb_re, ug)
    bu_im = jnp.einsum('gnc,blgc->blgn', bb_im, ug)
    a_re = jnp.broadcast_to(ab_re, bu_re.shape)
    a_im = jnp.broadcast_to(ab_im, bu_im.shape)

    def combine(e1, e2):
        a1r, a1i, b1r, b1i = e1
        a2r, a2i, b2r, b2i = e2
        return (a2r * a1r - a2i * a1i,
                a2r * a1i + a2i * a1r,
                a2r * b1r - a2i * b1i + b2r,
                a2r * b1i + a2i * b1r + b2i)

    _, _, x_re, x_im = lax.associative_scan(combine, (a_re, a_im, bu_re, bu_im), axis=1)
    y = (jnp.einsum('gcn,blgn->blgc', c_re.astype(jnp.float32), x_re)
         - jnp.einsum('gcn,blgn->blgc', c_im.astype(jnp.float32), x_im))
    y = y.reshape(bsz, seq, S5_WIDTH) + d_skip.astype(jnp.float32) * u
    y = jax.nn.gelu(y).astype(h.dtype)
    val, gate = jnp.split(y @ w_out, 2, axis=-1)
    return val * jax.nn.sigmoid(gate)


def _sq_relu_mlp(h, w_up, w_down):
    a = jax.nn.relu(h @ w_up)
    return (a * a) @ w_down


def _fwd_setup_inputs(seed: int = 0) -> dict:
    key = jax.random.key(seed)
    ks = jax.random.split(key, 24)
    n_gla = len(range(0, DEPTH, N_MIXERS))
    n_s5 = len(range(1, DEPTH, N_MIXERS))
    res_scale = (2 * DEPTH) ** -0.5

    def nrm(k, shape, scale):
        return jax.random.normal(k, shape, jnp.float32) * scale

    def gain(k, shape):
        return 1.0 + 0.02 * jax.random.normal(k, shape, jnp.float32)

    lam_im0 = math.pi * jnp.arange(S5_STATE, dtype=jnp.float32)
    return {
        'x': jax.random.normal(ks[0], (BATCH, SEQ, D_MODEL), jnp.float32),
        'gla_norm': gain(ks[1], (n_gla, D_MODEL)),
        'gla_w_in': nrm(ks[2], (n_gla, D_MODEL, GLA_IN_WIDTH), D_MODEL ** -0.5),
        'gla_w_gate_up': nrm(ks[3], (n_gla, GLA_GATE_RANK, GLA_KEY_WIDTH), GLA_GATE_RANK ** -0.5),
        'gla_b_gate': 1.0 + 0.1 * jax.random.normal(ks[4], (n_gla, GLA_KEY_WIDTH), jnp.float32),
        'gla_o_norm': gain(ks[5], (n_gla, GLA_DV)),
        'gla_w_out': nrm(ks[6], (n_gla, GLA_VAL_WIDTH, D_MODEL), GLA_VAL_WIDTH ** -0.5 * res_scale),
        's5_norm': gain(ks[7], (n_s5, D_MODEL)),
        's5_w_in': nrm(ks[8], (n_s5, D_MODEL, S5_WIDTH), D_MODEL ** -0.5),
        's5_lam_re': -0.5 + 0.01 * jax.random.normal(ks[9], (n_s5, S5_GROUPS, S5_STATE), jnp.float32),
        's5_lam_im': lam_im0 + 0.01 * jax.random.normal(ks[10], (n_s5, S5_GROUPS, S5_STATE), jnp.float32),
        's5_log_dt': jax.random.uniform(ks[11], (n_s5, S5_GROUPS), jnp.float32,
                                        minval=math.log(S5_DT_MIN), maxval=math.log(S5_DT_MAX)),
        's5_b_re': nrm(ks[12], (n_s5, S5_GROUPS, S5_STATE, S5_GROUP), (2 * S5_GROUP) ** -0.5),
        's5_b_im': nrm(ks[13], (n_s5, S5_GROUPS, S5_STATE, S5_GROUP), (2 * S5_GROUP) ** -0.5),
        's5_c_re': nrm(ks[14], (n_s5, S5_GROUPS, S5_GROUP, S5_STATE), (2 * S5_STATE) ** -0.5),
        's5_c_im': nrm(ks[15], (n_s5, S5_GROUPS, S5_GROUP, S5_STATE), (2 * S5_STATE) ** -0.5),
        's5_d': jax.random.normal(ks[16], (n_s5, S5_WIDTH), jnp.float32),
        's5_w_out': nrm(ks[17], (n_s5, S5_WIDTH, 2 * D_MODEL), S5_WIDTH ** -0.5 * res_scale),
        'mlp_norm': gain(ks[18], (DEPTH, D_MODEL)),
        'mlp_w_up': nrm(ks[19], (DEPTH, D_MODEL, MLP_HIDDEN), D_MODEL ** -0.5),
        'mlp_w_down': nrm(ks[20], (DEPTH, MLP_HIDDEN, D_MODEL), MLP_HIDDEN ** -0.5 * res_scale),
        'final_norm': gain(ks[21], (D_MODEL,)),
    }


def _fwd_reference(x, gla_norm, gla_w_in, gla_w_gate_up, gla_b_gate, gla_o_norm, gla_w_out,
              s5_norm, s5_w_in, s5_lam_re, s5_lam_im, s5_log_dt, s5_b_re, s5_b_im,
              s5_c_re, s5_c_im, s5_d, s5_w_out, mlp_norm, mlp_w_up, mlp_w_down, final_norm):
    h = x
    for i in range(DEPTH):
        j = i // N_MIXERS
        if i % N_MIXERS == 0:
            h = h + _gla_mixer(_rmsnorm(h, gla_norm[j]), gla_w_in[j], gla_w_gate_up[j],
                               gla_b_gate[j], gla_o_norm[j], gla_w_out[j])
        else:
            h = h + _s5_mixer(_rmsnorm(h, s5_norm[j]), s5_w_in[j], s5_lam_re[j], s5_lam_im[j],
                              s5_log_dt[j], s5_b_re[j], s5_b_im[j], s5_c_re[j], s5_c_im[j],
                              s5_d[j], s5_w_out[j])
        h = h + _sq_relu_mlp(_rmsnorm(h, mlp_norm[i]), mlp_w_up[i], mlp_w_down[i])
    return _rmsnorm(h, final_norm)


import jax as _jax
import jax.numpy as _jnp

TWIN_FORMAT = 'train_step'
FWD_PARAMS = ['x', 'gla_norm', 'gla_w_in', 'gla_w_gate_up', 'gla_b_gate', 'gla_o_norm', 'gla_w_out', 's5_norm', 's5_w_in', 's5_lam_re', 's5_lam_im', 's5_log_dt', 's5_b_re', 's5_b_im', 's5_c_re', 's5_c_im', 's5_d', 's5_w_out', 'mlp_norm', 'mlp_w_up', 'mlp_w_down', 'final_norm']
TWIN_WEIGHTS = ['gla_norm', 'gla_w_in', 'gla_w_gate_up', 'gla_b_gate', 'gla_o_norm', 'gla_w_out', 's5_norm', 's5_w_in', 's5_lam_re', 's5_lam_im', 's5_log_dt', 's5_b_re', 's5_b_im', 's5_c_re', 's5_c_im', 's5_d', 's5_w_out', 'mlp_norm', 'mlp_w_up', 'mlp_w_down', 'final_norm']
TWIN_DIFF_INPUT = 'x'
TWIN_INPUTS = ['x', 'gla_norm', 'gla_w_in', 'gla_w_gate_up', 'gla_b_gate', 'gla_o_norm', 'gla_w_out', 's5_norm', 's5_w_in', 's5_lam_re', 's5_lam_im', 's5_log_dt', 's5_b_re', 's5_b_im', 's5_c_re', 's5_c_im', 's5_d', 's5_w_out', 'mlp_norm', 'mlp_w_up', 'mlp_w_down', 'final_norm', 'loss_target', 'm_gla_norm', 'm_gla_w_in', 'm_gla_w_gate_up', 'm_gla_b_gate', 'm_gla_o_norm', 'm_gla_w_out', 'm_s5_norm', 'm_s5_w_in', 'm_s5_lam_re', 'm_s5_lam_im', 'm_s5_log_dt', 'm_s5_b_re', 'm_s5_b_im', 'm_s5_c_re', 'm_s5_c_im', 'm_s5_d', 'm_s5_w_out', 'm_mlp_norm', 'm_mlp_w_up', 'm_mlp_w_down', 'm_final_norm', 'v_gla_norm', 'v_gla_w_in', 'v_gla_w_gate_up', 'v_gla_b_gate', 'v_gla_o_norm', 'v_gla_w_out', 'v_s5_norm', 'v_s5_w_in', 'v_s5_lam_re', 'v_s5_lam_im', 'v_s5_log_dt', 'v_s5_b_re', 'v_s5_b_im', 'v_s5_c_re', 'v_s5_c_im', 'v_s5_d', 'v_s5_w_out', 'v_mlp_norm', 'v_mlp_w_up', 'v_mlp_w_down', 'v_final_norm']
TWIN_OUTPUTS = ['loss', 'grad_x', 'grad_gla_norm', 'grad_gla_w_in', 'grad_gla_w_gate_up', 'grad_gla_b_gate', 'grad_gla_o_norm', 'grad_gla_w_out', 'grad_s5_norm', 'grad_s5_w_in', 'grad_s5_lam_re', 'grad_s5_lam_im', 'grad_s5_log_dt', 'grad_s5_b_re', 'grad_s5_b_im', 'grad_s5_c_re', 'grad_s5_c_im', 'grad_s5_d', 'grad_s5_w_out', 'grad_mlp_norm', 'grad_mlp_w_up', 'grad_mlp_w_down', 'grad_final_norm', 'delta_gla_norm', 'delta_gla_w_in', 'delta_gla_w_gate_up', 'delta_gla_b_gate', 'delta_gla_o_norm', 'delta_gla_w_out', 'delta_s5_norm', 'delta_s5_w_in', 'delta_s5_lam_re', 'delta_s5_lam_im', 'delta_s5_log_dt', 'delta_s5_b_re', 'delta_s5_b_im', 'delta_s5_c_re', 'delta_s5_c_im', 'delta_s5_d', 'delta_s5_w_out', 'delta_mlp_norm', 'delta_mlp_w_up', 'delta_mlp_w_down', 'delta_final_norm', 'new_m_gla_norm', 'new_m_gla_w_in', 'new_m_gla_w_gate_up', 'new_m_gla_b_gate', 'new_m_gla_o_norm', 'new_m_gla_w_out', 'new_m_s5_norm', 'new_m_s5_w_in', 'new_m_s5_lam_re', 'new_m_s5_lam_im', 'new_m_s5_log_dt', 'new_m_s5_b_re', 'new_m_s5_b_im', 'new_m_s5_c_re', 'new_m_s5_c_im', 'new_m_s5_d', 'new_m_s5_w_out', 'new_m_mlp_norm', 'new_m_mlp_w_up', 'new_m_mlp_w_down', 'new_m_final_norm', 'new_v_gla_norm', 'new_v_gla_w_in', 'new_v_gla_w_gate_up', 'new_v_gla_b_gate', 'new_v_gla_o_norm', 'new_v_gla_w_out', 'new_v_s5_norm', 'new_v_s5_w_in', 'new_v_s5_lam_re', 'new_v_s5_lam_im', 'new_v_s5_log_dt', 'new_v_s5_b_re', 'new_v_s5_b_im', 'new_v_s5_c_re', 'new_v_s5_c_im', 'new_v_s5_d', 'new_v_s5_w_out', 'new_v_mlp_norm', 'new_v_mlp_w_up', 'new_v_mlp_w_down', 'new_v_final_norm']
TWIN_LEAF_KINDS = {'loss': 'loss', 'grad_x': 'grad_x', 'grad_gla_norm': 'grad_w', 'grad_gla_w_in': 'grad_w', 'grad_gla_w_gate_up': 'grad_w', 'grad_gla_b_gate': 'grad_w', 'grad_gla_o_norm': 'grad_w', 'grad_gla_w_out': 'grad_w', 'grad_s5_norm': 'grad_w', 'grad_s5_w_in': 'grad_w', 'grad_s5_lam_re': 'grad_w', 'grad_s5_lam_im': 'grad_w', 'grad_s5_log_dt': 'grad_w', 'grad_s5_b_re': 'grad_w', 'grad_s5_b_im': 'grad_w', 'grad_s5_c_re': 'grad_w', 'grad_s5_c_im': 'grad_w', 'grad_s5_d': 'grad_w', 'grad_s5_w_out': 'grad_w', 'grad_mlp_norm': 'grad_w', 'grad_mlp_w_up': 'grad_w', 'grad_mlp_w_down': 'grad_w', 'grad_final_norm': 'grad_w', 'delta_gla_norm': 'delta_w', 'delta_gla_w_in': 'delta_w', 'delta_gla_w_gate_up': 'delta_w', 'delta_gla_b_gate': 'delta_w', 'delta_gla_o_norm': 'delta_w', 'delta_gla_w_out': 'delta_w', 'delta_s5_norm': 'delta_w', 'delta_s5_w_in': 'delta_w', 'delta_s5_lam_re': 'delta_w', 'delta_s5_lam_im': 'delta_w', 'delta_s5_log_dt': 'delta_w', 'delta_s5_b_re': 'delta_w', 'delta_s5_b_im': 'delta_w', 'delta_s5_c_re': 'delta_w', 'delta_s5_c_im': 'delta_w', 'delta_s5_d': 'delta_w', 'delta_s5_w_out': 'delta_w', 'delta_mlp_norm': 'delta_w', 'delta_mlp_w_up': 'delta_w', 'delta_mlp_w_down': 'delta_w', 'delta_final_norm': 'delta_w', 'new_m_gla_norm': 'new_m', 'new_m_gla_w_in': 'new_m', 'new_m_gla_w_gate_up': 'new_m', 'new_m_gla_b_gate': 'new_m', 'new_m_gla_o_norm': 'new_m', 'new_m_gla_w_out': 'new_m', 'new_m_s5_norm': 'new_m', 'new_m_s5_w_in': 'new_m', 'new_m_s5_lam_re': 'new_m', 'new_m_s5_lam_im': 'new_m', 'new_m_s5_log_dt': 'new_m', 'new_m_s5_b_re': 'new_m', 'new_m_s5_b_im': 'new_m', 'new_m_s5_c_re': 'new_m', 'new_m_s5_c_im': 'new_m', 'new_m_s5_d': 'new_m', 'new_m_s5_w_out': 'new_m', 'new_m_mlp_norm': 'new_m', 'new_m_mlp_w_up': 'new_m', 'new_m_mlp_w_down': 'new_m', 'new_m_final_norm': 'new_m', 'new_v_gla_norm': 'new_v', 'new_v_gla_w_in': 'new_v', 'new_v_gla_w_gate_up': 'new_v', 'new_v_gla_b_gate': 'new_v', 'new_v_gla_o_norm': 'new_v', 'new_v_gla_w_out': 'new_v', 'new_v_s5_norm': 'new_v', 'new_v_s5_w_in': 'new_v', 'new_v_s5_lam_re': 'new_v', 'new_v_s5_lam_im': 'new_v', 'new_v_s5_log_dt': 'new_v', 'new_v_s5_b_re': 'new_v', 'new_v_s5_b_im': 'new_v', 'new_v_s5_c_re': 'new_v', 'new_v_s5_c_im': 'new_v', 'new_v_s5_d': 'new_v', 'new_v_s5_w_out': 'new_v', 'new_v_mlp_norm': 'new_v', 'new_v_mlp_w_up': 'new_v', 'new_v_mlp_w_down': 'new_v', 'new_v_final_norm': 'new_v'}


def _forward(args):
    return _fwd_reference(*[args[k] for k in FWD_PARAMS])


def _output_shape():
    out = _jax.eval_shape(lambda: _forward(_fwd_setup_inputs(0)))
    return out.shape, out.dtype

N_MICROBATCH = 1
ADAM_LR = 0.001
ADAM_B1 = 0.9
ADAM_B2 = 0.999
ADAM_EPS = 1e-08
ADAM_WD = 0.01
ADAM_STEP = 10
PER_EXAMPLE_BATCH_AXIS = {'x': 0, 'loss_target': 0}
SHARED_INPUTS = []
_WEIGHT_DTYPES = {'gla_norm': _jnp.float32, 'gla_w_in': _jnp.float32, 'gla_w_gate_up': _jnp.float32, 'gla_b_gate': _jnp.float32, 'gla_o_norm': _jnp.float32, 'gla_w_out': _jnp.float32, 's5_norm': _jnp.float32, 's5_w_in': _jnp.float32, 's5_lam_re': _jnp.float32, 's5_lam_im': _jnp.float32, 's5_log_dt': _jnp.float32, 's5_b_re': _jnp.float32, 's5_b_im': _jnp.float32, 's5_c_re': _jnp.float32, 's5_c_im': _jnp.float32, 's5_d': _jnp.float32, 's5_w_out': _jnp.float32, 'mlp_norm': _jnp.float32, 'mlp_w_up': _jnp.float32, 'mlp_w_down': _jnp.float32, 'final_norm': _jnp.float32}
MOMENT_SCALE = {'gla_norm': 2.682904e-02, 'gla_w_in': 1.540689e-02, 'gla_w_gate_up': 2.942353e-03, 'gla_b_gate': 1.105763e-02, 'gla_o_norm': 2.673375e-02, 'gla_w_out': 3.697166e-02, 's5_norm': 7.187976e-03, 's5_w_in': 9.777785e-03, 's5_lam_re': 5.627060e-04, 's5_lam_im': 6.298593e-04, 's5_log_dt': 2.746518e-01, 's5_b_re': 3.734184e-04, 's5_b_im': 3.530395e-04, 's5_c_re': 7.245008e-04, 's5_c_im': 7.024528e-04, 's5_d': 1.068114e-02, 's5_w_out': 1.452395e-02, 'mlp_norm': 2.729588e-02, 'mlp_w_up': 1.353214e-02, 'mlp_w_down': 7.082419e-02, 'final_norm': 8.049166e+00}


def _to_microbatches(a, axis):
    t = _jnp.moveaxis(a, axis, 0)
    t = t.reshape((N_MICROBATCH, t.shape[0] // N_MICROBATCH) + t.shape[1:])
    return _jnp.moveaxis(t, 1, axis + 1)


def setup_inputs(seed: int = 0) -> dict:
    inp = _fwd_setup_inputs(seed)
    key = _jax.random.fold_in(_jax.random.key(seed), 7919)
    shape, _ = _output_shape()
    out = dict(inp)
    out["loss_target"] = _jax.random.normal(_jax.random.fold_in(key, 0), shape, _jnp.float32)
    for i, name in enumerate(TWIN_WEIGHTS):
        w = inp[name].astype(_jnp.float32)
        if MOMENT_SCALE is None:
            s = _jnp.sqrt(_jnp.mean(_jnp.square(w)) + 1e-30)
        else:
            s = MOMENT_SCALE[name]
        km, kv = _jax.random.split(_jax.random.fold_in(key, i + 1))
        out[name] = w
        out["m_" + name] = s * _jax.random.normal(km, w.shape, _jnp.float32)
        out["v_" + name] = (s * s) * _jax.random.uniform(kv, w.shape, _jnp.float32, 0.5, 1.5)
    if N_MICROBATCH > 1:
        for name, axis in PER_EXAMPLE_BATCH_AXIS.items():
            out[name] = _to_microbatches(out[name], axis)
    return {'x': out['x'], 'gla_norm': out['gla_norm'], 'gla_w_in': out['gla_w_in'], 'gla_w_gate_up': out['gla_w_gate_up'], 'gla_b_gate': out['gla_b_gate'], 'gla_o_norm': out['gla_o_norm'], 'gla_w_out': out['gla_w_out'], 's5_norm': out['s5_norm'], 's5_w_in': out['s5_w_in'], 's5_lam_re': out['s5_lam_re'], 's5_lam_im': out['s5_lam_im'], 's5_log_dt': out['s5_log_dt'], 's5_b_re': out['s5_b_re'], 's5_b_im': out['s5_b_im'], 's5_c_re': out['s5_c_re'], 's5_c_im': out['s5_c_im'], 's5_d': out['s5_d'], 's5_w_out': out['s5_w_out'], 'mlp_norm': out['mlp_norm'], 'mlp_w_up': out['mlp_w_up'], 'mlp_w_down': out['mlp_w_down'], 'final_norm': out['final_norm'], 'loss_target': out['loss_target'], 'm_gla_norm': out['m_gla_norm'], 'm_gla_w_in': out['m_gla_w_in'], 'm_gla_w_gate_up': out['m_gla_w_gate_up'], 'm_gla_b_gate': out['m_gla_b_gate'], 'm_gla_o_norm': out['m_gla_o_norm'], 'm_gla_w_out': out['m_gla_w_out'], 'm_s5_norm': out['m_s5_norm'], 'm_s5_w_in': out['m_s5_w_in'], 'm_s5_lam_re': out['m_s5_lam_re'], 'm_s5_lam_im': out['m_s5_lam_im'], 'm_s5_log_dt': out['m_s5_log_dt'], 'm_s5_b_re': out['m_s5_b_re'], 'm_s5_b_im': out['m_s5_b_im'], 'm_s5_c_re': out['m_s5_c_re'], 'm_s5_c_im': out['m_s5_c_im'], 'm_s5_d': out['m_s5_d'], 'm_s5_w_out': out['m_s5_w_out'], 'm_mlp_norm': out['m_mlp_norm'], 'm_mlp_w_up': out['m_mlp_w_up'], 'm_mlp_w_down': out['m_mlp_w_down'], 'm_final_norm': out['m_final_norm'], 'v_gla_norm': out['v_gla_norm'], 'v_gla_w_in': out['v_gla_w_in'], 'v_gla_w_gate_up': out['v_gla_w_gate_up'], 'v_gla_b_gate': out['v_gla_b_gate'], 'v_gla_o_norm': out['v_gla_o_norm'], 'v_gla_w_out': out['v_gla_w_out'], 'v_s5_norm': out['v_s5_norm'], 'v_s5_w_in': out['v_s5_w_in'], 'v_s5_lam_re': out['v_s5_lam_re'], 'v_s5_lam_im': out['v_s5_lam_im'], 'v_s5_log_dt': out['v_s5_log_dt'], 'v_s5_b_re': out['v_s5_b_re'], 'v_s5_b_im': out['v_s5_b_im'], 'v_s5_c_re': out['v_s5_c_re'], 'v_s5_c_im': out['v_s5_c_im'], 'v_s5_d': out['v_s5_d'], 'v_s5_w_out': out['v_s5_w_out'], 'v_mlp_norm': out['v_mlp_norm'], 'v_mlp_w_up': out['v_mlp_w_up'], 'v_mlp_w_down': out['v_mlp_w_down'], 'v_final_norm': out['v_final_norm']}


def _loss(weights, diff, rest, loss_target):
    with _jax.named_scope("forward"):
        args = {**rest, TWIN_DIFF_INPUT: diff, **{k: w.astype(_WEIGHT_DTYPES[k]) for k, w in weights.items()}}
        y = _forward(args)
    with _jax.named_scope("loss_head"):
        err = _jnp.square(y.astype(_jnp.float32) - loss_target)
        return 0.5 * _jnp.sum(_jnp.mean(err, axis=-1)) if err.ndim else 0.5 * err


def _adamw(w, g, m, v):
    m = ADAM_B1 * m + (1.0 - ADAM_B1) * g
    v = ADAM_B2 * v + (1.0 - ADAM_B2) * _jnp.square(g)
    m_hat = m / (1.0 - ADAM_B1 ** ADAM_STEP)
    v_hat = v / (1.0 - ADAM_B2 ** ADAM_STEP)
    delta = -ADAM_LR * (m_hat / (_jnp.sqrt(v_hat) + ADAM_EPS) + ADAM_WD * w)
    return delta, m, v


def reference(x, gla_norm, gla_w_in, gla_w_gate_up, gla_b_gate, gla_o_norm, gla_w_out, s5_norm, s5_w_in, s5_lam_re, s5_lam_im, s5_log_dt, s5_b_re, s5_b_im, s5_c_re, s5_c_im, s5_d, s5_w_out, mlp_norm, mlp_w_up, mlp_w_down, final_norm, loss_target, m_gla_norm, m_gla_w_in, m_gla_w_gate_up, m_gla_b_gate, m_gla_o_norm, m_gla_w_out, m_s5_norm, m_s5_w_in, m_s5_lam_re, m_s5_lam_im, m_s5_log_dt, m_s5_b_re, m_s5_b_im, m_s5_c_re, m_s5_c_im, m_s5_d, m_s5_w_out, m_mlp_norm, m_mlp_w_up, m_mlp_w_down, m_final_norm, v_gla_norm, v_gla_w_in, v_gla_w_gate_up, v_gla_b_gate, v_gla_o_norm, v_gla_w_out, v_s5_norm, v_s5_w_in, v_s5_lam_re, v_s5_lam_im, v_s5_log_dt, v_s5_b_re, v_s5_b_im, v_s5_c_re, v_s5_c_im, v_s5_d, v_s5_w_out, v_mlp_norm, v_mlp_w_up, v_mlp_w_down, v_final_norm):
    given = dict(x=x, gla_norm=gla_norm, gla_w_in=gla_w_in, gla_w_gate_up=gla_w_gate_up, gla_b_gate=gla_b_gate, gla_o_norm=gla_o_norm, gla_w_out=gla_w_out, s5_norm=s5_norm, s5_w_in=s5_w_in, s5_lam_re=s5_lam_re, s5_lam_im=s5_lam_im, s5_log_dt=s5_log_dt, s5_b_re=s5_b_re, s5_b_im=s5_b_im, s5_c_re=s5_c_re, s5_c_im=s5_c_im, s5_d=s5_d, s5_w_out=s5_w_out, mlp_norm=mlp_norm, mlp_w_up=mlp_w_up, mlp_w_down=mlp_w_down, final_norm=final_norm, loss_target=loss_target, m_gla_norm=m_gla_norm, m_gla_w_in=m_gla_w_in, m_gla_w_gate_up=m_gla_w_gate_up, m_gla_b_gate=m_gla_b_gate, m_gla_o_norm=m_gla_o_norm, m_gla_w_out=m_gla_w_out, m_s5_norm=m_s5_norm, m_s5_w_in=m_s5_w_in, m_s5_lam_re=m_s5_lam_re, m_s5_lam_im=m_s5_lam_im, m_s5_log_dt=m_s5_log_dt, m_s5_b_re=m_s5_b_re, m_s5_b_im=m_s5_b_im, m_s5_c_re=m_s5_c_re, m_s5_c_im=m_s5_c_im, m_s5_d=m_s5_d, m_s5_w_out=m_s5_w_out, m_mlp_norm=m_mlp_norm, m_mlp_w_up=m_mlp_w_up, m_mlp_w_down=m_mlp_w_down, m_final_norm=m_final_norm, v_gla_norm=v_gla_norm, v_gla_w_in=v_gla_w_in, v_gla_w_gate_up=v_gla_w_gate_up, v_gla_b_gate=v_gla_b_gate, v_gla_o_norm=v_gla_o_norm, v_gla_w_out=v_gla_w_out, v_s5_norm=v_s5_norm, v_s5_w_in=v_s5_w_in, v_s5_lam_re=v_s5_lam_re, v_s5_lam_im=v_s5_lam_im, v_s5_log_dt=v_s5_log_dt, v_s5_b_re=v_s5_b_re, v_s5_b_im=v_s5_b_im, v_s5_c_re=v_s5_c_re, v_s5_c_im=v_s5_c_im, v_s5_d=v_s5_d, v_s5_w_out=v_s5_w_out, v_mlp_norm=v_mlp_norm, v_mlp_w_up=v_mlp_w_up, v_mlp_w_down=v_mlp_w_down, v_final_norm=v_final_norm)
    weights = {n: given[n] for n in TWIN_WEIGHTS}
    shared = {n: given[n] for n in SHARED_INPUTS}
    per_example = {n: given[n] for n in ['x']}
    grad_fn = _jax.value_and_grad(_loss, argnums=(0, 1))

    def one_microbatch(ex, loss_target):
        ex = dict(ex)
        diff = ex.pop(TWIN_DIFF_INPUT)
        return grad_fn(weights, diff, {**shared, **ex}, loss_target)

    if N_MICROBATCH == 1:
        loss, (grad_w, grad_x) = one_microbatch(per_example, given["loss_target"])
    else:
        def body(carry, xs):
            loss_sum, grad_sum = carry
            l_k, (gw_k, gx_k) = one_microbatch(xs[0], xs[1])
            with _jax.named_scope("update"):
                return (loss_sum + l_k, _jax.tree.map(_jnp.add, grad_sum, gw_k)), gx_k

        init = (_jnp.zeros((), _jnp.float32), _jax.tree.map(_jnp.zeros_like, weights))
        (loss, grad_w), grad_x = _jax.lax.scan(body, init, (per_example, given["loss_target"]))
    with _jax.named_scope("update"):
        delta_w, new_m, new_v = {}, {}, {}
        for n in TWIN_WEIGHTS:
            delta_w[n], new_m[n], new_v[n] = _adamw(weights[n], grad_w[n], given["m_" + n], given["v_" + n])
    return (loss, grad_x, *[grad_w[n] for n in TWIN_WEIGHTS], *[delta_w[n] for n in TWIN_WEIGHTS],
            *[new_m[n] for n in TWIN_WEIGHTS], *[new_v[n] for n in TWIN_WEIGHTS])
```

```python
import functools
import math

import jax
import jax.numpy as jnp
from jax import lax
from jax.experimental import pallas as pl
from jax.experimental.pallas import tpu as pltpu

F32 = jnp.float32
BF16 = jnp.bfloat16

EPS = 1e-6
CHUNK = 64
GLA_GATE_TEMP = 16.0
S5_EIG_CLIP = -1e-4
S5_SEGMENTS = 8
S5_GROUPS_PER_BLOCK = 8
LANES = 128
ADAM_LR = 0.001
ADAM_B1 = 0.9
ADAM_B2 = 0.999
ADAM_EPS = 1e-08
ADAM_WD = 0.01
ADAM_STEP = 10
VMEM_LIMIT_BYTES = 56 * 1024 * 1024

MESH = pl.DeviceIdType.MESH
ANY = pl.BlockSpec(memory_space=pl.ANY)


def _pcall(body, **kw):
    return pl.pallas_call(body, **kw)


def _params(*sem):
    return pltpu.CompilerParams(dimension_semantics=sem, vmem_limit_bytes=VMEM_LIMIT_BYTES)


def _tile(dim, target, unit=LANES):
    if dim <= target:
        return dim
    best = None
    for t in range(unit, target + 1, unit):
        if dim % t == 0:
            best = t
    assert best is not None, (dim, target)
    return best


def _exchange(x, group, mode, name):
    n = 2 if group == 'c' else 4
    if mode == 'bcast':
        blk = x.shape
    else:
        blk = x.shape[1:]
    if mode == 'a2a':
        assert x.shape[0] == n
    flips = [(0, 0, 1)] if group == 'c' else [(1, 0, 0), (0, 1, 0), (1, 1, 0)]

    def body(x_ref, y_ref, send_sems, recv_sems, local_sem):
        ix, iy, ic = lax.axis_index('x'), lax.axis_index('y'), lax.axis_index('c')

        def slot(px, py, pc):
            return pc if group == 'c' else 2 * px + py

        def src(px, py, pc):
            if mode == 'a2a':
                return x_ref.at[slot(px, py, pc)]
            if mode == 'bcast_c':
                return x_ref.at[ic]
            return x_ref

        me = (ix, iy, ic)
        local = pltpu.make_async_copy(src(*me), y_ref.at[slot(*me)], local_sem)
        local.start()
        peers = []
        for fx, fy, fc in flips:
            peers.append((1 - ix if fx else ix, 1 - iy if fy else iy, 1 - ic if fc else ic))
        sends = []
        for k, peer in enumerate(peers):
            cp = pltpu.make_async_remote_copy(
                src_ref=src(*peer), dst_ref=y_ref.at[slot(*me)],
                send_sem=send_sems.at[k], recv_sem=recv_sems.at[k],
                device_id=peer, device_id_type=MESH)
            cp.start()
            sends.append(cp)
        for k, peer in enumerate(peers):
            pltpu.make_async_remote_copy(
                src_ref=src(*peer), dst_ref=y_ref.at[slot(*peer)],
                send_sem=send_sems.at[k], recv_sem=recv_sems.at[k],
                device_id=peer, device_id_type=MESH).wait_recv()
        for cp in sends:
            cp.wait_send()
        local.wait()

    return _pcall(
        body, name=name,
        out_shape=jax.ShapeDtypeStruct((n,) + tuple(blk), x.dtype),
        in_specs=[ANY], out_specs=ANY,
        scratch_shapes=[pltpu.SemaphoreType.DMA((len(flips),)),
                        pltpu.SemaphoreType.DMA((len(flips),)),
                        pltpu.SemaphoreType.DMA(())],
    )(x)


def _sum_slots(y, out_dtype, name):
    n, rows, cols = y.shape
    tm = _tile(rows, max(8, (1 << 20) // max(cols, 1) // 8 * 8), unit=8)

    def body(y_ref, o_ref):
        acc = y_ref[0].astype(F32)
        for k in range(1, n):
            acc = acc + y_ref[k].astype(F32)
        o_ref[...] = acc.astype(o_ref.dtype)

    return _pcall(
        body, name=name, grid=(rows // tm,),
        out_shape=jax.ShapeDtypeStruct((rows, cols), out_dtype),
        in_specs=[pl.BlockSpec((n, tm, cols), lambda i: (0, i, 0))],
        out_specs=pl.BlockSpec((tm, cols), lambda i: (i, 0)),
        compiler_params=_params('parallel'),
    )(y)


def _gather_weight(w, name):
    nl, rows, cols = w.shape
    nh = nl // 2
    wb = w.astype(BF16).reshape(2, nh, rows, cols)
    y1 = _exchange(wb, 'xy', 'bcast_c', name + '_xy')
    return _exchange(y1, 'c', 'bcast', name + '_c')


def _reduce_scatter(dw, name):
    _, _, rh, cols = dw.shape
    ya = _exchange(dw, 'c', 'a2a', name + '_pair')
    pre = _sum_slots(ya.reshape(2, 4 * rh, cols), BF16, name + '_pairsum').reshape(4, rh, cols)
    yb = _exchange(pre, 'xy', 'a2a', name + '_chips')
    fin = _sum_slots(yb, F32, name + '_chipsum')
    yc = _exchange(fin, 'c', 'bcast', name + '_back')
    return yc.reshape(2 * rh, cols)


class _Op:
    def __init__(self, arr, spec):
        self.arr = arr
        self.spec = spec


def _plain(arr):
    return _Op(arr, lambda t0, t1: ((t0, t1), lambda b0, b1: (b0, b1)))


def _plain_shape(shape):
    return lambda t0, t1: ((t0, t1), lambda b0, b1: (b0, b1))


def _w_cols(g, j):
    _, _, nh, rows, cols = g.shape
    ch, lj = j // nh, j % nh

    def spec(t0, t1):
        assert rows % t0 == 0 and cols % t1 == 0, (rows, cols, t0, t1)
        q = cols // t1
        return (None, None, None, t0, t1), lambda b0, b1: (ch, b1 // q, lj, b0, b1 % q)
    return _Op(g, spec)


def _w_rows(g, j):
    _, _, nh, rows, cols = g.shape
    ch, lj = j // nh, j % nh

    def spec(t0, t1):
        assert rows % t0 == 0 and cols % t1 == 0, (rows, cols, t0, t1)
        q = rows // t0
        return (None, None, None, t0, t1), lambda b0, b1: (ch, b0 // q, lj, b0 % q, b1)
    return _Op(g, spec)


def _dw_cols(rows, cols):
    rh = rows // 2

    def spec(t0, t1):
        assert rh % t0 == 0 and cols % t1 == 0, (rh, cols, t0, t1)
        qr, qc = rh // t0, cols // t1
        return (None, None, t0, t1), lambda b0, b1: (b0 // qr, b1 // qc, b0 % qr, b1 % qc)
    return (2, 4, rh, cols), spec


def _dw_rows(rows, cols):
    rh = rows // 2

    def spec(t0, t1):
        assert rh % t0 == 0 and cols % t1 == 0, (rh, cols, t0, t1)
        qr = rh // t0
        return (None, None, t0, t1), lambda b0, b1: ((b0 // qr) % 2, b0 // (2 * qr), b0 % qr, b1)
    return (2, 4, rh, cols), spec


def _mm(name, mode, a, b, dims, outs, tiles, epilogue=None, extras=()):
    m, n, k = dims
    tm, tn, tk = tiles
    assert m % tm == 0 and n % tn == 0 and k % tk == 0, (name, dims, tiles)
    nk = k // tk
    if mode == 'nn':
        a_t, a_ix, b_t, b_ix, ca, cb = (tm, tk), (lambda i, j, kk: (i, kk)), (tk, tn), (lambda i, j, kk: (kk, j)), 1, 0
    elif mode == 'nt':
        a_t, a_ix, b_t, b_ix, ca, cb = (tm, tk), (lambda i, j, kk: (i, kk)), (tn, tk), (lambda i, j, kk: (j, kk)), 1, 1
    else:
        a_t, a_ix, b_t, b_ix, ca, cb = (tk, tm), (lambda i, j, kk: (kk, i)), (tk, tn), (lambda i, j, kk: (kk, j)), 0, 0
    a_blk, a_fn = a.spec(*a_t)
    b_blk, b_fn = b.spec(*b_t)
    in_specs = [pl.BlockSpec(a_blk, lambda i, j, kk: a_fn(*a_ix(i, j, kk))),
                pl.BlockSpec(b_blk, lambda i, j, kk: b_fn(*b_ix(i, j, kk)))]
    operands = [a.arr, b.arr]
    for e in extras:
        e_blk, e_fn = e.spec(tm, tn)
        in_specs.append(pl.BlockSpec(e_blk, functools.partial(lambda i, j, kk, f: f(i, j), f=e_fn)))
        operands.append(e.arr)
    out_shapes, out_specs = [], []
    for shape, dtype, spec in outs:
        o_blk, o_fn = spec(tm, tn)
        out_shapes.append(jax.ShapeDtypeStruct(shape, dtype))
        out_specs.append(pl.BlockSpec(o_blk, functools.partial(lambda i, j, kk, f: f(i, j), f=o_fn)))
    n_ex, n_out = len(extras), len(outs)
    if epilogue is None:
        epilogue = lambda acc: (acc,)

    def body(a_ref, b_ref, *rest):
        ex_refs = rest[:n_ex]
        out_refs = rest[n_ex:n_ex + n_out]
        p = lax.dot_general(a_ref[...].astype(BF16), b_ref[...].astype(BF16),
                            (((ca,), (cb,)), ((), ())), preferred_element_type=F32)

        def finish(acc):
            res = epilogue(acc, *[r[...] for r in ex_refs])
            for o_ref, val in zip(out_refs, res):
                o_ref[...] = val.astype(o_ref.dtype)

        if nk == 1:
            finish(p)
        else:
            acc_ref = rest[n_ex + n_out]
            kk = pl.program_id(2)

            @pl.when(kk == 0)
            def _():
                acc_ref[...] = p

            @pl.when(kk > 0)
            def _():
                acc_ref[...] += p

            @pl.when(kk == nk - 1)
            def _():
                finish(acc_ref[...])

    res = _pcall(
        body, name=name, grid=(m // tm, n // tn, nk),
        out_shape=out_shapes, in_specs=in_specs, out_specs=out_specs,
        scratch_shapes=[pltpu.VMEM((tm, tn), F32)] if nk > 1 else [],
        compiler_params=_params('parallel', 'parallel', 'arbitrary'),
    )(*operands)
    return res


def _rowwise(name, fn, row_ins, vec_ins, outs, reds, tm):
    rows = row_ins[0].shape[0]
    assert rows % tm == 0
    n_in = len(row_ins) + len(vec_ins)
    n_out = len(outs)

    def body(*refs):
        vals = [r[...] for r in refs[:n_in]]
        res = fn(*vals)
        for o_ref, val in zip(refs[n_in:n_in + n_out], res[:n_out]):
            o_ref[...] = val.astype(o_ref.dtype)
        first = pl.program_id(0) == 0
        for r_ref, val in zip(refs[n_in + n_out:], res[n_out:]):
            @pl.when(first)
            def _(r_ref=r_ref, val=val):
                r_ref[...] = val

            @pl.when(jnp.logical_not(first))
            def _(r_ref=r_ref, val=val):
                r_ref[...] += val

    in_specs = [pl.BlockSpec((tm, a.shape[1]), lambda i: (i, 0)) for a in row_ins]
    in_specs += [pl.BlockSpec((1, v.shape[1]), lambda i: (0, 0)) for v in vec_ins]
    out_shapes = [jax.ShapeDtypeStruct((rows, w), dt) for w, dt in outs]
    out_shapes += [jax.ShapeDtypeStruct((1, w), F32) for w in reds]
    out_specs = [pl.BlockSpec((tm, w), lambda i: (i, 0)) for w, _ in outs]
    out_specs += [pl.BlockSpec((1, w), lambda i: (0, 0)) for w in reds]
    return _pcall(
        body, name=name, grid=(rows // tm,),
        out_shape=out_shapes, in_specs=in_specs, out_specs=out_specs,
        compiler_params=_params('arbitrary'),
    )(*row_ins, *vec_ins)


def _norm_fwd(h, g, name):
    def fn(hv, gv):
        rstd = lax.rsqrt(jnp.mean(hv * hv, axis=-1, keepdims=True) + EPS)
        return (hv * rstd * gv,)
    return _rowwise(name, fn, [h], [g], [(h.shape[1], BF16)], [], 256)[0]


def _norm_bwd(h, dhn, dres, g, name):
    def fn(hv, dv, rv, gv):
        rstd = lax.rsqrt(jnp.mean(hv * hv, axis=-1, keepdims=True) + EPS)
        xhat = hv * rstd
        dxhat = dv * gv
        dh = rv + rstd * (dxhat - xhat * jnp.mean(dxhat * xhat, axis=-1, keepdims=True))
        return dh, jnp.sum(dv * xhat, axis=0, keepdims=True)
    w = h.shape[1]
    return _rowwise(name, fn, [h, dhn, dres], [g], [(w, F32)], [w], 256)


def _loss_head(h, target, g, name):
    w = h.shape[1]

    def fn(hv, tv, gv):
        rstd = lax.rsqrt(jnp.mean(hv * hv, axis=-1, keepdims=True) + EPS)
        xhat = hv * rstd
        diff = xhat * gv - tv
        dy = diff * (1.0 / w)
        dxhat = dy * gv
        dh = rstd * (dxhat - xhat * jnp.mean(dxhat * xhat, axis=-1, keepdims=True))
        return (dh, jnp.sum(0.5 * dy * diff, axis=0, keepdims=True),
                jnp.sum(dy * xhat, axis=0, keepdims=True))
    return _rowwise(name, fn, [h, target], [g], [(w, F32)], [w, w], 256)


def _split3(x):
    hi = x.astype(BF16)
    r1 = x - hi.astype(F32)
    mid = r1.astype(BF16)
    lo = (r1 - mid.astype(F32)).astype(BF16)
    return hi, mid, lo


def _tri_dot(tri, x):
    hi, mid, lo = _split3(x)
    d = lambda p: jnp.dot(tri, p, preferred_element_type=F32)
    return d(hi) + d(mid) + d(lo)


def _log_sigmoid(x):
    return jnp.minimum(x, 0.0) - jnp.log(1.0 + jnp.exp(-jnp.abs(x)))


def _gla_dims(proj_w, kw, vw, dk, dv):
    assert kw % dk == 0 and (2 * kw) % dv == 0 and (2 * kw + vw) % dv == 0 and (2 * kw + 2 * vw) % LANES == 0
    return dict(q0=0, k0=kw // dk, v0=2 * kw // dv, r0=(2 * kw + vw) // dv, g0=(2 * kw + 2 * vw) // LANES)


def _gla_gates(gl, wgu, bias):
    pre = jnp.dot(gl.astype(BF16), wgu, preferred_element_type=F32) + bias
    la = _log_sigmoid(pre) * (1.0 / GLA_GATE_TEMP)
    r_i = lax.broadcasted_iota(jnp.int32, (CHUNK, CHUNK), 0)
    c_i = lax.broadcasted_iota(jnp.int32, (CHUNK, CHUNK), 1)
    cum = _tri_dot((c_i <= r_i).astype(BF16), la)
    total = cum[CHUNK - 1:CHUNK, :]
    return pre, cum, total


def _gla_scan_fwd(proj, wgu_pad, b_gate, o_norm, heads, kw, vw, tb, name):
    seq, pw = proj.shape
    dk, dv = kw // heads, vw // heads
    cb = tb // CHUNK
    nt = seq // tb
    o = _gla_dims(pw, kw, vw, dk, dv)
    scale = dk ** -0.5

    def body(q_ref, k_ref, v_ref, r_ref, gl_ref, wgu_ref, b_ref, on_ref, out_ref, st_ref, s_scr):
        @pl.when(pl.program_id(1) == 0)
        def _():
            s_scr[...] = jnp.zeros_like(s_scr)

        wgu = wgu_ref[...].astype(BF16)
        bias = b_ref[...]
        onorm = on_ref[...]
        st = s_scr[...]
        for ci in range(cb):
            rows = pl.ds(ci * CHUNK, CHUNK)
            _, cum, total = _gla_gates(gl_ref[rows, :], wgu, bias)
            kdec = k_ref[rows, :] * jnp.exp(total - cum)
            st = st * jnp.exp(total) + lax.dot_general(
                v_ref[rows, :].astype(BF16), kdec.astype(BF16), (((0,), (0,)), ((), ())),
                preferred_element_type=F32)
            st_ref[ci] = st
            qs = (q_ref[rows, :] * scale).astype(BF16)
            ov = lax.dot_general(qs, st.astype(BF16), (((1,), (1,)), ((), ())), preferred_element_type=F32)
            rstd = lax.rsqrt(jnp.mean(ov * ov, axis=-1, keepdims=True) + EPS)
            rv = r_ref[rows, :]
            out_ref[rows, :] = (ov * rstd * onorm * (rv * jax.nn.sigmoid(rv))).astype(out_ref.dtype)
        s_scr[...] = st

    in_specs = [
        pl.BlockSpec((tb, dk), lambda h, t: (t, o['q0'] + h)),
        pl.BlockSpec((tb, dk), lambda h, t: (t, o['k0'] + h)),
        pl.BlockSpec((tb, dv), lambda h, t: (t, o['v0'] + h)),
        pl.BlockSpec((tb, dv), lambda h, t: (t, o['r0'] + h)),
        pl.BlockSpec((tb, LANES), lambda h, t: (t, o['g0'])),
        pl.BlockSpec((LANES, dk), lambda h, t: (0, h)),
        pl.BlockSpec((1, dk), lambda h, t: (0, h)),
        pl.BlockSpec((1, dv), lambda h, t: (0, 0)),
    ]
    return _pcall(
        body, name=name, grid=(heads, nt),
        out_shape=[jax.ShapeDtypeStruct((seq, vw), BF16),
                   jax.ShapeDtypeStruct((heads, seq // CHUNK, dv, dk), F32)],
        in_specs=in_specs,
        out_specs=[pl.BlockSpec((tb, dv), lambda h, t: (t, h)),
                   pl.BlockSpec((None, cb, dv, dk), lambda h, t: (h, t, 0, 0))],
        scratch_shapes=[pltpu.VMEM((dv, dk), F32)],
        compiler_params=_params('parallel', 'arbitrary'),
    )(proj, proj, proj, proj, proj, wgu_pad, b_gate, o_norm)


def _gla_scan_bwd(proj, wgu_pad, b_gate, o_norm, states, dgated, heads, kw, vw, tb, name):
    seq, pw = proj.shape
    dk, dv = kw // heads, vw // heads
    cb = tb // CHUNK
    nt = seq // tb
    o = _gla_dims(pw, kw, vw, dk, dv)
    scale = dk ** -0.5

    def body(q_ref, k_ref, v_ref, r_ref, gl_ref, wgu_ref, b_ref, on_ref, st_ref, stp_ref, dg_ref,
             dq_ref, dk_ref, dv_ref, dr_ref, dpre_ref, db_ref, don_ref, ds_scr):
        hh = pl.program_id(0)
        t = pl.program_id(1)

        @pl.when(t == 0)
        def _():
            ds_scr[...] = jnp.zeros_like(ds_scr)
            db_ref[...] = jnp.zeros_like(db_ref)

        @pl.when(jnp.logical_and(hh == 0, t == 0))
        def _():
            don_ref[...] = jnp.zeros_like(don_ref)

        wgu = wgu_ref[...].astype(BF16)
        bias = b_ref[...]
        onorm = on_ref[...]
        has_prev = (t < nt - 1).astype(F32)
        r_i = lax.broadcasted_iota(jnp.int32, (CHUNK, CHUNK), 0)
        c_i = lax.broadcasted_iota(jnp.int32, (CHUNK, CHUNK), 1)
        strict = (c_i < r_i).astype(BF16)
        carry = ds_scr[...]
        db_acc = jnp.zeros((1, dk), F32)
        don_acc = jnp.zeros((1, dv), F32)
        for ci in reversed(range(cb)):
            rows = pl.ds(ci * CHUNK, CHUNK)
            pre, cum, total = _gla_gates(gl_ref[rows, :], wgu, bias)
            edec = jnp.exp(total - cum)
            decay = jnp.exp(total)
            kdec = k_ref[rows, :] * edec
            st = st_ref[ci]
            st_prev = st_ref[ci - 1] if ci > 0 else stp_ref[0] * has_prev
            stb = st.astype(BF16)
            qs = (q_ref[rows, :] * scale).astype(BF16)
            vb = v_ref[rows, :].astype(BF16)
            ov = lax.dot_general(qs, stb, (((1,), (1,)), ((), ())), preferred_element_type=F32)
            rstd = lax.rsqrt(jnp.mean(ov * ov, axis=-1, keepdims=True) + EPS)
            ohat = ov * rstd
            rv = r_ref[rows, :]
            sr = jax.nn.sigmoid(rv)
            dgv = dg_ref[rows, :]
            dy = dgv * (rv * sr)
            dr_ref[rows, :] = (dgv * (ohat * onorm) * (sr * (1.0 + rv * (1.0 - sr)))).astype(dr_ref.dtype)
            don_acc = don_acc + jnp.sum(dy * ohat, axis=0, keepdims=True)
            dohat = dy * onorm
            do = (rstd * (dohat - ohat * jnp.mean(dohat * ohat, axis=-1, keepdims=True))).astype(BF16)
            dq_ref[rows, :] = (jnp.dot(do, stb, preferred_element_type=F32) * scale).astype(dq_ref.dtype)
            dst = carry + lax.dot_general(do, qs, (((0,), (0,)), ((), ())), preferred_element_type=F32)
            dstb = dst.astype(BF16)
            dkdec = jnp.dot(vb, dstb, preferred_element_type=F32)
            dv_ref[rows, :] = lax.dot_general(kdec.astype(BF16), dstb, (((1,), (1,)), ((), ())),
                                              preferred_element_type=F32).astype(dv_ref.dtype)
            ddecay = jnp.sum(dst * st_prev, axis=0, keepdims=True)
            dk_ref[rows, :] = (dkdec * edec).astype(dk_ref.dtype)
            da = ddecay * decay + _tri_dot(strict, dkdec * kdec)
            dpre = da * (1.0 / GLA_GATE_TEMP) * (1.0 - jax.nn.sigmoid(pre))
            dpre_ref[rows, :] = dpre.astype(dpre_ref.dtype)
            db_acc = db_acc + jnp.sum(dpre, axis=0, keepdims=True)
            carry = dst * decay
        ds_scr[...] = carry
        db_ref[...] += db_acc
        don_ref[...] += don_acc

    rt = lambda t: nt - 1 - t
    in_specs = [
        pl.BlockSpec((tb, dk), lambda h, t: (rt(t), o['q0'] + h)),
        pl.BlockSpec((tb, dk), lambda h, t: (rt(t), o['k0'] + h)),
        pl.BlockSpec((tb, dv), lambda h, t: (rt(t), o['v0'] + h)),
        pl.BlockSpec((tb, dv), lambda h, t: (rt(t), o['r0'] + h)),
        pl.BlockSpec((tb, LANES), lambda h, t: (rt(t), o['g0'])),
        pl.BlockSpec((LANES, dk), lambda h, t: (0, h)),
        pl.BlockSpec((1, dk), lambda h, t: (0, h)),
        pl.BlockSpec((1, dv), lambda h, t: (0, 0)),
        pl.BlockSpec((None, cb, dv, dk), lambda h, t: (h, rt(t), 0, 0)),
        pl.BlockSpec((None, 1, dv, dk), lambda h, t: (h, jnp.maximum(rt(t) * cb - 1, 0), 0, 0)),
        pl.BlockSpec((tb, dv), lambda h, t: (rt(t), h)),
    ]
    out_shape = [jax.ShapeDtypeStruct((seq, kw), BF16), jax.ShapeDtypeStruct((seq, kw), BF16),
                 jax.ShapeDtypeStruct((seq, vw), BF16), jax.ShapeDtypeStruct((seq, vw), BF16),
                 jax.ShapeDtypeStruct((seq, kw), BF16),
                 jax.ShapeDtypeStruct((1, kw), F32), jax.ShapeDtypeStruct((1, dv), F32)]
    out_specs = [pl.BlockSpec((tb, dk), lambda h, t: (rt(t), h)),
                 pl.BlockSpec((tb, dk), lambda h, t: (rt(t), h)),
                 pl.BlockSpec((tb, dv), lambda h, t: (rt(t), h)),
                 pl.BlockSpec((tb, dv), lambda h, t: (rt(t), h)),
                 pl.BlockSpec((tb, dk), lambda h, t: (rt(t), h)),
                 pl.BlockSpec((1, dk), lambda h, t: (0, h)),
                 pl.BlockSpec((1, dv), lambda h, t: (0, 0))]
    return _pcall(
        body, name=name, grid=(heads, nt),
        out_shape=out_shape, in_specs=in_specs, out_specs=out_specs,
        scratch_shapes=[pltpu.VMEM((dv, dk), F32)],
        compiler_params=_params('arbitrary', 'arbitrary'),
    )(proj, proj, proj, proj, proj, wgu_pad, b_gate, o_norm, states, states, dgated)


def _cmul(ar, ai, br, bi):
    return ar * br - ai * bi, ar * bi + ai * br


def _gelu(y):
    c = math.sqrt(2.0 / math.pi)
    return 0.5 * y * (1.0 + jnp.tanh(c * (y + 0.044715 * y * y * y)))


def _gelu_grad(y):
    c = math.sqrt(2.0 / math.pi)
    th = jnp.tanh(c * (y + 0.044715 * y * y * y))
    return 0.5 * (1.0 + th) + 0.5 * y * (1.0 - th * th) * (c * (1.0 + 3.0 * 0.044715 * y * y))


def _power_pow2(ar, ai, n):
    assert n & (n - 1) == 0
    for _ in range(n.bit_length() - 1):
        ar, ai = _cmul(ar, ai, ar, ai)
    return ar, ai


def _s5_fwd(u, bre, bim, cre, cim, are, aim, dskip, name):
    seq, width = u.shape
    nb, ub, sb = bre.shape
    ls = seq // S5_SEGMENTS
    seg = S5_SEGMENTS

    def body(u_ref, bre_ref, bim_ref, cre_ref, cim_ref, are_ref, aim_ref, d_ref, y_ref, z_ref, xr_ref, xi_ref):
        uv = u_ref[...]
        ub16 = uv.astype(BF16)
        xr_ref[...] = jnp.dot(ub16, bre_ref[...].astype(BF16), preferred_element_type=F32)
        xi_ref[...] = jnp.dot(ub16, bim_ref[...].astype(BF16), preferred_element_type=F32)
        ar = jnp.broadcast_to(are_ref[...], (seg, sb))
        ai = jnp.broadcast_to(aim_ref[...], (seg, sb))

        def step(i, c):
            rows = pl.ds(pl.multiple_of(i * seg, seg), seg)
            pr, pi = _cmul(ar, ai, c[0], c[1])
            nr = pr + xr_ref[rows, :]
            ni = pi + xi_ref[rows, :]
            xr_ref[rows, :] = nr
            xi_ref[rows, :] = ni
            return nr, ni

        zero = jnp.zeros((seg, sb), F32)
        er, ei = lax.fori_loop(0, ls, step, (zero, zero), unroll=8)
        pr, pi = _power_pow2(ar, ai, ls)
        row = lax.broadcasted_iota(jnp.int32, (seg, sb), 0)
        sr, si = zero, zero
        for _ in range(seg - 1):
            tr, ti = _cmul(pr, pi, sr, si)
            sr = jnp.where(row == 0, 0.0, pltpu.roll(tr + er, 1, 0))
            si = jnp.where(row == 0, 0.0, pltpu.roll(ti + ei, 1, 0))

        def fix(i, c):
            rows = pl.ds(pl.multiple_of(i * seg, seg), seg)
            fr, fi = _cmul(c[0], c[1], sr, si)
            xr_ref[rows, :] += fr
            xi_ref[rows, :] += fi
            return _cmul(c[0], c[1], ar, ai)

        lax.fori_loop(0, ls, fix, (ar, ai), unroll=8)
        y = (jnp.dot(xr_ref[...].astype(BF16), cre_ref[...].astype(BF16), preferred_element_type=F32)
             - jnp.dot(xi_ref[...].astype(BF16), cim_ref[...].astype(BF16), preferred_element_type=F32)
             + d_ref[...] * uv)
        y_ref[...] = y
        z_ref[...] = _gelu(y).astype(z_ref.dtype)

    mat = lambda r, c: pl.BlockSpec((None, r, c), lambda b: (b, 0, 0))
    return _pcall(
        body, name=name, grid=(nb,),
        out_shape=[jax.ShapeDtypeStruct((seq, width), F32), jax.ShapeDtypeStruct((seq, width), BF16),
                   jax.ShapeDtypeStruct((seq, nb * sb), F32), jax.ShapeDtypeStruct((seq, nb * sb), F32)],
        in_specs=[pl.BlockSpec((seq, ub), lambda b: (0, b)), mat(ub, sb), mat(ub, sb), mat(sb, ub), mat(sb, ub),
                  mat(1, sb), mat(1, sb), pl.BlockSpec((1, ub), lambda b: (0, b))],
        out_specs=[pl.BlockSpec((seq, ub), lambda b: (0, b)), pl.BlockSpec((seq, ub), lambda b: (0, b)),
                   pl.BlockSpec((seq, sb), lambda b: (0, b)), pl.BlockSpec((seq, sb), lambda b: (0, b))],
        compiler_params=_params('parallel'),
    )(u, bre, bim, cre, cim, are, aim, dskip)


def _s5_bwd(dz, y, u, xr, xi, bre, bim, cre, cim, are, aim, dskip, name):
    seq, width = u.shape
    nb, ub, sb = bre.shape
    ls = seq // S5_SEGMENTS
    seg = S5_SEGMENTS

    def body(dz_ref, y_ref, u_ref, xr_ref, xi_ref, bre_ref, bim_ref, cre_ref, cim_ref, are_ref, aim_ref, d_ref,
             du_ref, dcr_ref, dci_ref, dbr_ref, dbi_ref, dar_ref, dai_ref, dd_ref, lr_ref, li_ref):
        uv = u_ref[...]
        dy = dz_ref[...] * _gelu_grad(y_ref[...])
        dd_ref[...] = jnp.sum(dy * uv, axis=0, keepdims=True)
        dyb = dy.astype(BF16)
        nt = (((1,), (1,)), ((), ()))
        tn = (((0,), (0,)), ((), ()))
        lr_ref[...] = lax.dot_general(dyb, cre_ref[...].astype(BF16), nt, preferred_element_type=F32)
        li_ref[...] = -lax.dot_general(dyb, cim_ref[...].astype(BF16), nt, preferred_element_type=F32)
        dcr_ref[...] = lax.dot_general(dyb, xr_ref[...].astype(BF16), tn, preferred_element_type=F32)
        dci_ref[...] = -lax.dot_general(dyb, xi_ref[...].astype(BF16), tn, preferred_element_type=F32)
        ar = jnp.broadcast_to(are_ref[...], (seg, sb))
        ai = jnp.broadcast_to(aim_ref[...], (seg, sb))
        nai = -ai

        def step(ii, c):
            rows = pl.ds(pl.multiple_of((ls - 1 - ii) * seg, seg), seg)
            pr, pi = _cmul(ar, nai, c[0], c[1])
            nr = pr + lr_ref[rows, :]
            ni = pi + li_ref[rows, :]
            lr_ref[rows, :] = nr
            li_ref[rows, :] = ni
            return nr, ni

        zero = jnp.zeros((seg, sb), F32)
        er, ei = lax.fori_loop(0, ls, step, (zero, zero), unroll=8)
        pr, pi = _power_pow2(ar, nai, ls)
        row = lax.broadcasted_iota(jnp.int32, (seg, sb), 0)
        rr, ri = zero, zero
        for _ in range(seg - 1):
            tr, ti = _cmul(pr, pi, rr, ri)
            rr = jnp.where(row == seg - 1, 0.0, pltpu.roll(tr + er, seg - 1, 0))
            ri = jnp.where(row == seg - 1, 0.0, pltpu.roll(ti + ei, seg - 1, 0))

        def corrected(rows, qr, qi):
            fr, fi = _cmul(qr, qi, rr, ri)
            nr = lr_ref[rows, :] + fr
            ni = li_ref[rows, :] + fi
            lr_ref[rows, :] = nr
            li_ref[rows, :] = ni
            return nr, ni

        def grad_a(nr, ni, xpr, xpi, accr, acci):
            return accr + nr * xpr + ni * xpi, acci + ni * xpr - nr * xpi

        def fix(ii, c):
            qr, qi, accr, acci = c
            i = ls - 1 - ii
            rows = pl.ds(pl.multiple_of(i * seg, seg), seg)
            prev = pl.ds(pl.multiple_of((i - 1) * seg, seg), seg)
            nr, ni = corrected(rows, qr, qi)
            accr, acci = grad_a(nr, ni, xr_ref[prev, :], xi_ref[prev, :], accr, acci)
            qr, qi = _cmul(qr, qi, ar, nai)
            return qr, qi, accr, acci

        qr, qi, accr, acci = lax.fori_loop(0, ls - 1, fix, (ar, nai, zero, zero), unroll=8)
        nr, ni = corrected(pl.ds(0, seg), qr, qi)
        last = pl.ds((ls - 1) * seg, seg)
        xpr = jnp.where(row == 0, 0.0, pltpu.roll(xr_ref[last, :], 1, 0))
        xpi = jnp.where(row == 0, 0.0, pltpu.roll(xi_ref[last, :], 1, 0))
        accr, acci = grad_a(nr, ni, xpr, xpi, accr, acci)
        dar_ref[...] = jnp.sum(accr, axis=0, keepdims=True)
        dai_ref[...] = jnp.sum(acci, axis=0, keepdims=True)
        lrb = lr_ref[...].astype(BF16)
        lib = li_ref[...].astype(BF16)
        ub16 = uv.astype(BF16)
        dbr_ref[...] = lax.dot_general(ub16, lrb, tn, preferred_element_type=F32)
        dbi_ref[...] = lax.dot_general(ub16, lib, tn, preferred_element_type=F32)
        du_ref[...] = (d_ref[...] * dy
                       + lax.dot_general(lrb, bre_ref[...].astype(BF16), nt, preferred_element_type=F32)
                       + lax.dot_general(lib, bim_ref[...].astype(BF16), nt, preferred_element_type=F32))

    mat = lambda r, c: pl.BlockSpec((None, r, c), lambda b: (b, 0, 0))
    col = lambda w: pl.BlockSpec((seq, w), lambda b: (0, b))
    return _pcall(
        body, name=name, grid=(nb,),
        out_shape=[jax.ShapeDtypeStruct((seq, width), F32)]
        + [jax.ShapeDtypeStruct((nb, ub, sb), F32)] * 4
        + [jax.ShapeDtypeStruct((nb, 1, sb), F32)] * 2
        + [jax.ShapeDtypeStruct((1, width), F32)],
        in_specs=[col(ub), col(ub), col(ub), col(sb), col(sb), mat(ub, sb), mat(ub, sb), mat(sb, ub), mat(sb, ub),
                  mat(1, sb), mat(1, sb), pl.BlockSpec((1, ub), lambda b: (0, b))],
        out_specs=[col(ub), mat(ub, sb), mat(ub, sb), mat(ub, sb), mat(ub, sb), mat(1, sb), mat(1, sb),
                   pl.BlockSpec((1, ub), lambda b: (0, b))],
        scratch_shapes=[pltpu.VMEM((seq, sb), F32), pltpu.VMEM((seq, sb), F32)],
        compiler_params=_params('parallel'),
    )(dz, y, u, xr, xi, bre, bim, cre, cim, are, aim, dskip)


def _s5_discretise(lam_re, lam_im, log_dt, b_re, b_im):
    lr = jnp.minimum(lam_re, S5_EIG_CLIP)
    li = lam_im
    dt = jnp.exp(log_dt)[:, None]
    mag = jnp.exp(lr * dt)
    ang = li * dt
    ab_re = mag * jnp.cos(ang)
    ab_im = mag * jnp.sin(ang)
    den = lr * lr + li * li
    nr = ab_re - 1.0
    f_re = (nr * lr + ab_im * li) / den
    f_im = (ab_im * lr - nr * li) / den
    bb_re = f_re[..., None] * b_re - f_im[..., None] * b_im
    bb_im = f_re[..., None] * b_im + f_im[..., None] * b_re
    return ab_re, ab_im, bb_re, bb_im


def _to_blocks(m):
    g, a, b = m.shape
    gb = S5_GROUPS_PER_BLOCK
    eye = jnp.eye(gb, dtype=m.dtype)
    return jnp.einsum('bgac,gh->bgahc', m.reshape(g // gb, gb, a, b), eye).reshape(g // gb, gb * a, gb * b)


def _from_blocks(m, a, b):
    nb = m.shape[0]
    gb = S5_GROUPS_PER_BLOCK
    eye = jnp.eye(gb, dtype=m.dtype)
    return jnp.einsum('bgahc,gh->bgac', m.reshape(nb, gb, a, gb, b), eye).reshape(nb * gb, a, b)


def _glu_fwd(o, h, name):
    half = o.shape[1] // 2

    def fn(ov, hv):
        return (hv + ov[:, :half] * jax.nn.sigmoid(ov[:, half:]),)
    return _rowwise(name, fn, [o, h], [], [(half, F32)], [], 256)[0]


def _glu_bwd(o, dout, name):
    half = o.shape[1] // 2

    def fn(ov, dv):
        val, gate = ov[:, :half], ov[:, half:]
        sg = jax.nn.sigmoid(gate)
        return (jnp.concatenate([dv * sg, dv * val * sg * (1.0 - sg)], axis=1),)
    return _rowwise(name, fn, [o, dout], [], [(2 * half, BF16)], [], 256)[0]


def _adam_math(w, g, m, v):
    m = ADAM_B1 * m + (1.0 - ADAM_B1) * g
    v = ADAM_B2 * v + (1.0 - ADAM_B2) * (g * g)
    m_hat = m / (1.0 - ADAM_B1 ** ADAM_STEP)
    v_hat = v / (1.0 - ADAM_B2 ** ADAM_STEP)
    delta = -ADAM_LR * (m_hat / (jnp.sqrt(v_hat) + ADAM_EPS) + ADAM_WD * w)
    return delta, m, v


def _adamw(w, m, v, grads, name):
    nl, rows, cols = w.shape
    tm = _tile(rows, max(8, (1 << 18) // cols // 8 * 8), unit=8)
    nbk = rows // tm

    def body(*refs):
        w_ref, m_ref, v_ref = refs[:3]
        g_refs = refs[3:3 + nl]
        go_ref, d_ref, mo_ref, vo_ref = refs[3 + nl:]
        layer = pl.program_id(0)
        g = g_refs[0][...]
        for l in range(1, nl):
            g = jnp.where(layer == l, g_refs[l][...], g)
        delta, mn, vn = _adam_math(w_ref[...], g, m_ref[...], v_ref[...])
        go_ref[...] = g
        d_ref[...] = delta
        mo_ref[...] = mn
        vo_ref[...] = vn

    stacked = pl.BlockSpec((None, tm, cols), lambda l, i: (l, i, 0))

    def g_spec(layer):
        return pl.BlockSpec((tm, cols), lambda l, i: (jnp.where(l == layer, i, jnp.where(l < layer, 0, nbk - 1)), 0))

    return _pcall(
        body, name=name, grid=(nl, nbk),
        out_shape=[jax.ShapeDtypeStruct(w.shape, F32)] * 4,
        in_specs=[stacked] * 3 + [g_spec(l) for l in range(nl)],
        out_specs=[stacked] * 4,
        compiler_params=_params('arbitrary', 'arbitrary'),
    )(w, m, v, *grads)


def _pack(arrs, rows_mult=8):
    flat = jnp.concatenate([a.reshape(-1) for a in arrs])
    total = flat.shape[0]
    rows = -(-total // LANES)
    rows = -(-rows // rows_mult) * rows_mult
    flat = jnp.pad(flat, (0, rows * LANES - total))
    return flat.reshape(rows, LANES)


def _unpack(packed, shapes):
    flat = packed.reshape(-1)
    out, off = [], 0
    for s in shapes:
        size = math.prod(s)
        out.append(flat[off:off + size].reshape(s))
        off += size
    return out


def _permute(a):
    seq, w = a.shape
    return a.reshape(S5_SEGMENTS, seq // S5_SEGMENTS, w).transpose(1, 0, 2).reshape(seq, w)


def _unpermute(a):
    seq, w = a.shape
    return a.reshape(seq // S5_SEGMENTS, S5_SEGMENTS, w).transpose(1, 0, 2).reshape(seq, w)


def kernel(x, gla_norm, gla_w_in, gla_w_gate_up, gla_b_gate, gla_o_norm, gla_w_out, s5_norm, s5_w_in, s5_lam_re, s5_lam_im, s5_log_dt, s5_b_re, s5_b_im, s5_c_re, s5_c_im, s5_d, s5_w_out, mlp_norm, mlp_w_up, mlp_w_down, final_norm, loss_target, m_gla_norm, m_gla_w_in, m_gla_w_gate_up, m_gla_b_gate, m_gla_o_norm, m_gla_w_out, m_s5_norm, m_s5_w_in, m_s5_lam_re, m_s5_lam_im, m_s5_log_dt, m_s5_b_re, m_s5_b_im, m_s5_c_re, m_s5_c_im, m_s5_d, m_s5_w_out, m_mlp_norm, m_mlp_w_up, m_mlp_w_down, m_final_norm, v_gla_norm, v_gla_w_in, v_gla_w_gate_up, v_gla_b_gate, v_gla_o_norm, v_gla_w_out, v_s5_norm, v_s5_w_in, v_s5_lam_re, v_s5_lam_im, v_s5_log_dt, v_s5_b_re, v_s5_b_im, v_s5_c_re, v_s5_c_im, v_s5_d, v_s5_w_out, v_mlp_norm, v_mlp_w_up, v_mlp_w_down, v_final_norm):
    weights = dict(gla_norm=gla_norm, gla_w_in=gla_w_in, gla_w_gate_up=gla_w_gate_up, gla_b_gate=gla_b_gate, gla_o_norm=gla_o_norm, gla_w_out=gla_w_out, s5_norm=s5_norm, s5_w_in=s5_w_in, s5_lam_re=s5_lam_re, s5_lam_im=s5_lam_im, s5_log_dt=s5_log_dt, s5_b_re=s5_b_re, s5_b_im=s5_b_im, s5_c_re=s5_c_re, s5_c_im=s5_c_im, s5_d=s5_d, s5_w_out=s5_w_out, mlp_norm=mlp_norm, mlp_w_up=mlp_w_up, mlp_w_down=mlp_w_down, final_norm=final_norm)
    mom1 = dict(gla_norm=m_gla_norm, gla_w_in=m_gla_w_in, gla_w_gate_up=m_gla_w_gate_up, gla_b_gate=m_gla_b_gate, gla_o_norm=m_gla_o_norm, gla_w_out=m_gla_w_out, s5_norm=m_s5_norm, s5_w_in=m_s5_w_in, s5_lam_re=m_s5_lam_re, s5_lam_im=m_s5_lam_im, s5_log_dt=m_s5_log_dt, s5_b_re=m_s5_b_re, s5_b_im=m_s5_b_im, s5_c_re=m_s5_c_re, s5_c_im=m_s5_c_im, s5_d=m_s5_d, s5_w_out=m_s5_w_out, mlp_norm=m_mlp_norm, mlp_w_up=m_mlp_w_up, mlp_w_down=m_mlp_w_down, final_norm=m_final_norm)
    mom2 = dict(gla_norm=v_gla_norm, gla_w_in=v_gla_w_in, gla_w_gate_up=v_gla_w_gate_up, gla_b_gate=v_gla_b_gate, gla_o_norm=v_gla_o_norm, gla_w_out=v_gla_w_out, s5_norm=v_s5_norm, s5_w_in=v_s5_w_in, s5_lam_re=v_s5_lam_re, s5_lam_im=v_s5_lam_im, s5_log_dt=v_s5_log_dt, s5_b_re=v_s5_b_re, s5_b_im=v_s5_b_im, s5_c_re=v_s5_c_re, s5_c_im=v_s5_c_im, s5_d=v_s5_d, s5_w_out=v_s5_w_out, mlp_norm=v_mlp_norm, mlp_w_up=v_mlp_w_up, mlp_w_down=v_mlp_w_down, final_norm=v_final_norm)
    names = list(weights)
    big = ['gla_w_in', 'gla_w_out', 's5_w_in', 's5_w_out', 'mlp_w_up', 'mlp_w_down']
    small = [n for n in names if n not in big]

    chip = 2 * lax.axis_index('x') + lax.axis_index('y')
    h0 = x[0]
    target = loss_target[0]
    seq, dm = h0.shape
    depth = mlp_norm.shape[0]
    n_gla = gla_norm.shape[0]
    n_s5 = s5_lam_re.shape[0]
    rank = gla_w_gate_up.shape[1]
    kw = gla_b_gate.shape[1]
    dv = gla_o_norm.shape[1]
    in_w = 4 * gla_w_in.shape[2]
    vw = (in_w - rank - 2 * kw) // 2
    heads = vw // dv
    dk = kw // heads
    pw = -(-in_w // LANES) * LANES
    s5w = s5_w_in.shape[2]
    n_grp, n_state, grp = s5_b_re.shape[1:]
    hid = 4 * mlp_w_up.shape[2]
    tb = min(seq, 8 * CHUNK)
    tm = _tile(seq, 1024)

    g_w_in = _gather_weight(gla_w_in, 'ag_gla_w_in')
    g_gla_out = _gather_weight(gla_w_out, 'ag_gla_w_out')
    g_s5_in = _gather_weight(s5_w_in, 'ag_s5_w_in')
    g_s5_out = _gather_weight(s5_w_out, 'ag_s5_w_out')
    g_up = _gather_weight(mlp_w_up, 'ag_mlp_w_up')
    g_down = _gather_weight(mlp_w_down, 'ag_mlp_w_down')
    sharded_small = [gla_w_gate_up, s5_norm, s5_d]
    gathered_small = _exchange(_pack(sharded_small), 'xy', 'bcast', 'ag_small')
    parts = [_unpack(gathered_small[k], [a.shape for a in sharded_small]) for k in range(4)]
    wgu_full = jnp.concatenate([p[0] for p in parts], axis=2)
    s5_norm_full = jnp.concatenate([p[1] for p in parts], axis=1)
    s5_d_full = jnp.concatenate([p[2] for p in parts], axis=1)

    def gla_w_in_padded(j):
        nh = g_w_in.shape[2]
        wj = g_w_in[j // nh, :, j % nh]
        wj = wj.transpose(1, 0, 2).reshape(dm, in_w)
        return jnp.pad(wj, ((0, 0), (0, pw - in_w)))

    grads = {n: [None] * weights[n].shape[0] for n in names if n != 'final_norm'}

    saved = []
    h = h0
    for i in range(depth):
        j = i // 2
        rec = {}
        if i % 2 == 0:
            rec['h_in'] = h
            hn = _norm_fwd(h, gla_norm[j:j + 1], 'gla_norm_fwd')
            w_in_pad = gla_w_in_padded(j)
            proj = _mm('gla_proj', 'nn', _plain(hn), _plain(w_in_pad), (seq, pw, dm),
                       [((seq, pw), F32, _plain_shape(None))], (tm, _tile(pw, 1024), dm))[0]
            wgu_pad = jnp.pad(wgu_full[j], ((0, LANES - rank), (0, 0)))
            gated, states = _gla_scan_fwd(proj, wgu_pad, gla_b_gate[j:j + 1], gla_o_norm[j:j + 1],
                                          heads, kw, vw, tb, 'gla_scan_fwd')
            h = _mm('gla_out', 'nn', _plain(gated), _w_rows(g_gla_out, j), (seq, dm, vw),
                    [((seq, dm), F32, _plain_shape(None))],
                    (tm, _tile(dm, 1024), _tile(g_gla_out.shape[3], 1024)),
                    epilogue=lambda acc, hv: (acc + hv,), extras=[_plain(h)])[0]
            rec.update(hn=hn, w_in_pad=w_in_pad, proj=proj, wgu_pad=wgu_pad, gated=gated, states=states)
        else:
            hp = _permute(h)
            rec['h_in'] = hp
            hn = _norm_fwd(hp, s5_norm_full[j:j + 1], 's5_norm_fwd')
            u = _mm('s5_in', 'nn', _plain(hn), _w_rows(g_s5_in, j), (seq, s5w, dm),
                    [((seq, s5w), F32, _plain_shape(None))],
                    (tm, _tile(s5w, 1024), _tile(g_s5_in.shape[3], 1024)))[0]
            disc, disc_vjp = jax.vjp(_s5_discretise, s5_lam_re[j], s5_lam_im[j], s5_log_dt[j], s5_b_re[j], s5_b_im[j])
            ab_re, ab_im, bb_re, bb_im = disc
            bre = _to_blocks(bb_re.transpose(0, 2, 1))
            bim = _to_blocks(bb_im.transpose(0, 2, 1))
            cre = _to_blocks(s5_c_re[j].transpose(0, 2, 1))
            cim = _to_blocks(s5_c_im[j].transpose(0, 2, 1))
            nb = n_grp // S5_GROUPS_PER_BLOCK
            are = ab_re.reshape(nb, 1, S5_GROUPS_PER_BLOCK * n_state)
            aim = ab_im.reshape(nb, 1, S5_GROUPS_PER_BLOCK * n_state)
            dskip = s5_d_full[j:j + 1]
            y, z, xr, xi = _s5_fwd(u, bre, bim, cre, cim, are, aim, dskip, 's5_scan_fwd')
            o = _mm('s5_out', 'nn', _plain(z), _w_cols(g_s5_out, j), (seq, 2 * dm, s5w),
                    [((seq, 2 * dm), F32, _plain_shape(None))],
                    (tm, _tile(g_s5_out.shape[4], 1024), _tile(s5w, 1024)))[0]
            h = _unpermute(_glu_fwd(o, hp, 's5_glu_fwd'))
            rec.update(hn=hn, u=u, y=y, z=z, xr=xr, xi=xi, o=o, mats=(bre, bim, cre, cim, are, aim, dskip),
                       disc_vjp=disc_vjp)
        rec['h_mid'] = h
        hn2 = _norm_fwd(h, mlp_norm[i:i + 1], 'mlp_norm_fwd')
        act, act2 = _mm('mlp_up', 'nn', _plain(hn2), _w_cols(g_up, i), (seq, hid, dm),
                        [((seq, hid), BF16, _plain_shape(None))] * 2,
                        (tm, _tile(g_up.shape[4], 1024), dm),
                        epilogue=lambda acc: (jnp.maximum(acc, 0.0), jnp.square(jnp.maximum(acc, 0.0))))
        h = _mm('mlp_down', 'nn', _plain(act2), _w_rows(g_down, i), (seq, dm, hid),
                [((seq, dm), F32, _plain_shape(None))],
                (tm, _tile(dm, 1024), _tile(g_down.shape[3], 2048)),
                epilogue=lambda acc, hv: (acc + hv,), extras=[_plain(h)])[0]
        rec.update(hn2=hn2, act=act, act2=act2)
        saved.append(rec)

    dh, loss_cols, d_final = _loss_head(h, target, final_norm.reshape(1, dm), 'loss_head')
    loss = lax.psum(jnp.sum(loss_cols), ('x', 'y', 'c'))
    grads['final_norm'] = [d_final.reshape(dm)]

    big_grads = {n: [None] * weights[n].shape[0] for n in big}
    for i in reversed(range(depth)):
        j = i // 2
        rec = saved[i]
        r_dn, c_dn = mlp_w_down.shape[1:]
        shape, spec = _dw_rows(r_dn, c_dn)
        dw = _mm('mlp_down_dw', 'tn', _plain(rec['act2']), _plain(dh), (hid, dm, seq),
                 [(shape, BF16, spec)], (_tile(r_dn // 2, 1024), _tile(c_dn, 1024), seq))[0]
        big_grads['mlp_w_down'][i] = _reduce_scatter(dw, 'rs_mlp_down')
        dpre = _mm('mlp_down_dx', 'nt', _plain(dh), _w_rows(g_down, i), (seq, hid, dm),
                   [((seq, hid), BF16, _plain_shape(None))],
                   (tm, _tile(g_down.shape[3], 1024), dm),
                   epilogue=lambda acc, av: (acc * (2.0 * av.astype(F32)),), extras=[_plain(rec['act'])])[0]
        r_up, c_up = mlp_w_up.shape[1:]
        shape, spec = _dw_cols(r_up, c_up)
        dw = _mm('mlp_up_dw', 'tn', _plain(rec['hn2']), _plain(dpre), (dm, hid, seq),
                 [(shape, BF16, spec)], (_tile(r_up // 2, 1024), _tile(c_up, 1024), seq))[0]
        big_grads['mlp_w_up'][i] = _reduce_scatter(dw, 'rs_mlp_up')
        dhn = _mm('mlp_up_dx', 'nt', _plain(dpre), _w_cols(g_up, i), (seq, dm, hid),
                  [((seq, dm), F32, _plain_shape(None))],
                  (tm, _tile(dm, 1024), _tile(g_up.shape[4], 2048)))[0]
        dh, dg = _norm_bwd(rec['h_mid'], dhn, dh, mlp_norm[i:i + 1], 'mlp_norm_bwd')
        grads['mlp_norm'][i] = dg[0]

        if i % 2 == 0:
            r_o, c_o = gla_w_out.shape[1:]
            shape, spec = _dw_rows(r_o, c_o)
            dw = _mm('gla_out_dw', 'tn', _plain(rec['gated']), _plain(dh), (vw, dm, seq),
                     [(shape, BF16, spec)], (_tile(r_o // 2, 1024), _tile(c_o, 1024), seq))[0]
            big_grads['gla_w_out'][j] = _reduce_scatter(dw, 'rs_gla_out')
            dgated = _mm('gla_out_dx', 'nt', _plain(dh), _w_rows(g_gla_out, j), (seq, vw, dm),
                         [((seq, vw), F32, _plain_shape(None))],
                         (tm, _tile(g_gla_out.shape[3], 1024), dm))[0]
            dq, dkk, dvv, dr, dpre_g, db, don = _gla_scan_bwd(
                rec['proj'], rec['wgu_pad'], gla_b_gate[j:j + 1], gla_o_norm[j:j + 1], rec['states'], dgated,
                heads, kw, vw, tb, 'gla_scan_bwd')
            grads['gla_b_gate'][j] = db[0]
            grads['gla_o_norm'][j] = don[0]
            dgl = _mm('gla_gate_dx', 'nt', _plain(dpre_g), _plain(rec['wgu_pad']), (seq, LANES, kw),
                      [((seq, LANES), BF16, _plain_shape(None))], (tm, LANES, kw))[0]
            g_low = rec['proj'][:, pw - LANES:]
            dwgu = _mm('gla_gate_dw', 'tn', _plain(g_low), _plain(dpre_g), (LANES, kw, seq),
                       [((LANES, kw), F32, _plain_shape(None))], (LANES, kw, seq))[0]
            grads['gla_w_gate_up'][j] = dwgu[:rank]
            dproj = jnp.concatenate([dq, dkk, dvv, dr, dgl], axis=1)
            dw_pad = _mm('gla_proj_dw', 'tn', _plain(rec['hn']), _plain(dproj), (dm, pw, seq),
                         [((dm, pw), BF16, _plain_shape(None))], (_tile(dm, 1024), _tile(pw, 1024), seq))[0]
            shard_w = in_w // 4
            dw = dw_pad[:, :in_w].reshape(2, dm // 2, 4, shard_w).transpose(0, 2, 1, 3)
            big_grads['gla_w_in'][j] = _reduce_scatter(dw, 'rs_gla_in')
            dhn = _mm('gla_proj_dx', 'nt', _plain(dproj), _plain(rec['w_in_pad']), (seq, dm, pw),
                      [((seq, dm), F32, _plain_shape(None))], (tm, _tile(dm, 1024), _tile(pw, 1024)))[0]
            dh, dg = _norm_bwd(rec['h_in'], dhn, dh, gla_norm[j:j + 1], 'gla_norm_bwd')
            grads['gla_norm'][j] = dg[0]
        else:
            dhp = _permute(dh)
            do = _glu_bwd(rec['o'], dhp, 's5_glu_bwd')
            r_o, c_o = s5_w_out.shape[1:]
            shape, spec = _dw_cols(r_o, c_o)
            dw = _mm('s5_out_dw', 'tn', _plain(rec['z']), _plain(do), (s5w, 2 * dm, seq),
                     [(shape, BF16, spec)], (_tile(r_o // 2, 1024), _tile(c_o, 1024), seq))[0]
            big_grads['s5_w_out'][j] = _reduce_scatter(dw, 'rs_s5_out')
            dz = _mm('s5_out_dx', 'nt', _plain(do), _w_cols(g_s5_out, j), (seq, s5w, 2 * dm),
                     [((seq, s5w), F32, _plain_shape(None))],
                     (tm, _tile(s5w, 1024), _tile(g_s5_out.shape[4], 1024)))[0]
            bre, bim, cre, cim, are, aim, dskip = rec['mats']
            du, dcr, dci, dbr, dbi, dar, dai, dd = _s5_bwd(dz, rec['y'], rec['u'], rec['xr'], rec['xi'],
                                                           bre, bim, cre, cim, are, aim, dskip, 's5_scan_bwd')
            grads['s5_c_re'][j] = _from_blocks(dcr, grp, n_state)
            grads['s5_c_im'][j] = _from_blocks(dci, grp, n_state)
            dbb_re = _from_blocks(dbr, grp, n_state).transpose(0, 2, 1)
            dbb_im = _from_blocks(dbi, grp, n_state).transpose(0, 2, 1)
            d_lr, d_li, d_dt, d_bre, d_bim = rec['disc_vjp'](
                (dar.reshape(n_grp, n_state), dai.reshape(n_grp, n_state), dbb_re, dbb_im))
            grads['s5_lam_re'][j] = d_lr
            grads['s5_lam_im'][j] = d_li
            grads['s5_log_dt'][j] = d_dt
            grads['s5_b_re'][j] = d_bre
            grads['s5_b_im'][j] = d_bim
            grads['s5_d'][j] = dd[0]
            r_i, c_i = s5_w_in.shape[1:]
            shape, spec = _dw_rows(r_i, c_i)
            dw = _mm('s5_in_dw', 'tn', _plain(rec['hn']), _plain(du), (dm, s5w, seq),
                     [(shape, BF16, spec)], (_tile(r_i // 2, 1024), _tile(c_i, 1024), seq))[0]
            big_grads['s5_w_in'][j] = _reduce_scatter(dw, 'rs_s5_in')
            dhn = _mm('s5_in_dx', 'nt', _plain(du), _w_rows(g_s5_in, j), (seq, dm, s5w),
                      [((seq, dm), F32, _plain_shape(None))],
                      (tm, _tile(g_s5_in.shape[3], 1024), _tile(s5w, 1024)))[0]
            dhp, dg = _norm_bwd(rec['h_in'], dhn, dhp, s5_norm_full[j:j + 1], 's5_norm_bwd')
            dh = _unpermute(dhp)
            grads['s5_norm'][j] = dg[0]
    grad_x = dh[None]

    local_small = [jnp.stack(grads[n]) if n != 'final_norm' else grads[n][0] for n in small]
    full_shapes = [a.shape for a in local_small]
    gathered = _exchange(_exchange(_pack(local_small), 'xy', 'bcast', 'ar_small_xy'), 'c', 'bcast', 'ar_small_c')
    rows = gathered.shape[2]
    summed = _sum_slots(gathered.reshape(8, rows, LANES), F32, 'ar_small_sum')
    small_full = dict(zip(small, _unpack(summed, full_shapes)))
    small_grad = {}
    for n in small:
        g = small_full[n]
        if g.shape != weights[n].shape:
            ax = [a for a in range(g.ndim) if g.shape[a] != weights[n].shape[a]][0]
            g = lax.dynamic_slice_in_dim(g, chip * weights[n].shape[ax], weights[n].shape[ax], axis=ax)
        small_grad[n] = g

    out_g, out_d, out_m, out_v = {}, {}, {}, {}
    for n in big:
        out_g[n], out_d[n], out_m[n], out_v[n] = _adamw(weights[n], mom1[n], mom2[n], big_grads[n], 'adamw_' + n)
    shapes = [weights[n].shape for n in small]
    pw_, pm_, pv_, pg_ = (_pack([d[n] for n in small]) for d in (weights, mom1, mom2, small_grad))
    _, sd, sm, sv = _adamw(pw_[None], pm_[None], pv_[None], [pg_], 'adamw_small')
    for n, d_, m_, v_ in zip(small, _unpack(sd[0], shapes), _unpack(sm[0], shapes), _unpack(sv[0], shapes)):
        out_g[n], out_d[n], out_m[n], out_v[n] = small_grad[n], d_, m_, v_

    return (loss, grad_x, *[out_g[n] for n in names], *[out_d[n] for n in names],
            *[out_m[n] for n in names], *[out_v[n] for n in names])
```

```python
import functools
import math

import jax
import jax.numpy as jnp
from jax import lax
from jax.experimental import pallas as pl
from jax.experimental.pallas import tpu as pltpu

F32 = jnp.float32
BF16 = jnp.bfloat16

EPS = 1e-6
CHUNK = 64
GLA_GATE_TEMP = 16.0
S5_EIG_CLIP = -1e-4
S5_SEGMENTS = 8
S5_GROUPS_PER_BLOCK = 8
LANES = 128
ADAM_LR = 0.001
ADAM_B1 = 0.9
ADAM_B2 = 0.999
ADAM_EPS = 1e-08
ADAM_WD = 0.01
ADAM_STEP = 10
VMEM_LIMIT_BYTES = 56 * 1024 * 1024
D2D_STREAMS = 16
ICI_STREAMS = 1

MESH = pl.DeviceIdType.MESH
ANY = pl.BlockSpec(memory_space=pl.ANY)


def _pcall(body, **kw):
    return pl.pallas_call(body, **kw)


def _params(*sem):
    return pltpu.CompilerParams(dimension_semantics=sem, vmem_limit_bytes=VMEM_LIMIT_BYTES)


def _tile(dim, target, unit=LANES):
    if dim <= target:
        return dim
    best = None
    for t in range(unit, target + 1, unit):
        if dim % t == 0:
            best = t
    assert best is not None, (dim, target)
    return best


def _exchange(x, group, mode, name):
    n = 2 if group == 'c' else 4
    if mode == 'bcast':
        blk = x.shape
    else:
        blk = x.shape[1:]
    if mode == 'a2a':
        assert x.shape[0] == n
    flips = [(0, 0, 1)] if group == 'c' else [(1, 0, 0), (0, 1, 0), (1, 1, 0)]
    ax, pieces = _split_axis(blk, x.dtype, D2D_STREAMS if group == 'c' else ICI_STREAMS)
    step = blk[ax] // pieces

    def piece(ref, p):
        if pieces == 1:
            return ref
        return ref.at[(slice(None),) * ax + (pl.ds(p * step, step),)]

    def body(x_ref, y_ref, send_sems, recv_sems, local_sem):
        ix, iy, ic = lax.axis_index('x'), lax.axis_index('y'), lax.axis_index('c')

        def slot(px, py, pc):
            return pc if group == 'c' else 2 * px + py

        def src(px, py, pc):
            if mode == 'a2a':
                return x_ref.at[slot(px, py, pc)]
            if mode == 'bcast_c':
                return x_ref.at[ic]
            return x_ref

        me = (ix, iy, ic)
        local = pltpu.make_async_copy(src(*me), y_ref.at[slot(*me)], local_sem)
        local.start()
        peers = []
        for fx, fy, fc in flips:
            peers.append((1 - ix if fx else ix, 1 - iy if fy else iy, 1 - ic if fc else ic))
        sends = []
        for p in range(pieces):
            for k, peer in enumerate(peers):
                cp = pltpu.make_async_remote_copy(
                    src_ref=piece(src(*peer), p), dst_ref=piece(y_ref.at[slot(*me)], p),
                    send_sem=send_sems.at[k * pieces + p], recv_sem=recv_sems.at[k * pieces + p],
                    device_id=peer, device_id_type=MESH)
                cp.start()
                sends.append(cp)
        for p in range(pieces):
            for k, peer in enumerate(peers):
                pltpu.make_async_remote_copy(
                    src_ref=piece(src(*peer), p), dst_ref=piece(y_ref.at[slot(*peer)], p),
                    send_sem=send_sems.at[k * pieces + p], recv_sem=recv_sems.at[k * pieces + p],
                    device_id=peer, device_id_type=MESH).wait_recv()
        for cp in sends:
            cp.wait_send()
        local.wait()

    return _pcall(
        body, name=name,
        out_shape=jax.ShapeDtypeStruct((n,) + tuple(blk), x.dtype),
        in_specs=[ANY], out_specs=ANY,
        scratch_shapes=[pltpu.SemaphoreType.DMA((len(flips) * pieces,)),
                        pltpu.SemaphoreType.DMA((len(flips) * pieces,)),
                        pltpu.SemaphoreType.DMA(())],
    )(x)


def _split_axis(blk, dtype, streams):
    sublanes = 8 * 4 // jnp.dtype(dtype).itemsize
    for pieces in [s for s in (16, 8, 4, 2) if s <= streams]:
        for ax in range(len(blk) - 1):
            unit = sublanes if ax == len(blk) - 2 else 1
            if blk[ax] % (pieces * unit) == 0:
                return ax, pieces
    return 0, 1


def _sum_slots(y, out_dtype, name):
    n, rows, cols = y.shape
    tm = _tile(rows, max(8, (1 << 20) // max(cols, 1) // 8 * 8), unit=8)

    def body(y_ref, o_ref):
        acc = y_ref[0].astype(F32)
        for k in range(1, n):
            acc = acc + y_ref[k].astype(F32)
        o_ref[...] = acc.astype(o_ref.dtype)

    return _pcall(
        body, name=name, grid=(rows // tm,),
        out_shape=jax.ShapeDtypeStruct((rows, cols), out_dtype),
        in_specs=[pl.BlockSpec((n, tm, cols), lambda i: (0, i, 0))],
        out_specs=pl.BlockSpec((tm, cols), lambda i: (i, 0)),
        compiler_params=_params('parallel'),
    )(y)


def _gather_weight(w, name):
    nl, rows, cols = w.shape
    nh = nl // 2
    wb = w.astype(BF16).reshape(2, nh, rows, cols)
    y1 = _exchange(wb, 'xy', 'bcast_c', name + '_xy')
    return _exchange(y1, 'c', 'bcast', name + '_c')


def _reduce_scatter(dw, name):
    _, _, rh, cols = dw.shape
    ya = _exchange(dw, 'c', 'a2a', name + '_pair')
    pre = _sum_slots(ya.reshape(2, 4 * rh, cols), BF16, name + '_pairsum').reshape(4, rh, cols)
    yb = _exchange(pre, 'xy', 'a2a', name + '_chips')
    fin = _sum_slots(yb, F32, name + '_chipsum')
    yc = _exchange(fin, 'c', 'bcast', name + '_back')
    return yc.reshape(2 * rh, cols)


class _Op:
    def __init__(self, arr, spec):
        self.arr = arr
        self.spec = spec


def _plain(arr):
    return _Op(arr, lambda t0, t1: ((t0, t1), lambda b0, b1: (b0, b1)))


def _plain_shape(shape):
    return lambda t0, t1: ((t0, t1), lambda b0, b1: (b0, b1))


def _w_cols(g, j):
    _, _, nh, rows, cols = g.shape
    ch, lj = j // nh, j % nh

    def spec(t0, t1):
        assert rows % t0 == 0 and cols % t1 == 0, (rows, cols, t0, t1)
        q = cols // t1
        return (None, None, None, t0, t1), lambda b0, b1: (ch, b1 // q, lj, b0, b1 % q)
    return _Op(g, spec)


def _w_rows(g, j):
    _, _, nh, rows, cols = g.shape
    ch, lj = j // nh, j % nh

    def spec(t0, t1):
        assert rows % t0 == 0 and cols % t1 == 0, (rows, cols, t0, t1)
        q = rows // t0
        return (None, None, None, t0, t1), lambda b0, b1: (ch, b0 // q, lj, b0 % q, b1)
    return _Op(g, spec)


def _dw_cols(rows, cols):
    rh = rows // 2

    def spec(t0, t1):
        assert rh % t0 == 0 and cols % t1 == 0, (rh, cols, t0, t1)
        qr, qc = rh // t0, cols // t1
        return (None, None, t0, t1), lambda b0, b1: (b0 // qr, b1 // qc, b0 % qr, b1 % qc)
    return (2, 4, rh, cols), spec


def _dw_rows(rows, cols):
    rh = rows // 2

    def spec(t0, t1):
        assert rh % t0 == 0 and cols % t1 == 0, (rh, cols, t0, t1)
        qr = rh // t0
        return (None, None, t0, t1), lambda b0, b1: ((b0 // qr) % 2, b0 // (2 * qr), b0 % qr, b1)
    return (2, 4, rh, cols), spec


def _mm(name, mode, a, b, dims, outs, tiles, epilogue=None, extras=()):
    m, n, k = dims
    tm, tn, tk = tiles
    assert m % tm == 0 and n % tn == 0 and k % tk == 0, (name, dims, tiles)
    nk = k // tk
    if mode == 'nn':
        a_t, a_ix, b_t, b_ix, ca, cb = (tm, tk), (lambda i, j, kk: (i, kk)), (tk, tn), (lambda i, j, kk: (kk, j)), 1, 0
    elif mode == 'nt':
        a_t, a_ix, b_t, b_ix, ca, cb = (tm, tk), (lambda i, j, kk: (i, kk)), (tn, tk), (lambda i, j, kk: (j, kk)), 1, 1
    else:
        a_t, a_ix, b_t, b_ix, ca, cb = (tk, tm), (lambda i, j, kk: (kk, i)), (tk, tn), (lambda i, j, kk: (kk, j)), 0, 0
    a_blk, a_fn = a.spec(*a_t)
    b_blk, b_fn = b.spec(*b_t)
    in_specs = [pl.BlockSpec(a_blk, lambda i, j, kk: a_fn(*a_ix(i, j, kk))),
                pl.BlockSpec(b_blk, lambda i, j, kk: b_fn(*b_ix(i, j, kk)))]
    operands = [a.arr, b.arr]
    for e in extras:
        e_blk, e_fn = e.spec(tm, tn)
        in_specs.append(pl.BlockSpec(e_blk, functools.partial(lambda i, j, kk, f: f(i, j), f=e_fn)))
        operands.append(e.arr)
    out_shapes, out_specs = [], []
    for shape, dtype, spec in outs:
        o_blk, o_fn = spec(tm, tn)
        out_shapes.append(jax.ShapeDtypeStruct(shape, dtype))
        out_specs.append(pl.BlockSpec(o_blk, functools.partial(lambda i, j, kk, f: f(i, j), f=o_fn)))
    n_ex, n_out = len(extras), len(outs)
    if epilogue is None:
        epilogue = lambda acc: (acc,)

    def body(a_ref, b_ref, *rest):
        ex_refs = rest[:n_ex]
        out_refs = rest[n_ex:n_ex + n_out]
        p = lax.dot_general(a_ref[...].astype(BF16), b_ref[...].astype(BF16),
                            (((ca,), (cb,)), ((), ())), preferred_element_type=F32)

        def finish(acc):
            res = epilogue(acc, *[r[...] for r in ex_refs])
            for o_ref, val in zip(out_refs, res):
                o_ref[...] = val.astype(o_ref.dtype)

        if nk == 1:
            finish(p)
        else:
            acc_ref = rest[n_ex + n_out]
            kk = pl.program_id(2)

            @pl.when(kk == 0)
            def _():
                acc_ref[...] = p

            @pl.when(kk > 0)
            def _():
                acc_ref[...] += p

            @pl.when(kk == nk - 1)
            def _():
                finish(acc_ref[...])

    res = _pcall(
        body, name=name, grid=(m // tm, n // tn, nk),
        out_shape=out_shapes, in_specs=in_specs, out_specs=out_specs,
        scratch_shapes=[pltpu.VMEM((tm, tn), F32)] if nk > 1 else [],
        compiler_params=_params('parallel', 'parallel', 'arbitrary'),
    )(*operands)
    return res


def _rowwise(name, fn, row_ins, vec_ins, outs, reds, tm):
    rows = row_ins[0].shape[0]
    assert rows % tm == 0
    n_in = len(row_ins) + len(vec_ins)
    n_out = len(outs)

    def body(*refs):
        vals = [r[...] for r in refs[:n_in]]
        res = fn(*vals)
        for o_ref, val in zip(refs[n_in:n_in + n_out], res[:n_out]):
            o_ref[...] = val.astype(o_ref.dtype)
        first = pl.program_id(0) == 0
        for r_ref, val in zip(refs[n_in + n_out:], res[n_out:]):
            @pl.when(first)
            def _(r_ref=r_ref, val=val):
                r_ref[...] = val

            @pl.when(jnp.logical_not(first))
            def _(r_ref=r_ref, val=val):
                r_ref[...] += val

    in_specs = [pl.BlockSpec((tm, a.shape[1]), lambda i: (i, 0)) for a in row_ins]
    in_specs += [pl.BlockSpec((1, v.shape[1]), lambda i: (0, 0)) for v in vec_ins]
    out_shapes = [jax.ShapeDtypeStruct((rows, w), dt) for w, dt in outs]
    out_shapes += [jax.ShapeDtypeStruct((1, w), F32) for w in reds]
    out_specs = [pl.BlockSpec((tm, w), lambda i: (i, 0)) for w, _ in outs]
    out_specs += [pl.BlockSpec((1, w), lambda i: (0, 0)) for w in reds]
    return _pcall(
        body, name=name, grid=(rows // tm,),
        out_shape=out_shapes, in_specs=in_specs, out_specs=out_specs,
        compiler_params=_params('arbitrary'),
    )(*row_ins, *vec_ins)


def _norm_fwd(h, g, name):
    def fn(hv, gv):
        rstd = lax.rsqrt(jnp.mean(hv * hv, axis=-1, keepdims=True) + EPS)
        return (hv * rstd * gv,)
    return _rowwise(name, fn, [h], [g], [(h.shape[1], BF16)], [], 256)[0]


def _norm_bwd(h, dhn, dres, g, name):
    def fn(hv, dv, rv, gv):
        rstd = lax.rsqrt(jnp.mean(hv * hv, axis=-1, keepdims=True) + EPS)
        xhat = hv * rstd
        dxhat = dv * gv
        dh = rv + rstd * (dxhat - xhat * jnp.mean(dxhat * xhat, axis=-1, keepdims=True))
        return dh, jnp.sum(dv * xhat, axis=0, keepdims=True)
    w = h.shape[1]
    return _rowwise(name, fn, [h, dhn, dres], [g], [(w, F32)], [w], 256)


def _loss_head(h, target, g, name):
    w = h.shape[1]

    def fn(hv, tv, gv):
        rstd = lax.rsqrt(jnp.mean(hv * hv, axis=-1, keepdims=True) + EPS)
        xhat = hv * rstd
        diff = xhat * gv - tv
        dy = diff * (1.0 / w)
        dxhat = dy * gv
        dh = rstd * (dxhat - xhat * jnp.mean(dxhat * xhat, axis=-1, keepdims=True))
        return (dh, jnp.sum(0.5 * dy * diff, axis=0, keepdims=True),
                jnp.sum(dy * xhat, axis=0, keepdims=True))
    return _rowwise(name, fn, [h, target], [g], [(w, F32)], [w, w], 256)


def _split3(x):
    hi = x.astype(BF16)
    r1 = x - hi.astype(F32)
    mid = r1.astype(BF16)
    lo = (r1 - mid.astype(F32)).astype(BF16)
    return hi, mid, lo


def _tri_dot(tri, x):
    hi, mid, lo = _split3(x)
    d = lambda p: jnp.dot(tri, p, preferred_element_type=F32)
    return d(hi) + d(mid) + d(lo)


def _log_sigmoid(x):
    return jnp.minimum(x, 0.0) - jnp.log(1.0 + jnp.exp(-jnp.abs(x)))


def _gla_dims(proj_w, kw, vw, dk, dv):
    assert kw % dk == 0 and (2 * kw) % dv == 0 and (2 * kw + vw) % dv == 0 and (2 * kw + 2 * vw) % LANES == 0
    return dict(q0=0, k0=kw // dk, v0=2 * kw // dv, r0=(2 * kw + vw) // dv, g0=(2 * kw + 2 * vw) // LANES)


def _gla_gates(gl, wgu, bias):
    pre = jnp.dot(gl.astype(BF16), wgu, preferred_element_type=F32) + bias
    la = _log_sigmoid(pre) * (1.0 / GLA_GATE_TEMP)
    r_i = lax.broadcasted_iota(jnp.int32, (CHUNK, CHUNK), 0)
    c_i = lax.broadcasted_iota(jnp.int32, (CHUNK, CHUNK), 1)
    cum = _tri_dot((c_i <= r_i).astype(BF16), la)
    total = cum[CHUNK - 1:CHUNK, :]
    return pre, cum, total


def _gla_scan_fwd(proj, wgu_pad, b_gate, o_norm, heads, kw, vw, tb, name):
    seq, pw = proj.shape
    dk, dv = kw // heads, vw // heads
    cb = tb // CHUNK
    nt = seq // tb
    o = _gla_dims(pw, kw, vw, dk, dv)
    scale = dk ** -0.5

    def body(q_ref, k_ref, v_ref, r_ref, gl_ref, wgu_ref, b_ref, on_ref, out_ref, st_ref, s_scr):
        @pl.when(pl.program_id(1) == 0)
        def _():
            s_scr[...] = jnp.zeros_like(s_scr)

        wgu = wgu_ref[...].astype(BF16)
        bias = b_ref[...]
        onorm = on_ref[...]
        st = s_scr[...]
        for ci in range(cb):
            rows = pl.ds(ci * CHUNK, CHUNK)
            _, cum, total = _gla_gates(gl_ref[rows, :], wgu, bias)
            kdec = k_ref[rows, :] * jnp.exp(total - cum)
            st = st * jnp.exp(total) + lax.dot_general(
                v_ref[rows, :].astype(BF16), kdec.astype(BF16), (((0,), (0,)), ((), ())),
                preferred_element_type=F32)
            st_ref[ci] = st
            qs = (q_ref[rows, :] * scale).astype(BF16)
            ov = lax.dot_general(qs, st.astype(BF16), (((1,), (1,)), ((), ())), preferred_element_type=F32)
            rstd = lax.rsqrt(jnp.mean(ov * ov, axis=-1, keepdims=True) + EPS)
            rv = r_ref[rows, :]
            out_ref[rows, :] = (ov * rstd * onorm * (rv * jax.nn.sigmoid(rv))).astype(out_ref.dtype)
        s_scr[...] = st

    in_specs = [
        pl.BlockSpec((tb, dk), lambda h, t: (t, o['q0'] + h)),
        pl.BlockSpec((tb, dk), lambda h, t: (t, o['k0'] + h)),
        pl.BlockSpec((tb, dv), lambda h, t: (t, o['v0'] + h)),
        pl.BlockSpec((tb, dv), lambda h, t: (t, o['r0'] + h)),
        pl.BlockSpec((tb, LANES), lambda h, t: (t, o['g0'])),
        pl.BlockSpec((LANES, dk), lambda h, t: (0, h)),
        pl.BlockSpec((1, dk), lambda h, t: (0, h)),
        pl.BlockSpec((1, dv), lambda h, t: (0, 0)),
    ]
    return _pcall(
        body, name=name, grid=(heads, nt),
        out_shape=[jax.ShapeDtypeStruct((seq, vw), BF16),
                   jax.ShapeDtypeStruct((heads, seq // CHUNK, dv, dk), F32)],
        in_specs=in_specs,
        out_specs=[pl.BlockSpec((tb, dv), lambda h, t: (t, h)),
                   pl.BlockSpec((None, cb, dv, dk), lambda h, t: (h, t, 0, 0))],
        scratch_shapes=[pltpu.VMEM((dv, dk), F32)],
        compiler_params=_params('parallel', 'arbitrary'),
    )(proj, proj, proj, proj, proj, wgu_pad, b_gate, o_norm)


def _gla_scan_bwd(proj, wgu_pad, b_gate, o_norm, states, dgated, heads, kw, vw, tb, name):
    seq, pw = proj.shape
    dk, dv = kw // heads, vw // heads
    cb = tb // CHUNK
    nt = seq // tb
    o = _gla_dims(pw, kw, vw, dk, dv)
    scale = dk ** -0.5

    def body(q_ref, k_ref, v_ref, r_ref, gl_ref, wgu_ref, b_ref, on_ref, st_ref, stp_ref, dg_ref,
             dq_ref, dk_ref, dv_ref, dr_ref, dpre_ref, db_ref, don_ref, ds_scr):
        hh = pl.program_id(0)
        t = pl.program_id(1)

        @pl.when(t == 0)
        def _():
            ds_scr[...] = jnp.zeros_like(ds_scr)
            db_ref[...] = jnp.zeros_like(db_ref)

        @pl.when(jnp.logical_and(hh == 0, t == 0))
        def _():
            don_ref[...] = jnp.zeros_like(don_ref)

        wgu = wgu_ref[...].astype(BF16)
        bias = b_ref[...]
        onorm = on_ref[...]
        has_prev = (t < nt - 1).astype(F32)
        r_i = lax.broadcasted_iota(jnp.int32, (CHUNK, CHUNK), 0)
        c_i = lax.broadcasted_iota(jnp.int32, (CHUNK, CHUNK), 1)
        strict = (c_i < r_i).astype(BF16)
        carry = ds_scr[...]
        db_acc = jnp.zeros((1, dk), F32)
        don_acc = jnp.zeros((1, dv), F32)
        for ci in reversed(range(cb)):
            rows = pl.ds(ci * CHUNK, CHUNK)
            pre, cum, total = _gla_gates(gl_ref[rows, :], wgu, bias)
            edec = jnp.exp(total - cum)
            decay = jnp.exp(total)
            kdec = k_ref[rows, :] * edec
            st = st_ref[ci]
            st_prev = st_ref[ci - 1] if ci > 0 else stp_ref[0] * has_prev
            stb = st.astype(BF16)
            qs = (q_ref[rows, :] * scale).astype(BF16)
            vb = v_ref[rows, :].astype(BF16)
            ov = lax.dot_general(qs, stb, (((1,), (1,)), ((), ())), preferred_element_type=F32)
            rstd = lax.rsqrt(jnp.mean(ov * ov, axis=-1, keepdims=True) + EPS)
            ohat = ov * rstd
            rv = r_ref[rows, :]
            sr = jax.nn.sigmoid(rv)
            dgv = dg_ref[rows, :]
            dy = dgv * (rv * sr)
            dr_ref[rows, :] = (dgv * (ohat * onorm) * (sr * (1.0 + rv * (1.0 - sr)))).astype(dr_ref.dtype)
            don_acc = don_acc + jnp.sum(dy * ohat, axis=0, keepdims=True)
            dohat = dy * onorm
            do = (rstd * (dohat - ohat * jnp.mean(dohat * ohat, axis=-1, keepdims=True))).astype(BF16)
            dq_ref[rows, :] = (jnp.dot(do, stb, preferred_element_type=F32) * scale).astype(dq_ref.dtype)
            dst = carry + lax.dot_general(do, qs, (((0,), (0,)), ((), ())), preferred_element_type=F32)
            dstb = dst.astype(BF16)
            dkdec = jnp.dot(vb, dstb, preferred_element_type=F32)
            dv_ref[rows, :] = lax.dot_general(kdec.astype(BF16), dstb, (((1,), (1,)), ((), ())),
                                              preferred_element_type=F32).astype(dv_ref.dtype)
            ddecay = jnp.sum(dst * st_prev, axis=0, keepdims=True)
            dk_ref[rows, :] = (dkdec * edec).astype(dk_ref.dtype)
            da = ddecay * decay + _tri_dot(strict, dkdec * kdec)
            dpre = da * (1.0 / GLA_GATE_TEMP) * (1.0 - jax.nn.sigmoid(pre))
            dpre_ref[rows, :] = dpre.astype(dpre_ref.dtype)
            db_acc = db_acc + jnp.sum(dpre, axis=0, keepdims=True)
            carry = dst * decay
        ds_scr[...] = carry
        db_ref[...] += db_acc
        don_ref[...] += don_acc

    rt = lambda t: nt - 1 - t
    in_specs = [
        pl.BlockSpec((tb, dk), lambda h, t: (rt(t), o['q0'] + h)),
        pl.BlockSpec((tb, dk), lambda h, t: (rt(t), o['k0'] + h)),
        pl.BlockSpec((tb, dv), lambda h, t: (rt(t), o['v0'] + h)),
        pl.BlockSpec((tb, dv), lambda h, t: (rt(t), o['r0'] + h)),
        pl.BlockSpec((tb, LANES), lambda h, t: (rt(t), o['g0'])),
        pl.BlockSpec((LANES, dk), lambda h, t: (0, h)),
        pl.BlockSpec((1, dk), lambda h, t: (0, h)),
        pl.BlockSpec((1, dv), lambda h, t: (0, 0)),
        pl.BlockSpec((None, cb, dv, dk), lambda h, t: (h, rt(t), 0, 0)),
        pl.BlockSpec((None, 1, dv, dk), lambda h, t: (h, jnp.maximum(rt(t) * cb - 1, 0), 0, 0)),
        pl.BlockSpec((tb, dv), lambda h, t: (rt(t), h)),
    ]
    out_shape = [jax.ShapeDtypeStruct((seq, kw), BF16), jax.ShapeDtypeStruct((seq, kw), BF16),
                 jax.ShapeDtypeStruct((seq, vw), BF16), jax.ShapeDtypeStruct((seq, vw), BF16),
                 jax.ShapeDtypeStruct((seq, kw), BF16),
                 jax.ShapeDtypeStruct((1, kw), F32), jax.ShapeDtypeStruct((1, dv), F32)]
    out_specs = [pl.BlockSpec((tb, dk), lambda h, t: (rt(t), h)),
                 pl.BlockSpec((tb, dk), lambda h, t: (rt(t), h)),
                 pl.BlockSpec((tb, dv), lambda h, t: (rt(t), h)),
                 pl.BlockSpec((tb, dv), lambda h, t: (rt(t), h)),
                 pl.BlockSpec((tb, dk), lambda h, t: (rt(t), h)),
                 pl.BlockSpec((1, dk), lambda h, t: (0, h)),
                 pl.BlockSpec((1, dv), lambda h, t: (0, 0))]
    return _pcall(
        body, name=name, grid=(heads, nt),
        out_shape=out_shape, in_specs=in_specs, out_specs=out_specs,
        scratch_shapes=[pltpu.VMEM((dv, dk), F32)],
        compiler_params=_params('arbitrary', 'arbitrary'),
    )(proj, proj, proj, proj, proj, wgu_pad, b_gate, o_norm, states, states, dgated)


def _cmul(ar, ai, br, bi):
    return ar * br - ai * bi, ar * bi + ai * br


def _gelu(y):
    c = math.sqrt(2.0 / math.pi)
    return 0.5 * y * (1.0 + jnp.tanh(c * (y + 0.044715 * y * y * y)))


def _gelu_grad(y):
    c = math.sqrt(2.0 / math.pi)
    th = jnp.tanh(c * (y + 0.044715 * y * y * y))
    return 0.5 * (1.0 + th) + 0.5 * y * (1.0 - th * th) * (c * (1.0 + 3.0 * 0.044715 * y * y))


def _power_pow2(ar, ai, n):
    assert n & (n - 1) == 0
    for _ in range(n.bit_length() - 1):
        ar, ai = _cmul(ar, ai, ar, ai)
    return ar, ai


def _s5_fwd(u, bre, bim, cre, cim, are, aim, dskip, name):
    seq, width = u.shape
    nb, ub, sb = bre.shape
    ls = seq // S5_SEGMENTS
    seg = S5_SEGMENTS

    def body(u_ref, bre_ref, bim_ref, cre_ref, cim_ref, are_ref, aim_ref, d_ref, y_ref, z_ref, xr_ref, xi_ref):
        uv = u_ref[...]
        ub16 = uv.astype(BF16)
        xr_ref[...] = jnp.dot(ub16, bre_ref[...].astype(BF16), preferred_element_type=F32)
        xi_ref[...] = jnp.dot(ub16, bim_ref[...].astype(BF16), preferred_element_type=F32)
        ar = jnp.broadcast_to(are_ref[...], (seg, sb))
        ai = jnp.broadcast_to(aim_ref[...], (seg, sb))

        def step(i, c):
            rows = pl.ds(pl.multiple_of(i * seg, seg), seg)
            pr, pi = _cmul(ar, ai, c[0], c[1])
            nr = pr + xr_ref[rows, :]
            ni = pi + xi_ref[rows, :]
            xr_ref[rows, :] = nr
            xi_ref[rows, :] = ni
            return nr, ni

        zero = jnp.zeros((seg, sb), F32)
        er, ei = lax.fori_loop(0, ls, step, (zero, zero), unroll=8)
        pr, pi = _power_pow2(ar, ai, ls)
        row = lax.broadcasted_iota(jnp.int32, (seg, sb), 0)
        sr, si = zero, zero
        for _ in range(seg - 1):
            tr, ti = _cmul(pr, pi, sr, si)
            sr = jnp.where(row == 0, 0.0, pltpu.roll(tr + er, 1, 0))
            si = jnp.where(row == 0, 0.0, pltpu.roll(ti + ei, 1, 0))

        def fix(i, c):
            rows = pl.ds(pl.multiple_of(i * seg, seg), seg)
            fr, fi = _cmul(c[0], c[1], sr, si)
            xr_ref[rows, :] += fr
            xi_ref[rows, :] += fi
            return _cmul(c[0], c[1], ar, ai)

        lax.fori_loop(0, ls, fix, (ar, ai), unroll=8)
        y = (jnp.dot(xr_ref[...].astype(BF16), cre_ref[...].astype(BF16), preferred_element_type=F32)
             - jnp.dot(xi_ref[...].astype(BF16), cim_ref[...].astype(BF16), preferred_element_type=F32)
             + d_ref[...] * uv)
        y_ref[...] = y
        z_ref[...] = _gelu(y).astype(z_ref.dtype)

    mat = lambda r, c: pl.BlockSpec((None, r, c), lambda b: (b, 0, 0))
    return _pcall(
        body, name=name, grid=(nb,),
        out_shape=[jax.ShapeDtypeStruct((seq, width), F32), jax.ShapeDtypeStruct((seq, width), BF16),
                   jax.ShapeDtypeStruct((seq, nb * sb), F32), jax.ShapeDtypeStruct((seq, nb * sb), F32)],
        in_specs=[pl.BlockSpec((seq, ub), lambda b: (0, b)), mat(ub, sb), mat(ub, sb), mat(sb, ub), mat(sb, ub),
                  mat(1, sb), mat(1, sb), pl.BlockSpec((1, ub), lambda b: (0, b))],
        out_specs=[pl.BlockSpec((seq, ub), lambda b: (0, b)), pl.BlockSpec((seq, ub), lambda b: (0, b)),
                   pl.BlockSpec((seq, sb), lambda b: (0, b)), pl.BlockSpec((seq, sb), lambda b: (0, b))],
        compiler_params=_params('parallel'),
    )(u, bre, bim, cre, cim, are, aim, dskip)


def _s5_bwd(dz, y, u, xr, xi, bre, bim, cre, cim, are, aim, dskip, name):
    seq, width = u.shape
    nb, ub, sb = bre.shape
    ls = seq // S5_SEGMENTS
    seg = S5_SEGMENTS

    def body(dz_ref, y_ref, u_ref, xr_ref, xi_ref, bre_ref, bim_ref, cre_ref, cim_ref, are_ref, aim_ref, d_ref,
             du_ref, dcr_ref, dci_ref, dbr_ref, dbi_ref, dar_ref, dai_ref, dd_ref, lr_ref, li_ref):
        uv = u_ref[...]
        dy = dz_ref[...] * _gelu_grad(y_ref[...])
        dd_ref[...] = jnp.sum(dy * uv, axis=0, keepdims=True)
        dyb = dy.astype(BF16)
        nt = (((1,), (1,)), ((), ()))
        tn = (((0,), (0,)), ((), ()))
        lr_ref[...] = lax.dot_general(dyb, cre_ref[...].astype(BF16), nt, preferred_element_type=F32)
        li_ref[...] = -lax.dot_general(dyb, cim_ref[...].astype(BF16), nt, preferred_element_type=F32)
        dcr_ref[...] = lax.dot_general(dyb, xr_ref[...].astype(BF16), tn, preferred_element_type=F32)
        dci_ref[...] = -lax.dot_general(dyb, xi_ref[...].astype(BF16), tn, preferred_element_type=F32)
        ar = jnp.broadcast_to(are_ref[...], (seg, sb))
        ai = jnp.broadcast_to(aim_ref[...], (seg, sb))
        nai = -ai

        def step(ii, c):
            rows = pl.ds(pl.multiple_of((ls - 1 - ii) * seg, seg), seg)
            pr, pi = _cmul(ar, nai, c[0], c[1])
            nr = pr + lr_ref[rows, :]
            ni = pi + li_ref[rows, :]
            lr_ref[rows, :] = nr
            li_ref[rows, :] = ni
            return nr, ni

        zero = jnp.zeros((seg, sb), F32)
        er, ei = lax.fori_loop(0, ls, step, (zero, zero), unroll=8)
        pr, pi = _power_pow2(ar, nai, ls)
        row = lax.broadcasted_iota(jnp.int32, (seg, sb), 0)
        rr, ri = zero, zero
        for _ in range(seg - 1):
            tr, ti = _cmul(pr, pi, rr, ri)
            rr = jnp.where(row == seg - 1, 0.0, pltpu.roll(tr + er, seg - 1, 0))
            ri = jnp.where(row == seg - 1, 0.0, pltpu.roll(ti + ei, seg - 1, 0))

        def corrected(rows, qr, qi):
            fr, fi = _cmul(qr, qi, rr, ri)
            nr = lr_ref[rows, :] + fr
            ni = li_ref[rows, :] + fi
            lr_ref[rows, :] = nr
            li_ref[rows, :] = ni
            return nr, ni

        def grad_a(nr, ni, xpr, xpi, accr, acci):
            return accr + nr * xpr + ni * xpi, acci + ni * xpr - nr * xpi

        def fix(ii, c):
            qr, qi, accr, acci = c
            i = ls - 1 - ii
            rows = pl.ds(pl.multiple_of(i * seg, seg), seg)
            prev = pl.ds(pl.multiple_of((i - 1) * seg, seg), seg)
            nr, ni = corrected(rows, qr, qi)
            accr, acci = grad_a(nr, ni, xr_ref[prev, :], xi_ref[prev, :], accr, acci)
            qr, qi = _cmul(qr, qi, ar, nai)
            return qr, qi, accr, acci

        qr, qi, accr, acci = lax.fori_loop(0, ls - 1, fix, (ar, nai, zero, zero), unroll=8)
        nr, ni = corrected(pl.ds(0, seg), qr, qi)
        last = pl.ds((ls - 1) * seg, seg)
        xpr = jnp.where(row == 0, 0.0, pltpu.roll(xr_ref[last, :], 1, 0))
        xpi = jnp.where(row == 0, 0.0, pltpu.roll(xi_ref[last, :], 1, 0))
        accr, acci = grad_a(nr, ni, xpr, xpi, accr, acci)
        dar_ref[...] = jnp.sum(accr, axis=0, keepdims=True)
        dai_ref[...] = jnp.sum(acci, axis=0, keepdims=True)
        lrb = lr_ref[...].astype(BF16)
        lib = li_ref[...].astype(BF16)
        ub16 = uv.astype(BF16)
        dbr_ref[...] = lax.dot_general(ub16, lrb, tn, preferred_element_type=F32)
        dbi_ref[...] = lax.dot_general(ub16, lib, tn, preferred_element_type=F32)
        du_ref[...] = (d_ref[...] * dy
                       + lax.dot_general(lrb, bre_ref[...].astype(BF16), nt, preferred_element_type=F32)
                       + lax.dot_general(lib, bim_ref[...].astype(BF16), nt, preferred_element_type=F32))

    mat = lambda r, c: pl.BlockSpec((None, r, c), lambda b: (b, 0, 0))
    col = lambda w: pl.BlockSpec((seq, w), lambda b: (0, b))
    return _pcall(
        body, name=name, grid=(nb,),
        out_shape=[jax.ShapeDtypeStruct((seq, width), F32)]
        + [jax.ShapeDtypeStruct((nb, ub, sb), F32)] * 4
        + [jax.ShapeDtypeStruct((nb, 1, sb), F32)] * 2
        + [jax.ShapeDtypeStruct((1, width), F32)],
        in_specs=[col(ub), col(ub), col(ub), col(sb), col(sb), mat(ub, sb), mat(ub, sb), mat(sb, ub), mat(sb, ub),
                  mat(1, sb), mat(1, sb), pl.BlockSpec((1, ub), lambda b: (0, b))],
        out_specs=[col(ub), mat(ub, sb), mat(ub, sb), mat(ub, sb), mat(ub, sb), mat(1, sb), mat(1, sb),
                   pl.BlockSpec((1, ub), lambda b: (0, b))],
        scratch_shapes=[pltpu.VMEM((seq, sb), F32), pltpu.VMEM((seq, sb), F32)],
        compiler_params=_params('parallel'),
    )(dz, y, u, xr, xi, bre, bim, cre, cim, are, aim, dskip)


def _s5_discretise(lam_re, lam_im, log_dt, b_re, b_im):
    lr = jnp.minimum(lam_re, S5_EIG_CLIP)
    li = lam_im
    dt = jnp.exp(log_dt)[:, None]
    mag = jnp.exp(lr * dt)
    ang = li * dt
    ab_re = mag * jnp.cos(ang)
    ab_im = mag * jnp.sin(ang)
    den = lr * lr + li * li
    nr = ab_re - 1.0
    f_re = (nr * lr + ab_im * li) / den
    f_im = (ab_im * lr - nr * li) / den
    bb_re = f_re[..., None] * b_re - f_im[..., None] * b_im
    bb_im = f_re[..., None] * b_im + f_im[..., None] * b_re
    return ab_re, ab_im, bb_re, bb_im


def _to_blocks(m):
    g, a, b = m.shape
    gb = S5_GROUPS_PER_BLOCK
    eye = jnp.eye(gb, dtype=m.dtype)
    return jnp.einsum('bgac,gh->bgahc', m.reshape(g // gb, gb, a, b), eye).reshape(g // gb, gb * a, gb * b)


def _from_blocks(m, a, b):
    nb = m.shape[0]
    gb = S5_GROUPS_PER_BLOCK
    eye = jnp.eye(gb, dtype=m.dtype)
    return jnp.einsum('bgahc,gh->bgac', m.reshape(nb, gb, a, gb, b), eye).reshape(nb * gb, a, b)


def _glu_fwd(o, h, name):
    half = o.shape[1] // 2

    def fn(ov, hv):
        return (hv + ov[:, :half] * jax.nn.sigmoid(ov[:, half:]),)
    return _rowwise(name, fn, [o, h], [], [(half, F32)], [], 256)[0]


def _glu_bwd(o, dout, name):
    half = o.shape[1] // 2

    def fn(ov, dv):
        val, gate = ov[:, :half], ov[:, half:]
        sg = jax.nn.sigmoid(gate)
        return (jnp.concatenate([dv * sg, dv * val * sg * (1.0 - sg)], axis=1),)
    return _rowwise(name, fn, [o, dout], [], [(2 * half, BF16)], [], 256)[0]


def _adam_math(w, g, m, v):
    m = ADAM_B1 * m + (1.0 - ADAM_B1) * g
    v = ADAM_B2 * v + (1.0 - ADAM_B2) * (g * g)
    m_hat = m / (1.0 - ADAM_B1 ** ADAM_STEP)
    v_hat = v / (1.0 - ADAM_B2 ** ADAM_STEP)
    delta = -ADAM_LR * (m_hat / (jnp.sqrt(v_hat) + ADAM_EPS) + ADAM_WD * w)
    return delta, m, v


def _adamw(w, m, v, grads, name):
    nl, rows, cols = w.shape
    tm = _tile(rows, max(8, (1 << 18) // cols // 8 * 8), unit=8)
    nbk = rows // tm

    def body(*refs):
        w_ref, m_ref, v_ref = refs[:3]
        g_refs = refs[3:3 + nl]
        go_ref, d_ref, mo_ref, vo_ref = refs[3 + nl:]
        layer = pl.program_id(0)
        g = g_refs[0][...]
        for l in range(1, nl):
            g = jnp.where(layer == l, g_refs[l][...], g)
        delta, mn, vn = _adam_math(w_ref[...], g, m_ref[...], v_ref[...])
        go_ref[...] = g
        d_ref[...] = delta
        mo_ref[...] = mn
        vo_ref[...] = vn

    stacked = pl.BlockSpec((None, tm, cols), lambda l, i: (l, i, 0))

    def g_spec(layer):
        return pl.BlockSpec((tm, cols), lambda l, i: (jnp.where(l == layer, i, jnp.where(l < layer, 0, nbk - 1)), 0))

    return _pcall(
        body, name=name, grid=(nl, nbk),
        out_shape=[jax.ShapeDtypeStruct(w.shape, F32)] * 4,
        in_specs=[stacked] * 3 + [g_spec(l) for l in range(nl)],
        out_specs=[stacked] * 4,
        compiler_params=_params('arbitrary', 'arbitrary'),
    )(w, m, v, *grads)


def _pack(arrs, rows_mult=512):
    flat = jnp.concatenate([a.reshape(-1) for a in arrs])
    total = flat.shape[0]
    rows = -(-total // LANES)
    rows = -(-rows // rows_mult) * rows_mult
    flat = jnp.pad(flat, (0, rows * LANES - total))
    return flat.reshape(rows, LANES)


def _unpack(packed, shapes):
    flat = packed.reshape(-1)
    out, off = [], 0
    for s in shapes:
        size = math.prod(s)
        out.append(flat[off:off + size].reshape(s))
        off += size
    return out


def _permute(a):
    seq, w = a.shape
    return a.reshape(S5_SEGMENTS, seq // S5_SEGMENTS, w).transpose(1, 0, 2).reshape(seq, w)


def _unpermute(a):
    seq, w = a.shape
    return a.reshape(seq // S5_SEGMENTS, S5_SEGMENTS, w).transpose(1, 0, 2).reshape(seq, w)


def kernel(x, gla_norm, gla_w_in, gla_w_gate_up, gla_b_gate, gla_o_norm, gla_w_out, s5_norm, s5_w_in, s5_lam_re, s5_lam_im, s5_log_dt, s5_b_re, s5_b_im, s5_c_re, s5_c_im, s5_d, s5_w_out, mlp_norm, mlp_w_up, mlp_w_down, final_norm, loss_target, m_gla_norm, m_gla_w_in, m_gla_w_gate_up, m_gla_b_gate, m_gla_o_norm, m_gla_w_out, m_s5_norm, m_s5_w_in, m_s5_lam_re, m_s5_lam_im, m_s5_log_dt, m_s5_b_re, m_s5_b_im, m_s5_c_re, m_s5_c_im, m_s5_d, m_s5_w_out, m_mlp_norm, m_mlp_w_up, m_mlp_w_down, m_final_norm, v_gla_norm, v_gla_w_in, v_gla_w_gate_up, v_gla_b_gate, v_gla_o_norm, v_gla_w_out, v_s5_norm, v_s5_w_in, v_s5_lam_re, v_s5_lam_im, v_s5_log_dt, v_s5_b_re, v_s5_b_im, v_s5_c_re, v_s5_c_im, v_s5_d, v_s5_w_out, v_mlp_norm, v_mlp_w_up, v_mlp_w_down, v_final_norm):
    weights = dict(gla_norm=gla_norm, gla_w_in=gla_w_in, gla_w_gate_up=gla_w_gate_up, gla_b_gate=gla_b_gate, gla_o_norm=gla_o_norm, gla_w_out=gla_w_out, s5_norm=s5_norm, s5_w_in=s5_w_in, s5_lam_re=s5_lam_re, s5_lam_im=s5_lam_im, s5_log_dt=s5_log_dt, s5_b_re=s5_b_re, s5_b_im=s5_b_im, s5_c_re=s5_c_re, s5_c_im=s5_c_im, s5_d=s5_d, s5_w_out=s5_w_out, mlp_norm=mlp_norm, mlp_w_up=mlp_w_up, mlp_w_down=mlp_w_down, final_norm=final_norm)
    mom1 = dict(gla_norm=m_gla_norm, gla_w_in=m_gla_w_in, gla_w_gate_up=m_gla_w_gate_up, gla_b_gate=m_gla_b_gate, gla_o_norm=m_gla_o_norm, gla_w_out=m_gla_w_out, s5_norm=m_s5_norm, s5_w_in=m_s5_w_in, s5_lam_re=m_s5_lam_re, s5_lam_im=m_s5_lam_im, s5_log_dt=m_s5_log_dt, s5_b_re=m_s5_b_re, s5_b_im=m_s5_b_im, s5_c_re=m_s5_c_re, s5_c_im=m_s5_c_im, s5_d=m_s5_d, s5_w_out=m_s5_w_out, mlp_norm=m_mlp_norm, mlp_w_up=m_mlp_w_up, mlp_w_down=m_mlp_w_down, final_norm=m_final_norm)
    mom2 = dict(gla_norm=v_gla_norm, gla_w_in=v_gla_w_in, gla_w_gate_up=v_gla_w_gate_up, gla_b_gate=v_gla_b_gate, gla_o_norm=v_gla_o_norm, gla_w_out=v_gla_w_out, s5_norm=v_s5_norm, s5_w_in=v_s5_w_in, s5_lam_re=v_s5_lam_re, s5_lam_im=v_s5_lam_im, s5_log_dt=v_s5_log_dt, s5_b_re=v_s5_b_re, s5_b_im=v_s5_b_im, s5_c_re=v_s5_c_re, s5_c_im=v_s5_c_im, s5_d=v_s5_d, s5_w_out=v_s5_w_out, mlp_norm=v_mlp_norm, mlp_w_up=v_mlp_w_up, mlp_w_down=v_mlp_w_down, final_norm=v_final_norm)
    names = list(weights)
    big = ['gla_w_in', 'gla_w_out', 's5_w_in', 's5_w_out', 'mlp_w_up', 'mlp_w_down']
    small = [n for n in names if n not in big]

    chip = 2 * lax.axis_index('x') + lax.axis_index('y')
    h0 = x[0]
    target = loss_target[0]
    seq, dm = h0.shape
    depth = mlp_norm.shape[0]
    n_gla = gla_norm.shape[0]
    n_s5 = s5_lam_re.shape[0]
    rank = gla_w_gate_up.shape[1]
    kw = gla_b_gate.shape[1]
    dv = gla_o_norm.shape[1]
    in_w = 4 * gla_w_in.shape[2]
    vw = (in_w - rank - 2 * kw) // 2
    heads = vw // dv
    dk = kw // heads
    pw = -(-in_w // LANES) * LANES
    s5w = s5_w_in.shape[2]
    n_grp, n_state, grp = s5_b_re.shape[1:]
    hid = 4 * mlp_w_up.shape[2]
    tb = min(seq, 8 * CHUNK)
    tm = _tile(seq, 1024)

    g_w_in = _gather_weight(gla_w_in, 'ag_gla_w_in')
    g_gla_out = _gather_weight(gla_w_out, 'ag_gla_w_out')
    g_s5_in = _gather_weight(s5_w_in, 'ag_s5_w_in')
    g_s5_out = _gather_weight(s5_w_out, 'ag_s5_w_out')
    g_up = _gather_weight(mlp_w_up, 'ag_mlp_w_up')
    g_down = _gather_weight(mlp_w_down, 'ag_mlp_w_down')
    sharded_small = [gla_w_gate_up, s5_norm, s5_d]
    gathered_small = _exchange(_pack(sharded_small), 'xy', 'bcast', 'ag_small')
    parts = [_unpack(gathered_small[k], [a.shape for a in sharded_small]) for k in range(4)]
    wgu_full = jnp.concatenate([p[0] for p in parts], axis=2)
    s5_norm_full = jnp.concatenate([p[1] for p in parts], axis=1)
    s5_d_full = jnp.concatenate([p[2] for p in parts], axis=1)

    def gla_w_in_padded(j):
        nh = g_w_in.shape[2]
        wj = g_w_in[j // nh, :, j % nh]
        wj = wj.transpose(1, 0, 2).reshape(dm, in_w)
        return jnp.pad(wj, ((0, 0), (0, pw - in_w)))

    grads = {n: [None] * weights[n].shape[0] for n in names if n != 'final_norm'}

    saved = []
    h = h0
    for i in range(depth):
        j = i // 2
        rec = {}
        if i % 2 == 0:
            rec['h_in'] = h
            hn = _norm_fwd(h, gla_norm[j:j + 1], 'gla_norm_fwd')
            w_in_pad = gla_w_in_padded(j)
            proj = _mm('gla_proj', 'nn', _plain(hn), _plain(w_in_pad), (seq, pw, dm),
                       [((seq, pw), F32, _plain_shape(None))], (tm, _tile(pw, 1024), dm))[0]
            wgu_pad = jnp.pad(wgu_full[j], ((0, LANES - rank), (0, 0)))
            gated, states = _gla_scan_fwd(proj, wgu_pad, gla_b_gate[j:j + 1], gla_o_norm[j:j + 1],
                                          heads, kw, vw, tb, 'gla_scan_fwd')
            h = _mm('gla_out', 'nn', _plain(gated), _w_rows(g_gla_out, j), (seq, dm, vw),
                    [((seq, dm), F32, _plain_shape(None))],
                    (tm, _tile(dm, 1024), _tile(g_gla_out.shape[3], 1024)),
                    epilogue=lambda acc, hv: (acc + hv,), extras=[_plain(h)])[0]
            rec.update(hn=hn, w_in_pad=w_in_pad, proj=proj, wgu_pad=wgu_pad, gated=gated, states=states)
        else:
            hp = _permute(h)
            rec['h_in'] = hp
            hn = _norm_fwd(hp, s5_norm_full[j:j + 1], 's5_norm_fwd')
            u = _mm('s5_in', 'nn', _plain(hn), _w_rows(g_s5_in, j), (seq, s5w, dm),
                    [((seq, s5w), F32, _plain_shape(None))],
                    (tm, _tile(s5w, 1024), _tile(g_s5_in.shape[3], 1024)))[0]
            disc, disc_vjp = jax.vjp(_s5_discretise, s5_lam_re[j], s5_lam_im[j], s5_log_dt[j], s5_b_re[j], s5_b_im[j])
            ab_re, ab_im, bb_re, bb_im = disc
            bre = _to_blocks(bb_re.transpose(0, 2, 1))
            bim = _to_blocks(bb_im.transpose(0, 2, 1))
            cre = _to_blocks(s5_c_re[j].transpose(0, 2, 1))
            cim = _to_blocks(s5_c_im[j].transpose(0, 2, 1))
            nb = n_grp // S5_GROUPS_PER_BLOCK
            are = ab_re.reshape(nb, 1, S5_GROUPS_PER_BLOCK * n_state)
            aim = ab_im.reshape(nb, 1, S5_GROUPS_PER_BLOCK * n_state)
            dskip = s5_d_full[j:j + 1]
            y, z, xr, xi = _s5_fwd(u, bre, bim, cre, cim, are, aim, dskip, 's5_scan_fwd')
            o = _mm('s5_out', 'nn', _plain(z), _w_cols(g_s5_out, j), (seq, 2 * dm, s5w),
                    [((seq, 2 * dm), F32, _plain_shape(None))],
                    (tm, _tile(g_s5_out.shape[4], 1024), _tile(s5w, 1024)))[0]
            h = _unpermute(_glu_fwd(o, hp, 's5_glu_fwd'))
            rec.update(hn=hn, u=u, y=y, z=z, xr=xr, xi=xi, o=o, mats=(bre, bim, cre, cim, are, aim, dskip),
                       disc_vjp=disc_vjp)
        rec['h_mid'] = h
        hn2 = _norm_fwd(h, mlp_norm[i:i + 1], 'mlp_norm_fwd')
        act, act2 = _mm('mlp_up', 'nn', _plain(hn2), _w_cols(g_up, i), (seq, hid, dm),
                        [((seq, hid), BF16, _plain_shape(None))] * 2,
                        (tm, _tile(g_up.shape[4], 1024), dm),
                        epilogue=lambda acc: (jnp.maximum(acc, 0.0), jnp.square(jnp.maximum(acc, 0.0))))
        h = _mm('mlp_down', 'nn', _plain(act2), _w_rows(g_down, i), (seq, dm, hid),
                [((seq, dm), F32, _plain_shape(None))],
                (tm, _tile(dm, 1024), _tile(g_down.shape[3], 2048)),
                epilogue=lambda acc, hv: (acc + hv,), extras=[_plain(h)])[0]
        rec.update(hn2=hn2, act=act, act2=act2)
        saved.append(rec)

    dh, loss_cols, d_final = _loss_head(h, target, final_norm.reshape(1, dm), 'loss_head')
    loss = lax.psum(jnp.sum(loss_cols), ('x', 'y', 'c'))
    grads['final_norm'] = [d_final.reshape(dm)]

    big_grads = {n: [None] * weights[n].shape[0] for n in big}
    for i in reversed(range(depth)):
        j = i // 2
        rec = saved[i]
        r_dn, c_dn = mlp_w_down.shape[1:]
        shape, spec = _dw_rows(r_dn, c_dn)
        dw = _mm('mlp_down_dw', 'tn', _plain(rec['act2']), _plain(dh), (hid, dm, seq),
                 [(shape, BF16, spec)], (_tile(r_dn // 2, 1024), _tile(c_dn, 1024), seq))[0]
        big_grads['mlp_w_down'][i] = _reduce_scatter(dw, 'rs_mlp_down')
        dpre = _mm('mlp_down_dx', 'nt', _plain(dh), _w_rows(g_down, i), (seq, hid, dm),
                   [((seq, hid), BF16, _plain_shape(None))],
                   (tm, _tile(g_down.shape[3], 1024), dm),
                   epilogue=lambda acc, av: (acc * (2.0 * av.astype(F32)),), extras=[_plain(rec['act'])])[0]
        r_up, c_up = mlp_w_up.shape[1:]
        shape, spec = _dw_cols(r_up, c_up)
        dw = _mm('mlp_up_dw', 'tn', _plain(rec['hn2']), _plain(dpre), (dm, hid, seq),
                 [(shape, BF16, spec)], (_tile(r_up // 2, 1024), _tile(c_up, 1024), seq))[0]
        big_grads['mlp_w_up'][i] = _reduce_scatter(dw, 'rs_mlp_up')
        dhn = _mm('mlp_up_dx', 'nt', _plain(dpre), _w_cols(g_up, i), (seq, dm, hid),
                  [((seq, dm), F32, _plain_shape(None))],
                  (tm, _tile(dm, 1024), _tile(g_up.shape[4], 2048)))[0]
        dh, dg = _norm_bwd(rec['h_mid'], dhn, dh, mlp_norm[i:i + 1], 'mlp_norm_bwd')
        grads['mlp_norm'][i] = dg[0]

        if i % 2 == 0:
            r_o, c_o = gla_w_out.shape[1:]
            shape, spec = _dw_rows(r_o, c_o)
            dw = _mm('gla_out_dw', 'tn', _plain(rec['gated']), _plain(dh), (vw, dm, seq),
                     [(shape, BF16, spec)], (_tile(r_o // 2, 1024), _tile(c_o, 1024), seq))[0]
            big_grads['gla_w_out'][j] = _reduce_scatter(dw, 'rs_gla_out')
            dgated = _mm('gla_out_dx', 'nt', _plain(dh), _w_rows(g_gla_out, j), (seq, vw, dm),
                         [((seq, vw), F32, _plain_shape(None))],
                         (tm, _tile(g_gla_out.shape[3], 1024), dm))[0]
            dq, dkk, dvv, dr, dpre_g, db, don = _gla_scan_bwd(
                rec['proj'], rec['wgu_pad'], gla_b_gate[j:j + 1], gla_o_norm[j:j + 1], rec['states'], dgated,
                heads, kw, vw, tb, 'gla_scan_bwd')
            grads['gla_b_gate'][j] = db[0]
            grads['gla_o_norm'][j] = don[0]
            dgl = _mm('gla_gate_dx', 'nt', _plain(dpre_g), _plain(rec['wgu_pad']), (seq, LANES, kw),
                      [((seq, LANES), BF16, _plain_shape(None))], (tm, LANES, kw))[0]
            g_low = rec['proj'][:, pw - LANES:]
            dwgu = _mm('gla_gate_dw', 'tn', _plain(g_low), _plain(dpre_g), (LANES, kw, seq),
                       [((LANES, kw), F32, _plain_shape(None))], (LANES, kw, seq))[0]
            grads['gla_w_gate_up'][j] = dwgu[:rank]
            dproj = jnp.concatenate([dq, dkk, dvv, dr, dgl], axis=1)
            dw_pad = _mm('gla_proj_dw', 'tn', _plain(rec['hn']), _plain(dproj), (dm, pw, seq),
                         [((dm, pw), BF16, _plain_shape(None))], (_tile(dm, 1024), _tile(pw, 1024), seq))[0]
            shard_w = in_w // 4
            dw = dw_pad[:, :in_w].reshape(2, dm // 2, 4, shard_w).transpose(0, 2, 1, 3)
            big_grads['gla_w_in'][j] = _reduce_scatter(dw, 'rs_gla_in')
            dhn = _mm('gla_proj_dx', 'nt', _plain(dproj), _plain(rec['w_in_pad']), (seq, dm, pw),
                      [((seq, dm), F32, _plain_shape(None))], (tm, _tile(dm, 1024), _tile(pw, 1024)))[0]
            dh, dg = _norm_bwd(rec['h_in'], dhn, dh, gla_norm[j:j + 1], 'gla_norm_bwd')
            grads['gla_norm'][j] = dg[0]
        else:
            dhp = _permute(dh)
            do = _glu_bwd(rec['o'], dhp, 's5_glu_bwd')
            r_o, c_o = s5_w_out.shape[1:]
            shape, spec = _dw_cols(r_o, c_o)
            dw = _mm('s5_out_dw', 'tn', _plain(rec['z']), _plain(do), (s5w, 2 * dm, seq),
                     [(shape, BF16, spec)], (_tile(r_o // 2, 1024), _tile(c_o, 1024), seq))[0]
            big_grads['s5_w_out'][j] = _reduce_scatter(dw, 'rs_s5_out')
            dz = _mm('s5_out_dx', 'nt', _plain(do), _w_cols(g_s5_out, j), (seq, s5w, 2 * dm),
                     [((seq, s5w), F32, _plain_shape(None))],
                     (tm, _tile(s5w, 1024), _tile(g_s5_out.shape[4], 1024)))[0]
            bre, bim, cre, cim, are, aim, dskip = rec['mats']
            du, dcr, dci, dbr, dbi, dar, dai, dd = _s5_bwd(dz, rec['y'], rec['u'], rec['xr'], rec['xi'],
                                                           bre, bim, cre, cim, are, aim, dskip, 's5_scan_bwd')
            grads['s5_c_re'][j] = _from_blocks(dcr, grp, n_state)
            grads['s5_c_im'][j] = _from_blocks(dci, grp, n_state)
            dbb_re = _from_blocks(dbr, grp, n_state).transpose(0, 2, 1)
            dbb_im = _from_blocks(dbi, grp, n_state).transpose(0, 2, 1)
            d_lr, d_li, d_dt, d_bre, d_bim = rec['disc_vjp'](
                (dar.reshape(n_grp, n_state), dai.reshape(n_grp, n_state), dbb_re, dbb_im))
            grads['s5_lam_re'][j] = d_lr
            grads['s5_lam_im'][j] = d_li
            grads['s5_log_dt'][j] = d_dt
            grads['s5_b_re'][j] = d_bre
            grads['s5_b_im'][j] = d_bim
            grads['s5_d'][j] = dd[0]
            r_i, c_i = s5_w_in.shape[1:]
            shape, spec = _dw_rows(r_i, c_i)
            dw = _mm('s5_in_dw', 'tn', _plain(rec['hn']), _plain(du), (dm, s5w, seq),
                     [(shape, BF16, spec)], (_tile(r_i // 2, 1024), _tile(c_i, 1024), seq))[0]
            big_grads['s5_w_in'][j] = _reduce_scatter(dw, 'rs_s5_in')
            dhn = _mm('s5_in_dx', 'nt', _plain(du), _w_rows(g_s5_in, j), (seq, dm, s5w),
                      [((seq, dm), F32, _plain_shape(None))],
                      (tm, _tile(g_s5_in.shape[3], 1024), _tile(s5w, 1024)))[0]
            dhp, dg = _norm_bwd(rec['h_in'], dhn, dhp, s5_norm_full[j:j + 1], 's5_norm_bwd')
            dh = _unpermute(dhp)
            grads['s5_norm'][j] = dg[0]
    grad_x = dh[None]

    local_small = [jnp.stack(grads[n]) if n != 'final_norm' else grads[n][0] for n in small]
    full_shapes = [a.shape for a in local_small]
    gathered = _exchange(_exchange(_pack(local_small), 'xy', 'bcast', 'ar_small_xy'), 'c', 'bcast', 'ar_small_c')
    rows = gathered.shape[2]
    summed = _sum_slots(gathered.reshape(8, rows, LANES), F32, 'ar_small_sum')
    small_full = dict(zip(small, _unpack(summed, full_shapes)))
    small_grad = {}
    for n in small:
        g = small_full[n]
        if g.shape != weights[n].shape:
            ax = [a for a in range(g.ndim) if g.shape[a] != weights[n].shape[a]][0]
            g = lax.dynamic_slice_in_dim(g, chip * weights[n].shape[ax], weights[n].shape[ax], axis=ax)
        small_grad[n] = g

    out_g, out_d, out_m, out_v = {}, {}, {}, {}
    for n in big:
        out_g[n], out_d[n], out_m[n], out_v[n] = _adamw(weights[n], mom1[n], mom2[n], big_grads[n], 'adamw_' + n)
    shapes = [weights[n].shape for n in small]
    pw_, pm_, pv_, pg_ = (_pack([d[n] for n in small]) for d in (weights, mom1, mom2, small_grad))
    _, sd, sm, sv = _adamw(pw_[None], pm_[None], pv_[None], [pg_], 'adamw_small')
    for n, d_, m_, v_ in zip(small, _unpack(sd[0], shapes), _unpack(sm[0], shapes), _unpack(sv[0], shapes)):
        out_g[n], out_d[n], out_m[n], out_v[n] = small_grad[n], d_, m_, v_

    return (loss, grad_x, *[out_g[n] for n in names], *[out_d[n] for n in names],
            *[out_m[n] for n in names], *[out_v[n] for n in names])
```

```python
import functools
import math

import jax
import jax.numpy as jnp
from jax import lax
from jax.experimental import pallas as pl
from jax.experimental.pallas import tpu as pltpu

F32 = jnp.float32
BF16 = jnp.bfloat16

EPS = 1e-6
CHUNK = 64
GLA_GATE_TEMP = 16.0
S5_EIG_CLIP = -1e-4
S5_SEGMENTS = 8
S5_GROUPS_PER_BLOCK = 8
LANES = 128
ADAM_LR = 0.001
ADAM_B1 = 0.9
ADAM_B2 = 0.999
ADAM_EPS = 1e-08
ADAM_WD = 0.01
ADAM_STEP = 10
VMEM_LIMIT_BYTES = 56 * 1024 * 1024
PAIR_PIECE_BYTES = 2 * 1024 * 1024
PAIR_VMEM_BYTES = 40 * 1024 * 1024

MESH = pl.DeviceIdType.MESH
ANY = pl.BlockSpec(memory_space=pl.ANY)
IN_VMEM = pl.BlockSpec(memory_space=pltpu.VMEM)


def _pcall(body, **kw):
    return pl.pallas_call(body, **kw)


def _params(*sem):
    return pltpu.CompilerParams(dimension_semantics=sem, vmem_limit_bytes=VMEM_LIMIT_BYTES)


def _tile(dim, target, unit=LANES):
    if dim <= target:
        return dim
    best = None
    for t in range(unit, target + 1, unit):
        if dim % t == 0:
            best = t
    assert best is not None, (dim, target)
    return best


def _exchange(x, group, mode, name):
    n = 2 if group == 'c' else 4
    blk = x.shape if mode == 'bcast' else x.shape[1:]
    if mode == 'a2a':
        assert x.shape[0] == n
    flips = [(0, 0, 1)] if group == 'c' else [(1, 0, 0), (0, 1, 0), (1, 1, 0)]
    itemsize = jnp.dtype(x.dtype).itemsize
    staged = group == 'c' and (x.size + n * math.prod(blk)) * itemsize <= PAIR_VMEM_BYTES
    if group == 'c' and not staged:
        return _pair_exchange_chunked(x, mode, name)

    def body(x_ref, y_ref, send_sems, recv_sems, local_sem):
        ix, iy, ic = lax.axis_index('x'), lax.axis_index('y'), lax.axis_index('c')

        def slot(px, py, pc):
            return pc if group == 'c' else 2 * px + py

        def src(px, py, pc):
            if mode == 'a2a':
                return x_ref.at[slot(px, py, pc)]
            if mode == 'bcast_c':
                return x_ref.at[ic]
            return x_ref

        me = (ix, iy, ic)
        local = pltpu.make_async_copy(src(*me), y_ref.at[slot(*me)], local_sem)
        local.start()
        peers = []
        for fx, fy, fc in flips:
            peers.append((1 - ix if fx else ix, 1 - iy if fy else iy, 1 - ic if fc else ic))
        sends = []
        for k, peer in enumerate(peers):
            cp = pltpu.make_async_remote_copy(
                src_ref=src(*peer), dst_ref=y_ref.at[slot(*me)],
                send_sem=send_sems.at[k], recv_sem=recv_sems.at[k],
                device_id=peer, device_id_type=MESH)
            cp.start()
            sends.append(cp)
        for k, peer in enumerate(peers):
            pltpu.make_async_remote_copy(
                src_ref=src(*peer), dst_ref=y_ref.at[slot(*peer)],
                send_sem=send_sems.at[k], recv_sem=recv_sems.at[k],
                device_id=peer, device_id_type=MESH).wait_recv()
        for cp in sends:
            cp.wait_send()
        local.wait()

    return _pcall(
        body, name=name,
        out_shape=jax.ShapeDtypeStruct((n,) + tuple(blk), x.dtype),
        in_specs=[IN_VMEM if staged else ANY], out_specs=IN_VMEM if staged else ANY,
        scratch_shapes=[pltpu.SemaphoreType.DMA((len(flips),)),
                        pltpu.SemaphoreType.DMA((len(flips),)),
                        pltpu.SemaphoreType.DMA(())],
        compiler_params=pltpu.CompilerParams(vmem_limit_bytes=VMEM_LIMIT_BYTES),
    )(x)


def _split_axis(blk, dtype, piece_bytes):
    itemsize = jnp.dtype(dtype).itemsize
    sublanes = 8 * 4 // itemsize
    want = max(1, math.prod(blk) * itemsize // piece_bytes)
    for pieces in [s for s in (64, 32, 16, 8, 4, 2) if s <= want]:
        for ax in range(len(blk) - 1):
            unit = sublanes if ax == len(blk) - 2 else 1
            if blk[ax] % (pieces * unit) == 0:
                return ax, pieces
    return 0, 1


def _pair_exchange_chunked(x, mode, name):
    blk = x.shape if mode == 'bcast' else x.shape[1:]
    ax, pieces = _split_axis(blk, x.dtype, PAIR_PIECE_BYTES)
    step = blk[ax] // pieces
    piece_shape = tuple(blk[:ax]) + (step,) + tuple(blk[ax + 1:])

    def piece(ref, p):
        return ref.at[(slice(None),) * ax + (pl.ds(p * step, step),)]

    def body(x_ref, y_ref, out_buf, in_buf, send_sems, recv_sems, stage_sems, drain_sems, local_sem, credit_sem):
        ix, iy, ic = lax.axis_index('x'), lax.axis_index('y'), lax.axis_index('c')
        sibling = (ix, iy, 1 - ic)
        mine = x_ref.at[ic] if mode != 'bcast' else x_ref
        theirs = x_ref.at[1 - ic] if mode == 'a2a' else mine
        local = pltpu.make_async_copy(mine, y_ref.at[ic], local_sem)
        local.start()

        def stage(p):
            return pltpu.make_async_copy(piece(theirs, p), out_buf.at[p % 2], stage_sems.at[p % 2])

        def remote(p):
            return pltpu.make_async_remote_copy(
                src_ref=out_buf.at[p % 2], dst_ref=in_buf.at[p % 2],
                send_sem=send_sems.at[p], recv_sem=recv_sems.at[p],
                device_id=sibling, device_id_type=MESH)

        def drain(p):
            return pltpu.make_async_copy(in_buf.at[p % 2], piece(y_ref.at[1 - ic], p), drain_sems.at[p % 2])

        stage(0).start()
        for p in range(pieces):
            stage(p).wait()
            if p >= 2:
                pl.semaphore_wait(credit_sem, 1)
            remote(p).start()
            if p + 1 < pieces:
                if p >= 1:
                    remote(p - 1).wait_send()
                stage(p + 1).start()
            remote(p).wait_recv()
            drain(p).start()
            drain(p).wait()
            if p + 2 < pieces:
                pl.semaphore_signal(credit_sem, inc=1, device_id=sibling, device_id_type=MESH)
        for p in range(max(0, pieces - 2), pieces):
            remote(p).wait_send()
        local.wait()

    return _pcall(
        body, name=name,
        out_shape=jax.ShapeDtypeStruct((2,) + tuple(blk), x.dtype),
        in_specs=[ANY], out_specs=ANY,
        scratch_shapes=[pltpu.VMEM((2,) + piece_shape, x.dtype), pltpu.VMEM((2,) + piece_shape, x.dtype),
                        pltpu.SemaphoreType.DMA((pieces,)), pltpu.SemaphoreType.DMA((pieces,)),
                        pltpu.SemaphoreType.DMA((2,)), pltpu.SemaphoreType.DMA((2,)),
                        pltpu.SemaphoreType.DMA(()), pltpu.SemaphoreType.REGULAR],
        compiler_params=pltpu.CompilerParams(vmem_limit_bytes=VMEM_LIMIT_BYTES),
    )(x)


def _sum_slots(y, out_dtype, name):
    n, rows, cols = y.shape
    tm = _tile(rows, max(8, (1 << 20) // (n * cols) // 8 * 8), unit=8)

    def body(y_ref, o_ref):
        acc = y_ref[0].astype(F32)
        for k in range(1, n):
            acc = acc + y_ref[k].astype(F32)
        o_ref[...] = acc.astype(o_ref.dtype)

    return _pcall(
        body, name=name, grid=(rows // tm,),
        out_shape=jax.ShapeDtypeStruct((rows, cols), out_dtype),
        in_specs=[pl.BlockSpec((n, tm, cols), lambda i: (0, i, 0))],
        out_specs=pl.BlockSpec((tm, cols), lambda i: (i, 0)),
        compiler_params=_params('parallel'),
    )(y)


def _gather_weight(w, name):
    nl, rows, cols = w.shape
    nh = nl // 2
    wb = w.astype(BF16).reshape(2, nh, rows, cols)
    y1 = _exchange(wb, 'xy', 'bcast_c', name + '_xy')
    return _exchange(y1, 'c', 'bcast', name + '_c')


def _reduce_scatter(dw, name):
    _, _, rh, cols = dw.shape
    ya = _exchange(dw, 'c', 'a2a', name + '_pair')
    pre = _sum_slots(ya.reshape(2, 4 * rh, cols), BF16, name + '_pairsum').reshape(4, rh, cols)
    yb = _exchange(pre, 'xy', 'a2a', name + '_chips')
    fin = _sum_slots(yb, F32, name + '_chipsum')
    yc = _exchange(fin, 'c', 'bcast', name + '_back')
    return yc.reshape(2 * rh, cols)


class _Op:
    def __init__(self, arr, spec):
        self.arr = arr
        self.spec = spec


def _plain(arr):
    return _Op(arr, lambda t0, t1: ((t0, t1), lambda b0, b1: (b0, b1)))


def _plain_shape(shape):
    return lambda t0, t1: ((t0, t1), lambda b0, b1: (b0, b1))


def _w_cols(g, j):
    _, _, nh, rows, cols = g.shape
    ch, lj = j // nh, j % nh

    def spec(t0, t1):
        assert rows % t0 == 0 and cols % t1 == 0, (rows, cols, t0, t1)
        q = cols // t1
        return (None, None, None, t0, t1), lambda b0, b1: (ch, b1 // q, lj, b0, b1 % q)
    return _Op(g, spec)


def _w_rows(g, j):
    _, _, nh, rows, cols = g.shape
    ch, lj = j // nh, j % nh

    def spec(t0, t1):
        assert rows % t0 == 0 and cols % t1 == 0, (rows, cols, t0, t1)
        q = rows // t0
        return (None, None, None, t0, t1), lambda b0, b1: (ch, b0 // q, lj, b0 % q, b1)
    return _Op(g, spec)


def _dw_cols(rows, cols):
    rh = rows // 2

    def spec(t0, t1):
        assert rh % t0 == 0 and cols % t1 == 0, (rh, cols, t0, t1)
        qr, qc = rh // t0, cols // t1
        return (None, None, t0, t1), lambda b0, b1: (b0 // qr, b1 // qc, b0 % qr, b1 % qc)
    return (2, 4, rh, cols), spec


def _dw_rows(rows, cols):
    rh = rows // 2

    def spec(t0, t1):
        assert rh % t0 == 0 and cols % t1 == 0, (rh, cols, t0, t1)
        qr = rh // t0
        return (None, None, t0, t1), lambda b0, b1: ((b0 // qr) % 2, b0 // (2 * qr), b0 % qr, b1)
    return (2, 4, rh, cols), spec


def _mm(name, mode, a, b, dims, outs, tiles, epilogue=None, extras=()):
    m, n, k = dims
    tm, tn, tk = tiles
    assert m % tm == 0 and n % tn == 0 and k % tk == 0, (name, dims, tiles)
    nk = k // tk
    if mode == 'nn':
        a_t, a_ix, b_t, b_ix, ca, cb = (tm, tk), (lambda i, j, kk: (i, kk)), (tk, tn), (lambda i, j, kk: (kk, j)), 1, 0
    elif mode == 'nt':
        a_t, a_ix, b_t, b_ix, ca, cb = (tm, tk), (lambda i, j, kk: (i, kk)), (tn, tk), (lambda i, j, kk: (j, kk)), 1, 1
    else:
        a_t, a_ix, b_t, b_ix, ca, cb = (tk, tm), (lambda i, j, kk: (kk, i)), (tk, tn), (lambda i, j, kk: (kk, j)), 0, 0
    a_blk, a_fn = a.spec(*a_t)
    b_blk, b_fn = b.spec(*b_t)
    in_specs = [pl.BlockSpec(a_blk, lambda i, j, kk: a_fn(*a_ix(i, j, kk))),
                pl.BlockSpec(b_blk, lambda i, j, kk: b_fn(*b_ix(i, j, kk)))]
    operands = [a.arr, b.arr]
    for e in extras:
        e_blk, e_fn = e.spec(tm, tn)
        in_specs.append(pl.BlockSpec(e_blk, functools.partial(lambda i, j, kk, f: f(i, j), f=e_fn)))
        operands.append(e.arr)
    out_shapes, out_specs = [], []
    for shape, dtype, spec in outs:
        o_blk, o_fn = spec(tm, tn)
        out_shapes.append(jax.ShapeDtypeStruct(shape, dtype))
        out_specs.append(pl.BlockSpec(o_blk, functools.partial(lambda i, j, kk, f: f(i, j), f=o_fn)))
    n_ex, n_out = len(extras), len(outs)
    if epilogue is None:
        epilogue = lambda acc: (acc,)

    def body(a_ref, b_ref, *rest):
        ex_refs = rest[:n_ex]
        out_refs = rest[n_ex:n_ex + n_out]
        p = lax.dot_general(a_ref[...].astype(BF16), b_ref[...].astype(BF16),
                            (((ca,), (cb,)), ((), ())), preferred_element_type=F32)

        def finish(acc):
            res = epilogue(acc, *[r[...] for r in ex_refs])
            for o_ref, val in zip(out_refs, res):
                o_ref[...] = val.astype(o_ref.dtype)

        if nk == 1:
            finish(p)
        else:
            acc_ref = rest[n_ex + n_out]
            kk = pl.program_id(2)

            @pl.when(kk == 0)
            def _():
                acc_ref[...] = p

            @pl.when(kk > 0)
            def _():
                acc_ref[...] += p

            @pl.when(kk == nk - 1)
            def _():
                finish(acc_ref[...])

    res = _pcall(
        body, name=name, grid=(m // tm, n // tn, nk),
        out_shape=out_shapes, in_specs=in_specs, out_specs=out_specs,
        scratch_shapes=[pltpu.VMEM((tm, tn), F32)] if nk > 1 else [],
        compiler_params=_params('parallel', 'parallel', 'arbitrary'),
    )(*operands)
    return res


def _rowwise(name, fn, row_ins, vec_ins, outs, reds, tm):
    rows = row_ins[0].shape[0]
    assert rows % tm == 0
    n_in = len(row_ins) + len(vec_ins)
    n_out = len(outs)

    def body(*refs):
        vals = [r[...] for r in refs[:n_in]]
        res = fn(*vals)
        for o_ref, val in zip(refs[n_in:n_in + n_out], res[:n_out]):
            o_ref[...] = val.astype(o_ref.dtype)
        first = pl.program_id(0) == 0
        for r_ref, val in zip(refs[n_in + n_out:], res[n_out:]):
            @pl.when(first)
            def _(r_ref=r_ref, val=val):
                r_ref[...] = val

            @pl.when(jnp.logical_not(first))
            def _(r_ref=r_ref, val=val):
                r_ref[...] += val

    in_specs = [pl.BlockSpec((tm, a.shape[1]), lambda i: (i, 0)) for a in row_ins]
    in_specs += [pl.BlockSpec((1, v.shape[1]), lambda i: (0, 0)) for v in vec_ins]
    out_shapes = [jax.ShapeDtypeStruct((rows, w), dt) for w, dt in outs]
    out_shapes += [jax.ShapeDtypeStruct((1, w), F32) for w in reds]
    out_specs = [pl.BlockSpec((tm, w), lambda i: (i, 0)) for w, _ in outs]
    out_specs += [pl.BlockSpec((1, w), lambda i: (0, 0)) for w in reds]
    return _pcall(
        body, name=name, grid=(rows // tm,),
        out_shape=out_shapes, in_specs=in_specs, out_specs=out_specs,
        compiler_params=_params('arbitrary'),
    )(*row_ins, *vec_ins)


def _norm_fwd(h, g, name):
    def fn(hv, gv):
        rstd = lax.rsqrt(jnp.mean(hv * hv, axis=-1, keepdims=True) + EPS)
        return (hv * rstd * gv,)
    return _rowwise(name, fn, [h], [g], [(h.shape[1], BF16)], [], 256)[0]


def _norm_bwd(h, dhn, dres, g, name):
    def fn(hv, dv, rv, gv):
        rstd = lax.rsqrt(jnp.mean(hv * hv, axis=-1, keepdims=True) + EPS)
        xhat = hv * rstd
        dxhat = dv * gv
        dh = rv + rstd * (dxhat - xhat * jnp.mean(dxhat * xhat, axis=-1, keepdims=True))
        return dh, jnp.sum(dv * xhat, axis=0, keepdims=True)
    w = h.shape[1]
    return _rowwise(name, fn, [h, dhn, dres], [g], [(w, F32)], [w], 256)


def _loss_head(h, target, g, name):
    w = h.shape[1]

    def fn(hv, tv, gv):
        rstd = lax.rsqrt(jnp.mean(hv * hv, axis=-1, keepdims=True) + EPS)
        xhat = hv * rstd
        diff = xhat * gv - tv
        dy = diff * (1.0 / w)
        dxhat = dy * gv
        dh = rstd * (dxhat - xhat * jnp.mean(dxhat * xhat, axis=-1, keepdims=True))
        return (dh, jnp.sum(0.5 * dy * diff, axis=0, keepdims=True),
                jnp.sum(dy * xhat, axis=0, keepdims=True))
    return _rowwise(name, fn, [h, target], [g], [(w, F32)], [w, w], 256)


def _split3(x):
    hi = x.astype(BF16)
    r1 = x - hi.astype(F32)
    mid = r1.astype(BF16)
    lo = (r1 - mid.astype(F32)).astype(BF16)
    return hi, mid, lo


def _tri_dot(tri, x):
    hi, mid, lo = _split3(x)
    d = lambda p: jnp.dot(tri, p, preferred_element_type=F32)
    return d(hi) + d(mid) + d(lo)


def _log_sigmoid(x):
    return jnp.minimum(x, 0.0) - jnp.log(1.0 + jnp.exp(-jnp.abs(x)))


def _gla_dims(proj_w, kw, vw, dk, dv):
    assert kw % dk == 0 and (2 * kw) % dv == 0 and (2 * kw + vw) % dv == 0 and (2 * kw + 2 * vw) % LANES == 0
    return dict(q0=0, k0=kw // dk, v0=2 * kw // dv, r0=(2 * kw + vw) // dv, g0=(2 * kw + 2 * vw) // LANES)


def _gla_gates(gl, wgu, bias):
    pre = jnp.dot(gl.astype(BF16), wgu, preferred_element_type=F32) + bias
    la = _log_sigmoid(pre) * (1.0 / GLA_GATE_TEMP)
    r_i = lax.broadcasted_iota(jnp.int32, (CHUNK, CHUNK), 0)
    c_i = lax.broadcasted_iota(jnp.int32, (CHUNK, CHUNK), 1)
    cum = _tri_dot((c_i <= r_i).astype(BF16), la)
    total = cum[CHUNK - 1:CHUNK, :]
    return pre, cum, total


def _gla_scan_fwd(proj, wgu_pad, b_gate, o_norm, heads, kw, vw, tb, name):
    seq, pw = proj.shape
    dk, dv = kw // heads, vw // heads
    cb = tb // CHUNK
    nt = seq // tb
    o = _gla_dims(pw, kw, vw, dk, dv)
    scale = dk ** -0.5

    def body(q_ref, k_ref, v_ref, r_ref, gl_ref, wgu_ref, b_ref, on_ref, out_ref, st_ref, s_scr):
        @pl.when(pl.program_id(1) == 0)
        def _():
            s_scr[...] = jnp.zeros_like(s_scr)

        wgu = wgu_ref[...].astype(BF16)
        bias = b_ref[...]
        onorm = on_ref[...]
        st = s_scr[...]
        for ci in range(cb):
            rows = pl.ds(ci * CHUNK, CHUNK)
            _, cum, total = _gla_gates(gl_ref[rows, :], wgu, bias)
            kdec = k_ref[rows, :] * jnp.exp(total - cum)
            st = st * jnp.exp(total) + lax.dot_general(
                v_ref[rows, :].astype(BF16), kdec.astype(BF16), (((0,), (0,)), ((), ())),
                preferred_element_type=F32)
            st_ref[ci] = st
            qs = (q_ref[rows, :] * scale).astype(BF16)
            ov = lax.dot_general(qs, st.astype(BF16), (((1,), (1,)), ((), ())), preferred_element_type=F32)
            rstd = lax.rsqrt(jnp.mean(ov * ov, axis=-1, keepdims=True) + EPS)
            rv = r_ref[rows, :]
            out_ref[rows, :] = (ov * rstd * onorm * (rv * jax.nn.sigmoid(rv))).astype(out_ref.dtype)
        s_scr[...] = st

    in_specs = [
        pl.BlockSpec((tb, dk), lambda h, t: (t, o['q0'] + h)),
        pl.BlockSpec((tb, dk), lambda h, t: (t, o['k0'] + h)),
        pl.BlockSpec((tb, dv), lambda h, t: (t, o['v0'] + h)),
        pl.BlockSpec((tb, dv), lambda h, t: (t, o['r0'] + h)),
        pl.BlockSpec((tb, LANES), lambda h, t: (t, o['g0'])),
        pl.BlockSpec((LANES, dk), lambda h, t: (0, h)),
        pl.BlockSpec((1, dk), lambda h, t: (0, h)),
        pl.BlockSpec((1, dv), lambda h, t: (0, 0)),
    ]
    return _pcall(
        body, name=name, grid=(heads, nt),
        out_shape=[jax.ShapeDtypeStruct((seq, vw), BF16),
                   jax.ShapeDtypeStruct((heads, seq // CHUNK, dv, dk), F32)],
        in_specs=in_specs,
        out_specs=[pl.BlockSpec((tb, dv), lambda h, t: (t, h)),
                   pl.BlockSpec((None, cb, dv, dk), lambda h, t: (h, t, 0, 0))],
        scratch_shapes=[pltpu.VMEM((dv, dk), F32)],
        compiler_params=_params('parallel', 'arbitrary'),
    )(proj, proj, proj, proj, proj, wgu_pad, b_gate, o_norm)


def _gla_scan_bwd(proj, wgu_pad, b_gate, o_norm, states, dgated, heads, kw, vw, tb, name):
    seq, pw = proj.shape
    dk, dv = kw // heads, vw // heads
    cb = tb // CHUNK
    nt = seq // tb
    o = _gla_dims(pw, kw, vw, dk, dv)
    scale = dk ** -0.5

    def body(q_ref, k_ref, v_ref, r_ref, gl_ref, wgu_ref, b_ref, on_ref, st_ref, stp_ref, dg_ref,
             dq_ref, dk_ref, dv_ref, dr_ref, dpre_ref, db_ref, don_ref, ds_scr):
        hh = pl.program_id(0)
        t = pl.program_id(1)

        @pl.when(t == 0)
        def _():
            ds_scr[...] = jnp.zeros_like(ds_scr)
            db_ref[...] = jnp.zeros_like(db_ref)

        @pl.when(jnp.logical_and(hh == 0, t == 0))
        def _():
            don_ref[...] = jnp.zeros_like(don_ref)

        wgu = wgu_ref[...].astype(BF16)
        bias = b_ref[...]
        onorm = on_ref[...]
        has_prev = (t < nt - 1).astype(F32)
        r_i = lax.broadcasted_iota(jnp.int32, (CHUNK, CHUNK), 0)
        c_i = lax.broadcasted_iota(jnp.int32, (CHUNK, CHUNK), 1)
        strict = (c_i < r_i).astype(BF16)
        carry = ds_scr[...]
        db_acc = jnp.zeros((1, dk), F32)
        don_acc = jnp.zeros((1, dv), F32)
        for ci in reversed(range(cb)):
            rows = pl.ds(ci * CHUNK, CHUNK)
            pre, cum, total = _gla_gates(gl_ref[rows, :], wgu, bias)
            edec = jnp.exp(total - cum)
            decay = jnp.exp(total)
            kdec = k_ref[rows, :] * edec
            st = st_ref[ci]
            st_prev = st_ref[ci - 1] if ci > 0 else stp_ref[0] * has_prev
            stb = st.astype(BF16)
            qs = (q_ref[rows, :] * scale).astype(BF16)
            vb = v_ref[rows, :].astype(BF16)
            ov = lax.dot_general(qs, stb, (((1,), (1,)), ((), ())), preferred_element_type=F32)
            rstd = lax.rsqrt(jnp.mean(ov * ov, axis=-1, keepdims=True) + EPS)
            ohat = ov * rstd
            rv = r_ref[rows, :]
            sr = jax.nn.sigmoid(rv)
            dgv = dg_ref[rows, :]
            dy = dgv * (rv * sr)
            dr_ref[rows, :] = (dgv * (ohat * onorm) * (sr * (1.0 + rv * (1.0 - sr)))).astype(dr_ref.dtype)
            don_acc = don_acc + jnp.sum(dy * ohat, axis=0, keepdims=True)
            dohat = dy * onorm
            do = (rstd * (dohat - ohat * jnp.mean(dohat * ohat, axis=-1, keepdims=True))).astype(BF16)
            dq_ref[rows, :] = (jnp.dot(do, stb, preferred_element_type=F32) * scale).astype(dq_ref.dtype)
            dst = carry + lax.dot_general(do, qs, (((0,), (0,)), ((), ())), preferred_element_type=F32)
            dstb = dst.astype(BF16)
            dkdec = jnp.dot(vb, dstb, preferred_element_type=F32)
            dv_ref[rows, :] = lax.dot_general(kdec.astype(BF16), dstb, (((1,), (1,)), ((), ())),
                                              preferred_element_type=F32).astype(dv_ref.dtype)
            ddecay = jnp.sum(dst * st_prev, axis=0, keepdims=True)
            dk_ref[rows, :] = (dkdec * edec).astype(dk_ref.dtype)
            da = ddecay * decay + _tri_dot(strict, dkdec * kdec)
            dpre = da * (1.0 / GLA_GATE_TEMP) * (1.0 - jax.nn.sigmoid(pre))
            dpre_ref[rows, :] = dpre.astype(dpre_ref.dtype)
            db_acc = db_acc + jnp.sum(dpre, axis=0, keepdims=True)
            carry = dst * decay
        ds_scr[...] = carry
        db_ref[...] += db_acc
        don_ref[...] += don_acc

    rt = lambda t: nt - 1 - t
    in_specs = [
        pl.BlockSpec((tb, dk), lambda h, t: (rt(t), o['q0'] + h)),
        pl.BlockSpec((tb, dk), lambda h, t: (rt(t), o['k0'] + h)),
        pl.BlockSpec((tb, dv), lambda h, t: (rt(t), o['v0'] + h)),
        pl.BlockSpec((tb, dv), lambda h, t: (rt(t), o['r0'] + h)),
        pl.BlockSpec((tb, LANES), lambda h, t: (rt(t), o['g0'])),
        pl.BlockSpec((LANES, dk), lambda h, t: (0, h)),
        pl.BlockSpec((1, dk), lambda h, t: (0, h)),
        pl.BlockSpec((1, dv), lambda h, t: (0, 0)),
        pl.BlockSpec((None, cb, dv, dk), lambda h, t: (h, rt(t), 0, 0)),
        pl.BlockSpec((None, 1, dv, dk), lambda h, t: (h, jnp.maximum(rt(t) * cb - 1, 0), 0, 0)),
        pl.BlockSpec((tb, dv), lambda h, t: (rt(t), h)),
    ]
    out_shape = [jax.ShapeDtypeStruct((seq, kw), BF16), jax.ShapeDtypeStruct((seq, kw), BF16),
                 jax.ShapeDtypeStruct((seq, vw), BF16), jax.ShapeDtypeStruct((seq, vw), BF16),
                 jax.ShapeDtypeStruct((seq, kw), BF16),
                 jax.ShapeDtypeStruct((1, kw), F32), jax.ShapeDtypeStruct((1, dv), F32)]
    out_specs = [pl.BlockSpec((tb, dk), lambda h, t: (rt(t), h)),
                 pl.BlockSpec((tb, dk), lambda h, t: (rt(t), h)),
                 pl.BlockSpec((tb, dv), lambda h, t: (rt(t), h)),
                 pl.BlockSpec((tb, dv), lambda h, t: (rt(t), h)),
                 pl.BlockSpec((tb, dk), lambda h, t: (rt(t), h)),
                 pl.BlockSpec((1, dk), lambda h, t: (0, h)),
                 pl.BlockSpec((1, dv), lambda h, t: (0, 0))]
    return _pcall(
        body, name=name, grid=(heads, nt),
        out_shape=out_shape, in_specs=in_specs, out_specs=out_specs,
        scratch_shapes=[pltpu.VMEM((dv, dk), F32)],
        compiler_params=_params('arbitrary', 'arbitrary'),
    )(proj, proj, proj, proj, proj, wgu_pad, b_gate, o_norm, states, states, dgated)


def _cmul(ar, ai, br, bi):
    return ar * br - ai * bi, ar * bi + ai * br


def _gelu(y):
    c = math.sqrt(2.0 / math.pi)
    return 0.5 * y * (1.0 + jnp.tanh(c * (y + 0.044715 * y * y * y)))


def _gelu_grad(y):
    c = math.sqrt(2.0 / math.pi)
    th = jnp.tanh(c * (y + 0.044715 * y * y * y))
    return 0.5 * (1.0 + th) + 0.5 * y * (1.0 - th * th) * (c * (1.0 + 3.0 * 0.044715 * y * y))


def _power_pow2(ar, ai, n):
    assert n & (n - 1) == 0
    for _ in range(n.bit_length() - 1):
        ar, ai = _cmul(ar, ai, ar, ai)
    return ar, ai


def _s5_fwd(u, bre, bim, cre, cim, are, aim, dskip, name):
    seq, width = u.shape
    nb, ub, sb = bre.shape
    ls = seq // S5_SEGMENTS
    seg = S5_SEGMENTS

    def body(u_ref, bre_ref, bim_ref, cre_ref, cim_ref, are_ref, aim_ref, d_ref, y_ref, z_ref, xr_ref, xi_ref):
        uv = u_ref[...]
        ub16 = uv.astype(BF16)
        xr_ref[...] = jnp.dot(ub16, bre_ref[...].astype(BF16), preferred_element_type=F32)
        xi_ref[...] = jnp.dot(ub16, bim_ref[...].astype(BF16), preferred_element_type=F32)
        ar = jnp.broadcast_to(are_ref[...], (seg, sb))
        ai = jnp.broadcast_to(aim_ref[...], (seg, sb))

        def step(i, c):
            rows = pl.ds(pl.multiple_of(i * seg, seg), seg)
            pr, pi = _cmul(ar, ai, c[0], c[1])
            nr = pr + xr_ref[rows, :]
            ni = pi + xi_ref[rows, :]
            xr_ref[rows, :] = nr
            xi_ref[rows, :] = ni
            return nr, ni

        zero = jnp.zeros((seg, sb), F32)
        er, ei = lax.fori_loop(0, ls, step, (zero, zero), unroll=8)
        pr, pi = _power_pow2(ar, ai, ls)
        row = lax.broadcasted_iota(jnp.int32, (seg, sb), 0)
        sr, si = zero, zero
        for _ in range(seg - 1):
            tr, ti = _cmul(pr, pi, sr, si)
            sr = jnp.where(row == 0, 0.0, pltpu.roll(tr + er, 1, 0))
            si = jnp.where(row == 0, 0.0, pltpu.roll(ti + ei, 1, 0))

        def fix(i, c):
            rows = pl.ds(pl.multiple_of(i * seg, seg), seg)
            fr, fi = _cmul(c[0], c[1], sr, si)
            xr_ref[rows, :] += fr
            xi_ref[rows, :] += fi
            return _cmul(c[0], c[1], ar, ai)

        lax.fori_loop(0, ls, fix, (ar, ai), unroll=8)
        y = (jnp.dot(xr_ref[...].astype(BF16), cre_ref[...].astype(BF16), preferred_element_type=F32)
             - jnp.dot(xi_ref[...].astype(BF16), cim_ref[...].astype(BF16), preferred_element_type=F32)
             + d_ref[...] * uv)
        y_ref[...] = y
        z_ref[...] = _gelu(y).astype(z_ref.dtype)

    mat = lambda r, c: pl.BlockSpec((None, r, c), lambda b: (b, 0, 0))
    return _pcall(
        body, name=name, grid=(nb,),
        out_shape=[jax.ShapeDtypeStruct((seq, width), F32), jax.ShapeDtypeStruct((seq, width), BF16),
                   jax.ShapeDtypeStruct((seq, nb * sb), F32), jax.ShapeDtypeStruct((seq, nb * sb), F32)],
        in_specs=[pl.BlockSpec((seq, ub), lambda b: (0, b)), mat(ub, sb), mat(ub, sb), mat(sb, ub), mat(sb, ub),
                  mat(1, sb), mat(1, sb), pl.BlockSpec((1, ub), lambda b: (0, b))],
        out_specs=[pl.BlockSpec((seq, ub), lambda b: (0, b)), pl.BlockSpec((seq, ub), lambda b: (0, b)),
                   pl.BlockSpec((seq, sb), lambda b: (0, b)), pl.BlockSpec((seq, sb), lambda b: (0, b))],
        compiler_params=_params('parallel'),
    )(u, bre, bim, cre, cim, are, aim, dskip)


def _s5_bwd(dz, y, u, xr, xi, bre, bim, cre, cim, are, aim, dskip, name):
    seq, width = u.shape
    nb, ub, sb = bre.shape
    ls = seq // S5_SEGMENTS
    seg = S5_SEGMENTS

    def body(dz_ref, y_ref, u_ref, xr_ref, xi_ref, bre_ref, bim_ref, cre_ref, cim_ref, are_ref, aim_ref, d_ref,
             du_ref, dcr_ref, dci_ref, dbr_ref, dbi_ref, dar_ref, dai_ref, dd_ref, lr_ref, li_ref):
        uv = u_ref[...]
        dy = dz_ref[...] * _gelu_grad(y_ref[...])
        dd_ref[...] = jnp.sum(dy * uv, axis=0, keepdims=True)
        dyb = dy.astype(BF16)
        nt = (((1,), (1,)), ((), ()))
        tn = (((0,), (0,)), ((), ()))
        lr_ref[...] = lax.dot_general(dyb, cre_ref[...].astype(BF16), nt, preferred_element_type=F32)
        li_ref[...] = -lax.dot_general(dyb, cim_ref[...].astype(BF16), nt, preferred_element_type=F32)
        dcr_ref[...] = lax.dot_general(dyb, xr_ref[...].astype(BF16), tn, preferred_element_type=F32)
        dci_ref[...] = -lax.dot_general(dyb, xi_ref[...].astype(BF16), tn, preferred_element_type=F32)
        ar = jnp.broadcast_to(are_ref[...], (seg, sb))
        ai = jnp.broadcast_to(aim_ref[...], (seg, sb))
        nai = -ai

        def step(ii, c):
            rows = pl.ds(pl.multiple_of((ls - 1 - ii) * seg, seg), seg)
            pr, pi = _cmul(ar, nai, c[0], c[1])
            nr = pr + lr_ref[rows, :]
            ni = pi + li_ref[rows, :]
            lr_ref[rows, :] = nr
            li_ref[rows, :] = ni
            return nr, ni

        zero = jnp.zeros((seg, sb), F32)
        er, ei = lax.fori_loop(0, ls, step, (zero, zero), unroll=8)
        pr, pi = _power_pow2(ar, nai, ls)
        row = lax.broadcasted_iota(jnp.int32, (seg, sb), 0)
        rr, ri = zero, zero
        for _ in range(seg - 1):
            tr, ti = _cmul(pr, pi, rr, ri)
            rr = jnp.where(row == seg - 1, 0.0, pltpu.roll(tr + er, seg - 1, 0))
            ri = jnp.where(row == seg - 1, 0.0, pltpu.roll(ti + ei, seg - 1, 0))

        def corrected(rows, qr, qi):
            fr, fi = _cmul(qr, qi, rr, ri)
            nr = lr_ref[rows, :] + fr
            ni = li_ref[rows, :] + fi
            lr_ref[rows, :] = nr
            li_ref[rows, :] = ni
            return nr, ni

        def grad_a(nr, ni, xpr, xpi, accr, acci):
            return accr + nr * xpr + ni * xpi, acci + ni * xpr - nr * xpi

        def fix(ii, c):
            qr, qi, accr, acci = c
            i = ls - 1 - ii
            rows = pl.ds(pl.multiple_of(i * seg, seg), seg)
            prev = pl.ds(pl.multiple_of((i - 1) * seg, seg), seg)
            nr, ni = corrected(rows, qr, qi)
            accr, acci = grad_a(nr, ni, xr_ref[prev, :], xi_ref[prev, :], accr, acci)
            qr, qi = _cmul(qr, qi, ar, nai)
            return qr, qi, accr, acci

        qr, qi, accr, acci = lax.fori_loop(0, ls - 1, fix, (ar, nai, zero, zero), unroll=8)
        nr, ni = corrected(pl.ds(0, seg), qr, qi)
        last = pl.ds((ls - 1) * seg, seg)
        xpr = jnp.where(row == 0, 0.0, pltpu.roll(xr_ref[last, :], 1, 0))
        xpi = jnp.where(row == 0, 0.0, pltpu.roll(xi_ref[last, :], 1, 0))
        accr, acci = grad_a(nr, ni, xpr, xpi, accr, acci)
        dar_ref[...] = jnp.sum(accr, axis=0, keepdims=True)
        dai_ref[...] = jnp.sum(acci, axis=0, keepdims=True)
        lrb = lr_ref[...].astype(BF16)
        lib = li_ref[...].astype(BF16)
        ub16 = uv.astype(BF16)
        dbr_ref[...] = lax.dot_general(ub16, lrb, tn, preferred_element_type=F32)
        dbi_ref[...] = lax.dot_general(ub16, lib, tn, preferred_element_type=F32)
        du_ref[...] = (d_ref[...] * dy
                       + lax.dot_general(lrb, bre_ref[...].astype(BF16), nt, preferred_element_type=F32)
                       + lax.dot_general(lib, bim_ref[...].astype(BF16), nt, preferred_element_type=F32))

    mat = lambda r, c: pl.BlockSpec((None, r, c), lambda b: (b, 0, 0))
    col = lambda w: pl.BlockSpec((seq, w), lambda b: (0, b))
    return _pcall(
        body, name=name, grid=(nb,),
        out_shape=[jax.ShapeDtypeStruct((seq, width), F32)]
        + [jax.ShapeDtypeStruct((nb, ub, sb), F32)] * 4
        + [jax.ShapeDtypeStruct((nb, 1, sb), F32)] * 2
        + [jax.ShapeDtypeStruct((1, width), F32)],
        in_specs=[col(ub), col(ub), col(ub), col(sb), col(sb), mat(ub, sb), mat(ub, sb), mat(sb, ub), mat(sb, ub),
                  mat(1, sb), mat(1, sb), pl.BlockSpec((1, ub), lambda b: (0, b))],
        out_specs=[col(ub), mat(ub, sb), mat(ub, sb), mat(ub, sb), mat(ub, sb), mat(1, sb), mat(1, sb),
                   pl.BlockSpec((1, ub), lambda b: (0, b))],
        scratch_shapes=[pltpu.VMEM((seq, sb), F32), pltpu.VMEM((seq, sb), F32)],
        compiler_params=_params('parallel'),
    )(dz, y, u, xr, xi, bre, bim, cre, cim, are, aim, dskip)


def _s5_discretise(lam_re, lam_im, log_dt, b_re, b_im):
    lr = jnp.minimum(lam_re, S5_EIG_CLIP)
    li = lam_im
    dt = jnp.exp(log_dt)[:, None]
    mag = jnp.exp(lr * dt)
    ang = li * dt
    ab_re = mag * jnp.cos(ang)
    ab_im = mag * jnp.sin(ang)
    den = lr * lr + li * li
    nr = ab_re - 1.0
    f_re = (nr * lr + ab_im * li) / den
    f_im = (ab_im * lr - nr * li) / den
    bb_re = f_re[..., None] * b_re - f_im[..., None] * b_im
    bb_im = f_re[..., None] * b_im + f_im[..., None] * b_re
    return ab_re, ab_im, bb_re, bb_im


def _to_blocks(m):
    g, a, b = m.shape
    gb = S5_GROUPS_PER_BLOCK
    eye = jnp.eye(gb, dtype=m.dtype)
    return jnp.einsum('bgac,gh->bgahc', m.reshape(g // gb, gb, a, b), eye).reshape(g // gb, gb * a, gb * b)


def _from_blocks(m, a, b):
    nb = m.shape[0]
    gb = S5_GROUPS_PER_BLOCK
    eye = jnp.eye(gb, dtype=m.dtype)
    return jnp.einsum('bgahc,gh->bgac', m.reshape(nb, gb, a, gb, b), eye).reshape(nb * gb, a, b)


def _glu_fwd(o, h, name):
    half = o.shape[1] // 2

    def fn(ov, hv):
        return (hv + ov[:, :half] * jax.nn.sigmoid(ov[:, half:]),)
    return _rowwise(name, fn, [o, h], [], [(half, F32)], [], 256)[0]


def _glu_bwd(o, dout, name):
    half = o.shape[1] // 2

    def fn(ov, dv):
        val, gate = ov[:, :half], ov[:, half:]
        sg = jax.nn.sigmoid(gate)
        return (jnp.concatenate([dv * sg, dv * val * sg * (1.0 - sg)], axis=1),)
    return _rowwise(name, fn, [o, dout], [], [(2 * half, BF16)], [], 256)[0]


def _adam_math(w, g, m, v):
    m = ADAM_B1 * m + (1.0 - ADAM_B1) * g
    v = ADAM_B2 * v + (1.0 - ADAM_B2) * (g * g)
    m_hat = m / (1.0 - ADAM_B1 ** ADAM_STEP)
    v_hat = v / (1.0 - ADAM_B2 ** ADAM_STEP)
    delta = -ADAM_LR * (m_hat / (jnp.sqrt(v_hat) + ADAM_EPS) + ADAM_WD * w)
    return delta, m, v


def _adamw(w, m, v, grads, name):
    nl, rows, cols = w.shape
    tm = _tile(rows, max(8, (1 << 18) // cols // 8 * 8), unit=8)
    nbk = rows // tm

    def body(*refs):
        w_ref, m_ref, v_ref = refs[:3]
        g_refs = refs[3:3 + nl]
        go_ref, d_ref, mo_ref, vo_ref = refs[3 + nl:]
        layer = pl.program_id(0)
        g = g_refs[0][...]
        for l in range(1, nl):
            g = jnp.where(layer == l, g_refs[l][...], g)
        delta, mn, vn = _adam_math(w_ref[...], g, m_ref[...], v_ref[...])
        go_ref[...] = g
        d_ref[...] = delta
        mo_ref[...] = mn
        vo_ref[...] = vn

    stacked = pl.BlockSpec((None, tm, cols), lambda l, i: (l, i, 0))

    def g_spec(layer):
        return pl.BlockSpec((tm, cols), lambda l, i: (jnp.where(l == layer, i, jnp.where(l < layer, 0, nbk - 1)), 0))

    return _pcall(
        body, name=name, grid=(nl, nbk),
        out_shape=[jax.ShapeDtypeStruct(w.shape, F32)] * 4,
        in_specs=[stacked] * 3 + [g_spec(l) for l in range(nl)],
        out_specs=[stacked] * 4,
        compiler_params=_params('arbitrary', 'arbitrary'),
    )(w, m, v, *grads)


def _pack(arrs, rows_mult=512):
    flat = jnp.concatenate([a.reshape(-1) for a in arrs])
    total = flat.shape[0]
    rows = -(-total // LANES)
    rows = -(-rows // rows_mult) * rows_mult
    flat = jnp.pad(flat, (0, rows * LANES - total))
    return flat.reshape(rows, LANES)


def _unpack(packed, shapes):
    flat = packed.reshape(-1)
    out, off = [], 0
    for s in shapes:
        size = math.prod(s)
        out.append(flat[off:off + size].reshape(s))
        off += size
    return out


def _permute(a):
    seq, w = a.shape
    return a.reshape(S5_SEGMENTS, seq // S5_SEGMENTS, w).transpose(1, 0, 2).reshape(seq, w)


def _unpermute(a):
    seq, w = a.shape
    return a.reshape(seq // S5_SEGMENTS, S5_SEGMENTS, w).transpose(1, 0, 2).reshape(seq, w)


def kernel(x, gla_norm, gla_w_in, gla_w_gate_up, gla_b_gate, gla_o_norm, gla_w_out, s5_norm, s5_w_in, s5_lam_re, s5_lam_im, s5_log_dt, s5_b_re, s5_b_im, s5_c_re, s5_c_im, s5_d, s5_w_out, mlp_norm, mlp_w_up, mlp_w_down, final_norm, loss_target, m_gla_norm, m_gla_w_in, m_gla_w_gate_up, m_gla_b_gate, m_gla_o_norm, m_gla_w_out, m_s5_norm, m_s5_w_in, m_s5_lam_re, m_s5_lam_im, m_s5_log_dt, m_s5_b_re, m_s5_b_im, m_s5_c_re, m_s5_c_im, m_s5_d, m_s5_w_out, m_mlp_norm, m_mlp_w_up, m_mlp_w_down, m_final_norm, v_gla_norm, v_gla_w_in, v_gla_w_gate_up, v_gla_b_gate, v_gla_o_norm, v_gla_w_out, v_s5_norm, v_s5_w_in, v_s5_lam_re, v_s5_lam_im, v_s5_log_dt, v_s5_b_re, v_s5_b_im, v_s5_c_re, v_s5_c_im, v_s5_d, v_s5_w_out, v_mlp_norm, v_mlp_w_up, v_mlp_w_down, v_final_norm):
    weights = dict(gla_norm=gla_norm, gla_w_in=gla_w_in, gla_w_gate_up=gla_w_gate_up, gla_b_gate=gla_b_gate, gla_o_norm=gla_o_norm, gla_w_out=gla_w_out, s5_norm=s5_norm, s5_w_in=s5_w_in, s5_lam_re=s5_lam_re, s5_lam_im=s5_lam_im, s5_log_dt=s5_log_dt, s5_b_re=s5_b_re, s5_b_im=s5_b_im, s5_c_re=s5_c_re, s5_c_im=s5_c_im, s5_d=s5_d, s5_w_out=s5_w_out, mlp_norm=mlp_norm, mlp_w_up=mlp_w_up, mlp_w_down=mlp_w_down, final_norm=final_norm)
    mom1 = dict(gla_norm=m_gla_norm, gla_w_in=m_gla_w_in, gla_w_gate_up=m_gla_w_gate_up, gla_b_gate=m_gla_b_gate, gla_o_norm=m_gla_o_norm, gla_w_out=m_gla_w_out, s5_norm=m_s5_norm, s5_w_in=m_s5_w_in, s5_lam_re=m_s5_lam_re, s5_lam_im=m_s5_lam_im, s5_log_dt=m_s5_log_dt, s5_b_re=m_s5_b_re, s5_b_im=m_s5_b_im, s5_c_re=m_s5_c_re, s5_c_im=m_s5_c_im, s5_d=m_s5_d, s5_w_out=m_s5_w_out, mlp_norm=m_mlp_norm, mlp_w_up=m_mlp_w_up, mlp_w_down=m_mlp_w_down, final_norm=m_final_norm)
    mom2 = dict(gla_norm=v_gla_norm, gla_w_in=v_gla_w_in, gla_w_gate_up=v_gla_w_gate_up, gla_b_gate=v_gla_b_gate, gla_o_norm=v_gla_o_norm, gla_w_out=v_gla_w_out, s5_norm=v_s5_norm, s5_w_in=v_s5_w_in, s5_lam_re=v_s5_lam_re, s5_lam_im=v_s5_lam_im, s5_log_dt=v_s5_log_dt, s5_b_re=v_s5_b_re, s5_b_im=v_s5_b_im, s5_c_re=v_s5_c_re, s5_c_im=v_s5_c_im, s5_d=v_s5_d, s5_w_out=v_s5_w_out, mlp_norm=v_mlp_norm, mlp_w_up=v_mlp_w_up, mlp_w_down=v_mlp_w_down, final_norm=v_final_norm)
    names = list(weights)
    big = ['gla_w_in', 'gla_w_out', 's5_w_in', 's5_w_out', 'mlp_w_up', 'mlp_w_down']
    small = [n for n in names if n not in big]

    chip = 2 * lax.axis_index('x') + lax.axis_index('y')
    h0 = x[0]
    target = loss_target[0]
    seq, dm = h0.shape
    depth = mlp_norm.shape[0]
    n_gla = gla_norm.shape[0]
    n_s5 = s5_lam_re.shape[0]
    rank = gla_w_gate_up.shape[1]
    kw = gla_b_gate.shape[1]
    dv = gla_o_norm.shape[1]
    in_w = 4 * gla_w_in.shape[2]
    vw = (in_w - rank - 2 * kw) // 2
    heads = vw // dv
    dk = kw // heads
    pw = -(-in_w // LANES) * LANES
    s5w = s5_w_in.shape[2]
    n_grp, n_state, grp = s5_b_re.shape[1:]
    hid = 4 * mlp_w_up.shape[2]
    tb = min(seq, 8 * CHUNK)
    tm = _tile(seq, 1024)

    g_w_in = _gather_weight(gla_w_in, 'ag_gla_w_in')
    g_gla_out = _gather_weight(gla_w_out, 'ag_gla_w_out')
    g_s5_in = _gather_weight(s5_w_in, 'ag_s5_w_in')
    g_s5_out = _gather_weight(s5_w_out, 'ag_s5_w_out')
    g_up = _gather_weight(mlp_w_up, 'ag_mlp_w_up')
    g_down = _gather_weight(mlp_w_down, 'ag_mlp_w_down')
    sharded_small = [gla_w_gate_up, s5_norm, s5_d]
    gathered_small = _exchange(_pack(sharded_small), 'xy', 'bcast', 'ag_small')
    parts = [_unpack(gathered_small[k], [a.shape for a in sharded_small]) for k in range(4)]
    wgu_full = jnp.concatenate([p[0] for p in parts], axis=2)
    s5_norm_full = jnp.concatenate([p[1] for p in parts], axis=1)
    s5_d_full = jnp.concatenate([p[2] for p in parts], axis=1)

    def gla_w_in_padded(j):
        nh = g_w_in.shape[2]
        wj = g_w_in[j // nh, :, j % nh]
        wj = wj.transpose(1, 0, 2).reshape(dm, in_w)
        return jnp.pad(wj, ((0, 0), (0, pw - in_w)))

    grads = {n: [None] * weights[n].shape[0] for n in names if n != 'final_norm'}

    saved = []
    h = h0
    for i in range(depth):
        j = i // 2
        rec = {}
        if i % 2 == 0:
            rec['h_in'] = h
            hn = _norm_fwd(h, gla_norm[j:j + 1], 'gla_norm_fwd')
            w_in_pad = gla_w_in_padded(j)
            proj = _mm('gla_proj', 'nn', _plain(hn), _plain(w_in_pad), (seq, pw, dm),
                       [((seq, pw), F32, _plain_shape(None))], (tm, _tile(pw, 1024), dm))[0]
            wgu_pad = jnp.pad(wgu_full[j], ((0, LANES - rank), (0, 0)))
            gated, states = _gla_scan_fwd(proj, wgu_pad, gla_b_gate[j:j + 1], gla_o_norm[j:j + 1],
                                          heads, kw, vw, tb, 'gla_scan_fwd')
            h = _mm('gla_out', 'nn', _plain(gated), _w_rows(g_gla_out, j), (seq, dm, vw),
                    [((seq, dm), F32, _plain_shape(None))],
                    (tm, _tile(dm, 1024), _tile(g_gla_out.shape[3], 1024)),
                    epilogue=lambda acc, hv: (acc + hv,), extras=[_plain(h)])[0]
            rec.update(hn=hn, w_in_pad=w_in_pad, proj=proj, wgu_pad=wgu_pad, gated=gated, states=states)
        else:
            hp = _permute(h)
            rec['h_in'] = hp
            hn = _norm_fwd(hp, s5_norm_full[j:j + 1], 's5_norm_fwd')
            u = _mm('s5_in', 'nn', _plain(hn), _w_rows(g_s5_in, j), (seq, s5w, dm),
                    [((seq, s5w), F32, _plain_shape(None))],
                    (tm, _tile(s5w, 1024), _tile(g_s5_in.shape[3], 1024)))[0]
            disc, disc_vjp = jax.vjp(_s5_discretise, s5_lam_re[j], s5_lam_im[j], s5_log_dt[j], s5_b_re[j], s5_b_im[j])
            ab_re, ab_im, bb_re, bb_im = disc
            bre = _to_blocks(bb_re.transpose(0, 2, 1))
            bim = _to_blocks(bb_im.transpose(0, 2, 1))
            cre = _to_blocks(s5_c_re[j].transpose(0, 2, 1))
            cim = _to_blocks(s5_c_im[j].transpose(0, 2, 1))
            nb = n_grp // S5_GROUPS_PER_BLOCK
            are = ab_re.reshape(nb, 1, S5_GROUPS_PER_BLOCK * n_state)
            aim = ab_im.reshape(nb, 1, S5_GROUPS_PER_BLOCK * n_state)
            dskip = s5_d_full[j:j + 1]
            y, z, xr, xi = _s5_fwd(u, bre, bim, cre, cim, are, aim, dskip, 's5_scan_fwd')
            o = _mm('s5_out', 'nn', _plain(z), _w_cols(g_s5_out, j), (seq, 2 * dm, s5w),
                    [((seq, 2 * dm), F32, _plain_shape(None))],
                    (tm, _tile(g_s5_out.shape[4], 1024), _tile(s5w, 1024)))[0]
            h = _unpermute(_glu_fwd(o, hp, 's5_glu_fwd'))
            rec.update(hn=hn, u=u, y=y, z=z, xr=xr, xi=xi, o=o, mats=(bre, bim, cre, cim, are, aim, dskip),
                       disc_vjp=disc_vjp)
        rec['h_mid'] = h
        hn2 = _norm_fwd(h, mlp_norm[i:i + 1], 'mlp_norm_fwd')
        act, act2 = _mm('mlp_up', 'nn', _plain(hn2), _w_cols(g_up, i), (seq, hid, dm),
                        [((seq, hid), BF16, _plain_shape(None))] * 2,
                        (tm, _tile(g_up.shape[4], 1024), dm),
                        epilogue=lambda acc: (jnp.maximum(acc, 0.0), jnp.square(jnp.maximum(acc, 0.0))))
        h = _mm('mlp_down', 'nn', _plain(act2), _w_rows(g_down, i), (seq, dm, hid),
                [((seq, dm), F32, _plain_shape(None))],
                (tm, _tile(dm, 1024), _tile(g_down.shape[3], 2048)),
                epilogue=lambda acc, hv: (acc + hv,), extras=[_plain(h)])[0]
        rec.update(hn2=hn2, act=act, act2=act2)
        saved.append(rec)

    dh, loss_cols, d_final = _loss_head(h, target, final_norm.reshape(1, dm), 'loss_head')
    loss = lax.psum(jnp.sum(loss_cols), ('x', 'y', 'c'))
    grads['final_norm'] = [d_final.reshape(dm)]

    big_grads = {n: [None] * weights[n].shape[0] for n in big}
    for i in reversed(range(depth)):
        j = i // 2
        rec = saved[i]
        r_dn, c_dn = mlp_w_down.shape[1:]
        shape, spec = _dw_rows(r_dn, c_dn)
        dw = _mm('mlp_down_dw', 'tn', _plain(rec['act2']), _plain(dh), (hid, dm, seq),
                 [(shape, BF16, spec)], (_tile(r_dn // 2, 1024), _tile(c_dn, 1024), seq))[0]
        big_grads['mlp_w_down'][i] = _reduce_scatter(dw, 'rs_mlp_down')
        dpre = _mm('mlp_down_dx', 'nt', _plain(dh), _w_rows(g_down, i), (seq, hid, dm),
                   [((seq, hid), BF16, _plain_shape(None))],
                   (tm, _tile(g_down.shape[3], 1024), dm),
                   epilogue=lambda acc, av: (acc * (2.0 * av.astype(F32)),), extras=[_plain(rec['act'])])[0]
        r_up, c_up = mlp_w_up.shape[1:]
        shape, spec = _dw_cols(r_up, c_up)
        dw = _mm('mlp_up_dw', 'tn', _plain(rec['hn2']), _plain(dpre), (dm, hid, seq),
                 [(shape, BF16, spec)], (_tile(r_up // 2, 1024), _tile(c_up, 1024), seq))[0]
        big_grads['mlp_w_up'][i] = _reduce_scatter(dw, 'rs_mlp_up')
        dhn = _mm('mlp_up_dx', 'nt', _plain(dpre), _w_cols(g_up, i), (seq, dm, hid),
                  [((seq, dm), F32, _plain_shape(None))],
                  (tm, _tile(dm, 1024), _tile(g_up.shape[4], 2048)))[0]
        dh, dg = _norm_bwd(rec['h_mid'], dhn, dh, mlp_norm[i:i + 1], 'mlp_norm_bwd')
        grads['mlp_norm'][i] = dg[0]

        if i % 2 == 0:
            r_o, c_o = gla_w_out.shape[1:]
            shape, spec = _dw_rows(r_o, c_o)
            dw = _mm('gla_out_dw', 'tn', _plain(rec['gated']), _plain(dh), (vw, dm, seq),
                     [(shape, BF16, spec)], (_tile(r_o // 2, 1024), _tile(c_o, 1024), seq))[0]
            big_grads['gla_w_out'][j] = _reduce_scatter(dw, 'rs_gla_out')
            dgated = _mm('gla_out_dx', 'nt', _plain(dh), _w_rows(g_gla_out, j), (seq, vw, dm),
                         [((seq, vw), F32, _plain_shape(None))],
                         (tm, _tile(g_gla_out.shape[3], 1024), dm))[0]
            dq, dkk, dvv, dr, dpre_g, db, don = _gla_scan_bwd(
                rec['proj'], rec['wgu_pad'], gla_b_gate[j:j + 1], gla_o_norm[j:j + 1], rec['states'], dgated,
                heads, kw, vw, tb, 'gla_scan_bwd')
            grads['gla_b_gate'][j] = db[0]
            grads['gla_o_norm'][j] = don[0]
            dgl = _mm('gla_gate_dx', 'nt', _plain(dpre_g), _plain(rec['wgu_pad']), (seq, LANES, kw),
                      [((seq, LANES), BF16, _plain_shape(None))], (tm, LANES, kw))[0]
            g_low = rec['proj'][:, pw - LANES:]
            dwgu = _mm('gla_gate_dw', 'tn', _plain(g_low), _plain(dpre_g), (LANES, kw, seq),
                       [((LANES, kw), F32, _plain_shape(None))], (LANES, kw, seq))[0]
            grads['gla_w_gate_up'][j] = dwgu[:rank]
            dproj = jnp.concatenate([dq, dkk, dvv, dr, dgl], axis=1)
            dw_pad = _mm('gla_proj_dw', 'tn', _plain(rec['hn']), _plain(dproj), (dm, pw, seq),
                         [((dm, pw), BF16, _plain_shape(None))], (_tile(dm, 1024), _tile(pw, 1024), seq))[0]
            shard_w = in_w // 4
            dw = dw_pad[:, :in_w].reshape(2, dm // 2, 4, shard_w).transpose(0, 2, 1, 3)
            big_grads['gla_w_in'][j] = _reduce_scatter(dw, 'rs_gla_in')
            dhn = _mm('gla_proj_dx', 'nt', _plain(dproj), _plain(rec['w_in_pad']), (seq, dm, pw),
                      [((seq, dm), F32, _plain_shape(None))], (tm, _tile(dm, 1024), _tile(pw, 1024)))[0]
            dh, dg = _norm_bwd(rec['h_in'], dhn, dh, gla_norm[j:j + 1], 'gla_norm_bwd')
            grads['gla_norm'][j] = dg[0]
        else:
            dhp = _permute(dh)
            do = _glu_bwd(rec['o'], dhp, 's5_glu_bwd')
            r_o, c_o = s5_w_out.shape[1:]
            shape, spec = _dw_cols(r_o, c_o)
            dw = _mm('s5_out_dw', 'tn', _plain(rec['z']), _plain(do), (s5w, 2 * dm, seq),
                     [(shape, BF16, spec)], (_tile(r_o // 2, 1024), _tile(c_o, 1024), seq))[0]
            big_grads['s5_w_out'][j] = _reduce_scatter(dw, 'rs_s5_out')
            dz = _mm('s5_out_dx', 'nt', _plain(do), _w_cols(g_s5_out, j), (seq, s5w, 2 * dm),
                     [((seq, s5w), F32, _plain_shape(None))],
                     (tm, _tile(s5w, 1024), _tile(g_s5_out.shape[4], 1024)))[0]
            bre, bim, cre, cim, are, aim, dskip = rec['mats']
            du, dcr, dci, dbr, dbi, dar, dai, dd = _s5_bwd(dz, rec['y'], rec['u'], rec['xr'], rec['xi'],
                                                           bre, bim, cre, cim, are, aim, dskip, 's5_scan_bwd')
            grads['s5_c_re'][j] = _from_blocks(dcr, grp, n_state)
            grads['s5_c_im'][j] = _from_blocks(dci, grp, n_state)
            dbb_re = _from_blocks(dbr, grp, n_state).transpose(0, 2, 1)
            dbb_im = _from_blocks(dbi, grp, n_state).transpose(0, 2, 1)
            d_lr, d_li, d_dt, d_bre, d_bim = rec['disc_vjp'](
                (dar.reshape(n_grp, n_state), dai.reshape(n_grp, n_state), dbb_re, dbb_im))
            grads['s5_lam_re'][j] = d_lr
            grads['s5_lam_im'][j] = d_li
            grads['s5_log_dt'][j] = d_dt
            grads['s5_b_re'][j] = d_bre
            grads['s5_b_im'][j] = d_bim
            grads['s5_d'][j] = dd[0]
            r_i, c_i = s5_w_in.shape[1:]
            shape, spec = _dw_rows(r_i, c_i)
            dw = _mm('s5_in_dw', 'tn', _plain(rec['hn']), _plain(du), (dm, s5w, seq),
                     [(shape, BF16, spec)], (_tile(r_i // 2, 1024), _tile(c_i, 1024), seq))[0]
            big_grads['s5_w_in'][j] = _reduce_scatter(dw, 'rs_s5_in')
            dhn = _mm('s5_in_dx', 'nt', _plain(du), _w_rows(g_s5_in, j), (seq, dm, s5w),
                      [((seq, dm), F32, _plain_shape(None))],
                      (tm, _tile(g_s5_in.shape[3], 1024), _tile(s5w, 1024)))[0]
            dhp, dg = _norm_bwd(rec['h_in'], dhn, dhp, s5_norm_full[j:j + 1], 's5_norm_bwd')
            dh = _unpermute(dhp)
            grads['s5_norm'][j] = dg[0]
    grad_x = dh[None]

    local_small = [jnp.stack(grads[n]) if n != 'final_norm' else grads[n][0] for n in small]
    full_shapes = [a.shape for a in local_small]
    gathered = _exchange(_exchange(_pack(local_small), 'xy', 'bcast', 'ar_small_xy'), 'c', 'bcast', 'ar_small_c')
    rows = gathered.shape[2]
    summed = _sum_slots(gathered.reshape(8, rows, LANES), F32, 'ar_small_sum')
    small_full = dict(zip(small, _unpack(summed, full_shapes)))
    small_grad = {}
    for n in small:
        g = small_full[n]
        if g.shape != weights[n].shape:
            ax = [a for a in range(g.ndim) if g.shape[a] != weights[n].shape[a]][0]
            g = lax.dynamic_slice_in_dim(g, chip * weights[n].shape[ax], weights[n].shape[ax], axis=ax)
        small_grad[n] = g

    out_g, out_d, out_m, out_v = {}, {}, {}, {}
    for n in big:
        out_g[n], out_d[n], out_m[n], out_v[n] = _adamw(weights[n], mom1[n], mom2[n], big_grads[n], 'adamw_' + n)
    shapes = [weights[n].shape for n in small]
    pw_, pm_, pv_, pg_ = (_pack([d[n] for n in small]) for d in (weights, mom1, mom2, small_grad))
    _, sd, sm, sv = _adamw(pw_[None], pm_[None], pv_[None], [pg_], 'adamw_small')
    for n, d_, m_, v_ in zip(small, _unpack(sd[0], shapes), _unpack(sm[0], shapes), _unpack(sv[0], shapes)):
        out_g[n], out_d[n], out_m[n], out_v[n] = small_grad[n], d_, m_, v_

    return (loss, grad_x, *[out_g[n] for n in names], *[out_d[n] for n in names],
            *[out_m[n] for n in names], *[out_v[n] for n in names])
```

```python
import functools
import math

import jax
import jax.numpy as jnp
from jax import lax
from jax.experimental import pallas as pl
from jax.experimental.pallas import tpu as pltpu

F32 = jnp.float32
BF16 = jnp.bfloat16

EPS = 1e-6
CHUNK = 64
GLA_GATE_TEMP = 16.0
S5_EIG_CLIP = -1e-4
S5_SEGMENTS = 8
S5_GROUPS_PER_BLOCK = 8
LANES = 128
ADAM_LR = 0.001
ADAM_B1 = 0.9
ADAM_B2 = 0.999
ADAM_EPS = 1e-08
ADAM_WD = 0.01
ADAM_STEP = 10
VMEM_LIMIT_BYTES = 56 * 1024 * 1024
PAIR_PIECE_BYTES = 2 * 1024 * 1024
PAIR_VMEM_BYTES = 40 * 1024 * 1024

MESH = pl.DeviceIdType.MESH
ANY = pl.BlockSpec(memory_space=pl.ANY)
IN_VMEM = pl.BlockSpec(memory_space=pltpu.VMEM)


def _pcall(body, **kw):
    return pl.pallas_call(body, **kw)


def _params(*sem):
    return pltpu.CompilerParams(dimension_semantics=sem, vmem_limit_bytes=VMEM_LIMIT_BYTES)


def _tile(dim, target, unit=LANES):
    if dim <= target:
        return dim
    best = None
    for t in range(unit, target + 1, unit):
        if dim % t == 0:
            best = t
    assert best is not None, (dim, target)
    return best


def _exchange(x, group, mode, name):
    n = 2 if group == 'c' else 4
    blk = x.shape if mode == 'bcast' else x.shape[1:]
    if mode == 'a2a':
        assert x.shape[0] == n
    flips = [(0, 0, 1)] if group == 'c' else [(1, 0, 0), (0, 1, 0), (1, 1, 0)]
    itemsize = jnp.dtype(x.dtype).itemsize
    staged = group == 'c' and (x.size + n * math.prod(blk)) * itemsize <= PAIR_VMEM_BYTES
    if group == 'c' and not staged:
        ic = lax.axis_index('c')
        own = x if mode == 'bcast' else lax.dynamic_index_in_dim(x, ic, 0, keepdims=False)
        return lax.dynamic_update_index_in_dim(_pair_exchange_chunked(x, mode, name), own, ic, 0)

    def body(x_ref, y_ref, send_sems, recv_sems, local_sem):
        ix, iy, ic = lax.axis_index('x'), lax.axis_index('y'), lax.axis_index('c')

        def slot(px, py, pc):
            return pc if group == 'c' else 2 * px + py

        def src(px, py, pc):
            if mode == 'a2a':
                return x_ref.at[slot(px, py, pc)]
            if mode == 'bcast_c':
                return x_ref.at[ic]
            return x_ref

        me = (ix, iy, ic)
        local = pltpu.make_async_copy(src(*me), y_ref.at[slot(*me)], local_sem)
        local.start()
        peers = []
        for fx, fy, fc in flips:
            peers.append((1 - ix if fx else ix, 1 - iy if fy else iy, 1 - ic if fc else ic))
        sends = []
        for k, peer in enumerate(peers):
            cp = pltpu.make_async_remote_copy(
                src_ref=src(*peer), dst_ref=y_ref.at[slot(*me)],
                send_sem=send_sems.at[k], recv_sem=recv_sems.at[k],
                device_id=peer, device_id_type=MESH)
            cp.start()
            sends.append(cp)
        for k, peer in enumerate(peers):
            pltpu.make_async_remote_copy(
                src_ref=src(*peer), dst_ref=y_ref.at[slot(*peer)],
                send_sem=send_sems.at[k], recv_sem=recv_sems.at[k],
                device_id=peer, device_id_type=MESH).wait_recv()
        for cp in sends:
            cp.wait_send()
        local.wait()

    return _pcall(
        body, name=name,
        out_shape=jax.ShapeDtypeStruct((n,) + tuple(blk), x.dtype),
        in_specs=[IN_VMEM if staged else ANY], out_specs=IN_VMEM if staged else ANY,
        scratch_shapes=[pltpu.SemaphoreType.DMA((len(flips),)),
                        pltpu.SemaphoreType.DMA((len(flips),)),
                        pltpu.SemaphoreType.DMA(())],
        compiler_params=pltpu.CompilerParams(vmem_limit_bytes=VMEM_LIMIT_BYTES),
    )(x)


def _split_axis(blk, dtype, piece_bytes):
    itemsize = jnp.dtype(dtype).itemsize
    sublanes = 8 * 4 // itemsize
    want = max(1, math.prod(blk) * itemsize // piece_bytes)
    for pieces in [s for s in (64, 32, 16, 8, 4, 2) if s <= want]:
        for ax in range(len(blk) - 1):
            unit = sublanes if ax == len(blk) - 2 else 1
            if blk[ax] % (pieces * unit) == 0:
                return ax, pieces
    return 0, 1


def _pair_exchange_chunked(x, mode, name):
    blk = x.shape if mode == 'bcast' else x.shape[1:]
    ax, pieces = _split_axis(blk, x.dtype, PAIR_PIECE_BYTES)
    step = blk[ax] // pieces
    piece_shape = tuple(blk[:ax]) + (step,) + tuple(blk[ax + 1:])

    def piece(ref, p):
        return ref.at[(slice(None),) * ax + (pl.ds(p * step, step),)]

    def body(x_ref, y_ref, out_buf, in_buf, send_sems, recv_sems, stage_sems, drain_sems, credit_sem):
        ix, iy, ic = lax.axis_index('x'), lax.axis_index('y'), lax.axis_index('c')
        sibling = (ix, iy, 1 - ic)
        mine = x_ref.at[ic] if mode != 'bcast' else x_ref
        theirs = x_ref.at[1 - ic] if mode == 'a2a' else mine

        def stage(p):
            return pltpu.make_async_copy(piece(theirs, p), out_buf.at[p % 2], stage_sems.at[p % 2])

        def remote(p):
            return pltpu.make_async_remote_copy(
                src_ref=out_buf.at[p % 2], dst_ref=in_buf.at[p % 2],
                send_sem=send_sems.at[p], recv_sem=recv_sems.at[p],
                device_id=sibling, device_id_type=MESH)

        def drain(p):
            return pltpu.make_async_copy(in_buf.at[p % 2], piece(y_ref.at[1 - ic], p), drain_sems.at[p % 2])

        stage(0).start()
        for p in range(pieces):
            stage(p).wait()
            if p >= 2:
                pl.semaphore_wait(credit_sem, 1)
            remote(p).start()
            if p + 1 < pieces:
                if p >= 1:
                    remote(p - 1).wait_send()
                stage(p + 1).start()
            remote(p).wait_recv()
            drain(p).start()
            drain(p).wait()
            if p + 2 < pieces:
                pl.semaphore_signal(credit_sem, inc=1, device_id=sibling, device_id_type=MESH)
        for p in range(max(0, pieces - 2), pieces):
            remote(p).wait_send()

    return _pcall(
        body, name=name,
        out_shape=jax.ShapeDtypeStruct((2,) + tuple(blk), x.dtype),
        in_specs=[ANY], out_specs=ANY,
        scratch_shapes=[pltpu.VMEM((2,) + piece_shape, x.dtype), pltpu.VMEM((2,) + piece_shape, x.dtype),
                        pltpu.SemaphoreType.DMA((pieces,)), pltpu.SemaphoreType.DMA((pieces,)),
                        pltpu.SemaphoreType.DMA((2,)), pltpu.SemaphoreType.DMA((2,)),
                        pltpu.SemaphoreType.REGULAR],
        compiler_params=pltpu.CompilerParams(vmem_limit_bytes=VMEM_LIMIT_BYTES),
    )(x)


def _sum_slots(y, out_dtype, name):
    n, rows, cols = y.shape
    tm = _tile(rows, max(8, (1 << 20) // (n * cols) // 8 * 8), unit=8)

    def body(y_ref, o_ref):
        acc = y_ref[0].astype(F32)
        for k in range(1, n):
            acc = acc + y_ref[k].astype(F32)
        o_ref[...] = acc.astype(o_ref.dtype)

    return _pcall(
        body, name=name, grid=(rows // tm,),
        out_shape=jax.ShapeDtypeStruct((rows, cols), out_dtype),
        in_specs=[pl.BlockSpec((n, tm, cols), lambda i: (0, i, 0))],
        out_specs=pl.BlockSpec((tm, cols), lambda i: (i, 0)),
        compiler_params=_params('parallel'),
    )(y)


def _gather_weight(w, name):
    nl, rows, cols = w.shape
    nh = nl // 2
    wb = w.astype(BF16).reshape(2, nh, rows, cols)
    y1 = _exchange(wb, 'xy', 'bcast_c', name + '_xy')
    return _exchange(y1, 'c', 'bcast', name + '_c')


def _reduce_scatter(dw, name):
    _, _, rh, cols = dw.shape
    ya = _exchange(dw, 'c', 'a2a', name + '_pair')
    pre = _sum_slots(ya.reshape(2, 4 * rh, cols), BF16, name + '_pairsum').reshape(4, rh, cols)
    yb = _exchange(pre, 'xy', 'a2a', name + '_chips')
    fin = _sum_slots(yb, F32, name + '_chipsum')
    yc = _exchange(fin, 'c', 'bcast', name + '_back')
    return yc.reshape(2 * rh, cols)


class _Op:
    def __init__(self, arr, spec):
        self.arr = arr
        self.spec = spec


def _plain(arr):
    return _Op(arr, lambda t0, t1: ((t0, t1), lambda b0, b1: (b0, b1)))


def _plain_shape(shape):
    return lambda t0, t1: ((t0, t1), lambda b0, b1: (b0, b1))


def _w_cols(g, j):
    _, _, nh, rows, cols = g.shape
    ch, lj = j // nh, j % nh

    def spec(t0, t1):
        assert rows % t0 == 0 and cols % t1 == 0, (rows, cols, t0, t1)
        q = cols // t1
        return (None, None, None, t0, t1), lambda b0, b1: (ch, b1 // q, lj, b0, b1 % q)
    return _Op(g, spec)


def _w_rows(g, j):
    _, _, nh, rows, cols = g.shape
    ch, lj = j // nh, j % nh

    def spec(t0, t1):
        assert rows % t0 == 0 and cols % t1 == 0, (rows, cols, t0, t1)
        q = rows // t0
        return (None, None, None, t0, t1), lambda b0, b1: (ch, b0 // q, lj, b0 % q, b1)
    return _Op(g, spec)


def _dw_cols(rows, cols):
    rh = rows // 2

    def spec(t0, t1):
        assert rh % t0 == 0 and cols % t1 == 0, (rh, cols, t0, t1)
        qr, qc = rh // t0, cols // t1
        return (None, None, t0, t1), lambda b0, b1: (b0 // qr, b1 // qc, b0 % qr, b1 % qc)
    return (2, 4, rh, cols), spec


def _dw_rows(rows, cols):
    rh = rows // 2

    def spec(t0, t1):
        assert rh % t0 == 0 and cols % t1 == 0, (rh, cols, t0, t1)
        qr = rh // t0
        return (None, None, t0, t1), lambda b0, b1: ((b0 // qr) % 2, b0 // (2 * qr), b0 % qr, b1)
    return (2, 4, rh, cols), spec


def _mm(name, mode, a, b, dims, outs, tiles, epilogue=None, extras=()):
    m, n, k = dims
    tm, tn, tk = tiles
    assert m % tm == 0 and n % tn == 0 and k % tk == 0, (name, dims, tiles)
    nk = k // tk
    if mode == 'nn':
        a_t, a_ix, b_t, b_ix, ca, cb = (tm, tk), (lambda i, j, kk: (i, kk)), (tk, tn), (lambda i, j, kk: (kk, j)), 1, 0
    elif mode == 'nt':
        a_t, a_ix, b_t, b_ix, ca, cb = (tm, tk), (lambda i, j, kk: (i, kk)), (tn, tk), (lambda i, j, kk: (j, kk)), 1, 1
    else:
        a_t, a_ix, b_t, b_ix, ca, cb = (tk, tm), (lambda i, j, kk: (kk, i)), (tk, tn), (lambda i, j, kk: (kk, j)), 0, 0
    a_blk, a_fn = a.spec(*a_t)
    b_blk, b_fn = b.spec(*b_t)
    in_specs = [pl.BlockSpec(a_blk, lambda i, j, kk: a_fn(*a_ix(i, j, kk))),
                pl.BlockSpec(b_blk, lambda i, j, kk: b_fn(*b_ix(i, j, kk)))]
    operands = [a.arr, b.arr]
    for e in extras:
        e_blk, e_fn = e.spec(tm, tn)
        in_specs.append(pl.BlockSpec(e_blk, functools.partial(lambda i, j, kk, f: f(i, j), f=e_fn)))
        operands.append(e.arr)
    out_shapes, out_specs = [], []
    for shape, dtype, spec in outs:
        o_blk, o_fn = spec(tm, tn)
        out_shapes.append(jax.ShapeDtypeStruct(shape, dtype))
        out_specs.append(pl.BlockSpec(o_blk, functools.partial(lambda i, j, kk, f: f(i, j), f=o_fn)))
    n_ex, n_out = len(extras), len(outs)
    if epilogue is None:
        epilogue = lambda acc: (acc,)

    def body(a_ref, b_ref, *rest):
        ex_refs = rest[:n_ex]
        out_refs = rest[n_ex:n_ex + n_out]
        p = lax.dot_general(a_ref[...].astype(BF16), b_ref[...].astype(BF16),
                            (((ca,), (cb,)), ((), ())), preferred_element_type=F32)

        def finish(acc):
            res = epilogue(acc, *[r[...] for r in ex_refs])
            for o_ref, val in zip(out_refs, res):
                o_ref[...] = val.astype(o_ref.dtype)

        if nk == 1:
            finish(p)
        else:
            acc_ref = rest[n_ex + n_out]
            kk = pl.program_id(2)

            @pl.when(kk == 0)
            def _():
                acc_ref[...] = p

            @pl.when(kk > 0)
            def _():
                acc_ref[...] += p

            @pl.when(kk == nk - 1)
            def _():
                finish(acc_ref[...])

    res = _pcall(
        body, name=name, grid=(m // tm, n // tn, nk),
        out_shape=out_shapes, in_specs=in_specs, out_specs=out_specs,
        scratch_shapes=[pltpu.VMEM((tm, tn), F32)] if nk > 1 else [],
        compiler_params=_params('parallel', 'parallel', 'arbitrary'),
    )(*operands)
    return res


def _rowwise(name, fn, row_ins, vec_ins, outs, reds, tm):
    rows = row_ins[0].shape[0]
    assert rows % tm == 0
    n_in = len(row_ins) + len(vec_ins)
    n_out = len(outs)

    def body(*refs):
        vals = [r[...] for r in refs[:n_in]]
        res = fn(*vals)
        for o_ref, val in zip(refs[n_in:n_in + n_out], res[:n_out]):
            o_ref[...] = val.astype(o_ref.dtype)
        first = pl.program_id(0) == 0
        for r_ref, val in zip(refs[n_in + n_out:], res[n_out:]):
            @pl.when(first)
            def _(r_ref=r_ref, val=val):
                r_ref[...] = val

            @pl.when(jnp.logical_not(first))
            def _(r_ref=r_ref, val=val):
                r_ref[...] += val

    in_specs = [pl.BlockSpec((tm, a.shape[1]), lambda i: (i, 0)) for a in row_ins]
    in_specs += [pl.BlockSpec((1, v.shape[1]), lambda i: (0, 0)) for v in vec_ins]
    out_shapes = [jax.ShapeDtypeStruct((rows, w), dt) for w, dt in outs]
    out_shapes += [jax.ShapeDtypeStruct((1, w), F32) for w in reds]
    out_specs = [pl.BlockSpec((tm, w), lambda i: (i, 0)) for w, _ in outs]
    out_specs += [pl.BlockSpec((1, w), lambda i: (0, 0)) for w in reds]
    return _pcall(
        body, name=name, grid=(rows // tm,),
        out_shape=out_shapes, in_specs=in_specs, out_specs=out_specs,
        compiler_params=_params('arbitrary'),
    )(*row_ins, *vec_ins)


def _norm_fwd(h, g, name):
    def fn(hv, gv):
        rstd = lax.rsqrt(jnp.mean(hv * hv, axis=-1, keepdims=True) + EPS)
        return (hv * rstd * gv,)
    return _rowwise(name, fn, [h], [g], [(h.shape[1], BF16)], [], 256)[0]


def _norm_bwd(h, dhn, dres, g, name):
    def fn(hv, dv, rv, gv):
        rstd = lax.rsqrt(jnp.mean(hv * hv, axis=-1, keepdims=True) + EPS)
        xhat = hv * rstd
        dxhat = dv * gv
        dh = rv + rstd * (dxhat - xhat * jnp.mean(dxhat * xhat, axis=-1, keepdims=True))
        return dh, jnp.sum(dv * xhat, axis=0, keepdims=True)
    w = h.shape[1]
    return _rowwise(name, fn, [h, dhn, dres], [g], [(w, F32)], [w], 256)


def _loss_head(h, target, g, name):
    w = h.shape[1]

    def fn(hv, tv, gv):
        rstd = lax.rsqrt(jnp.mean(hv * hv, axis=-1, keepdims=True) + EPS)
        xhat = hv * rstd
        diff = xhat * gv - tv
        dy = diff * (1.0 / w)
        dxhat = dy * gv
        dh = rstd * (dxhat - xhat * jnp.mean(dxhat * xhat, axis=-1, keepdims=True))
        return (dh, jnp.sum(0.5 * dy * diff, axis=0, keepdims=True),
                jnp.sum(dy * xhat, axis=0, keepdims=True))
    return _rowwise(name, fn, [h, target], [g], [(w, F32)], [w, w], 256)


def _split3(x):
    hi = x.astype(BF16)
    r1 = x - hi.astype(F32)
    mid = r1.astype(BF16)
    lo = (r1 - mid.astype(F32)).astype(BF16)
    return hi, mid, lo


def _tri_dot(tri, x):
    hi, mid, lo = _split3(x)
    d = lambda p: jnp.dot(tri, p, preferred_element_type=F32)
    return d(hi) + d(mid) + d(lo)


def _log_sigmoid(x):
    return jnp.minimum(x, 0.0) - jnp.log(1.0 + jnp.exp(-jnp.abs(x)))


def _gla_dims(proj_w, kw, vw, dk, dv):
    assert kw % dk == 0 and (2 * kw) % dv == 0 and (2 * kw + vw) % dv == 0 and (2 * kw + 2 * vw) % LANES == 0
    return dict(q0=0, k0=kw // dk, v0=2 * kw // dv, r0=(2 * kw + vw) // dv, g0=(2 * kw + 2 * vw) // LANES)


def _gla_gates(gl, wgu, bias):
    pre = jnp.dot(gl.astype(BF16), wgu, preferred_element_type=F32) + bias
    la = _log_sigmoid(pre) * (1.0 / GLA_GATE_TEMP)
    r_i = lax.broadcasted_iota(jnp.int32, (CHUNK, CHUNK), 0)
    c_i = lax.broadcasted_iota(jnp.int32, (CHUNK, CHUNK), 1)
    cum = _tri_dot((c_i <= r_i).astype(BF16), la)
    total = cum[CHUNK - 1:CHUNK, :]
    return pre, cum, total


def _gla_scan_fwd(proj, wgu_pad, b_gate, o_norm, heads, kw, vw, tb, name):
    seq, pw = proj.shape
    dk, dv = kw // heads, vw // heads
    cb = tb // CHUNK
    nt = seq // tb
    o = _gla_dims(pw, kw, vw, dk, dv)
    scale = dk ** -0.5

    def body(q_ref, k_ref, v_ref, r_ref, gl_ref, wgu_ref, b_ref, on_ref, out_ref, st_ref, s_scr):
        @pl.when(pl.program_id(1) == 0)
        def _():
            s_scr[...] = jnp.zeros_like(s_scr)

        wgu = wgu_ref[...].astype(BF16)
        bias = b_ref[...]
        onorm = on_ref[...]
        st = s_scr[...]
        for ci in range(cb):
            rows = pl.ds(ci * CHUNK, CHUNK)
            _, cum, total = _gla_gates(gl_ref[rows, :], wgu, bias)
            kdec = k_ref[rows, :] * jnp.exp(total - cum)
            st = st * jnp.exp(total) + lax.dot_general(
                v_ref[rows, :].astype(BF16), kdec.astype(BF16), (((0,), (0,)), ((), ())),
                preferred_element_type=F32)
            st_ref[ci] = st
            qs = (q_ref[rows, :] * scale).astype(BF16)
            ov = lax.dot_general(qs, st.astype(BF16), (((1,), (1,)), ((), ())), preferred_element_type=F32)
            rstd = lax.rsqrt(jnp.mean(ov * ov, axis=-1, keepdims=True) + EPS)
            rv = r_ref[rows, :]
            out_ref[rows, :] = (ov * rstd * onorm * (rv * jax.nn.sigmoid(rv))).astype(out_ref.dtype)
        s_scr[...] = st

    in_specs = [
        pl.BlockSpec((tb, dk), lambda h, t: (t, o['q0'] + h)),
        pl.BlockSpec((tb, dk), lambda h, t: (t, o['k0'] + h)),
        pl.BlockSpec((tb, dv), lambda h, t: (t, o['v0'] + h)),
        pl.BlockSpec((tb, dv), lambda h, t: (t, o['r0'] + h)),
        pl.BlockSpec((tb, LANES), lambda h, t: (t, o['g0'])),
        pl.BlockSpec((LANES, dk), lambda h, t: (0, h)),
        pl.BlockSpec((1, dk), lambda h, t: (0, h)),
        pl.BlockSpec((1, dv), lambda h, t: (0, 0)),
    ]
    return _pcall(
        body, name=name, grid=(heads, nt),
        out_shape=[jax.ShapeDtypeStruct((seq, vw), BF16),
                   jax.ShapeDtypeStruct((heads, seq // CHUNK, dv, dk), F32)],
        in_specs=in_specs,
        out_specs=[pl.BlockSpec((tb, dv), lambda h, t: (t, h)),
                   pl.BlockSpec((None, cb, dv, dk), lambda h, t: (h, t, 0, 0))],
        scratch_shapes=[pltpu.VMEM((dv, dk), F32)],
        compiler_params=_params('parallel', 'arbitrary'),
    )(proj, proj, proj, proj, proj, wgu_pad, b_gate, o_norm)


def _gla_scan_bwd(proj, wgu_pad, b_gate, o_norm, states, dgated, heads, kw, vw, tb, name):
    seq, pw = proj.shape
    dk, dv = kw // heads, vw // heads
    cb = tb // CHUNK
    nt = seq // tb
    o = _gla_dims(pw, kw, vw, dk, dv)
    scale = dk ** -0.5

    def body(q_ref, k_ref, v_ref, r_ref, gl_ref, wgu_ref, b_ref, on_ref, st_ref, stp_ref, dg_ref,
             dq_ref, dk_ref, dv_ref, dr_ref, dpre_ref, db_ref, don_ref, ds_scr):
        hh = pl.program_id(0)
        t = pl.program_id(1)

        @pl.when(t == 0)
        def _():
            ds_scr[...] = jnp.zeros_like(ds_scr)
            db_ref[...] = jnp.zeros_like(db_ref)

        @pl.when(jnp.logical_and(hh == 0, t == 0))
        def _():
            don_ref[...] = jnp.zeros_like(don_ref)

        wgu = wgu_ref[...].astype(BF16)
        bias = b_ref[...]
        onorm = on_ref[...]
        has_prev = (t < nt - 1).astype(F32)
        r_i = lax.broadcasted_iota(jnp.int32, (CHUNK, CHUNK), 0)
        c_i = lax.broadcasted_iota(jnp.int32, (CHUNK, CHUNK), 1)
        strict = (c_i < r_i).astype(BF16)
        carry = ds_scr[...]
        db_acc = jnp.zeros((1, dk), F32)
        don_acc = jnp.zeros((1, dv), F32)
        for ci in reversed(range(cb)):
            rows = pl.ds(ci * CHUNK, CHUNK)
            pre, cum, total = _gla_gates(gl_ref[rows, :], wgu, bias)
            edec = jnp.exp(total - cum)
            decay = jnp.exp(total)
            kdec = k_ref[rows, :] * edec
            st = st_ref[ci]
            st_prev = st_ref[ci - 1] if ci > 0 else stp_ref[0] * has_prev
            stb = st.astype(BF16)
            qs = (q_ref[rows, :] * scale).astype(BF16)
            vb = v_ref[rows, :].astype(BF16)
            ov = lax.dot_general(qs, stb, (((1,), (1,)), ((), ())), preferred_element_type=F32)
            rstd = lax.rsqrt(jnp.mean(ov * ov, axis=-1, keepdims=True) + EPS)
            ohat = ov * rstd
            rv = r_ref[rows, :]
            sr = jax.nn.sigmoid(rv)
            dgv = dg_ref[rows, :]
            dy = dgv * (rv * sr)
            dr_ref[rows, :] = (dgv * (ohat * onorm) * (sr * (1.0 + rv * (1.0 - sr)))).astype(dr_ref.dtype)
            don_acc = don_acc + jnp.sum(dy * ohat, axis=0, keepdims=True)
            dohat = dy * onorm
            do = (rstd * (dohat - ohat * jnp.mean(dohat * ohat, axis=-1, keepdims=True))).astype(BF16)
            dq_ref[rows, :] = (jnp.dot(do, stb, preferred_element_type=F32) * scale).astype(dq_ref.dtype)
            dst = carry + lax.dot_general(do, qs, (((0,), (0,)), ((), ())), preferred_element_type=F32)
            dstb = dst.astype(BF16)
            dkdec = jnp.dot(vb, dstb, preferred_element_type=F32)
            dv_ref[rows, :] = lax.dot_general(kdec.astype(BF16), dstb, (((1,), (1,)), ((), ())),
                                              preferred_element_type=F32).astype(dv_ref.dtype)
            ddecay = jnp.sum(dst * st_prev, axis=0, keepdims=True)
            dk_ref[rows, :] = (dkdec * edec).astype(dk_ref.dtype)
            da = ddecay * decay + _tri_dot(strict, dkdec * kdec)
            dpre = da * (1.0 / GLA_GATE_TEMP) * (1.0 - jax.nn.sigmoid(pre))
            dpre_ref[rows, :] = dpre.astype(dpre_ref.dtype)
            db_acc = db_acc + jnp.sum(dpre, axis=0, keepdims=True)
            carry = dst * decay
        ds_scr[...] = carry
        db_ref[...] += db_acc
        don_ref[...] += don_acc

    rt = lambda t: nt - 1 - t
    in_specs = [
        pl.BlockSpec((tb, dk), lambda h, t: (rt(t), o['q0'] + h)),
        pl.BlockSpec((tb, dk), lambda h, t: (rt(t), o['k0'] + h)),
        pl.BlockSpec((tb, dv), lambda h, t: (rt(t), o['v0'] + h)),
        pl.BlockSpec((tb, dv), lambda h, t: (rt(t), o['r0'] + h)),
        pl.BlockSpec((tb, LANES), lambda h, t: (rt(t), o['g0'])),
        pl.BlockSpec((LANES, dk), lambda h, t: (0, h)),
        pl.BlockSpec((1, dk), lambda h, t: (0, h)),
        pl.BlockSpec((1, dv), lambda h, t: (0, 0)),
        pl.BlockSpec((None, cb, dv, dk), lambda h, t: (h, rt(t), 0, 0)),
        pl.BlockSpec((None, 1, dv, dk), lambda h, t: (h, jnp.maximum(rt(t) * cb - 1, 0), 0, 0)),
        pl.BlockSpec((tb, dv), lambda h, t: (rt(t), h)),
    ]
    out_shape = [jax.ShapeDtypeStruct((seq, kw), BF16), jax.ShapeDtypeStruct((seq, kw), BF16),
                 jax.ShapeDtypeStruct((seq, vw), BF16), jax.ShapeDtypeStruct((seq, vw), BF16),
                 jax.ShapeDtypeStruct((seq, kw), BF16),
                 jax.ShapeDtypeStruct((1, kw), F32), jax.ShapeDtypeStruct((1, dv), F32)]
    out_specs = [pl.BlockSpec((tb, dk), lambda h, t: (rt(t), h)),
                 pl.BlockSpec((tb, dk), lambda h, t: (rt(t), h)),
                 pl.BlockSpec((tb, dv), lambda h, t: (rt(t), h)),
                 pl.BlockSpec((tb, dv), lambda h, t: (rt(t), h)),
                 pl.BlockSpec((tb, dk), lambda h, t: (rt(t), h)),
                 pl.BlockSpec((1, dk), lambda h, t: (0, h)),
                 pl.BlockSpec((1, dv), lambda h, t: (0, 0))]
    return _pcall(
        body, name=name, grid=(heads, nt),
        out_shape=out_shape, in_specs=in_specs, out_specs=out_specs,
        scratch_shapes=[pltpu.VMEM((dv, dk), F32)],
        compiler_params=_params('arbitrary', 'arbitrary'),
    )(proj, proj, proj, proj, proj, wgu_pad, b_gate, o_norm, states, states, dgated)


def _cmul(ar, ai, br, bi):
    return ar * br - ai * bi, ar * bi + ai * br


def _gelu(y):
    c = math.sqrt(2.0 / math.pi)
    return 0.5 * y * (1.0 + jnp.tanh(c * (y + 0.044715 * y * y * y)))


def _gelu_grad(y):
    c = math.sqrt(2.0 / math.pi)
    th = jnp.tanh(c * (y + 0.044715 * y * y * y))
    return 0.5 * (1.0 + th) + 0.5 * y * (1.0 - th * th) * (c * (1.0 + 3.0 * 0.044715 * y * y))


def _power_pow2(ar, ai, n):
    assert n & (n - 1) == 0
    for _ in range(n.bit_length() - 1):
        ar, ai = _cmul(ar, ai, ar, ai)
    return ar, ai


def _s5_fwd(u, bre, bim, cre, cim, are, aim, dskip, name):
    seq, width = u.shape
    nb, ub, sb = bre.shape
    ls = seq // S5_SEGMENTS
    seg = S5_SEGMENTS

    def body(u_ref, bre_ref, bim_ref, cre_ref, cim_ref, are_ref, aim_ref, d_ref, y_ref, z_ref, xr_ref, xi_ref):
        uv = u_ref[...]
        ub16 = uv.astype(BF16)
        xr_ref[...] = jnp.dot(ub16, bre_ref[...].astype(BF16), preferred_element_type=F32)
        xi_ref[...] = jnp.dot(ub16, bim_ref[...].astype(BF16), preferred_element_type=F32)
        ar = jnp.broadcast_to(are_ref[...], (seg, sb))
        ai = jnp.broadcast_to(aim_ref[...], (seg, sb))

        def step(i, c):
            rows = pl.ds(pl.multiple_of(i * seg, seg), seg)
            pr, pi = _cmul(ar, ai, c[0], c[1])
            nr = pr + xr_ref[rows, :]
            ni = pi + xi_ref[rows, :]
            xr_ref[rows, :] = nr
            xi_ref[rows, :] = ni
            return nr, ni

        zero = jnp.zeros((seg, sb), F32)
        er, ei = lax.fori_loop(0, ls, step, (zero, zero), unroll=8)
        pr, pi = _power_pow2(ar, ai, ls)
        row = lax.broadcasted_iota(jnp.int32, (seg, sb), 0)
        sr, si = zero, zero
        for _ in range(seg - 1):
            tr, ti = _cmul(pr, pi, sr, si)
            sr = jnp.where(row == 0, 0.0, pltpu.roll(tr + er, 1, 0))
            si = jnp.where(row == 0, 0.0, pltpu.roll(ti + ei, 1, 0))

        def fix(i, c):
            rows = pl.ds(pl.multiple_of(i * seg, seg), seg)
            fr, fi = _cmul(c[0], c[1], sr, si)
            xr_ref[rows, :] += fr
            xi_ref[rows, :] += fi
            return _cmul(c[0], c[1], ar, ai)

        lax.fori_loop(0, ls, fix, (ar, ai), unroll=8)
        y = (jnp.dot(xr_ref[...].astype(BF16), cre_ref[...].astype(BF16), preferred_element_type=F32)
             - jnp.dot(xi_ref[...].astype(BF16), cim_ref[...].astype(BF16), preferred_element_type=F32)
             + d_ref[...] * uv)
        y_ref[...] = y
        z_ref[...] = _gelu(y).astype(z_ref.dtype)

    mat = lambda r, c: pl.BlockSpec((None, r, c), lambda b: (b, 0, 0))
    return _pcall(
        body, name=name, grid=(nb,),
        out_shape=[jax.ShapeDtypeStruct((seq, width), F32), jax.ShapeDtypeStruct((seq, width), BF16),
                   jax.ShapeDtypeStruct((seq, nb * sb), F32), jax.ShapeDtypeStruct((seq, nb * sb), F32)],
        in_specs=[pl.BlockSpec((seq, ub), lambda b: (0, b)), mat(ub, sb), mat(ub, sb), mat(sb, ub), mat(sb, ub),
                  mat(1, sb), mat(1, sb), pl.BlockSpec((1, ub), lambda b: (0, b))],
        out_specs=[pl.BlockSpec((seq, ub), lambda b: (0, b)), pl.BlockSpec((seq, ub), lambda b: (0, b)),
                   pl.BlockSpec((seq, sb), lambda b: (0, b)), pl.BlockSpec((seq, sb), lambda b: (0, b))],
        compiler_params=_params('parallel'),
    )(u, bre, bim, cre, cim, are, aim, dskip)


def _s5_bwd(dz, y, u, xr, xi, bre, bim, cre, cim, are, aim, dskip, name):
    seq, width = u.shape
    nb, ub, sb = bre.shape
    ls = seq // S5_SEGMENTS
    seg = S5_SEGMENTS

    def body(dz_ref, y_ref, u_ref, xr_ref, xi_ref, bre_ref, bim_ref, cre_ref, cim_ref, are_ref, aim_ref, d_ref,
             du_ref, dcr_ref, dci_ref, dbr_ref, dbi_ref, dar_ref, dai_ref, dd_ref, lr_ref, li_ref):
        uv = u_ref[...]
        dy = dz_ref[...] * _gelu_grad(y_ref[...])
        dd_ref[...] = jnp.sum(dy * uv, axis=0, keepdims=True)
        dyb = dy.astype(BF16)
        nt = (((1,), (1,)), ((), ()))
        tn = (((0,), (0,)), ((), ()))
        lr_ref[...] = lax.dot_general(dyb, cre_ref[...].astype(BF16), nt, preferred_element_type=F32)
        li_ref[...] = -lax.dot_general(dyb, cim_ref[...].astype(BF16), nt, preferred_element_type=F32)
        dcr_ref[...] = lax.dot_general(dyb, xr_ref[...].astype(BF16), tn, preferred_element_type=F32)
        dci_ref[...] = -lax.dot_general(dyb, xi_ref[...].astype(BF16), tn, preferred_element_type=F32)
        ar = jnp.broadcast_to(are_ref[...], (seg, sb))
        ai = jnp.broadcast_to(aim_ref[...], (seg, sb))
        nai = -ai

        def step(ii, c):
            rows = pl.ds(pl.multiple_of((ls - 1 - ii) * seg, seg), seg)
            pr, pi = _cmul(ar, nai, c[0], c[1])
            nr = pr + lr_ref[rows, :]
            ni = pi + li_ref[rows, :]
            lr_ref[rows, :] = nr
            li_ref[rows, :] = ni
            return nr, ni

        zero = jnp.zeros((seg, sb), F32)
        er, ei = lax.fori_loop(0, ls, step, (zero, zero), unroll=8)
        pr, pi = _power_pow2(ar, nai, ls)
        row = lax.broadcasted_iota(jnp.int32, (seg, sb), 0)
        rr, ri = zero, zero
        for _ in range(seg - 1):
            tr, ti = _cmul(pr, pi, rr, ri)
            rr = jnp.where(row == seg - 1, 0.0, pltpu.roll(tr + er, seg - 1, 0))
            ri = jnp.where(row == seg - 1, 0.0, pltpu.roll(ti + ei, seg - 1, 0))

        def corrected(rows, qr, qi):
            fr, fi = _cmul(qr, qi, rr, ri)
            nr = lr_ref[rows, :] + fr
            ni = li_ref[rows, :] + fi
            lr_ref[rows, :] = nr
            li_ref[rows, :] = ni
            return nr, ni

        def grad_a(nr, ni, xpr, xpi, accr, acci):
            return accr + nr * xpr + ni * xpi, acci + ni * xpr - nr * xpi

        def fix(ii, c):
            qr, qi, accr, acci = c
            i = ls - 1 - ii
            rows = pl.ds(pl.multiple_of(i * seg, seg), seg)
            prev = pl.ds(pl.multiple_of((i - 1) * seg, seg), seg)
            nr, ni = corrected(rows, qr, qi)
            accr, acci = grad_a(nr, ni, xr_ref[prev, :], xi_ref[prev, :], accr, acci)
            qr, qi = _cmul(qr, qi, ar, nai)
            return qr, qi, accr, acci

        qr, qi, accr, acci = lax.fori_loop(0, ls - 1, fix, (ar, nai, zero, zero), unroll=8)
        nr, ni = corrected(pl.ds(0, seg), qr, qi)
        last = pl.ds((ls - 1) * seg, seg)
        xpr = jnp.where(row == 0, 0.0, pltpu.roll(xr_ref[last, :], 1, 0))
        xpi = jnp.where(row == 0, 0.0, pltpu.roll(xi_ref[last, :], 1, 0))
        accr, acci = grad_a(nr, ni, xpr, xpi, accr, acci)
        dar_ref[...] = jnp.sum(accr, axis=0, keepdims=True)
        dai_ref[...] = jnp.sum(acci, axis=0, keepdims=True)
        lrb = lr_ref[...].astype(BF16)
        lib = li_ref[...].astype(BF16)
        ub16 = uv.astype(BF16)
        dbr_ref[...] = lax.dot_general(ub16, lrb, tn, preferred_element_type=F32)
        dbi_ref[...] = lax.dot_general(ub16, lib, tn, preferred_element_type=F32)
        du_ref[...] = (d_ref[...] * dy
                       + lax.dot_general(lrb, bre_ref[...].astype(BF16), nt, preferred_element_type=F32)
                       + lax.dot_general(lib, bim_ref[...].astype(BF16), nt, preferred_element_type=F32))

    mat = lambda r, c: pl.BlockSpec((None, r, c), lambda b: (b, 0, 0))
    col = lambda w: pl.BlockSpec((seq, w), lambda b: (0, b))
    return _pcall(
        body, name=name, grid=(nb,),
        out_shape=[jax.ShapeDtypeStruct((seq, width), F32)]
        + [jax.ShapeDtypeStruct((nb, ub, sb), F32)] * 4
        + [jax.ShapeDtypeStruct((nb, 1, sb), F32)] * 2
        + [jax.ShapeDtypeStruct((1, width), F32)],
        in_specs=[col(ub), col(ub), col(ub), col(sb), col(sb), mat(ub, sb), mat(ub, sb), mat(sb, ub), mat(sb, ub),
                  mat(1, sb), mat(1, sb), pl.BlockSpec((1, ub), lambda b: (0, b))],
        out_specs=[col(ub), mat(ub, sb), mat(ub, sb), mat(ub, sb), mat(ub, sb), mat(1, sb), mat(1, sb),
                   pl.BlockSpec((1, ub), lambda b: (0, b))],
        scratch_shapes=[pltpu.VMEM((seq, sb), F32), pltpu.VMEM((seq, sb), F32)],
        compiler_params=_params('parallel'),
    )(dz, y, u, xr, xi, bre, bim, cre, cim, are, aim, dskip)


def _s5_discretise(lam_re, lam_im, log_dt, b_re, b_im):
    lr = jnp.minimum(lam_re, S5_EIG_CLIP)
    li = lam_im
    dt = jnp.exp(log_dt)[:, None]
    mag = jnp.exp(lr * dt)
    ang = li * dt
    ab_re = mag * jnp.cos(ang)
    ab_im = mag * jnp.sin(ang)
    den = lr * lr + li * li
    nr = ab_re - 1.0
    f_re = (nr * lr + ab_im * li) / den
    f_im = (ab_im * lr - nr * li) / den
    bb_re = f_re[..., None] * b_re - f_im[..., None] * b_im
    bb_im = f_re[..., None] * b_im + f_im[..., None] * b_re
    return ab_re, ab_im, bb_re, bb_im


def _to_blocks(m):
    g, a, b = m.shape
    gb = S5_GROUPS_PER_BLOCK
    eye = jnp.eye(gb, dtype=m.dtype)
    return jnp.einsum('bgac,gh->bgahc', m.reshape(g // gb, gb, a, b), eye).reshape(g // gb, gb * a, gb * b)


def _from_blocks(m, a, b):
    nb = m.shape[0]
    gb = S5_GROUPS_PER_BLOCK
    eye = jnp.eye(gb, dtype=m.dtype)
    return jnp.einsum('bgahc,gh->bgac', m.reshape(nb, gb, a, gb, b), eye).reshape(nb * gb, a, b)


def _glu_fwd(o, h, name):
    half = o.shape[1] // 2

    def fn(ov, hv):
        return (hv + ov[:, :half] * jax.nn.sigmoid(ov[:, half:]),)
    return _rowwise(name, fn, [o, h], [], [(half, F32)], [], 256)[0]


def _glu_bwd(o, dout, name):
    half = o.shape[1] // 2

    def fn(ov, dv):
        val, gate = ov[:, :half], ov[:, half:]
        sg = jax.nn.sigmoid(gate)
        return (jnp.concatenate([dv * sg, dv * val * sg * (1.0 - sg)], axis=1),)
    return _rowwise(name, fn, [o, dout], [], [(2 * half, BF16)], [], 256)[0]


def _adam_math(w, g, m, v):
    m = ADAM_B1 * m + (1.0 - ADAM_B1) * g
    v = ADAM_B2 * v + (1.0 - ADAM_B2) * (g * g)
    m_hat = m / (1.0 - ADAM_B1 ** ADAM_STEP)
    v_hat = v / (1.0 - ADAM_B2 ** ADAM_STEP)
    delta = -ADAM_LR * (m_hat / (jnp.sqrt(v_hat) + ADAM_EPS) + ADAM_WD * w)
    return delta, m, v


def _adamw(w, m, v, grads, name):
    nl, rows, cols = w.shape
    tm = _tile(rows, max(8, (1 << 18) // cols // 8 * 8), unit=8)
    nbk = rows // tm

    def body(*refs):
        w_ref, m_ref, v_ref = refs[:3]
        g_refs = refs[3:3 + nl]
        go_ref, d_ref, mo_ref, vo_ref = refs[3 + nl:]
        layer = pl.program_id(0)
        g = g_refs[0][...]
        for l in range(1, nl):
            g = jnp.where(layer == l, g_refs[l][...], g)
        delta, mn, vn = _adam_math(w_ref[...], g, m_ref[...], v_ref[...])
        go_ref[...] = g
        d_ref[...] = delta
        mo_ref[...] = mn
        vo_ref[...] = vn

    stacked = pl.BlockSpec((None, tm, cols), lambda l, i: (l, i, 0))

    def g_spec(layer):
        return pl.BlockSpec((tm, cols), lambda l, i: (jnp.where(l == layer, i, jnp.where(l < layer, 0, nbk - 1)), 0))

    return _pcall(
        body, name=name, grid=(nl, nbk),
        out_shape=[jax.ShapeDtypeStruct(w.shape, F32)] * 4,
        in_specs=[stacked] * 3 + [g_spec(l) for l in range(nl)],
        out_specs=[stacked] * 4,
        compiler_params=_params('arbitrary', 'arbitrary'),
    )(w, m, v, *grads)


def _pack(arrs, rows_mult=512):
    flat = jnp.concatenate([a.reshape(-1) for a in arrs])
    total = flat.shape[0]
    rows = -(-total // LANES)
    rows = -(-rows // rows_mult) * rows_mult
    flat = jnp.pad(flat, (0, rows * LANES - total))
    return flat.reshape(rows, LANES)


def _unpack(packed, shapes):
    flat = packed.reshape(-1)
    out, off = [], 0
    for s in shapes:
        size = math.prod(s)
        out.append(flat[off:off + size].reshape(s))
        off += size
    return out


def _permute(a):
    seq, w = a.shape
    return a.reshape(S5_SEGMENTS, seq // S5_SEGMENTS, w).transpose(1, 0, 2).reshape(seq, w)


def _unpermute(a):
    seq, w = a.shape
    return a.reshape(seq // S5_SEGMENTS, S5_SEGMENTS, w).transpose(1, 0, 2).reshape(seq, w)


def kernel(x, gla_norm, gla_w_in, gla_w_gate_up, gla_b_gate, gla_o_norm, gla_w_out, s5_norm, s5_w_in, s5_lam_re, s5_lam_im, s5_log_dt, s5_b_re, s5_b_im, s5_c_re, s5_c_im, s5_d, s5_w_out, mlp_norm, mlp_w_up, mlp_w_down, final_norm, loss_target, m_gla_norm, m_gla_w_in, m_gla_w_gate_up, m_gla_b_gate, m_gla_o_norm, m_gla_w_out, m_s5_norm, m_s5_w_in, m_s5_lam_re, m_s5_lam_im, m_s5_log_dt, m_s5_b_re, m_s5_b_im, m_s5_c_re, m_s5_c_im, m_s5_d, m_s5_w_out, m_mlp_norm, m_mlp_w_up, m_mlp_w_down, m_final_norm, v_gla_norm, v_gla_w_in, v_gla_w_gate_up, v_gla_b_gate, v_gla_o_norm, v_gla_w_out, v_s5_norm, v_s5_w_in, v_s5_lam_re, v_s5_lam_im, v_s5_log_dt, v_s5_b_re, v_s5_b_im, v_s5_c_re, v_s5_c_im, v_s5_d, v_s5_w_out, v_mlp_norm, v_mlp_w_up, v_mlp_w_down, v_final_norm):
    weights = dict(gla_norm=gla_norm, gla_w_in=gla_w_in, gla_w_gate_up=gla_w_gate_up, gla_b_gate=gla_b_gate, gla_o_norm=gla_o_norm, gla_w_out=gla_w_out, s5_norm=s5_norm, s5_w_in=s5_w_in, s5_lam_re=s5_lam_re, s5_lam_im=s5_lam_im, s5_log_dt=s5_log_dt, s5_b_re=s5_b_re, s5_b_im=s5_b_im, s5_c_re=s5_c_re, s5_c_im=s5_c_im, s5_d=s5_d, s5_w_out=s5_w_out, mlp_norm=mlp_norm, mlp_w_up=mlp_w_up, mlp_w_down=mlp_w_down, final_norm=final_norm)
    mom1 = dict(gla_norm=m_gla_norm, gla_w_in=m_gla_w_in, gla_w_gate_up=m_gla_w_gate_up, gla_b_gate=m_gla_b_gate, gla_o_norm=m_gla_o_norm, gla_w_out=m_gla_w_out, s5_norm=m_s5_norm, s5_w_in=m_s5_w_in, s5_lam_re=m_s5_lam_re, s5_lam_im=m_s5_lam_im, s5_log_dt=m_s5_log_dt, s5_b_re=m_s5_b_re, s5_b_im=m_s5_b_im, s5_c_re=m_s5_c_re, s5_c_im=m_s5_c_im, s5_d=m_s5_d, s5_w_out=m_s5_w_out, mlp_norm=m_mlp_norm, mlp_w_up=m_mlp_w_up, mlp_w_down=m_mlp_w_down, final_norm=m_final_norm)
    mom2 = dict(gla_norm=v_gla_norm, gla_w_in=v_gla_w_in, gla_w_gate_up=v_gla_w_gate_up, gla_b_gate=v_gla_b_gate, gla_o_norm=v_gla_o_norm, gla_w_out=v_gla_w_out, s5_norm=v_s5_norm, s5_w_in=v_s5_w_in, s5_lam_re=v_s5_lam_re, s5_lam_im=v_s5_lam_im, s5_log_dt=v_s5_log_dt, s5_b_re=v_s5_b_re, s5_b_im=v_s5_b_im, s5_c_re=v_s5_c_re, s5_c_im=v_s5_c_im, s5_d=v_s5_d, s5_w_out=v_s5_w_out, mlp_norm=v_mlp_norm, mlp_w_up=v_mlp_w_up, mlp_w_down=v_mlp_w_down, final_norm=v_final_norm)
    names = list(weights)
    big = ['gla_w_in', 'gla_w_out', 's5_w_in', 's5_w_out', 'mlp_w_up', 'mlp_w_down']
    small = [n for n in names if n not in big]

    chip = 2 * lax.axis_index('x') + lax.axis_index('y')
    h0 = x[0]
    target = loss_target[0]
    seq, dm = h0.shape
    depth = mlp_norm.shape[0]
    n_gla = gla_norm.shape[0]
    n_s5 = s5_lam_re.shape[0]
    rank = gla_w_gate_up.shape[1]
    kw = gla_b_gate.shape[1]
    dv = gla_o_norm.shape[1]
    in_w = 4 * gla_w_in.shape[2]
    vw = (in_w - rank - 2 * kw) // 2
    heads = vw // dv
    dk = kw // heads
    pw = -(-in_w // LANES) * LANES
    s5w = s5_w_in.shape[2]
    n_grp, n_state, grp = s5_b_re.shape[1:]
    hid = 4 * mlp_w_up.shape[2]
    tb = min(seq, 8 * CHUNK)
    tm = _tile(seq, 1024)

    g_w_in = _gather_weight(gla_w_in, 'ag_gla_w_in')
    g_gla_out = _gather_weight(gla_w_out, 'ag_gla_w_out')
    g_s5_in = _gather_weight(s5_w_in, 'ag_s5_w_in')
    g_s5_out = _gather_weight(s5_w_out, 'ag_s5_w_out')
    g_up = _gather_weight(mlp_w_up, 'ag_mlp_w_up')
    g_down = _gather_weight(mlp_w_down, 'ag_mlp_w_down')
    sharded_small = [gla_w_gate_up, s5_norm, s5_d]
    gathered_small = _exchange(_pack(sharded_small), 'xy', 'bcast', 'ag_small')
    parts = [_unpack(gathered_small[k], [a.shape for a in sharded_small]) for k in range(4)]
    wgu_full = jnp.concatenate([p[0] for p in parts], axis=2)
    s5_norm_full = jnp.concatenate([p[1] for p in parts], axis=1)
    s5_d_full = jnp.concatenate([p[2] for p in parts], axis=1)

    def gla_w_in_padded(j):
        nh = g_w_in.shape[2]
        wj = g_w_in[j // nh, :, j % nh]
        wj = wj.transpose(1, 0, 2).reshape(dm, in_w)
        return jnp.pad(wj, ((0, 0), (0, pw - in_w)))

    grads = {n: [None] * weights[n].shape[0] for n in names if n != 'final_norm'}

    saved = []
    h = h0
    for i in range(depth):
        j = i // 2
        rec = {}
        if i % 2 == 0:
            rec['h_in'] = h
            hn = _norm_fwd(h, gla_norm[j:j + 1], 'gla_norm_fwd')
            w_in_pad = gla_w_in_padded(j)
            proj = _mm('gla_proj', 'nn', _plain(hn), _plain(w_in_pad), (seq, pw, dm),
                       [((seq, pw), F32, _plain_shape(None))], (tm, _tile(pw, 1024), dm))[0]
            wgu_pad = jnp.pad(wgu_full[j], ((0, LANES - rank), (0, 0)))
            gated, states = _gla_scan_fwd(proj, wgu_pad, gla_b_gate[j:j + 1], gla_o_norm[j:j + 1],
                                          heads, kw, vw, tb, 'gla_scan_fwd')
            h = _mm('gla_out', 'nn', _plain(gated), _w_rows(g_gla_out, j), (seq, dm, vw),
                    [((seq, dm), F32, _plain_shape(None))],
                    (tm, _tile(dm, 1024), _tile(g_gla_out.shape[3], 1024)),
                    epilogue=lambda acc, hv: (acc + hv,), extras=[_plain(h)])[0]
            rec.update(hn=hn, w_in_pad=w_in_pad, proj=proj, wgu_pad=wgu_pad, gated=gated, states=states)
        else:
            hp = _permute(h)
            rec['h_in'] = hp
            hn = _norm_fwd(hp, s5_norm_full[j:j + 1], 's5_norm_fwd')
            u = _mm('s5_in', 'nn', _plain(hn), _w_rows(g_s5_in, j), (seq, s5w, dm),
                    [((seq, s5w), F32, _plain_shape(None))],
                    (tm, _tile(s5w, 1024), _tile(g_s5_in.shape[3], 1024)))[0]
            disc, disc_vjp = jax.vjp(_s5_discretise, s5_lam_re[j], s5_lam_im[j], s5_log_dt[j], s5_b_re[j], s5_b_im[j])
            ab_re, ab_im, bb_re, bb_im = disc
            bre = _to_blocks(bb_re.transpose(0, 2, 1))
            bim = _to_blocks(bb_im.transpose(0, 2, 1))
            cre = _to_blocks(s5_c_re[j].transpose(0, 2, 1))
            cim = _to_blocks(s5_c_im[j].transpose(0, 2, 1))
            nb = n_grp // S5_GROUPS_PER_BLOCK
            are = ab_re.reshape(nb, 1, S5_GROUPS_PER_BLOCK * n_state)
            aim = ab_im.reshape(nb, 1, S5_GROUPS_PER_BLOCK * n_state)
            dskip = s5_d_full[j:j + 1]
            y, z, xr, xi = _s5_fwd(u, bre, bim, cre, cim, are, aim, dskip, 's5_scan_fwd')
            o = _mm('s5_out', 'nn', _plain(z), _w_cols(g_s5_out, j), (seq, 2 * dm, s5w),
                    [((seq, 2 * dm), F32, _plain_shape(None))],
                    (tm, _tile(g_s5_out.shape[4], 1024), _tile(s5w, 1024)))[0]
            h = _unpermute(_glu_fwd(o, hp, 's5_glu_fwd'))
            rec.update(hn=hn, u=u, y=y, z=z, xr=xr, xi=xi, o=o, mats=(bre, bim, cre, cim, are, aim, dskip),
                       disc_vjp=disc_vjp)
        rec['h_mid'] = h
        hn2 = _norm_fwd(h, mlp_norm[i:i + 1], 'mlp_norm_fwd')
        act, act2 = _mm('mlp_up', 'nn', _plain(hn2), _w_cols(g_up, i), (seq, hid, dm),
                        [((seq, hid), BF16, _plain_shape(None))] * 2,
                        (tm, _tile(g_up.shape[4], 1024), dm),
                        epilogue=lambda acc: (jnp.maximum(acc, 0.0), jnp.square(jnp.maximum(acc, 0.0))))
        h = _mm('mlp_down', 'nn', _plain(act2), _w_rows(g_down, i), (seq, dm, hid),
                [((seq, dm), F32, _plain_shape(None))],
                (tm, _tile(dm, 1024), _tile(g_down.shape[3], 2048)),
                epilogue=lambda acc, hv: (acc + hv,), extras=[_plain(h)])[0]
        rec.update(hn2=hn2, act=act, act2=act2)
        saved.append(rec)

    dh, loss_cols, d_final = _loss_head(h, target, final_norm.reshape(1, dm), 'loss_head')
    loss = lax.psum(jnp.sum(loss_cols), ('x', 'y', 'c'))
    grads['final_norm'] = [d_final.reshape(dm)]

    big_grads = {n: [None] * weights[n].shape[0] for n in big}
    for i in reversed(range(depth)):
        j = i // 2
        rec = saved[i]
        r_dn, c_dn = mlp_w_down.shape[1:]
        shape, spec = _dw_rows(r_dn, c_dn)
        dw = _mm('mlp_down_dw', 'tn', _plain(rec['act2']), _plain(dh), (hid, dm, seq),
                 [(shape, BF16, spec)], (_tile(r_dn // 2, 1024), _tile(c_dn, 1024), seq))[0]
        big_grads['mlp_w_down'][i] = _reduce_scatter(dw, 'rs_mlp_down')
        dpre = _mm('mlp_down_dx', 'nt', _plain(dh), _w_rows(g_down, i), (seq, hid, dm),
                   [((seq, hid), BF16, _plain_shape(None))],
                   (tm, _tile(g_down.shape[3], 1024), dm),
                   epilogue=lambda acc, av: (acc * (2.0 * av.astype(F32)),), extras=[_plain(rec['act'])])[0]
        r_up, c_up = mlp_w_up.shape[1:]
        shape, spec = _dw_cols(r_up, c_up)
        dw = _mm('mlp_up_dw', 'tn', _plain(rec['hn2']), _plain(dpre), (dm, hid, seq),
                 [(shape, BF16, spec)], (_tile(r_up // 2, 1024), _tile(c_up, 1024), seq))[0]
        big_grads['mlp_w_up'][i] = _reduce_scatter(dw, 'rs_mlp_up')
        dhn = _mm('mlp_up_dx', 'nt', _plain(dpre), _w_cols(g_up, i), (seq, dm, hid),
                  [((seq, dm), F32, _plain_shape(None))],
                  (tm, _tile(dm, 1024), _tile(g_up.shape[4], 2048)))[0]
        dh, dg = _norm_bwd(rec['h_mid'], dhn, dh, mlp_norm[i:i + 1], 'mlp_norm_bwd')
        grads['mlp_norm'][i] = dg[0]

        if i % 2 == 0:
            r_o, c_o = gla_w_out.shape[1:]
            shape, spec = _dw_rows(r_o, c_o)
            dw = _mm('gla_out_dw', 'tn', _plain(rec['gated']), _plain(dh), (vw, dm, seq),
                     [(shape, BF16, spec)], (_tile(r_o // 2, 1024), _tile(c_o, 1024), seq))[0]
            big_grads['gla_w_out'][j] = _reduce_scatter(dw, 'rs_gla_out')
            dgated = _mm('gla_out_dx', 'nt', _plain(dh), _w_rows(g_gla_out, j), (seq, vw, dm),
                         [((seq, vw), F32, _plain_shape(None))],
                         (tm, _tile(g_gla_out.shape[3], 1024), dm))[0]
            dq, dkk, dvv, dr, dpre_g, db, don = _gla_scan_bwd(
                rec['proj'], rec['wgu_pad'], gla_b_gate[j:j + 1], gla_o_norm[j:j + 1], rec['states'], dgated,
                heads, kw, vw, tb, 'gla_scan_bwd')
            grads['gla_b_gate'][j] = db[0]
            grads['gla_o_norm'][j] = don[0]
            dgl = _mm('gla_gate_dx', 'nt', _plain(dpre_g), _plain(rec['wgu_pad']), (seq, LANES, kw),
                      [((seq, LANES), BF16, _plain_shape(None))], (tm, LANES, kw))[0]
            g_low = rec['proj'][:, pw - LANES:]
            dwgu = _mm('gla_gate_dw', 'tn', _plain(g_low), _plain(dpre_g), (LANES, kw, seq),
                       [((LANES, kw), F32, _plain_shape(None))], (LANES, kw, seq))[0]
            grads['gla_w_gate_up'][j] = dwgu[:rank]
            dproj = jnp.concatenate([dq, dkk, dvv, dr, dgl], axis=1)
            dw_pad = _mm('gla_proj_dw', 'tn', _plain(rec['hn']), _plain(dproj), (dm, pw, seq),
                         [((dm, pw), BF16, _plain_shape(None))], (_tile(dm, 1024), _tile(pw, 1024), seq))[0]
            shard_w = in_w // 4
            dw = dw_pad[:, :in_w].reshape(2, dm // 2, 4, shard_w).transpose(0, 2, 1, 3)
            big_grads['gla_w_in'][j] = _reduce_scatter(dw, 'rs_gla_in')
            dhn = _mm('gla_proj_dx', 'nt', _plain(dproj), _plain(rec['w_in_pad']), (seq, dm, pw),
                      [((seq, dm), F32, _plain_shape(None))], (tm, _tile(dm, 1024), _tile(pw, 1024)))[0]
            dh, dg = _norm_bwd(rec['h_in'], dhn, dh, gla_norm[j:j + 1], 'gla_norm_bwd')
            grads['gla_norm'][j] = dg[0]
        else:
            dhp = _permute(dh)
            do = _glu_bwd(rec['o'], dhp, 's5_glu_bwd')
            r_o, c_o = s5_w_out.shape[1:]
            shape, spec = _dw_cols(r_o, c_o)
            dw = _mm('s5_out_dw', 'tn', _plain(rec['z']), _plain(do), (s5w, 2 * dm, seq),
                     [(shape, BF16, spec)], (_tile(r_o // 2, 1024), _tile(c_o, 1024), seq))[0]
            big_grads['s5_w_out'][j] = _reduce_scatter(dw, 'rs_s5_out')
            dz = _mm('s5_out_dx', 'nt', _plain(do), _w_cols(g_s5_out, j), (seq, s5w, 2 * dm),
                     [((seq, s5w), F32, _plain_shape(None))],
                     (tm, _tile(s5w, 1024), _tile(g_s5_out.shape[4], 1024)))[0]
            bre, bim, cre, cim, are, aim, dskip = rec['mats']
            du, dcr, dci, dbr, dbi, dar, dai, dd = _s5_bwd(dz, rec['y'], rec['u'], rec['xr'], rec['xi'],
                                                           bre, bim, cre, cim, are, aim, dskip, 's5_scan_bwd')
            grads['s5_c_re'][j] = _from_blocks(dcr, grp, n_state)
            grads['s5_c_im'][j] = _from_blocks(dci, grp, n_state)
            dbb_re = _from_blocks(dbr, grp, n_state).transpose(0, 2, 1)
            dbb_im = _from_blocks(dbi, grp, n_state).transpose(0, 2, 1)
            d_lr, d_li, d_dt, d_bre, d_bim = rec['disc_vjp'](
                (dar.reshape(n_grp, n_state), dai.reshape(n_grp, n_state), dbb_re, dbb_im))
            grads['s5_lam_re'][j] = d_lr
            grads['s5_lam_im'][j] = d_li
            grads['s5_log_dt'][j] = d_dt
            grads['s5_b_re'][j] = d_bre
            grads['s5_b_im'][j] = d_bim
            grads['s5_d'][j] = dd[0]
            r_i, c_i = s5_w_in.shape[1:]
            shape, spec = _dw_rows(r_i, c_i)
            dw = _mm('s5_in_dw', 'tn', _plain(rec['hn']), _plain(du), (dm, s5w, seq),
                     [(shape, BF16, spec)], (_tile(r_i // 2, 1024), _tile(c_i, 1024), seq))[0]
            big_grads['s5_w_in'][j] = _reduce_scatter(dw, 'rs_s5_in')
            dhn = _mm('s5_in_dx', 'nt', _plain(du), _w_rows(g_s5_in, j), (seq, dm, s5w),
                      [((seq, dm), F32, _plain_shape(None))],
                      (tm, _tile(g_s5_in.shape[3], 1024), _tile(s5w, 1024)))[0]
            dhp, dg = _norm_bwd(rec['h_in'], dhn, dhp, s5_norm_full[j:j + 1], 's5_norm_bwd')
            dh = _unpermute(dhp)
            grads['s5_norm'][j] = dg[0]
    grad_x = dh[None]

    local_small = [jnp.stack(grads[n]) if n != 'final_norm' else grads[n][0] for n in small]
    full_shapes = [a.shape for a in local_small]
    gathered = _exchange(_exchange(_pack(local_small), 'xy', 'bcast', 'ar_small_xy'), 'c', 'bcast', 'ar_small_c')
    rows = gathered.shape[2]
    summed = _sum_slots(gathered.reshape(8, rows, LANES), F32, 'ar_small_sum')
    small_full = dict(zip(small, _unpack(summed, full_shapes)))
    small_grad = {}
    for n in small:
        g = small_full[n]
        if g.shape != weights[n].shape:
            ax = [a for a in range(g.ndim) if g.shape[a] != weights[n].shape[a]][0]
            g = lax.dynamic_slice_in_dim(g, chip * weights[n].shape[ax], weights[n].shape[ax], axis=ax)
        small_grad[n] = g

    out_g, out_d, out_m, out_v = {}, {}, {}, {}
    for n in big:
        out_g[n], out_d[n], out_m[n], out_v[n] = _adamw(weights[n], mom1[n], mom2[n], big_grads[n], 'adamw_' + n)
    shapes = [weights[n].shape for n in small]
    pw_, pm_, pv_, pg_ = (_pack([d[n] for n in small]) for d in (weights, mom1, mom2, small_grad))
    _, sd, sm, sv = _adamw(pw_[None], pm_[None], pv_[None], [pg_], 'adamw_small')
    for n, d_, m_, v_ in zip(small, _unpack(sd[0], shapes), _unpack(sm[0], shapes), _unpack(sv[0], shapes)):
        out_g[n], out_d[n], out_m[n], out_v[n] = small_grad[n], d_, m_, v_

    return (loss, grad_x, *[out_g[n] for n in names], *[out_d[n] for n in names],
            *[out_m[n] for n in names], *[out_v[n] for n in names])
```

```python
import functools
import math

import jax
import jax.numpy as jnp
from jax import lax
from jax.experimental import pallas as pl
from jax.experimental.pallas import tpu as pltpu

F32 = jnp.float32
BF16 = jnp.bfloat16

EPS = 1e-6
CHUNK = 64
GLA_GATE_TEMP = 16.0
S5_EIG_CLIP = -1e-4
S5_SEGMENTS = 8
S5_GROUPS_PER_BLOCK = 8
LANES = 128
ADAM_LR = 0.001
ADAM_B1 = 0.9
ADAM_B2 = 0.999
ADAM_EPS = 1e-08
ADAM_WD = 0.01
ADAM_STEP = 10
VMEM_LIMIT_BYTES = 56 * 1024 * 1024
PAIR_PIECE_BYTES = 2 * 1024 * 1024
PAIR_VMEM_BYTES = 40 * 1024 * 1024

MESH = pl.DeviceIdType.MESH
ANY = pl.BlockSpec(memory_space=pl.ANY)
IN_VMEM = pl.BlockSpec(memory_space=pltpu.VMEM)
IN_HBM = pl.BlockSpec(memory_space=pltpu.HBM)
IN_SEM = pl.BlockSpec(memory_space=pltpu.SEMAPHORE)
DATAFLOW = pltpu.SideEffectType.DATAFLOW_SIDE_EFFECTING


def _pcall(body, **kw):
    return pl.pallas_call(body, **kw)


def _params(*sem):
    return pltpu.CompilerParams(dimension_semantics=sem, vmem_limit_bytes=VMEM_LIMIT_BYTES)


def _tile(dim, target, unit=LANES):
    if dim <= target:
        return dim
    best = None
    for t in range(unit, target + 1, unit):
        if dim % t == 0:
            best = t
    assert best is not None, (dim, target)
    return best


def _exchange(x, group, mode, name):
    n = 2 if group == 'c' else 4
    blk = x.shape if mode == 'bcast' else x.shape[1:]
    if mode == 'a2a':
        assert x.shape[0] == n
    flips = [(0, 0, 1)] if group == 'c' else [(1, 0, 0), (0, 1, 0), (1, 1, 0)]
    itemsize = jnp.dtype(x.dtype).itemsize
    staged = group == 'c' and (x.size + n * math.prod(blk)) * itemsize <= PAIR_VMEM_BYTES
    if group == 'c' and not staged:
        ic = lax.axis_index('c')
        own = x if mode == 'bcast' else lax.dynamic_index_in_dim(x, ic, 0, keepdims=False)
        return lax.dynamic_update_index_in_dim(_pair_exchange_chunked(x, mode, name), own, ic, 0)

    def body(x_ref, y_ref, send_sems, recv_sems, local_sem):
        ix, iy, ic = lax.axis_index('x'), lax.axis_index('y'), lax.axis_index('c')

        def slot(px, py, pc):
            return pc if group == 'c' else 2 * px + py

        def src(px, py, pc):
            if mode == 'a2a':
                return x_ref.at[slot(px, py, pc)]
            if mode == 'bcast_c':
                return x_ref.at[ic]
            return x_ref

        me = (ix, iy, ic)
        local = pltpu.make_async_copy(src(*me), y_ref.at[slot(*me)], local_sem)
        local.start()
        peers = []
        for fx, fy, fc in flips:
            peers.append((1 - ix if fx else ix, 1 - iy if fy else iy, 1 - ic if fc else ic))
        sends = []
        for k, peer in enumerate(peers):
            cp = pltpu.make_async_remote_copy(
                src_ref=src(*peer), dst_ref=y_ref.at[slot(*me)],
                send_sem=send_sems.at[k], recv_sem=recv_sems.at[k],
                device_id=peer, device_id_type=MESH)
            cp.start()
            sends.append(cp)
        for k, peer in enumerate(peers):
            pltpu.make_async_remote_copy(
                src_ref=src(*peer), dst_ref=y_ref.at[slot(*peer)],
                send_sem=send_sems.at[k], recv_sem=recv_sems.at[k],
                device_id=peer, device_id_type=MESH).wait_recv()
        for cp in sends:
            cp.wait_send()
        local.wait()

    return _pcall(
        body, name=name,
        out_shape=jax.ShapeDtypeStruct((n,) + tuple(blk), x.dtype),
        in_specs=[IN_VMEM if staged else ANY], out_specs=IN_VMEM if staged else ANY,
        scratch_shapes=[pltpu.SemaphoreType.DMA((len(flips),)),
                        pltpu.SemaphoreType.DMA((len(flips),)),
                        pltpu.SemaphoreType.DMA(())],
        compiler_params=pltpu.CompilerParams(vmem_limit_bytes=VMEM_LIMIT_BYTES),
    )(x)


def _split_axis(blk, dtype, piece_bytes):
    itemsize = jnp.dtype(dtype).itemsize
    sublanes = 8 * 4 // itemsize
    want = max(1, math.prod(blk) * itemsize // piece_bytes)
    for pieces in [s for s in (64, 32, 16, 8, 4, 2) if s <= want]:
        for ax in range(len(blk) - 1):
            unit = sublanes if ax == len(blk) - 2 else 1
            if blk[ax] % (pieces * unit) == 0:
                return ax, pieces
    return 0, 1


def _pair_exchange_chunked(x, mode, name):
    blk = x.shape if mode == 'bcast' else x.shape[1:]
    ax, pieces = _split_axis(blk, x.dtype, PAIR_PIECE_BYTES)
    step = blk[ax] // pieces
    piece_shape = tuple(blk[:ax]) + (step,) + tuple(blk[ax + 1:])

    def piece(ref, p):
        return ref.at[(slice(None),) * ax + (pl.ds(p * step, step),)]

    def body(x_ref, y_ref, out_buf, in_buf, send_sems, recv_sems, stage_sems, drain_sems, credit_sem):
        ix, iy, ic = lax.axis_index('x'), lax.axis_index('y'), lax.axis_index('c')
        sibling = (ix, iy, 1 - ic)
        mine = x_ref.at[ic] if mode != 'bcast' else x_ref
        theirs = x_ref.at[1 - ic] if mode == 'a2a' else mine

        def stage(p):
            return pltpu.make_async_copy(piece(theirs, p), out_buf.at[p % 2], stage_sems.at[p % 2])

        def remote(p):
            return pltpu.make_async_remote_copy(
                src_ref=out_buf.at[p % 2], dst_ref=in_buf.at[p % 2],
                send_sem=send_sems.at[p], recv_sem=recv_sems.at[p],
                device_id=sibling, device_id_type=MESH)

        def drain(p):
            return pltpu.make_async_copy(in_buf.at[p % 2], piece(y_ref.at[1 - ic], p), drain_sems.at[p % 2])

        stage(0).start()
        for p in range(pieces):
            stage(p).wait()
            if p >= 2:
                pl.semaphore_wait(credit_sem, 1)
            remote(p).start()
            if p + 1 < pieces:
                if p >= 1:
                    remote(p - 1).wait_send()
                stage(p + 1).start()
            remote(p).wait_recv()
            drain(p).start()
            drain(p).wait()
            if p + 2 < pieces:
                pl.semaphore_signal(credit_sem, inc=1, device_id=sibling, device_id_type=MESH)
        for p in range(max(0, pieces - 2), pieces):
            remote(p).wait_send()

    return _pcall(
        body, name=name,
        out_shape=jax.ShapeDtypeStruct((2,) + tuple(blk), x.dtype),
        in_specs=[ANY], out_specs=ANY,
        scratch_shapes=[pltpu.VMEM((2,) + piece_shape, x.dtype), pltpu.VMEM((2,) + piece_shape, x.dtype),
                        pltpu.SemaphoreType.DMA((pieces,)), pltpu.SemaphoreType.DMA((pieces,)),
                        pltpu.SemaphoreType.DMA((2,)), pltpu.SemaphoreType.DMA((2,)),
                        pltpu.SemaphoreType.REGULAR],
        compiler_params=pltpu.CompilerParams(vmem_limit_bytes=VMEM_LIMIT_BYTES),
    )(x)


_XY_FLIPS = [(1, 0), (0, 1), (1, 1)]


def _xy_copies(mode, x_ref, land_ref, sems):
    ix, iy, ic = lax.axis_index('x'), lax.axis_index('y'), lax.axis_index('c')
    out = []
    for k, (fx, fy) in enumerate(_XY_FLIPS):
        px, py = (1 - ix if fx else ix), (1 - iy if fy else iy)
        if mode == 'a2a':
            src = x_ref.at[2 * px + py]
        elif mode == 'bcast_c':
            src = x_ref.at[ic]
        else:
            src = x_ref
        mk = lambda dst, src=src, k=k, px=px, py=py: pltpu.make_async_remote_copy(
            src_ref=src, dst_ref=dst, send_sem=sems[k], recv_sem=sems[3 + k],
            device_id=(px, py, ic), device_id_type=MESH)
        out.append((mk(land_ref.at[2 * ix + iy]), mk(land_ref.at[2 * px + py])))
    return out


def _xy_start(x, mode, name, after):
    blk = x.shape if mode == 'bcast' else x.shape[1:]
    land_shape = (4,) + tuple(blk)
    n_after = len(after)

    def body(*refs):
        x_ref, land_ref = refs[0], refs[1]
        sems = refs[2 + n_after:8 + n_after]
        for send, _ in _xy_copies(mode, x_ref, land_ref, sems):
            send.start()
        refs[-1][...] = jnp.zeros_like(refs[-1])

    outs = _pcall(
        body, name=name,
        out_shape=(pltpu.SemaphoreType.DMA(()),) * 6
        + (pltpu.HBM(x.shape, x.dtype), pltpu.HBM(land_shape, x.dtype), jax.ShapeDtypeStruct((8, LANES), F32)),
        in_specs=(IN_HBM, IN_HBM) + (ANY,) * n_after,
        out_specs=(IN_SEM,) * 6 + (IN_HBM, IN_HBM, IN_VMEM),
        input_output_aliases={0: 6, 1: 7},
        compiler_params=pltpu.CompilerParams(has_side_effects=DATAFLOW),
    )(pltpu.with_memory_space_constraint(x, pltpu.HBM),
      pltpu.with_memory_space_constraint(lax.empty(land_shape, x.dtype), pltpu.HBM), *after)
    return dict(sems=outs[:6], sent=outs[6], land=outs[7], token=outs[8], mode=mode)


def _xy_wait(handle, name, after):
    mode = handle['mode']
    n_after = len(after)

    def body(*refs):
        x_ref, land_ref = refs[0], refs[1]
        for _, recv in _xy_copies(mode, x_ref, land_ref, refs[2:8]):
            recv.wait_send()
            recv.wait_recv()

    sent, land = handle['sent'], handle['land']
    return _pcall(
        body, name=name,
        out_shape=(pltpu.HBM(sent.shape, sent.dtype), pltpu.HBM(land.shape, land.dtype)),
        in_specs=(IN_HBM, IN_HBM) + (IN_SEM,) * 6 + (ANY,) * n_after,
        out_specs=(IN_HBM, IN_HBM),
        input_output_aliases={0: 0, 1: 1},
        compiler_params=pltpu.CompilerParams(has_side_effects=DATAFLOW),
    )(sent, land, *handle['sems'], *after)


def _sum_slots(y, out_dtype, name):
    n, rows, cols = y.shape
    tm = _tile(rows, max(8, (1 << 20) // (n * cols) // 8 * 8), unit=8)

    def body(y_ref, o_ref):
        acc = y_ref[0].astype(F32)
        for k in range(1, n):
            acc = acc + y_ref[k].astype(F32)
        o_ref[...] = acc.astype(o_ref.dtype)

    return _pcall(
        body, name=name, grid=(rows // tm,),
        out_shape=jax.ShapeDtypeStruct((rows, cols), out_dtype),
        in_specs=[pl.BlockSpec((n, tm, cols), lambda i: (0, i, 0))],
        out_specs=pl.BlockSpec((tm, cols), lambda i: (i, 0)),
        compiler_params=_params('parallel'),
    )(y)


class _Op:
    def __init__(self, arr, spec):
        self.arr = arr
        self.spec = spec


def _plain(arr):
    return _Op(arr, lambda t0, t1: ((t0, t1), lambda b0, b1: (b0, b1)))


def _plain_shape(shape):
    return lambda t0, t1: ((t0, t1), lambda b0, b1: (b0, b1))


def _dw_cols(rows, cols):
    rh = rows // 2

    def spec(t0, t1):
        assert rh % t0 == 0 and cols % t1 == 0, (rh, cols, t0, t1)
        qr, qc = rh // t0, cols // t1
        return (None, None, t0, t1), lambda b0, b1: (b0 // qr, b1 // qc, b0 % qr, b1 % qc)
    return (2, 4, rh, cols), spec


def _dw_rows(rows, cols):
    rh = rows // 2

    def spec(t0, t1):
        assert rh % t0 == 0 and cols % t1 == 0, (rh, cols, t0, t1)
        qr = rh // t0
        return (None, None, t0, t1), lambda b0, b1: ((b0 // qr) % 2, b0 // (2 * qr), b0 % qr, b1)
    return (2, 4, rh, cols), spec


def _w_cols(g, j):
    _, _, rh, cols = g[j].shape
    return _Op(g[j], _dw_cols(2 * rh, cols)[1])


def _w_rows(g, j):
    _, _, rh, cols = g[j].shape
    return _Op(g[j], _dw_rows(2 * rh, cols)[1])


def _mm(name, mode, a, b, dims, outs, tiles, epilogue=None, extras=(), after=()):
    m, n, k = dims
    tm, tn, tk = tiles
    assert m % tm == 0 and n % tn == 0 and k % tk == 0, (name, dims, tiles)
    nk = k // tk
    if mode == 'nn':
        a_t, a_ix, b_t, b_ix, ca, cb = (tm, tk), (lambda i, j, kk: (i, kk)), (tk, tn), (lambda i, j, kk: (kk, j)), 1, 0
    elif mode == 'nt':
        a_t, a_ix, b_t, b_ix, ca, cb = (tm, tk), (lambda i, j, kk: (i, kk)), (tn, tk), (lambda i, j, kk: (j, kk)), 1, 1
    else:
        a_t, a_ix, b_t, b_ix, ca, cb = (tk, tm), (lambda i, j, kk: (kk, i)), (tk, tn), (lambda i, j, kk: (kk, j)), 0, 0
    a_blk, a_fn = a.spec(*a_t)
    b_blk, b_fn = b.spec(*b_t)
    in_specs = [pl.BlockSpec(a_blk, lambda i, j, kk: a_fn(*a_ix(i, j, kk))),
                pl.BlockSpec(b_blk, lambda i, j, kk: b_fn(*b_ix(i, j, kk)))]
    operands = [a.arr, b.arr]
    for e in extras:
        e_blk, e_fn = e.spec(tm, tn)
        in_specs.append(pl.BlockSpec(e_blk, functools.partial(lambda i, j, kk, f: f(i, j), f=e_fn)))
        operands.append(e.arr)
    out_shapes, out_specs = [], []
    for shape, dtype, spec in outs:
        o_blk, o_fn = spec(tm, tn)
        out_shapes.append(jax.ShapeDtypeStruct(shape, dtype))
        out_specs.append(pl.BlockSpec(o_blk, functools.partial(lambda i, j, kk, f: f(i, j), f=o_fn)))
    n_ex, n_out = len(extras), len(outs)
    in_specs += [ANY] * len(after)
    operands += list(after)
    if epilogue is None:
        epilogue = lambda acc: (acc,)

    def body(a_ref, b_ref, *rest):
        ex_refs = rest[:n_ex]
        rest = rest[:n_ex] + rest[n_ex + len(after):]
        out_refs = rest[n_ex:n_ex + n_out]
        p = lax.dot_general(a_ref[...].astype(BF16), b_ref[...].astype(BF16),
                            (((ca,), (cb,)), ((), ())), preferred_element_type=F32)

        def finish(acc):
            res = epilogue(acc, *[r[...] for r in ex_refs])
            for o_ref, val in zip(out_refs, res):
                o_ref[...] = val.astype(o_ref.dtype)

        if nk == 1:
            finish(p)
        else:
            acc_ref = rest[n_ex + n_out]
            kk = pl.program_id(2)

            @pl.when(kk == 0)
            def _():
                acc_ref[...] = p

            @pl.when(kk > 0)
            def _():
                acc_ref[...] += p

            @pl.when(kk == nk - 1)
            def _():
                finish(acc_ref[...])

    res = _pcall(
        body, name=name, grid=(m // tm, n // tn, nk),
        out_shape=out_shapes, in_specs=in_specs, out_specs=out_specs,
        scratch_shapes=[pltpu.VMEM((tm, tn), F32)] if nk > 1 else [],
        compiler_params=_params('parallel', 'parallel', 'arbitrary'),
    )(*operands)
    return res


def _rowwise(name, fn, row_ins, vec_ins, outs, reds, tm, after=()):
    rows = row_ins[0].shape[0]
    assert rows % tm == 0
    n_in = len(row_ins) + len(vec_ins)
    n_out = len(outs)

    def body(*refs):
        vals = [r[...] for r in refs[:n_in]]
        refs = refs[:n_in] + refs[n_in + len(after):]
        res = fn(*vals)
        for o_ref, val in zip(refs[n_in:n_in + n_out], res[:n_out]):
            o_ref[...] = val.astype(o_ref.dtype)
        first = pl.program_id(0) == 0
        for r_ref, val in zip(refs[n_in + n_out:], res[n_out:]):
            @pl.when(first)
            def _(r_ref=r_ref, val=val):
                r_ref[...] = val

            @pl.when(jnp.logical_not(first))
            def _(r_ref=r_ref, val=val):
                r_ref[...] += val

    in_specs = [pl.BlockSpec((tm, a.shape[1]), lambda i: (i, 0)) for a in row_ins]
    in_specs += [pl.BlockSpec((1, v.shape[1]), lambda i: (0, 0)) for v in vec_ins]
    in_specs += [ANY] * len(after)
    out_shapes = [jax.ShapeDtypeStruct((rows, w), dt) for w, dt in outs]
    out_shapes += [jax.ShapeDtypeStruct((1, w), F32) for w in reds]
    out_specs = [pl.BlockSpec((tm, w), lambda i: (i, 0)) for w, _ in outs]
    out_specs += [pl.BlockSpec((1, w), lambda i: (0, 0)) for w in reds]
    return _pcall(
        body, name=name, grid=(rows // tm,),
        out_shape=out_shapes, in_specs=in_specs, out_specs=out_specs,
        compiler_params=_params('arbitrary'),
    )(*row_ins, *vec_ins, *after)


def _norm_fwd(h, g, name, after=()):
    def fn(hv, gv):
        rstd = lax.rsqrt(jnp.mean(hv * hv, axis=-1, keepdims=True) + EPS)
        return (hv * rstd * gv,)
    return _rowwise(name, fn, [h], [g], [(h.shape[1], BF16)], [], 256, after)[0]


def _norm_bwd(h, dhn, dres, g, name):
    def fn(hv, dv, rv, gv):
        rstd = lax.rsqrt(jnp.mean(hv * hv, axis=-1, keepdims=True) + EPS)
        xhat = hv * rstd
        dxhat = dv * gv
        dh = rv + rstd * (dxhat - xhat * jnp.mean(dxhat * xhat, axis=-1, keepdims=True))
        return dh, jnp.sum(dv * xhat, axis=0, keepdims=True)
    w = h.shape[1]
    return _rowwise(name, fn, [h, dhn, dres], [g], [(w, F32)], [w], 256)


def _loss_head(h, target, g, name):
    w = h.shape[1]

    def fn(hv, tv, gv):
        rstd = lax.rsqrt(jnp.mean(hv * hv, axis=-1, keepdims=True) + EPS)
        xhat = hv * rstd
        diff = xhat * gv - tv
        dy = diff * (1.0 / w)
        dxhat = dy * gv
        dh = rstd * (dxhat - xhat * jnp.mean(dxhat * xhat, axis=-1, keepdims=True))
        return (dh, jnp.sum(0.5 * dy * diff, axis=0, keepdims=True),
                jnp.sum(dy * xhat, axis=0, keepdims=True))
    return _rowwise(name, fn, [h, target], [g], [(w, F32)], [w, w], 256)


def _split3(x):
    hi = x.astype(BF16)
    r1 = x - hi.astype(F32)
    mid = r1.astype(BF16)
    lo = (r1 - mid.astype(F32)).astype(BF16)
    return hi, mid, lo


def _tri_dot(tri, x):
    hi, mid, lo = _split3(x)
    d = lambda p: jnp.dot(tri, p, preferred_element_type=F32)
    return d(hi) + d(mid) + d(lo)


def _log_sigmoid(x):
    return jnp.minimum(x, 0.0) - jnp.log(1.0 + jnp.exp(-jnp.abs(x)))


def _gla_dims(proj_w, kw, vw, dk, dv):
    assert kw % dk == 0 and (2 * kw) % dv == 0 and (2 * kw + vw) % dv == 0 and (2 * kw + 2 * vw) % LANES == 0
    return dict(q0=0, k0=kw // dk, v0=2 * kw // dv, r0=(2 * kw + vw) // dv, g0=(2 * kw + 2 * vw) // LANES)


def _gla_gates(gl, wgu, bias):
    pre = jnp.dot(gl.astype(BF16), wgu, preferred_element_type=F32) + bias
    la = _log_sigmoid(pre) * (1.0 / GLA_GATE_TEMP)
    r_i = lax.broadcasted_iota(jnp.int32, (CHUNK, CHUNK), 0)
    c_i = lax.broadcasted_iota(jnp.int32, (CHUNK, CHUNK), 1)
    cum = _tri_dot((c_i <= r_i).astype(BF16), la)
    total = cum[CHUNK - 1:CHUNK, :]
    return pre, cum, total


def _gla_scan_fwd(proj, wgu_pad, b_gate, o_norm, heads, kw, vw, tb, name):
    seq, pw = proj.shape
    dk, dv = kw // heads, vw // heads
    cb = tb // CHUNK
    nt = seq // tb
    o = _gla_dims(pw, kw, vw, dk, dv)
    scale = dk ** -0.5

    def body(q_ref, k_ref, v_ref, r_ref, gl_ref, wgu_ref, b_ref, on_ref, out_ref, st_ref, s_scr):
        @pl.when(pl.program_id(1) == 0)
        def _():
            s_scr[...] = jnp.zeros_like(s_scr)

        wgu = wgu_ref[...].astype(BF16)
        bias = b_ref[...]
        onorm = on_ref[...]
        st = s_scr[...]
        for ci in range(cb):
            rows = pl.ds(ci * CHUNK, CHUNK)
            _, cum, total = _gla_gates(gl_ref[rows, :], wgu, bias)
            kdec = k_ref[rows, :] * jnp.exp(total - cum)
            st = st * jnp.exp(total) + lax.dot_general(
                v_ref[rows, :].astype(BF16), kdec.astype(BF16), (((0,), (0,)), ((), ())),
                preferred_element_type=F32)
            st_ref[ci] = st
            qs = (q_ref[rows, :] * scale).astype(BF16)
            ov = lax.dot_general(qs, st.astype(BF16), (((1,), (1,)), ((), ())), preferred_element_type=F32)
            rstd = lax.rsqrt(jnp.mean(ov * ov, axis=-1, keepdims=True) + EPS)
            rv = r_ref[rows, :]
            out_ref[rows, :] = (ov * rstd * onorm * (rv * jax.nn.sigmoid(rv))).astype(out_ref.dtype)
        s_scr[...] = st

    in_specs = [
        pl.BlockSpec((tb, dk), lambda h, t: (t, o['q0'] + h)),
        pl.BlockSpec((tb, dk), lambda h, t: (t, o['k0'] + h)),
        pl.BlockSpec((tb, dv), lambda h, t: (t, o['v0'] + h)),
        pl.BlockSpec((tb, dv), lambda h, t: (t, o['r0'] + h)),
        pl.BlockSpec((tb, LANES), lambda h, t: (t, o['g0'])),
        pl.BlockSpec((LANES, dk), lambda h, t: (0, h)),
        pl.BlockSpec((1, dk), lambda h, t: (0, h)),
        pl.BlockSpec((1, dv), lambda h, t: (0, 0)),
    ]
    return _pcall(
        body, name=name, grid=(heads, nt),
        out_shape=[jax.ShapeDtypeStruct((seq, vw), BF16),
                   jax.ShapeDtypeStruct((heads, seq // CHUNK, dv, dk), F32)],
        in_specs=in_specs,
        out_specs=[pl.BlockSpec((tb, dv), lambda h, t: (t, h)),
                   pl.BlockSpec((None, cb, dv, dk), lambda h, t: (h, t, 0, 0))],
        scratch_shapes=[pltpu.VMEM((dv, dk), F32)],
        compiler_params=_params('parallel', 'arbitrary'),
    )(proj, proj, proj, proj, proj, wgu_pad, b_gate, o_norm)


def _gla_scan_bwd(proj, wgu_pad, b_gate, o_norm, states, dgated, heads, kw, vw, tb, name):
    seq, pw = proj.shape
    dk, dv = kw // heads, vw // heads
    cb = tb // CHUNK
    nt = seq // tb
    o = _gla_dims(pw, kw, vw, dk, dv)
    scale = dk ** -0.5

    def body(q_ref, k_ref, v_ref, r_ref, gl_ref, wgu_ref, b_ref, on_ref, st_ref, stp_ref, dg_ref,
             dq_ref, dk_ref, dv_ref, dr_ref, dpre_ref, db_ref, don_ref, ds_scr):
        hh = pl.program_id(0)
        t = pl.program_id(1)

        @pl.when(t == 0)
        def _():
            ds_scr[...] = jnp.zeros_like(ds_scr)
            db_ref[...] = jnp.zeros_like(db_ref)

        @pl.when(jnp.logical_and(hh == 0, t == 0))
        def _():
            don_ref[...] = jnp.zeros_like(don_ref)

        wgu = wgu_ref[...].astype(BF16)
        bias = b_ref[...]
        onorm = on_ref[...]
        has_prev = (t < nt - 1).astype(F32)
        r_i = lax.broadcasted_iota(jnp.int32, (CHUNK, CHUNK), 0)
        c_i = lax.broadcasted_iota(jnp.int32, (CHUNK, CHUNK), 1)
        strict = (c_i < r_i).astype(BF16)
        carry = ds_scr[...]
        db_acc = jnp.zeros((1, dk), F32)
        don_acc = jnp.zeros((1, dv), F32)
        for ci in reversed(range(cb)):
            rows = pl.ds(ci * CHUNK, CHUNK)
            pre, cum, total = _gla_gates(gl_ref[rows, :], wgu, bias)
            edec = jnp.exp(total - cum)
            decay = jnp.exp(total)
            kdec = k_ref[rows, :] * edec
            st = st_ref[ci]
            st_prev = st_ref[ci - 1] if ci > 0 else stp_ref[0] * has_prev
            stb = st.astype(BF16)
            qs = (q_ref[rows, :] * scale).astype(BF16)
            vb = v_ref[rows, :].astype(BF16)
            ov = lax.dot_general(qs, stb, (((1,), (1,)), ((), ())), preferred_element_type=F32)
            rstd = lax.rsqrt(jnp.mean(ov * ov, axis=-1, keepdims=True) + EPS)
            ohat = ov * rstd
            rv = r_ref[rows, :]
            sr = jax.nn.sigmoid(rv)
            dgv = dg_ref[rows, :]
            dy = dgv * (rv * sr)
            dr_ref[rows, :] = (dgv * (ohat * onorm) * (sr * (1.0 + rv * (1.0 - sr)))).astype(dr_ref.dtype)
            don_acc = don_acc + jnp.sum(dy * ohat, axis=0, keepdims=True)
            dohat = dy * onorm
            do = (rstd * (dohat - ohat * jnp.mean(dohat * ohat, axis=-1, keepdims=True))).astype(BF16)
            dq_ref[rows, :] = (jnp.dot(do, stb, preferred_element_type=F32) * scale).astype(dq_ref.dtype)
            dst = carry + lax.dot_general(do, qs, (((0,), (0,)), ((), ())), preferred_element_type=F32)
            dstb = dst.astype(BF16)
            dkdec = jnp.dot(vb, dstb, preferred_element_type=F32)
            dv_ref[rows, :] = lax.dot_general(kdec.astype(BF16), dstb, (((1,), (1,)), ((), ())),
                                              preferred_element_type=F32).astype(dv_ref.dtype)
            ddecay = jnp.sum(dst * st_prev, axis=0, keepdims=True)
            dk_ref[rows, :] = (dkdec * edec).astype(dk_ref.dtype)
            da = ddecay * decay + _tri_dot(strict, dkdec * kdec)
            dpre = da * (1.0 / GLA_GATE_TEMP) * (1.0 - jax.nn.sigmoid(pre))
            dpre_ref[rows, :] = dpre.astype(dpre_ref.dtype)
            db_acc = db_acc + jnp.sum(dpre, axis=0, keepdims=True)
            carry = dst * decay
        ds_scr[...] = carry
        db_ref[...] += db_acc
        don_ref[...] += don_acc

    rt = lambda t: nt - 1 - t
    in_specs = [
        pl.BlockSpec((tb, dk), lambda h, t: (rt(t), o['q0'] + h)),
        pl.BlockSpec((tb, dk), lambda h, t: (rt(t), o['k0'] + h)),
        pl.BlockSpec((tb, dv), lambda h, t: (rt(t), o['v0'] + h)),
        pl.BlockSpec((tb, dv), lambda h, t: (rt(t), o['r0'] + h)),
        pl.BlockSpec((tb, LANES), lambda h, t: (rt(t), o['g0'])),
        pl.BlockSpec((LANES, dk), lambda h, t: (0, h)),
        pl.BlockSpec((1, dk), lambda h, t: (0, h)),
        pl.BlockSpec((1, dv), lambda h, t: (0, 0)),
        pl.BlockSpec((None, cb, dv, dk), lambda h, t: (h, rt(t), 0, 0)),
        pl.BlockSpec((None, 1, dv, dk), lambda h, t: (h, jnp.maximum(rt(t) * cb - 1, 0), 0, 0)),
        pl.BlockSpec((tb, dv), lambda h, t: (rt(t), h)),
    ]
    out_shape = [jax.ShapeDtypeStruct((seq, kw), BF16), jax.ShapeDtypeStruct((seq, kw), BF16),
                 jax.ShapeDtypeStruct((seq, vw), BF16), jax.ShapeDtypeStruct((seq, vw), BF16),
                 jax.ShapeDtypeStruct((seq, kw), BF16),
                 jax.ShapeDtypeStruct((1, kw), F32), jax.ShapeDtypeStruct((1, dv), F32)]
    out_specs = [pl.BlockSpec((tb, dk), lambda h, t: (rt(t), h)),
                 pl.BlockSpec((tb, dk), lambda h, t: (rt(t), h)),
                 pl.BlockSpec((tb, dv), lambda h, t: (rt(t), h)),
                 pl.BlockSpec((tb, dv), lambda h, t: (rt(t), h)),
                 pl.BlockSpec((tb, dk), lambda h, t: (rt(t), h)),
                 pl.BlockSpec((1, dk), lambda h, t: (0, h)),
                 pl.BlockSpec((1, dv), lambda h, t: (0, 0))]
    return _pcall(
        body, name=name, grid=(heads, nt),
        out_shape=out_shape, in_specs=in_specs, out_specs=out_specs,
        scratch_shapes=[pltpu.VMEM((dv, dk), F32)],
        compiler_params=_params('arbitrary', 'arbitrary'),
    )(proj, proj, proj, proj, proj, wgu_pad, b_gate, o_norm, states, states, dgated)


def _cmul(ar, ai, br, bi):
    return ar * br - ai * bi, ar * bi + ai * br


def _gelu(y):
    c = math.sqrt(2.0 / math.pi)
    return 0.5 * y * (1.0 + jnp.tanh(c * (y + 0.044715 * y * y * y)))


def _gelu_grad(y):
    c = math.sqrt(2.0 / math.pi)
    th = jnp.tanh(c * (y + 0.044715 * y * y * y))
    return 0.5 * (1.0 + th) + 0.5 * y * (1.0 - th * th) * (c * (1.0 + 3.0 * 0.044715 * y * y))


def _power_pow2(ar, ai, n):
    assert n & (n - 1) == 0
    for _ in range(n.bit_length() - 1):
        ar, ai = _cmul(ar, ai, ar, ai)
    return ar, ai


def _s5_fwd(u, bre, bim, cre, cim, are, aim, dskip, name):
    seq, width = u.shape
    nb, ub, sb = bre.shape
    ls = seq // S5_SEGMENTS
    seg = S5_SEGMENTS

    def body(u_ref, bre_ref, bim_ref, cre_ref, cim_ref, are_ref, aim_ref, d_ref, y_ref, z_ref, xr_ref, xi_ref):
        uv = u_ref[...]
        ub16 = uv.astype(BF16)
        xr_ref[...] = jnp.dot(ub16, bre_ref[...].astype(BF16), preferred_element_type=F32)
        xi_ref[...] = jnp.dot(ub16, bim_ref[...].astype(BF16), preferred_element_type=F32)
        ar = jnp.broadcast_to(are_ref[...], (seg, sb))
        ai = jnp.broadcast_to(aim_ref[...], (seg, sb))

        def step(i, c):
            rows = pl.ds(pl.multiple_of(i * seg, seg), seg)
            pr, pi = _cmul(ar, ai, c[0], c[1])
            nr = pr + xr_ref[rows, :]
            ni = pi + xi_ref[rows, :]
            xr_ref[rows, :] = nr
            xi_ref[rows, :] = ni
            return nr, ni

        zero = jnp.zeros((seg, sb), F32)
        er, ei = lax.fori_loop(0, ls, step, (zero, zero), unroll=8)
        pr, pi = _power_pow2(ar, ai, ls)
        row = lax.broadcasted_iota(jnp.int32, (seg, sb), 0)
        sr, si = zero, zero
        for _ in range(seg - 1):
            tr, ti = _cmul(pr, pi, sr, si)
            sr = jnp.where(row == 0, 0.0, pltpu.roll(tr + er, 1, 0))
            si = jnp.where(row == 0, 0.0, pltpu.roll(ti + ei, 1, 0))

        def fix(i, c):
            rows = pl.ds(pl.multiple_of(i * seg, seg), seg)
            fr, fi = _cmul(c[0], c[1], sr, si)
            xr_ref[rows, :] += fr
            xi_ref[rows, :] += fi
            return _cmul(c[0], c[1], ar, ai)

        lax.fori_loop(0, ls, fix, (ar, ai), unroll=8)
        y = (jnp.dot(xr_ref[...].astype(BF16), cre_ref[...].astype(BF16), preferred_element_type=F32)
             - jnp.dot(xi_ref[...].astype(BF16), cim_ref[...].astype(BF16), preferred_element_type=F32)
             + d_ref[...] * uv)
        y_ref[...] = y
        z_ref[...] = _gelu(y).astype(z_ref.dtype)

    mat = lambda r, c: pl.BlockSpec((None, r, c), lambda b: (b, 0, 0))
    return _pcall(
        body, name=name, grid=(nb,),
        out_shape=[jax.ShapeDtypeStruct((seq, width), F32), jax.ShapeDtypeStruct((seq, width), BF16),
                   jax.ShapeDtypeStruct((seq, nb * sb), F32), jax.ShapeDtypeStruct((seq, nb * sb), F32)],
        in_specs=[pl.BlockSpec((seq, ub), lambda b: (0, b)), mat(ub, sb), mat(ub, sb), mat(sb, ub), mat(sb, ub),
                  mat(1, sb), mat(1, sb), pl.BlockSpec((1, ub), lambda b: (0, b))],
        out_specs=[pl.BlockSpec((seq, ub), lambda b: (0, b)), pl.BlockSpec((seq, ub), lambda b: (0, b)),
                   pl.BlockSpec((seq, sb), lambda b: (0, b)), pl.BlockSpec((seq, sb), lambda b: (0, b))],
        compiler_params=_params('parallel'),
    )(u, bre, bim, cre, cim, are, aim, dskip)


def _s5_bwd(dz, y, u, xr, xi, bre, bim, cre, cim, are, aim, dskip, name):
    seq, width = u.shape
    nb, ub, sb = bre.shape
    ls = seq // S5_SEGMENTS
    seg = S5_SEGMENTS

    def body(dz_ref, y_ref, u_ref, xr_ref, xi_ref, bre_ref, bim_ref, cre_ref, cim_ref, are_ref, aim_ref, d_ref,
             du_ref, dcr_ref, dci_ref, dbr_ref, dbi_ref, dar_ref, dai_ref, dd_ref, lr_ref, li_ref):
        uv = u_ref[...]
        dy = dz_ref[...] * _gelu_grad(y_ref[...])
        dd_ref[...] = jnp.sum(dy * uv, axis=0, keepdims=True)
        dyb = dy.astype(BF16)
        nt = (((1,), (1,)), ((), ()))
        tn = (((0,), (0,)), ((), ()))
        lr_ref[...] = lax.dot_general(dyb, cre_ref[...].astype(BF16), nt, preferred_element_type=F32)
        li_ref[...] = -lax.dot_general(dyb, cim_ref[...].astype(BF16), nt, preferred_element_type=F32)
        dcr_ref[...] = lax.dot_general(dyb, xr_ref[...].astype(BF16), tn, preferred_element_type=F32)
        dci_ref[...] = -lax.dot_general(dyb, xi_ref[...].astype(BF16), tn, preferred_element_type=F32)
        ar = jnp.broadcast_to(are_ref[...], (seg, sb))
        ai = jnp.broadcast_to(aim_ref[...], (seg, sb))
        nai = -ai

        def step(ii, c):
            rows = pl.ds(pl.multiple_of((ls - 1 - ii) * seg, seg), seg)
            pr, pi = _cmul(ar, nai, c[0], c[1])
            nr = pr + lr_ref[rows, :]
            ni = pi + li_ref[rows, :]
            lr_ref[rows, :] = nr
            li_ref[rows, :] = ni
            return nr, ni

        zero = jnp.zeros((seg, sb), F32)
        er, ei = lax.fori_loop(0, ls, step, (zero, zero), unroll=8)
        pr, pi = _power_pow2(ar, nai, ls)
        row = lax.broadcasted_iota(jnp.int32, (seg, sb), 0)
        rr, ri = zero, zero
        for _ in range(seg - 1):
            tr, ti = _cmul(pr, pi, rr, ri)
            rr = jnp.where(row == seg - 1, 0.0, pltpu.roll(tr + er, seg - 1, 0))
            ri = jnp.where(row == seg - 1, 0.0, pltpu.roll(ti + ei, seg - 1, 0))

        def corrected(rows, qr, qi):
            fr, fi = _cmul(qr, qi, rr, ri)
            nr = lr_ref[rows, :] + fr
            ni = li_ref[rows, :] + fi
            lr_ref[rows, :] = nr
            li_ref[rows, :] = ni
            return nr, ni

        def grad_a(nr, ni, xpr, xpi, accr, acci):
            return accr + nr * xpr + ni * xpi, acci + ni * xpr - nr * xpi

        def fix(ii, c):
            qr, qi, accr, acci = c
            i = ls - 1 - ii
            rows = pl.ds(pl.multiple_of(i * seg, seg), seg)
            prev = pl.ds(pl.multiple_of((i - 1) * seg, seg), seg)
            nr, ni = corrected(rows, qr, qi)
            accr, acci = grad_a(nr, ni, xr_ref[prev, :], xi_ref[prev, :], accr, acci)
            qr, qi = _cmul(qr, qi, ar, nai)
            return qr, qi, accr, acci

        qr, qi, accr, acci = lax.fori_loop(0, ls - 1, fix, (ar, nai, zero, zero), unroll=8)
        nr, ni = corrected(pl.ds(0, seg), qr, qi)
        last = pl.ds((ls - 1) * seg, seg)
        xpr = jnp.where(row == 0, 0.0, pltpu.roll(xr_ref[last, :], 1, 0))
        xpi = jnp.where(row == 0, 0.0, pltpu.roll(xi_ref[last, :], 1, 0))
        accr, acci = grad_a(nr, ni, xpr, xpi, accr, acci)
        dar_ref[...] = jnp.sum(accr, axis=0, keepdims=True)
        dai_ref[...] = jnp.sum(acci, axis=0, keepdims=True)
        lrb = lr_ref[...].astype(BF16)
        lib = li_ref[...].astype(BF16)
        ub16 = uv.astype(BF16)
        dbr_ref[...] = lax.dot_general(ub16, lrb, tn, preferred_element_type=F32)
        dbi_ref[...] = lax.dot_general(ub16, lib, tn, preferred_element_type=F32)
        du_ref[...] = (d_ref[...] * dy
                       + lax.dot_general(lrb, bre_ref[...].astype(BF16), nt, preferred_element_type=F32)
                       + lax.dot_general(lib, bim_ref[...].astype(BF16), nt, preferred_element_type=F32))

    mat = lambda r, c: pl.BlockSpec((None, r, c), lambda b: (b, 0, 0))
    col = lambda w: pl.BlockSpec((seq, w), lambda b: (0, b))
    return _pcall(
        body, name=name, grid=(nb,),
        out_shape=[jax.ShapeDtypeStruct((seq, width), F32)]
        + [jax.ShapeDtypeStruct((nb, ub, sb), F32)] * 4
        + [jax.ShapeDtypeStruct((nb, 1, sb), F32)] * 2
        + [jax.ShapeDtypeStruct((1, width), F32)],
        in_specs=[col(ub), col(ub), col(ub), col(sb), col(sb), mat(ub, sb), mat(ub, sb), mat(sb, ub), mat(sb, ub),
                  mat(1, sb), mat(1, sb), pl.BlockSpec((1, ub), lambda b: (0, b))],
        out_specs=[col(ub), mat(ub, sb), mat(ub, sb), mat(ub, sb), mat(ub, sb), mat(1, sb), mat(1, sb),
                   pl.BlockSpec((1, ub), lambda b: (0, b))],
        scratch_shapes=[pltpu.VMEM((seq, sb), F32), pltpu.VMEM((seq, sb), F32)],
        compiler_params=_params('parallel'),
    )(dz, y, u, xr, xi, bre, bim, cre, cim, are, aim, dskip)


def _s5_discretise(lam_re, lam_im, log_dt, b_re, b_im):
    lr = jnp.minimum(lam_re, S5_EIG_CLIP)
    li = lam_im
    dt = jnp.exp(log_dt)[:, None]
    mag = jnp.exp(lr * dt)
    ang = li * dt
    ab_re = mag * jnp.cos(ang)
    ab_im = mag * jnp.sin(ang)
    den = lr * lr + li * li
    nr = ab_re - 1.0
    f_re = (nr * lr + ab_im * li) / den
    f_im = (ab_im * lr - nr * li) / den
    bb_re = f_re[..., None] * b_re - f_im[..., None] * b_im
    bb_im = f_re[..., None] * b_im + f_im[..., None] * b_re
    return ab_re, ab_im, bb_re, bb_im


def _to_blocks(m):
    g, a, b = m.shape
    gb = S5_GROUPS_PER_BLOCK
    eye = jnp.eye(gb, dtype=m.dtype)
    return jnp.einsum('bgac,gh->bgahc', m.reshape(g // gb, gb, a, b), eye).reshape(g // gb, gb * a, gb * b)


def _from_blocks(m, a, b):
    nb = m.shape[0]
    gb = S5_GROUPS_PER_BLOCK
    eye = jnp.eye(gb, dtype=m.dtype)
    return jnp.einsum('bgahc,gh->bgac', m.reshape(nb, gb, a, gb, b), eye).reshape(nb * gb, a, b)


def _glu_fwd(o, h, name):
    half = o.shape[1] // 2

    def fn(ov, hv):
        return (hv + ov[:, :half] * jax.nn.sigmoid(ov[:, half:]),)
    return _rowwise(name, fn, [o, h], [], [(half, F32)], [], 256)[0]


def _glu_bwd(o, dout, name):
    half = o.shape[1] // 2

    def fn(ov, dv):
        val, gate = ov[:, :half], ov[:, half:]
        sg = jax.nn.sigmoid(gate)
        return (jnp.concatenate([dv * sg, dv * val * sg * (1.0 - sg)], axis=1),)
    return _rowwise(name, fn, [o, dout], [], [(2 * half, BF16)], [], 256)[0]


def _adam_math(w, g, m, v):
    m = ADAM_B1 * m + (1.0 - ADAM_B1) * g
    v = ADAM_B2 * v + (1.0 - ADAM_B2) * (g * g)
    m_hat = m / (1.0 - ADAM_B1 ** ADAM_STEP)
    v_hat = v / (1.0 - ADAM_B2 ** ADAM_STEP)
    delta = -ADAM_LR * (m_hat / (jnp.sqrt(v_hat) + ADAM_EPS) + ADAM_WD * w)
    return delta, m, v


def _adamw(w, m, v, grads, name):
    nl, rows, cols = w.shape
    tm = _tile(rows, max(8, (1 << 18) // cols // 8 * 8), unit=8)
    nbk = rows // tm

    def body(*refs):
        w_ref, m_ref, v_ref = refs[:3]
        g_refs = refs[3:3 + nl]
        go_ref, d_ref, mo_ref, vo_ref = refs[3 + nl:]
        layer = pl.program_id(0)
        g = g_refs[0][...]
        for l in range(1, nl):
            g = jnp.where(layer == l, g_refs[l][...], g)
        delta, mn, vn = _adam_math(w_ref[...], g, m_ref[...], v_ref[...])
        go_ref[...] = g
        d_ref[...] = delta
        mo_ref[...] = mn
        vo_ref[...] = vn

    stacked = pl.BlockSpec((None, tm, cols), lambda l, i: (l, i, 0))

    def g_spec(layer):
        return pl.BlockSpec((tm, cols), lambda l, i: (jnp.where(l == layer, i, jnp.where(l < layer, 0, nbk - 1)), 0))

    return _pcall(
        body, name=name, grid=(nl, nbk),
        out_shape=[jax.ShapeDtypeStruct(w.shape, F32)] * 4,
        in_specs=[stacked] * 3 + [g_spec(l) for l in range(nl)],
        out_specs=[stacked] * 4,
        compiler_params=_params('arbitrary', 'arbitrary'),
    )(w, m, v, *grads)


def _pack(arrs, rows_mult=512):
    flat = jnp.concatenate([a.reshape(-1) for a in arrs])
    total = flat.shape[0]
    rows = -(-total // LANES)
    rows = -(-rows // rows_mult) * rows_mult
    flat = jnp.pad(flat, (0, rows * LANES - total))
    return flat.reshape(rows, LANES)


def _unpack(packed, shapes):
    flat = packed.reshape(-1)
    out, off = [], 0
    for s in shapes:
        size = math.prod(s)
        out.append(flat[off:off + size].reshape(s))
        off += size
    return out


def _permute(a):
    seq, w = a.shape
    return a.reshape(S5_SEGMENTS, seq // S5_SEGMENTS, w).transpose(1, 0, 2).reshape(seq, w)


def _unpermute(a):
    seq, w = a.shape
    return a.reshape(seq // S5_SEGMENTS, S5_SEGMENTS, w).transpose(1, 0, 2).reshape(seq, w)


def kernel(x, gla_norm, gla_w_in, gla_w_gate_up, gla_b_gate, gla_o_norm, gla_w_out, s5_norm, s5_w_in, s5_lam_re, s5_lam_im, s5_log_dt, s5_b_re, s5_b_im, s5_c_re, s5_c_im, s5_d, s5_w_out, mlp_norm, mlp_w_up, mlp_w_down, final_norm, loss_target, m_gla_norm, m_gla_w_in, m_gla_w_gate_up, m_gla_b_gate, m_gla_o_norm, m_gla_w_out, m_s5_norm, m_s5_w_in, m_s5_lam_re, m_s5_lam_im, m_s5_log_dt, m_s5_b_re, m_s5_b_im, m_s5_c_re, m_s5_c_im, m_s5_d, m_s5_w_out, m_mlp_norm, m_mlp_w_up, m_mlp_w_down, m_final_norm, v_gla_norm, v_gla_w_in, v_gla_w_gate_up, v_gla_b_gate, v_gla_o_norm, v_gla_w_out, v_s5_norm, v_s5_w_in, v_s5_lam_re, v_s5_lam_im, v_s5_log_dt, v_s5_b_re, v_s5_b_im, v_s5_c_re, v_s5_c_im, v_s5_d, v_s5_w_out, v_mlp_norm, v_mlp_w_up, v_mlp_w_down, v_final_norm):
    weights = dict(gla_norm=gla_norm, gla_w_in=gla_w_in, gla_w_gate_up=gla_w_gate_up, gla_b_gate=gla_b_gate, gla_o_norm=gla_o_norm, gla_w_out=gla_w_out, s5_norm=s5_norm, s5_w_in=s5_w_in, s5_lam_re=s5_lam_re, s5_lam_im=s5_lam_im, s5_log_dt=s5_log_dt, s5_b_re=s5_b_re, s5_b_im=s5_b_im, s5_c_re=s5_c_re, s5_c_im=s5_c_im, s5_d=s5_d, s5_w_out=s5_w_out, mlp_norm=mlp_norm, mlp_w_up=mlp_w_up, mlp_w_down=mlp_w_down, final_norm=final_norm)
    mom1 = dict(gla_norm=m_gla_norm, gla_w_in=m_gla_w_in, gla_w_gate_up=m_gla_w_gate_up, gla_b_gate=m_gla_b_gate, gla_o_norm=m_gla_o_norm, gla_w_out=m_gla_w_out, s5_norm=m_s5_norm, s5_w_in=m_s5_w_in, s5_lam_re=m_s5_lam_re, s5_lam_im=m_s5_lam_im, s5_log_dt=m_s5_log_dt, s5_b_re=m_s5_b_re, s5_b_im=m_s5_b_im, s5_c_re=m_s5_c_re, s5_c_im=m_s5_c_im, s5_d=m_s5_d, s5_w_out=m_s5_w_out, mlp_norm=m_mlp_norm, mlp_w_up=m_mlp_w_up, mlp_w_down=m_mlp_w_down, final_norm=m_final_norm)
    mom2 = dict(gla_norm=v_gla_norm, gla_w_in=v_gla_w_in, gla_w_gate_up=v_gla_w_gate_up, gla_b_gate=v_gla_b_gate, gla_o_norm=v_gla_o_norm, gla_w_out=v_gla_w_out, s5_norm=v_s5_norm, s5_w_in=v_s5_w_in, s5_lam_re=v_s5_lam_re, s5_lam_im=v_s5_lam_im, s5_log_dt=v_s5_log_dt, s5_b_re=v_s5_b_re, s5_b_im=v_s5_b_im, s5_c_re=v_s5_c_re, s5_c_im=v_s5_c_im, s5_d=v_s5_d, s5_w_out=v_s5_w_out, mlp_norm=v_mlp_norm, mlp_w_up=v_mlp_w_up, mlp_w_down=v_mlp_w_down, final_norm=v_final_norm)
    names = list(weights)
    big = ['gla_w_in', 'gla_w_out', 's5_w_in', 's5_w_out', 'mlp_w_up', 'mlp_w_down']
    small = [n for n in names if n not in big]

    chip = 2 * lax.axis_index('x') + lax.axis_index('y')
    h0 = x[0]
    target = loss_target[0]
    seq, dm = h0.shape
    depth = mlp_norm.shape[0]
    n_gla = gla_norm.shape[0]
    n_s5 = s5_lam_re.shape[0]
    rank = gla_w_gate_up.shape[1]
    kw = gla_b_gate.shape[1]
    dv = gla_o_norm.shape[1]
    in_w = 4 * gla_w_in.shape[2]
    vw = (in_w - rank - 2 * kw) // 2
    heads = vw // dv
    dk = kw // heads
    pw = -(-in_w // LANES) * LANES
    s5w = s5_w_in.shape[2]
    n_grp, n_state, grp = s5_b_re.shape[1:]
    hid = 4 * mlp_w_up.shape[2]
    tb = min(seq, 8 * CHUNK)
    tm = _tile(seq, 1024)

    ic = lax.axis_index('c')
    rh = lambda w: w.shape[1] // 2
    wb16 = {n: weights[n].astype(BF16) for n in big}
    gathered = {n: [None] * weights[n].shape[0] for n in big}
    g_w_in, g_gla_out, g_s5_in, g_s5_out, g_up, g_down = (gathered[n] for n in big)
    in_flight = {}

    def start_gather(i, after):
        mix = ['gla_w_in', 'gla_w_out'] if i % 2 == 0 else ['s5_w_in', 's5_w_out']
        for n, l in [(m, i // 2) for m in mix] + [('mlp_w_up', i), ('mlp_w_down', i)]:
            rows, cols = weights[n].shape[1:]
            hd = _xy_start(wb16[n][l].reshape(2, rows // 2, cols), 'bcast_c', f'ag_{n}_{l}_start', after)
            in_flight[n, l] = hd
            after = [hd['token']]
        return after

    def finish_gather(n, l, after):
        sent, y1 = _xy_wait(in_flight.pop((n, l)), f'ag_{n}_{l}_wait', after)
        y1 = lax.dynamic_update_index_in_dim(y1, lax.dynamic_index_in_dim(sent, ic, 0, keepdims=False), chip, 0)
        gathered[n][l] = _exchange(y1, 'c', 'bcast', f'ag_{n}_c')

    started = start_gather(1, start_gather(0, []))
    sharded_small = [gla_w_gate_up, s5_norm, s5_d]
    gathered_small = _exchange(_pack(sharded_small), 'xy', 'bcast', 'ag_small')
    parts = [_unpack(gathered_small[k], [a.shape for a in sharded_small]) for k in range(4)]
    wgu_full = jnp.concatenate([p[0] for p in parts], axis=2)
    s5_norm_full = jnp.concatenate([p[1] for p in parts], axis=1)
    s5_d_full = jnp.concatenate([p[2] for p in parts], axis=1)

    def gla_w_in_padded(j):
        wj = g_w_in[j].transpose(0, 2, 1, 3).reshape(dm, in_w)
        return jnp.pad(wj, ((0, 0), (0, pw - in_w)))

    grads = {n: [None] * weights[n].shape[0] for n in names if n != 'final_norm'}

    saved = []
    h = h0
    for i in range(depth):
        j = i // 2
        rec = {}
        if i + 2 < depth and i > 0:
            started = start_gather(i + 2, [h])
        if i % 2 == 0:
            rec['h_in'] = h
            finish_gather('gla_w_in', j, [h])
            if i == 0 and depth > 2:
                started = start_gather(2, [g_w_in[j]])
            hn = _norm_fwd(h, gla_norm[j:j + 1], 'gla_norm_fwd', started)
            w_in_pad = gla_w_in_padded(j)
            proj = _mm('gla_proj', 'nn', _plain(hn), _plain(w_in_pad), (seq, pw, dm),
                       [((seq, pw), F32, _plain_shape(None))], (tm, _tile(pw, 1024), dm))[0]
            wgu_pad = jnp.pad(wgu_full[j], ((0, LANES - rank), (0, 0)))
            gated, states = _gla_scan_fwd(proj, wgu_pad, gla_b_gate[j:j + 1], gla_o_norm[j:j + 1],
                                          heads, kw, vw, tb, 'gla_scan_fwd')
            finish_gather('gla_w_out', j, [gated])
            h = _mm('gla_out', 'nn', _plain(gated), _w_rows(g_gla_out, j), (seq, dm, vw),
                    [((seq, dm), F32, _plain_shape(None))],
                    (tm, _tile(dm, 1024), _tile(rh(gla_w_out), 1024)),
                    epilogue=lambda acc, hv: (acc + hv,), extras=[_plain(h)])[0]
            rec.update(hn=hn, w_in_pad=w_in_pad, proj=proj, wgu_pad=wgu_pad, gated=gated, states=states)
        else:
            hp = _permute(h)
            rec['h_in'] = hp
            hn = _norm_fwd(hp, s5_norm_full[j:j + 1], 's5_norm_fwd', started)
            finish_gather('s5_w_in', j, [hn])
            u = _mm('s5_in', 'nn', _plain(hn), _w_rows(g_s5_in, j), (seq, s5w, dm),
                    [((seq, s5w), F32, _plain_shape(None))],
                    (tm, _tile(s5w, 1024), _tile(rh(s5_w_in), 1024)))[0]
            disc, disc_vjp = jax.vjp(_s5_discretise, s5_lam_re[j], s5_lam_im[j], s5_log_dt[j], s5_b_re[j], s5_b_im[j])
            ab_re, ab_im, bb_re, bb_im = disc
            bre = _to_blocks(bb_re.transpose(0, 2, 1))
            bim = _to_blocks(bb_im.transpose(0, 2, 1))
            cre = _to_blocks(s5_c_re[j].transpose(0, 2, 1))
            cim = _to_blocks(s5_c_im[j].transpose(0, 2, 1))
            nb = n_grp // S5_GROUPS_PER_BLOCK
            are = ab_re.reshape(nb, 1, S5_GROUPS_PER_BLOCK * n_state)
            aim = ab_im.reshape(nb, 1, S5_GROUPS_PER_BLOCK * n_state)
            dskip = s5_d_full[j:j + 1]
            y, z, xr, xi = _s5_fwd(u, bre, bim, cre, cim, are, aim, dskip, 's5_scan_fwd')
            finish_gather('s5_w_out', j, [z])
            o = _mm('s5_out', 'nn', _plain(z), _w_cols(g_s5_out, j), (seq, 2 * dm, s5w),
                    [((seq, 2 * dm), F32, _plain_shape(None))],
                    (tm, _tile(s5_w_out.shape[2], 1024), _tile(rh(s5_w_out), 1024)))[0]
            h = _unpermute(_glu_fwd(o, hp, 's5_glu_fwd'))
            rec.update(hn=hn, u=u, y=y, z=z, xr=xr, xi=xi, o=o, mats=(bre, bim, cre, cim, are, aim, dskip),
                       disc_vjp=disc_vjp)
        rec['h_mid'] = h
        hn2 = _norm_fwd(h, mlp_norm[i:i + 1], 'mlp_norm_fwd')
        finish_gather('mlp_w_up', i, [hn2])
        act, act2 = _mm('mlp_up', 'nn', _plain(hn2), _w_cols(g_up, i), (seq, hid, dm),
                        [((seq, hid), BF16, _plain_shape(None))] * 2,
                        (tm, _tile(mlp_w_up.shape[2], 1024), _tile(rh(mlp_w_up), 1024)),
                        epilogue=lambda acc: (jnp.maximum(acc, 0.0), jnp.square(jnp.maximum(acc, 0.0))))
        finish_gather('mlp_w_down', i, [act2])
        h = _mm('mlp_down', 'nn', _plain(act2), _w_rows(g_down, i), (seq, dm, hid),
                [((seq, dm), F32, _plain_shape(None))],
                (tm, _tile(dm, 1024), _tile(rh(mlp_w_down), 1024)),
                epilogue=lambda acc, hv: (acc + hv,), extras=[_plain(h)])[0]
        rec.update(hn2=hn2, act=act, act2=act2)
        saved.append(rec)

    dh, loss_cols, d_final = _loss_head(h, target, final_norm.reshape(1, dm), 'loss_head')
    loss = lax.psum(jnp.sum(loss_cols), ('x', 'y', 'c'))
    grads['final_norm'] = [d_final.reshape(dm)]

    big_grads = {n: [None] * weights[n].shape[0] for n in big}
    reducing = []

    def reduce_begin(dw, n, l):
        _, _, rows_h, cols = dw.shape
        ya = _exchange(dw, 'c', 'a2a', f'rs_{n}_pair')
        pre = _sum_slots(ya.reshape(2, 4 * rows_h, cols), BF16, f'rs_{n}_pairsum').reshape(4, rows_h, cols)
        hd = _xy_start(pre, 'a2a', f'rs_{n}_{l}_start', [])
        reducing.append((n, l, hd))
        return [hd['token']]

    def reduce_end(after):
        n, l, hd = reducing.pop(0)
        sent, yb = _xy_wait(hd, f'rs_{n}_{l}_wait', after)
        yb = lax.dynamic_update_index_in_dim(yb, lax.dynamic_index_in_dim(sent, chip, 0, keepdims=False), chip, 0)
        fin = _sum_slots(yb, F32, f'rs_{n}_chipsum')
        yc = _exchange(fin, 'c', 'bcast', f'rs_{n}_back')
        big_grads[n][l] = yc.reshape(2 * fin.shape[0], fin.shape[1])

    def reduce_scatter(dw, n, l):
        started_ = reduce_begin(dw, n, l)
        while len(reducing) > 1:
            reduce_end(started_)
        return started_
    for i in reversed(range(depth)):
        j = i // 2
        rec = saved[i]
        r_dn, c_dn = mlp_w_down.shape[1:]
        shape, spec = _dw_rows(r_dn, c_dn)
        dw = _mm('mlp_down_dw', 'tn', _plain(rec['act2']), _plain(dh), (hid, dm, seq),
                 [(shape, BF16, spec)], (_tile(r_dn // 2, 1024), _tile(c_dn, 1024), seq))[0]
        behind = reduce_scatter(dw, 'mlp_w_down', i)
        dpre = _mm('mlp_down_dx', 'nt', _plain(dh), _w_rows(g_down, i), (seq, hid, dm),
                   [((seq, hid), BF16, _plain_shape(None))],
                   (tm, _tile(rh(mlp_w_down), 1024), dm),
                   epilogue=lambda acc, av: (acc * (2.0 * av.astype(F32)),), extras=[_plain(rec['act'])],
                   after=behind)[0]
        r_up, c_up = mlp_w_up.shape[1:]
        shape, spec = _dw_cols(r_up, c_up)
        dw = _mm('mlp_up_dw', 'tn', _plain(rec['hn2']), _plain(dpre), (dm, hid, seq),
                 [(shape, BF16, spec)], (_tile(r_up // 2, 1024), _tile(c_up, 1024), seq))[0]
        behind = reduce_scatter(dw, 'mlp_w_up', i)
        dhn = _mm('mlp_up_dx', 'nt', _plain(dpre), _w_cols(g_up, i), (seq, dm, hid),
                  [((seq, dm), F32, _plain_shape(None))],
                  (tm, _tile(rh(mlp_w_up), 1024), _tile(mlp_w_up.shape[2], 2048)), after=behind)[0]
        dh, dg = _norm_bwd(rec['h_mid'], dhn, dh, mlp_norm[i:i + 1], 'mlp_norm_bwd')
        grads['mlp_norm'][i] = dg[0]

        if i % 2 == 0:
            r_o, c_o = gla_w_out.shape[1:]
            shape, spec = _dw_rows(r_o, c_o)
            dw = _mm('gla_out_dw', 'tn', _plain(rec['gated']), _plain(dh), (vw, dm, seq),
                     [(shape, BF16, spec)], (_tile(r_o // 2, 1024), _tile(c_o, 1024), seq))[0]
            behind = reduce_scatter(dw, 'gla_w_out', j)
            dgated = _mm('gla_out_dx', 'nt', _plain(dh), _w_rows(g_gla_out, j), (seq, vw, dm),
                         [((seq, vw), F32, _plain_shape(None))],
                         (tm, _tile(rh(gla_w_out), 1024), dm), after=behind)[0]
            dq, dkk, dvv, dr, dpre_g, db, don = _gla_scan_bwd(
                rec['proj'], rec['wgu_pad'], gla_b_gate[j:j + 1], gla_o_norm[j:j + 1], rec['states'], dgated,
                heads, kw, vw, tb, 'gla_scan_bwd')
            grads['gla_b_gate'][j] = db[0]
            grads['gla_o_norm'][j] = don[0]
            dgl = _mm('gla_gate_dx', 'nt', _plain(dpre_g), _plain(rec['wgu_pad']), (seq, LANES, kw),
                      [((seq, LANES), BF16, _plain_shape(None))], (tm, LANES, kw))[0]
            g_low = rec['proj'][:, pw - LANES:]
            dwgu = _mm('gla_gate_dw', 'tn', _plain(g_low), _plain(dpre_g), (LANES, kw, seq),
                       [((LANES, kw), F32, _plain_shape(None))], (LANES, kw, seq))[0]
            grads['gla_w_gate_up'][j] = dwgu[:rank]
            dproj = jnp.concatenate([dq, dkk, dvv, dr, dgl], axis=1)
            dw_pad = _mm('gla_proj_dw', 'tn', _plain(rec['hn']), _plain(dproj), (dm, pw, seq),
                         [((dm, pw), BF16, _plain_shape(None))], (_tile(dm, 1024), _tile(pw, 1024), seq))[0]
            shard_w = in_w // 4
            dw = dw_pad[:, :in_w].reshape(2, dm // 2, 4, shard_w).transpose(0, 2, 1, 3)
            behind = reduce_scatter(dw, 'gla_w_in', j)
            dhn = _mm('gla_proj_dx', 'nt', _plain(dproj), _plain(rec['w_in_pad']), (seq, dm, pw),
                      [((seq, dm), F32, _plain_shape(None))], (tm, _tile(dm, 1024), _tile(pw, 1024)),
                      after=behind)[0]
            dh, dg = _norm_bwd(rec['h_in'], dhn, dh, gla_norm[j:j + 1], 'gla_norm_bwd')
            grads['gla_norm'][j] = dg[0]
        else:
            dhp = _permute(dh)
            do = _glu_bwd(rec['o'], dhp, 's5_glu_bwd')
            r_o, c_o = s5_w_out.shape[1:]
            shape, spec = _dw_cols(r_o, c_o)
            dw = _mm('s5_out_dw', 'tn', _plain(rec['z']), _plain(do), (s5w, 2 * dm, seq),
                     [(shape, BF16, spec)], (_tile(r_o // 2, 1024), _tile(c_o, 1024), seq))[0]
            behind = reduce_scatter(dw, 's5_w_out', j)
            dz = _mm('s5_out_dx', 'nt', _plain(do), _w_cols(g_s5_out, j), (seq, s5w, 2 * dm),
                     [((seq, s5w), F32, _plain_shape(None))],
                     (tm, _tile(rh(s5_w_out), 1024), _tile(s5_w_out.shape[2], 1024)), after=behind)[0]
            bre, bim, cre, cim, are, aim, dskip = rec['mats']
            du, dcr, dci, dbr, dbi, dar, dai, dd = _s5_bwd(dz, rec['y'], rec['u'], rec['xr'], rec['xi'],
                                                           bre, bim, cre, cim, are, aim, dskip, 's5_scan_bwd')
            grads['s5_c_re'][j] = _from_blocks(dcr, grp, n_state)
            grads['s5_c_im'][j] = _from_blocks(dci, grp, n_state)
            dbb_re = _from_blocks(dbr, grp, n_state).transpose(0, 2, 1)
            dbb_im = _from_blocks(dbi, grp, n_state).transpose(0, 2, 1)
            d_lr, d_li, d_dt, d_bre, d_bim = rec['disc_vjp'](
                (dar.reshape(n_grp, n_state), dai.reshape(n_grp, n_state), dbb_re, dbb_im))
            grads['s5_lam_re'][j] = d_lr
            grads['s5_lam_im'][j] = d_li
            grads['s5_log_dt'][j] = d_dt
            grads['s5_b_re'][j] = d_bre
            grads['s5_b_im'][j] = d_bim
            grads['s5_d'][j] = dd[0]
            r_i, c_i = s5_w_in.shape[1:]
            shape, spec = _dw_rows(r_i, c_i)
            dw = _mm('s5_in_dw', 'tn', _plain(rec['hn']), _plain(du), (dm, s5w, seq),
                     [(shape, BF16, spec)], (_tile(r_i // 2, 1024), _tile(c_i, 1024), seq))[0]
            behind = reduce_scatter(dw, 's5_w_in', j)
            dhn = _mm('s5_in_dx', 'nt', _plain(du), _w_rows(g_s5_in, j), (seq, dm, s5w),
                      [((seq, dm), F32, _plain_shape(None))],
                      (tm, _tile(rh(s5_w_in), 1024), _tile(s5w, 1024)), after=behind)[0]
            dhp, dg = _norm_bwd(rec['h_in'], dhn, dhp, s5_norm_full[j:j + 1], 's5_norm_bwd')
            dh = _unpermute(dhp)
            grads['s5_norm'][j] = dg[0]
    grad_x = dh[None]
    while reducing:
        reduce_end([dh])

    local_small = [jnp.stack(grads[n]) if n != 'final_norm' else grads[n][0] for n in small]
    full_shapes = [a.shape for a in local_small]
    gathered = _exchange(_exchange(_pack(local_small), 'xy', 'bcast', 'ar_small_xy'), 'c', 'bcast', 'ar_small_c')
    rows = gathered.shape[2]
    summed = _sum_slots(gathered.reshape(8, rows, LANES), F32, 'ar_small_sum')
    small_full = dict(zip(small, _unpack(summed, full_shapes)))
    small_grad = {}
    for n in small:
        g = small_full[n]
        if g.shape != weights[n].shape:
            ax = [a for a in range(g.ndim) if g.shape[a] != weights[n].shape[a]][0]
            g = lax.dynamic_slice_in_dim(g, chip * weights[n].shape[ax], weights[n].shape[ax], axis=ax)
        small_grad[n] = g

    out_g, out_d, out_m, out_v = {}, {}, {}, {}
    for n in big:
        out_g[n], out_d[n], out_m[n], out_v[n] = _adamw(weights[n], mom1[n], mom2[n], big_grads[n], 'adamw_' + n)
    shapes = [weights[n].shape for n in small]
    pw_, pm_, pv_, pg_ = (_pack([d[n] for n in small]) for d in (weights, mom1, mom2, small_grad))
    _, sd, sm, sv = _adamw(pw_[None], pm_[None], pv_[None], [pg_], 'adamw_small')
    for n, d_, m_, v_ in zip(small, _unpack(sd[0], shapes), _unpack(sm[0], shapes), _unpack(sv[0], shapes)):
        out_g[n], out_d[n], out_m[n], out_v[n] = small_grad[n], d_, m_, v_

    return (loss, grad_x, *[out_g[n] for n in names], *[out_d[n] for n in names],
            *[out_m[n] for n in names], *[out_v[n] for n in names])
```

```python
import functools
import math

import jax
import jax.numpy as jnp
from jax import lax
from jax.experimental import pallas as pl
from jax.experimental.pallas import tpu as pltpu

F32 = jnp.float32
BF16 = jnp.bfloat16

EPS = 1e-6
CHUNK = 64
GLA_GATE_TEMP = 16.0
S5_EIG_CLIP = -1e-4
S5_SEGMENTS = 8
S5_GROUPS_PER_BLOCK = 8
LANES = 128
ADAM_LR = 0.001
ADAM_B1 = 0.9
ADAM_B2 = 0.999
ADAM_EPS = 1e-08
ADAM_WD = 0.01
ADAM_STEP = 10
VMEM_LIMIT_BYTES = 56 * 1024 * 1024
PAIR_PIECE_BYTES = 2 * 1024 * 1024
GATHERS_IN_FLIGHT = 2
PAIR_VMEM_BYTES = 40 * 1024 * 1024

MESH = pl.DeviceIdType.MESH
ANY = pl.BlockSpec(memory_space=pl.ANY)
IN_VMEM = pl.BlockSpec(memory_space=pltpu.VMEM)
IN_HBM = pl.BlockSpec(memory_space=pltpu.HBM)
IN_SEM = pl.BlockSpec(memory_space=pltpu.SEMAPHORE)
DATAFLOW = pltpu.SideEffectType.DATAFLOW_SIDE_EFFECTING


def _pcall(body, **kw):
    return pl.pallas_call(body, **kw)


def _params(*sem):
    return pltpu.CompilerParams(dimension_semantics=sem, vmem_limit_bytes=VMEM_LIMIT_BYTES)


def _tile(dim, target, unit=LANES):
    if dim <= target:
        return dim
    best = None
    for t in range(unit, target + 1, unit):
        if dim % t == 0:
            best = t
    assert best is not None, (dim, target)
    return best


def _exchange(x, group, mode, name):
    n = 2 if group == 'c' else 4
    blk = x.shape if mode == 'bcast' else x.shape[1:]
    if mode == 'a2a':
        assert x.shape[0] == n
    flips = [(0, 0, 1)] if group == 'c' else [(1, 0, 0), (0, 1, 0), (1, 1, 0)]
    itemsize = jnp.dtype(x.dtype).itemsize
    staged = group == 'c' and (x.size + n * math.prod(blk)) * itemsize <= PAIR_VMEM_BYTES
    if group == 'c' and not staged:
        ic = lax.axis_index('c')
        own = x if mode == 'bcast' else lax.dynamic_index_in_dim(x, ic, 0, keepdims=False)
        return lax.dynamic_update_index_in_dim(_pair_exchange_chunked(x, mode, name), own, ic, 0)

    def body(x_ref, y_ref, send_sems, recv_sems, local_sem):
        ix, iy, ic = lax.axis_index('x'), lax.axis_index('y'), lax.axis_index('c')

        def slot(px, py, pc):
            return pc if group == 'c' else 2 * px + py

        def src(px, py, pc):
            if mode == 'a2a':
                return x_ref.at[slot(px, py, pc)]
            if mode == 'bcast_c':
                return x_ref.at[ic]
            return x_ref

        me = (ix, iy, ic)
        local = pltpu.make_async_copy(src(*me), y_ref.at[slot(*me)], local_sem)
        local.start()
        peers = []
        for fx, fy, fc in flips:
            peers.append((1 - ix if fx else ix, 1 - iy if fy else iy, 1 - ic if fc else ic))
        sends = []
        for k, peer in enumerate(peers):
            cp = pltpu.make_async_remote_copy(
                src_ref=src(*peer), dst_ref=y_ref.at[slot(*me)],
                send_sem=send_sems.at[k], recv_sem=recv_sems.at[k],
                device_id=peer, device_id_type=MESH)
            cp.start()
            sends.append(cp)
        for k, peer in enumerate(peers):
            pltpu.make_async_remote_copy(
                src_ref=src(*peer), dst_ref=y_ref.at[slot(*peer)],
                send_sem=send_sems.at[k], recv_sem=recv_sems.at[k],
                device_id=peer, device_id_type=MESH).wait_recv()
        for cp in sends:
            cp.wait_send()
        local.wait()

    return _pcall(
        body, name=name,
        out_shape=jax.ShapeDtypeStruct((n,) + tuple(blk), x.dtype),
        in_specs=[IN_VMEM if staged else ANY], out_specs=IN_VMEM if staged else ANY,
        scratch_shapes=[pltpu.SemaphoreType.DMA((len(flips),)),
                        pltpu.SemaphoreType.DMA((len(flips),)),
                        pltpu.SemaphoreType.DMA(())],
        compiler_params=pltpu.CompilerParams(vmem_limit_bytes=VMEM_LIMIT_BYTES),
    )(x)


def _split_axis(blk, dtype, piece_bytes):
    itemsize = jnp.dtype(dtype).itemsize
    sublanes = 8 * 4 // itemsize
    want = max(1, math.prod(blk) * itemsize // piece_bytes)
    for pieces in [s for s in (64, 32, 16, 8, 4, 2) if s <= want]:
        for ax in range(len(blk) - 1):
            unit = sublanes if ax == len(blk) - 2 else 1
            if blk[ax] % (pieces * unit) == 0:
                return ax, pieces
    return 0, 1


def _pair_exchange_chunked(x, mode, name):
    blk = x.shape if mode == 'bcast' else x.shape[1:]
    ax, pieces = _split_axis(blk, x.dtype, PAIR_PIECE_BYTES)
    step = blk[ax] // pieces
    piece_shape = tuple(blk[:ax]) + (step,) + tuple(blk[ax + 1:])

    def piece(ref, p):
        return ref.at[(slice(None),) * ax + (pl.ds(p * step, step),)]

    def body(x_ref, y_ref, out_buf, in_buf, send_sems, recv_sems, stage_sems, drain_sems, credit_sem):
        ix, iy, ic = lax.axis_index('x'), lax.axis_index('y'), lax.axis_index('c')
        sibling = (ix, iy, 1 - ic)
        mine = x_ref.at[ic] if mode != 'bcast' else x_ref
        theirs = x_ref.at[1 - ic] if mode == 'a2a' else mine

        def stage(p):
            return pltpu.make_async_copy(piece(theirs, p), out_buf.at[p % 2], stage_sems.at[p % 2])

        def remote(p):
            return pltpu.make_async_remote_copy(
                src_ref=out_buf.at[p % 2], dst_ref=in_buf.at[p % 2],
                send_sem=send_sems.at[p], recv_sem=recv_sems.at[p],
                device_id=sibling, device_id_type=MESH)

        def drain(p):
            return pltpu.make_async_copy(in_buf.at[p % 2], piece(y_ref.at[1 - ic], p), drain_sems.at[p % 2])

        stage(0).start()
        for p in range(pieces):
            stage(p).wait()
            if p >= 2:
                pl.semaphore_wait(credit_sem, 1)
            remote(p).start()
            if p + 1 < pieces:
                if p >= 1:
                    remote(p - 1).wait_send()
                stage(p + 1).start()
            remote(p).wait_recv()
            drain(p).start()
            drain(p).wait()
            if p + 2 < pieces:
                pl.semaphore_signal(credit_sem, inc=1, device_id=sibling, device_id_type=MESH)
        for p in range(max(0, pieces - 2), pieces):
            remote(p).wait_send()

    return _pcall(
        body, name=name,
        out_shape=jax.ShapeDtypeStruct((2,) + tuple(blk), x.dtype),
        in_specs=[ANY], out_specs=ANY,
        scratch_shapes=[pltpu.VMEM((2,) + piece_shape, x.dtype), pltpu.VMEM((2,) + piece_shape, x.dtype),
                        pltpu.SemaphoreType.DMA((pieces,)), pltpu.SemaphoreType.DMA((pieces,)),
                        pltpu.SemaphoreType.DMA((2,)), pltpu.SemaphoreType.DMA((2,)),
                        pltpu.SemaphoreType.REGULAR],
        compiler_params=pltpu.CompilerParams(vmem_limit_bytes=VMEM_LIMIT_BYTES),
    )(x)


_XY_FLIPS = [(1, 0), (0, 1), (1, 1)]


def _xy_copies(mode, x_ref, land_ref, sems):
    ix, iy, ic = lax.axis_index('x'), lax.axis_index('y'), lax.axis_index('c')
    out = []
    for k, (fx, fy) in enumerate(_XY_FLIPS):
        px, py = (1 - ix if fx else ix), (1 - iy if fy else iy)
        if mode == 'a2a':
            src = x_ref.at[2 * px + py]
        elif mode == 'bcast_c':
            src = x_ref.at[ic]
        else:
            src = x_ref
        mk = lambda dst, src=src, k=k, px=px, py=py: pltpu.make_async_remote_copy(
            src_ref=src, dst_ref=dst, send_sem=sems[k], recv_sem=sems[3 + k],
            device_id=(px, py, ic), device_id_type=MESH)
        out.append((mk(land_ref.at[2 * ix + iy]), mk(land_ref.at[2 * px + py])))
    return out


def _xy_start(x, mode, name, after):
    blk = x.shape if mode == 'bcast' else x.shape[1:]
    land_shape = (4,) + tuple(blk)
    n_after = len(after)

    def body(*refs):
        x_ref, land_ref = refs[0], refs[1]
        sems = refs[2 + n_after:8 + n_after]
        for send, _ in _xy_copies(mode, x_ref, land_ref, sems):
            send.start()
        refs[-1][...] = jnp.zeros_like(refs[-1])

    outs = _pcall(
        body, name=name,
        out_shape=(pltpu.SemaphoreType.DMA(()),) * 6
        + (pltpu.HBM(x.shape, x.dtype), pltpu.HBM(land_shape, x.dtype), jax.ShapeDtypeStruct((8, LANES), F32)),
        in_specs=(IN_HBM, IN_HBM) + (ANY,) * n_after,
        out_specs=(IN_SEM,) * 6 + (IN_HBM, IN_HBM, IN_VMEM),
        input_output_aliases={0: 6, 1: 7},
        compiler_params=pltpu.CompilerParams(has_side_effects=DATAFLOW),
    )(pltpu.with_memory_space_constraint(x, pltpu.HBM),
      pltpu.with_memory_space_constraint(lax.empty(land_shape, x.dtype), pltpu.HBM), *after)
    return dict(sems=outs[:6], sent=outs[6], land=outs[7], token=outs[8], mode=mode)


def _xy_wait(handle, name, after):
    mode = handle['mode']
    n_after = len(after)

    def body(*refs):
        x_ref, land_ref = refs[0], refs[1]
        for _, recv in _xy_copies(mode, x_ref, land_ref, refs[2:8]):
            recv.wait_send()
            recv.wait_recv()

    sent, land = handle['sent'], handle['land']
    return _pcall(
        body, name=name,
        out_shape=(pltpu.HBM(sent.shape, sent.dtype), pltpu.HBM(land.shape, land.dtype)),
        in_specs=(IN_HBM, IN_HBM) + (IN_SEM,) * 6 + (ANY,) * n_after,
        out_specs=(IN_HBM, IN_HBM),
        input_output_aliases={0: 0, 1: 1},
        compiler_params=pltpu.CompilerParams(has_side_effects=DATAFLOW),
    )(sent, land, *handle['sems'], *after)


def _sum_slots(y, out_dtype, name):
    n, rows, cols = y.shape
    tm = _tile(rows, max(8, (1 << 20) // (n * cols) // 8 * 8), unit=8)

    def body(y_ref, o_ref):
        acc = y_ref[0].astype(F32)
        for k in range(1, n):
            acc = acc + y_ref[k].astype(F32)
        o_ref[...] = acc.astype(o_ref.dtype)

    return _pcall(
        body, name=name, grid=(rows // tm,),
        out_shape=jax.ShapeDtypeStruct((rows, cols), out_dtype),
        in_specs=[pl.BlockSpec((n, tm, cols), lambda i: (0, i, 0))],
        out_specs=pl.BlockSpec((tm, cols), lambda i: (i, 0)),
        compiler_params=_params('parallel'),
    )(y)


class _Op:
    def __init__(self, arr, spec):
        self.arr = arr
        self.spec = spec


def _plain(arr):
    return _Op(arr, lambda t0, t1: ((t0, t1), lambda b0, b1: (b0, b1)))


def _plain_shape(shape):
    return lambda t0, t1: ((t0, t1), lambda b0, b1: (b0, b1))


def _dw_cols(rows, cols):
    rh = rows // 2

    def spec(t0, t1):
        assert (rh % t0 == 0 or t0 == rows) and cols % t1 == 0, (rh, cols, t0, t1)
        qr, qc = max(1, rh // t0), cols // t1
        if t0 == rows:
            return (2, None, rh, t1), lambda b0, b1: (0, b1 // qc, 0, b1 % qc)
        return (None, None, t0, t1), lambda b0, b1: (b0 // qr, b1 // qc, b0 % qr, b1 % qc)
    return (2, 4, rh, cols), spec


def _dw_rows(rows, cols):
    rh = rows // 2

    def spec(t0, t1):
        assert (rh % t0 == 0 or t0 == rows) and cols % t1 == 0, (rh, cols, t0, t1)
        qr = max(1, rh // t0)
        if t0 == rows:
            return (2, None, rh, t1), lambda b0, b1: (0, b0, 0, b1)
        return (None, None, t0, t1), lambda b0, b1: ((b0 // qr) % 2, b0 // (2 * qr), b0 % qr, b1)
    return (2, 4, rh, cols), spec


def _w_cols(g, j):
    _, _, rh, cols = g[j].shape
    return _Op(g[j], _dw_cols(2 * rh, cols)[1])


def _w_rows(g, j):
    _, _, rh, cols = g[j].shape
    return _Op(g[j], _dw_rows(2 * rh, cols)[1])


def _mm(name, mode, a, b, dims, outs, tiles, epilogue=None, extras=(), after=()):
    m, n, k = dims
    tm, tn, tk = tiles
    assert m % tm == 0 and n % tn == 0 and k % tk == 0, (name, dims, tiles)
    nk = k // tk
    if mode == 'nn':
        a_t, a_ix, b_t, b_ix, ca, cb = (tm, tk), (lambda i, j, kk: (i, kk)), (tk, tn), (lambda i, j, kk: (kk, j)), 1, 0
    elif mode == 'nt':
        a_t, a_ix, b_t, b_ix, ca, cb = (tm, tk), (lambda i, j, kk: (i, kk)), (tn, tk), (lambda i, j, kk: (j, kk)), 1, 1
    else:
        a_t, a_ix, b_t, b_ix, ca, cb = (tk, tm), (lambda i, j, kk: (kk, i)), (tk, tn), (lambda i, j, kk: (kk, j)), 0, 0
    a_blk, a_fn = a.spec(*a_t)
    b_blk, b_fn = b.spec(*b_t)
    in_specs = [pl.BlockSpec(a_blk, lambda i, j, kk: a_fn(*a_ix(i, j, kk))),
                pl.BlockSpec(b_blk, lambda i, j, kk: b_fn(*b_ix(i, j, kk)))]
    operands = [a.arr, b.arr]
    for e in extras:
        e_blk, e_fn = e.spec(tm, tn)
        in_specs.append(pl.BlockSpec(e_blk, functools.partial(lambda i, j, kk, f: f(i, j), f=e_fn)))
        operands.append(e.arr)
    out_shapes, out_specs = [], []
    for shape, dtype, spec in outs:
        o_blk, o_fn = spec(tm, tn)
        out_shapes.append(jax.ShapeDtypeStruct(shape, dtype))
        out_specs.append(pl.BlockSpec(o_blk, functools.partial(lambda i, j, kk, f: f(i, j), f=o_fn)))
    n_ex, n_out = len(extras), len(outs)
    in_specs += [ANY] * len(after)
    operands += list(after)
    if epilogue is None:
        epilogue = lambda acc: (acc,)

    def body(a_ref, b_ref, *rest):
        ex_refs = rest[:n_ex]
        rest = rest[:n_ex] + rest[n_ex + len(after):]
        out_refs = rest[n_ex:n_ex + n_out]
        bv = b_ref[...]
        if bv.ndim == 3:
            bv = bv.reshape(bv.shape[0] * bv.shape[1], bv.shape[2])
        p = lax.dot_general(a_ref[...].astype(BF16), bv.astype(BF16),
                            (((ca,), (cb,)), ((), ())), preferred_element_type=F32)

        def finish(acc):
            res = epilogue(acc, *[r[...] for r in ex_refs])
            for o_ref, val in zip(out_refs, res):
                o_ref[...] = val.astype(o_ref.dtype)

        if nk == 1:
            finish(p)
        else:
            acc_ref = rest[n_ex + n_out]
            kk = pl.program_id(2)

            @pl.when(kk == 0)
            def _():
                acc_ref[...] = p

            @pl.when(kk > 0)
            def _():
                acc_ref[...] += p

            @pl.when(kk == nk - 1)
            def _():
                finish(acc_ref[...])

    res = _pcall(
        body, name=name, grid=(m // tm, n // tn, nk),
        out_shape=out_shapes, in_specs=in_specs, out_specs=out_specs,
        scratch_shapes=[pltpu.VMEM((tm, tn), F32)] if nk > 1 else [],
        compiler_params=_params('parallel', 'parallel', 'arbitrary'),
    )(*operands)
    return res


def _rowwise(name, fn, row_ins, vec_ins, outs, reds, tm, after=()):
    rows = row_ins[0].shape[0]
    assert rows % tm == 0
    n_in = len(row_ins) + len(vec_ins)
    n_out = len(outs)

    def body(*refs):
        vals = [r[...] for r in refs[:n_in]]
        refs = refs[:n_in] + refs[n_in + len(after):]
        res = fn(*vals)
        for o_ref, val in zip(refs[n_in:n_in + n_out], res[:n_out]):
            o_ref[...] = val.astype(o_ref.dtype)
        first = pl.program_id(0) == 0
        for r_ref, val in zip(refs[n_in + n_out:], res[n_out:]):
            @pl.when(first)
            def _(r_ref=r_ref, val=val):
                r_ref[...] = val

            @pl.when(jnp.logical_not(first))
            def _(r_ref=r_ref, val=val):
                r_ref[...] += val

    in_specs = [pl.BlockSpec((tm, a.shape[1]), lambda i: (i, 0)) for a in row_ins]
    in_specs += [pl.BlockSpec((1, v.shape[1]), lambda i: (0, 0)) for v in vec_ins]
    in_specs += [ANY] * len(after)
    out_shapes = [jax.ShapeDtypeStruct((rows, w), dt) for w, dt in outs]
    out_shapes += [jax.ShapeDtypeStruct((1, w), F32) for w in reds]
    out_specs = [pl.BlockSpec((tm, w), lambda i: (i, 0)) for w, _ in outs]
    out_specs += [pl.BlockSpec((1, w), lambda i: (0, 0)) for w in reds]
    return _pcall(
        body, name=name, grid=(rows // tm,),
        out_shape=out_shapes, in_specs=in_specs, out_specs=out_specs,
        compiler_params=_params('arbitrary'),
    )(*row_ins, *vec_ins, *after)


def _norm_fwd(h, g, name, after=()):
    def fn(hv, gv):
        rstd = lax.rsqrt(jnp.mean(hv * hv, axis=-1, keepdims=True) + EPS)
        return (hv * rstd * gv,)
    return _rowwise(name, fn, [h], [g], [(h.shape[1], BF16)], [], 256, after)[0]


def _norm_bwd(h, dhn, dres, g, name):
    def fn(hv, dv, rv, gv):
        rstd = lax.rsqrt(jnp.mean(hv * hv, axis=-1, keepdims=True) + EPS)
        xhat = hv * rstd
        dxhat = dv * gv
        dh = rv + rstd * (dxhat - xhat * jnp.mean(dxhat * xhat, axis=-1, keepdims=True))
        return dh, jnp.sum(dv * xhat, axis=0, keepdims=True)
    w = h.shape[1]
    return _rowwise(name, fn, [h, dhn, dres], [g], [(w, F32)], [w], 256)


def _loss_head(h, target, g, name):
    w = h.shape[1]

    def fn(hv, tv, gv):
        rstd = lax.rsqrt(jnp.mean(hv * hv, axis=-1, keepdims=True) + EPS)
        xhat = hv * rstd
        diff = xhat * gv - tv
        dy = diff * (1.0 / w)
        dxhat = dy * gv
        dh = rstd * (dxhat - xhat * jnp.mean(dxhat * xhat, axis=-1, keepdims=True))
        return (dh, jnp.sum(0.5 * dy * diff, axis=0, keepdims=True),
                jnp.sum(dy * xhat, axis=0, keepdims=True))
    return _rowwise(name, fn, [h, target], [g], [(w, F32)], [w, w], 256)


def _split3(x):
    hi = x.astype(BF16)
    r1 = x - hi.astype(F32)
    mid = r1.astype(BF16)
    lo = (r1 - mid.astype(F32)).astype(BF16)
    return hi, mid, lo


def _tri_dot(tri, x):
    hi, mid, lo = _split3(x)
    d = lambda p: jnp.dot(tri, p, preferred_element_type=F32)
    return d(hi) + d(mid) + d(lo)


def _log_sigmoid(x):
    return jnp.minimum(x, 0.0) - jnp.log(1.0 + jnp.exp(-jnp.abs(x)))


def _gla_dims(proj_w, kw, vw, dk, dv):
    assert kw % dk == 0 and (2 * kw) % dv == 0 and (2 * kw + vw) % dv == 0 and (2 * kw + 2 * vw) % LANES == 0
    return dict(q0=0, k0=kw // dk, v0=2 * kw // dv, r0=(2 * kw + vw) // dv, g0=(2 * kw + 2 * vw) // LANES)


def _gla_gates(gl, wgu, bias):
    pre = jnp.dot(gl.astype(BF16), wgu, preferred_element_type=F32) + bias
    la = _log_sigmoid(pre) * (1.0 / GLA_GATE_TEMP)
    r_i = lax.broadcasted_iota(jnp.int32, (CHUNK, CHUNK), 0)
    c_i = lax.broadcasted_iota(jnp.int32, (CHUNK, CHUNK), 1)
    cum = _tri_dot((c_i <= r_i).astype(BF16), la)
    total = cum[CHUNK - 1:CHUNK, :]
    return pre, cum, total


def _gla_scan_fwd(proj, wgu_pad, b_gate, o_norm, heads, kw, vw, tb, name):
    seq, pw = proj.shape
    dk, dv = kw // heads, vw // heads
    cb = tb // CHUNK
    nt = seq // tb
    o = _gla_dims(pw, kw, vw, dk, dv)
    scale = dk ** -0.5

    def body(q_ref, k_ref, v_ref, r_ref, gl_ref, wgu_ref, b_ref, on_ref, out_ref, st_ref, s_scr):
        @pl.when(pl.program_id(1) == 0)
        def _():
            s_scr[...] = jnp.zeros_like(s_scr)

        wgu = wgu_ref[...].astype(BF16)
        bias = b_ref[...]
        onorm = on_ref[...]
        st = s_scr[...]
        for ci in range(cb):
            rows = pl.ds(ci * CHUNK, CHUNK)
            _, cum, total = _gla_gates(gl_ref[rows, :], wgu, bias)
            kdec = k_ref[rows, :] * jnp.exp(total - cum)
            st = st * jnp.exp(total) + lax.dot_general(
                v_ref[rows, :].astype(BF16), kdec.astype(BF16), (((0,), (0,)), ((), ())),
                preferred_element_type=F32)
            st_ref[ci] = st
            qs = (q_ref[rows, :] * scale).astype(BF16)
            ov = lax.dot_general(qs, st.astype(BF16), (((1,), (1,)), ((), ())), preferred_element_type=F32)
            rstd = lax.rsqrt(jnp.mean(ov * ov, axis=-1, keepdims=True) + EPS)
            rv = r_ref[rows, :]
            out_ref[rows, :] = (ov * rstd * onorm * (rv * jax.nn.sigmoid(rv))).astype(out_ref.dtype)
        s_scr[...] = st

    in_specs = [
        pl.BlockSpec((tb, dk), lambda h, t: (t, o['q0'] + h)),
        pl.BlockSpec((tb, dk), lambda h, t: (t, o['k0'] + h)),
        pl.BlockSpec((tb, dv), lambda h, t: (t, o['v0'] + h)),
        pl.BlockSpec((tb, dv), lambda h, t: (t, o['r0'] + h)),
        pl.BlockSpec((tb, LANES), lambda h, t: (t, o['g0'])),
        pl.BlockSpec((LANES, dk), lambda h, t: (0, h)),
        pl.BlockSpec((1, dk), lambda h, t: (0, h)),
        pl.BlockSpec((1, dv), lambda h, t: (0, 0)),
    ]
    return _pcall(
        body, name=name, grid=(heads, nt),
        out_shape=[jax.ShapeDtypeStruct((seq, vw), BF16),
                   jax.ShapeDtypeStruct((heads, seq // CHUNK, dv, dk), F32)],
        in_specs=in_specs,
        out_specs=[pl.BlockSpec((tb, dv), lambda h, t: (t, h)),
                   pl.BlockSpec((None, cb, dv, dk), lambda h, t: (h, t, 0, 0))],
        scratch_shapes=[pltpu.VMEM((dv, dk), F32)],
        compiler_params=_params('parallel', 'arbitrary'),
    )(proj, proj, proj, proj, proj, wgu_pad, b_gate, o_norm)


def _gla_scan_bwd(proj, wgu_pad, b_gate, o_norm, states, dgated, heads, kw, vw, tb, name):
    seq, pw = proj.shape
    dk, dv = kw // heads, vw // heads
    cb = tb // CHUNK
    nt = seq // tb
    o = _gla_dims(pw, kw, vw, dk, dv)
    scale = dk ** -0.5

    def body(q_ref, k_ref, v_ref, r_ref, gl_ref, wgu_ref, b_ref, on_ref, st_ref, stp_ref, dg_ref,
             dq_ref, dk_ref, dv_ref, dr_ref, dpre_ref, db_ref, don_ref, ds_scr):
        hh = pl.program_id(0)
        t = pl.program_id(1)

        @pl.when(t == 0)
        def _():
            ds_scr[...] = jnp.zeros_like(ds_scr)
            db_ref[...] = jnp.zeros_like(db_ref)

        @pl.when(jnp.logical_and(hh == 0, t == 0))
        def _():
            don_ref[...] = jnp.zeros_like(don_ref)

        wgu = wgu_ref[...].astype(BF16)
        bias = b_ref[...]
        onorm = on_ref[...]
        has_prev = (t < nt - 1).astype(F32)
        r_i = lax.broadcasted_iota(jnp.int32, (CHUNK, CHUNK), 0)
        c_i = lax.broadcasted_iota(jnp.int32, (CHUNK, CHUNK), 1)
        strict = (c_i < r_i).astype(BF16)
        carry = ds_scr[...]
        db_acc = jnp.zeros((1, dk), F32)
        don_acc = jnp.zeros((1, dv), F32)
        for ci in reversed(range(cb)):
            rows = pl.ds(ci * CHUNK, CHUNK)
            pre, cum, total = _gla_gates(gl_ref[rows, :], wgu, bias)
            edec = jnp.exp(total - cum)
            decay = jnp.exp(total)
            kdec = k_ref[rows, :] * edec
            st = st_ref[ci]
            st_prev = st_ref[ci - 1] if ci > 0 else stp_ref[0] * has_prev
            stb = st.astype(BF16)
            qs = (q_ref[rows, :] * scale).astype(BF16)
            vb = v_ref[rows, :].astype(BF16)
            ov = lax.dot_general(qs, stb, (((1,), (1,)), ((), ())), preferred_element_type=F32)
            rstd = lax.rsqrt(jnp.mean(ov * ov, axis=-1, keepdims=True) + EPS)
            ohat = ov * rstd
            rv = r_ref[rows, :]
            sr = jax.nn.sigmoid(rv)
            dgv = dg_ref[rows, :]
            dy = dgv * (rv * sr)
            dr_ref[rows, :] = (dgv * (ohat * onorm) * (sr * (1.0 + rv * (1.0 - sr)))).astype(dr_ref.dtype)
            don_acc = don_acc + jnp.sum(dy * ohat, axis=0, keepdims=True)
            dohat = dy * onorm
            do = (rstd * (dohat - ohat * jnp.mean(dohat * ohat, axis=-1, keepdims=True))).astype(BF16)
            dq_ref[rows, :] = (jnp.dot(do, stb, preferred_element_type=F32) * scale).astype(dq_ref.dtype)
            dst = carry + lax.dot_general(do, qs, (((0,), (0,)), ((), ())), preferred_element_type=F32)
            dstb = dst.astype(BF16)
            dkdec = jnp.dot(vb, dstb, preferred_element_type=F32)
            dv_ref[rows, :] = lax.dot_general(kdec.astype(BF16), dstb, (((1,), (1,)), ((), ())),
                                              preferred_element_type=F32).astype(dv_ref.dtype)
            ddecay = jnp.sum(dst * st_prev, axis=0, keepdims=True)
            dk_ref[rows, :] = (dkdec * edec).astype(dk_ref.dtype)
            da = ddecay * decay + _tri_dot(strict, dkdec * kdec)
            dpre = da * (1.0 / GLA_GATE_TEMP) * (1.0 - jax.nn.sigmoid(pre))
            dpre_ref[rows, :] = dpre.astype(dpre_ref.dtype)
            db_acc = db_acc + jnp.sum(dpre, axis=0, keepdims=True)
            carry = dst * decay
        ds_scr[...] = carry
        db_ref[...] += db_acc
        don_ref[...] += don_acc

    rt = lambda t: nt - 1 - t
    in_specs = [
        pl.BlockSpec((tb, dk), lambda h, t: (rt(t), o['q0'] + h)),
        pl.BlockSpec((tb, dk), lambda h, t: (rt(t), o['k0'] + h)),
        pl.BlockSpec((tb, dv), lambda h, t: (rt(t), o['v0'] + h)),
        pl.BlockSpec((tb, dv), lambda h, t: (rt(t), o['r0'] + h)),
        pl.BlockSpec((tb, LANES), lambda h, t: (rt(t), o['g0'])),
        pl.BlockSpec((LANES, dk), lambda h, t: (0, h)),
        pl.BlockSpec((1, dk), lambda h, t: (0, h)),
        pl.BlockSpec((1, dv), lambda h, t: (0, 0)),
        pl.BlockSpec((None, cb, dv, dk), lambda h, t: (h, rt(t), 0, 0)),
        pl.BlockSpec((None, 1, dv, dk), lambda h, t: (h, jnp.maximum(rt(t) * cb - 1, 0), 0, 0)),
        pl.BlockSpec((tb, dv), lambda h, t: (rt(t), h)),
    ]
    out_shape = [jax.ShapeDtypeStruct((seq, kw), BF16), jax.ShapeDtypeStruct((seq, kw), BF16),
                 jax.ShapeDtypeStruct((seq, vw), BF16), jax.ShapeDtypeStruct((seq, vw), BF16),
                 jax.ShapeDtypeStruct((seq, kw), BF16),
                 jax.ShapeDtypeStruct((1, kw), F32), jax.ShapeDtypeStruct((1, dv), F32)]
    out_specs = [pl.BlockSpec((tb, dk), lambda h, t: (rt(t), h)),
                 pl.BlockSpec((tb, dk), lambda h, t: (rt(t), h)),
                 pl.BlockSpec((tb, dv), lambda h, t: (rt(t), h)),
                 pl.BlockSpec((tb, dv), lambda h, t: (rt(t), h)),
                 pl.BlockSpec((tb, dk), lambda h, t: (rt(t), h)),
                 pl.BlockSpec((1, dk), lambda h, t: (0, h)),
                 pl.BlockSpec((1, dv), lambda h, t: (0, 0))]
    return _pcall(
        body, name=name, grid=(heads, nt),
        out_shape=out_shape, in_specs=in_specs, out_specs=out_specs,
        scratch_shapes=[pltpu.VMEM((dv, dk), F32)],
        compiler_params=_params('arbitrary', 'arbitrary'),
    )(proj, proj, proj, proj, proj, wgu_pad, b_gate, o_norm, states, states, dgated)


def _cmul(ar, ai, br, bi):
    return ar * br - ai * bi, ar * bi + ai * br


def _gelu(y):
    c = math.sqrt(2.0 / math.pi)
    return 0.5 * y * (1.0 + jnp.tanh(c * (y + 0.044715 * y * y * y)))


def _gelu_grad(y):
    c = math.sqrt(2.0 / math.pi)
    th = jnp.tanh(c * (y + 0.044715 * y * y * y))
    return 0.5 * (1.0 + th) + 0.5 * y * (1.0 - th * th) * (c * (1.0 + 3.0 * 0.044715 * y * y))


def _power_pow2(ar, ai, n):
    assert n & (n - 1) == 0
    for _ in range(n.bit_length() - 1):
        ar, ai = _cmul(ar, ai, ar, ai)
    return ar, ai


def _s5_fwd(u, bre, bim, cre, cim, are, aim, dskip, name):
    seq, width = u.shape
    nb, ub, sb = bre.shape
    ls = seq // S5_SEGMENTS
    seg = S5_SEGMENTS

    def body(u_ref, bre_ref, bim_ref, cre_ref, cim_ref, are_ref, aim_ref, d_ref, y_ref, z_ref, xr_ref, xi_ref):
        uv = u_ref[...]
        ub16 = uv.astype(BF16)
        xr_ref[...] = jnp.dot(ub16, bre_ref[...].astype(BF16), preferred_element_type=F32)
        xi_ref[...] = jnp.dot(ub16, bim_ref[...].astype(BF16), preferred_element_type=F32)
        ar = jnp.broadcast_to(are_ref[...], (seg, sb))
        ai = jnp.broadcast_to(aim_ref[...], (seg, sb))

        def step(i, c):
            rows = pl.ds(pl.multiple_of(i * seg, seg), seg)
            pr, pi = _cmul(ar, ai, c[0], c[1])
            nr = pr + xr_ref[rows, :]
            ni = pi + xi_ref[rows, :]
            xr_ref[rows, :] = nr
            xi_ref[rows, :] = ni
            return nr, ni

        zero = jnp.zeros((seg, sb), F32)
        er, ei = lax.fori_loop(0, ls, step, (zero, zero), unroll=8)
        pr, pi = _power_pow2(ar, ai, ls)
        row = lax.broadcasted_iota(jnp.int32, (seg, sb), 0)
        sr, si = zero, zero
        for _ in range(seg - 1):
            tr, ti = _cmul(pr, pi, sr, si)
            sr = jnp.where(row == 0, 0.0, pltpu.roll(tr + er, 1, 0))
            si = jnp.where(row == 0, 0.0, pltpu.roll(ti + ei, 1, 0))

        def fix(i, c):
            rows = pl.ds(pl.multiple_of(i * seg, seg), seg)
            fr, fi = _cmul(c[0], c[1], sr, si)
            xr_ref[rows, :] += fr
            xi_ref[rows, :] += fi
            return _cmul(c[0], c[1], ar, ai)

        lax.fori_loop(0, ls, fix, (ar, ai), unroll=8)
        y = (jnp.dot(xr_ref[...].astype(BF16), cre_ref[...].astype(BF16), preferred_element_type=F32)
             - jnp.dot(xi_ref[...].astype(BF16), cim_ref[...].astype(BF16), preferred_element_type=F32)
             + d_ref[...] * uv)
        y_ref[...] = y
        z_ref[...] = _gelu(y).astype(z_ref.dtype)

    mat = lambda r, c: pl.BlockSpec((None, r, c), lambda b: (b, 0, 0))
    return _pcall(
        body, name=name, grid=(nb,),
        out_shape=[jax.ShapeDtypeStruct((seq, width), F32), jax.ShapeDtypeStruct((seq, width), BF16),
                   jax.ShapeDtypeStruct((seq, nb * sb), F32), jax.ShapeDtypeStruct((seq, nb * sb), F32)],
        in_specs=[pl.BlockSpec((seq, ub), lambda b: (0, b)), mat(ub, sb), mat(ub, sb), mat(sb, ub), mat(sb, ub),
                  mat(1, sb), mat(1, sb), pl.BlockSpec((1, ub), lambda b: (0, b))],
        out_specs=[pl.BlockSpec((seq, ub), lambda b: (0, b)), pl.BlockSpec((seq, ub), lambda b: (0, b)),
                   pl.BlockSpec((seq, sb), lambda b: (0, b)), pl.BlockSpec((seq, sb), lambda b: (0, b))],
        compiler_params=_params('parallel'),
    )(u, bre, bim, cre, cim, are, aim, dskip)


def _s5_bwd(dz, y, u, xr, xi, bre, bim, cre, cim, are, aim, dskip, name):
    seq, width = u.shape
    nb, ub, sb = bre.shape
    ls = seq // S5_SEGMENTS
    seg = S5_SEGMENTS

    def body(dz_ref, y_ref, u_ref, xr_ref, xi_ref, bre_ref, bim_ref, cre_ref, cim_ref, are_ref, aim_ref, d_ref,
             du_ref, dcr_ref, dci_ref, dbr_ref, dbi_ref, dar_ref, dai_ref, dd_ref, lr_ref, li_ref):
        uv = u_ref[...]
        dy = dz_ref[...] * _gelu_grad(y_ref[...])
        dd_ref[...] = jnp.sum(dy * uv, axis=0, keepdims=True)
        dyb = dy.astype(BF16)
        nt = (((1,), (1,)), ((), ()))
        tn = (((0,), (0,)), ((), ()))
        lr_ref[...] = lax.dot_general(dyb, cre_ref[...].astype(BF16), nt, preferred_element_type=F32)
        li_ref[...] = -lax.dot_general(dyb, cim_ref[...].astype(BF16), nt, preferred_element_type=F32)
        dcr_ref[...] = lax.dot_general(dyb, xr_ref[...].astype(BF16), tn, preferred_element_type=F32)
        dci_ref[...] = -lax.dot_general(dyb, xi_ref[...].astype(BF16), tn, preferred_element_type=F32)
        ar = jnp.broadcast_to(are_ref[...], (seg, sb))
        ai = jnp.broadcast_to(aim_ref[...], (seg, sb))
        nai = -ai

        def step(ii, c):
            rows = pl.ds(pl.multiple_of((ls - 1 - ii) * seg, seg), seg)
            pr, pi = _cmul(ar, nai, c[0], c[1])
            nr = pr + lr_ref[rows, :]
            ni = pi + li_ref[rows, :]
            lr_ref[rows, :] = nr
            li_ref[rows, :] = ni
            return nr, ni

        zero = jnp.zeros((seg, sb), F32)
        er, ei = lax.fori_loop(0, ls, step, (zero, zero), unroll=8)
        pr, pi = _power_pow2(ar, nai, ls)
        row = lax.broadcasted_iota(jnp.int32, (seg, sb), 0)
        rr, ri = zero, zero
        for _ in range(seg - 1):
            tr, ti = _cmul(pr, pi, rr, ri)
            rr = jnp.where(row == seg - 1, 0.0, pltpu.roll(tr + er, seg - 1, 0))
            ri = jnp.where(row == seg - 1, 0.0, pltpu.roll(ti + ei, seg - 1, 0))

        def corrected(rows, qr, qi):
            fr, fi = _cmul(qr, qi, rr, ri)
            nr = lr_ref[rows, :] + fr
            ni = li_ref[rows, :] + fi
            lr_ref[rows, :] = nr
            li_ref[rows, :] = ni
            return nr, ni

        def grad_a(nr, ni, xpr, xpi, accr, acci):
            return accr + nr * xpr + ni * xpi, acci + ni * xpr - nr * xpi

        def fix(ii, c):
            qr, qi, accr, acci = c
            i = ls - 1 - ii
            rows = pl.ds(pl.multiple_of(i * seg, seg), seg)
            prev = pl.ds(pl.multiple_of((i - 1) * seg, seg), seg)
            nr, ni = corrected(rows, qr, qi)
            accr, acci = grad_a(nr, ni, xr_ref[prev, :], xi_ref[prev, :], accr, acci)
            qr, qi = _cmul(qr, qi, ar, nai)
            return qr, qi, accr, acci

        qr, qi, accr, acci = lax.fori_loop(0, ls - 1, fix, (ar, nai, zero, zero), unroll=8)
        nr, ni = corrected(pl.ds(0, seg), qr, qi)
        last = pl.ds((ls - 1) * seg, seg)
        xpr = jnp.where(row == 0, 0.0, pltpu.roll(xr_ref[last, :], 1, 0))
        xpi = jnp.where(row == 0, 0.0, pltpu.roll(xi_ref[last, :], 1, 0))
        accr, acci = grad_a(nr, ni, xpr, xpi, accr, acci)
        dar_ref[...] = jnp.sum(accr, axis=0, keepdims=True)
        dai_ref[...] = jnp.sum(acci, axis=0, keepdims=True)
        lrb = lr_ref[...].astype(BF16)
        lib = li_ref[...].astype(BF16)
        ub16 = uv.astype(BF16)
        dbr_ref[...] = lax.dot_general(ub16, lrb, tn, preferred_element_type=F32)
        dbi_ref[...] = lax.dot_general(ub16, lib, tn, preferred_element_type=F32)
        du_ref[...] = (d_ref[...] * dy
                       + lax.dot_general(lrb, bre_ref[...].astype(BF16), nt, preferred_element_type=F32)
                       + lax.dot_general(lib, bim_ref[...].astype(BF16), nt, preferred_element_type=F32))

    mat = lambda r, c: pl.BlockSpec((None, r, c), lambda b: (b, 0, 0))
    col = lambda w: pl.BlockSpec((seq, w), lambda b: (0, b))
    return _pcall(
        body, name=name, grid=(nb,),
        out_shape=[jax.ShapeDtypeStruct((seq, width), F32)]
        + [jax.ShapeDtypeStruct((nb, ub, sb), F32)] * 4
        + [jax.ShapeDtypeStruct((nb, 1, sb), F32)] * 2
        + [jax.ShapeDtypeStruct((1, width), F32)],
        in_specs=[col(ub), col(ub), col(ub), col(sb), col(sb), mat(ub, sb), mat(ub, sb), mat(sb, ub), mat(sb, ub),
                  mat(1, sb), mat(1, sb), pl.BlockSpec((1, ub), lambda b: (0, b))],
        out_specs=[col(ub), mat(ub, sb), mat(ub, sb), mat(ub, sb), mat(ub, sb), mat(1, sb), mat(1, sb),
                   pl.BlockSpec((1, ub), lambda b: (0, b))],
        scratch_shapes=[pltpu.VMEM((seq, sb), F32), pltpu.VMEM((seq, sb), F32)],
        compiler_params=_params('parallel'),
    )(dz, y, u, xr, xi, bre, bim, cre, cim, are, aim, dskip)


def _s5_discretise(lam_re, lam_im, log_dt, b_re, b_im):
    lr = jnp.minimum(lam_re, S5_EIG_CLIP)
    li = lam_im
    dt = jnp.exp(log_dt)[:, None]
    mag = jnp.exp(lr * dt)
    ang = li * dt
    ab_re = mag * jnp.cos(ang)
    ab_im = mag * jnp.sin(ang)
    den = lr * lr + li * li
    nr = ab_re - 1.0
    f_re = (nr * lr + ab_im * li) / den
    f_im = (ab_im * lr - nr * li) / den
    bb_re = f_re[..., None] * b_re - f_im[..., None] * b_im
    bb_im = f_re[..., None] * b_im + f_im[..., None] * b_re
    return ab_re, ab_im, bb_re, bb_im


def _to_blocks(m):
    g, a, b = m.shape
    gb = S5_GROUPS_PER_BLOCK
    eye = jnp.eye(gb, dtype=m.dtype)
    return jnp.einsum('bgac,gh->bgahc', m.reshape(g // gb, gb, a, b), eye).reshape(g // gb, gb * a, gb * b)


def _from_blocks(m, a, b):
    nb = m.shape[0]
    gb = S5_GROUPS_PER_BLOCK
    eye = jnp.eye(gb, dtype=m.dtype)
    return jnp.einsum('bgahc,gh->bgac', m.reshape(nb, gb, a, gb, b), eye).reshape(nb * gb, a, b)


def _glu_fwd(o, h, name):
    half = o.shape[1] // 2

    def fn(ov, hv):
        return (hv + ov[:, :half] * jax.nn.sigmoid(ov[:, half:]),)
    return _rowwise(name, fn, [o, h], [], [(half, F32)], [], 256)[0]


def _glu_bwd(o, dout, name):
    half = o.shape[1] // 2

    def fn(ov, dv):
        val, gate = ov[:, :half], ov[:, half:]
        sg = jax.nn.sigmoid(gate)
        return (jnp.concatenate([dv * sg, dv * val * sg * (1.0 - sg)], axis=1),)
    return _rowwise(name, fn, [o, dout], [], [(2 * half, BF16)], [], 256)[0]


def _adam_math(w, g, m, v):
    m = ADAM_B1 * m + (1.0 - ADAM_B1) * g
    v = ADAM_B2 * v + (1.0 - ADAM_B2) * (g * g)
    m_hat = m / (1.0 - ADAM_B1 ** ADAM_STEP)
    v_hat = v / (1.0 - ADAM_B2 ** ADAM_STEP)
    delta = -ADAM_LR * (m_hat / (jnp.sqrt(v_hat) + ADAM_EPS) + ADAM_WD * w)
    return delta, m, v


def _adamw(w, m, v, grads, name):
    nl, rows, cols = w.shape
    tm = _tile(rows, max(8, (1 << 18) // cols // 8 * 8), unit=8)
    nbk = rows // tm

    def body(*refs):
        w_ref, m_ref, v_ref = refs[:3]
        g_refs = refs[3:3 + nl]
        go_ref, d_ref, mo_ref, vo_ref = refs[3 + nl:]
        layer = pl.program_id(0)
        g = g_refs[0][...]
        for l in range(1, nl):
            g = jnp.where(layer == l, g_refs[l][...], g)
        delta, mn, vn = _adam_math(w_ref[...], g, m_ref[...], v_ref[...])
        go_ref[...] = g
        d_ref[...] = delta
        mo_ref[...] = mn
        vo_ref[...] = vn

    stacked = pl.BlockSpec((None, tm, cols), lambda l, i: (l, i, 0))

    def g_spec(layer):
        return pl.BlockSpec((tm, cols), lambda l, i: (jnp.where(l == layer, i, jnp.where(l < layer, 0, nbk - 1)), 0))

    return _pcall(
        body, name=name, grid=(nl, nbk),
        out_shape=[jax.ShapeDtypeStruct(w.shape, F32)] * 4,
        in_specs=[stacked] * 3 + [g_spec(l) for l in range(nl)],
        out_specs=[stacked] * 4,
        compiler_params=_params('arbitrary', 'arbitrary'),
    )(w, m, v, *grads)


def _pack(arrs, rows_mult=512):
    flat = jnp.concatenate([a.reshape(-1) for a in arrs])
    total = flat.shape[0]
    rows = -(-total // LANES)
    rows = -(-rows // rows_mult) * rows_mult
    flat = jnp.pad(flat, (0, rows * LANES - total))
    return flat.reshape(rows, LANES)


def _unpack(packed, shapes):
    flat = packed.reshape(-1)
    out, off = [], 0
    for s in shapes:
        size = math.prod(s)
        out.append(flat[off:off + size].reshape(s))
        off += size
    return out


def _permute(a):
    seq, w = a.shape
    return a.reshape(S5_SEGMENTS, seq // S5_SEGMENTS, w).transpose(1, 0, 2).reshape(seq, w)


def _unpermute(a):
    seq, w = a.shape
    return a.reshape(seq // S5_SEGMENTS, S5_SEGMENTS, w).transpose(1, 0, 2).reshape(seq, w)


def kernel(x, gla_norm, gla_w_in, gla_w_gate_up, gla_b_gate, gla_o_norm, gla_w_out, s5_norm, s5_w_in, s5_lam_re, s5_lam_im, s5_log_dt, s5_b_re, s5_b_im, s5_c_re, s5_c_im, s5_d, s5_w_out, mlp_norm, mlp_w_up, mlp_w_down, final_norm, loss_target, m_gla_norm, m_gla_w_in, m_gla_w_gate_up, m_gla_b_gate, m_gla_o_norm, m_gla_w_out, m_s5_norm, m_s5_w_in, m_s5_lam_re, m_s5_lam_im, m_s5_log_dt, m_s5_b_re, m_s5_b_im, m_s5_c_re, m_s5_c_im, m_s5_d, m_s5_w_out, m_mlp_norm, m_mlp_w_up, m_mlp_w_down, m_final_norm, v_gla_norm, v_gla_w_in, v_gla_w_gate_up, v_gla_b_gate, v_gla_o_norm, v_gla_w_out, v_s5_norm, v_s5_w_in, v_s5_lam_re, v_s5_lam_im, v_s5_log_dt, v_s5_b_re, v_s5_b_im, v_s5_c_re, v_s5_c_im, v_s5_d, v_s5_w_out, v_mlp_norm, v_mlp_w_up, v_mlp_w_down, v_final_norm):
    weights = dict(gla_norm=gla_norm, gla_w_in=gla_w_in, gla_w_gate_up=gla_w_gate_up, gla_b_gate=gla_b_gate, gla_o_norm=gla_o_norm, gla_w_out=gla_w_out, s5_norm=s5_norm, s5_w_in=s5_w_in, s5_lam_re=s5_lam_re, s5_lam_im=s5_lam_im, s5_log_dt=s5_log_dt, s5_b_re=s5_b_re, s5_b_im=s5_b_im, s5_c_re=s5_c_re, s5_c_im=s5_c_im, s5_d=s5_d, s5_w_out=s5_w_out, mlp_norm=mlp_norm, mlp_w_up=mlp_w_up, mlp_w_down=mlp_w_down, final_norm=final_norm)
    mom1 = dict(gla_norm=m_gla_norm, gla_w_in=m_gla_w_in, gla_w_gate_up=m_gla_w_gate_up, gla_b_gate=m_gla_b_gate, gla_o_norm=m_gla_o_norm, gla_w_out=m_gla_w_out, s5_norm=m_s5_norm, s5_w_in=m_s5_w_in, s5_lam_re=m_s5_lam_re, s5_lam_im=m_s5_lam_im, s5_log_dt=m_s5_log_dt, s5_b_re=m_s5_b_re, s5_b_im=m_s5_b_im, s5_c_re=m_s5_c_re, s5_c_im=m_s5_c_im, s5_d=m_s5_d, s5_w_out=m_s5_w_out, mlp_norm=m_mlp_norm, mlp_w_up=m_mlp_w_up, mlp_w_down=m_mlp_w_down, final_norm=m_final_norm)
    mom2 = dict(gla_norm=v_gla_norm, gla_w_in=v_gla_w_in, gla_w_gate_up=v_gla_w_gate_up, gla_b_gate=v_gla_b_gate, gla_o_norm=v_gla_o_norm, gla_w_out=v_gla_w_out, s5_norm=v_s5_norm, s5_w_in=v_s5_w_in, s5_lam_re=v_s5_lam_re, s5_lam_im=v_s5_lam_im, s5_log_dt=v_s5_log_dt, s5_b_re=v_s5_b_re, s5_b_im=v_s5_b_im, s5_c_re=v_s5_c_re, s5_c_im=v_s5_c_im, s5_d=v_s5_d, s5_w_out=v_s5_w_out, mlp_norm=v_mlp_norm, mlp_w_up=v_mlp_w_up, mlp_w_down=v_mlp_w_down, final_norm=v_final_norm)
    names = list(weights)
    big = ['gla_w_in', 'gla_w_out', 's5_w_in', 's5_w_out', 'mlp_w_up', 'mlp_w_down']
    small = [n for n in names if n not in big]

    chip = 2 * lax.axis_index('x') + lax.axis_index('y')
    h0 = x[0]
    target = loss_target[0]
    seq, dm = h0.shape
    depth = mlp_norm.shape[0]
    n_gla = gla_norm.shape[0]
    n_s5 = s5_lam_re.shape[0]
    rank = gla_w_gate_up.shape[1]
    kw = gla_b_gate.shape[1]
    dv = gla_o_norm.shape[1]
    in_w = 4 * gla_w_in.shape[2]
    vw = (in_w - rank - 2 * kw) // 2
    heads = vw // dv
    dk = kw // heads
    pw = -(-in_w // LANES) * LANES
    s5w = s5_w_in.shape[2]
    n_grp, n_state, grp = s5_b_re.shape[1:]
    hid = 4 * mlp_w_up.shape[2]
    tb = min(seq, 8 * CHUNK)
    tm = _tile(seq, 1024)

    ic = lax.axis_index('c')
    rh = lambda w: w.shape[1] // 2
    wb16 = {n: weights[n].astype(BF16) for n in big}
    gathered = {n: [None] * weights[n].shape[0] for n in big}
    g_w_in, g_gla_out, g_s5_in, g_s5_out, g_up, g_down = (gathered[n] for n in big)
    in_flight = {}

    to_start = []
    for i in range(depth):
        mix = ['gla_w_in', 'gla_w_out'] if i % 2 == 0 else ['s5_w_in', 's5_w_out']
        to_start += [(m, i // 2) for m in mix] + [('mlp_w_up', i), ('mlp_w_down', i)]

    def start_gathers(after):
        while to_start and len(in_flight) < GATHERS_IN_FLIGHT:
            n, l = to_start.pop(0)
            rows, cols = weights[n].shape[1:]
            hd = _xy_start(wb16[n][l].reshape(2, rows // 2, cols), 'bcast_c', f'ag_{n}_{l}_start', after)
            in_flight[n, l] = hd
            after = [hd['token']]
        return after

    def finish_gather(n, l, after):
        sent, y1 = _xy_wait(in_flight.pop((n, l)), f'ag_{n}_{l}_wait', after)
        behind = start_gathers([y1])
        y1 = lax.dynamic_update_index_in_dim(y1, lax.dynamic_index_in_dim(sent, ic, 0, keepdims=False), chip, 0)
        gathered[n][l] = _exchange(y1, 'c', 'bcast', f'ag_{n}_c')
        return behind

    start_gathers([])
    sharded_small = [gla_w_gate_up, s5_norm, s5_d]
    gathered_small = _exchange(_pack(sharded_small), 'xy', 'bcast', 'ag_small')
    parts = [_unpack(gathered_small[k], [a.shape for a in sharded_small]) for k in range(4)]
    wgu_full = jnp.concatenate([p[0] for p in parts], axis=2)
    s5_norm_full = jnp.concatenate([p[1] for p in parts], axis=1)
    s5_d_full = jnp.concatenate([p[2] for p in parts], axis=1)

    def gla_w_in_padded(j):
        wj = g_w_in[j].transpose(0, 2, 1, 3).reshape(dm, in_w)
        return jnp.pad(wj, ((0, 0), (0, pw - in_w)))

    grads = {n: [None] * weights[n].shape[0] for n in names if n != 'final_norm'}

    saved = []
    h = h0
    for i in range(depth):
        j = i // 2
        rec = {}
        if i % 2 == 0:
            rec['h_in'] = h
            behind = finish_gather('gla_w_in', j, [h])
            hn = _norm_fwd(h, gla_norm[j:j + 1], 'gla_norm_fwd', behind)
            w_in_pad = gla_w_in_padded(j)
            proj = _mm('gla_proj', 'nn', _plain(hn), _plain(w_in_pad), (seq, pw, dm),
                       [((seq, pw), F32, _plain_shape(None))], (tm, _tile(pw, 1024), dm))[0]
            wgu_pad = jnp.pad(wgu_full[j], ((0, LANES - rank), (0, 0)))
            gated, states = _gla_scan_fwd(proj, wgu_pad, gla_b_gate[j:j + 1], gla_o_norm[j:j + 1],
                                          heads, kw, vw, tb, 'gla_scan_fwd')
            behind = finish_gather('gla_w_out', j, [gated])
            h = _mm('gla_out', 'nn', _plain(gated), _w_rows(g_gla_out, j), (seq, dm, vw),
                    [((seq, dm), F32, _plain_shape(None))],
                    (tm, _tile(dm, 1024), gla_w_out.shape[1]),
                    epilogue=lambda acc, hv: (acc + hv,), extras=[_plain(h)], after=behind)[0]
            rec.update(hn=hn, w_in_pad=w_in_pad, proj=proj, wgu_pad=wgu_pad, gated=gated, states=states)
        else:
            hp = _permute(h)
            rec['h_in'] = hp
            hn = _norm_fwd(hp, s5_norm_full[j:j + 1], 's5_norm_fwd')
            behind = finish_gather('s5_w_in', j, [hn])
            u = _mm('s5_in', 'nn', _plain(hn), _w_rows(g_s5_in, j), (seq, s5w, dm),
                    [((seq, s5w), F32, _plain_shape(None))],
                    (tm, _tile(s5w, 1024), s5_w_in.shape[1]), after=behind)[0]
            disc, disc_vjp = jax.vjp(_s5_discretise, s5_lam_re[j], s5_lam_im[j], s5_log_dt[j], s5_b_re[j], s5_b_im[j])
            ab_re, ab_im, bb_re, bb_im = disc
            bre = _to_blocks(bb_re.transpose(0, 2, 1))
            bim = _to_blocks(bb_im.transpose(0, 2, 1))
            cre = _to_blocks(s5_c_re[j].transpose(0, 2, 1))
            cim = _to_blocks(s5_c_im[j].transpose(0, 2, 1))
            nb = n_grp // S5_GROUPS_PER_BLOCK
            are = ab_re.reshape(nb, 1, S5_GROUPS_PER_BLOCK * n_state)
            aim = ab_im.reshape(nb, 1, S5_GROUPS_PER_BLOCK * n_state)
            dskip = s5_d_full[j:j + 1]
            y, z, xr, xi = _s5_fwd(u, bre, bim, cre, cim, are, aim, dskip, 's5_scan_fwd')
            behind = finish_gather('s5_w_out', j, [z])
            o = _mm('s5_out', 'nn', _plain(z), _w_cols(g_s5_out, j), (seq, 2 * dm, s5w),
                    [((seq, 2 * dm), F32, _plain_shape(None))],
                    (tm, _tile(s5_w_out.shape[2], 1024), s5_w_out.shape[1]), after=behind)[0]
            h = _unpermute(_glu_fwd(o, hp, 's5_glu_fwd'))
            rec.update(hn=hn, u=u, y=y, z=z, xr=xr, xi=xi, o=o, mats=(bre, bim, cre, cim, are, aim, dskip),
                       disc_vjp=disc_vjp)
        rec['h_mid'] = h
        hn2 = _norm_fwd(h, mlp_norm[i:i + 1], 'mlp_norm_fwd')
        behind = finish_gather('mlp_w_up', i, [hn2])
        act, act2 = _mm('mlp_up', 'nn', _plain(hn2), _w_cols(g_up, i), (seq, hid, dm),
                        [((seq, hid), BF16, _plain_shape(None))] * 2,
                        (tm, _tile(mlp_w_up.shape[2], 1024), mlp_w_up.shape[1]),
                        epilogue=lambda acc: (jnp.maximum(acc, 0.0), jnp.square(jnp.maximum(acc, 0.0))),
                        after=behind)
        behind = finish_gather('mlp_w_down', i, [act2])
        h = _mm('mlp_down', 'nn', _plain(act2), _w_rows(g_down, i), (seq, dm, hid),
                [((seq, dm), F32, _plain_shape(None))],
                (tm, _tile(dm, 1024), mlp_w_down.shape[1]),
                epilogue=lambda acc, hv: (acc + hv,), extras=[_plain(h)], after=behind)[0]
        rec.update(hn2=hn2, act=act, act2=act2)
        saved.append(rec)

    dh, loss_cols, d_final = _loss_head(h, target, final_norm.reshape(1, dm), 'loss_head')
    loss = lax.psum(jnp.sum(loss_cols), ('x', 'y', 'c'))
    grads['final_norm'] = [d_final.reshape(dm)]

    big_grads = {n: [None] * weights[n].shape[0] for n in big}
    reducing = []

    def reduce_begin(dw, n, l):
        _, _, rows_h, cols = dw.shape
        ya = _exchange(dw, 'c', 'a2a', f'rs_{n}_pair')
        pre = _sum_slots(ya.reshape(2, 4 * rows_h, cols), BF16, f'rs_{n}_pairsum').reshape(4, rows_h, cols)
        hd = _xy_start(pre, 'a2a', f'rs_{n}_{l}_start', [])
        reducing.append((n, l, hd))
        return [hd['token']]

    def reduce_end(after):
        n, l, hd = reducing.pop(0)
        sent, yb = _xy_wait(hd, f'rs_{n}_{l}_wait', after)
        yb = lax.dynamic_update_index_in_dim(yb, lax.dynamic_index_in_dim(sent, chip, 0, keepdims=False), chip, 0)
        fin = _sum_slots(yb, F32, f'rs_{n}_chipsum')
        yc = _exchange(fin, 'c', 'bcast', f'rs_{n}_back')
        big_grads[n][l] = yc.reshape(2 * fin.shape[0], fin.shape[1])

    def reduce_scatter(dw, n, l):
        started_ = reduce_begin(dw, n, l)
        while len(reducing) > 1:
            reduce_end(started_)
        return started_
    for i in reversed(range(depth)):
        j = i // 2
        rec = saved[i]
        r_dn, c_dn = mlp_w_down.shape[1:]
        shape, spec = _dw_rows(r_dn, c_dn)
        dw = _mm('mlp_down_dw', 'tn', _plain(rec['act2']), _plain(dh), (hid, dm, seq),
                 [(shape, BF16, spec)], (_tile(r_dn // 2, 1024), _tile(c_dn, 1024), seq))[0]
        behind = reduce_scatter(dw, 'mlp_w_down', i)
        dpre = _mm('mlp_down_dx', 'nt', _plain(dh), _w_rows(g_down, i), (seq, hid, dm),
                   [((seq, hid), BF16, _plain_shape(None))],
                   (tm, _tile(rh(mlp_w_down), 1024), dm),
                   epilogue=lambda acc, av: (acc * (2.0 * av.astype(F32)),), extras=[_plain(rec['act'])],
                   after=behind)[0]
        r_up, c_up = mlp_w_up.shape[1:]
        shape, spec = _dw_cols(r_up, c_up)
        dw = _mm('mlp_up_dw', 'tn', _plain(rec['hn2']), _plain(dpre), (dm, hid, seq),
                 [(shape, BF16, spec)], (_tile(r_up // 2, 1024), _tile(c_up, 1024), seq))[0]
        behind = reduce_scatter(dw, 'mlp_w_up', i)
        dhn = _mm('mlp_up_dx', 'nt', _plain(dpre), _w_cols(g_up, i), (seq, dm, hid),
                  [((seq, dm), F32, _plain_shape(None))],
                  (tm, _tile(rh(mlp_w_up), 1024), _tile(mlp_w_up.shape[2], 2048)), after=behind)[0]
        dh, dg = _norm_bwd(rec['h_mid'], dhn, dh, mlp_norm[i:i + 1], 'mlp_norm_bwd')
        grads['mlp_norm'][i] = dg[0]

        if i % 2 == 0:
            r_o, c_o = gla_w_out.shape[1:]
            shape, spec = _dw_rows(r_o, c_o)
            dw = _mm('gla_out_dw', 'tn', _plain(rec['gated']), _plain(dh), (vw, dm, seq),
                     [(shape, BF16, spec)], (_tile(r_o // 2, 1024), _tile(c_o, 1024), seq))[0]
            behind = reduce_scatter(dw, 'gla_w_out', j)
            dgated = _mm('gla_out_dx', 'nt', _plain(dh), _w_rows(g_gla_out, j), (seq, vw, dm),
                         [((seq, vw), F32, _plain_shape(None))],
                         (tm, _tile(rh(gla_w_out), 1024), dm), after=behind)[0]
            dq, dkk, dvv, dr, dpre_g, db, don = _gla_scan_bwd(
                rec['proj'], rec['wgu_pad'], gla_b_gate[j:j + 1], gla_o_norm[j:j + 1], rec['states'], dgated,
                heads, kw, vw, tb, 'gla_scan_bwd')
            grads['gla_b_gate'][j] = db[0]
            grads['gla_o_norm'][j] = don[0]
            dgl = _mm('gla_gate_dx', 'nt', _plain(dpre_g), _plain(rec['wgu_pad']), (seq, LANES, kw),
                      [((seq, LANES), BF16, _plain_shape(None))], (tm, LANES, kw))[0]
            g_low = rec['proj'][:, pw - LANES:]
            dwgu = _mm('gla_gate_dw', 'tn', _plain(g_low), _plain(dpre_g), (LANES, kw, seq),
                       [((LANES, kw), F32, _plain_shape(None))], (LANES, kw, seq))[0]
            grads['gla_w_gate_up'][j] = dwgu[:rank]
            dproj = jnp.concatenate([dq, dkk, dvv, dr, dgl], axis=1)
            dw_pad = _mm('gla_proj_dw', 'tn', _plain(rec['hn']), _plain(dproj), (dm, pw, seq),
                         [((dm, pw), BF16, _plain_shape(None))], (_tile(dm, 1024), _tile(pw, 1024), seq))[0]
            shard_w = in_w // 4
            dw = dw_pad[:, :in_w].reshape(2, dm // 2, 4, shard_w).transpose(0, 2, 1, 3)
            behind = reduce_scatter(dw, 'gla_w_in', j)
            dhn = _mm('gla_proj_dx', 'nt', _plain(dproj), _plain(rec['w_in_pad']), (seq, dm, pw),
                      [((seq, dm), F32, _plain_shape(None))], (tm, _tile(dm, 1024), _tile(pw, 1024)),
                      after=behind)[0]
            dh, dg = _norm_bwd(rec['h_in'], dhn, dh, gla_norm[j:j + 1], 'gla_norm_bwd')
            grads['gla_norm'][j] = dg[0]
        else:
            dhp = _permute(dh)
            do = _glu_bwd(rec['o'], dhp, 's5_glu_bwd')
            r_o, c_o = s5_w_out.shape[1:]
            shape, spec = _dw_cols(r_o, c_o)
            dw = _mm('s5_out_dw', 'tn', _plain(rec['z']), _plain(do), (s5w, 2 * dm, seq),
                     [(shape, BF16, spec)], (_tile(r_o // 2, 1024), _tile(c_o, 1024), seq))[0]
            behind = reduce_scatter(dw, 's5_w_out', j)
            dz = _mm('s5_out_dx', 'nt', _plain(do), _w_cols(g_s5_out, j), (seq, s5w, 2 * dm),
                     [((seq, s5w), F32, _plain_shape(None))],
                     (tm, _tile(rh(s5_w_out), 1024), _tile(s5_w_out.shape[2], 1024)), after=behind)[0]
            bre, bim, cre, cim, are, aim, dskip = rec['mats']
            du, dcr, dci, dbr, dbi, dar, dai, dd = _s5_bwd(dz, rec['y'], rec['u'], rec['xr'], rec['xi'],
                                                           bre, bim, cre, cim, are, aim, dskip, 's5_scan_bwd')
            grads['s5_c_re'][j] = _from_blocks(dcr, grp, n_state)
            grads['s5_c_im'][j] = _from_blocks(dci, grp, n_state)
            dbb_re = _from_blocks(dbr, grp, n_state).transpose(0, 2, 1)
            dbb_im = _from_blocks(dbi, grp, n_state).transpose(0, 2, 1)
            d_lr, d_li, d_dt, d_bre, d_bim = rec['disc_vjp'](
                (dar.reshape(n_grp, n_state), dai.reshape(n_grp, n_state), dbb_re, dbb_im))
            grads['s5_lam_re'][j] = d_lr
            grads['s5_lam_im'][j] = d_li
            grads['s5_log_dt'][j] = d_dt
            grads['s5_b_re'][j] = d_bre
            grads['s5_b_im'][j] = d_bim
            grads['s5_d'][j] = dd[0]
            r_i, c_i = s5_w_in.shape[1:]
            shape, spec = _dw_rows(r_i, c_i)
            dw = _mm('s5_in_dw', 'tn', _plain(rec['hn']), _plain(du), (dm, s5w, seq),
                     [(shape, BF16, spec)], (_tile(r_i // 2, 1024), _tile(c_i, 1024), seq))[0]
            behind = reduce_scatter(dw, 's5_w_in', j)
            dhn = _mm('s5_in_dx', 'nt', _plain(du), _w_rows(g_s5_in, j), (seq, dm, s5w),
                      [((seq, dm), F32, _plain_shape(None))],
                      (tm, _tile(rh(s5_w_in), 1024), _tile(s5w, 1024)), after=behind)[0]
            dhp, dg = _norm_bwd(rec['h_in'], dhn, dhp, s5_norm_full[j:j + 1], 's5_norm_bwd')
            dh = _unpermute(dhp)
            grads['s5_norm'][j] = dg[0]
    grad_x = dh[None]
    while reducing:
        reduce_end([dh])

    local_small = [jnp.stack(grads[n]) if n != 'final_norm' else grads[n][0] for n in small]
    full_shapes = [a.shape for a in local_small]
    gathered = _exchange(_exchange(_pack(local_small), 'xy', 'bcast', 'ar_small_xy'), 'c', 'bcast', 'ar_small_c')
    rows = gathered.shape[2]
    summed = _sum_slots(gathered.reshape(8, rows, LANES), F32, 'ar_small_sum')
    small_full = dict(zip(small, _unpack(summed, full_shapes)))
    small_grad = {}
    for n in small:
        g = small_full[n]
        if g.shape != weights[n].shape:
            ax = [a for a in range(g.ndim) if g.shape[a] != weights[n].shape[a]][0]
            g = lax.dynamic_slice_in_dim(g, chip * weights[n].shape[ax], weights[n].shape[ax], axis=ax)
        small_grad[n] = g

    out_g, out_d, out_m, out_v = {}, {}, {}, {}
    for n in big:
        out_g[n], out_d[n], out_m[n], out_v[n] = _adamw(weights[n], mom1[n], mom2[n], big_grads[n], 'adamw_' + n)
    shapes = [weights[n].shape for n in small]
    pw_, pm_, pv_, pg_ = (_pack([d[n] for n in small]) for d in (weights, mom1, mom2, small_grad))
    _, sd, sm, sv = _adamw(pw_[None], pm_[None], pv_[None], [pg_], 'adamw_small')
    for n, d_, m_, v_ in zip(small, _unpack(sd[0], shapes), _unpack(sm[0], shapes), _unpack(sv[0], shapes)):
        out_g[n], out_d[n], out_m[n], out_v[n] = small_grad[n], d_, m_, v_

    return (loss, grad_x, *[out_g[n] for n in names], *[out_d[n] for n in names],
            *[out_m[n] for n in names], *[out_v[n] for n in names])
```

```python
import functools
import math

import jax
import jax.numpy as jnp
from jax import lax
from jax.experimental import pallas as pl
from jax.experimental.pallas import tpu as pltpu

F32 = jnp.float32
BF16 = jnp.bfloat16

EPS = 1e-6
CHUNK = 64
GLA_GATE_TEMP = 16.0
S5_EIG_CLIP = -1e-4
S5_SEGMENTS = 8
S5_GROUPS_PER_BLOCK = 8
LANES = 128
ADAM_LR = 0.001
ADAM_B1 = 0.9
ADAM_B2 = 0.999
ADAM_EPS = 1e-08
ADAM_WD = 0.01
ADAM_STEP = 10
VMEM_LIMIT_BYTES = 56 * 1024 * 1024
PAIR_PIECE_BYTES = 2 * 1024 * 1024
GATHERS_IN_FLIGHT = 2
PAIR_VMEM_BYTES = 40 * 1024 * 1024

MESH = pl.DeviceIdType.MESH
ANY = pl.BlockSpec(memory_space=pl.ANY)
IN_VMEM = pl.BlockSpec(memory_space=pltpu.VMEM)
IN_HBM = pl.BlockSpec(memory_space=pltpu.HBM)
IN_SEM = pl.BlockSpec(memory_space=pltpu.SEMAPHORE)
DATAFLOW = pltpu.SideEffectType.DATAFLOW_SIDE_EFFECTING


def _pcall(body, **kw):
    return pl.pallas_call(body, **kw)


def _params(*sem):
    return pltpu.CompilerParams(dimension_semantics=sem, vmem_limit_bytes=VMEM_LIMIT_BYTES)


def _tile(dim, target, unit=LANES):
    if dim <= target:
        return dim
    best = None
    for t in range(unit, target + 1, unit):
        if dim % t == 0:
            best = t
    assert best is not None, (dim, target)
    return best


def _exchange(x, group, mode, name):
    n = 2 if group == 'c' else 4
    blk = x.shape if mode == 'bcast' else x.shape[1:]
    if mode == 'a2a':
        assert x.shape[0] == n
    flips = [(0, 0, 1)] if group == 'c' else [(1, 0, 0), (0, 1, 0), (1, 1, 0)]
    itemsize = jnp.dtype(x.dtype).itemsize
    staged = group == 'c' and (x.size + n * math.prod(blk)) * itemsize <= PAIR_VMEM_BYTES
    if group == 'c' and not staged:
        ic = lax.axis_index('c')
        own = x if mode == 'bcast' else lax.dynamic_index_in_dim(x, ic, 0, keepdims=False)
        return lax.dynamic_update_index_in_dim(_pair_exchange_chunked(x, mode, name), own, ic, 0)

    def body(x_ref, y_ref, send_sems, recv_sems, local_sem):
        ix, iy, ic = lax.axis_index('x'), lax.axis_index('y'), lax.axis_index('c')

        def slot(px, py, pc):
            return pc if group == 'c' else 2 * px + py

        def src(px, py, pc):
            if mode == 'a2a':
                return x_ref.at[slot(px, py, pc)]
            if mode == 'bcast_c':
                return x_ref.at[ic]
            return x_ref

        me = (ix, iy, ic)
        local = pltpu.make_async_copy(src(*me), y_ref.at[slot(*me)], local_sem)
        local.start()
        peers = []
        for fx, fy, fc in flips:
            peers.append((1 - ix if fx else ix, 1 - iy if fy else iy, 1 - ic if fc else ic))
        sends = []
        for k, peer in enumerate(peers):
            cp = pltpu.make_async_remote_copy(
                src_ref=src(*peer), dst_ref=y_ref.at[slot(*me)],
                send_sem=send_sems.at[k], recv_sem=recv_sems.at[k],
                device_id=peer, device_id_type=MESH)
            cp.start()
            sends.append(cp)
        for k, peer in enumerate(peers):
            pltpu.make_async_remote_copy(
                src_ref=src(*peer), dst_ref=y_ref.at[slot(*peer)],
                send_sem=send_sems.at[k], recv_sem=recv_sems.at[k],
                device_id=peer, device_id_type=MESH).wait_recv()
        for cp in sends:
            cp.wait_send()
        local.wait()

    return _pcall(
        body, name=name,
        out_shape=jax.ShapeDtypeStruct((n,) + tuple(blk), x.dtype),
        in_specs=[IN_VMEM if staged else ANY], out_specs=IN_VMEM if staged else ANY,
        scratch_shapes=[pltpu.SemaphoreType.DMA((len(flips),)),
                        pltpu.SemaphoreType.DMA((len(flips),)),
                        pltpu.SemaphoreType.DMA(())],
        compiler_params=pltpu.CompilerParams(vmem_limit_bytes=VMEM_LIMIT_BYTES),
    )(x)


def _split_axis(blk, dtype, piece_bytes):
    itemsize = jnp.dtype(dtype).itemsize
    sublanes = 8 * 4 // itemsize
    want = max(1, math.prod(blk) * itemsize // piece_bytes)
    for pieces in [s for s in (64, 32, 16, 8, 4, 2) if s <= want]:
        for ax in range(len(blk) - 1):
            unit = sublanes if ax == len(blk) - 2 else 1
            if blk[ax] % (pieces * unit) == 0:
                return ax, pieces
    return 0, 1


def _pair_exchange_chunked(x, mode, name, reduce=False):
    assert mode == 'a2a' or not reduce
    blk = x.shape if mode == 'bcast' else x.shape[1:]
    ax, pieces = _split_axis(blk, x.dtype, PAIR_PIECE_BYTES)
    step = blk[ax] // pieces
    piece_shape = tuple(blk[:ax]) + (step,) + tuple(blk[ax + 1:])

    def piece(ref, p):
        return ref.at[(slice(None),) * ax + (pl.ds(p * step, step),)]

    def body(x_ref, y_ref, out_buf, in_buf, own_buf, send_sems, recv_sems, stage_sems, drain_sems, own_sems, credit_sem):
        ix, iy, ic = lax.axis_index('x'), lax.axis_index('y'), lax.axis_index('c')
        sibling = (ix, iy, 1 - ic)
        mine = x_ref.at[ic] if mode != 'bcast' else x_ref
        theirs = x_ref.at[1 - ic] if mode == 'a2a' else mine

        def own(p):
            return pltpu.make_async_copy(piece(mine, p), own_buf.at[p % 2], own_sems.at[p % 2])

        def stage(p):
            return pltpu.make_async_copy(piece(theirs, p), out_buf.at[p % 2], stage_sems.at[p % 2])

        def remote(p):
            return pltpu.make_async_remote_copy(
                src_ref=out_buf.at[p % 2], dst_ref=in_buf.at[p % 2],
                send_sem=send_sems.at[p], recv_sem=recv_sems.at[p],
                device_id=sibling, device_id_type=MESH)

        def drain(p):
            dst = y_ref if reduce else y_ref.at[1 - ic]
            return pltpu.make_async_copy(in_buf.at[p % 2], piece(dst, p), drain_sems.at[p % 2])

        stage(0).start()
        if reduce:
            own(0).start()
        for p in range(pieces):
            stage(p).wait()
            if p >= 2:
                pl.semaphore_wait(credit_sem, 1)
            remote(p).start()
            if p + 1 < pieces:
                if p >= 1:
                    remote(p - 1).wait_send()
                stage(p + 1).start()
                if reduce:
                    own(p + 1).start()
            remote(p).wait_recv()
            if reduce:
                own(p).wait()
                in_buf[p % 2] = (in_buf[p % 2].astype(F32) + own_buf[p % 2].astype(F32)).astype(in_buf.dtype)
            drain(p).start()
            drain(p).wait()
            if p + 2 < pieces:
                pl.semaphore_signal(credit_sem, inc=1, device_id=sibling, device_id_type=MESH)
        for p in range(max(0, pieces - 2), pieces):
            remote(p).wait_send()

    return _pcall(
        body, name=name,
        out_shape=jax.ShapeDtypeStruct(tuple(blk) if reduce else (2,) + tuple(blk), x.dtype),
        in_specs=[ANY], out_specs=ANY,
        scratch_shapes=[pltpu.VMEM((2,) + piece_shape, x.dtype), pltpu.VMEM((2,) + piece_shape, x.dtype),
                        pltpu.VMEM((2,) + piece_shape if reduce else (2, 8, LANES), x.dtype),
                        pltpu.SemaphoreType.DMA((pieces,)), pltpu.SemaphoreType.DMA((pieces,)),
                        pltpu.SemaphoreType.DMA((2,)), pltpu.SemaphoreType.DMA((2,)), pltpu.SemaphoreType.DMA((2,)),
                        pltpu.SemaphoreType.REGULAR],
        compiler_params=pltpu.CompilerParams(vmem_limit_bytes=VMEM_LIMIT_BYTES),
    )(x)


_XY_FLIPS = [(1, 0), (0, 1), (1, 1)]


def _xy_copies(mode, x_ref, land_ref, sems):
    ix, iy, ic = lax.axis_index('x'), lax.axis_index('y'), lax.axis_index('c')
    out = []
    for k, (fx, fy) in enumerate(_XY_FLIPS):
        px, py = (1 - ix if fx else ix), (1 - iy if fy else iy)
        if mode == 'a2a':
            src = x_ref.at[2 * px + py]
        elif mode == 'bcast_c':
            src = x_ref.at[ic]
        else:
            src = x_ref
        mk = lambda dst, src=src, k=k, px=px, py=py: pltpu.make_async_remote_copy(
            src_ref=src, dst_ref=dst, send_sem=sems[k], recv_sem=sems[3 + k],
            device_id=(px, py, ic), device_id_type=MESH)
        out.append((mk(land_ref.at[2 * ix + iy]), mk(land_ref.at[2 * px + py])))
    return out


def _xy_start(x, mode, name, after):
    blk = x.shape if mode == 'bcast' else x.shape[1:]
    land_shape = (4,) + tuple(blk)
    n_after = len(after)

    def body(*refs):
        x_ref, land_ref = refs[0], refs[1]
        sems = refs[2 + n_after:8 + n_after]
        for send, _ in _xy_copies(mode, x_ref, land_ref, sems):
            send.start()
        refs[-1][...] = jnp.zeros_like(refs[-1])

    outs = _pcall(
        body, name=name,
        out_shape=(pltpu.SemaphoreType.DMA(()),) * 6
        + (pltpu.HBM(x.shape, x.dtype), pltpu.HBM(land_shape, x.dtype), jax.ShapeDtypeStruct((8, LANES), F32)),
        in_specs=(IN_HBM, IN_HBM) + (ANY,) * n_after,
        out_specs=(IN_SEM,) * 6 + (IN_HBM, IN_HBM, IN_VMEM),
        input_output_aliases={0: 6, 1: 7},
        compiler_params=pltpu.CompilerParams(has_side_effects=DATAFLOW),
    )(pltpu.with_memory_space_constraint(x, pltpu.HBM),
      pltpu.with_memory_space_constraint(lax.empty(land_shape, x.dtype), pltpu.HBM), *after)
    return dict(sems=outs[:6], sent=outs[6], land=outs[7], token=outs[8], mode=mode)


def _xy_wait(handle, name, after):
    mode = handle['mode']
    n_after = len(after)

    def body(*refs):
        x_ref, land_ref = refs[0], refs[1]
        for _, recv in _xy_copies(mode, x_ref, land_ref, refs[2:8]):
            recv.wait_send()
            recv.wait_recv()

    sent, land = handle['sent'], handle['land']
    return _pcall(
        body, name=name,
        out_shape=(pltpu.HBM(sent.shape, sent.dtype), pltpu.HBM(land.shape, land.dtype)),
        in_specs=(IN_HBM, IN_HBM) + (IN_SEM,) * 6 + (ANY,) * n_after,
        out_specs=(IN_HBM, IN_HBM),
        input_output_aliases={0: 0, 1: 1},
        compiler_params=pltpu.CompilerParams(has_side_effects=DATAFLOW),
    )(sent, land, *handle['sems'], *after)


def _sum_slots(y, out_dtype, name):
    n, rows, cols = y.shape
    tm = _tile(rows, max(8, (1 << 20) // (n * cols) // 8 * 8), unit=8)

    def body(y_ref, o_ref):
        acc = y_ref[0].astype(F32)
        for k in range(1, n):
            acc = acc + y_ref[k].astype(F32)
        o_ref[...] = acc.astype(o_ref.dtype)

    return _pcall(
        body, name=name, grid=(rows // tm,),
        out_shape=jax.ShapeDtypeStruct((rows, cols), out_dtype),
        in_specs=[pl.BlockSpec((n, tm, cols), lambda i: (0, i, 0))],
        out_specs=pl.BlockSpec((tm, cols), lambda i: (i, 0)),
        compiler_params=_params('parallel'),
    )(y)


class _Op:
    def __init__(self, arr, spec):
        self.arr = arr
        self.spec = spec


def _plain(arr):
    return _Op(arr, lambda t0, t1: ((t0, t1), lambda b0, b1: (b0, b1)))


def _plain_shape(shape):
    return lambda t0, t1: ((t0, t1), lambda b0, b1: (b0, b1))


def _dw_cols(rows, cols):
    rh = rows // 2

    def spec(t0, t1):
        assert (rh % t0 == 0 or t0 == rows) and cols % t1 == 0, (rh, cols, t0, t1)
        qr, qc = max(1, rh // t0), cols // t1
        if t0 == rows:
            return (2, None, rh, t1), lambda b0, b1: (0, b1 // qc, 0, b1 % qc)
        return (None, None, t0, t1), lambda b0, b1: (b0 // qr, b1 // qc, b0 % qr, b1 % qc)
    return (2, 4, rh, cols), spec


def _dw_rows(rows, cols):
    rh = rows // 2

    def spec(t0, t1):
        assert (rh % t0 == 0 or t0 == rows) and cols % t1 == 0, (rh, cols, t0, t1)
        qr = max(1, rh // t0)
        if t0 == rows:
            return (2, None, rh, t1), lambda b0, b1: (0, b0, 0, b1)
        return (None, None, t0, t1), lambda b0, b1: ((b0 // qr) % 2, b0 // (2 * qr), b0 % qr, b1)
    return (2, 4, rh, cols), spec


def _w_cols(g, j):
    _, _, rh, cols = g[j].shape
    return _Op(g[j], _dw_cols(2 * rh, cols)[1])


def _w_rows(g, j):
    _, _, rh, cols = g[j].shape
    return _Op(g[j], _dw_rows(2 * rh, cols)[1])


def _mm(name, mode, a, b, dims, outs, tiles, epilogue=None, extras=(), after=()):
    m, n, k = dims
    tm, tn, tk = tiles
    assert m % tm == 0 and n % tn == 0 and k % tk == 0, (name, dims, tiles)
    nk = k // tk
    if mode == 'nn':
        a_t, a_ix, b_t, b_ix, ca, cb = (tm, tk), (lambda i, j, kk: (i, kk)), (tk, tn), (lambda i, j, kk: (kk, j)), 1, 0
    elif mode == 'nt':
        a_t, a_ix, b_t, b_ix, ca, cb = (tm, tk), (lambda i, j, kk: (i, kk)), (tn, tk), (lambda i, j, kk: (j, kk)), 1, 1
    else:
        a_t, a_ix, b_t, b_ix, ca, cb = (tk, tm), (lambda i, j, kk: (kk, i)), (tk, tn), (lambda i, j, kk: (kk, j)), 0, 0
    a_blk, a_fn = a.spec(*a_t)
    b_blk, b_fn = b.spec(*b_t)
    in_specs = [pl.BlockSpec(a_blk, lambda i, j, kk: a_fn(*a_ix(i, j, kk))),
                pl.BlockSpec(b_blk, lambda i, j, kk: b_fn(*b_ix(i, j, kk)))]
    operands = [a.arr, b.arr]
    for e in extras:
        e_blk, e_fn = e.spec(tm, tn)
        in_specs.append(pl.BlockSpec(e_blk, functools.partial(lambda i, j, kk, f: f(i, j), f=e_fn)))
        operands.append(e.arr)
    out_shapes, out_specs = [], []
    for shape, dtype, spec in outs:
        o_blk, o_fn = spec(tm, tn)
        out_shapes.append(jax.ShapeDtypeStruct(shape, dtype))
        out_specs.append(pl.BlockSpec(o_blk, functools.partial(lambda i, j, kk, f: f(i, j), f=o_fn)))
    n_ex, n_out = len(extras), len(outs)
    in_specs += [ANY] * len(after)
    operands += list(after)
    if epilogue is None:
        epilogue = lambda acc: (acc,)

    def body(a_ref, b_ref, *rest):
        ex_refs = rest[:n_ex]
        rest = rest[:n_ex] + rest[n_ex + len(after):]
        out_refs = rest[n_ex:n_ex + n_out]
        bv = b_ref[...]
        if bv.ndim == 3:
            bv = bv.reshape(bv.shape[0] * bv.shape[1], bv.shape[2])
        p = lax.dot_general(a_ref[...].astype(BF16), bv.astype(BF16),
                            (((ca,), (cb,)), ((), ())), preferred_element_type=F32)

        def finish(acc):
            res = epilogue(acc, *[r[...] for r in ex_refs])
            for o_ref, val in zip(out_refs, res):
                o_ref[...] = val.astype(o_ref.dtype)

        if nk == 1:
            finish(p)
        else:
            acc_ref = rest[n_ex + n_out]
            kk = pl.program_id(2)

            @pl.when(kk == 0)
            def _():
                acc_ref[...] = p

            @pl.when(kk > 0)
            def _():
                acc_ref[...] += p

            @pl.when(kk == nk - 1)
            def _():
                finish(acc_ref[...])

    res = _pcall(
        body, name=name, grid=(m // tm, n // tn, nk),
        out_shape=out_shapes, in_specs=in_specs, out_specs=out_specs,
        scratch_shapes=[pltpu.VMEM((tm, tn), F32)] if nk > 1 else [],
        compiler_params=_params('parallel', 'parallel', 'arbitrary'),
    )(*operands)
    return res


def _rowwise(name, fn, row_ins, vec_ins, outs, reds, tm, after=()):
    rows = row_ins[0].shape[0]
    assert rows % tm == 0
    n_in = len(row_ins) + len(vec_ins)
    n_out = len(outs)

    def body(*refs):
        vals = [r[...] for r in refs[:n_in]]
        refs = refs[:n_in] + refs[n_in + len(after):]
        res = fn(*vals)
        for o_ref, val in zip(refs[n_in:n_in + n_out], res[:n_out]):
            o_ref[...] = val.astype(o_ref.dtype)
        first = pl.program_id(0) == 0
        for r_ref, val in zip(refs[n_in + n_out:], res[n_out:]):
            @pl.when(first)
            def _(r_ref=r_ref, val=val):
                r_ref[...] = val

            @pl.when(jnp.logical_not(first))
            def _(r_ref=r_ref, val=val):
                r_ref[...] += val

    in_specs = [pl.BlockSpec((tm, a.shape[1]), lambda i: (i, 0)) for a in row_ins]
    in_specs += [pl.BlockSpec((1, v.shape[1]), lambda i: (0, 0)) for v in vec_ins]
    in_specs += [ANY] * len(after)
    out_shapes = [jax.ShapeDtypeStruct((rows, w), dt) for w, dt in outs]
    out_shapes += [jax.ShapeDtypeStruct((1, w), F32) for w in reds]
    out_specs = [pl.BlockSpec((tm, w), lambda i: (i, 0)) for w, _ in outs]
    out_specs += [pl.BlockSpec((1, w), lambda i: (0, 0)) for w in reds]
    return _pcall(
        body, name=name, grid=(rows // tm,),
        out_shape=out_shapes, in_specs=in_specs, out_specs=out_specs,
        compiler_params=_params('arbitrary'),
    )(*row_ins, *vec_ins, *after)


def _norm_fwd(h, g, name, after=()):
    def fn(hv, gv):
        rstd = lax.rsqrt(jnp.mean(hv * hv, axis=-1, keepdims=True) + EPS)
        return (hv * rstd * gv,)
    return _rowwise(name, fn, [h], [g], [(h.shape[1], BF16)], [], 256, after)[0]


def _norm_bwd(h, dhn, dres, g, name):
    def fn(hv, dv, rv, gv):
        rstd = lax.rsqrt(jnp.mean(hv * hv, axis=-1, keepdims=True) + EPS)
        xhat = hv * rstd
        dxhat = dv * gv
        dh = rv + rstd * (dxhat - xhat * jnp.mean(dxhat * xhat, axis=-1, keepdims=True))
        return dh, jnp.sum(dv * xhat, axis=0, keepdims=True)
    w = h.shape[1]
    return _rowwise(name, fn, [h, dhn, dres], [g], [(w, F32)], [w], 256)


def _loss_head(h, target, g, name):
    w = h.shape[1]

    def fn(hv, tv, gv):
        rstd = lax.rsqrt(jnp.mean(hv * hv, axis=-1, keepdims=True) + EPS)
        xhat = hv * rstd
        diff = xhat * gv - tv
        dy = diff * (1.0 / w)
        dxhat = dy * gv
        dh = rstd * (dxhat - xhat * jnp.mean(dxhat * xhat, axis=-1, keepdims=True))
        return (dh, jnp.sum(0.5 * dy * diff, axis=0, keepdims=True),
                jnp.sum(dy * xhat, axis=0, keepdims=True))
    return _rowwise(name, fn, [h, target], [g], [(w, F32)], [w, w], 256)


def _split3(x):
    hi = x.astype(BF16)
    r1 = x - hi.astype(F32)
    mid = r1.astype(BF16)
    lo = (r1 - mid.astype(F32)).astype(BF16)
    return hi, mid, lo


def _tri_dot(tri, x):
    hi, mid, lo = _split3(x)
    d = lambda p: jnp.dot(tri, p, preferred_element_type=F32)
    return d(hi) + d(mid) + d(lo)


def _log_sigmoid(x):
    return jnp.minimum(x, 0.0) - jnp.log(1.0 + jnp.exp(-jnp.abs(x)))


def _gla_dims(proj_w, kw, vw, dk, dv):
    assert kw % dk == 0 and (2 * kw) % dv == 0 and (2 * kw + vw) % dv == 0 and (2 * kw + 2 * vw) % LANES == 0
    return dict(q0=0, k0=kw // dk, v0=2 * kw // dv, r0=(2 * kw + vw) // dv, g0=(2 * kw + 2 * vw) // LANES)


def _gla_gates(gl, wgu, bias):
    pre = jnp.dot(gl.astype(BF16), wgu, preferred_element_type=F32) + bias
    la = _log_sigmoid(pre) * (1.0 / GLA_GATE_TEMP)
    r_i = lax.broadcasted_iota(jnp.int32, (CHUNK, CHUNK), 0)
    c_i = lax.broadcasted_iota(jnp.int32, (CHUNK, CHUNK), 1)
    cum = _tri_dot((c_i <= r_i).astype(BF16), la)
    total = cum[CHUNK - 1:CHUNK, :]
    return pre, cum, total


def _gla_scan_fwd(proj, wgu_pad, b_gate, o_norm, heads, kw, vw, tb, name):
    seq, pw = proj.shape
    dk, dv = kw // heads, vw // heads
    cb = tb // CHUNK
    nt = seq // tb
    o = _gla_dims(pw, kw, vw, dk, dv)
    scale = dk ** -0.5

    def body(q_ref, k_ref, v_ref, r_ref, gl_ref, wgu_ref, b_ref, on_ref, out_ref, st_ref, s_scr):
        @pl.when(pl.program_id(1) == 0)
        def _():
            s_scr[...] = jnp.zeros_like(s_scr)

        wgu = wgu_ref[...].astype(BF16)
        bias = b_ref[...]
        onorm = on_ref[...]
        st = s_scr[...]
        for ci in range(cb):
            rows = pl.ds(ci * CHUNK, CHUNK)
            _, cum, total = _gla_gates(gl_ref[rows, :], wgu, bias)
            kdec = k_ref[rows, :] * jnp.exp(total - cum)
            st = st * jnp.exp(total) + lax.dot_general(
                v_ref[rows, :].astype(BF16), kdec.astype(BF16), (((0,), (0,)), ((), ())),
                preferred_element_type=F32)
            st_ref[ci] = st
            qs = (q_ref[rows, :] * scale).astype(BF16)
            ov = lax.dot_general(qs, st.astype(BF16), (((1,), (1,)), ((), ())), preferred_element_type=F32)
            rstd = lax.rsqrt(jnp.mean(ov * ov, axis=-1, keepdims=True) + EPS)
            rv = r_ref[rows, :]
            out_ref[rows, :] = (ov * rstd * onorm * (rv * jax.nn.sigmoid(rv))).astype(out_ref.dtype)
        s_scr[...] = st

    in_specs = [
        pl.BlockSpec((tb, dk), lambda h, t: (t, o['q0'] + h)),
        pl.BlockSpec((tb, dk), lambda h, t: (t, o['k0'] + h)),
        pl.BlockSpec((tb, dv), lambda h, t: (t, o['v0'] + h)),
        pl.BlockSpec((tb, dv), lambda h, t: (t, o['r0'] + h)),
        pl.BlockSpec((tb, LANES), lambda h, t: (t, o['g0'])),
        pl.BlockSpec((LANES, dk), lambda h, t: (0, h)),
        pl.BlockSpec((1, dk), lambda h, t: (0, h)),
        pl.BlockSpec((1, dv), lambda h, t: (0, 0)),
    ]
    return _pcall(
        body, name=name, grid=(heads, nt),
        out_shape=[jax.ShapeDtypeStruct((seq, vw), BF16),
                   jax.ShapeDtypeStruct((heads, seq // CHUNK, dv, dk), F32)],
        in_specs=in_specs,
        out_specs=[pl.BlockSpec((tb, dv), lambda h, t: (t, h)),
                   pl.BlockSpec((None, cb, dv, dk), lambda h, t: (h, t, 0, 0))],
        scratch_shapes=[pltpu.VMEM((dv, dk), F32)],
        compiler_params=_params('parallel', 'arbitrary'),
    )(proj, proj, proj, proj, proj, wgu_pad, b_gate, o_norm)


def _gla_scan_bwd(proj, wgu_pad, b_gate, o_norm, states, dgated, heads, kw, vw, tb, name):
    seq, pw = proj.shape
    dk, dv = kw // heads, vw // heads
    cb = tb // CHUNK
    nt = seq // tb
    o = _gla_dims(pw, kw, vw, dk, dv)
    scale = dk ** -0.5

    def body(q_ref, k_ref, v_ref, r_ref, gl_ref, wgu_ref, b_ref, on_ref, st_ref, stp_ref, dg_ref,
             dq_ref, dk_ref, dv_ref, dr_ref, dpre_ref, db_ref, don_ref, ds_scr):
        hh = pl.program_id(0)
        t = pl.program_id(1)

        @pl.when(t == 0)
        def _():
            ds_scr[...] = jnp.zeros_like(ds_scr)
            db_ref[...] = jnp.zeros_like(db_ref)

        @pl.when(jnp.logical_and(hh == 0, t == 0))
        def _():
            don_ref[...] = jnp.zeros_like(don_ref)

        wgu = wgu_ref[...].astype(BF16)
        bias = b_ref[...]
        onorm = on_ref[...]
        has_prev = (t < nt - 1).astype(F32)
        r_i = lax.broadcasted_iota(jnp.int32, (CHUNK, CHUNK), 0)
        c_i = lax.broadcasted_iota(jnp.int32, (CHUNK, CHUNK), 1)
        strict = (c_i < r_i).astype(BF16)
        carry = ds_scr[...]
        db_acc = jnp.zeros((1, dk), F32)
        don_acc = jnp.zeros((1, dv), F32)
        for ci in reversed(range(cb)):
            rows = pl.ds(ci * CHUNK, CHUNK)
            pre, cum, total = _gla_gates(gl_ref[rows, :], wgu, bias)
            edec = jnp.exp(total - cum)
            decay = jnp.exp(total)
            kdec = k_ref[rows, :] * edec
            st = st_ref[ci]
            st_prev = st_ref[ci - 1] if ci > 0 else stp_ref[0] * has_prev
            stb = st.astype(BF16)
            qs = (q_ref[rows, :] * scale).astype(BF16)
            vb = v_ref[rows, :].astype(BF16)
            ov = lax.dot_general(qs, stb, (((1,), (1,)), ((), ())), preferred_element_type=F32)
            rstd = lax.rsqrt(jnp.mean(ov * ov, axis=-1, keepdims=True) + EPS)
            ohat = ov * rstd
            rv = r_ref[rows, :]
            sr = jax.nn.sigmoid(rv)
            dgv = dg_ref[rows, :]
            dy = dgv * (rv * sr)
            dr_ref[rows, :] = (dgv * (ohat * onorm) * (sr * (1.0 + rv * (1.0 - sr)))).astype(dr_ref.dtype)
            don_acc = don_acc + jnp.sum(dy * ohat, axis=0, keepdims=True)
            dohat = dy * onorm
            do = (rstd * (dohat - ohat * jnp.mean(dohat * ohat, axis=-1, keepdims=True))).astype(BF16)
            dq_ref[rows, :] = (jnp.dot(do, stb, preferred_element_type=F32) * scale).astype(dq_ref.dtype)
            dst = carry + lax.dot_general(do, qs, (((0,), (0,)), ((), ())), preferred_element_type=F32)
            dstb = dst.astype(BF16)
            dkdec = jnp.dot(vb, dstb, preferred_element_type=F32)
            dv_ref[rows, :] = lax.dot_general(kdec.astype(BF16), dstb, (((1,), (1,)), ((), ())),
                                              preferred_element_type=F32).astype(dv_ref.dtype)
            ddecay = jnp.sum(dst * st_prev, axis=0, keepdims=True)
            dk_ref[rows, :] = (dkdec * edec).astype(dk_ref.dtype)
            da = ddecay * decay + _tri_dot(strict, dkdec * kdec)
            dpre = da * (1.0 / GLA_GATE_TEMP) * (1.0 - jax.nn.sigmoid(pre))
            dpre_ref[rows, :] = dpre.astype(dpre_ref.dtype)
            db_acc = db_acc + jnp.sum(dpre, axis=0, keepdims=True)
            carry = dst * decay
        ds_scr[...] = carry
        db_ref[...] += db_acc
        don_ref[...] += don_acc

    rt = lambda t: nt - 1 - t
    in_specs = [
        pl.BlockSpec((tb, dk), lambda h, t: (rt(t), o['q0'] + h)),
        pl.BlockSpec((tb, dk), lambda h, t: (rt(t), o['k0'] + h)),
        pl.BlockSpec((tb, dv), lambda h, t: (rt(t), o['v0'] + h)),
        pl.BlockSpec((tb, dv), lambda h, t: (rt(t), o['r0'] + h)),
        pl.BlockSpec((tb, LANES), lambda h, t: (rt(t), o['g0'])),
        pl.BlockSpec((LANES, dk), lambda h, t: (0, h)),
        pl.BlockSpec((1, dk), lambda h, t: (0, h)),
        pl.BlockSpec((1, dv), lambda h, t: (0, 0)),
        pl.BlockSpec((None, cb, dv, dk), lambda h, t: (h, rt(t), 0, 0)),
        pl.BlockSpec((None, 1, dv, dk), lambda h, t: (h, jnp.maximum(rt(t) * cb - 1, 0), 0, 0)),
        pl.BlockSpec((tb, dv), lambda h, t: (rt(t), h)),
    ]
    out_shape = [jax.ShapeDtypeStruct((seq, kw), BF16), jax.ShapeDtypeStruct((seq, kw), BF16),
                 jax.ShapeDtypeStruct((seq, vw), BF16), jax.ShapeDtypeStruct((seq, vw), BF16),
                 jax.ShapeDtypeStruct((seq, kw), BF16),
                 jax.ShapeDtypeStruct((1, kw), F32), jax.ShapeDtypeStruct((1, dv), F32)]
    out_specs = [pl.BlockSpec((tb, dk), lambda h, t: (rt(t), h)),
                 pl.BlockSpec((tb, dk), lambda h, t: (rt(t), h)),
                 pl.BlockSpec((tb, dv), lambda h, t: (rt(t), h)),
                 pl.BlockSpec((tb, dv), lambda h, t: (rt(t), h)),
                 pl.BlockSpec((tb, dk), lambda h, t: (rt(t), h)),
                 pl.BlockSpec((1, dk), lambda h, t: (0, h)),
                 pl.BlockSpec((1, dv), lambda h, t: (0, 0))]
    return _pcall(
        body, name=name, grid=(heads, nt),
        out_shape=out_shape, in_specs=in_specs, out_specs=out_specs,
        scratch_shapes=[pltpu.VMEM((dv, dk), F32)],
        compiler_params=_params('arbitrary', 'arbitrary'),
    )(proj, proj, proj, proj, proj, wgu_pad, b_gate, o_norm, states, states, dgated)


def _cmul(ar, ai, br, bi):
    return ar * br - ai * bi, ar * bi + ai * br


def _gelu(y):
    c = math.sqrt(2.0 / math.pi)
    return 0.5 * y * (1.0 + jnp.tanh(c * (y + 0.044715 * y * y * y)))


def _gelu_grad(y):
    c = math.sqrt(2.0 / math.pi)
    th = jnp.tanh(c * (y + 0.044715 * y * y * y))
    return 0.5 * (1.0 + th) + 0.5 * y * (1.0 - th * th) * (c * (1.0 + 3.0 * 0.044715 * y * y))


def _power_pow2(ar, ai, n):
    assert n & (n - 1) == 0
    for _ in range(n.bit_length() - 1):
        ar, ai = _cmul(ar, ai, ar, ai)
    return ar, ai


def _s5_fwd(u, bre, bim, cre, cim, are, aim, dskip, name):
    seq, width = u.shape
    nb, ub, sb = bre.shape
    ls = seq // S5_SEGMENTS
    seg = S5_SEGMENTS

    def body(u_ref, bre_ref, bim_ref, cre_ref, cim_ref, are_ref, aim_ref, d_ref, y_ref, z_ref, xr_ref, xi_ref):
        uv = u_ref[...]
        ub16 = uv.astype(BF16)
        xr_ref[...] = jnp.dot(ub16, bre_ref[...].astype(BF16), preferred_element_type=F32)
        xi_ref[...] = jnp.dot(ub16, bim_ref[...].astype(BF16), preferred_element_type=F32)
        ar = jnp.broadcast_to(are_ref[...], (seg, sb))
        ai = jnp.broadcast_to(aim_ref[...], (seg, sb))

        def step(i, c):
            rows = pl.ds(pl.multiple_of(i * seg, seg), seg)
            pr, pi = _cmul(ar, ai, c[0], c[1])
            nr = pr + xr_ref[rows, :]
            ni = pi + xi_ref[rows, :]
            xr_ref[rows, :] = nr
            xi_ref[rows, :] = ni
            return nr, ni

        zero = jnp.zeros((seg, sb), F32)
        er, ei = lax.fori_loop(0, ls, step, (zero, zero), unroll=8)
        pr, pi = _power_pow2(ar, ai, ls)
        row = lax.broadcasted_iota(jnp.int32, (seg, sb), 0)
        sr, si = zero, zero
        for _ in range(seg - 1):
            tr, ti = _cmul(pr, pi, sr, si)
            sr = jnp.where(row == 0, 0.0, pltpu.roll(tr + er, 1, 0))
            si = jnp.where(row == 0, 0.0, pltpu.roll(ti + ei, 1, 0))

        def fix(i, c):
            rows = pl.ds(pl.multiple_of(i * seg, seg), seg)
            fr, fi = _cmul(c[0], c[1], sr, si)
            xr_ref[rows, :] += fr
            xi_ref[rows, :] += fi
            return _cmul(c[0], c[1], ar, ai)

        lax.fori_loop(0, ls, fix, (ar, ai), unroll=8)
        y = (jnp.dot(xr_ref[...].astype(BF16), cre_ref[...].astype(BF16), preferred_element_type=F32)
             - jnp.dot(xi_ref[...].astype(BF16), cim_ref[...].astype(BF16), preferred_element_type=F32)
             + d_ref[...] * uv)
        y_ref[...] = y
        z_ref[...] = _gelu(y).astype(z_ref.dtype)

    mat = lambda r, c: pl.BlockSpec((None, r, c), lambda b: (b, 0, 0))
    return _pcall(
        body, name=name, grid=(nb,),
        out_shape=[jax.ShapeDtypeStruct((seq, width), F32), jax.ShapeDtypeStruct((seq, width), BF16),
                   jax.ShapeDtypeStruct((seq, nb * sb), F32), jax.ShapeDtypeStruct((seq, nb * sb), F32)],
        in_specs=[pl.BlockSpec((seq, ub), lambda b: (0, b)), mat(ub, sb), mat(ub, sb), mat(sb, ub), mat(sb, ub),
                  mat(1, sb), mat(1, sb), pl.BlockSpec((1, ub), lambda b: (0, b))],
        out_specs=[pl.BlockSpec((seq, ub), lambda b: (0, b)), pl.BlockSpec((seq, ub), lambda b: (0, b)),
                   pl.BlockSpec((seq, sb), lambda b: (0, b)), pl.BlockSpec((seq, sb), lambda b: (0, b))],
        compiler_params=_params('parallel'),
    )(u, bre, bim, cre, cim, are, aim, dskip)


def _s5_bwd(dz, y, u, xr, xi, bre, bim, cre, cim, are, aim, dskip, name):
    seq, width = u.shape
    nb, ub, sb = bre.shape
    ls = seq // S5_SEGMENTS
    seg = S5_SEGMENTS

    def body(dz_ref, y_ref, u_ref, xr_ref, xi_ref, bre_ref, bim_ref, cre_ref, cim_ref, are_ref, aim_ref, d_ref,
             du_ref, dcr_ref, dci_ref, dbr_ref, dbi_ref, dar_ref, dai_ref, dd_ref, lr_ref, li_ref):
        uv = u_ref[...]
        dy = dz_ref[...] * _gelu_grad(y_ref[...])
        dd_ref[...] = jnp.sum(dy * uv, axis=0, keepdims=True)
        dyb = dy.astype(BF16)
        nt = (((1,), (1,)), ((), ()))
        tn = (((0,), (0,)), ((), ()))
        lr_ref[...] = lax.dot_general(dyb, cre_ref[...].astype(BF16), nt, preferred_element_type=F32)
        li_ref[...] = -lax.dot_general(dyb, cim_ref[...].astype(BF16), nt, preferred_element_type=F32)
        dcr_ref[...] = lax.dot_general(dyb, xr_ref[...].astype(BF16), tn, preferred_element_type=F32)
        dci_ref[...] = -lax.dot_general(dyb, xi_ref[...].astype(BF16), tn, preferred_element_type=F32)
        ar = jnp.broadcast_to(are_ref[...], (seg, sb))
        ai = jnp.broadcast_to(aim_ref[...], (seg, sb))
        nai = -ai

        def step(ii, c):
            rows = pl.ds(pl.multiple_of((ls - 1 - ii) * seg, seg), seg)
            pr, pi = _cmul(ar, nai, c[0], c[1])
            nr = pr + lr_ref[rows, :]
            ni = pi + li_ref[rows, :]
            lr_ref[rows, :] = nr
            li_ref[rows, :] = ni
            return nr, ni

        zero = jnp.zeros((seg, sb), F32)
        er, ei = lax.fori_loop(0, ls, step, (zero, zero), unroll=8)
        pr, pi = _power_pow2(ar, nai, ls)
        row = lax.broadcasted_iota(jnp.int32, (seg, sb), 0)
        rr, ri = zero, zero
        for _ in range(seg - 1):
            tr, ti = _cmul(pr, pi, rr, ri)
            rr = jnp.where(row == seg - 1, 0.0, pltpu.roll(tr + er, seg - 1, 0))
            ri = jnp.where(row == seg - 1, 0.0, pltpu.roll(ti + ei, seg - 1, 0))

        def corrected(rows, qr, qi):
            fr, fi = _cmul(qr, qi, rr, ri)
            nr = lr_ref[rows, :] + fr
            ni = li_ref[rows, :] + fi
            lr_ref[rows, :] = nr
            li_ref[rows, :] = ni
            return nr, ni

        def grad_a(nr, ni, xpr, xpi, accr, acci):
            return accr + nr * xpr + ni * xpi, acci + ni * xpr - nr * xpi

        def fix(ii, c):
            qr, qi, accr, acci = c
            i = ls - 1 - ii
            rows = pl.ds(pl.multiple_of(i * seg, seg), seg)
            prev = pl.ds(pl.multiple_of((i - 1) * seg, seg), seg)
            nr, ni = corrected(rows, qr, qi)
            accr, acci = grad_a(nr, ni, xr_ref[prev, :], xi_ref[prev, :], accr, acci)
            qr, qi = _cmul(qr, qi, ar, nai)
            return qr, qi, accr, acci

        qr, qi, accr, acci = lax.fori_loop(0, ls - 1, fix, (ar, nai, zero, zero), unroll=8)
        nr, ni = corrected(pl.ds(0, seg), qr, qi)
        last = pl.ds((ls - 1) * seg, seg)
        xpr = jnp.where(row == 0, 0.0, pltpu.roll(xr_ref[last, :], 1, 0))
        xpi = jnp.where(row == 0, 0.0, pltpu.roll(xi_ref[last, :], 1, 0))
        accr, acci = grad_a(nr, ni, xpr, xpi, accr, acci)
        dar_ref[...] = jnp.sum(accr, axis=0, keepdims=True)
        dai_ref[...] = jnp.sum(acci, axis=0, keepdims=True)
        lrb = lr_ref[...].astype(BF16)
        lib = li_ref[...].astype(BF16)
        ub16 = uv.astype(BF16)
        dbr_ref[...] = lax.dot_general(ub16, lrb, tn, preferred_element_type=F32)
        dbi_ref[...] = lax.dot_general(ub16, lib, tn, preferred_element_type=F32)
        du_ref[...] = (d_ref[...] * dy
                       + lax.dot_general(lrb, bre_ref[...].astype(BF16), nt, preferred_element_type=F32)
                       + lax.dot_general(lib, bim_ref[...].astype(BF16), nt, preferred_element_type=F32))

    mat = lambda r, c: pl.BlockSpec((None, r, c), lambda b: (b, 0, 0))
    col = lambda w: pl.BlockSpec((seq, w), lambda b: (0, b))
    return _pcall(
        body, name=name, grid=(nb,),
        out_shape=[jax.ShapeDtypeStruct((seq, width), F32)]
        + [jax.ShapeDtypeStruct((nb, ub, sb), F32)] * 4
        + [jax.ShapeDtypeStruct((nb, 1, sb), F32)] * 2
        + [jax.ShapeDtypeStruct((1, width), F32)],
        in_specs=[col(ub), col(ub), col(ub), col(sb), col(sb), mat(ub, sb), mat(ub, sb), mat(sb, ub), mat(sb, ub),
                  mat(1, sb), mat(1, sb), pl.BlockSpec((1, ub), lambda b: (0, b))],
        out_specs=[col(ub), mat(ub, sb), mat(ub, sb), mat(ub, sb), mat(ub, sb), mat(1, sb), mat(1, sb),
                   pl.BlockSpec((1, ub), lambda b: (0, b))],
        scratch_shapes=[pltpu.VMEM((seq, sb), F32), pltpu.VMEM((seq, sb), F32)],
        compiler_params=_params('parallel'),
    )(dz, y, u, xr, xi, bre, bim, cre, cim, are, aim, dskip)


def _s5_discretise(lam_re, lam_im, log_dt, b_re, b_im):
    lr = jnp.minimum(lam_re, S5_EIG_CLIP)
    li = lam_im
    dt = jnp.exp(log_dt)[:, None]
    mag = jnp.exp(lr * dt)
    ang = li * dt
    ab_re = mag * jnp.cos(ang)
    ab_im = mag * jnp.sin(ang)
    den = lr * lr + li * li
    nr = ab_re - 1.0
    f_re = (nr * lr + ab_im * li) / den
    f_im = (ab_im * lr - nr * li) / den
    bb_re = f_re[..., None] * b_re - f_im[..., None] * b_im
    bb_im = f_re[..., None] * b_im + f_im[..., None] * b_re
    return ab_re, ab_im, bb_re, bb_im


def _to_blocks(m):
    g, a, b = m.shape
    gb = S5_GROUPS_PER_BLOCK
    eye = jnp.eye(gb, dtype=m.dtype)
    return jnp.einsum('bgac,gh->bgahc', m.reshape(g // gb, gb, a, b), eye).reshape(g // gb, gb * a, gb * b)


def _from_blocks(m, a, b):
    nb = m.shape[0]
    gb = S5_GROUPS_PER_BLOCK
    eye = jnp.eye(gb, dtype=m.dtype)
    return jnp.einsum('bgahc,gh->bgac', m.reshape(nb, gb, a, gb, b), eye).reshape(nb * gb, a, b)


def _glu_fwd(o, h, name):
    half = o.shape[1] // 2

    def fn(ov, hv):
        return (hv + ov[:, :half] * jax.nn.sigmoid(ov[:, half:]),)
    return _rowwise(name, fn, [o, h], [], [(half, F32)], [], 256)[0]


def _glu_bwd(o, dout, name):
    half = o.shape[1] // 2

    def fn(ov, dv):
        val, gate = ov[:, :half], ov[:, half:]
        sg = jax.nn.sigmoid(gate)
        return (jnp.concatenate([dv * sg, dv * val * sg * (1.0 - sg)], axis=1),)
    return _rowwise(name, fn, [o, dout], [], [(2 * half, BF16)], [], 256)[0]


def _adam_math(w, g, m, v):
    m = ADAM_B1 * m + (1.0 - ADAM_B1) * g
    v = ADAM_B2 * v + (1.0 - ADAM_B2) * (g * g)
    m_hat = m / (1.0 - ADAM_B1 ** ADAM_STEP)
    v_hat = v / (1.0 - ADAM_B2 ** ADAM_STEP)
    delta = -ADAM_LR * (m_hat / (jnp.sqrt(v_hat) + ADAM_EPS) + ADAM_WD * w)
    return delta, m, v


def _adamw(w, m, v, grads, name, after=()):
    nl, rows, cols = w.shape
    tm = _tile(rows, max(8, (1 << 18) // cols // 8 * 8), unit=8)
    nbk = rows // tm

    def body(*refs):
        w_ref, m_ref, v_ref = refs[:3]
        g_refs = refs[3:3 + nl]
        go_ref, d_ref, mo_ref, vo_ref = refs[3 + nl + len(after):]
        layer = pl.program_id(0)
        g = g_refs[0][...]
        for l in range(1, nl):
            g = jnp.where(layer == l, g_refs[l][...], g)
        delta, mn, vn = _adam_math(w_ref[...], g, m_ref[...], v_ref[...])
        go_ref[...] = g
        d_ref[...] = delta
        mo_ref[...] = mn
        vo_ref[...] = vn

    stacked = pl.BlockSpec((None, tm, cols), lambda l, i: (l, i, 0))

    def g_spec(layer):
        return pl.BlockSpec((tm, cols), lambda l, i: (jnp.where(l == layer, i, jnp.where(l < layer, 0, nbk - 1)), 0))

    return _pcall(
        body, name=name, grid=(nl, nbk),
        out_shape=[jax.ShapeDtypeStruct(w.shape, F32)] * 4,
        in_specs=[stacked] * 3 + [g_spec(l) for l in range(nl)] + [ANY] * len(after),
        out_specs=[stacked] * 4,
        compiler_params=_params('arbitrary', 'arbitrary'),
    )(w, m, v, *grads, *after)


def _pack(arrs, rows_mult=512):
    flat = jnp.concatenate([a.reshape(-1) for a in arrs])
    total = flat.shape[0]
    rows = -(-total // LANES)
    rows = -(-rows // rows_mult) * rows_mult
    flat = jnp.pad(flat, (0, rows * LANES - total))
    return flat.reshape(rows, LANES)


def _unpack(packed, shapes):
    flat = packed.reshape(-1)
    out, off = [], 0
    for s in shapes:
        size = math.prod(s)
        out.append(flat[off:off + size].reshape(s))
        off += size
    return out


def _permute(a):
    seq, w = a.shape
    return a.reshape(S5_SEGMENTS, seq // S5_SEGMENTS, w).transpose(1, 0, 2).reshape(seq, w)


def _unpermute(a):
    seq, w = a.shape
    return a.reshape(seq // S5_SEGMENTS, S5_SEGMENTS, w).transpose(1, 0, 2).reshape(seq, w)


def kernel(x, gla_norm, gla_w_in, gla_w_gate_up, gla_b_gate, gla_o_norm, gla_w_out, s5_norm, s5_w_in, s5_lam_re, s5_lam_im, s5_log_dt, s5_b_re, s5_b_im, s5_c_re, s5_c_im, s5_d, s5_w_out, mlp_norm, mlp_w_up, mlp_w_down, final_norm, loss_target, m_gla_norm, m_gla_w_in, m_gla_w_gate_up, m_gla_b_gate, m_gla_o_norm, m_gla_w_out, m_s5_norm, m_s5_w_in, m_s5_lam_re, m_s5_lam_im, m_s5_log_dt, m_s5_b_re, m_s5_b_im, m_s5_c_re, m_s5_c_im, m_s5_d, m_s5_w_out, m_mlp_norm, m_mlp_w_up, m_mlp_w_down, m_final_norm, v_gla_norm, v_gla_w_in, v_gla_w_gate_up, v_gla_b_gate, v_gla_o_norm, v_gla_w_out, v_s5_norm, v_s5_w_in, v_s5_lam_re, v_s5_lam_im, v_s5_log_dt, v_s5_b_re, v_s5_b_im, v_s5_c_re, v_s5_c_im, v_s5_d, v_s5_w_out, v_mlp_norm, v_mlp_w_up, v_mlp_w_down, v_final_norm):
    weights = dict(gla_norm=gla_norm, gla_w_in=gla_w_in, gla_w_gate_up=gla_w_gate_up, gla_b_gate=gla_b_gate, gla_o_norm=gla_o_norm, gla_w_out=gla_w_out, s5_norm=s5_norm, s5_w_in=s5_w_in, s5_lam_re=s5_lam_re, s5_lam_im=s5_lam_im, s5_log_dt=s5_log_dt, s5_b_re=s5_b_re, s5_b_im=s5_b_im, s5_c_re=s5_c_re, s5_c_im=s5_c_im, s5_d=s5_d, s5_w_out=s5_w_out, mlp_norm=mlp_norm, mlp_w_up=mlp_w_up, mlp_w_down=mlp_w_down, final_norm=final_norm)
    mom1 = dict(gla_norm=m_gla_norm, gla_w_in=m_gla_w_in, gla_w_gate_up=m_gla_w_gate_up, gla_b_gate=m_gla_b_gate, gla_o_norm=m_gla_o_norm, gla_w_out=m_gla_w_out, s5_norm=m_s5_norm, s5_w_in=m_s5_w_in, s5_lam_re=m_s5_lam_re, s5_lam_im=m_s5_lam_im, s5_log_dt=m_s5_log_dt, s5_b_re=m_s5_b_re, s5_b_im=m_s5_b_im, s5_c_re=m_s5_c_re, s5_c_im=m_s5_c_im, s5_d=m_s5_d, s5_w_out=m_s5_w_out, mlp_norm=m_mlp_norm, mlp_w_up=m_mlp_w_up, mlp_w_down=m_mlp_w_down, final_norm=m_final_norm)
    mom2 = dict(gla_norm=v_gla_norm, gla_w_in=v_gla_w_in, gla_w_gate_up=v_gla_w_gate_up, gla_b_gate=v_gla_b_gate, gla_o_norm=v_gla_o_norm, gla_w_out=v_gla_w_out, s5_norm=v_s5_norm, s5_w_in=v_s5_w_in, s5_lam_re=v_s5_lam_re, s5_lam_im=v_s5_lam_im, s5_log_dt=v_s5_log_dt, s5_b_re=v_s5_b_re, s5_b_im=v_s5_b_im, s5_c_re=v_s5_c_re, s5_c_im=v_s5_c_im, s5_d=v_s5_d, s5_w_out=v_s5_w_out, mlp_norm=v_mlp_norm, mlp_w_up=v_mlp_w_up, mlp_w_down=v_mlp_w_down, final_norm=v_final_norm)
    names = list(weights)
    big = ['gla_w_in', 'gla_w_out', 's5_w_in', 's5_w_out', 'mlp_w_up', 'mlp_w_down']
    small = [n for n in names if n not in big]

    chip = 2 * lax.axis_index('x') + lax.axis_index('y')
    h0 = x[0]
    target = loss_target[0]
    seq, dm = h0.shape
    depth = mlp_norm.shape[0]
    n_gla = gla_norm.shape[0]
    n_s5 = s5_lam_re.shape[0]
    rank = gla_w_gate_up.shape[1]
    kw = gla_b_gate.shape[1]
    dv = gla_o_norm.shape[1]
    in_w = 4 * gla_w_in.shape[2]
    vw = (in_w - rank - 2 * kw) // 2
    heads = vw // dv
    dk = kw // heads
    pw = -(-in_w // LANES) * LANES
    s5w = s5_w_in.shape[2]
    n_grp, n_state, grp = s5_b_re.shape[1:]
    hid = 4 * mlp_w_up.shape[2]
    tb = min(seq, 8 * CHUNK)
    tm = _tile(seq, 1024)

    ic = lax.axis_index('c')
    rh = lambda w: w.shape[1] // 2
    wb16 = {n: weights[n].astype(BF16) for n in big}
    gathered = {n: [None] * weights[n].shape[0] for n in big}
    g_w_in, g_gla_out, g_s5_in, g_s5_out, g_up, g_down = (gathered[n] for n in big)
    in_flight = {}

    to_start = []
    for i in range(depth):
        mix = ['gla_w_in', 'gla_w_out'] if i % 2 == 0 else ['s5_w_in', 's5_w_out']
        to_start += [(m, i // 2) for m in mix] + [('mlp_w_up', i), ('mlp_w_down', i)]

    def start_gathers(after):
        while to_start and len(in_flight) < GATHERS_IN_FLIGHT:
            n, l = to_start.pop(0)
            rows, cols = weights[n].shape[1:]
            hd = _xy_start(wb16[n][l].reshape(2, rows // 2, cols), 'bcast_c', f'ag_{n}_{l}_start', after)
            in_flight[n, l] = hd
            after = [hd['token']]
        return after

    def finish_gather(n, l, after):
        sent, y1 = _xy_wait(in_flight.pop((n, l)), f'ag_{n}_{l}_wait', after)
        behind = start_gathers([y1])
        y1 = lax.dynamic_update_index_in_dim(y1, lax.dynamic_index_in_dim(sent, ic, 0, keepdims=False), chip, 0)
        gathered[n][l] = _exchange(y1, 'c', 'bcast', f'ag_{n}_c')
        return behind

    sharded_small = [gla_w_gate_up, s5_norm, s5_d]
    gathered_small = _exchange(_pack(sharded_small), 'xy', 'bcast', 'ag_small')
    start_gathers([gathered_small])
    parts = [_unpack(gathered_small[k], [a.shape for a in sharded_small]) for k in range(4)]
    wgu_full = jnp.concatenate([p[0] for p in parts], axis=2)
    s5_norm_full = jnp.concatenate([p[1] for p in parts], axis=1)
    s5_d_full = jnp.concatenate([p[2] for p in parts], axis=1)

    def gla_w_in_padded(j):
        wj = g_w_in[j].transpose(0, 2, 1, 3).reshape(dm, in_w)
        return jnp.pad(wj, ((0, 0), (0, pw - in_w)))

    grads = {n: [None] * weights[n].shape[0] for n in names if n != 'final_norm'}

    saved = []
    h = h0
    for i in range(depth):
        j = i // 2
        rec = {}
        if i % 2 == 0:
            rec['h_in'] = h
            behind = finish_gather('gla_w_in', j, [h])
            hn = _norm_fwd(h, gla_norm[j:j + 1], 'gla_norm_fwd', behind)
            w_in_pad = gla_w_in_padded(j)
            proj = _mm('gla_proj', 'nn', _plain(hn), _plain(w_in_pad), (seq, pw, dm),
                       [((seq, pw), F32, _plain_shape(None))], (tm, _tile(pw, 1024), dm))[0]
            wgu_pad = jnp.pad(wgu_full[j], ((0, LANES - rank), (0, 0)))
            gated, states = _gla_scan_fwd(proj, wgu_pad, gla_b_gate[j:j + 1], gla_o_norm[j:j + 1],
                                          heads, kw, vw, tb, 'gla_scan_fwd')
            behind = finish_gather('gla_w_out', j, [gated])
            h = _mm('gla_out', 'nn', _plain(gated), _w_rows(g_gla_out, j), (seq, dm, vw),
                    [((seq, dm), F32, _plain_shape(None))],
                    (tm, _tile(dm, 1024), gla_w_out.shape[1]),
                    epilogue=lambda acc, hv: (acc + hv,), extras=[_plain(h)], after=behind)[0]
            rec.update(hn=hn, w_in_pad=w_in_pad, proj=proj, wgu_pad=wgu_pad, gated=gated, states=states)
        else:
            hp = _permute(h)
            rec['h_in'] = hp
            hn = _norm_fwd(hp, s5_norm_full[j:j + 1], 's5_norm_fwd')
            behind = finish_gather('s5_w_in', j, [hn])
            u = _mm('s5_in', 'nn', _plain(hn), _w_rows(g_s5_in, j), (seq, s5w, dm),
                    [((seq, s5w), F32, _plain_shape(None))],
                    (tm, _tile(s5w, 1024), s5_w_in.shape[1]), after=behind)[0]
            disc, disc_vjp = jax.vjp(_s5_discretise, s5_lam_re[j], s5_lam_im[j], s5_log_dt[j], s5_b_re[j], s5_b_im[j])
            ab_re, ab_im, bb_re, bb_im = disc
            bre = _to_blocks(bb_re.transpose(0, 2, 1))
            bim = _to_blocks(bb_im.transpose(0, 2, 1))
            cre = _to_blocks(s5_c_re[j].transpose(0, 2, 1))
            cim = _to_blocks(s5_c_im[j].transpose(0, 2, 1))
            nb = n_grp // S5_GROUPS_PER_BLOCK
            are = ab_re.reshape(nb, 1, S5_GROUPS_PER_BLOCK * n_state)
            aim = ab_im.reshape(nb, 1, S5_GROUPS_PER_BLOCK * n_state)
            dskip = s5_d_full[j:j + 1]
            y, z, xr, xi = _s5_fwd(u, bre, bim, cre, cim, are, aim, dskip, 's5_scan_fwd')
            behind = finish_gather('s5_w_out', j, [z])
            o = _mm('s5_out', 'nn', _plain(z), _w_cols(g_s5_out, j), (seq, 2 * dm, s5w),
                    [((seq, 2 * dm), F32, _plain_shape(None))],
                    (tm, _tile(s5_w_out.shape[2], 1024), s5_w_out.shape[1]), after=behind)[0]
            h = _unpermute(_glu_fwd(o, hp, 's5_glu_fwd'))
            rec.update(hn=hn, u=u, y=y, z=z, xr=xr, xi=xi, o=o, mats=(bre, bim, cre, cim, are, aim, dskip),
                       disc_vjp=disc_vjp)
        rec['h_mid'] = h
        hn2 = _norm_fwd(h, mlp_norm[i:i + 1], 'mlp_norm_fwd')
        behind = finish_gather('mlp_w_up', i, [hn2])
        act, act2 = _mm('mlp_up', 'nn', _plain(hn2), _w_cols(g_up, i), (seq, hid, dm),
                        [((seq, hid), BF16, _plain_shape(None))] * 2,
                        (tm, _tile(mlp_w_up.shape[2], 1024), mlp_w_up.shape[1]),
                        epilogue=lambda acc: (jnp.maximum(acc, 0.0), jnp.square(jnp.maximum(acc, 0.0))),
                        after=behind)
        behind = finish_gather('mlp_w_down', i, [act2])
        h = _mm('mlp_down', 'nn', _plain(act2), _w_rows(g_down, i), (seq, dm, hid),
                [((seq, dm), F32, _plain_shape(None))],
                (tm, _tile(dm, 1024), mlp_w_down.shape[1]),
                epilogue=lambda acc, hv: (acc + hv,), extras=[_plain(h)], after=behind)[0]
        rec.update(hn2=hn2, act=act, act2=act2)
        saved.append(rec)

    dh, loss_cols, d_final = _loss_head(h, target, final_norm.reshape(1, dm), 'loss_head')
    loss = lax.psum(jnp.sum(loss_cols), ('x', 'y', 'c'))
    grads['final_norm'] = [d_final.reshape(dm)]

    big_grads = {n: [None] * weights[n].shape[0] for n in big}
    reducing = []

    def reduce_begin(dw, n, l):
        pre = _pair_exchange_chunked(dw, 'a2a', f'rs_{n}_pair', reduce=True)
        hd = _xy_start(pre, 'a2a', f'rs_{n}_{l}_start', [])
        reducing.append((n, l, hd))
        return [hd['token']]

    def reduce_end(after):
        n, l, hd = reducing.pop(0)
        sent, yb = _xy_wait(hd, f'rs_{n}_{l}_wait', after)
        yb = lax.dynamic_update_index_in_dim(yb, lax.dynamic_index_in_dim(sent, chip, 0, keepdims=False), chip, 0)
        fin = _sum_slots(yb, F32, f'rs_{n}_chipsum')
        yc = _exchange(fin, 'c', 'bcast', f'rs_{n}_back')
        big_grads[n][l] = yc.reshape(2 * fin.shape[0], fin.shape[1])

    def reduce_scatter(dw, n, l):
        started_ = reduce_begin(dw, n, l)
        while len(reducing) > 1:
            reduce_end(started_)
        return started_
    for i in reversed(range(depth)):
        j = i // 2
        rec = saved[i]
        r_dn, c_dn = mlp_w_down.shape[1:]
        shape, spec = _dw_rows(r_dn, c_dn)
        dw = _mm('mlp_down_dw', 'tn', _plain(rec['act2']), _plain(dh), (hid, dm, seq),
                 [(shape, BF16, spec)], (_tile(r_dn // 2, 1024), _tile(c_dn, 1024), seq))[0]
        behind = reduce_scatter(dw, 'mlp_w_down', i)
        dpre = _mm('mlp_down_dx', 'nt', _plain(dh), _w_rows(g_down, i), (seq, hid, dm),
                   [((seq, hid), BF16, _plain_shape(None))],
                   (tm, _tile(rh(mlp_w_down), 1024), dm),
                   epilogue=lambda acc, av: (acc * (2.0 * av.astype(F32)),), extras=[_plain(rec['act'])],
                   after=behind)[0]
        r_up, c_up = mlp_w_up.shape[1:]
        shape, spec = _dw_cols(r_up, c_up)
        dw = _mm('mlp_up_dw', 'tn', _plain(rec['hn2']), _plain(dpre), (dm, hid, seq),
                 [(shape, BF16, spec)], (_tile(r_up // 2, 1024), _tile(c_up, 1024), seq))[0]
        behind = reduce_scatter(dw, 'mlp_w_up', i)
        dhn = _mm('mlp_up_dx', 'nt', _plain(dpre), _w_cols(g_up, i), (seq, dm, hid),
                  [((seq, dm), F32, _plain_shape(None))],
                  (tm, _tile(rh(mlp_w_up), 1024), _tile(mlp_w_up.shape[2], 2048)), after=behind)[0]
        dh, dg = _norm_bwd(rec['h_mid'], dhn, dh, mlp_norm[i:i + 1], 'mlp_norm_bwd')
        grads['mlp_norm'][i] = dg[0]

        if i % 2 == 0:
            r_o, c_o = gla_w_out.shape[1:]
            shape, spec = _dw_rows(r_o, c_o)
            dw = _mm('gla_out_dw', 'tn', _plain(rec['gated']), _plain(dh), (vw, dm, seq),
                     [(shape, BF16, spec)], (_tile(r_o // 2, 1024), _tile(c_o, 1024), seq))[0]
            behind = reduce_scatter(dw, 'gla_w_out', j)
            dgated = _mm('gla_out_dx', 'nt', _plain(dh), _w_rows(g_gla_out, j), (seq, vw, dm),
                         [((seq, vw), F32, _plain_shape(None))],
                         (tm, _tile(rh(gla_w_out), 1024), dm), after=behind)[0]
            dq, dkk, dvv, dr, dpre_g, db, don = _gla_scan_bwd(
                rec['proj'], rec['wgu_pad'], gla_b_gate[j:j + 1], gla_o_norm[j:j + 1], rec['states'], dgated,
                heads, kw, vw, tb, 'gla_scan_bwd')
            grads['gla_b_gate'][j] = db[0]
            grads['gla_o_norm'][j] = don[0]
            dgl = _mm('gla_gate_dx', 'nt', _plain(dpre_g), _plain(rec['wgu_pad']), (seq, LANES, kw),
                      [((seq, LANES), BF16, _plain_shape(None))], (tm, LANES, kw))[0]
            g_low = rec['proj'][:, pw - LANES:]
            dwgu = _mm('gla_gate_dw', 'tn', _plain(g_low), _plain(dpre_g), (LANES, kw, seq),
                       [((LANES, kw), F32, _plain_shape(None))], (LANES, kw, seq))[0]
            grads['gla_w_gate_up'][j] = dwgu[:rank]
            dproj = jnp.concatenate([dq, dkk, dvv, dr, dgl], axis=1)
            dw_pad = _mm('gla_proj_dw', 'tn', _plain(rec['hn']), _plain(dproj), (dm, pw, seq),
                         [((dm, pw), BF16, _plain_shape(None))], (_tile(dm, 1024), _tile(pw, 1024), seq))[0]
            shard_w = in_w // 4
            dw = dw_pad[:, :in_w].reshape(2, dm // 2, 4, shard_w).transpose(0, 2, 1, 3)
            behind = reduce_scatter(dw, 'gla_w_in', j)
            dhn = _mm('gla_proj_dx', 'nt', _plain(dproj), _plain(rec['w_in_pad']), (seq, dm, pw),
                      [((seq, dm), F32, _plain_shape(None))], (tm, _tile(dm, 1024), _tile(pw, 1024)),
                      after=behind)[0]
            dh, dg = _norm_bwd(rec['h_in'], dhn, dh, gla_norm[j:j + 1], 'gla_norm_bwd')
            grads['gla_norm'][j] = dg[0]
        else:
            dhp = _permute(dh)
            do = _glu_bwd(rec['o'], dhp, 's5_glu_bwd')
            r_o, c_o = s5_w_out.shape[1:]
            shape, spec = _dw_cols(r_o, c_o)
            dw = _mm('s5_out_dw', 'tn', _plain(rec['z']), _plain(do), (s5w, 2 * dm, seq),
                     [(shape, BF16, spec)], (_tile(r_o // 2, 1024), _tile(c_o, 1024), seq))[0]
            behind = reduce_scatter(dw, 's5_w_out', j)
            dz = _mm('s5_out_dx', 'nt', _plain(do), _w_cols(g_s5_out, j), (seq, s5w, 2 * dm),
                     [((seq, s5w), F32, _plain_shape(None))],
                     (tm, _tile(rh(s5_w_out), 1024), _tile(s5_w_out.shape[2], 1024)), after=behind)[0]
            bre, bim, cre, cim, are, aim, dskip = rec['mats']
            du, dcr, dci, dbr, dbi, dar, dai, dd = _s5_bwd(dz, rec['y'], rec['u'], rec['xr'], rec['xi'],
                                                           bre, bim, cre, cim, are, aim, dskip, 's5_scan_bwd')
            grads['s5_c_re'][j] = _from_blocks(dcr, grp, n_state)
            grads['s5_c_im'][j] = _from_blocks(dci, grp, n_state)
            dbb_re = _from_blocks(dbr, grp, n_state).transpose(0, 2, 1)
            dbb_im = _from_blocks(dbi, grp, n_state).transpose(0, 2, 1)
            d_lr, d_li, d_dt, d_bre, d_bim = rec['disc_vjp'](
                (dar.reshape(n_grp, n_state), dai.reshape(n_grp, n_state), dbb_re, dbb_im))
            grads['s5_lam_re'][j] = d_lr
            grads['s5_lam_im'][j] = d_li
            grads['s5_log_dt'][j] = d_dt
            grads['s5_b_re'][j] = d_bre
            grads['s5_b_im'][j] = d_bim
            grads['s5_d'][j] = dd[0]
            r_i, c_i = s5_w_in.shape[1:]
            shape, spec = _dw_rows(r_i, c_i)
            dw = _mm('s5_in_dw', 'tn', _plain(rec['hn']), _plain(du), (dm, s5w, seq),
                     [(shape, BF16, spec)], (_tile(r_i // 2, 1024), _tile(c_i, 1024), seq))[0]
            behind = reduce_scatter(dw, 's5_w_in', j)
            dhn = _mm('s5_in_dx', 'nt', _plain(du), _w_rows(g_s5_in, j), (seq, dm, s5w),
                      [((seq, dm), F32, _plain_shape(None))],
                      (tm, _tile(rh(s5_w_in), 1024), _tile(s5w, 1024)), after=behind)[0]
            dhp, dg = _norm_bwd(rec['h_in'], dhn, dhp, s5_norm_full[j:j + 1], 's5_norm_bwd')
            dh = _unpermute(dhp)
            grads['s5_norm'][j] = dg[0]
    grad_x = dh[None]
    while reducing:
        reduce_end([dh])

    local_small = [jnp.stack(grads[n]) if n != 'final_norm' else grads[n][0] for n in small]
    full_shapes = [a.shape for a in local_small]
    packed_small = _pack(local_small)
    ar_small = _xy_start(packed_small, 'bcast', 'ar_small_start', [])
    out_g, out_d, out_m, out_v = {}, {}, {}, {}
    behind = [ar_small['token']]
    for n in big:
        out_g[n], out_d[n], out_m[n], out_v[n] = _adamw(weights[n], mom1[n], mom2[n], big_grads[n], 'adamw_' + n,
                                                        behind)
        behind = [out_d[n]]
    sent, by_chip = _xy_wait(ar_small, 'ar_small_wait', behind)
    by_chip = lax.dynamic_update_index_in_dim(by_chip, sent, chip, 0)
    gathered = _exchange(by_chip, 'c', 'bcast', 'ar_small_c')
    rows = gathered.shape[2]
    summed = _sum_slots(gathered.reshape(8, rows, LANES), F32, 'ar_small_sum')
    small_full = dict(zip(small, _unpack(summed, full_shapes)))
    small_grad = {}
    for n in small:
        g = small_full[n]
        if g.shape != weights[n].shape:
            ax = [a for a in range(g.ndim) if g.shape[a] != weights[n].shape[a]][0]
            g = lax.dynamic_slice_in_dim(g, chip * weights[n].shape[ax], weights[n].shape[ax], axis=ax)
        small_grad[n] = g

    shapes = [weights[n].shape for n in small]
    pw_, pm_, pv_, pg_ = (_pack([d[n] for n in small]) for d in (weights, mom1, mom2, small_grad))
    _, sd, sm, sv = _adamw(pw_[None], pm_[None], pv_[None], [pg_], 'adamw_small')
    for n, d_, m_, v_ in zip(small, _unpack(sd[0], shapes), _unpack(sm[0], shapes), _unpack(sv[0], shapes)):
        out_g[n], out_d[n], out_m[n], out_v[n] = small_grad[n], d_, m_, v_

    return (loss, grad_x, *[out_g[n] for n in names], *[out_d[n] for n in names],
            *[out_m[n] for n in names], *[out_v[n] for n in names])
```

```python
import functools
import math

import jax
import jax.numpy as jnp
from jax import lax
from jax.experimental import pallas as pl
from jax.experimental.pallas import tpu as pltpu

F32 = jnp.float32
BF16 = jnp.bfloat16

EPS = 1e-6
CHUNK = 64
GLA_GATE_TEMP = 16.0
S5_EIG_CLIP = -1e-4
S5_SEGMENTS = 8
S5_GROUPS_PER_BLOCK = 8
LANES = 128
ADAM_LR = 0.001
ADAM_B1 = 0.9
ADAM_B2 = 0.999
ADAM_EPS = 1e-08
ADAM_WD = 0.01
ADAM_STEP = 10
VMEM_LIMIT_BYTES = 56 * 1024 * 1024
PAIR_PIECE_BYTES = 2 * 1024 * 1024
GATHERS_IN_FLIGHT = 2
PAIR_VMEM_BYTES = 40 * 1024 * 1024

MESH = pl.DeviceIdType.MESH
ANY = pl.BlockSpec(memory_space=pl.ANY)
IN_VMEM = pl.BlockSpec(memory_space=pltpu.VMEM)
IN_HBM = pl.BlockSpec(memory_space=pltpu.HBM)
IN_SEM = pl.BlockSpec(memory_space=pltpu.SEMAPHORE)
DATAFLOW = pltpu.SideEffectType.DATAFLOW_SIDE_EFFECTING


def _pcall(body, **kw):
    return pl.pallas_call(body, **kw)


def _params(*sem):
    return pltpu.CompilerParams(dimension_semantics=sem, vmem_limit_bytes=VMEM_LIMIT_BYTES)


def _tile(dim, target, unit=LANES):
    if dim <= target:
        return dim
    best = None
    for t in range(unit, target + 1, unit):
        if dim % t == 0:
            best = t
    assert best is not None, (dim, target)
    return best


def _exchange(x, group, mode, name):
    n = 2 if group == 'c' else 4
    blk = x.shape if mode == 'bcast' else x.shape[1:]
    if mode == 'a2a':
        assert x.shape[0] == n
    flips = [(0, 0, 1)] if group == 'c' else [(1, 0, 0), (0, 1, 0), (1, 1, 0)]
    itemsize = jnp.dtype(x.dtype).itemsize
    staged = group == 'c' and (x.size + n * math.prod(blk)) * itemsize <= PAIR_VMEM_BYTES
    if group == 'c' and not staged:
        ic = lax.axis_index('c')
        own = x if mode == 'bcast' else lax.dynamic_index_in_dim(x, ic, 0, keepdims=False)
        return lax.dynamic_update_index_in_dim(_pair_exchange_chunked(x, mode, name), own, ic, 0)

    def body(x_ref, y_ref, send_sems, recv_sems, local_sem):
        ix, iy, ic = lax.axis_index('x'), lax.axis_index('y'), lax.axis_index('c')

        def slot(px, py, pc):
            return pc if group == 'c' else 2 * px + py

        def src(px, py, pc):
            if mode == 'a2a':
                return x_ref.at[slot(px, py, pc)]
            if mode == 'bcast_c':
                return x_ref.at[ic]
            return x_ref

        me = (ix, iy, ic)
        local = pltpu.make_async_copy(src(*me), y_ref.at[slot(*me)], local_sem)
        local.start()
        peers = []
        for fx, fy, fc in flips:
            peers.append((1 - ix if fx else ix, 1 - iy if fy else iy, 1 - ic if fc else ic))
        sends = []
        for k, peer in enumerate(peers):
            cp = pltpu.make_async_remote_copy(
                src_ref=src(*peer), dst_ref=y_ref.at[slot(*me)],
                send_sem=send_sems.at[k], recv_sem=recv_sems.at[k],
                device_id=peer, device_id_type=MESH)
            cp.start()
            sends.append(cp)
        for k, peer in enumerate(peers):
            pltpu.make_async_remote_copy(
                src_ref=src(*peer), dst_ref=y_ref.at[slot(*peer)],
                send_sem=send_sems.at[k], recv_sem=recv_sems.at[k],
                device_id=peer, device_id_type=MESH).wait_recv()
        for cp in sends:
            cp.wait_send()
        local.wait()

    return _pcall(
        body, name=name,
        out_shape=jax.ShapeDtypeStruct((n,) + tuple(blk), x.dtype),
        in_specs=[IN_VMEM if staged else ANY], out_specs=IN_VMEM if staged else ANY,
        scratch_shapes=[pltpu.SemaphoreType.DMA((len(flips),)),
                        pltpu.SemaphoreType.DMA((len(flips),)),
                        pltpu.SemaphoreType.DMA(())],
        compiler_params=pltpu.CompilerParams(vmem_limit_bytes=VMEM_LIMIT_BYTES),
    )(x)


def _split_axis(blk, dtype, piece_bytes):
    itemsize = jnp.dtype(dtype).itemsize
    sublanes = 8 * 4 // itemsize
    want = max(1, math.prod(blk) * itemsize // piece_bytes)
    for pieces in [s for s in (64, 32, 16, 8, 4, 2) if s <= want]:
        for ax in range(len(blk) - 1):
            unit = sublanes if ax == len(blk) - 2 else 1
            if blk[ax] % (pieces * unit) == 0:
                return ax, pieces
    return 0, 1


def _pair_exchange_chunked(x, mode, name, reduce=False):
    assert mode == 'a2a' or not reduce
    blk = x.shape if mode == 'bcast' else x.shape[1:]
    ax, pieces = _split_axis(blk, x.dtype, PAIR_PIECE_BYTES)
    step = blk[ax] // pieces
    piece_shape = tuple(blk[:ax]) + (step,) + tuple(blk[ax + 1:])

    def piece(ref, p):
        return ref.at[(slice(None),) * ax + (pl.ds(p * step, step),)]

    def body(x_ref, y_ref, out_buf, in_buf, own_buf, send_sems, recv_sems, stage_sems, drain_sems, own_sems, credit_sem):
        ix, iy, ic = lax.axis_index('x'), lax.axis_index('y'), lax.axis_index('c')
        sibling = (ix, iy, 1 - ic)
        mine = x_ref.at[ic] if mode != 'bcast' else x_ref
        theirs = x_ref.at[1 - ic] if mode == 'a2a' else mine

        def own(p):
            return pltpu.make_async_copy(piece(mine, p), own_buf.at[p % 2], own_sems.at[p % 2])

        def stage(p):
            return pltpu.make_async_copy(piece(theirs, p), out_buf.at[p % 2], stage_sems.at[p % 2])

        def remote(p):
            return pltpu.make_async_remote_copy(
                src_ref=out_buf.at[p % 2], dst_ref=in_buf.at[p % 2],
                send_sem=send_sems.at[p], recv_sem=recv_sems.at[p],
                device_id=sibling, device_id_type=MESH)

        def drain(p):
            dst = y_ref if reduce else y_ref.at[1 - ic]
            return pltpu.make_async_copy(in_buf.at[p % 2], piece(dst, p), drain_sems.at[p % 2])

        stage(0).start()
        if reduce:
            own(0).start()
        for p in range(pieces):
            stage(p).wait()
            if p >= 2:
                pl.semaphore_wait(credit_sem, 1)
            remote(p).start()
            if p + 1 < pieces:
                if p >= 1:
                    remote(p - 1).wait_send()
                stage(p + 1).start()
                if reduce:
                    own(p + 1).start()
            remote(p).wait_recv()
            if reduce:
                own(p).wait()
                in_buf[p % 2] = (in_buf[p % 2].astype(F32) + own_buf[p % 2].astype(F32)).astype(in_buf.dtype)
            drain(p).start()
            drain(p).wait()
            if p + 2 < pieces:
                pl.semaphore_signal(credit_sem, inc=1, device_id=sibling, device_id_type=MESH)
        for p in range(max(0, pieces - 2), pieces):
            remote(p).wait_send()

    return _pcall(
        body, name=name,
        out_shape=jax.ShapeDtypeStruct(tuple(blk) if reduce else (2,) + tuple(blk), x.dtype),
        in_specs=[ANY], out_specs=ANY,
        scratch_shapes=[pltpu.VMEM((2,) + piece_shape, x.dtype), pltpu.VMEM((2,) + piece_shape, x.dtype),
                        pltpu.VMEM((2,) + piece_shape if reduce else (2, 8, LANES), x.dtype),
                        pltpu.SemaphoreType.DMA((pieces,)), pltpu.SemaphoreType.DMA((pieces,)),
                        pltpu.SemaphoreType.DMA((2,)), pltpu.SemaphoreType.DMA((2,)), pltpu.SemaphoreType.DMA((2,)),
                        pltpu.SemaphoreType.REGULAR],
        compiler_params=pltpu.CompilerParams(vmem_limit_bytes=VMEM_LIMIT_BYTES),
    )(x)


_FLIPS = {'xy': [(1, 0, 0), (0, 1, 0), (1, 1, 0)], 'c': [(0, 0, 1)]}


def _split_copies(group, mode, x_ref, land_ref, sems):
    ix, iy, ic = lax.axis_index('x'), lax.axis_index('y'), lax.axis_index('c')
    slot = (lambda px, py, pc: pc) if group == 'c' else (lambda px, py, pc: 2 * px + py)
    n_peers = len(_FLIPS[group])
    out = []
    for k, (fx, fy, fc) in enumerate(_FLIPS[group]):
        peer = (1 - ix if fx else ix, 1 - iy if fy else iy, 1 - ic if fc else ic)
        if mode == 'a2a':
            src = x_ref.at[slot(*peer)]
        elif mode == 'bcast_c':
            src = x_ref.at[ic]
        else:
            src = x_ref
        mk = lambda dst, src=src, k=k, peer=peer: pltpu.make_async_remote_copy(
            src_ref=src, dst_ref=dst, send_sem=sems[k], recv_sem=sems[n_peers + k],
            device_id=peer, device_id_type=MESH)
        out.append((mk(land_ref.at[slot(ix, iy, ic)]), mk(land_ref.at[slot(*peer)])))
    return out


def _xy_start(x, mode, name, after, group='xy'):
    blk = x.shape if mode == 'bcast' else x.shape[1:]
    land_shape = (2 if group == 'c' else 4,) + tuple(blk)
    n_after = len(after)
    n_sems = 2 * len(_FLIPS[group])

    def body(*refs):
        x_ref, land_ref = refs[0], refs[1]
        sems = refs[2 + n_after:2 + n_sems + n_after]
        for send, _ in _split_copies(group, mode, x_ref, land_ref, sems):
            send.start()
        refs[-1][...] = jnp.zeros_like(refs[-1])

    outs = _pcall(
        body, name=name,
        out_shape=(pltpu.SemaphoreType.DMA(()),) * n_sems
        + (pltpu.HBM(x.shape, x.dtype), pltpu.HBM(land_shape, x.dtype), jax.ShapeDtypeStruct((8, LANES), F32)),
        in_specs=(IN_HBM, IN_HBM) + (ANY,) * n_after,
        out_specs=(IN_SEM,) * n_sems + (IN_HBM, IN_HBM, IN_VMEM),
        input_output_aliases={0: n_sems, 1: n_sems + 1},
        compiler_params=pltpu.CompilerParams(has_side_effects=DATAFLOW),
    )(pltpu.with_memory_space_constraint(x, pltpu.HBM),
      pltpu.with_memory_space_constraint(lax.empty(land_shape, x.dtype), pltpu.HBM), *after)
    return dict(sems=outs[:n_sems], sent=outs[n_sems], land=outs[n_sems + 1], token=outs[n_sems + 2], mode=mode,
                group=group)


def _xy_wait(handle, name, after):
    mode, group = handle['mode'], handle['group']
    n_after = len(after)
    n_sems = len(handle['sems'])

    def body(*refs):
        x_ref, land_ref = refs[0], refs[1]
        for _, recv in _split_copies(group, mode, x_ref, land_ref, refs[2:2 + n_sems]):
            recv.wait_send()
            recv.wait_recv()

    sent, land = handle['sent'], handle['land']
    return _pcall(
        body, name=name,
        out_shape=(pltpu.HBM(sent.shape, sent.dtype), pltpu.HBM(land.shape, land.dtype)),
        in_specs=(IN_HBM, IN_HBM) + (IN_SEM,) * n_sems + (ANY,) * n_after,
        out_specs=(IN_HBM, IN_HBM),
        input_output_aliases={0: 0, 1: 1},
        compiler_params=pltpu.CompilerParams(has_side_effects=DATAFLOW),
    )(sent, land, *handle['sems'], *after)


_KIND_GROUP = {'chips_a2a': 'xy', 'chips_gather': 'xy', 'pair_swap': 'c', 'pair_bcast': 'c', 'pair_inplace': 'c'}


def _plan(kind, x_ref, land_ref, sems):
    ix, iy, ic = lax.axis_index('x'), lax.axis_index('y'), lax.axis_index('c')
    flips = _FLIPS[_KIND_GROUP[kind]]
    me_chip = 2 * ix + iy
    out = []
    for k, (fx, fy, fc) in enumerate(flips):
        peer = (1 - ix if fx else ix, 1 - iy if fy else iy, 1 - ic if fc else ic)
        peer_chip = 2 * peer[0] + peer[1]
        if kind == 'chips_a2a':
            src, dst, got = x_ref.at[peer_chip], land_ref.at[me_chip], land_ref.at[peer_chip]
        elif kind == 'chips_gather':
            src, dst, got = x_ref.at[ic], land_ref.at[ic, me_chip], land_ref.at[ic, peer_chip]
        elif kind == 'pair_swap':
            src, dst, got = x_ref.at[1 - ic], land_ref, land_ref
        elif kind == 'pair_bcast':
            src, dst, got = x_ref, land_ref.at[ic], land_ref.at[1 - ic]
        else:
            src, dst, got = land_ref.at[ic], land_ref.at[ic], land_ref.at[1 - ic]
        mk = lambda d, src=src, k=k, peer=peer: pltpu.make_async_remote_copy(
            src_ref=src, dst_ref=d, send_sem=sems[k], recv_sem=sems[len(flips) + k],
            device_id=peer, device_id_type=MESH)
        out.append((mk(dst), mk(got)))
    return out


def _start(kind, x, land, name, after):
    n_sems = 2 * len(_FLIPS[_KIND_GROUP[kind]])
    if not hasattr(land, 'dtype'):
        land = lax.empty(tuple(land), x.dtype)
    arrays = ([] if x is None else [x]) + [land]
    n_arr, n_after = len(arrays), len(after)

    def body(*refs):
        sems = refs[n_arr + n_after:n_arr + n_after + n_sems]
        for send, _ in _plan(kind, None if x is None else refs[0], refs[n_arr - 1], sems):
            send.start()
        refs[-1][...] = jnp.zeros_like(refs[-1])

    outs = _pcall(
        body, name=name,
        out_shape=(pltpu.SemaphoreType.DMA(()),) * n_sems + tuple(pltpu.HBM(a.shape, a.dtype) for a in arrays)
        + (jax.ShapeDtypeStruct((8, LANES), F32),),
        in_specs=(IN_HBM,) * n_arr + (ANY,) * n_after,
        out_specs=(IN_SEM,) * n_sems + (IN_HBM,) * n_arr + (IN_VMEM,),
        input_output_aliases={i: n_sems + i for i in range(n_arr)},
        compiler_params=pltpu.CompilerParams(has_side_effects=DATAFLOW),
    )(*[pltpu.with_memory_space_constraint(a, pltpu.HBM) for a in arrays], *after)
    return dict(kind=kind, sems=outs[:n_sems], arrays=outs[n_sems:n_sems + n_arr], token=outs[-1])


def _wait(handle, name, after):
    kind, arrays, sems = handle['kind'], handle['arrays'], handle['sems']
    n_arr, n_sems = len(arrays), len(sems)

    def body(*refs):
        for _, got in _plan(kind, refs[0] if n_arr == 2 else None, refs[n_arr - 1], refs[n_arr:n_arr + n_sems]):
            got.wait_send()
            got.wait_recv()

    return _pcall(
        body, name=name,
        out_shape=tuple(pltpu.HBM(a.shape, a.dtype) for a in arrays),
        in_specs=(IN_HBM,) * n_arr + (IN_SEM,) * n_sems + (ANY,) * len(after),
        out_specs=(IN_HBM,) * n_arr,
        input_output_aliases={i: i for i in range(n_arr)},
        compiler_params=pltpu.CompilerParams(has_side_effects=DATAFLOW),
    )(*arrays, *sems, *after)


def _sum_pair(x, recv, ic, out_dtype, name):
    _, rows, cols = x.shape
    tm = _tile(rows, max(8, (1 << 19) // cols // 8 * 8), unit=8)

    def body(c_ref, x_ref, r_ref, o_ref):
        o_ref[...] = (x_ref[...].astype(F32) + r_ref[...].astype(F32)).astype(o_ref.dtype)

    return _pcall(
        body, name=name,
        grid_spec=pltpu.PrefetchScalarGridSpec(
            num_scalar_prefetch=1, grid=(rows // tm,),
            in_specs=[pl.BlockSpec((None, tm, cols), lambda i, c: (c[0], i, 0)),
                      pl.BlockSpec((tm, cols), lambda i, c: (i, 0))],
            out_specs=pl.BlockSpec((tm, cols), lambda i, c: (i, 0))),
        out_shape=jax.ShapeDtypeStruct((rows, cols), out_dtype),
        compiler_params=_params('parallel'),
    )(jnp.reshape(ic, (1,)).astype(jnp.int32), x, recv)


def _sum_slots(y, out_dtype, name):
    n, rows, cols = y.shape
    tm = _tile(rows, max(8, (1 << 20) // (n * cols) // 8 * 8), unit=8)

    def body(y_ref, o_ref):
        acc = y_ref[0].astype(F32)
        for k in range(1, n):
            acc = acc + y_ref[k].astype(F32)
        o_ref[...] = acc.astype(o_ref.dtype)

    return _pcall(
        body, name=name, grid=(rows // tm,),
        out_shape=jax.ShapeDtypeStruct((rows, cols), out_dtype),
        in_specs=[pl.BlockSpec((n, tm, cols), lambda i: (0, i, 0))],
        out_specs=pl.BlockSpec((tm, cols), lambda i: (i, 0)),
        compiler_params=_params('parallel'),
    )(y)


class _Op:
    def __init__(self, arr, spec):
        self.arr = arr
        self.spec = spec


def _plain(arr):
    return _Op(arr, lambda t0, t1: ((t0, t1), lambda b0, b1: (b0, b1)))


def _plain_shape(shape):
    return lambda t0, t1: ((t0, t1), lambda b0, b1: (b0, b1))


def _dw_cols(rows, cols):
    rh = rows // 2

    def spec(t0, t1):
        assert (rh % t0 == 0 or t0 == rows) and cols % t1 == 0, (rh, cols, t0, t1)
        qr, qc = max(1, rh // t0), cols // t1
        if t0 == rows:
            return (2, None, rh, t1), lambda b0, b1: (0, b1 // qc, 0, b1 % qc)
        return (None, None, t0, t1), lambda b0, b1: (b0 // qr, b1 // qc, b0 % qr, b1 % qc)
    return (2, 4, rh, cols), spec


def _dw_rows(rows, cols):
    rh = rows // 2

    def spec(t0, t1):
        assert (rh % t0 == 0 or t0 == rows) and cols % t1 == 0, (rh, cols, t0, t1)
        qr = max(1, rh // t0)
        if t0 == rows:
            return (2, None, rh, t1), lambda b0, b1: (0, b0, 0, b1)
        return (None, None, t0, t1), lambda b0, b1: ((b0 // qr) % 2, b0 // (2 * qr), b0 % qr, b1)
    return (2, 4, rh, cols), spec


def _w_cols(g, j):
    _, _, rh, cols = g[j].shape
    return _Op(g[j], _dw_cols(2 * rh, cols)[1])


def _w_rows(g, j):
    _, _, rh, cols = g[j].shape
    return _Op(g[j], _dw_rows(2 * rh, cols)[1])


def _mm(name, mode, a, b, dims, outs, tiles, epilogue=None, extras=(), after=()):
    m, n, k = dims
    tm, tn, tk = tiles
    assert m % tm == 0 and n % tn == 0 and k % tk == 0, (name, dims, tiles)
    nk = k // tk
    if mode == 'nn':
        a_t, a_ix, b_t, b_ix, ca, cb = (tm, tk), (lambda i, j, kk: (i, kk)), (tk, tn), (lambda i, j, kk: (kk, j)), 1, 0
    elif mode == 'nt':
        a_t, a_ix, b_t, b_ix, ca, cb = (tm, tk), (lambda i, j, kk: (i, kk)), (tn, tk), (lambda i, j, kk: (j, kk)), 1, 1
    else:
        a_t, a_ix, b_t, b_ix, ca, cb = (tk, tm), (lambda i, j, kk: (kk, i)), (tk, tn), (lambda i, j, kk: (kk, j)), 0, 0
    a_blk, a_fn = a.spec(*a_t)
    b_blk, b_fn = b.spec(*b_t)
    in_specs = [pl.BlockSpec(a_blk, lambda i, j, kk: a_fn(*a_ix(i, j, kk))),
                pl.BlockSpec(b_blk, lambda i, j, kk: b_fn(*b_ix(i, j, kk)))]
    operands = [a.arr, b.arr]
    for e in extras:
        e_blk, e_fn = e.spec(tm, tn)
        in_specs.append(pl.BlockSpec(e_blk, functools.partial(lambda i, j, kk, f: f(i, j), f=e_fn)))
        operands.append(e.arr)
    out_shapes, out_specs = [], []
    for shape, dtype, spec in outs:
        o_blk, o_fn = spec(tm, tn)
        out_shapes.append(jax.ShapeDtypeStruct(shape, dtype))
        out_specs.append(pl.BlockSpec(o_blk, functools.partial(lambda i, j, kk, f: f(i, j), f=o_fn)))
    n_ex, n_out = len(extras), len(outs)
    in_specs += [ANY] * len(after)
    operands += list(after)
    if epilogue is None:
        epilogue = lambda acc: (acc,)

    def body(a_ref, b_ref, *rest):
        ex_refs = rest[:n_ex]
        rest = rest[:n_ex] + rest[n_ex + len(after):]
        out_refs = rest[n_ex:n_ex + n_out]
        bv = b_ref[...]
        if bv.ndim == 3:
            bv = bv.reshape(bv.shape[0] * bv.shape[1], bv.shape[2])
        p = lax.dot_general(a_ref[...].astype(BF16), bv.astype(BF16),
                            (((ca,), (cb,)), ((), ())), preferred_element_type=F32)

        def finish(acc):
            res = epilogue(acc, *[r[...] for r in ex_refs])
            for o_ref, val in zip(out_refs, res):
                o_ref[...] = val.astype(o_ref.dtype)

        if nk == 1:
            finish(p)
        else:
            acc_ref = rest[n_ex + n_out]
            kk = pl.program_id(2)

            @pl.when(kk == 0)
            def _():
                acc_ref[...] = p

            @pl.when(kk > 0)
            def _():
                acc_ref[...] += p

            @pl.when(kk == nk - 1)
            def _():
                finish(acc_ref[...])

    res = _pcall(
        body, name=name, grid=(m // tm, n // tn, nk),
        out_shape=out_shapes, in_specs=in_specs, out_specs=out_specs,
        scratch_shapes=[pltpu.VMEM((tm, tn), F32)] if nk > 1 else [],
        compiler_params=_params('parallel', 'parallel', 'arbitrary'),
    )(*operands)
    return res


def _rowwise(name, fn, row_ins, vec_ins, outs, reds, tm, after=()):
    rows = row_ins[0].shape[0]
    assert rows % tm == 0
    n_in = len(row_ins) + len(vec_ins)
    n_out = len(outs)

    def body(*refs):
        vals = [r[...] for r in refs[:n_in]]
        refs = refs[:n_in] + refs[n_in + len(after):]
        res = fn(*vals)
        for o_ref, val in zip(refs[n_in:n_in + n_out], res[:n_out]):
            o_ref[...] = val.astype(o_ref.dtype)
        first = pl.program_id(0) == 0
        for r_ref, val in zip(refs[n_in + n_out:], res[n_out:]):
            @pl.when(first)
            def _(r_ref=r_ref, val=val):
                r_ref[...] = val

            @pl.when(jnp.logical_not(first))
            def _(r_ref=r_ref, val=val):
                r_ref[...] += val

    in_specs = [pl.BlockSpec((tm, a.shape[1]), lambda i: (i, 0)) for a in row_ins]
    in_specs += [pl.BlockSpec((1, v.shape[1]), lambda i: (0, 0)) for v in vec_ins]
    in_specs += [ANY] * len(after)
    out_shapes = [jax.ShapeDtypeStruct((rows, w), dt) for w, dt in outs]
    out_shapes += [jax.ShapeDtypeStruct((1, w), F32) for w in reds]
    out_specs = [pl.BlockSpec((tm, w), lambda i: (i, 0)) for w, _ in outs]
    out_specs += [pl.BlockSpec((1, w), lambda i: (0, 0)) for w in reds]
    return _pcall(
        body, name=name, grid=(rows // tm,),
        out_shape=out_shapes, in_specs=in_specs, out_specs=out_specs,
        compiler_params=_params('arbitrary'),
    )(*row_ins, *vec_ins, *after)


def _norm_fwd(h, g, name, after=()):
    def fn(hv, gv):
        rstd = lax.rsqrt(jnp.mean(hv * hv, axis=-1, keepdims=True) + EPS)
        return (hv * rstd * gv,)
    return _rowwise(name, fn, [h], [g], [(h.shape[1], BF16)], [], 256, after)[0]


def _norm_bwd(h, dhn, dres, g, name):
    def fn(hv, dv, rv, gv):
        rstd = lax.rsqrt(jnp.mean(hv * hv, axis=-1, keepdims=True) + EPS)
        xhat = hv * rstd
        dxhat = dv * gv
        dh = rv + rstd * (dxhat - xhat * jnp.mean(dxhat * xhat, axis=-1, keepdims=True))
        return dh, jnp.sum(dv * xhat, axis=0, keepdims=True)
    w = h.shape[1]
    return _rowwise(name, fn, [h, dhn, dres], [g], [(w, F32)], [w], 256)


def _loss_head(h, target, g, name):
    w = h.shape[1]

    def fn(hv, tv, gv):
        rstd = lax.rsqrt(jnp.mean(hv * hv, axis=-1, keepdims=True) + EPS)
        xhat = hv * rstd
        diff = xhat * gv - tv
        dy = diff * (1.0 / w)
        dxhat = dy * gv
        dh = rstd * (dxhat - xhat * jnp.mean(dxhat * xhat, axis=-1, keepdims=True))
        return (dh, jnp.sum(0.5 * dy * diff, axis=0, keepdims=True),
                jnp.sum(dy * xhat, axis=0, keepdims=True))
    return _rowwise(name, fn, [h, target], [g], [(w, F32)], [w, w], 256)


def _split3(x):
    hi = x.astype(BF16)
    r1 = x - hi.astype(F32)
    mid = r1.astype(BF16)
    lo = (r1 - mid.astype(F32)).astype(BF16)
    return hi, mid, lo


def _tri_dot(tri, x):
    hi, mid, lo = _split3(x)
    d = lambda p: jnp.dot(tri, p, preferred_element_type=F32)
    return d(hi) + d(mid) + d(lo)


def _log_sigmoid(x):
    return jnp.minimum(x, 0.0) - jnp.log(1.0 + jnp.exp(-jnp.abs(x)))


def _gla_dims(proj_w, kw, vw, dk, dv):
    assert kw % dk == 0 and (2 * kw) % dv == 0 and (2 * kw + vw) % dv == 0 and (2 * kw + 2 * vw) % LANES == 0
    return dict(q0=0, k0=kw // dk, v0=2 * kw // dv, r0=(2 * kw + vw) // dv, g0=(2 * kw + 2 * vw) // LANES)


def _gla_gates(gl, wgu, bias):
    pre = jnp.dot(gl.astype(BF16), wgu, preferred_element_type=F32) + bias
    la = _log_sigmoid(pre) * (1.0 / GLA_GATE_TEMP)
    r_i = lax.broadcasted_iota(jnp.int32, (CHUNK, CHUNK), 0)
    c_i = lax.broadcasted_iota(jnp.int32, (CHUNK, CHUNK), 1)
    cum = _tri_dot((c_i <= r_i).astype(BF16), la)
    total = cum[CHUNK - 1:CHUNK, :]
    return pre, cum, total


def _gla_scan_fwd(proj, wgu_pad, b_gate, o_norm, heads, kw, vw, tb, name):
    seq, pw = proj.shape
    dk, dv = kw // heads, vw // heads
    cb = tb // CHUNK
    nt = seq // tb
    o = _gla_dims(pw, kw, vw, dk, dv)
    scale = dk ** -0.5

    def body(q_ref, k_ref, v_ref, r_ref, gl_ref, wgu_ref, b_ref, on_ref, out_ref, st_ref, s_scr):
        @pl.when(pl.program_id(1) == 0)
        def _():
            s_scr[...] = jnp.zeros_like(s_scr)

        wgu = wgu_ref[...].astype(BF16)
        bias = b_ref[...]
        onorm = on_ref[...]
        st = s_scr[...]
        for ci in range(cb):
            rows = pl.ds(ci * CHUNK, CHUNK)
            _, cum, total = _gla_gates(gl_ref[rows, :], wgu, bias)
            kdec = k_ref[rows, :] * jnp.exp(total - cum)
            st = st * jnp.exp(total) + lax.dot_general(
                v_ref[rows, :].astype(BF16), kdec.astype(BF16), (((0,), (0,)), ((), ())),
                preferred_element_type=F32)
            st_ref[ci] = st
            qs = (q_ref[rows, :] * scale).astype(BF16)
            ov = lax.dot_general(qs, st.astype(BF16), (((1,), (1,)), ((), ())), preferred_element_type=F32)
            rstd = lax.rsqrt(jnp.mean(ov * ov, axis=-1, keepdims=True) + EPS)
            rv = r_ref[rows, :]
            out_ref[rows, :] = (ov * rstd * onorm * (rv * jax.nn.sigmoid(rv))).astype(out_ref.dtype)
        s_scr[...] = st

    in_specs = [
        pl.BlockSpec((tb, dk), lambda h, t: (t, o['q0'] + h)),
        pl.BlockSpec((tb, dk), lambda h, t: (t, o['k0'] + h)),
        pl.BlockSpec((tb, dv), lambda h, t: (t, o['v0'] + h)),
        pl.BlockSpec((tb, dv), lambda h, t: (t, o['r0'] + h)),
        pl.BlockSpec((tb, LANES), lambda h, t: (t, o['g0'])),
        pl.BlockSpec((LANES, dk), lambda h, t: (0, h)),
        pl.BlockSpec((1, dk), lambda h, t: (0, h)),
        pl.BlockSpec((1, dv), lambda h, t: (0, 0)),
    ]
    return _pcall(
        body, name=name, grid=(heads, nt),
        out_shape=[jax.ShapeDtypeStruct((seq, vw), BF16),
                   jax.ShapeDtypeStruct((heads, seq // CHUNK, dv, dk), F32)],
        in_specs=in_specs,
        out_specs=[pl.BlockSpec((tb, dv), lambda h, t: (t, h)),
                   pl.BlockSpec((None, cb, dv, dk), lambda h, t: (h, t, 0, 0))],
        scratch_shapes=[pltpu.VMEM((dv, dk), F32)],
        compiler_params=_params('parallel', 'arbitrary'),
    )(proj, proj, proj, proj, proj, wgu_pad, b_gate, o_norm)


def _gla_scan_bwd(proj, wgu_pad, b_gate, o_norm, states, dgated, heads, kw, vw, tb, name):
    seq, pw = proj.shape
    dk, dv = kw // heads, vw // heads
    cb = tb // CHUNK
    nt = seq // tb
    o = _gla_dims(pw, kw, vw, dk, dv)
    scale = dk ** -0.5

    def body(q_ref, k_ref, v_ref, r_ref, gl_ref, wgu_ref, b_ref, on_ref, st_ref, stp_ref, dg_ref,
             dq_ref, dk_ref, dv_ref, dr_ref, dpre_ref, db_ref, don_ref, ds_scr):
        hh = pl.program_id(0)
        t = pl.program_id(1)

        @pl.when(t == 0)
        def _():
            ds_scr[...] = jnp.zeros_like(ds_scr)
            db_ref[...] = jnp.zeros_like(db_ref)

        @pl.when(jnp.logical_and(hh == 0, t == 0))
        def _():
            don_ref[...] = jnp.zeros_like(don_ref)

        wgu = wgu_ref[...].astype(BF16)
        bias = b_ref[...]
        onorm = on_ref[...]
        has_prev = (t < nt - 1).astype(F32)
        r_i = lax.broadcasted_iota(jnp.int32, (CHUNK, CHUNK), 0)
        c_i = lax.broadcasted_iota(jnp.int32, (CHUNK, CHUNK), 1)
        strict = (c_i < r_i).astype(BF16)
        carry = ds_scr[...]
        db_acc = jnp.zeros((1, dk), F32)
        don_acc = jnp.zeros((1, dv), F32)
        for ci in reversed(range(cb)):
            rows = pl.ds(ci * CHUNK, CHUNK)
            pre, cum, total = _gla_gates(gl_ref[rows, :], wgu, bias)
            edec = jnp.exp(total - cum)
            decay = jnp.exp(total)
            kdec = k_ref[rows, :] * edec
            st = st_ref[ci]
            st_prev = st_ref[ci - 1] if ci > 0 else stp_ref[0] * has_prev
            stb = st.astype(BF16)
            qs = (q_ref[rows, :] * scale).astype(BF16)
            vb = v_ref[rows, :].astype(BF16)
            ov = lax.dot_general(qs, stb, (((1,), (1,)), ((), ())), preferred_element_type=F32)
            rstd = lax.rsqrt(jnp.mean(ov * ov, axis=-1, keepdims=True) + EPS)
            ohat = ov * rstd
            rv = r_ref[rows, :]
            sr = jax.nn.sigmoid(rv)
            dgv = dg_ref[rows, :]
            dy = dgv * (rv * sr)
            dr_ref[rows, :] = (dgv * (ohat * onorm) * (sr * (1.0 + rv * (1.0 - sr)))).astype(dr_ref.dtype)
            don_acc = don_acc + jnp.sum(dy * ohat, axis=0, keepdims=True)
            dohat = dy * onorm
            do = (rstd * (dohat - ohat * jnp.mean(dohat * ohat, axis=-1, keepdims=True))).astype(BF16)
            dq_ref[rows, :] = (jnp.dot(do, stb, preferred_element_type=F32) * scale).astype(dq_ref.dtype)
            dst = carry + lax.dot_general(do, qs, (((0,), (0,)), ((), ())), preferred_element_type=F32)
            dstb = dst.astype(BF16)
            dkdec = jnp.dot(vb, dstb, preferred_element_type=F32)
            dv_ref[rows, :] = lax.dot_general(kdec.astype(BF16), dstb, (((1,), (1,)), ((), ())),
                                              preferred_element_type=F32).astype(dv_ref.dtype)
            ddecay = jnp.sum(dst * st_prev, axis=0, keepdims=True)
            dk_ref[rows, :] = (dkdec * edec).astype(dk_ref.dtype)
            da = ddecay * decay + _tri_dot(strict, dkdec * kdec)
            dpre = da * (1.0 / GLA_GATE_TEMP) * (1.0 - jax.nn.sigmoid(pre))
            dpre_ref[rows, :] = dpre.astype(dpre_ref.dtype)
            db_acc = db_acc + jnp.sum(dpre, axis=0, keepdims=True)
            carry = dst * decay
        ds_scr[...] = carry
        db_ref[...] += db_acc
        don_ref[...] += don_acc

    rt = lambda t: nt - 1 - t
    in_specs = [
        pl.BlockSpec((tb, dk), lambda h, t: (rt(t), o['q0'] + h)),
        pl.BlockSpec((tb, dk), lambda h, t: (rt(t), o['k0'] + h)),
        pl.BlockSpec((tb, dv), lambda h, t: (rt(t), o['v0'] + h)),
        pl.BlockSpec((tb, dv), lambda h, t: (rt(t), o['r0'] + h)),
        pl.BlockSpec((tb, LANES), lambda h, t: (rt(t), o['g0'])),
        pl.BlockSpec((LANES, dk), lambda h, t: (0, h)),
        pl.BlockSpec((1, dk), lambda h, t: (0, h)),
        pl.BlockSpec((1, dv), lambda h, t: (0, 0)),
        pl.BlockSpec((None, cb, dv, dk), lambda h, t: (h, rt(t), 0, 0)),
        pl.BlockSpec((None, 1, dv, dk), lambda h, t: (h, jnp.maximum(rt(t) * cb - 1, 0), 0, 0)),
        pl.BlockSpec((tb, dv), lambda h, t: (rt(t), h)),
    ]
    out_shape = [jax.ShapeDtypeStruct((seq, kw), BF16), jax.ShapeDtypeStruct((seq, kw), BF16),
                 jax.ShapeDtypeStruct((seq, vw), BF16), jax.ShapeDtypeStruct((seq, vw), BF16),
                 jax.ShapeDtypeStruct((seq, kw), BF16),
                 jax.ShapeDtypeStruct((1, kw), F32), jax.ShapeDtypeStruct((1, dv), F32)]
    out_specs = [pl.BlockSpec((tb, dk), lambda h, t: (rt(t), h)),
                 pl.BlockSpec((tb, dk), lambda h, t: (rt(t), h)),
                 pl.BlockSpec((tb, dv), lambda h, t: (rt(t), h)),
                 pl.BlockSpec((tb, dv), lambda h, t: (rt(t), h)),
                 pl.BlockSpec((tb, dk), lambda h, t: (rt(t), h)),
                 pl.BlockSpec((1, dk), lambda h, t: (0, h)),
                 pl.BlockSpec((1, dv), lambda h, t: (0, 0))]
    return _pcall(
        body, name=name, grid=(heads, nt),
        out_shape=out_shape, in_specs=in_specs, out_specs=out_specs,
        scratch_shapes=[pltpu.VMEM((dv, dk), F32)],
        compiler_params=_params('arbitrary', 'arbitrary'),
    )(proj, proj, proj, proj, proj, wgu_pad, b_gate, o_norm, states, states, dgated)


def _cmul(ar, ai, br, bi):
    return ar * br - ai * bi, ar * bi + ai * br


def _gelu(y):
    c = math.sqrt(2.0 / math.pi)
    return 0.5 * y * (1.0 + jnp.tanh(c * (y + 0.044715 * y * y * y)))


def _gelu_grad(y):
    c = math.sqrt(2.0 / math.pi)
    th = jnp.tanh(c * (y + 0.044715 * y * y * y))
    return 0.5 * (1.0 + th) + 0.5 * y * (1.0 - th * th) * (c * (1.0 + 3.0 * 0.044715 * y * y))


def _power_pow2(ar, ai, n):
    assert n & (n - 1) == 0
    for _ in range(n.bit_length() - 1):
        ar, ai = _cmul(ar, ai, ar, ai)
    return ar, ai


def _s5_fwd(u, bre, bim, cre, cim, are, aim, dskip, name):
    seq, width = u.shape
    nb, ub, sb = bre.shape
    ls = seq // S5_SEGMENTS
    seg = S5_SEGMENTS

    def body(u_ref, bre_ref, bim_ref, cre_ref, cim_ref, are_ref, aim_ref, d_ref, y_ref, z_ref, xr_ref, xi_ref):
        uv = u_ref[...]
        ub16 = uv.astype(BF16)
        xr_ref[...] = jnp.dot(ub16, bre_ref[...].astype(BF16), preferred_element_type=F32)
        xi_ref[...] = jnp.dot(ub16, bim_ref[...].astype(BF16), preferred_element_type=F32)
        ar = jnp.broadcast_to(are_ref[...], (seg, sb))
        ai = jnp.broadcast_to(aim_ref[...], (seg, sb))

        def step(i, c):
            rows = pl.ds(pl.multiple_of(i * seg, seg), seg)
            pr, pi = _cmul(ar, ai, c[0], c[1])
            nr = pr + xr_ref[rows, :]
            ni = pi + xi_ref[rows, :]
            xr_ref[rows, :] = nr
            xi_ref[rows, :] = ni
            return nr, ni

        zero = jnp.zeros((seg, sb), F32)
        er, ei = lax.fori_loop(0, ls, step, (zero, zero), unroll=8)
        pr, pi = _power_pow2(ar, ai, ls)
        row = lax.broadcasted_iota(jnp.int32, (seg, sb), 0)
        sr, si = zero, zero
        for _ in range(seg - 1):
            tr, ti = _cmul(pr, pi, sr, si)
            sr = jnp.where(row == 0, 0.0, pltpu.roll(tr + er, 1, 0))
            si = jnp.where(row == 0, 0.0, pltpu.roll(ti + ei, 1, 0))

        def fix(i, c):
            rows = pl.ds(pl.multiple_of(i * seg, seg), seg)
            fr, fi = _cmul(c[0], c[1], sr, si)
            xr_ref[rows, :] += fr
            xi_ref[rows, :] += fi
            return _cmul(c[0], c[1], ar, ai)

        lax.fori_loop(0, ls, fix, (ar, ai), unroll=8)
        y = (jnp.dot(xr_ref[...].astype(BF16), cre_ref[...].astype(BF16), preferred_element_type=F32)
             - jnp.dot(xi_ref[...].astype(BF16), cim_ref[...].astype(BF16), preferred_element_type=F32)
             + d_ref[...] * uv)
        y_ref[...] = y
        z_ref[...] = _gelu(y).astype(z_ref.dtype)

    mat = lambda r, c: pl.BlockSpec((None, r, c), lambda b: (b, 0, 0))
    return _pcall(
        body, name=name, grid=(nb,),
        out_shape=[jax.ShapeDtypeStruct((seq, width), F32), jax.ShapeDtypeStruct((seq, width), BF16),
                   jax.ShapeDtypeStruct((seq, nb * sb), F32), jax.ShapeDtypeStruct((seq, nb * sb), F32)],
        in_specs=[pl.BlockSpec((seq, ub), lambda b: (0, b)), mat(ub, sb), mat(ub, sb), mat(sb, ub), mat(sb, ub),
                  mat(1, sb), mat(1, sb), pl.BlockSpec((1, ub), lambda b: (0, b))],
        out_specs=[pl.BlockSpec((seq, ub), lambda b: (0, b)), pl.BlockSpec((seq, ub), lambda b: (0, b)),
                   pl.BlockSpec((seq, sb), lambda b: (0, b)), pl.BlockSpec((seq, sb), lambda b: (0, b))],
        compiler_params=_params('parallel'),
    )(u, bre, bim, cre, cim, are, aim, dskip)


def _s5_bwd(dz, y, u, xr, xi, bre, bim, cre, cim, are, aim, dskip, name):
    seq, width = u.shape
    nb, ub, sb = bre.shape
    ls = seq // S5_SEGMENTS
    seg = S5_SEGMENTS

    def body(dz_ref, y_ref, u_ref, xr_ref, xi_ref, bre_ref, bim_ref, cre_ref, cim_ref, are_ref, aim_ref, d_ref,
             du_ref, dcr_ref, dci_ref, dbr_ref, dbi_ref, dar_ref, dai_ref, dd_ref, lr_ref, li_ref):
        uv = u_ref[...]
        dy = dz_ref[...] * _gelu_grad(y_ref[...])
        dd_ref[...] = jnp.sum(dy * uv, axis=0, keepdims=True)
        dyb = dy.astype(BF16)
        nt = (((1,), (1,)), ((), ()))
        tn = (((0,), (0,)), ((), ()))
        lr_ref[...] = lax.dot_general(dyb, cre_ref[...].astype(BF16), nt, preferred_element_type=F32)
        li_ref[...] = -lax.dot_general(dyb, cim_ref[...].astype(BF16), nt, preferred_element_type=F32)
        dcr_ref[...] = lax.dot_general(dyb, xr_ref[...].astype(BF16), tn, preferred_element_type=F32)
        dci_ref[...] = -lax.dot_general(dyb, xi_ref[...].astype(BF16), tn, preferred_element_type=F32)
        ar = jnp.broadcast_to(are_ref[...], (seg, sb))
        ai = jnp.broadcast_to(aim_ref[...], (seg, sb))
        nai = -ai

        def step(ii, c):
            rows = pl.ds(pl.multiple_of((ls - 1 - ii) * seg, seg), seg)
            pr, pi = _cmul(ar, nai, c[0], c[1])
            nr = pr + lr_ref[rows, :]
            ni = pi + li_ref[rows, :]
            lr_ref[rows, :] = nr
            li_ref[rows, :] = ni
            return nr, ni

        zero = jnp.zeros((seg, sb), F32)
        er, ei = lax.fori_loop(0, ls, step, (zero, zero), unroll=8)
        pr, pi = _power_pow2(ar, nai, ls)
        row = lax.broadcasted_iota(jnp.int32, (seg, sb), 0)
        rr, ri = zero, zero
        for _ in range(seg - 1):
            tr, ti = _cmul(pr, pi, rr, ri)
            rr = jnp.where(row == seg - 1, 0.0, pltpu.roll(tr + er, seg - 1, 0))
            ri = jnp.where(row == seg - 1, 0.0, pltpu.roll(ti + ei, seg - 1, 0))

        def corrected(rows, qr, qi):
            fr, fi = _cmul(qr, qi, rr, ri)
            nr = lr_ref[rows, :] + fr
            ni = li_ref[rows, :] + fi
            lr_ref[rows, :] = nr
            li_ref[rows, :] = ni
            return nr, ni

        def grad_a(nr, ni, xpr, xpi, accr, acci):
            return accr + nr * xpr + ni * xpi, acci + ni * xpr - nr * xpi

        def fix(ii, c):
            qr, qi, accr, acci = c
            i = ls - 1 - ii
            rows = pl.ds(pl.multiple_of(i * seg, seg), seg)
            prev = pl.ds(pl.multiple_of((i - 1) * seg, seg), seg)
            nr, ni = corrected(rows, qr, qi)
            accr, acci = grad_a(nr, ni, xr_ref[prev, :], xi_ref[prev, :], accr, acci)
            qr, qi = _cmul(qr, qi, ar, nai)
            return qr, qi, accr, acci

        qr, qi, accr, acci = lax.fori_loop(0, ls - 1, fix, (ar, nai, zero, zero), unroll=8)
        nr, ni = corrected(pl.ds(0, seg), qr, qi)
        last = pl.ds((ls - 1) * seg, seg)
        xpr = jnp.where(row == 0, 0.0, pltpu.roll(xr_ref[last, :], 1, 0))
        xpi = jnp.where(row == 0, 0.0, pltpu.roll(xi_ref[last, :], 1, 0))
        accr, acci = grad_a(nr, ni, xpr, xpi, accr, acci)
        dar_ref[...] = jnp.sum(accr, axis=0, keepdims=True)
        dai_ref[...] = jnp.sum(acci, axis=0, keepdims=True)
        lrb = lr_ref[...].astype(BF16)
        lib = li_ref[...].astype(BF16)
        ub16 = uv.astype(BF16)
        dbr_ref[...] = lax.dot_general(ub16, lrb, tn, preferred_element_type=F32)
        dbi_ref[...] = lax.dot_general(ub16, lib, tn, preferred_element_type=F32)
        du_ref[...] = (d_ref[...] * dy
                       + lax.dot_general(lrb, bre_ref[...].astype(BF16), nt, preferred_element_type=F32)
                       + lax.dot_general(lib, bim_ref[...].astype(BF16), nt, preferred_element_type=F32))

    mat = lambda r, c: pl.BlockSpec((None, r, c), lambda b: (b, 0, 0))
    col = lambda w: pl.BlockSpec((seq, w), lambda b: (0, b))
    return _pcall(
        body, name=name, grid=(nb,),
        out_shape=[jax.ShapeDtypeStruct((seq, width), F32)]
        + [jax.ShapeDtypeStruct((nb, ub, sb), F32)] * 4
        + [jax.ShapeDtypeStruct((nb, 1, sb), F32)] * 2
        + [jax.ShapeDtypeStruct((1, width), F32)],
        in_specs=[col(ub), col(ub), col(ub), col(sb), col(sb), mat(ub, sb), mat(ub, sb), mat(sb, ub), mat(sb, ub),
                  mat(1, sb), mat(1, sb), pl.BlockSpec((1, ub), lambda b: (0, b))],
        out_specs=[col(ub), mat(ub, sb), mat(ub, sb), mat(ub, sb), mat(ub, sb), mat(1, sb), mat(1, sb),
                   pl.BlockSpec((1, ub), lambda b: (0, b))],
        scratch_shapes=[pltpu.VMEM((seq, sb), F32), pltpu.VMEM((seq, sb), F32)],
        compiler_params=_params('parallel'),
    )(dz, y, u, xr, xi, bre, bim, cre, cim, are, aim, dskip)


def _s5_discretise(lam_re, lam_im, log_dt, b_re, b_im):
    lr = jnp.minimum(lam_re, S5_EIG_CLIP)
    li = lam_im
    dt = jnp.exp(log_dt)[:, None]
    mag = jnp.exp(lr * dt)
    ang = li * dt
    ab_re = mag * jnp.cos(ang)
    ab_im = mag * jnp.sin(ang)
    den = lr * lr + li * li
    nr = ab_re - 1.0
    f_re = (nr * lr + ab_im * li) / den
    f_im = (ab_im * lr - nr * li) / den
    bb_re = f_re[..., None] * b_re - f_im[..., None] * b_im
    bb_im = f_re[..., None] * b_im + f_im[..., None] * b_re
    return ab_re, ab_im, bb_re, bb_im


def _to_blocks(m):
    g, a, b = m.shape
    gb = S5_GROUPS_PER_BLOCK
    eye = jnp.eye(gb, dtype=m.dtype)
    return jnp.einsum('bgac,gh->bgahc', m.reshape(g // gb, gb, a, b), eye).reshape(g // gb, gb * a, gb * b)


def _from_blocks(m, a, b):
    nb = m.shape[0]
    gb = S5_GROUPS_PER_BLOCK
    eye = jnp.eye(gb, dtype=m.dtype)
    return jnp.einsum('bgahc,gh->bgac', m.reshape(nb, gb, a, gb, b), eye).reshape(nb * gb, a, b)


def _glu_fwd(o, h, name):
    half = o.shape[1] // 2

    def fn(ov, hv):
        return (hv + ov[:, :half] * jax.nn.sigmoid(ov[:, half:]),)
    return _rowwise(name, fn, [o, h], [], [(half, F32)], [], 256)[0]


def _glu_bwd(o, dout, name):
    half = o.shape[1] // 2

    def fn(ov, dv):
        val, gate = ov[:, :half], ov[:, half:]
        sg = jax.nn.sigmoid(gate)
        return (jnp.concatenate([dv * sg, dv * val * sg * (1.0 - sg)], axis=1),)
    return _rowwise(name, fn, [o, dout], [], [(2 * half, BF16)], [], 256)[0]


def _adam_math(w, g, m, v):
    m = ADAM_B1 * m + (1.0 - ADAM_B1) * g
    v = ADAM_B2 * v + (1.0 - ADAM_B2) * (g * g)
    m_hat = m / (1.0 - ADAM_B1 ** ADAM_STEP)
    v_hat = v / (1.0 - ADAM_B2 ** ADAM_STEP)
    delta = -ADAM_LR * (m_hat / (jnp.sqrt(v_hat) + ADAM_EPS) + ADAM_WD * w)
    return delta, m, v


def _adamw(w, m, v, grads, name, after=()):
    nl, rows, cols = w.shape
    tm = _tile(rows, max(8, (1 << 18) // cols // 8 * 8), unit=8)
    nbk = rows // tm

    def body(*refs):
        w_ref, m_ref, v_ref = refs[:3]
        g_refs = refs[3:3 + nl]
        go_ref, d_ref, mo_ref, vo_ref = refs[3 + nl + len(after):]
        layer = pl.program_id(0)
        g = g_refs[0][...]
        for l in range(1, nl):
            g = jnp.where(layer == l, g_refs[l][...], g)
        delta, mn, vn = _adam_math(w_ref[...], g, m_ref[...], v_ref[...])
        go_ref[...] = g
        d_ref[...] = delta
        mo_ref[...] = mn
        vo_ref[...] = vn

    stacked = pl.BlockSpec((None, tm, cols), lambda l, i: (l, i, 0))

    def g_spec(layer):
        return pl.BlockSpec((tm, cols), lambda l, i: (jnp.where(l == layer, i, jnp.where(l < layer, 0, nbk - 1)), 0))

    return _pcall(
        body, name=name, grid=(nl, nbk),
        out_shape=[jax.ShapeDtypeStruct(w.shape, F32)] * 4,
        in_specs=[stacked] * 3 + [g_spec(l) for l in range(nl)] + [ANY] * len(after),
        out_specs=[stacked] * 4,
        compiler_params=_params('arbitrary', 'arbitrary'),
    )(w, m, v, *grads, *after)


def _pack(arrs, rows_mult=512):
    flat = jnp.concatenate([a.reshape(-1) for a in arrs])
    total = flat.shape[0]
    rows = -(-total // LANES)
    rows = -(-rows // rows_mult) * rows_mult
    flat = jnp.pad(flat, (0, rows * LANES - total))
    return flat.reshape(rows, LANES)


def _unpack(packed, shapes):
    flat = packed.reshape(-1)
    out, off = [], 0
    for s in shapes:
        size = math.prod(s)
        out.append(flat[off:off + size].reshape(s))
        off += size
    return out


def _permute(a):
    seq, w = a.shape
    return a.reshape(S5_SEGMENTS, seq // S5_SEGMENTS, w).transpose(1, 0, 2).reshape(seq, w)


def _unpermute(a):
    seq, w = a.shape
    return a.reshape(seq // S5_SEGMENTS, S5_SEGMENTS, w).transpose(1, 0, 2).reshape(seq, w)


def kernel(x, gla_norm, gla_w_in, gla_w_gate_up, gla_b_gate, gla_o_norm, gla_w_out, s5_norm, s5_w_in, s5_lam_re, s5_lam_im, s5_log_dt, s5_b_re, s5_b_im, s5_c_re, s5_c_im, s5_d, s5_w_out, mlp_norm, mlp_w_up, mlp_w_down, final_norm, loss_target, m_gla_norm, m_gla_w_in, m_gla_w_gate_up, m_gla_b_gate, m_gla_o_norm, m_gla_w_out, m_s5_norm, m_s5_w_in, m_s5_lam_re, m_s5_lam_im, m_s5_log_dt, m_s5_b_re, m_s5_b_im, m_s5_c_re, m_s5_c_im, m_s5_d, m_s5_w_out, m_mlp_norm, m_mlp_w_up, m_mlp_w_down, m_final_norm, v_gla_norm, v_gla_w_in, v_gla_w_gate_up, v_gla_b_gate, v_gla_o_norm, v_gla_w_out, v_s5_norm, v_s5_w_in, v_s5_lam_re, v_s5_lam_im, v_s5_log_dt, v_s5_b_re, v_s5_b_im, v_s5_c_re, v_s5_c_im, v_s5_d, v_s5_w_out, v_mlp_norm, v_mlp_w_up, v_mlp_w_down, v_final_norm):
    weights = dict(gla_norm=gla_norm, gla_w_in=gla_w_in, gla_w_gate_up=gla_w_gate_up, gla_b_gate=gla_b_gate, gla_o_norm=gla_o_norm, gla_w_out=gla_w_out, s5_norm=s5_norm, s5_w_in=s5_w_in, s5_lam_re=s5_lam_re, s5_lam_im=s5_lam_im, s5_log_dt=s5_log_dt, s5_b_re=s5_b_re, s5_b_im=s5_b_im, s5_c_re=s5_c_re, s5_c_im=s5_c_im, s5_d=s5_d, s5_w_out=s5_w_out, mlp_norm=mlp_norm, mlp_w_up=mlp_w_up, mlp_w_down=mlp_w_down, final_norm=final_norm)
    mom1 = dict(gla_norm=m_gla_norm, gla_w_in=m_gla_w_in, gla_w_gate_up=m_gla_w_gate_up, gla_b_gate=m_gla_b_gate, gla_o_norm=m_gla_o_norm, gla_w_out=m_gla_w_out, s5_norm=m_s5_norm, s5_w_in=m_s5_w_in, s5_lam_re=m_s5_lam_re, s5_lam_im=m_s5_lam_im, s5_log_dt=m_s5_log_dt, s5_b_re=m_s5_b_re, s5_b_im=m_s5_b_im, s5_c_re=m_s5_c_re, s5_c_im=m_s5_c_im, s5_d=m_s5_d, s5_w_out=m_s5_w_out, mlp_norm=m_mlp_norm, mlp_w_up=m_mlp_w_up, mlp_w_down=m_mlp_w_down, final_norm=m_final_norm)
    mom2 = dict(gla_norm=v_gla_norm, gla_w_in=v_gla_w_in, gla_w_gate_up=v_gla_w_gate_up, gla_b_gate=v_gla_b_gate, gla_o_norm=v_gla_o_norm, gla_w_out=v_gla_w_out, s5_norm=v_s5_norm, s5_w_in=v_s5_w_in, s5_lam_re=v_s5_lam_re, s5_lam_im=v_s5_lam_im, s5_log_dt=v_s5_log_dt, s5_b_re=v_s5_b_re, s5_b_im=v_s5_b_im, s5_c_re=v_s5_c_re, s5_c_im=v_s5_c_im, s5_d=v_s5_d, s5_w_out=v_s5_w_out, mlp_norm=v_mlp_norm, mlp_w_up=v_mlp_w_up, mlp_w_down=v_mlp_w_down, final_norm=v_final_norm)
    names = list(weights)
    big = ['gla_w_in', 'gla_w_out', 's5_w_in', 's5_w_out', 'mlp_w_up', 'mlp_w_down']
    small = [n for n in names if n not in big]

    chip = 2 * lax.axis_index('x') + lax.axis_index('y')
    h0 = x[0]
    target = loss_target[0]
    seq, dm = h0.shape
    depth = mlp_norm.shape[0]
    n_gla = gla_norm.shape[0]
    n_s5 = s5_lam_re.shape[0]
    rank = gla_w_gate_up.shape[1]
    kw = gla_b_gate.shape[1]
    dv = gla_o_norm.shape[1]
    in_w = 4 * gla_w_in.shape[2]
    vw = (in_w - rank - 2 * kw) // 2
    heads = vw // dv
    dk = kw // heads
    pw = -(-in_w // LANES) * LANES
    s5w = s5_w_in.shape[2]
    n_grp, n_state, grp = s5_b_re.shape[1:]
    hid = 4 * mlp_w_up.shape[2]
    tb = min(seq, 8 * CHUNK)
    tm = _tile(seq, 1024)

    ic = lax.axis_index('c')
    rh = lambda w: w.shape[1] // 2
    wb16 = {n: weights[n].astype(BF16) for n in big}
    gathered = {n: [None] * weights[n].shape[0] for n in big}
    g_w_in, g_gla_out, g_s5_in, g_s5_out, g_up, g_down = (gathered[n] for n in big)
    in_flight = {}

    to_start = []
    for i in range(depth):
        mix = ['gla_w_in', 'gla_w_out'] if i % 2 == 0 else ['s5_w_in', 's5_w_out']
        to_start += [(m, i // 2) for m in mix] + [('mlp_w_up', i), ('mlp_w_down', i)]

    def start_gathers(after):
        while to_start and len(in_flight) < GATHERS_IN_FLIGHT:
            n, l = to_start.pop(0)
            rows, cols = weights[n].shape[1:]
            hd = _start('chips_gather', wb16[n][l].reshape(2, rows // 2, cols), (2, 4, rows // 2, cols),
                        f'ag_{n}_{l}_start', after)
            in_flight[n, l] = hd
            after = [hd['token']]
        return after

    def finish_gather(n, l, after):
        halves, land = _wait(in_flight.pop((n, l)), f'ag_{n}_{l}_wait', after)
        behind = start_gathers([land])
        own = lax.dynamic_index_in_dim(halves, ic, 0, keepdims=True)
        land = lax.dynamic_update_slice(land, own[:, None], (ic, chip, 0, 0))
        hd = _start('pair_inplace', None, land, f'ag_{n}_{l}_pair_start', [])
        gathered[n][l] = _wait(hd, f'ag_{n}_{l}_pair_wait', [])[0]
        return behind

    sharded_small = [gla_w_gate_up, s5_norm, s5_d]
    gathered_small = _exchange(_pack(sharded_small), 'xy', 'bcast', 'ag_small')
    start_gathers([gathered_small])
    parts = [_unpack(gathered_small[k], [a.shape for a in sharded_small]) for k in range(4)]
    wgu_full = jnp.concatenate([p[0] for p in parts], axis=2)
    s5_norm_full = jnp.concatenate([p[1] for p in parts], axis=1)
    s5_d_full = jnp.concatenate([p[2] for p in parts], axis=1)

    def gla_w_in_padded(j):
        wj = g_w_in[j].transpose(0, 2, 1, 3).reshape(dm, in_w)
        return jnp.pad(wj, ((0, 0), (0, pw - in_w)))

    grads = {n: [None] * weights[n].shape[0] for n in names if n != 'final_norm'}

    saved = []
    h = h0
    for i in range(depth):
        j = i // 2
        rec = {}
        if i % 2 == 0:
            rec['h_in'] = h
            behind = finish_gather('gla_w_in', j, [h])
            hn = _norm_fwd(h, gla_norm[j:j + 1], 'gla_norm_fwd', behind)
            w_in_pad = gla_w_in_padded(j)
            proj = _mm('gla_proj', 'nn', _plain(hn), _plain(w_in_pad), (seq, pw, dm),
                       [((seq, pw), F32, _plain_shape(None))], (tm, _tile(pw, 1024), dm))[0]
            wgu_pad = jnp.pad(wgu_full[j], ((0, LANES - rank), (0, 0)))
            gated, states = _gla_scan_fwd(proj, wgu_pad, gla_b_gate[j:j + 1], gla_o_norm[j:j + 1],
                                          heads, kw, vw, tb, 'gla_scan_fwd')
            behind = finish_gather('gla_w_out', j, [gated])
            h = _mm('gla_out', 'nn', _plain(gated), _w_rows(g_gla_out, j), (seq, dm, vw),
                    [((seq, dm), F32, _plain_shape(None))],
                    (tm, _tile(dm, 1024), gla_w_out.shape[1]),
                    epilogue=lambda acc, hv: (acc + hv,), extras=[_plain(h)], after=behind)[0]
            rec.update(hn=hn, w_in_pad=w_in_pad, proj=proj, wgu_pad=wgu_pad, gated=gated, states=states)
        else:
            hp = _permute(h)
            rec['h_in'] = hp
            hn = _norm_fwd(hp, s5_norm_full[j:j + 1], 's5_norm_fwd')
            behind = finish_gather('s5_w_in', j, [hn])
            u = _mm('s5_in', 'nn', _plain(hn), _w_rows(g_s5_in, j), (seq, s5w, dm),
                    [((seq, s5w), F32, _plain_shape(None))],
                    (tm, _tile(s5w, 1024), s5_w_in.shape[1]), after=behind)[0]
            disc, disc_vjp = jax.vjp(_s5_discretise, s5_lam_re[j], s5_lam_im[j], s5_log_dt[j], s5_b_re[j], s5_b_im[j])
            ab_re, ab_im, bb_re, bb_im = disc
            bre = _to_blocks(bb_re.transpose(0, 2, 1))
            bim = _to_blocks(bb_im.transpose(0, 2, 1))
            cre = _to_blocks(s5_c_re[j].transpose(0, 2, 1))
            cim = _to_blocks(s5_c_im[j].transpose(0, 2, 1))
            nb = n_grp // S5_GROUPS_PER_BLOCK
            are = ab_re.reshape(nb, 1, S5_GROUPS_PER_BLOCK * n_state)
            aim = ab_im.reshape(nb, 1, S5_GROUPS_PER_BLOCK * n_state)
            dskip = s5_d_full[j:j + 1]
            y, z, xr, xi = _s5_fwd(u, bre, bim, cre, cim, are, aim, dskip, 's5_scan_fwd')
            behind = finish_gather('s5_w_out', j, [z])
            o = _mm('s5_out', 'nn', _plain(z), _w_cols(g_s5_out, j), (seq, 2 * dm, s5w),
                    [((seq, 2 * dm), F32, _plain_shape(None))],
                    (tm, _tile(s5_w_out.shape[2], 1024), s5_w_out.shape[1]), after=behind)[0]
            h = _unpermute(_glu_fwd(o, hp, 's5_glu_fwd'))
            rec.update(hn=hn, u=u, y=y, z=z, xr=xr, xi=xi, o=o, mats=(bre, bim, cre, cim, are, aim, dskip),
                       disc_vjp=disc_vjp)
        rec['h_mid'] = h
        hn2 = _norm_fwd(h, mlp_norm[i:i + 1], 'mlp_norm_fwd')
        behind = finish_gather('mlp_w_up', i, [hn2])
        act, act2 = _mm('mlp_up', 'nn', _plain(hn2), _w_cols(g_up, i), (seq, hid, dm),
                        [((seq, hid), BF16, _plain_shape(None))] * 2,
                        (tm, _tile(mlp_w_up.shape[2], 1024), mlp_w_up.shape[1]),
                        epilogue=lambda acc: (jnp.maximum(acc, 0.0), jnp.square(jnp.maximum(acc, 0.0))),
                        after=behind)
        behind = finish_gather('mlp_w_down', i, [act2])
        h = _mm('mlp_down', 'nn', _plain(act2), _w_rows(g_down, i), (seq, dm, hid),
                [((seq, dm), F32, _plain_shape(None))],
                (tm, _tile(dm, 1024), mlp_w_down.shape[1]),
                epilogue=lambda acc, hv: (acc + hv,), extras=[_plain(h)], after=behind)[0]
        rec.update(hn2=hn2, act=act, act2=act2)
        saved.append(rec)

    dh, loss_cols, d_final = _loss_head(h, target, final_norm.reshape(1, dm), 'loss_head')
    loss = lax.psum(jnp.sum(loss_cols), ('x', 'y', 'c'))
    grads['final_norm'] = [d_final.reshape(dm)]

    big_grads = {n: [None] * weights[n].shape[0] for n in big}
    reducing = []

    def reduce_step(item, after):
        n, l, hd = item['n'], item['l'], item['hd']
        if item['stage'] == 'pair':
            dw, got = _wait(hd, f'rs_{n}_{l}_pair_wait', after)
            _, _, rows_h, cols = dw.shape
            pre = _sum_pair(dw.reshape(2, 4 * rows_h, cols), got.reshape(4 * rows_h, cols), ic, BF16,
                            f'rs_{n}_pairsum').reshape(4, rows_h, cols)
            item.update(stage='chips', hd=_start('chips_a2a', pre, pre.shape, f'rs_{n}_{l}_start', []))
        elif item['stage'] == 'chips':
            pre, yb = _wait(hd, f'rs_{n}_{l}_wait', after)
            yb = lax.dynamic_update_index_in_dim(yb, lax.dynamic_index_in_dim(pre, chip, 0, keepdims=False), chip, 0)
            fin = _sum_slots(yb, F32, f'rs_{n}_chipsum')
            item.update(stage='back', hd=_start('pair_bcast', fin, (2,) + fin.shape, f'rs_{n}_{l}_back_start', []))
        else:
            fin, both = _wait(hd, f'rs_{n}_{l}_back_wait', after)
            both = lax.dynamic_update_index_in_dim(both, fin, ic, 0)
            big_grads[n][l] = both.reshape(2 * fin.shape[0], fin.shape[1])
            item.update(stage='done', hd=None)
            return after
        return [item['hd']['token']]

    def reduce_scatter(dw, n, l):
        hd = _start('pair_swap', dw, dw.shape[1:], f'rs_{n}_{l}_pair_start', [])
        reducing.append(dict(n=n, l=l, stage='pair', hd=hd))
        behind = [hd['token']]
        if len(reducing) >= 2:
            behind = reduce_step(reducing[-2], behind)
        if len(reducing) >= 4:
            behind = reduce_step(reducing[-4], behind)
        return behind
    for i in reversed(range(depth)):
        j = i // 2
        rec = saved[i]
        r_dn, c_dn = mlp_w_down.shape[1:]
        shape, spec = _dw_rows(r_dn, c_dn)
        dw = _mm('mlp_down_dw', 'tn', _plain(rec['act2']), _plain(dh), (hid, dm, seq),
                 [(shape, BF16, spec)], (_tile(r_dn // 2, 1024), _tile(c_dn, 1024), seq))[0]
        behind = reduce_scatter(dw, 'mlp_w_down', i)
        dpre = _mm('mlp_down_dx', 'nt', _plain(dh), _w_rows(g_down, i), (seq, hid, dm),
                   [((seq, hid), BF16, _plain_shape(None))],
                   (tm, _tile(rh(mlp_w_down), 1024), dm),
                   epilogue=lambda acc, av: (acc * (2.0 * av.astype(F32)),), extras=[_plain(rec['act'])],
                   after=behind)[0]
        r_up, c_up = mlp_w_up.shape[1:]
        shape, spec = _dw_cols(r_up, c_up)
        dw = _mm('mlp_up_dw', 'tn', _plain(rec['hn2']), _plain(dpre), (dm, hid, seq),
                 [(shape, BF16, spec)], (_tile(r_up // 2, 1024), _tile(c_up, 1024), seq))[0]
        behind = reduce_scatter(dw, 'mlp_w_up', i)
        dhn = _mm('mlp_up_dx', 'nt', _plain(dpre), _w_cols(g_up, i), (seq, dm, hid),
                  [((seq, dm), F32, _plain_shape(None))],
                  (tm, _tile(rh(mlp_w_up), 1024), _tile(mlp_w_up.shape[2], 2048)), after=behind)[0]
        dh, dg = _norm_bwd(rec['h_mid'], dhn, dh, mlp_norm[i:i + 1], 'mlp_norm_bwd')
        grads['mlp_norm'][i] = dg[0]

        if i % 2 == 0:
            r_o, c_o = gla_w_out.shape[1:]
            shape, spec = _dw_rows(r_o, c_o)
            dw = _mm('gla_out_dw', 'tn', _plain(rec['gated']), _plain(dh), (vw, dm, seq),
                     [(shape, BF16, spec)], (_tile(r_o // 2, 1024), _tile(c_o, 1024), seq))[0]
            behind = reduce_scatter(dw, 'gla_w_out', j)
            dgated = _mm('gla_out_dx', 'nt', _plain(dh), _w_rows(g_gla_out, j), (seq, vw, dm),
                         [((seq, vw), F32, _plain_shape(None))],
                         (tm, _tile(rh(gla_w_out), 1024), dm), after=behind)[0]
            dq, dkk, dvv, dr, dpre_g, db, don = _gla_scan_bwd(
                rec['proj'], rec['wgu_pad'], gla_b_gate[j:j + 1], gla_o_norm[j:j + 1], rec['states'], dgated,
                heads, kw, vw, tb, 'gla_scan_bwd')
            grads['gla_b_gate'][j] = db[0]
            grads['gla_o_norm'][j] = don[0]
            dgl = _mm('gla_gate_dx', 'nt', _plain(dpre_g), _plain(rec['wgu_pad']), (seq, LANES, kw),
                      [((seq, LANES), BF16, _plain_shape(None))], (tm, LANES, kw))[0]
            g_low = rec['proj'][:, pw - LANES:]
            dwgu = _mm('gla_gate_dw', 'tn', _plain(g_low), _plain(dpre_g), (LANES, kw, seq),
                       [((LANES, kw), F32, _plain_shape(None))], (LANES, kw, seq))[0]
            grads['gla_w_gate_up'][j] = dwgu[:rank]
            dproj = jnp.concatenate([dq, dkk, dvv, dr, dgl], axis=1)
            dw_pad = _mm('gla_proj_dw', 'tn', _plain(rec['hn']), _plain(dproj), (dm, pw, seq),
                         [((dm, pw), BF16, _plain_shape(None))], (_tile(dm, 1024), _tile(pw, 1024), seq))[0]
            shard_w = in_w // 4
            dw = dw_pad[:, :in_w].reshape(2, dm // 2, 4, shard_w).transpose(0, 2, 1, 3)
            behind = reduce_scatter(dw, 'gla_w_in', j)
            dhn = _mm('gla_proj_dx', 'nt', _plain(dproj), _plain(rec['w_in_pad']), (seq, dm, pw),
                      [((seq, dm), F32, _plain_shape(None))], (tm, _tile(dm, 1024), _tile(pw, 1024)),
                      after=behind)[0]
            dh, dg = _norm_bwd(rec['h_in'], dhn, dh, gla_norm[j:j + 1], 'gla_norm_bwd')
            grads['gla_norm'][j] = dg[0]
        else:
            dhp = _permute(dh)
            do = _glu_bwd(rec['o'], dhp, 's5_glu_bwd')
            r_o, c_o = s5_w_out.shape[1:]
            shape, spec = _dw_cols(r_o, c_o)
            dw = _mm('s5_out_dw', 'tn', _plain(rec['z']), _plain(do), (s5w, 2 * dm, seq),
                     [(shape, BF16, spec)], (_tile(r_o // 2, 1024), _tile(c_o, 1024), seq))[0]
            behind = reduce_scatter(dw, 's5_w_out', j)
            dz = _mm('s5_out_dx', 'nt', _plain(do), _w_cols(g_s5_out, j), (seq, s5w, 2 * dm),
                     [((seq, s5w), F32, _plain_shape(None))],
                     (tm, _tile(rh(s5_w_out), 1024), _tile(s5_w_out.shape[2], 1024)), after=behind)[0]
            bre, bim, cre, cim, are, aim, dskip = rec['mats']
            du, dcr, dci, dbr, dbi, dar, dai, dd = _s5_bwd(dz, rec['y'], rec['u'], rec['xr'], rec['xi'],
                                                           bre, bim, cre, cim, are, aim, dskip, 's5_scan_bwd')
            grads['s5_c_re'][j] = _from_blocks(dcr, grp, n_state)
            grads['s5_c_im'][j] = _from_blocks(dci, grp, n_state)
            dbb_re = _from_blocks(dbr, grp, n_state).transpose(0, 2, 1)
            dbb_im = _from_blocks(dbi, grp, n_state).transpose(0, 2, 1)
            d_lr, d_li, d_dt, d_bre, d_bim = rec['disc_vjp'](
                (dar.reshape(n_grp, n_state), dai.reshape(n_grp, n_state), dbb_re, dbb_im))
            grads['s5_lam_re'][j] = d_lr
            grads['s5_lam_im'][j] = d_li
            grads['s5_log_dt'][j] = d_dt
            grads['s5_b_re'][j] = d_bre
            grads['s5_b_im'][j] = d_bim
            grads['s5_d'][j] = dd[0]
            r_i, c_i = s5_w_in.shape[1:]
            shape, spec = _dw_rows(r_i, c_i)
            dw = _mm('s5_in_dw', 'tn', _plain(rec['hn']), _plain(du), (dm, s5w, seq),
                     [(shape, BF16, spec)], (_tile(r_i // 2, 1024), _tile(c_i, 1024), seq))[0]
            behind = reduce_scatter(dw, 's5_w_in', j)
            dhn = _mm('s5_in_dx', 'nt', _plain(du), _w_rows(g_s5_in, j), (seq, dm, s5w),
                      [((seq, dm), F32, _plain_shape(None))],
                      (tm, _tile(rh(s5_w_in), 1024), _tile(s5w, 1024)), after=behind)[0]
            dhp, dg = _norm_bwd(rec['h_in'], dhn, dhp, s5_norm_full[j:j + 1], 's5_norm_bwd')
            dh = _unpermute(dhp)
            grads['s5_norm'][j] = dg[0]
    grad_x = dh[None]
    behind = [dh]
    for stage in ('pair', 'chips', 'back'):
        for item in reducing:
            if item['stage'] == stage:
                behind = reduce_step(item, behind)

    local_small = [jnp.stack(grads[n]) if n != 'final_norm' else grads[n][0] for n in small]
    full_shapes = [a.shape for a in local_small]
    packed_small = _pack(local_small)
    ar_small = _xy_start(packed_small, 'bcast', 'ar_small_start', [])
    out_g, out_d, out_m, out_v = {}, {}, {}, {}
    behind = [ar_small['token']]
    for n in big:
        out_g[n], out_d[n], out_m[n], out_v[n] = _adamw(weights[n], mom1[n], mom2[n], big_grads[n], 'adamw_' + n,
                                                        behind)
        behind = [out_d[n]]
    sent, by_chip = _xy_wait(ar_small, 'ar_small_wait', behind)
    by_chip = lax.dynamic_update_index_in_dim(by_chip, sent, chip, 0)
    gathered = _exchange(by_chip, 'c', 'bcast', 'ar_small_c')
    rows = gathered.shape[2]
    summed = _sum_slots(gathered.reshape(8, rows, LANES), F32, 'ar_small_sum')
    small_full = dict(zip(small, _unpack(summed, full_shapes)))
    small_grad = {}
    for n in small:
        g = small_full[n]
        if g.shape != weights[n].shape:
            ax = [a for a in range(g.ndim) if g.shape[a] != weights[n].shape[a]][0]
            g = lax.dynamic_slice_in_dim(g, chip * weights[n].shape[ax], weights[n].shape[ax], axis=ax)
        small_grad[n] = g

    shapes = [weights[n].shape for n in small]
    pw_, pm_, pv_, pg_ = (_pack([d[n] for n in small]) for d in (weights, mom1, mom2, small_grad))
    _, sd, sm, sv = _adamw(pw_[None], pm_[None], pv_[None], [pg_], 'adamw_small')
    for n, d_, m_, v_ in zip(small, _unpack(sd[0], shapes), _unpack(sm[0], shapes), _unpack(sv[0], shapes)):
        out_g[n], out_d[n], out_m[n], out_v[n] = small_grad[n], d_, m_, v_

    return (loss, grad_x, *[out_g[n] for n in names], *[out_d[n] for n in names],
            *[out_m[n] for n in names], *[out_v[n] for n in names])
```

```python
import functools
import math

import jax
import jax.numpy as jnp
from jax import lax
from jax.experimental import pallas as pl
from jax.experimental.pallas import tpu as pltpu

F32 = jnp.float32
BF16 = jnp.bfloat16

EPS = 1e-6
CHUNK = 64
GLA_GATE_TEMP = 16.0
S5_EIG_CLIP = -1e-4
S5_SEGMENTS = 8
S5_GROUPS_PER_BLOCK = 8
LANES = 128
ADAM_LR = 0.001
ADAM_B1 = 0.9
ADAM_B2 = 0.999
ADAM_EPS = 1e-08
ADAM_WD = 0.01
ADAM_STEP = 10
VMEM_LIMIT_BYTES = 56 * 1024 * 1024
PAIR_PIECE_BYTES = 2 * 1024 * 1024
GATHERS_IN_FLIGHT = 2
PAIR_VMEM_BYTES = 40 * 1024 * 1024

MESH = pl.DeviceIdType.MESH
ANY = pl.BlockSpec(memory_space=pl.ANY)
IN_VMEM = pl.BlockSpec(memory_space=pltpu.VMEM)
IN_HBM = pl.BlockSpec(memory_space=pltpu.HBM)
IN_SEM = pl.BlockSpec(memory_space=pltpu.SEMAPHORE)
DATAFLOW = pltpu.SideEffectType.DATAFLOW_SIDE_EFFECTING


def _pcall(body, **kw):
    return pl.pallas_call(body, **kw)


def _params(*sem):
    return pltpu.CompilerParams(dimension_semantics=sem, vmem_limit_bytes=VMEM_LIMIT_BYTES)


def _tile(dim, target, unit=LANES):
    if dim <= target:
        return dim
    best = None
    for t in range(unit, target + 1, unit):
        if dim % t == 0:
            best = t
    assert best is not None, (dim, target)
    return best


def _exchange(x, group, mode, name):
    n = 2 if group == 'c' else 4
    blk = x.shape if mode == 'bcast' else x.shape[1:]
    if mode == 'a2a':
        assert x.shape[0] == n
    flips = [(0, 0, 1)] if group == 'c' else [(1, 0, 0), (0, 1, 0), (1, 1, 0)]
    itemsize = jnp.dtype(x.dtype).itemsize
    staged = group == 'c' and (x.size + n * math.prod(blk)) * itemsize <= PAIR_VMEM_BYTES
    if group == 'c' and not staged:
        ic = lax.axis_index('c')
        own = x if mode == 'bcast' else lax.dynamic_index_in_dim(x, ic, 0, keepdims=False)
        return lax.dynamic_update_index_in_dim(_pair_exchange_chunked(x, mode, name), own, ic, 0)

    def body(x_ref, y_ref, send_sems, recv_sems, local_sem):
        ix, iy, ic = lax.axis_index('x'), lax.axis_index('y'), lax.axis_index('c')

        def slot(px, py, pc):
            return pc if group == 'c' else 2 * px + py

        def src(px, py, pc):
            if mode == 'a2a':
                return x_ref.at[slot(px, py, pc)]
            if mode == 'bcast_c':
                return x_ref.at[ic]
            return x_ref

        me = (ix, iy, ic)
        local = pltpu.make_async_copy(src(*me), y_ref.at[slot(*me)], local_sem)
        local.start()
        peers = []
        for fx, fy, fc in flips:
            peers.append((1 - ix if fx else ix, 1 - iy if fy else iy, 1 - ic if fc else ic))
        sends = []
        for k, peer in enumerate(peers):
            cp = pltpu.make_async_remote_copy(
                src_ref=src(*peer), dst_ref=y_ref.at[slot(*me)],
                send_sem=send_sems.at[k], recv_sem=recv_sems.at[k],
                device_id=peer, device_id_type=MESH)
            cp.start()
            sends.append(cp)
        for k, peer in enumerate(peers):
            pltpu.make_async_remote_copy(
                src_ref=src(*peer), dst_ref=y_ref.at[slot(*peer)],
                send_sem=send_sems.at[k], recv_sem=recv_sems.at[k],
                device_id=peer, device_id_type=MESH).wait_recv()
        for cp in sends:
            cp.wait_send()
        local.wait()

    return _pcall(
        body, name=name,
        out_shape=jax.ShapeDtypeStruct((n,) + tuple(blk), x.dtype),
        in_specs=[IN_VMEM if staged else ANY], out_specs=IN_VMEM if staged else ANY,
        scratch_shapes=[pltpu.SemaphoreType.DMA((len(flips),)),
                        pltpu.SemaphoreType.DMA((len(flips),)),
                        pltpu.SemaphoreType.DMA(())],
        compiler_params=pltpu.CompilerParams(vmem_limit_bytes=VMEM_LIMIT_BYTES),
    )(x)


def _split_axis(blk, dtype, piece_bytes):
    itemsize = jnp.dtype(dtype).itemsize
    sublanes = 8 * 4 // itemsize
    want = max(1, math.prod(blk) * itemsize // piece_bytes)
    for pieces in [s for s in (64, 32, 16, 8, 4, 2) if s <= want]:
        for ax in range(len(blk) - 1):
            unit = sublanes if ax == len(blk) - 2 else 1
            if blk[ax] % (pieces * unit) == 0:
                return ax, pieces
    return 0, 1


def _pair_exchange_chunked(x, mode, name, reduce=False):
    assert mode == 'a2a' or not reduce
    blk = x.shape if mode == 'bcast' else x.shape[1:]
    ax, pieces = _split_axis(blk, x.dtype, PAIR_PIECE_BYTES)
    step = blk[ax] // pieces
    piece_shape = tuple(blk[:ax]) + (step,) + tuple(blk[ax + 1:])

    def piece(ref, p):
        return ref.at[(slice(None),) * ax + (pl.ds(p * step, step),)]

    def body(x_ref, y_ref, out_buf, in_buf, own_buf, send_sems, recv_sems, stage_sems, drain_sems, own_sems, credit_sem):
        ix, iy, ic = lax.axis_index('x'), lax.axis_index('y'), lax.axis_index('c')
        sibling = (ix, iy, 1 - ic)
        mine = x_ref.at[ic] if mode != 'bcast' else x_ref
        theirs = x_ref.at[1 - ic] if mode == 'a2a' else mine

        def own(p):
            return pltpu.make_async_copy(piece(mine, p), own_buf.at[p % 2], own_sems.at[p % 2])

        def stage(p):
            return pltpu.make_async_copy(piece(theirs, p), out_buf.at[p % 2], stage_sems.at[p % 2])

        def remote(p):
            return pltpu.make_async_remote_copy(
                src_ref=out_buf.at[p % 2], dst_ref=in_buf.at[p % 2],
                send_sem=send_sems.at[p], recv_sem=recv_sems.at[p],
                device_id=sibling, device_id_type=MESH)

        def drain(p):
            dst = y_ref if reduce else y_ref.at[1 - ic]
            return pltpu.make_async_copy(in_buf.at[p % 2], piece(dst, p), drain_sems.at[p % 2])

        stage(0).start()
        if reduce:
            own(0).start()
        for p in range(pieces):
            stage(p).wait()
            if p >= 2:
                pl.semaphore_wait(credit_sem, 1)
            remote(p).start()
            if p + 1 < pieces:
                if p >= 1:
                    remote(p - 1).wait_send()
                stage(p + 1).start()
                if reduce:
                    own(p + 1).start()
            remote(p).wait_recv()
            if reduce:
                own(p).wait()
                in_buf[p % 2] = (in_buf[p % 2].astype(F32) + own_buf[p % 2].astype(F32)).astype(in_buf.dtype)
            drain(p).start()
            drain(p).wait()
            if p + 2 < pieces:
                pl.semaphore_signal(credit_sem, inc=1, device_id=sibling, device_id_type=MESH)
        for p in range(max(0, pieces - 2), pieces):
            remote(p).wait_send()

    return _pcall(
        body, name=name,
        out_shape=jax.ShapeDtypeStruct(tuple(blk) if reduce else (2,) + tuple(blk), x.dtype),
        in_specs=[ANY], out_specs=ANY,
        scratch_shapes=[pltpu.VMEM((2,) + piece_shape, x.dtype), pltpu.VMEM((2,) + piece_shape, x.dtype),
                        pltpu.VMEM((2,) + piece_shape if reduce else (2, 8, LANES), x.dtype),
                        pltpu.SemaphoreType.DMA((pieces,)), pltpu.SemaphoreType.DMA((pieces,)),
                        pltpu.SemaphoreType.DMA((2,)), pltpu.SemaphoreType.DMA((2,)), pltpu.SemaphoreType.DMA((2,)),
                        pltpu.SemaphoreType.REGULAR],
        compiler_params=pltpu.CompilerParams(vmem_limit_bytes=VMEM_LIMIT_BYTES),
    )(x)


_FLIPS = {'xy': [(1, 0, 0), (0, 1, 0), (1, 1, 0)], 'c': [(0, 0, 1)]}


def _split_copies(group, mode, x_ref, land_ref, sems):
    ix, iy, ic = lax.axis_index('x'), lax.axis_index('y'), lax.axis_index('c')
    slot = (lambda px, py, pc: pc) if group == 'c' else (lambda px, py, pc: 2 * px + py)
    n_peers = len(_FLIPS[group])
    out = []
    for k, (fx, fy, fc) in enumerate(_FLIPS[group]):
        peer = (1 - ix if fx else ix, 1 - iy if fy else iy, 1 - ic if fc else ic)
        if mode == 'a2a':
            src = x_ref.at[slot(*peer)]
        elif mode == 'bcast_c':
            src = x_ref.at[ic]
        else:
            src = x_ref
        mk = lambda dst, src=src, k=k, peer=peer: pltpu.make_async_remote_copy(
            src_ref=src, dst_ref=dst, send_sem=sems[k], recv_sem=sems[n_peers + k],
            device_id=peer, device_id_type=MESH)
        out.append((mk(land_ref.at[slot(ix, iy, ic)]), mk(land_ref.at[slot(*peer)])))
    return out


def _xy_start(x, mode, name, after, group='xy'):
    blk = x.shape if mode == 'bcast' else x.shape[1:]
    land_shape = (2 if group == 'c' else 4,) + tuple(blk)
    n_after = len(after)
    n_sems = 2 * len(_FLIPS[group])

    def body(*refs):
        x_ref, land_ref = refs[0], refs[1]
        sems = refs[2 + n_after:2 + n_sems + n_after]
        for send, _ in _split_copies(group, mode, x_ref, land_ref, sems):
            send.start()
        refs[-1][...] = jnp.zeros_like(refs[-1])

    outs = _pcall(
        body, name=name,
        out_shape=(pltpu.SemaphoreType.DMA(()),) * n_sems
        + (pltpu.HBM(x.shape, x.dtype), pltpu.HBM(land_shape, x.dtype), jax.ShapeDtypeStruct((8, LANES), F32)),
        in_specs=(IN_HBM, IN_HBM) + (ANY,) * n_after,
        out_specs=(IN_SEM,) * n_sems + (IN_HBM, IN_HBM, IN_VMEM),
        input_output_aliases={0: n_sems, 1: n_sems + 1},
        compiler_params=pltpu.CompilerParams(has_side_effects=DATAFLOW),
    )(pltpu.with_memory_space_constraint(x, pltpu.HBM),
      pltpu.with_memory_space_constraint(lax.empty(land_shape, x.dtype), pltpu.HBM), *after)
    return dict(sems=outs[:n_sems], sent=outs[n_sems], land=outs[n_sems + 1], token=outs[n_sems + 2], mode=mode,
                group=group)


def _xy_wait(handle, name, after):
    mode, group = handle['mode'], handle['group']
    n_after = len(after)
    n_sems = len(handle['sems'])

    def body(*refs):
        x_ref, land_ref = refs[0], refs[1]
        for _, recv in _split_copies(group, mode, x_ref, land_ref, refs[2:2 + n_sems]):
            recv.wait_send()
            recv.wait_recv()

    sent, land = handle['sent'], handle['land']
    return _pcall(
        body, name=name,
        out_shape=(pltpu.HBM(sent.shape, sent.dtype), pltpu.HBM(land.shape, land.dtype)),
        in_specs=(IN_HBM, IN_HBM) + (IN_SEM,) * n_sems + (ANY,) * n_after,
        out_specs=(IN_HBM, IN_HBM),
        input_output_aliases={0: 0, 1: 1},
        compiler_params=pltpu.CompilerParams(has_side_effects=DATAFLOW),
    )(sent, land, *handle['sems'], *after)


_KIND_GROUP = {'chips_a2a': 'xy', 'chips_gather': 'xy', 'pair_swap': 'c', 'pair_bcast': 'c', 'pair_inplace': 'c'}


def _plan(kind, x_ref, land_ref, sems):
    ix, iy, ic = lax.axis_index('x'), lax.axis_index('y'), lax.axis_index('c')
    flips = _FLIPS[_KIND_GROUP[kind]]
    me_chip = 2 * ix + iy
    out = []
    for k, (fx, fy, fc) in enumerate(flips):
        peer = (1 - ix if fx else ix, 1 - iy if fy else iy, 1 - ic if fc else ic)
        peer_chip = 2 * peer[0] + peer[1]
        if kind == 'chips_a2a':
            src, dst, got = x_ref.at[peer_chip], land_ref.at[me_chip], land_ref.at[peer_chip]
        elif kind == 'chips_gather':
            src, dst, got = x_ref.at[ic], land_ref.at[ic, me_chip], land_ref.at[ic, peer_chip]
        elif kind == 'pair_swap':
            src, dst, got = x_ref.at[1 - ic], land_ref, land_ref
        elif kind == 'pair_bcast':
            src, dst, got = x_ref, land_ref.at[ic], land_ref.at[1 - ic]
        else:
            src, dst, got = land_ref.at[ic], land_ref.at[ic], land_ref.at[1 - ic]
        mk = lambda d, src=src, k=k, peer=peer: pltpu.make_async_remote_copy(
            src_ref=src, dst_ref=d, send_sem=sems[k], recv_sem=sems[len(flips) + k],
            device_id=peer, device_id_type=MESH)
        out.append((mk(dst), mk(got)))
    return out


def _start(kind, x, land, name, after):
    n_sems = 2 * len(_FLIPS[_KIND_GROUP[kind]])
    if not hasattr(land, 'dtype'):
        land = lax.empty(tuple(land), x.dtype)
    arrays = ([] if x is None else [x]) + [land]
    n_arr, n_after = len(arrays), len(after)

    def body(*refs):
        sems = refs[n_arr + n_after:n_arr + n_after + n_sems]
        for send, _ in _plan(kind, None if x is None else refs[0], refs[n_arr - 1], sems):
            send.start()
        refs[-1][...] = jnp.zeros_like(refs[-1])

    outs = _pcall(
        body, name=name,
        out_shape=(pltpu.SemaphoreType.DMA(()),) * n_sems + tuple(pltpu.HBM(a.shape, a.dtype) for a in arrays)
        + (jax.ShapeDtypeStruct((8, LANES), F32),),
        in_specs=(IN_HBM,) * n_arr + (ANY,) * n_after,
        out_specs=(IN_SEM,) * n_sems + (IN_HBM,) * n_arr + (IN_VMEM,),
        input_output_aliases={i: n_sems + i for i in range(n_arr)},
        compiler_params=pltpu.CompilerParams(has_side_effects=DATAFLOW),
    )(*[pltpu.with_memory_space_constraint(a, pltpu.HBM) for a in arrays], *after)
    return dict(kind=kind, sems=outs[:n_sems], arrays=outs[n_sems:n_sems + n_arr], token=outs[-1])


def _wait(handle, name, after):
    kind, arrays, sems = handle['kind'], handle['arrays'], handle['sems']
    n_arr, n_sems = len(arrays), len(sems)

    def body(*refs):
        for _, got in _plan(kind, refs[0] if n_arr == 2 else None, refs[n_arr - 1], refs[n_arr:n_arr + n_sems]):
            got.wait_send()
            got.wait_recv()

    return _pcall(
        body, name=name,
        out_shape=tuple(pltpu.HBM(a.shape, a.dtype) for a in arrays),
        in_specs=(IN_HBM,) * n_arr + (IN_SEM,) * n_sems + (ANY,) * len(after),
        out_specs=(IN_HBM,) * n_arr,
        input_output_aliases={i: i for i in range(n_arr)},
        compiler_params=pltpu.CompilerParams(has_side_effects=DATAFLOW),
    )(*arrays, *sems, *after)


def _sum_pair(x, recv, ic, out_dtype, name):
    _, rows, cols = x.shape
    tm = _tile(rows, max(8, (1 << 19) // cols // 8 * 8), unit=8)

    def body(c_ref, x_ref, r_ref, o_ref):
        o_ref[...] = (x_ref[...].astype(F32) + r_ref[...].astype(F32)).astype(o_ref.dtype)

    return _pcall(
        body, name=name,
        grid_spec=pltpu.PrefetchScalarGridSpec(
            num_scalar_prefetch=1, grid=(rows // tm,),
            in_specs=[pl.BlockSpec((None, tm, cols), lambda i, c: (c[0], i, 0)),
                      pl.BlockSpec((tm, cols), lambda i, c: (i, 0))],
            out_specs=pl.BlockSpec((tm, cols), lambda i, c: (i, 0))),
        out_shape=jax.ShapeDtypeStruct((rows, cols), out_dtype),
        compiler_params=_params('parallel'),
    )(jnp.reshape(ic, (1,)).astype(jnp.int32), x, recv)


def _sum_slots(y, out_dtype, name):
    n, rows, cols = y.shape
    tm = _tile(rows, max(8, (1 << 20) // (n * cols) // 8 * 8), unit=8)

    def body(y_ref, o_ref):
        acc = y_ref[0].astype(F32)
        for k in range(1, n):
            acc = acc + y_ref[k].astype(F32)
        o_ref[...] = acc.astype(o_ref.dtype)

    return _pcall(
        body, name=name, grid=(rows // tm,),
        out_shape=jax.ShapeDtypeStruct((rows, cols), out_dtype),
        in_specs=[pl.BlockSpec((n, tm, cols), lambda i: (0, i, 0))],
        out_specs=pl.BlockSpec((tm, cols), lambda i: (i, 0)),
        compiler_params=_params('parallel'),
    )(y)


class _Op:
    def __init__(self, arr, spec):
        self.arr = arr
        self.spec = spec


def _plain(arr):
    return _Op(arr, lambda t0, t1: ((t0, t1), lambda b0, b1: (b0, b1)))


def _plain_shape(shape):
    return lambda t0, t1: ((t0, t1), lambda b0, b1: (b0, b1))


def _dw_cols(rows, cols):
    rh = rows // 2

    def spec(t0, t1):
        assert (rh % t0 == 0 or t0 == rows) and cols % t1 == 0, (rh, cols, t0, t1)
        qr, qc = max(1, rh // t0), cols // t1
        if t0 == rows:
            return (2, None, rh, t1), lambda b0, b1: (0, b1 // qc, 0, b1 % qc)
        return (None, None, t0, t1), lambda b0, b1: (b0 // qr, b1 // qc, b0 % qr, b1 % qc)
    return (2, 4, rh, cols), spec


def _dw_rows(rows, cols):
    rh = rows // 2

    def spec(t0, t1):
        assert (rh % t0 == 0 or t0 == rows) and cols % t1 == 0, (rh, cols, t0, t1)
        qr = max(1, rh // t0)
        if t0 == rows:
            return (2, None, rh, t1), lambda b0, b1: (0, b0, 0, b1)
        return (None, None, t0, t1), lambda b0, b1: ((b0 // qr) % 2, b0 // (2 * qr), b0 % qr, b1)
    return (2, 4, rh, cols), spec


def _w_cols(g, j):
    _, _, rh, cols = g[j].shape
    return _Op(g[j], _dw_cols(2 * rh, cols)[1])


def _w_rows(g, j):
    _, _, rh, cols = g[j].shape
    return _Op(g[j], _dw_rows(2 * rh, cols)[1])


def _mm(name, mode, a, b, dims, outs, tiles, epilogue=None, extras=(), after=()):
    m, n, k = dims
    tm, tn, tk = tiles
    assert m % tm == 0 and n % tn == 0 and k % tk == 0, (name, dims, tiles)
    nk = k // tk
    if mode == 'nn':
        a_t, a_ix, b_t, b_ix, ca, cb = (tm, tk), (lambda i, j, kk: (i, kk)), (tk, tn), (lambda i, j, kk: (kk, j)), 1, 0
    elif mode == 'nt':
        a_t, a_ix, b_t, b_ix, ca, cb = (tm, tk), (lambda i, j, kk: (i, kk)), (tn, tk), (lambda i, j, kk: (j, kk)), 1, 1
    else:
        a_t, a_ix, b_t, b_ix, ca, cb = (tk, tm), (lambda i, j, kk: (kk, i)), (tk, tn), (lambda i, j, kk: (kk, j)), 0, 0
    a_blk, a_fn = a.spec(*a_t)
    b_blk, b_fn = b.spec(*b_t)
    in_specs = [pl.BlockSpec(a_blk, lambda i, j, kk: a_fn(*a_ix(i, j, kk))),
                pl.BlockSpec(b_blk, lambda i, j, kk: b_fn(*b_ix(i, j, kk)))]
    operands = [a.arr, b.arr]
    for e in extras:
        e_blk, e_fn = e.spec(tm, tn)
        in_specs.append(pl.BlockSpec(e_blk, functools.partial(lambda i, j, kk, f: f(i, j), f=e_fn)))
        operands.append(e.arr)
    out_shapes, out_specs = [], []
    for shape, dtype, spec in outs:
        o_blk, o_fn = spec(tm, tn)
        out_shapes.append(jax.ShapeDtypeStruct(shape, dtype))
        out_specs.append(pl.BlockSpec(o_blk, functools.partial(lambda i, j, kk, f: f(i, j), f=o_fn)))
    n_ex, n_out = len(extras), len(outs)
    in_specs += [ANY] * len(after)
    operands += list(after)
    if epilogue is None:
        epilogue = lambda acc: (acc,)

    def body(a_ref, b_ref, *rest):
        ex_refs = rest[:n_ex]
        rest = rest[:n_ex] + rest[n_ex + len(after):]
        out_refs = rest[n_ex:n_ex + n_out]
        bv = b_ref[...]
        if bv.ndim == 3:
            bv = bv.reshape(bv.shape[0] * bv.shape[1], bv.shape[2])
        p = lax.dot_general(a_ref[...].astype(BF16), bv.astype(BF16),
                            (((ca,), (cb,)), ((), ())), preferred_element_type=F32)

        def finish(acc):
            res = epilogue(acc, *[r[...] for r in ex_refs])
            for o_ref, val in zip(out_refs, res):
                o_ref[...] = val.astype(o_ref.dtype)

        if nk == 1:
            finish(p)
        else:
            acc_ref = rest[n_ex + n_out]
            kk = pl.program_id(2)

            @pl.when(kk == 0)
            def _():
                acc_ref[...] = p

            @pl.when(kk > 0)
            def _():
                acc_ref[...] += p

            @pl.when(kk == nk - 1)
            def _():
                finish(acc_ref[...])

    res = _pcall(
        body, name=name, grid=(m // tm, n // tn, nk),
        out_shape=out_shapes, in_specs=in_specs, out_specs=out_specs,
        scratch_shapes=[pltpu.VMEM((tm, tn), F32)] if nk > 1 else [],
        compiler_params=_params('parallel', 'parallel', 'arbitrary'),
    )(*operands)
    return res


def _rowwise(name, fn, row_ins, vec_ins, outs, reds, tm, after=()):
    rows = row_ins[0].shape[0]
    assert rows % tm == 0
    n_in = len(row_ins) + len(vec_ins)
    n_out = len(outs)

    def body(*refs):
        vals = [r[...] for r in refs[:n_in]]
        refs = refs[:n_in] + refs[n_in + len(after):]
        res = fn(*vals)
        for o_ref, val in zip(refs[n_in:n_in + n_out], res[:n_out]):
            o_ref[...] = val.astype(o_ref.dtype)
        first = pl.program_id(0) == 0
        for r_ref, val in zip(refs[n_in + n_out:], res[n_out:]):
            @pl.when(first)
            def _(r_ref=r_ref, val=val):
                r_ref[...] = val

            @pl.when(jnp.logical_not(first))
            def _(r_ref=r_ref, val=val):
                r_ref[...] += val

    in_specs = [pl.BlockSpec((tm, a.shape[1]), lambda i: (i, 0)) for a in row_ins]
    in_specs += [pl.BlockSpec((1, v.shape[1]), lambda i: (0, 0)) for v in vec_ins]
    in_specs += [ANY] * len(after)
    out_shapes = [jax.ShapeDtypeStruct((rows, w), dt) for w, dt in outs]
    out_shapes += [jax.ShapeDtypeStruct((1, w), F32) for w in reds]
    out_specs = [pl.BlockSpec((tm, w), lambda i: (i, 0)) for w, _ in outs]
    out_specs += [pl.BlockSpec((1, w), lambda i: (0, 0)) for w in reds]
    return _pcall(
        body, name=name, grid=(rows // tm,),
        out_shape=out_shapes, in_specs=in_specs, out_specs=out_specs,
        compiler_params=_params('arbitrary'),
    )(*row_ins, *vec_ins, *after)


def _norm_fwd(h, g, name, after=()):
    def fn(hv, gv):
        rstd = lax.rsqrt(jnp.mean(hv * hv, axis=-1, keepdims=True) + EPS)
        return (hv * rstd * gv,)
    return _rowwise(name, fn, [h], [g], [(h.shape[1], BF16)], [], 256, after)[0]


def _norm_bwd(h, dhn, dres, g, name):
    def fn(hv, dv, rv, gv):
        rstd = lax.rsqrt(jnp.mean(hv * hv, axis=-1, keepdims=True) + EPS)
        xhat = hv * rstd
        dxhat = dv * gv
        dh = rv + rstd * (dxhat - xhat * jnp.mean(dxhat * xhat, axis=-1, keepdims=True))
        return dh, jnp.sum(dv * xhat, axis=0, keepdims=True)
    w = h.shape[1]
    return _rowwise(name, fn, [h, dhn, dres], [g], [(w, F32)], [w], 256)


def _loss_head(h, target, g, name):
    w = h.shape[1]

    def fn(hv, tv, gv):
        rstd = lax.rsqrt(jnp.mean(hv * hv, axis=-1, keepdims=True) + EPS)
        xhat = hv * rstd
        diff = xhat * gv - tv
        dy = diff * (1.0 / w)
        dxhat = dy * gv
        dh = rstd * (dxhat - xhat * jnp.mean(dxhat * xhat, axis=-1, keepdims=True))
        return (dh, jnp.sum(0.5 * dy * diff, axis=0, keepdims=True),
                jnp.sum(dy * xhat, axis=0, keepdims=True))
    return _rowwise(name, fn, [h, target], [g], [(w, F32)], [w, w], 256)


def _split3(x):
    hi = x.astype(BF16)
    r1 = x - hi.astype(F32)
    mid = r1.astype(BF16)
    lo = (r1 - mid.astype(F32)).astype(BF16)
    return hi, mid, lo


def _tri_dot(tri, x):
    hi, mid, lo = _split3(x)
    d = lambda p: jnp.dot(tri, p, preferred_element_type=F32)
    return d(hi) + d(mid) + d(lo)


def _log_sigmoid(x):
    return jnp.minimum(x, 0.0) - jnp.log(1.0 + jnp.exp(-jnp.abs(x)))


def _gla_dims(proj_w, kw, vw, dk, dv):
    assert kw % dk == 0 and (2 * kw) % dv == 0 and (2 * kw + vw) % dv == 0 and (2 * kw + 2 * vw) % LANES == 0
    return dict(q0=0, k0=kw // dk, v0=2 * kw // dv, r0=(2 * kw + vw) // dv, g0=(2 * kw + 2 * vw) // LANES)


def _gla_gates(gl, wgu, bias):
    pre = jnp.dot(gl.astype(BF16), wgu, preferred_element_type=F32) + bias
    la = _log_sigmoid(pre) * (1.0 / GLA_GATE_TEMP)
    r_i = lax.broadcasted_iota(jnp.int32, (CHUNK, CHUNK), 0)
    c_i = lax.broadcasted_iota(jnp.int32, (CHUNK, CHUNK), 1)
    cum = _tri_dot((c_i <= r_i).astype(BF16), la)
    total = cum[CHUNK - 1:CHUNK, :]
    return pre, cum, total


def _gla_scan_fwd(proj, wgu_pad, b_gate, o_norm, heads, kw, vw, tb, name):
    seq, pw = proj.shape
    dk, dv = kw // heads, vw // heads
    cb = tb // CHUNK
    nt = seq // tb
    o = _gla_dims(pw, kw, vw, dk, dv)
    scale = dk ** -0.5

    def body(q_ref, k_ref, v_ref, r_ref, gl_ref, wgu_ref, b_ref, on_ref, out_ref, st_ref, s_scr):
        @pl.when(pl.program_id(1) == 0)
        def _():
            s_scr[...] = jnp.zeros_like(s_scr)

        wgu = wgu_ref[...].astype(BF16)
        bias = b_ref[...]
        onorm = on_ref[...]
        st = s_scr[...]
        for ci in range(cb):
            rows = pl.ds(ci * CHUNK, CHUNK)
            _, cum, total = _gla_gates(gl_ref[rows, :], wgu, bias)
            kdec = k_ref[rows, :] * jnp.exp(total - cum)
            st = st * jnp.exp(total) + lax.dot_general(
                v_ref[rows, :].astype(BF16), kdec.astype(BF16), (((0,), (0,)), ((), ())),
                preferred_element_type=F32)
            st_ref[ci] = st
            qs = (q_ref[rows, :] * scale).astype(BF16)
            ov = lax.dot_general(qs, st.astype(BF16), (((1,), (1,)), ((), ())), preferred_element_type=F32)
            rstd = lax.rsqrt(jnp.mean(ov * ov, axis=-1, keepdims=True) + EPS)
            rv = r_ref[rows, :]
            out_ref[rows, :] = (ov * rstd * onorm * (rv * jax.nn.sigmoid(rv))).astype(out_ref.dtype)
        s_scr[...] = st

    in_specs = [
        pl.BlockSpec((tb, dk), lambda h, t: (t, o['q0'] + h)),
        pl.BlockSpec((tb, dk), lambda h, t: (t, o['k0'] + h)),
        pl.BlockSpec((tb, dv), lambda h, t: (t, o['v0'] + h)),
        pl.BlockSpec((tb, dv), lambda h, t: (t, o['r0'] + h)),
        pl.BlockSpec((tb, LANES), lambda h, t: (t, o['g0'])),
        pl.BlockSpec((LANES, dk), lambda h, t: (0, h)),
        pl.BlockSpec((1, dk), lambda h, t: (0, h)),
        pl.BlockSpec((1, dv), lambda h, t: (0, 0)),
    ]
    return _pcall(
        body, name=name, grid=(heads, nt),
        out_shape=[jax.ShapeDtypeStruct((seq, vw), BF16),
                   jax.ShapeDtypeStruct((heads, seq // CHUNK, dv, dk), F32)],
        in_specs=in_specs,
        out_specs=[pl.BlockSpec((tb, dv), lambda h, t: (t, h)),
                   pl.BlockSpec((None, cb, dv, dk), lambda h, t: (h, t, 0, 0))],
        scratch_shapes=[pltpu.VMEM((dv, dk), F32)],
        compiler_params=_params('parallel', 'arbitrary'),
    )(proj, proj, proj, proj, proj, wgu_pad, b_gate, o_norm)


def _gla_scan_bwd(proj, wgu_pad, b_gate, o_norm, states, dgated, heads, kw, vw, tb, name):
    seq, pw = proj.shape
    dk, dv = kw // heads, vw // heads
    cb = tb // CHUNK
    nt = seq // tb
    o = _gla_dims(pw, kw, vw, dk, dv)
    scale = dk ** -0.5

    def body(q_ref, k_ref, v_ref, r_ref, gl_ref, wgu_ref, b_ref, on_ref, st_ref, stp_ref, dg_ref,
             dq_ref, dk_ref, dv_ref, dr_ref, dpre_ref, db_ref, don_ref, ds_scr):
        hh = pl.program_id(0)
        t = pl.program_id(1)

        @pl.when(t == 0)
        def _():
            ds_scr[...] = jnp.zeros_like(ds_scr)
            db_ref[...] = jnp.zeros_like(db_ref)

        @pl.when(jnp.logical_and(hh == 0, t == 0))
        def _():
            don_ref[...] = jnp.zeros_like(don_ref)

        wgu = wgu_ref[...].astype(BF16)
        bias = b_ref[...]
        onorm = on_ref[...]
        has_prev = (t < nt - 1).astype(F32)
        r_i = lax.broadcasted_iota(jnp.int32, (CHUNK, CHUNK), 0)
        c_i = lax.broadcasted_iota(jnp.int32, (CHUNK, CHUNK), 1)
        strict = (c_i < r_i).astype(BF16)
        carry = ds_scr[...]
        db_acc = jnp.zeros((1, dk), F32)
        don_acc = jnp.zeros((1, dv), F32)
        for ci in reversed(range(cb)):
            rows = pl.ds(ci * CHUNK, CHUNK)
            pre, cum, total = _gla_gates(gl_ref[rows, :], wgu, bias)
            edec = jnp.exp(total - cum)
            decay = jnp.exp(total)
            kdec = k_ref[rows, :] * edec
            st = st_ref[ci]
            st_prev = st_ref[ci - 1] if ci > 0 else stp_ref[0] * has_prev
            stb = st.astype(BF16)
            qs = (q_ref[rows, :] * scale).astype(BF16)
            vb = v_ref[rows, :].astype(BF16)
            ov = lax.dot_general(qs, stb, (((1,), (1,)), ((), ())), preferred_element_type=F32)
            rstd = lax.rsqrt(jnp.mean(ov * ov, axis=-1, keepdims=True) + EPS)
            ohat = ov * rstd
            rv = r_ref[rows, :]
            sr = jax.nn.sigmoid(rv)
            dgv = dg_ref[rows, :]
            dy = dgv * (rv * sr)
            dr_ref[rows, :] = (dgv * (ohat * onorm) * (sr * (1.0 + rv * (1.0 - sr)))).astype(dr_ref.dtype)
            don_acc = don_acc + jnp.sum(dy * ohat, axis=0, keepdims=True)
            dohat = dy * onorm
            do = (rstd * (dohat - ohat * jnp.mean(dohat * ohat, axis=-1, keepdims=True))).astype(BF16)
            dq_ref[rows, :] = (jnp.dot(do, stb, preferred_element_type=F32) * scale).astype(dq_ref.dtype)
            dst = carry + lax.dot_general(do, qs, (((0,), (0,)), ((), ())), preferred_element_type=F32)
            dstb = dst.astype(BF16)
            dkdec = jnp.dot(vb, dstb, preferred_element_type=F32)
            dv_ref[rows, :] = lax.dot_general(kdec.astype(BF16), dstb, (((1,), (1,)), ((), ())),
                                              preferred_element_type=F32).astype(dv_ref.dtype)
            ddecay = jnp.sum(dst * st_prev, axis=0, keepdims=True)
            dk_ref[rows, :] = (dkdec * edec).astype(dk_ref.dtype)
            da = ddecay * decay + _tri_dot(strict, dkdec * kdec)
            dpre = da * (1.0 / GLA_GATE_TEMP) * (1.0 - jax.nn.sigmoid(pre))
            dpre_ref[rows, :] = dpre.astype(dpre_ref.dtype)
            db_acc = db_acc + jnp.sum(dpre, axis=0, keepdims=True)
            carry = dst * decay
        ds_scr[...] = carry
        db_ref[...] += db_acc
        don_ref[...] += don_acc

    rt = lambda t: nt - 1 - t
    in_specs = [
        pl.BlockSpec((tb, dk), lambda h, t: (rt(t), o['q0'] + h)),
        pl.BlockSpec((tb, dk), lambda h, t: (rt(t), o['k0'] + h)),
        pl.BlockSpec((tb, dv), lambda h, t: (rt(t), o['v0'] + h)),
        pl.BlockSpec((tb, dv), lambda h, t: (rt(t), o['r0'] + h)),
        pl.BlockSpec((tb, LANES), lambda h, t: (rt(t), o['g0'])),
        pl.BlockSpec((LANES, dk), lambda h, t: (0, h)),
        pl.BlockSpec((1, dk), lambda h, t: (0, h)),
        pl.BlockSpec((1, dv), lambda h, t: (0, 0)),
        pl.BlockSpec((None, cb, dv, dk), lambda h, t: (h, rt(t), 0, 0)),
        pl.BlockSpec((None, 1, dv, dk), lambda h, t: (h, jnp.maximum(rt(t) * cb - 1, 0), 0, 0)),
        pl.BlockSpec((tb, dv), lambda h, t: (rt(t), h)),
    ]
    out_shape = [jax.ShapeDtypeStruct((seq, kw), BF16), jax.ShapeDtypeStruct((seq, kw), BF16),
                 jax.ShapeDtypeStruct((seq, vw), BF16), jax.ShapeDtypeStruct((seq, vw), BF16),
                 jax.ShapeDtypeStruct((seq, kw), BF16),
                 jax.ShapeDtypeStruct((1, kw), F32), jax.ShapeDtypeStruct((1, dv), F32)]
    out_specs = [pl.BlockSpec((tb, dk), lambda h, t: (rt(t), h)),
                 pl.BlockSpec((tb, dk), lambda h, t: (rt(t), h)),
                 pl.BlockSpec((tb, dv), lambda h, t: (rt(t), h)),
                 pl.BlockSpec((tb, dv), lambda h, t: (rt(t), h)),
                 pl.BlockSpec((tb, dk), lambda h, t: (rt(t), h)),
                 pl.BlockSpec((1, dk), lambda h, t: (0, h)),
                 pl.BlockSpec((1, dv), lambda h, t: (0, 0))]
    return _pcall(
        body, name=name, grid=(heads, nt),
        out_shape=out_shape, in_specs=in_specs, out_specs=out_specs,
        scratch_shapes=[pltpu.VMEM((dv, dk), F32)],
        compiler_params=_params('arbitrary', 'arbitrary'),
    )(proj, proj, proj, proj, proj, wgu_pad, b_gate, o_norm, states, states, dgated)


def _cmul(ar, ai, br, bi):
    return ar * br - ai * bi, ar * bi + ai * br


def _gelu(y):
    c = math.sqrt(2.0 / math.pi)
    return 0.5 * y * (1.0 + jnp.tanh(c * (y + 0.044715 * y * y * y)))


def _gelu_grad(y):
    c = math.sqrt(2.0 / math.pi)
    th = jnp.tanh(c * (y + 0.044715 * y * y * y))
    return 0.5 * (1.0 + th) + 0.5 * y * (1.0 - th * th) * (c * (1.0 + 3.0 * 0.044715 * y * y))


def _power_pow2(ar, ai, n):
    assert n & (n - 1) == 0
    for _ in range(n.bit_length() - 1):
        ar, ai = _cmul(ar, ai, ar, ai)
    return ar, ai


def _s5_fwd(u, bre, bim, cre, cim, are, aim, dskip, name):
    seq, width = u.shape
    nb, ub, sb = bre.shape
    ls = seq // S5_SEGMENTS
    seg = S5_SEGMENTS

    def body(u_ref, bre_ref, bim_ref, cre_ref, cim_ref, are_ref, aim_ref, d_ref, y_ref, z_ref, xr_ref, xi_ref):
        uv = u_ref[...]
        ub16 = uv.astype(BF16)
        xr_ref[...] = jnp.dot(ub16, bre_ref[...].astype(BF16), preferred_element_type=F32)
        xi_ref[...] = jnp.dot(ub16, bim_ref[...].astype(BF16), preferred_element_type=F32)
        ar = jnp.broadcast_to(are_ref[...], (seg, sb))
        ai = jnp.broadcast_to(aim_ref[...], (seg, sb))

        def step(i, c):
            rows = pl.ds(pl.multiple_of(i * seg, seg), seg)
            pr, pi = _cmul(ar, ai, c[0], c[1])
            nr = pr + xr_ref[rows, :]
            ni = pi + xi_ref[rows, :]
            xr_ref[rows, :] = nr
            xi_ref[rows, :] = ni
            return nr, ni

        zero = jnp.zeros((seg, sb), F32)
        er, ei = lax.fori_loop(0, ls, step, (zero, zero), unroll=8)
        pr, pi = _power_pow2(ar, ai, ls)
        row = lax.broadcasted_iota(jnp.int32, (seg, sb), 0)
        sr, si = zero, zero
        for _ in range(seg - 1):
            tr, ti = _cmul(pr, pi, sr, si)
            sr = jnp.where(row == 0, 0.0, pltpu.roll(tr + er, 1, 0))
            si = jnp.where(row == 0, 0.0, pltpu.roll(ti + ei, 1, 0))

        def fix(i, c):
            rows = pl.ds(pl.multiple_of(i * seg, seg), seg)
            fr, fi = _cmul(c[0], c[1], sr, si)
            xr_ref[rows, :] += fr
            xi_ref[rows, :] += fi
            return _cmul(c[0], c[1], ar, ai)

        lax.fori_loop(0, ls, fix, (ar, ai), unroll=8)
        y = (jnp.dot(xr_ref[...].astype(BF16), cre_ref[...].astype(BF16), preferred_element_type=F32)
             - jnp.dot(xi_ref[...].astype(BF16), cim_ref[...].astype(BF16), preferred_element_type=F32)
             + d_ref[...] * uv)
        y_ref[...] = y
        z_ref[...] = _gelu(y).astype(z_ref.dtype)

    mat = lambda r, c: pl.BlockSpec((None, r, c), lambda b: (b, 0, 0))
    return _pcall(
        body, name=name, grid=(nb,),
        out_shape=[jax.ShapeDtypeStruct((seq, width), F32), jax.ShapeDtypeStruct((seq, width), BF16),
                   jax.ShapeDtypeStruct((seq, nb * sb), F32), jax.ShapeDtypeStruct((seq, nb * sb), F32)],
        in_specs=[pl.BlockSpec((seq, ub), lambda b: (0, b)), mat(ub, sb), mat(ub, sb), mat(sb, ub), mat(sb, ub),
                  mat(1, sb), mat(1, sb), pl.BlockSpec((1, ub), lambda b: (0, b))],
        out_specs=[pl.BlockSpec((seq, ub), lambda b: (0, b)), pl.BlockSpec((seq, ub), lambda b: (0, b)),
                   pl.BlockSpec((seq, sb), lambda b: (0, b)), pl.BlockSpec((seq, sb), lambda b: (0, b))],
        compiler_params=_params('parallel'),
    )(u, bre, bim, cre, cim, are, aim, dskip)


def _s5_bwd(dz, y, u, xr, xi, bre, bim, cre, cim, are, aim, dskip, name):
    seq, width = u.shape
    nb, ub, sb = bre.shape
    ls = seq // S5_SEGMENTS
    seg = S5_SEGMENTS

    def body(dz_ref, y_ref, u_ref, xr_ref, xi_ref, bre_ref, bim_ref, cre_ref, cim_ref, are_ref, aim_ref, d_ref,
             du_ref, dcr_ref, dci_ref, dbr_ref, dbi_ref, dar_ref, dai_ref, dd_ref, lr_ref, li_ref):
        uv = u_ref[...]
        dy = dz_ref[...] * _gelu_grad(y_ref[...])
        dd_ref[...] = jnp.sum(dy * uv, axis=0, keepdims=True)
        dyb = dy.astype(BF16)
        nt = (((1,), (1,)), ((), ()))
        tn = (((0,), (0,)), ((), ()))
        lr_ref[...] = lax.dot_general(dyb, cre_ref[...].astype(BF16), nt, preferred_element_type=F32)
        li_ref[...] = -lax.dot_general(dyb, cim_ref[...].astype(BF16), nt, preferred_element_type=F32)
        dcr_ref[...] = lax.dot_general(dyb, xr_ref[...].astype(BF16), tn, preferred_element_type=F32)
        dci_ref[...] = -lax.dot_general(dyb, xi_ref[...].astype(BF16), tn, preferred_element_type=F32)
        ar = jnp.broadcast_to(are_ref[...], (seg, sb))
        ai = jnp.broadcast_to(aim_ref[...], (seg, sb))
        nai = -ai

        def step(ii, c):
            rows = pl.ds(pl.multiple_of((ls - 1 - ii) * seg, seg), seg)
            pr, pi = _cmul(ar, nai, c[0], c[1])
            nr = pr + lr_ref[rows, :]
            ni = pi + li_ref[rows, :]
            lr_ref[rows, :] = nr
            li_ref[rows, :] = ni
            return nr, ni

        zero = jnp.zeros((seg, sb), F32)
        er, ei = lax.fori_loop(0, ls, step, (zero, zero), unroll=8)
        pr, pi = _power_pow2(ar, nai, ls)
        row = lax.broadcasted_iota(jnp.int32, (seg, sb), 0)
        rr, ri = zero, zero
        for _ in range(seg - 1):
            tr, ti = _cmul(pr, pi, rr, ri)
            rr = jnp.where(row == seg - 1, 0.0, pltpu.roll(tr + er, seg - 1, 0))
            ri = jnp.where(row == seg - 1, 0.0, pltpu.roll(ti + ei, seg - 1, 0))

        def corrected(rows, qr, qi):
            fr, fi = _cmul(qr, qi, rr, ri)
            nr = lr_ref[rows, :] + fr
            ni = li_ref[rows, :] + fi
            lr_ref[rows, :] = nr
            li_ref[rows, :] = ni
            return nr, ni

        def grad_a(nr, ni, xpr, xpi, accr, acci):
            return accr + nr * xpr + ni * xpi, acci + ni * xpr - nr * xpi

        def fix(ii, c):
            qr, qi, accr, acci = c
            i = ls - 1 - ii
            rows = pl.ds(pl.multiple_of(i * seg, seg), seg)
            prev = pl.ds(pl.multiple_of((i - 1) * seg, seg), seg)
            nr, ni = corrected(rows, qr, qi)
            accr, acci = grad_a(nr, ni, xr_ref[prev, :], xi_ref[prev, :], accr, acci)
            qr, qi = _cmul(qr, qi, ar, nai)
            return qr, qi, accr, acci

        qr, qi, accr, acci = lax.fori_loop(0, ls - 1, fix, (ar, nai, zero, zero), unroll=8)
        nr, ni = corrected(pl.ds(0, seg), qr, qi)
        last = pl.ds((ls - 1) * seg, seg)
        xpr = jnp.where(row == 0, 0.0, pltpu.roll(xr_ref[last, :], 1, 0))
        xpi = jnp.where(row == 0, 0.0, pltpu.roll(xi_ref[last, :], 1, 0))
        accr, acci = grad_a(nr, ni, xpr, xpi, accr, acci)
        dar_ref[...] = jnp.sum(accr, axis=0, keepdims=True)
        dai_ref[...] = jnp.sum(acci, axis=0, keepdims=True)
        lrb = lr_ref[...].astype(BF16)
        lib = li_ref[...].astype(BF16)
        ub16 = uv.astype(BF16)
        dbr_ref[...] = lax.dot_general(ub16, lrb, tn, preferred_element_type=F32)
        dbi_ref[...] = lax.dot_general(ub16, lib, tn, preferred_element_type=F32)
        du_ref[...] = (d_ref[...] * dy
                       + lax.dot_general(lrb, bre_ref[...].astype(BF16), nt, preferred_element_type=F32)
                       + lax.dot_general(lib, bim_ref[...].astype(BF16), nt, preferred_element_type=F32))

    mat = lambda r, c: pl.BlockSpec((None, r, c), lambda b: (b, 0, 0))
    col = lambda w: pl.BlockSpec((seq, w), lambda b: (0, b))
    return _pcall(
        body, name=name, grid=(nb,),
        out_shape=[jax.ShapeDtypeStruct((seq, width), F32)]
        + [jax.ShapeDtypeStruct((nb, ub, sb), F32)] * 4
        + [jax.ShapeDtypeStruct((nb, 1, sb), F32)] * 2
        + [jax.ShapeDtypeStruct((1, width), F32)],
        in_specs=[col(ub), col(ub), col(ub), col(sb), col(sb), mat(ub, sb), mat(ub, sb), mat(sb, ub), mat(sb, ub),
                  mat(1, sb), mat(1, sb), pl.BlockSpec((1, ub), lambda b: (0, b))],
        out_specs=[col(ub), mat(ub, sb), mat(ub, sb), mat(ub, sb), mat(ub, sb), mat(1, sb), mat(1, sb),
                   pl.BlockSpec((1, ub), lambda b: (0, b))],
        scratch_shapes=[pltpu.VMEM((seq, sb), F32), pltpu.VMEM((seq, sb), F32)],
        compiler_params=_params('parallel'),
    )(dz, y, u, xr, xi, bre, bim, cre, cim, are, aim, dskip)


def _s5_discretise(lam_re, lam_im, log_dt, b_re, b_im):
    lr = jnp.minimum(lam_re, S5_EIG_CLIP)
    li = lam_im
    dt = jnp.exp(log_dt)[:, None]
    mag = jnp.exp(lr * dt)
    ang = li * dt
    ab_re = mag * jnp.cos(ang)
    ab_im = mag * jnp.sin(ang)
    den = lr * lr + li * li
    nr = ab_re - 1.0
    f_re = (nr * lr + ab_im * li) / den
    f_im = (ab_im * lr - nr * li) / den
    bb_re = f_re[..., None] * b_re - f_im[..., None] * b_im
    bb_im = f_re[..., None] * b_im + f_im[..., None] * b_re
    return ab_re, ab_im, bb_re, bb_im


def _to_blocks(m):
    g, a, b = m.shape
    gb = S5_GROUPS_PER_BLOCK
    eye = jnp.eye(gb, dtype=m.dtype)
    return jnp.einsum('bgac,gh->bgahc', m.reshape(g // gb, gb, a, b), eye).reshape(g // gb, gb * a, gb * b)


def _from_blocks(m, a, b):
    nb = m.shape[0]
    gb = S5_GROUPS_PER_BLOCK
    eye = jnp.eye(gb, dtype=m.dtype)
    return jnp.einsum('bgahc,gh->bgac', m.reshape(nb, gb, a, gb, b), eye).reshape(nb * gb, a, b)


def _glu_fwd(o, h, name):
    half = o.shape[1] // 2

    def fn(ov, hv):
        return (hv + ov[:, :half] * jax.nn.sigmoid(ov[:, half:]),)
    return _rowwise(name, fn, [o, h], [], [(half, F32)], [], 256)[0]


def _glu_bwd(o, dout, name):
    half = o.shape[1] // 2

    def fn(ov, dv):
        val, gate = ov[:, :half], ov[:, half:]
        sg = jax.nn.sigmoid(gate)
        return (jnp.concatenate([dv * sg, dv * val * sg * (1.0 - sg)], axis=1),)
    return _rowwise(name, fn, [o, dout], [], [(2 * half, BF16)], [], 256)[0]


def _adam_math(w, g, m, v):
    m = ADAM_B1 * m + (1.0 - ADAM_B1) * g
    v = ADAM_B2 * v + (1.0 - ADAM_B2) * (g * g)
    m_hat = m / (1.0 - ADAM_B1 ** ADAM_STEP)
    v_hat = v / (1.0 - ADAM_B2 ** ADAM_STEP)
    delta = -ADAM_LR * (m_hat / (jnp.sqrt(v_hat) + ADAM_EPS) + ADAM_WD * w)
    return delta, m, v


def _adamw(w, m, v, grads, name, after=()):
    nl, rows, cols = w.shape
    tm = _tile(rows, max(8, (1 << 18) // cols // 8 * 8), unit=8)
    nbk = rows // tm

    def body(*refs):
        w_ref, m_ref, v_ref = refs[:3]
        g_refs = refs[3:3 + nl]
        go_ref, d_ref, mo_ref, vo_ref = refs[3 + nl + len(after):]
        layer = pl.program_id(0)
        g = g_refs[0][...]
        for l in range(1, nl):
            g = jnp.where(layer == l, g_refs[l][...], g)
        delta, mn, vn = _adam_math(w_ref[...], g, m_ref[...], v_ref[...])
        go_ref[...] = g
        d_ref[...] = delta
        mo_ref[...] = mn
        vo_ref[...] = vn

    stacked = pl.BlockSpec((None, tm, cols), lambda l, i: (l, i, 0))

    def g_spec(layer):
        return pl.BlockSpec((tm, cols), lambda l, i: (jnp.where(l == layer, i, jnp.where(l < layer, 0, nbk - 1)), 0))

    return _pcall(
        body, name=name, grid=(nl, nbk),
        out_shape=[jax.ShapeDtypeStruct(w.shape, F32)] * 4,
        in_specs=[stacked] * 3 + [g_spec(l) for l in range(nl)] + [ANY] * len(after),
        out_specs=[stacked] * 4,
        compiler_params=_params('arbitrary', 'arbitrary'),
    )(w, m, v, *grads, *after)


def _adamw_t(w, m, v, g, name, after=()):
    cols, nl, rows = w.shape
    budget = max(1, (1 << 21) // (8 * rows * 4))
    tc = max(t for t in range(1, min(cols, budget) + 1) if cols % t == 0)

    def body(*refs):
        w_ref, m_ref, v_ref, g_ref = refs[:4]
        d_ref, mo_ref, vo_ref = refs[4 + len(after):]
        delta, mn, vn = _adam_math(w_ref[...], g_ref[...], m_ref[...], v_ref[...])
        d_ref[...] = delta
        mo_ref[...] = mn
        vo_ref[...] = vn

    blk = pl.BlockSpec((tc, nl, rows), lambda i: (i, 0, 0))
    return _pcall(
        body, name=name, grid=(cols // tc,),
        out_shape=[jax.ShapeDtypeStruct(w.shape, F32)] * 3,
        in_specs=[blk] * 4 + [ANY] * len(after), out_specs=[blk] * 3,
        compiler_params=_params('parallel'),
    )(w, m, v, g, *after)


def _pack(arrs, rows_mult=512):
    flat = jnp.concatenate([a.reshape(-1) for a in arrs])
    total = flat.shape[0]
    rows = -(-total // LANES)
    rows = -(-rows // rows_mult) * rows_mult
    flat = jnp.pad(flat, (0, rows * LANES - total))
    return flat.reshape(rows, LANES)


def _unpack(packed, shapes):
    flat = packed.reshape(-1)
    out, off = [], 0
    for s in shapes:
        size = math.prod(s)
        out.append(flat[off:off + size].reshape(s))
        off += size
    return out


def _permute(a):
    seq, w = a.shape
    return a.reshape(S5_SEGMENTS, seq // S5_SEGMENTS, w).transpose(1, 0, 2).reshape(seq, w)


def _unpermute(a):
    seq, w = a.shape
    return a.reshape(seq // S5_SEGMENTS, S5_SEGMENTS, w).transpose(1, 0, 2).reshape(seq, w)


def kernel(x, gla_norm, gla_w_in, gla_w_gate_up, gla_b_gate, gla_o_norm, gla_w_out, s5_norm, s5_w_in, s5_lam_re, s5_lam_im, s5_log_dt, s5_b_re, s5_b_im, s5_c_re, s5_c_im, s5_d, s5_w_out, mlp_norm, mlp_w_up, mlp_w_down, final_norm, loss_target, m_gla_norm, m_gla_w_in, m_gla_w_gate_up, m_gla_b_gate, m_gla_o_norm, m_gla_w_out, m_s5_norm, m_s5_w_in, m_s5_lam_re, m_s5_lam_im, m_s5_log_dt, m_s5_b_re, m_s5_b_im, m_s5_c_re, m_s5_c_im, m_s5_d, m_s5_w_out, m_mlp_norm, m_mlp_w_up, m_mlp_w_down, m_final_norm, v_gla_norm, v_gla_w_in, v_gla_w_gate_up, v_gla_b_gate, v_gla_o_norm, v_gla_w_out, v_s5_norm, v_s5_w_in, v_s5_lam_re, v_s5_lam_im, v_s5_log_dt, v_s5_b_re, v_s5_b_im, v_s5_c_re, v_s5_c_im, v_s5_d, v_s5_w_out, v_mlp_norm, v_mlp_w_up, v_mlp_w_down, v_final_norm):
    weights = dict(gla_norm=gla_norm, gla_w_in=gla_w_in, gla_w_gate_up=gla_w_gate_up, gla_b_gate=gla_b_gate, gla_o_norm=gla_o_norm, gla_w_out=gla_w_out, s5_norm=s5_norm, s5_w_in=s5_w_in, s5_lam_re=s5_lam_re, s5_lam_im=s5_lam_im, s5_log_dt=s5_log_dt, s5_b_re=s5_b_re, s5_b_im=s5_b_im, s5_c_re=s5_c_re, s5_c_im=s5_c_im, s5_d=s5_d, s5_w_out=s5_w_out, mlp_norm=mlp_norm, mlp_w_up=mlp_w_up, mlp_w_down=mlp_w_down, final_norm=final_norm)
    mom1 = dict(gla_norm=m_gla_norm, gla_w_in=m_gla_w_in, gla_w_gate_up=m_gla_w_gate_up, gla_b_gate=m_gla_b_gate, gla_o_norm=m_gla_o_norm, gla_w_out=m_gla_w_out, s5_norm=m_s5_norm, s5_w_in=m_s5_w_in, s5_lam_re=m_s5_lam_re, s5_lam_im=m_s5_lam_im, s5_log_dt=m_s5_log_dt, s5_b_re=m_s5_b_re, s5_b_im=m_s5_b_im, s5_c_re=m_s5_c_re, s5_c_im=m_s5_c_im, s5_d=m_s5_d, s5_w_out=m_s5_w_out, mlp_norm=m_mlp_norm, mlp_w_up=m_mlp_w_up, mlp_w_down=m_mlp_w_down, final_norm=m_final_norm)
    mom2 = dict(gla_norm=v_gla_norm, gla_w_in=v_gla_w_in, gla_w_gate_up=v_gla_w_gate_up, gla_b_gate=v_gla_b_gate, gla_o_norm=v_gla_o_norm, gla_w_out=v_gla_w_out, s5_norm=v_s5_norm, s5_w_in=v_s5_w_in, s5_lam_re=v_s5_lam_re, s5_lam_im=v_s5_lam_im, s5_log_dt=v_s5_log_dt, s5_b_re=v_s5_b_re, s5_b_im=v_s5_b_im, s5_c_re=v_s5_c_re, s5_c_im=v_s5_c_im, s5_d=v_s5_d, s5_w_out=v_s5_w_out, mlp_norm=v_mlp_norm, mlp_w_up=v_mlp_w_up, mlp_w_down=v_mlp_w_down, final_norm=v_final_norm)
    names = list(weights)
    big = ['gla_w_in', 'gla_w_out', 's5_w_in', 's5_w_out', 'mlp_w_up', 'mlp_w_down']
    small = [n for n in names if n not in big]

    chip = 2 * lax.axis_index('x') + lax.axis_index('y')
    h0 = x[0]
    target = loss_target[0]
    seq, dm = h0.shape
    depth = mlp_norm.shape[0]
    n_gla = gla_norm.shape[0]
    n_s5 = s5_lam_re.shape[0]
    rank = gla_w_gate_up.shape[1]
    kw = gla_b_gate.shape[1]
    dv = gla_o_norm.shape[1]
    in_w = 4 * gla_w_in.shape[2]
    vw = (in_w - rank - 2 * kw) // 2
    heads = vw // dv
    dk = kw // heads
    pw = -(-in_w // LANES) * LANES
    s5w = s5_w_in.shape[2]
    n_grp, n_state, grp = s5_b_re.shape[1:]
    hid = 4 * mlp_w_up.shape[2]
    tb = min(seq, 8 * CHUNK)
    tm = _tile(seq, 1024)

    ic = lax.axis_index('c')
    rh = lambda w: w.shape[1] // 2
    wb16 = {n: weights[n].astype(BF16) for n in big}
    gathered = {n: [None] * weights[n].shape[0] for n in big}
    g_w_in, g_gla_out, g_s5_in, g_s5_out, g_up, g_down = (gathered[n] for n in big)
    in_flight = {}

    to_start = []
    for i in range(depth):
        mix = ['gla_w_in', 'gla_w_out'] if i % 2 == 0 else ['s5_w_in', 's5_w_out']
        to_start += [(m, i // 2) for m in mix] + [('mlp_w_up', i), ('mlp_w_down', i)]

    def start_gathers(after):
        while to_start and len(in_flight) < GATHERS_IN_FLIGHT:
            n, l = to_start.pop(0)
            rows, cols = weights[n].shape[1:]
            hd = _start('chips_gather', wb16[n][l].reshape(2, rows // 2, cols), (2, 4, rows // 2, cols),
                        f'ag_{n}_{l}_start', after)
            in_flight[n, l] = hd
            after = [hd['token']]
        return after

    def finish_gather(n, l, after):
        halves, land = _wait(in_flight.pop((n, l)), f'ag_{n}_{l}_wait', after)
        behind = start_gathers([land])
        own = lax.dynamic_index_in_dim(halves, ic, 0, keepdims=True)
        land = lax.dynamic_update_slice(land, own[:, None], (ic, chip, 0, 0))
        hd = _start('pair_inplace', None, land, f'ag_{n}_{l}_pair_start', [])
        gathered[n][l] = _wait(hd, f'ag_{n}_{l}_pair_wait', [])[0]
        return behind

    sharded_small = [gla_w_gate_up, s5_norm, s5_d]
    gathered_small = _exchange(_pack(sharded_small), 'xy', 'bcast', 'ag_small')
    start_gathers([gathered_small])
    parts = [_unpack(gathered_small[k], [a.shape for a in sharded_small]) for k in range(4)]
    wgu_full = jnp.concatenate([p[0] for p in parts], axis=2)
    s5_norm_full = jnp.concatenate([p[1] for p in parts], axis=1)
    s5_d_full = jnp.concatenate([p[2] for p in parts], axis=1)

    def gla_w_in_padded(j):
        wj = g_w_in[j].transpose(0, 2, 1, 3).reshape(dm, in_w)
        return jnp.pad(wj, ((0, 0), (0, pw - in_w)))

    grads = {n: [None] * weights[n].shape[0] for n in names if n != 'final_norm'}

    saved = []
    h = h0
    for i in range(depth):
        j = i // 2
        rec = {}
        if i % 2 == 0:
            rec['h_in'] = h
            behind = finish_gather('gla_w_in', j, [h])
            hn = _norm_fwd(h, gla_norm[j:j + 1], 'gla_norm_fwd', behind)
            w_in_pad = gla_w_in_padded(j)
            proj = _mm('gla_proj', 'nn', _plain(hn), _plain(w_in_pad), (seq, pw, dm),
                       [((seq, pw), F32, _plain_shape(None))], (tm, _tile(pw, 1024), dm))[0]
            wgu_pad = jnp.pad(wgu_full[j], ((0, LANES - rank), (0, 0)))
            gated, states = _gla_scan_fwd(proj, wgu_pad, gla_b_gate[j:j + 1], gla_o_norm[j:j + 1],
                                          heads, kw, vw, tb, 'gla_scan_fwd')
            behind = finish_gather('gla_w_out', j, [gated])
            h = _mm('gla_out', 'nn', _plain(gated), _w_rows(g_gla_out, j), (seq, dm, vw),
                    [((seq, dm), F32, _plain_shape(None))],
                    (tm, _tile(dm, 1024), gla_w_out.shape[1]),
                    epilogue=lambda acc, hv: (acc + hv,), extras=[_plain(h)], after=behind)[0]
            rec.update(hn=hn, w_in_pad=w_in_pad, proj=proj, wgu_pad=wgu_pad, gated=gated, states=states)
        else:
            hp = _permute(h)
            rec['h_in'] = hp
            hn = _norm_fwd(hp, s5_norm_full[j:j + 1], 's5_norm_fwd')
            behind = finish_gather('s5_w_in', j, [hn])
            u = _mm('s5_in', 'nn', _plain(hn), _w_rows(g_s5_in, j), (seq, s5w, dm),
                    [((seq, s5w), F32, _plain_shape(None))],
                    (tm, _tile(s5w, 1024), s5_w_in.shape[1]), after=behind)[0]
            disc, disc_vjp = jax.vjp(_s5_discretise, s5_lam_re[j], s5_lam_im[j], s5_log_dt[j], s5_b_re[j], s5_b_im[j])
            ab_re, ab_im, bb_re, bb_im = disc
            bre = _to_blocks(bb_re.transpose(0, 2, 1))
            bim = _to_blocks(bb_im.transpose(0, 2, 1))
            cre = _to_blocks(s5_c_re[j].transpose(0, 2, 1))
            cim = _to_blocks(s5_c_im[j].transpose(0, 2, 1))
            nb = n_grp // S5_GROUPS_PER_BLOCK
            are = ab_re.reshape(nb, 1, S5_GROUPS_PER_BLOCK * n_state)
            aim = ab_im.reshape(nb, 1, S5_GROUPS_PER_BLOCK * n_state)
            dskip = s5_d_full[j:j + 1]
            y, z, xr, xi = _s5_fwd(u, bre, bim, cre, cim, are, aim, dskip, 's5_scan_fwd')
            behind = finish_gather('s5_w_out', j, [z])
            o = _mm('s5_out', 'nn', _plain(z), _w_cols(g_s5_out, j), (seq, 2 * dm, s5w),
                    [((seq, 2 * dm), F32, _plain_shape(None))],
                    (tm, _tile(s5_w_out.shape[2], 1024), s5_w_out.shape[1]), after=behind)[0]
            h = _unpermute(_glu_fwd(o, hp, 's5_glu_fwd'))
            rec.update(hn=hn, u=u, y=y, z=z, xr=xr, xi=xi, o=o, mats=(bre, bim, cre, cim, are, aim, dskip),
                       disc_vjp=disc_vjp)
        rec['h_mid'] = h
        hn2 = _norm_fwd(h, mlp_norm[i:i + 1], 'mlp_norm_fwd')
        behind = finish_gather('mlp_w_up', i, [hn2])
        act, act2 = _mm('mlp_up', 'nn', _plain(hn2), _w_cols(g_up, i), (seq, hid, dm),
                        [((seq, hid), BF16, _plain_shape(None))] * 2,
                        (tm, _tile(mlp_w_up.shape[2], 1024), mlp_w_up.shape[1]),
                        epilogue=lambda acc: (jnp.maximum(acc, 0.0), jnp.square(jnp.maximum(acc, 0.0))),
                        after=behind)
        behind = finish_gather('mlp_w_down', i, [act2])
        h = _mm('mlp_down', 'nn', _plain(act2), _w_rows(g_down, i), (seq, dm, hid),
                [((seq, dm), F32, _plain_shape(None))],
                (tm, _tile(dm, 1024), mlp_w_down.shape[1]),
                epilogue=lambda acc, hv: (acc + hv,), extras=[_plain(h)], after=behind)[0]
        rec.update(hn2=hn2, act=act, act2=act2)
        saved.append(rec)

    dh, loss_cols, d_final = _loss_head(h, target, final_norm.reshape(1, dm), 'loss_head')
    loss = lax.psum(jnp.sum(loss_cols), ('x', 'y', 'c'))
    grads['final_norm'] = [d_final.reshape(dm)]

    big_grads = {n: [None] * weights[n].shape[0] for n in big}
    reducing = []

    def reduce_step(item, after):
        n, l, hd = item['n'], item['l'], item['hd']
        if item['stage'] == 'pair':
            dw, got = _wait(hd, f'rs_{n}_{l}_pair_wait', after)
            _, _, rows_h, cols = dw.shape
            pre = _sum_pair(dw.reshape(2, 4 * rows_h, cols), got.reshape(4 * rows_h, cols), ic, BF16,
                            f'rs_{n}_pairsum').reshape(4, rows_h, cols)
            item.update(stage='chips', hd=_start('chips_a2a', pre, pre.shape, f'rs_{n}_{l}_start', []))
        elif item['stage'] == 'chips':
            pre, yb = _wait(hd, f'rs_{n}_{l}_wait', after)
            yb = lax.dynamic_update_index_in_dim(yb, lax.dynamic_index_in_dim(pre, chip, 0, keepdims=False), chip, 0)
            fin = _sum_slots(yb, F32, f'rs_{n}_chipsum')
            item.update(stage='back', hd=_start('pair_bcast', fin, (2,) + fin.shape, f'rs_{n}_{l}_back_start', []))
        else:
            fin, both = _wait(hd, f'rs_{n}_{l}_back_wait', after)
            both = lax.dynamic_update_index_in_dim(both, fin, ic, 0)
            big_grads[n][l] = both.reshape(2 * fin.shape[0], fin.shape[1])
            item.update(stage='done', hd=None)
            return after
        return [item['hd']['token']]

    def reduce_scatter(dw, n, l):
        hd = _start('pair_swap', dw, dw.shape[1:], f'rs_{n}_{l}_pair_start', [])
        reducing.append(dict(n=n, l=l, stage='pair', hd=hd))
        behind = [hd['token']]
        if len(reducing) >= 2:
            behind = reduce_step(reducing[-2], behind)
        if len(reducing) >= 4:
            behind = reduce_step(reducing[-4], behind)
        return behind
    for i in reversed(range(depth)):
        j = i // 2
        rec = saved[i]
        r_dn, c_dn = mlp_w_down.shape[1:]
        shape, spec = _dw_rows(r_dn, c_dn)
        dw = _mm('mlp_down_dw', 'tn', _plain(rec['act2']), _plain(dh), (hid, dm, seq),
                 [(shape, BF16, spec)], (_tile(r_dn // 2, 1024), _tile(c_dn, 1024), seq))[0]
        behind = reduce_scatter(dw, 'mlp_w_down', i)
        dpre = _mm('mlp_down_dx', 'nt', _plain(dh), _w_rows(g_down, i), (seq, hid, dm),
                   [((seq, hid), BF16, _plain_shape(None))],
                   (tm, _tile(rh(mlp_w_down), 1024), dm),
                   epilogue=lambda acc, av: (acc * (2.0 * av.astype(F32)),), extras=[_plain(rec['act'])],
                   after=behind)[0]
        r_up, c_up = mlp_w_up.shape[1:]
        shape, spec = _dw_cols(r_up, c_up)
        dw = _mm('mlp_up_dw', 'tn', _plain(rec['hn2']), _plain(dpre), (dm, hid, seq),
                 [(shape, BF16, spec)], (_tile(r_up // 2, 1024), _tile(c_up, 1024), seq))[0]
        behind = reduce_scatter(dw, 'mlp_w_up', i)
        dhn = _mm('mlp_up_dx', 'nt', _plain(dpre), _w_cols(g_up, i), (seq, dm, hid),
                  [((seq, dm), F32, _plain_shape(None))],
                  (tm, _tile(rh(mlp_w_up), 1024), _tile(mlp_w_up.shape[2], 2048)), after=behind)[0]
        dh, dg = _norm_bwd(rec['h_mid'], dhn, dh, mlp_norm[i:i + 1], 'mlp_norm_bwd')
        grads['mlp_norm'][i] = dg[0]

        if i % 2 == 0:
            r_o, c_o = gla_w_out.shape[1:]
            shape, spec = _dw_rows(r_o, c_o)
            dw = _mm('gla_out_dw', 'tn', _plain(rec['gated']), _plain(dh), (vw, dm, seq),
                     [(shape, BF16, spec)], (_tile(r_o // 2, 1024), _tile(c_o, 1024), seq))[0]
            behind = reduce_scatter(dw, 'gla_w_out', j)
            dgated = _mm('gla_out_dx', 'nt', _plain(dh), _w_rows(g_gla_out, j), (seq, vw, dm),
                         [((seq, vw), F32, _plain_shape(None))],
                         (tm, _tile(rh(gla_w_out), 1024), dm), after=behind)[0]
            dq, dkk, dvv, dr, dpre_g, db, don = _gla_scan_bwd(
                rec['proj'], rec['wgu_pad'], gla_b_gate[j:j + 1], gla_o_norm[j:j + 1], rec['states'], dgated,
                heads, kw, vw, tb, 'gla_scan_bwd')
            grads['gla_b_gate'][j] = db[0]
            grads['gla_o_norm'][j] = don[0]
            dgl = _mm('gla_gate_dx', 'nt', _plain(dpre_g), _plain(rec['wgu_pad']), (seq, LANES, kw),
                      [((seq, LANES), BF16, _plain_shape(None))], (tm, LANES, kw))[0]
            g_low = rec['proj'][:, pw - LANES:]
            dwgu = _mm('gla_gate_dw', 'tn', _plain(g_low), _plain(dpre_g), (LANES, kw, seq),
                       [((LANES, kw), F32, _plain_shape(None))], (LANES, kw, seq))[0]
            grads['gla_w_gate_up'][j] = dwgu[:rank]
            dproj = jnp.concatenate([dq, dkk, dvv, dr, dgl], axis=1)
            dw_pad = _mm('gla_proj_dw', 'tn', _plain(rec['hn']), _plain(dproj), (dm, pw, seq),
                         [((dm, pw), BF16, _plain_shape(None))], (_tile(dm, 1024), _tile(pw, 1024), seq))[0]
            shard_w = in_w // 4
            dw = dw_pad[:, :in_w].reshape(2, dm // 2, 4, shard_w).transpose(0, 2, 1, 3)
            behind = reduce_scatter(dw, 'gla_w_in', j)
            dhn = _mm('gla_proj_dx', 'nt', _plain(dproj), _plain(rec['w_in_pad']), (seq, dm, pw),
                      [((seq, dm), F32, _plain_shape(None))], (tm, _tile(dm, 1024), _tile(pw, 1024)),
                      after=behind)[0]
            dh, dg = _norm_bwd(rec['h_in'], dhn, dh, gla_norm[j:j + 1], 'gla_norm_bwd')
            grads['gla_norm'][j] = dg[0]
        else:
            dhp = _permute(dh)
            do = _glu_bwd(rec['o'], dhp, 's5_glu_bwd')
            r_o, c_o = s5_w_out.shape[1:]
            shape, spec = _dw_cols(r_o, c_o)
            dw = _mm('s5_out_dw', 'tn', _plain(rec['z']), _plain(do), (s5w, 2 * dm, seq),
                     [(shape, BF16, spec)], (_tile(r_o // 2, 1024), _tile(c_o, 1024), seq))[0]
            behind = reduce_scatter(dw, 's5_w_out', j)
            dz = _mm('s5_out_dx', 'nt', _plain(do), _w_cols(g_s5_out, j), (seq, s5w, 2 * dm),
                     [((seq, s5w), F32, _plain_shape(None))],
                     (tm, _tile(rh(s5_w_out), 1024), _tile(s5_w_out.shape[2], 1024)), after=behind)[0]
            bre, bim, cre, cim, are, aim, dskip = rec['mats']
            du, dcr, dci, dbr, dbi, dar, dai, dd = _s5_bwd(dz, rec['y'], rec['u'], rec['xr'], rec['xi'],
                                                           bre, bim, cre, cim, are, aim, dskip, 's5_scan_bwd')
            grads['s5_c_re'][j] = _from_blocks(dcr, grp, n_state)
            grads['s5_c_im'][j] = _from_blocks(dci, grp, n_state)
            dbb_re = _from_blocks(dbr, grp, n_state).transpose(0, 2, 1)
            dbb_im = _from_blocks(dbi, grp, n_state).transpose(0, 2, 1)
            d_lr, d_li, d_dt, d_bre, d_bim = rec['disc_vjp'](
                (dar.reshape(n_grp, n_state), dai.reshape(n_grp, n_state), dbb_re, dbb_im))
            grads['s5_lam_re'][j] = d_lr
            grads['s5_lam_im'][j] = d_li
            grads['s5_log_dt'][j] = d_dt
            grads['s5_b_re'][j] = d_bre
            grads['s5_b_im'][j] = d_bim
            grads['s5_d'][j] = dd[0]
            r_i, c_i = s5_w_in.shape[1:]
            shape, spec = _dw_rows(r_i, c_i)
            dw = _mm('s5_in_dw', 'tn', _plain(rec['hn']), _plain(du), (dm, s5w, seq),
                     [(shape, BF16, spec)], (_tile(r_i // 2, 1024), _tile(c_i, 1024), seq))[0]
            behind = reduce_scatter(dw, 's5_w_in', j)
            dhn = _mm('s5_in_dx', 'nt', _plain(du), _w_rows(g_s5_in, j), (seq, dm, s5w),
                      [((seq, dm), F32, _plain_shape(None))],
                      (tm, _tile(rh(s5_w_in), 1024), _tile(s5w, 1024)), after=behind)[0]
            dhp, dg = _norm_bwd(rec['h_in'], dhn, dhp, s5_norm_full[j:j + 1], 's5_norm_bwd')
            dh = _unpermute(dhp)
            grads['s5_norm'][j] = dg[0]
    grad_x = dh[None]
    behind = [dh]
    for stage, items in (('pair', reducing), ('chips', reducing[:-1]), ('back', reducing[:-1])):
        for item in items:
            if item['stage'] == stage:
                behind = reduce_step(item, behind)

    local_small = [jnp.stack(grads[n]) if n != 'final_norm' else grads[n][0] for n in small]
    full_shapes = [a.shape for a in local_small]
    packed_small = _pack(local_small)
    ar_small = _xy_start(packed_small, 'bcast', 'ar_small_start', [])
    out_g, out_d, out_m, out_v = {}, {}, {}, {}
    behind = [ar_small['token']]
    last_n = reducing[-1]['n']
    for n in [m for m in big if m != last_n] + [last_n]:
        if n == last_n:
            while reducing[-1]['stage'] != 'done':
                behind = reduce_step(reducing[-1], behind)
        if weights[n].shape[2] % LANES:
            to_t, from_t = (lambda a: a.transpose(2, 0, 1)), (lambda a: a.transpose(1, 2, 0))
            g_t = to_t(jnp.stack(big_grads[n]))
            res = _adamw_t(to_t(weights[n]), to_t(mom1[n]), to_t(mom2[n]), g_t, 'adamw_' + n, behind)
            out_g[n], out_d[n], out_m[n], out_v[n] = (from_t(a) for a in (g_t,) + tuple(res))
            behind = [res[0]]
        else:
            out_g[n], out_d[n], out_m[n], out_v[n] = _adamw(weights[n], mom1[n], mom2[n], big_grads[n],
                                                            'adamw_' + n, behind)
            behind = [out_d[n]]
    sent, by_chip = _xy_wait(ar_small, 'ar_small_wait', behind)
    by_chip = lax.dynamic_update_index_in_dim(by_chip, sent, chip, 0)
    gathered = _exchange(by_chip, 'c', 'bcast', 'ar_small_c')
    rows = gathered.shape[2]
    summed = _sum_slots(gathered.reshape(8, rows, LANES), F32, 'ar_small_sum')
    small_full = dict(zip(small, _unpack(summed, full_shapes)))
    small_grad = {}
    for n in small:
        g = small_full[n]
        if g.shape != weights[n].shape:
            ax = [a for a in range(g.ndim) if g.shape[a] != weights[n].shape[a]][0]
            g = lax.dynamic_slice_in_dim(g, chip * weights[n].shape[ax], weights[n].shape[ax], axis=ax)
        small_grad[n] = g

    shapes = [weights[n].shape for n in small]
    pw_, pm_, pv_, pg_ = (_pack([d[n] for n in small]) for d in (weights, mom1, mom2, small_grad))
    _, sd, sm, sv = _adamw(pw_[None], pm_[None], pv_[None], [pg_], 'adamw_small')
    for n, d_, m_, v_ in zip(small, _unpack(sd[0], shapes), _unpack(sm[0], shapes), _unpack(sv[0], shapes)):
        out_g[n], out_d[n], out_m[n], out_v[n] = small_grad[n], d_, m_, v_

    return (loss, grad_x, *[out_g[n] for n in names], *[out_d[n] for n in names],
            *[out_m[n] for n in names], *[out_v[n] for n in names])
```

```python
import functools
import math

import jax
import jax.numpy as jnp
from jax import lax
from jax.experimental import pallas as pl
from jax.experimental.pallas import tpu as pltpu

F32 = jnp.float32
BF16 = jnp.bfloat16

EPS = 1e-6
CHUNK = 64
GLA_GATE_TEMP = 16.0
S5_EIG_CLIP = -1e-4
S5_SEGMENTS = 8
S5_GROUPS_PER_BLOCK = 8
LANES = 128
ADAM_LR = 0.001
ADAM_B1 = 0.9
ADAM_B2 = 0.999
ADAM_EPS = 1e-08
ADAM_WD = 0.01
ADAM_STEP = 10
VMEM_LIMIT_BYTES = 56 * 1024 * 1024
PAIR_PIECE_BYTES = 2 * 1024 * 1024
GATHERS_IN_FLIGHT = 2
PAIR_VMEM_BYTES = 40 * 1024 * 1024

MESH = pl.DeviceIdType.MESH
ANY = pl.BlockSpec(memory_space=pl.ANY)
IN_VMEM = pl.BlockSpec(memory_space=pltpu.VMEM)
IN_HBM = pl.BlockSpec(memory_space=pltpu.HBM)
IN_SEM = pl.BlockSpec(memory_space=pltpu.SEMAPHORE)
DATAFLOW = pltpu.SideEffectType.DATAFLOW_SIDE_EFFECTING


def _pcall(body, **kw):
    return pl.pallas_call(body, **kw)


def _params(*sem):
    return pltpu.CompilerParams(dimension_semantics=sem, vmem_limit_bytes=VMEM_LIMIT_BYTES)


def _tile(dim, target, unit=LANES):
    if dim <= target:
        return dim
    best = None
    for t in range(unit, target + 1, unit):
        if dim % t == 0:
            best = t
    assert best is not None, (dim, target)
    return best


def _exchange(x, group, mode, name):
    n = 2 if group == 'c' else 4
    blk = x.shape if mode == 'bcast' else x.shape[1:]
    if mode == 'a2a':
        assert x.shape[0] == n
    flips = [(0, 0, 1)] if group == 'c' else [(1, 0, 0), (0, 1, 0), (1, 1, 0)]
    itemsize = jnp.dtype(x.dtype).itemsize
    staged = group == 'c' and (x.size + n * math.prod(blk)) * itemsize <= PAIR_VMEM_BYTES
    if group == 'c' and not staged:
        ic = lax.axis_index('c')
        own = x if mode == 'bcast' else lax.dynamic_index_in_dim(x, ic, 0, keepdims=False)
        return lax.dynamic_update_index_in_dim(_pair_exchange_chunked(x, mode, name), own, ic, 0)

    def body(x_ref, y_ref, send_sems, recv_sems, local_sem):
        ix, iy, ic = lax.axis_index('x'), lax.axis_index('y'), lax.axis_index('c')

        def slot(px, py, pc):
            return pc if group == 'c' else 2 * px + py

        def src(px, py, pc):
            if mode == 'a2a':
                return x_ref.at[slot(px, py, pc)]
            if mode == 'bcast_c':
                return x_ref.at[ic]
            return x_ref

        me = (ix, iy, ic)
        local = pltpu.make_async_copy(src(*me), y_ref.at[slot(*me)], local_sem)
        local.start()
        peers = []
        for fx, fy, fc in flips:
            peers.append((1 - ix if fx else ix, 1 - iy if fy else iy, 1 - ic if fc else ic))
        sends = []
        for k, peer in enumerate(peers):
            cp = pltpu.make_async_remote_copy(
                src_ref=src(*peer), dst_ref=y_ref.at[slot(*me)],
                send_sem=send_sems.at[k], recv_sem=recv_sems.at[k],
                device_id=peer, device_id_type=MESH)
            cp.start()
            sends.append(cp)
        for k, peer in enumerate(peers):
            pltpu.make_async_remote_copy(
                src_ref=src(*peer), dst_ref=y_ref.at[slot(*peer)],
                send_sem=send_sems.at[k], recv_sem=recv_sems.at[k],
                device_id=peer, device_id_type=MESH).wait_recv()
        for cp in sends:
            cp.wait_send()
        local.wait()

    return _pcall(
        body, name=name,
        out_shape=jax.ShapeDtypeStruct((n,) + tuple(blk), x.dtype),
        in_specs=[IN_VMEM if staged else ANY], out_specs=IN_VMEM if staged else ANY,
        scratch_shapes=[pltpu.SemaphoreType.DMA((len(flips),)),
                        pltpu.SemaphoreType.DMA((len(flips),)),
                        pltpu.SemaphoreType.DMA(())],
        compiler_params=pltpu.CompilerParams(vmem_limit_bytes=VMEM_LIMIT_BYTES),
    )(x)


def _split_axis(blk, dtype, piece_bytes):
    itemsize = jnp.dtype(dtype).itemsize
    sublanes = 8 * 4 // itemsize
    want = max(1, math.prod(blk) * itemsize // piece_bytes)
    for pieces in [s for s in (64, 32, 16, 8, 4, 2) if s <= want]:
        for ax in range(len(blk) - 1):
            unit = sublanes if ax == len(blk) - 2 else 1
            if blk[ax] % (pieces * unit) == 0:
                return ax, pieces
    return 0, 1


def _pair_exchange_chunked(x, mode, name, reduce=False):
    assert mode == 'a2a' or not reduce
    blk = x.shape if mode == 'bcast' else x.shape[1:]
    ax, pieces = _split_axis(blk, x.dtype, PAIR_PIECE_BYTES)
    step = blk[ax] // pieces
    piece_shape = tuple(blk[:ax]) + (step,) + tuple(blk[ax + 1:])

    def piece(ref, p):
        return ref.at[(slice(None),) * ax + (pl.ds(p * step, step),)]

    def body(x_ref, y_ref, out_buf, in_buf, own_buf, send_sems, recv_sems, stage_sems, drain_sems, own_sems, credit_sem):
        ix, iy, ic = lax.axis_index('x'), lax.axis_index('y'), lax.axis_index('c')
        sibling = (ix, iy, 1 - ic)
        mine = x_ref.at[ic] if mode != 'bcast' else x_ref
        theirs = x_ref.at[1 - ic] if mode == 'a2a' else mine

        def own(p):
            return pltpu.make_async_copy(piece(mine, p), own_buf.at[p % 2], own_sems.at[p % 2])

        def stage(p):
            return pltpu.make_async_copy(piece(theirs, p), out_buf.at[p % 2], stage_sems.at[p % 2])

        def remote(p):
            return pltpu.make_async_remote_copy(
                src_ref=out_buf.at[p % 2], dst_ref=in_buf.at[p % 2],
                send_sem=send_sems.at[p], recv_sem=recv_sems.at[p],
                device_id=sibling, device_id_type=MESH)

        def drain(p):
            dst = y_ref if reduce else y_ref.at[1 - ic]
            return pltpu.make_async_copy(in_buf.at[p % 2], piece(dst, p), drain_sems.at[p % 2])

        stage(0).start()
        if reduce:
            own(0).start()
        for p in range(pieces):
            stage(p).wait()
            if p >= 2:
                pl.semaphore_wait(credit_sem, 1)
            remote(p).start()
            if p + 1 < pieces:
                if p >= 1:
                    remote(p - 1).wait_send()
                stage(p + 1).start()
                if reduce:
                    own(p + 1).start()
            remote(p).wait_recv()
            if reduce:
                own(p).wait()
                in_buf[p % 2] = (in_buf[p % 2].astype(F32) + own_buf[p % 2].astype(F32)).astype(in_buf.dtype)
            drain(p).start()
            drain(p).wait()
            if p + 2 < pieces:
                pl.semaphore_signal(credit_sem, inc=1, device_id=sibling, device_id_type=MESH)
        for p in range(max(0, pieces - 2), pieces):
            remote(p).wait_send()

    return _pcall(
        body, name=name,
        out_shape=jax.ShapeDtypeStruct(tuple(blk) if reduce else (2,) + tuple(blk), x.dtype),
        in_specs=[ANY], out_specs=ANY,
        scratch_shapes=[pltpu.VMEM((2,) + piece_shape, x.dtype), pltpu.VMEM((2,) + piece_shape, x.dtype),
                        pltpu.VMEM((2,) + piece_shape if reduce else (2, 8, LANES), x.dtype),
                        pltpu.SemaphoreType.DMA((pieces,)), pltpu.SemaphoreType.DMA((pieces,)),
                        pltpu.SemaphoreType.DMA((2,)), pltpu.SemaphoreType.DMA((2,)), pltpu.SemaphoreType.DMA((2,)),
                        pltpu.SemaphoreType.REGULAR],
        compiler_params=pltpu.CompilerParams(vmem_limit_bytes=VMEM_LIMIT_BYTES),
    )(x)


_FLIPS = {'xy': [(1, 0, 0), (0, 1, 0), (1, 1, 0)], 'c': [(0, 0, 1)]}


def _split_copies(group, mode, x_ref, land_ref, sems):
    ix, iy, ic = lax.axis_index('x'), lax.axis_index('y'), lax.axis_index('c')
    slot = (lambda px, py, pc: pc) if group == 'c' else (lambda px, py, pc: 2 * px + py)
    n_peers = len(_FLIPS[group])
    out = []
    for k, (fx, fy, fc) in enumerate(_FLIPS[group]):
        peer = (1 - ix if fx else ix, 1 - iy if fy else iy, 1 - ic if fc else ic)
        if mode == 'a2a':
            src = x_ref.at[slot(*peer)]
        elif mode == 'bcast_c':
            src = x_ref.at[ic]
        else:
            src = x_ref
        mk = lambda dst, src=src, k=k, peer=peer: pltpu.make_async_remote_copy(
            src_ref=src, dst_ref=dst, send_sem=sems[k], recv_sem=sems[n_peers + k],
            device_id=peer, device_id_type=MESH)
        out.append((mk(land_ref.at[slot(ix, iy, ic)]), mk(land_ref.at[slot(*peer)])))
    return out


def _xy_start(x, mode, name, after, group='xy'):
    blk = x.shape if mode == 'bcast' else x.shape[1:]
    land_shape = (2 if group == 'c' else 4,) + tuple(blk)
    n_after = len(after)
    n_sems = 2 * len(_FLIPS[group])

    def body(*refs):
        x_ref, land_ref = refs[0], refs[1]
        sems = refs[2 + n_after:2 + n_sems + n_after]
        for send, _ in _split_copies(group, mode, x_ref, land_ref, sems):
            send.start()
        refs[-1][...] = jnp.zeros_like(refs[-1])

    outs = _pcall(
        body, name=name,
        out_shape=(pltpu.SemaphoreType.DMA(()),) * n_sems
        + (pltpu.HBM(x.shape, x.dtype), pltpu.HBM(land_shape, x.dtype), jax.ShapeDtypeStruct((8, LANES), F32)),
        in_specs=(IN_HBM, IN_HBM) + (ANY,) * n_after,
        out_specs=(IN_SEM,) * n_sems + (IN_HBM, IN_HBM, IN_VMEM),
        input_output_aliases={0: n_sems, 1: n_sems + 1},
        compiler_params=pltpu.CompilerParams(has_side_effects=DATAFLOW),
    )(pltpu.with_memory_space_constraint(x, pltpu.HBM),
      pltpu.with_memory_space_constraint(lax.empty(land_shape, x.dtype), pltpu.HBM), *after)
    return dict(sems=outs[:n_sems], sent=outs[n_sems], land=outs[n_sems + 1], token=outs[n_sems + 2], mode=mode,
                group=group)


def _xy_wait(handle, name, after):
    mode, group = handle['mode'], handle['group']
    n_after = len(after)
    n_sems = len(handle['sems'])

    def body(*refs):
        x_ref, land_ref = refs[0], refs[1]
        for _, recv in _split_copies(group, mode, x_ref, land_ref, refs[2:2 + n_sems]):
            recv.wait_send()
            recv.wait_recv()

    sent, land = handle['sent'], handle['land']
    return _pcall(
        body, name=name,
        out_shape=(pltpu.HBM(sent.shape, sent.dtype), pltpu.HBM(land.shape, land.dtype)),
        in_specs=(IN_HBM, IN_HBM) + (IN_SEM,) * n_sems + (ANY,) * n_after,
        out_specs=(IN_HBM, IN_HBM),
        input_output_aliases={0: 0, 1: 1},
        compiler_params=pltpu.CompilerParams(has_side_effects=DATAFLOW),
    )(sent, land, *handle['sems'], *after)


_KIND_GROUP = {'chips_a2a': 'xy', 'chips_gather': 'xy', 'pair_swap': 'c', 'pair_bcast': 'c', 'pair_inplace': 'c'}


def _plan(kind, x_ref, land_ref, sems):
    ix, iy, ic = lax.axis_index('x'), lax.axis_index('y'), lax.axis_index('c')
    flips = _FLIPS[_KIND_GROUP[kind]]
    me_chip = 2 * ix + iy
    out = []
    for k, (fx, fy, fc) in enumerate(flips):
        peer = (1 - ix if fx else ix, 1 - iy if fy else iy, 1 - ic if fc else ic)
        peer_chip = 2 * peer[0] + peer[1]
        if kind == 'chips_a2a':
            src, dst, got = x_ref.at[peer_chip], land_ref.at[me_chip], land_ref.at[peer_chip]
        elif kind == 'chips_gather':
            src, dst, got = x_ref.at[ic], land_ref.at[ic, me_chip], land_ref.at[ic, peer_chip]
        elif kind == 'pair_swap':
            src, dst, got = x_ref.at[1 - ic], land_ref, land_ref
        elif kind == 'pair_bcast':
            src, dst, got = x_ref, land_ref.at[ic], land_ref.at[1 - ic]
        else:
            src, dst, got = land_ref.at[ic], land_ref.at[ic], land_ref.at[1 - ic]
        mk = lambda d, src=src, k=k, peer=peer: pltpu.make_async_remote_copy(
            src_ref=src, dst_ref=d, send_sem=sems[k], recv_sem=sems[len(flips) + k],
            device_id=peer, device_id_type=MESH)
        out.append((mk(dst), mk(got)))
    return out


def _start_many(jobs, name, after):
    per_job = []
    for kind, x, land in jobs:
        if not hasattr(land, 'dtype'):
            land = lax.empty(tuple(land), x.dtype)
        per_job.append((kind, ([] if x is None else [x]) + [land], 2 * len(_FLIPS[_KIND_GROUP[kind]])))
    arrays = [a for _, arrs, _ in per_job for a in arrs]
    n_arr, n_after = len(arrays), len(after)
    n_sems = sum(ns for _, _, ns in per_job)

    def body(*refs):
        a0, s0 = 0, n_arr + n_after
        for kind, arrs, ns in per_job:
            x_ref = refs[a0] if len(arrs) == 2 else None
            for send, _ in _plan(kind, x_ref, refs[a0 + len(arrs) - 1], refs[s0:s0 + ns]):
                send.start()
            a0, s0 = a0 + len(arrs), s0 + ns
        refs[-1][...] = jnp.zeros_like(refs[-1])

    outs = _pcall(
        body, name=name,
        out_shape=(pltpu.SemaphoreType.DMA(()),) * n_sems + tuple(pltpu.HBM(a.shape, a.dtype) for a in arrays)
        + (jax.ShapeDtypeStruct((8, LANES), F32),),
        in_specs=(IN_HBM,) * n_arr + (ANY,) * n_after,
        out_specs=(IN_SEM,) * n_sems + (IN_HBM,) * n_arr + (IN_VMEM,),
        input_output_aliases={i: n_sems + i for i in range(n_arr)},
        compiler_params=pltpu.CompilerParams(has_side_effects=DATAFLOW),
    )(*[pltpu.with_memory_space_constraint(a, pltpu.HBM) for a in arrays], *after)
    handles, a0, s0 = [], n_sems, 0
    for kind, arrs, ns in per_job:
        handles.append(dict(kind=kind, sems=outs[s0:s0 + ns], arrays=outs[a0:a0 + len(arrs)], token=outs[-1]))
        a0, s0 = a0 + len(arrs), s0 + ns
    return handles


def _start(kind, x, land, name, after):
    return _start_many([(kind, x, land)], name, after)[0]


def _wait(handle, name, after):
    kind, arrays, sems = handle['kind'], handle['arrays'], handle['sems']
    n_arr, n_sems = len(arrays), len(sems)

    def body(*refs):
        for _, got in _plan(kind, refs[0] if n_arr == 2 else None, refs[n_arr - 1], refs[n_arr:n_arr + n_sems]):
            got.wait_send()
            got.wait_recv()

    return _pcall(
        body, name=name,
        out_shape=tuple(pltpu.HBM(a.shape, a.dtype) for a in arrays),
        in_specs=(IN_HBM,) * n_arr + (IN_SEM,) * n_sems + (ANY,) * len(after),
        out_specs=(IN_HBM,) * n_arr,
        input_output_aliases={i: i for i in range(n_arr)},
        compiler_params=pltpu.CompilerParams(has_side_effects=DATAFLOW),
    )(*arrays, *sems, *after)


def _sum_pair(x, recv, ic, out_dtype, name):
    _, rows, cols = x.shape
    tm = _tile(rows, max(8, (1 << 20) // cols // 8 * 8), unit=8)

    def body(c_ref, x_ref, r_ref, o_ref):
        o_ref[...] = (x_ref[...].astype(F32) + r_ref[...].astype(F32)).astype(o_ref.dtype)

    return _pcall(
        body, name=name,
        grid_spec=pltpu.PrefetchScalarGridSpec(
            num_scalar_prefetch=1, grid=(rows // tm,),
            in_specs=[pl.BlockSpec((None, tm, cols), lambda i, c: (c[0], i, 0)),
                      pl.BlockSpec((tm, cols), lambda i, c: (i, 0))],
            out_specs=pl.BlockSpec((tm, cols), lambda i, c: (i, 0))),
        out_shape=jax.ShapeDtypeStruct((rows, cols), out_dtype),
        compiler_params=_params('parallel'),
    )(jnp.reshape(ic, (1,)).astype(jnp.int32), x, recv)


def _sum_slots(y, out_dtype, name):
    n, rows, cols = y.shape
    tm = _tile(rows, max(8, (1 << 20) // (n * cols) // 8 * 8), unit=8)

    def body(y_ref, o_ref):
        acc = y_ref[0].astype(F32)
        for k in range(1, n):
            acc = acc + y_ref[k].astype(F32)
        o_ref[...] = acc.astype(o_ref.dtype)

    return _pcall(
        body, name=name, grid=(rows // tm,),
        out_shape=jax.ShapeDtypeStruct((rows, cols), out_dtype),
        in_specs=[pl.BlockSpec((n, tm, cols), lambda i: (0, i, 0))],
        out_specs=pl.BlockSpec((tm, cols), lambda i: (i, 0)),
        compiler_params=_params('parallel'),
    )(y)


class _Op:
    def __init__(self, arr, spec):
        self.arr = arr
        self.spec = spec


def _plain(arr):
    return _Op(arr, lambda t0, t1: ((t0, t1), lambda b0, b1: (b0, b1)))


def _plain_shape(shape):
    return lambda t0, t1: ((t0, t1), lambda b0, b1: (b0, b1))


def _dw_cols(rows, cols):
    rh = rows // 2

    def spec(t0, t1):
        assert (rh % t0 == 0 or t0 == rows) and cols % t1 == 0, (rh, cols, t0, t1)
        qr, qc = max(1, rh // t0), cols // t1
        if t0 == rows:
            return (2, None, rh, t1), lambda b0, b1: (0, b1 // qc, 0, b1 % qc)
        return (None, None, t0, t1), lambda b0, b1: (b0 // qr, b1 // qc, b0 % qr, b1 % qc)
    return (2, 4, rh, cols), spec


def _dw_rows(rows, cols):
    rh = rows // 2

    def spec(t0, t1):
        assert (rh % t0 == 0 or t0 == rows) and cols % t1 == 0, (rh, cols, t0, t1)
        qr = max(1, rh // t0)
        if t0 == rows:
            return (2, None, rh, t1), lambda b0, b1: (0, b0, 0, b1)
        return (None, None, t0, t1), lambda b0, b1: ((b0 // qr) % 2, b0 // (2 * qr), b0 % qr, b1)
    return (2, 4, rh, cols), spec


def _w_cols(g, j):
    _, _, rh, cols = g[j].shape
    return _Op(g[j], _dw_cols(2 * rh, cols)[1])


def _w_rows(g, j):
    _, _, rh, cols = g[j].shape
    return _Op(g[j], _dw_rows(2 * rh, cols)[1])


def _mm(name, mode, a, b, dims, outs, tiles, epilogue=None, extras=(), after=()):
    m, n, k = dims
    tm, tn, tk = tiles
    assert m % tm == 0 and n % tn == 0 and k % tk == 0, (name, dims, tiles)
    nk = k // tk
    if mode == 'nn':
        a_t, a_ix, b_t, b_ix, ca, cb = (tm, tk), (lambda i, j, kk: (i, kk)), (tk, tn), (lambda i, j, kk: (kk, j)), 1, 0
    elif mode == 'nt':
        a_t, a_ix, b_t, b_ix, ca, cb = (tm, tk), (lambda i, j, kk: (i, kk)), (tn, tk), (lambda i, j, kk: (j, kk)), 1, 1
    else:
        a_t, a_ix, b_t, b_ix, ca, cb = (tk, tm), (lambda i, j, kk: (kk, i)), (tk, tn), (lambda i, j, kk: (kk, j)), 0, 0
    a_blk, a_fn = a.spec(*a_t)
    b_blk, b_fn = b.spec(*b_t)
    in_specs = [pl.BlockSpec(a_blk, lambda i, j, kk: a_fn(*a_ix(i, j, kk))),
                pl.BlockSpec(b_blk, lambda i, j, kk: b_fn(*b_ix(i, j, kk)))]
    operands = [a.arr, b.arr]
    for e in extras:
        e_blk, e_fn = e.spec(tm, tn)
        in_specs.append(pl.BlockSpec(e_blk, functools.partial(lambda i, j, kk, f: f(i, j), f=e_fn)))
        operands.append(e.arr)
    out_shapes, out_specs = [], []
    for shape, dtype, spec in outs:
        o_blk, o_fn = spec(tm, tn)
        out_shapes.append(jax.ShapeDtypeStruct(shape, dtype))
        out_specs.append(pl.BlockSpec(o_blk, functools.partial(lambda i, j, kk, f: f(i, j), f=o_fn)))
    n_ex, n_out = len(extras), len(outs)
    in_specs += [ANY] * len(after)
    operands += list(after)
    if epilogue is None:
        epilogue = lambda acc: (acc,)

    def body(a_ref, b_ref, *rest):
        ex_refs = rest[:n_ex]
        rest = rest[:n_ex] + rest[n_ex + len(after):]
        out_refs = rest[n_ex:n_ex + n_out]
        bv = b_ref[...]
        if bv.ndim == 3:
            bv = bv.reshape(bv.shape[0] * bv.shape[1], bv.shape[2])
        p = lax.dot_general(a_ref[...].astype(BF16), bv.astype(BF16),
                            (((ca,), (cb,)), ((), ())), preferred_element_type=F32)

        def finish(acc):
            res = epilogue(acc, *[r[...] for r in ex_refs])
            for o_ref, val in zip(out_refs, res):
                o_ref[...] = val.astype(o_ref.dtype)

        if nk == 1:
            finish(p)
        else:
            acc_ref = rest[n_ex + n_out]
            kk = pl.program_id(2)

            @pl.when(kk == 0)
            def _():
                acc_ref[...] = p

            @pl.when(kk > 0)
            def _():
                acc_ref[...] += p

            @pl.when(kk == nk - 1)
            def _():
                finish(acc_ref[...])

    res = _pcall(
        body, name=name, grid=(m // tm, n // tn, nk),
        out_shape=out_shapes, in_specs=in_specs, out_specs=out_specs,
        scratch_shapes=[pltpu.VMEM((tm, tn), F32)] if nk > 1 else [],
        compiler_params=_params('parallel', 'parallel', 'arbitrary'),
    )(*operands)
    return res


def _rowwise(name, fn, row_ins, vec_ins, outs, reds, tm, after=()):
    rows = row_ins[0].shape[0]
    assert rows % tm == 0
    n_in = len(row_ins) + len(vec_ins)
    n_out = len(outs)

    def body(*refs):
        vals = [r[...] for r in refs[:n_in]]
        refs = refs[:n_in] + refs[n_in + len(after):]
        res = fn(*vals)
        for o_ref, val in zip(refs[n_in:n_in + n_out], res[:n_out]):
            o_ref[...] = val.astype(o_ref.dtype)
        first = pl.program_id(0) == 0
        for r_ref, val in zip(refs[n_in + n_out:], res[n_out:]):
            @pl.when(first)
            def _(r_ref=r_ref, val=val):
                r_ref[...] = val

            @pl.when(jnp.logical_not(first))
            def _(r_ref=r_ref, val=val):
                r_ref[...] += val

    in_specs = [pl.BlockSpec((tm, a.shape[1]), lambda i: (i, 0)) for a in row_ins]
    in_specs += [pl.BlockSpec((1, v.shape[1]), lambda i: (0, 0)) for v in vec_ins]
    in_specs += [ANY] * len(after)
    out_shapes = [jax.ShapeDtypeStruct((rows, w), dt) for w, dt in outs]
    out_shapes += [jax.ShapeDtypeStruct((1, w), F32) for w in reds]
    out_specs = [pl.BlockSpec((tm, w), lambda i: (i, 0)) for w, _ in outs]
    out_specs += [pl.BlockSpec((1, w), lambda i: (0, 0)) for w in reds]
    return _pcall(
        body, name=name, grid=(rows // tm,),
        out_shape=out_shapes, in_specs=in_specs, out_specs=out_specs,
        compiler_params=_params('arbitrary'),
    )(*row_ins, *vec_ins, *after)


def _norm_fwd(h, g, name, after=()):
    def fn(hv, gv):
        rstd = lax.rsqrt(jnp.mean(hv * hv, axis=-1, keepdims=True) + EPS)
        return (hv * rstd * gv,)
    return _rowwise(name, fn, [h], [g], [(h.shape[1], BF16)], [], 256, after)[0]


def _norm_bwd(h, dhn, dres, g, name):
    def fn(hv, dv, rv, gv):
        rstd = lax.rsqrt(jnp.mean(hv * hv, axis=-1, keepdims=True) + EPS)
        xhat = hv * rstd
        dxhat = dv * gv
        dh = rv + rstd * (dxhat - xhat * jnp.mean(dxhat * xhat, axis=-1, keepdims=True))
        return dh, jnp.sum(dv * xhat, axis=0, keepdims=True)
    w = h.shape[1]
    return _rowwise(name, fn, [h, dhn, dres], [g], [(w, F32)], [w], 256)


def _loss_head(h, target, g, name):
    w = h.shape[1]

    def fn(hv, tv, gv):
        rstd = lax.rsqrt(jnp.mean(hv * hv, axis=-1, keepdims=True) + EPS)
        xhat = hv * rstd
        diff = xhat * gv - tv
        dy = diff * (1.0 / w)
        dxhat = dy * gv
        dh = rstd * (dxhat - xhat * jnp.mean(dxhat * xhat, axis=-1, keepdims=True))
        return (dh, jnp.sum(0.5 * dy * diff, axis=0, keepdims=True),
                jnp.sum(dy * xhat, axis=0, keepdims=True))
    return _rowwise(name, fn, [h, target], [g], [(w, F32)], [w, w], 256)


def _split3(x):
    hi = x.astype(BF16)
    r1 = x - hi.astype(F32)
    mid = r1.astype(BF16)
    lo = (r1 - mid.astype(F32)).astype(BF16)
    return hi, mid, lo


def _tri_dot(tri, x):
    hi, mid, lo = _split3(x)
    d = lambda p: jnp.dot(tri, p, preferred_element_type=F32)
    return d(hi) + d(mid) + d(lo)


def _log_sigmoid(x):
    return jnp.minimum(x, 0.0) - jnp.log(1.0 + jnp.exp(-jnp.abs(x)))


def _gla_dims(proj_w, kw, vw, dk, dv):
    assert kw % dk == 0 and (2 * kw) % dv == 0 and (2 * kw + vw) % dv == 0 and (2 * kw + 2 * vw) % LANES == 0
    return dict(q0=0, k0=kw // dk, v0=2 * kw // dv, r0=(2 * kw + vw) // dv, g0=(2 * kw + 2 * vw) // LANES)


def _gla_gates(gl, wgu, bias):
    pre = jnp.dot(gl.astype(BF16), wgu, preferred_element_type=F32) + bias
    la = _log_sigmoid(pre) * (1.0 / GLA_GATE_TEMP)
    r_i = lax.broadcasted_iota(jnp.int32, (CHUNK, CHUNK), 0)
    c_i = lax.broadcasted_iota(jnp.int32, (CHUNK, CHUNK), 1)
    cum = _tri_dot((c_i <= r_i).astype(BF16), la)
    total = cum[CHUNK - 1:CHUNK, :]
    return pre, cum, total


def _gla_scan_fwd(proj, wgu_pad, b_gate, o_norm, heads, kw, vw, tb, name):
    seq, pw = proj.shape
    dk, dv = kw // heads, vw // heads
    cb = tb // CHUNK
    nt = seq // tb
    o = _gla_dims(pw, kw, vw, dk, dv)
    scale = dk ** -0.5

    def body(q_ref, k_ref, v_ref, r_ref, gl_ref, wgu_ref, b_ref, on_ref, out_ref, st_ref, s_scr):
        @pl.when(pl.program_id(1) == 0)
        def _():
            s_scr[...] = jnp.zeros_like(s_scr)

        wgu = wgu_ref[...].astype(BF16)
        bias = b_ref[...]
        onorm = on_ref[...]
        st = s_scr[...]
        for ci in range(cb):
            rows = pl.ds(ci * CHUNK, CHUNK)
            _, cum, total = _gla_gates(gl_ref[rows, :], wgu, bias)
            kdec = k_ref[rows, :] * jnp.exp(total - cum)
            st = st * jnp.exp(total) + lax.dot_general(
                v_ref[rows, :].astype(BF16), kdec.astype(BF16), (((0,), (0,)), ((), ())),
                preferred_element_type=F32)
            st_ref[ci] = st
            qs = (q_ref[rows, :] * scale).astype(BF16)
            ov = lax.dot_general(qs, st.astype(BF16), (((1,), (1,)), ((), ())), preferred_element_type=F32)
            rstd = lax.rsqrt(jnp.mean(ov * ov, axis=-1, keepdims=True) + EPS)
            rv = r_ref[rows, :]
            out_ref[rows, :] = (ov * rstd * onorm * (rv * jax.nn.sigmoid(rv))).astype(out_ref.dtype)
        s_scr[...] = st

    in_specs = [
        pl.BlockSpec((tb, dk), lambda h, t: (t, o['q0'] + h)),
        pl.BlockSpec((tb, dk), lambda h, t: (t, o['k0'] + h)),
        pl.BlockSpec((tb, dv), lambda h, t: (t, o['v0'] + h)),
        pl.BlockSpec((tb, dv), lambda h, t: (t, o['r0'] + h)),
        pl.BlockSpec((tb, LANES), lambda h, t: (t, o['g0'])),
        pl.BlockSpec((LANES, dk), lambda h, t: (0, h)),
        pl.BlockSpec((1, dk), lambda h, t: (0, h)),
        pl.BlockSpec((1, dv), lambda h, t: (0, 0)),
    ]
    return _pcall(
        body, name=name, grid=(heads, nt),
        out_shape=[jax.ShapeDtypeStruct((seq, vw), BF16),
                   jax.ShapeDtypeStruct((heads, seq // CHUNK, dv, dk), F32)],
        in_specs=in_specs,
        out_specs=[pl.BlockSpec((tb, dv), lambda h, t: (t, h)),
                   pl.BlockSpec((None, cb, dv, dk), lambda h, t: (h, t, 0, 0))],
        scratch_shapes=[pltpu.VMEM((dv, dk), F32)],
        compiler_params=_params('parallel', 'arbitrary'),
    )(proj, proj, proj, proj, proj, wgu_pad, b_gate, o_norm)


def _gla_scan_bwd(proj, wgu_pad, b_gate, o_norm, states, dgated, heads, kw, vw, tb, name):
    seq, pw = proj.shape
    dk, dv = kw // heads, vw // heads
    cb = tb // CHUNK
    nt = seq // tb
    o = _gla_dims(pw, kw, vw, dk, dv)
    scale = dk ** -0.5

    def body(q_ref, k_ref, v_ref, r_ref, gl_ref, wgu_ref, b_ref, on_ref, st_ref, stp_ref, dg_ref,
             dq_ref, dk_ref, dv_ref, dr_ref, dpre_ref, db_ref, don_ref, ds_scr):
        hh = pl.program_id(0)
        t = pl.program_id(1)

        @pl.when(t == 0)
        def _():
            ds_scr[...] = jnp.zeros_like(ds_scr)
            db_ref[...] = jnp.zeros_like(db_ref)

        @pl.when(jnp.logical_and(hh == 0, t == 0))
        def _():
            don_ref[...] = jnp.zeros_like(don_ref)

        wgu = wgu_ref[...].astype(BF16)
        bias = b_ref[...]
        onorm = on_ref[...]
        has_prev = (t < nt - 1).astype(F32)
        r_i = lax.broadcasted_iota(jnp.int32, (CHUNK, CHUNK), 0)
        c_i = lax.broadcasted_iota(jnp.int32, (CHUNK, CHUNK), 1)
        strict = (c_i < r_i).astype(BF16)
        carry = ds_scr[...]
        db_acc = jnp.zeros((1, dk), F32)
        don_acc = jnp.zeros((1, dv), F32)
        for ci in reversed(range(cb)):
            rows = pl.ds(ci * CHUNK, CHUNK)
            pre, cum, total = _gla_gates(gl_ref[rows, :], wgu, bias)
            edec = jnp.exp(total - cum)
            decay = jnp.exp(total)
            kdec = k_ref[rows, :] * edec
            st = st_ref[ci]
            st_prev = st_ref[ci - 1] if ci > 0 else stp_ref[0] * has_prev
            stb = st.astype(BF16)
            qs = (q_ref[rows, :] * scale).astype(BF16)
            vb = v_ref[rows, :].astype(BF16)
            ov = lax.dot_general(qs, stb, (((1,), (1,)), ((), ())), preferred_element_type=F32)
            rstd = lax.rsqrt(jnp.mean(ov * ov, axis=-1, keepdims=True) + EPS)
            ohat = ov * rstd
            rv = r_ref[rows, :]
            sr = jax.nn.sigmoid(rv)
            dgv = dg_ref[rows, :]
            dy = dgv * (rv * sr)
            dr_ref[rows, :] = (dgv * (ohat * onorm) * (sr * (1.0 + rv * (1.0 - sr)))).astype(dr_ref.dtype)
            don_acc = don_acc + jnp.sum(dy * ohat, axis=0, keepdims=True)
            dohat = dy * onorm
            do = (rstd * (dohat - ohat * jnp.mean(dohat * ohat, axis=-1, keepdims=True))).astype(BF16)
            dq_ref[rows, :] = (jnp.dot(do, stb, preferred_element_type=F32) * scale).astype(dq_ref.dtype)
            dst = carry + lax.dot_general(do, qs, (((0,), (0,)), ((), ())), preferred_element_type=F32)
            dstb = dst.astype(BF16)
            dkdec = jnp.dot(vb, dstb, preferred_element_type=F32)
            dv_ref[rows, :] = lax.dot_general(kdec.astype(BF16), dstb, (((1,), (1,)), ((), ())),
                                              preferred_element_type=F32).astype(dv_ref.dtype)
            ddecay = jnp.sum(dst * st_prev, axis=0, keepdims=True)
            dk_ref[rows, :] = (dkdec * edec).astype(dk_ref.dtype)
            da = ddecay * decay + _tri_dot(strict, dkdec * kdec)
            dpre = da * (1.0 / GLA_GATE_TEMP) * (1.0 - jax.nn.sigmoid(pre))
            dpre_ref[rows, :] = dpre.astype(dpre_ref.dtype)
            db_acc = db_acc + jnp.sum(dpre, axis=0, keepdims=True)
            carry = dst * decay
        ds_scr[...] = carry
        db_ref[...] += db_acc
        don_ref[...] += don_acc

    rt = lambda t: nt - 1 - t
    in_specs = [
        pl.BlockSpec((tb, dk), lambda h, t: (rt(t), o['q0'] + h)),
        pl.BlockSpec((tb, dk), lambda h, t: (rt(t), o['k0'] + h)),
        pl.BlockSpec((tb, dv), lambda h, t: (rt(t), o['v0'] + h)),
        pl.BlockSpec((tb, dv), lambda h, t: (rt(t), o['r0'] + h)),
        pl.BlockSpec((tb, LANES), lambda h, t: (rt(t), o['g0'])),
        pl.BlockSpec((LANES, dk), lambda h, t: (0, h)),
        pl.BlockSpec((1, dk), lambda h, t: (0, h)),
        pl.BlockSpec((1, dv), lambda h, t: (0, 0)),
        pl.BlockSpec((None, cb, dv, dk), lambda h, t: (h, rt(t), 0, 0)),
        pl.BlockSpec((None, 1, dv, dk), lambda h, t: (h, jnp.maximum(rt(t) * cb - 1, 0), 0, 0)),
        pl.BlockSpec((tb, dv), lambda h, t: (rt(t), h)),
    ]
    out_shape = [jax.ShapeDtypeStruct((seq, kw), BF16), jax.ShapeDtypeStruct((seq, kw), BF16),
                 jax.ShapeDtypeStruct((seq, vw), BF16), jax.ShapeDtypeStruct((seq, vw), BF16),
                 jax.ShapeDtypeStruct((seq, kw), BF16),
                 jax.ShapeDtypeStruct((1, kw), F32), jax.ShapeDtypeStruct((1, dv), F32)]
    out_specs = [pl.BlockSpec((tb, dk), lambda h, t: (rt(t), h)),
                 pl.BlockSpec((tb, dk), lambda h, t: (rt(t), h)),
                 pl.BlockSpec((tb, dv), lambda h, t: (rt(t), h)),
                 pl.BlockSpec((tb, dv), lambda h, t: (rt(t), h)),
                 pl.BlockSpec((tb, dk), lambda h, t: (rt(t), h)),
                 pl.BlockSpec((1, dk), lambda h, t: (0, h)),
                 pl.BlockSpec((1, dv), lambda h, t: (0, 0))]
    return _pcall(
        body, name=name, grid=(heads, nt),
        out_shape=out_shape, in_specs=in_specs, out_specs=out_specs,
        scratch_shapes=[pltpu.VMEM((dv, dk), F32)],
        compiler_params=_params('arbitrary', 'arbitrary'),
    )(proj, proj, proj, proj, proj, wgu_pad, b_gate, o_norm, states, states, dgated)


def _cmul(ar, ai, br, bi):
    return ar * br - ai * bi, ar * bi + ai * br


def _gelu(y):
    c = math.sqrt(2.0 / math.pi)
    return 0.5 * y * (1.0 + jnp.tanh(c * (y + 0.044715 * y * y * y)))


def _gelu_grad(y):
    c = math.sqrt(2.0 / math.pi)
    th = jnp.tanh(c * (y + 0.044715 * y * y * y))
    return 0.5 * (1.0 + th) + 0.5 * y * (1.0 - th * th) * (c * (1.0 + 3.0 * 0.044715 * y * y))


def _power_pow2(ar, ai, n):
    assert n & (n - 1) == 0
    for _ in range(n.bit_length() - 1):
        ar, ai = _cmul(ar, ai, ar, ai)
    return ar, ai


def _s5_fwd(u, bre, bim, cre, cim, are, aim, dskip, name):
    seq, width = u.shape
    nb, ub, sb = bre.shape
    ls = seq // S5_SEGMENTS
    seg = S5_SEGMENTS

    def body(u_ref, bre_ref, bim_ref, cre_ref, cim_ref, are_ref, aim_ref, d_ref, y_ref, z_ref, xr_ref, xi_ref):
        uv = u_ref[...]
        ub16 = uv.astype(BF16)
        xr_ref[...] = jnp.dot(ub16, bre_ref[...].astype(BF16), preferred_element_type=F32)
        xi_ref[...] = jnp.dot(ub16, bim_ref[...].astype(BF16), preferred_element_type=F32)
        ar = jnp.broadcast_to(are_ref[...], (seg, sb))
        ai = jnp.broadcast_to(aim_ref[...], (seg, sb))

        def step(i, c):
            rows = pl.ds(pl.multiple_of(i * seg, seg), seg)
            pr, pi = _cmul(ar, ai, c[0], c[1])
            nr = pr + xr_ref[rows, :]
            ni = pi + xi_ref[rows, :]
            xr_ref[rows, :] = nr
            xi_ref[rows, :] = ni
            return nr, ni

        zero = jnp.zeros((seg, sb), F32)
        er, ei = lax.fori_loop(0, ls, step, (zero, zero), unroll=8)
        pr, pi = _power_pow2(ar, ai, ls)
        row = lax.broadcasted_iota(jnp.int32, (seg, sb), 0)
        sr, si = zero, zero
        for _ in range(seg - 1):
            tr, ti = _cmul(pr, pi, sr, si)
            sr = jnp.where(row == 0, 0.0, pltpu.roll(tr + er, 1, 0))
            si = jnp.where(row == 0, 0.0, pltpu.roll(ti + ei, 1, 0))

        def fix(i, c):
            rows = pl.ds(pl.multiple_of(i * seg, seg), seg)
            fr, fi = _cmul(c[0], c[1], sr, si)
            xr_ref[rows, :] += fr
            xi_ref[rows, :] += fi
            return _cmul(c[0], c[1], ar, ai)

        lax.fori_loop(0, ls, fix, (ar, ai), unroll=8)
        y = (jnp.dot(xr_ref[...].astype(BF16), cre_ref[...].astype(BF16), preferred_element_type=F32)
             - jnp.dot(xi_ref[...].astype(BF16), cim_ref[...].astype(BF16), preferred_element_type=F32)
             + d_ref[...] * uv)
        y_ref[...] = y
        z_ref[...] = _gelu(y).astype(z_ref.dtype)

    mat = lambda r, c: pl.BlockSpec((None, r, c), lambda b: (b, 0, 0))
    return _pcall(
        body, name=name, grid=(nb,),
        out_shape=[jax.ShapeDtypeStruct((seq, width), F32), jax.ShapeDtypeStruct((seq, width), BF16),
                   jax.ShapeDtypeStruct((seq, nb * sb), F32), jax.ShapeDtypeStruct((seq, nb * sb), F32)],
        in_specs=[pl.BlockSpec((seq, ub), lambda b: (0, b)), mat(ub, sb), mat(ub, sb), mat(sb, ub), mat(sb, ub),
                  mat(1, sb), mat(1, sb), pl.BlockSpec((1, ub), lambda b: (0, b))],
        out_specs=[pl.BlockSpec((seq, ub), lambda b: (0, b)), pl.BlockSpec((seq, ub), lambda b: (0, b)),
                   pl.BlockSpec((seq, sb), lambda b: (0, b)), pl.BlockSpec((seq, sb), lambda b: (0, b))],
        compiler_params=_params('parallel'),
    )(u, bre, bim, cre, cim, are, aim, dskip)


def _s5_bwd(dz, y, u, xr, xi, bre, bim, cre, cim, are, aim, dskip, name):
    seq, width = u.shape
    nb, ub, sb = bre.shape
    ls = seq // S5_SEGMENTS
    seg = S5_SEGMENTS

    def body(dz_ref, y_ref, u_ref, xr_ref, xi_ref, bre_ref, bim_ref, cre_ref, cim_ref, are_ref, aim_ref, d_ref,
             du_ref, dcr_ref, dci_ref, dbr_ref, dbi_ref, dar_ref, dai_ref, dd_ref, lr_ref, li_ref):
        uv = u_ref[...]
        dy = dz_ref[...] * _gelu_grad(y_ref[...])
        dd_ref[...] = jnp.sum(dy * uv, axis=0, keepdims=True)
        dyb = dy.astype(BF16)
        nt = (((1,), (1,)), ((), ()))
        tn = (((0,), (0,)), ((), ()))
        lr_ref[...] = lax.dot_general(dyb, cre_ref[...].astype(BF16), nt, preferred_element_type=F32)
        li_ref[...] = -lax.dot_general(dyb, cim_ref[...].astype(BF16), nt, preferred_element_type=F32)
        dcr_ref[...] = lax.dot_general(dyb, xr_ref[...].astype(BF16), tn, preferred_element_type=F32)
        dci_ref[...] = -lax.dot_general(dyb, xi_ref[...].astype(BF16), tn, preferred_element_type=F32)
        ar = jnp.broadcast_to(are_ref[...], (seg, sb))
        ai = jnp.broadcast_to(aim_ref[...], (seg, sb))
        nai = -ai

        def step(ii, c):
            rows = pl.ds(pl.multiple_of((ls - 1 - ii) * seg, seg), seg)
            pr, pi = _cmul(ar, nai, c[0], c[1])
            nr = pr + lr_ref[rows, :]
            ni = pi + li_ref[rows, :]
            lr_ref[rows, :] = nr
            li_ref[rows, :] = ni
            return nr, ni

        zero = jnp.zeros((seg, sb), F32)
        er, ei = lax.fori_loop(0, ls, step, (zero, zero), unroll=8)
        pr, pi = _power_pow2(ar, nai, ls)
        row = lax.broadcasted_iota(jnp.int32, (seg, sb), 0)
        rr, ri = zero, zero
        for _ in range(seg - 1):
            tr, ti = _cmul(pr, pi, rr, ri)
            rr = jnp.where(row == seg - 1, 0.0, pltpu.roll(tr + er, seg - 1, 0))
            ri = jnp.where(row == seg - 1, 0.0, pltpu.roll(ti + ei, seg - 1, 0))

        def corrected(rows, qr, qi):
            fr, fi = _cmul(qr, qi, rr, ri)
            nr = lr_ref[rows, :] + fr
            ni = li_ref[rows, :] + fi
            lr_ref[rows, :] = nr
            li_ref[rows, :] = ni
            return nr, ni

        def grad_a(nr, ni, xpr, xpi, accr, acci):
            return accr + nr * xpr + ni * xpi, acci + ni * xpr - nr * xpi

        def fix(ii, c):
            qr, qi, accr, acci = c
            i = ls - 1 - ii
            rows = pl.ds(pl.multiple_of(i * seg, seg), seg)
            prev = pl.ds(pl.multiple_of((i - 1) * seg, seg), seg)
            nr, ni = corrected(rows, qr, qi)
            accr, acci = grad_a(nr, ni, xr_ref[prev, :], xi_ref[prev, :], accr, acci)
            qr, qi = _cmul(qr, qi, ar, nai)
            return qr, qi, accr, acci

        qr, qi, accr, acci = lax.fori_loop(0, ls - 1, fix, (ar, nai, zero, zero), unroll=8)
        nr, ni = corrected(pl.ds(0, seg), qr, qi)
        last = pl.ds((ls - 1) * seg, seg)
        xpr = jnp.where(row == 0, 0.0, pltpu.roll(xr_ref[last, :], 1, 0))
        xpi = jnp.where(row == 0, 0.0, pltpu.roll(xi_ref[last, :], 1, 0))
        accr, acci = grad_a(nr, ni, xpr, xpi, accr, acci)
        dar_ref[...] = jnp.sum(accr, axis=0, keepdims=True)
        dai_ref[...] = jnp.sum(acci, axis=0, keepdims=True)
        lrb = lr_ref[...].astype(BF16)
        lib = li_ref[...].astype(BF16)
        ub16 = uv.astype(BF16)
        dbr_ref[...] = lax.dot_general(ub16, lrb, tn, preferred_element_type=F32)
        dbi_ref[...] = lax.dot_general(ub16, lib, tn, preferred_element_type=F32)
        du_ref[...] = (d_ref[...] * dy
                       + lax.dot_general(lrb, bre_ref[...].astype(BF16), nt, preferred_element_type=F32)
                       + lax.dot_general(lib, bim_ref[...].astype(BF16), nt, preferred_element_type=F32))

    mat = lambda r, c: pl.BlockSpec((None, r, c), lambda b: (b, 0, 0))
    col = lambda w: pl.BlockSpec((seq, w), lambda b: (0, b))
    return _pcall(
        body, name=name, grid=(nb,),
        out_shape=[jax.ShapeDtypeStruct((seq, width), F32)]
        + [jax.ShapeDtypeStruct((nb, ub, sb), F32)] * 4
        + [jax.ShapeDtypeStruct((nb, 1, sb), F32)] * 2
        + [jax.ShapeDtypeStruct((1, width), F32)],
        in_specs=[col(ub), col(ub), col(ub), col(sb), col(sb), mat(ub, sb), mat(ub, sb), mat(sb, ub), mat(sb, ub),
                  mat(1, sb), mat(1, sb), pl.BlockSpec((1, ub), lambda b: (0, b))],
        out_specs=[col(ub), mat(ub, sb), mat(ub, sb), mat(ub, sb), mat(ub, sb), mat(1, sb), mat(1, sb),
                   pl.BlockSpec((1, ub), lambda b: (0, b))],
        scratch_shapes=[pltpu.VMEM((seq, sb), F32), pltpu.VMEM((seq, sb), F32)],
        compiler_params=_params('parallel'),
    )(dz, y, u, xr, xi, bre, bim, cre, cim, are, aim, dskip)


def _s5_discretise(lam_re, lam_im, log_dt, b_re, b_im):
    lr = jnp.minimum(lam_re, S5_EIG_CLIP)
    li = lam_im
    dt = jnp.exp(log_dt)[:, None]
    mag = jnp.exp(lr * dt)
    ang = li * dt
    ab_re = mag * jnp.cos(ang)
    ab_im = mag * jnp.sin(ang)
    den = lr * lr + li * li
    nr = ab_re - 1.0
    f_re = (nr * lr + ab_im * li) / den
    f_im = (ab_im * lr - nr * li) / den
    bb_re = f_re[..., None] * b_re - f_im[..., None] * b_im
    bb_im = f_re[..., None] * b_im + f_im[..., None] * b_re
    return ab_re, ab_im, bb_re, bb_im


def _to_blocks(m):
    g, a, b = m.shape
    gb = S5_GROUPS_PER_BLOCK
    eye = jnp.eye(gb, dtype=m.dtype)
    return jnp.einsum('bgac,gh->bgahc', m.reshape(g // gb, gb, a, b), eye).reshape(g // gb, gb * a, gb * b)


def _from_blocks(m, a, b):
    nb = m.shape[0]
    gb = S5_GROUPS_PER_BLOCK
    eye = jnp.eye(gb, dtype=m.dtype)
    return jnp.einsum('bgahc,gh->bgac', m.reshape(nb, gb, a, gb, b), eye).reshape(nb * gb, a, b)


def _glu_fwd(o, h, name):
    half = o.shape[1] // 2

    def fn(ov, hv):
        return (hv + ov[:, :half] * jax.nn.sigmoid(ov[:, half:]),)
    return _rowwise(name, fn, [o, h], [], [(half, F32)], [], 256)[0]


def _glu_bwd(o, dout, name):
    half = o.shape[1] // 2

    def fn(ov, dv):
        val, gate = ov[:, :half], ov[:, half:]
        sg = jax.nn.sigmoid(gate)
        return (jnp.concatenate([dv * sg, dv * val * sg * (1.0 - sg)], axis=1),)
    return _rowwise(name, fn, [o, dout], [], [(2 * half, BF16)], [], 256)[0]


def _adam_math(w, g, m, v):
    m = ADAM_B1 * m + (1.0 - ADAM_B1) * g
    v = ADAM_B2 * v + (1.0 - ADAM_B2) * (g * g)
    m_hat = m / (1.0 - ADAM_B1 ** ADAM_STEP)
    v_hat = v / (1.0 - ADAM_B2 ** ADAM_STEP)
    delta = -ADAM_LR * (m_hat / (jnp.sqrt(v_hat) + ADAM_EPS) + ADAM_WD * w)
    return delta, m, v


def _adamw(w, m, v, grads, name, after=()):
    nl, rows, cols = w.shape
    tm = _tile(rows, max(8, (1 << 18) // cols // 8 * 8), unit=8)
    nbk = rows // tm

    def body(*refs):
        w_ref, m_ref, v_ref = refs[:3]
        g_refs = refs[3:3 + nl]
        go_ref, d_ref, mo_ref, vo_ref = refs[3 + nl + len(after):]
        layer = pl.program_id(0)
        g = g_refs[0][...]
        for l in range(1, nl):
            g = jnp.where(layer == l, g_refs[l][...], g)
        delta, mn, vn = _adam_math(w_ref[...], g, m_ref[...], v_ref[...])
        go_ref[...] = g
        d_ref[...] = delta
        mo_ref[...] = mn
        vo_ref[...] = vn

    stacked = pl.BlockSpec((None, tm, cols), lambda l, i: (l, i, 0))

    def g_spec(layer):
        return pl.BlockSpec((tm, cols), lambda l, i: (jnp.where(l == layer, i, jnp.where(l < layer, 0, nbk - 1)), 0))

    return _pcall(
        body, name=name, grid=(nl, nbk),
        out_shape=[jax.ShapeDtypeStruct(w.shape, F32)] * 4,
        in_specs=[stacked] * 3 + [g_spec(l) for l in range(nl)] + [ANY] * len(after),
        out_specs=[stacked] * 4,
        compiler_params=_params('arbitrary', 'arbitrary'),
    )(w, m, v, *grads, *after)


def _adamw_t(w, m, v, g, name, after=()):
    cols, nl, rows = w.shape
    budget = max(1, (1 << 21) // (8 * rows * 4))
    tc = max(t for t in range(1, min(cols, budget) + 1) if cols % t == 0)

    def body(*refs):
        w_ref, m_ref, v_ref, g_ref = refs[:4]
        d_ref, mo_ref, vo_ref = refs[4 + len(after):]
        delta, mn, vn = _adam_math(w_ref[...], g_ref[...], m_ref[...], v_ref[...])
        d_ref[...] = delta
        mo_ref[...] = mn
        vo_ref[...] = vn

    blk = pl.BlockSpec((tc, nl, rows), lambda i: (i, 0, 0))
    return _pcall(
        body, name=name, grid=(cols // tc,),
        out_shape=[jax.ShapeDtypeStruct(w.shape, F32)] * 3,
        in_specs=[blk] * 4 + [ANY] * len(after), out_specs=[blk] * 3,
        compiler_params=_params('parallel'),
    )(w, m, v, g, *after)


def _pack(arrs, rows_mult=512):
    flat = jnp.concatenate([a.reshape(-1) for a in arrs])
    total = flat.shape[0]
    rows = -(-total // LANES)
    rows = -(-rows // rows_mult) * rows_mult
    flat = jnp.pad(flat, (0, rows * LANES - total))
    return flat.reshape(rows, LANES)


def _unpack(packed, shapes):
    flat = packed.reshape(-1)
    out, off = [], 0
    for s in shapes:
        size = math.prod(s)
        out.append(flat[off:off + size].reshape(s))
        off += size
    return out


def _permute(a):
    seq, w = a.shape
    return a.reshape(S5_SEGMENTS, seq // S5_SEGMENTS, w).transpose(1, 0, 2).reshape(seq, w)


def _unpermute(a):
    seq, w = a.shape
    return a.reshape(seq // S5_SEGMENTS, S5_SEGMENTS, w).transpose(1, 0, 2).reshape(seq, w)


def kernel(x, gla_norm, gla_w_in, gla_w_gate_up, gla_b_gate, gla_o_norm, gla_w_out, s5_norm, s5_w_in, s5_lam_re, s5_lam_im, s5_log_dt, s5_b_re, s5_b_im, s5_c_re, s5_c_im, s5_d, s5_w_out, mlp_norm, mlp_w_up, mlp_w_down, final_norm, loss_target, m_gla_norm, m_gla_w_in, m_gla_w_gate_up, m_gla_b_gate, m_gla_o_norm, m_gla_w_out, m_s5_norm, m_s5_w_in, m_s5_lam_re, m_s5_lam_im, m_s5_log_dt, m_s5_b_re, m_s5_b_im, m_s5_c_re, m_s5_c_im, m_s5_d, m_s5_w_out, m_mlp_norm, m_mlp_w_up, m_mlp_w_down, m_final_norm, v_gla_norm, v_gla_w_in, v_gla_w_gate_up, v_gla_b_gate, v_gla_o_norm, v_gla_w_out, v_s5_norm, v_s5_w_in, v_s5_lam_re, v_s5_lam_im, v_s5_log_dt, v_s5_b_re, v_s5_b_im, v_s5_c_re, v_s5_c_im, v_s5_d, v_s5_w_out, v_mlp_norm, v_mlp_w_up, v_mlp_w_down, v_final_norm):
    weights = dict(gla_norm=gla_norm, gla_w_in=gla_w_in, gla_w_gate_up=gla_w_gate_up, gla_b_gate=gla_b_gate, gla_o_norm=gla_o_norm, gla_w_out=gla_w_out, s5_norm=s5_norm, s5_w_in=s5_w_in, s5_lam_re=s5_lam_re, s5_lam_im=s5_lam_im, s5_log_dt=s5_log_dt, s5_b_re=s5_b_re, s5_b_im=s5_b_im, s5_c_re=s5_c_re, s5_c_im=s5_c_im, s5_d=s5_d, s5_w_out=s5_w_out, mlp_norm=mlp_norm, mlp_w_up=mlp_w_up, mlp_w_down=mlp_w_down, final_norm=final_norm)
    mom1 = dict(gla_norm=m_gla_norm, gla_w_in=m_gla_w_in, gla_w_gate_up=m_gla_w_gate_up, gla_b_gate=m_gla_b_gate, gla_o_norm=m_gla_o_norm, gla_w_out=m_gla_w_out, s5_norm=m_s5_norm, s5_w_in=m_s5_w_in, s5_lam_re=m_s5_lam_re, s5_lam_im=m_s5_lam_im, s5_log_dt=m_s5_log_dt, s5_b_re=m_s5_b_re, s5_b_im=m_s5_b_im, s5_c_re=m_s5_c_re, s5_c_im=m_s5_c_im, s5_d=m_s5_d, s5_w_out=m_s5_w_out, mlp_norm=m_mlp_norm, mlp_w_up=m_mlp_w_up, mlp_w_down=m_mlp_w_down, final_norm=m_final_norm)
    mom2 = dict(gla_norm=v_gla_norm, gla_w_in=v_gla_w_in, gla_w_gate_up=v_gla_w_gate_up, gla_b_gate=v_gla_b_gate, gla_o_norm=v_gla_o_norm, gla_w_out=v_gla_w_out, s5_norm=v_s5_norm, s5_w_in=v_s5_w_in, s5_lam_re=v_s5_lam_re, s5_lam_im=v_s5_lam_im, s5_log_dt=v_s5_log_dt, s5_b_re=v_s5_b_re, s5_b_im=v_s5_b_im, s5_c_re=v_s5_c_re, s5_c_im=v_s5_c_im, s5_d=v_s5_d, s5_w_out=v_s5_w_out, mlp_norm=v_mlp_norm, mlp_w_up=v_mlp_w_up, mlp_w_down=v_mlp_w_down, final_norm=v_final_norm)
    names = list(weights)
    big = ['gla_w_in', 'gla_w_out', 's5_w_in', 's5_w_out', 'mlp_w_up', 'mlp_w_down']
    small = [n for n in names if n not in big]

    chip = 2 * lax.axis_index('x') + lax.axis_index('y')
    h0 = x[0]
    target = loss_target[0]
    seq, dm = h0.shape
    depth = mlp_norm.shape[0]
    n_gla = gla_norm.shape[0]
    n_s5 = s5_lam_re.shape[0]
    rank = gla_w_gate_up.shape[1]
    kw = gla_b_gate.shape[1]
    dv = gla_o_norm.shape[1]
    in_w = 4 * gla_w_in.shape[2]
    vw = (in_w - rank - 2 * kw) // 2
    heads = vw // dv
    dk = kw // heads
    pw = -(-in_w // LANES) * LANES
    s5w = s5_w_in.shape[2]
    n_grp, n_state, grp = s5_b_re.shape[1:]
    hid = 4 * mlp_w_up.shape[2]
    tb = min(seq, 8 * CHUNK)
    tm = _tile(seq, 1024)

    ic = lax.axis_index('c')
    rh = lambda w: w.shape[1] // 2
    wb16 = {n: weights[n].astype(BF16) for n in big}
    gathered = {n: [None] * weights[n].shape[0] for n in big}
    g_w_in, g_gla_out, g_s5_in, g_s5_out, g_up, g_down = (gathered[n] for n in big)
    in_flight = {}

    to_start = []
    for i in range(depth):
        mix = ['gla_w_in', 'gla_w_out'] if i % 2 == 0 else ['s5_w_in', 's5_w_out']
        to_start += [(m, i // 2) for m in mix] + [('mlp_w_up', i), ('mlp_w_down', i)]

    def next_gathers(room):
        keys, jobs = [], []
        while to_start and room > 0:
            n, l = to_start.pop(0)
            rows, cols = weights[n].shape[1:]
            keys.append((n, l))
            jobs.append(('chips_gather', wb16[n][l].reshape(2, rows // 2, cols), (2, 4, rows // 2, cols)))
            room -= 1
        return keys, jobs

    def finish_gather(n, l, after):
        halves, land = _wait(in_flight.pop((n, l)), f'ag_{n}_{l}_wait', after)
        own = lax.dynamic_index_in_dim(halves, ic, 0, keepdims=True)
        land = lax.dynamic_update_slice(land, own[:, None], (ic, chip, 0, 0))
        keys, jobs = next_gathers(GATHERS_IN_FLIGHT - len(in_flight))
        handles = _start_many([('pair_inplace', None, land)] + jobs, f'ag_{n}_{l}_pair_start', [])
        in_flight.update(zip(keys, handles[1:]))
        gathered[n][l] = _wait(handles[0], f'ag_{n}_{l}_pair_wait', [])[0]
        return [gathered[n][l]]

    sharded_small = [gla_w_gate_up, s5_norm, s5_d]
    gathered_small = _exchange(_pack(sharded_small), 'xy', 'bcast', 'ag_small')
    keys, jobs = next_gathers(GATHERS_IN_FLIGHT)
    in_flight.update(zip(keys, _start_many(jobs, 'ag_first_start', [gathered_small])))
    parts = [_unpack(gathered_small[k], [a.shape for a in sharded_small]) for k in range(4)]
    wgu_full = jnp.concatenate([p[0] for p in parts], axis=2)
    s5_norm_full = jnp.concatenate([p[1] for p in parts], axis=1)
    s5_d_full = jnp.concatenate([p[2] for p in parts], axis=1)

    def gla_w_in_padded(j):
        wj = g_w_in[j].transpose(0, 2, 1, 3).reshape(dm, in_w)
        return jnp.pad(wj, ((0, 0), (0, pw - in_w)))

    grads = {n: [None] * weights[n].shape[0] for n in names if n != 'final_norm'}

    saved = []
    h = h0
    for i in range(depth):
        j = i // 2
        rec = {}
        if i % 2 == 0:
            rec['h_in'] = h
            behind = finish_gather('gla_w_in', j, [h])
            hn = _norm_fwd(h, gla_norm[j:j + 1], 'gla_norm_fwd', behind)
            w_in_pad = gla_w_in_padded(j)
            proj = _mm('gla_proj', 'nn', _plain(hn), _plain(w_in_pad), (seq, pw, dm),
                       [((seq, pw), F32, _plain_shape(None))], (tm, _tile(pw, 1024), dm))[0]
            wgu_pad = jnp.pad(wgu_full[j], ((0, LANES - rank), (0, 0)))
            gated, states = _gla_scan_fwd(proj, wgu_pad, gla_b_gate[j:j + 1], gla_o_norm[j:j + 1],
                                          heads, kw, vw, tb, 'gla_scan_fwd')
            behind = finish_gather('gla_w_out', j, [gated])
            h = _mm('gla_out', 'nn', _plain(gated), _w_rows(g_gla_out, j), (seq, dm, vw),
                    [((seq, dm), F32, _plain_shape(None))],
                    (tm, _tile(dm, 1024), gla_w_out.shape[1]),
                    epilogue=lambda acc, hv: (acc + hv,), extras=[_plain(h)], after=behind)[0]
            rec.update(hn=hn, w_in_pad=w_in_pad, proj=proj, wgu_pad=wgu_pad, gated=gated, states=states)
        else:
            hp = _permute(h)
            rec['h_in'] = hp
            hn = _norm_fwd(hp, s5_norm_full[j:j + 1], 's5_norm_fwd')
            behind = finish_gather('s5_w_in', j, [hn])
            u = _mm('s5_in', 'nn', _plain(hn), _w_rows(g_s5_in, j), (seq, s5w, dm),
                    [((seq, s5w), F32, _plain_shape(None))],
                    (tm, _tile(s5w, 1024), s5_w_in.shape[1]), after=behind)[0]
            disc, disc_vjp = jax.vjp(_s5_discretise, s5_lam_re[j], s5_lam_im[j], s5_log_dt[j], s5_b_re[j], s5_b_im[j])
            ab_re, ab_im, bb_re, bb_im = disc
            bre = _to_blocks(bb_re.transpose(0, 2, 1))
            bim = _to_blocks(bb_im.transpose(0, 2, 1))
            cre = _to_blocks(s5_c_re[j].transpose(0, 2, 1))
            cim = _to_blocks(s5_c_im[j].transpose(0, 2, 1))
            nb = n_grp // S5_GROUPS_PER_BLOCK
            are = ab_re.reshape(nb, 1, S5_GROUPS_PER_BLOCK * n_state)
            aim = ab_im.reshape(nb, 1, S5_GROUPS_PER_BLOCK * n_state)
            dskip = s5_d_full[j:j + 1]
            y, z, xr, xi = _s5_fwd(u, bre, bim, cre, cim, are, aim, dskip, 's5_scan_fwd')
            behind = finish_gather('s5_w_out', j, [z])
            o = _mm('s5_out', 'nn', _plain(z), _w_cols(g_s5_out, j), (seq, 2 * dm, s5w),
                    [((seq, 2 * dm), F32, _plain_shape(None))],
                    (tm, _tile(s5_w_out.shape[2], 1024), s5_w_out.shape[1]), after=behind)[0]
            h = _unpermute(_glu_fwd(o, hp, 's5_glu_fwd'))
            rec.update(hn=hn, u=u, y=y, z=z, xr=xr, xi=xi, o=o, mats=(bre, bim, cre, cim, are, aim, dskip),
                       disc_vjp=disc_vjp)
        rec['h_mid'] = h
        hn2 = _norm_fwd(h, mlp_norm[i:i + 1], 'mlp_norm_fwd')
        behind = finish_gather('mlp_w_up', i, [hn2])
        act, act2 = _mm('mlp_up', 'nn', _plain(hn2), _w_cols(g_up, i), (seq, hid, dm),
                        [((seq, hid), BF16, _plain_shape(None))] * 2,
                        (tm, _tile(mlp_w_up.shape[2], 1024), mlp_w_up.shape[1]),
                        epilogue=lambda acc: (jnp.maximum(acc, 0.0), jnp.square(jnp.maximum(acc, 0.0))),
                        after=behind)
        behind = finish_gather('mlp_w_down', i, [act2])
        h = _mm('mlp_down', 'nn', _plain(act2), _w_rows(g_down, i), (seq, dm, hid),
                [((seq, dm), F32, _plain_shape(None))],
                (tm, _tile(dm, 1024), mlp_w_down.shape[1]),
                epilogue=lambda acc, hv: (acc + hv,), extras=[_plain(h)], after=behind)[0]
        rec.update(hn2=hn2, act=act, act2=act2)
        saved.append(rec)

    dh, loss_cols, d_final = _loss_head(h, target, final_norm.reshape(1, dm), 'loss_head')
    loss = lax.psum(jnp.sum(loss_cols), ('x', 'y', 'c'))
    grads['final_norm'] = [d_final.reshape(dm)]

    big_grads = {n: [None] * weights[n].shape[0] for n in big}
    reducing = []

    def reduce_next(item, after):
        n, l, hd = item['n'], item['l'], item['hd']
        if item['stage'] == 'pair':
            dw, got = _wait(hd, f'rs_{n}_{l}_pair_wait', after)
            _, _, rows_h, cols = dw.shape
            pre = _sum_pair(dw.reshape(2, 4 * rows_h, cols), got.reshape(4 * rows_h, cols), ic, BF16,
                            f'rs_{n}_pairsum').reshape(4, rows_h, cols)
            item['stage'] = 'chips'
            return ('chips_a2a', pre, pre.shape)
        if item['stage'] == 'chips':
            pre, yb = _wait(hd, f'rs_{n}_{l}_wait', after)
            yb = lax.dynamic_update_index_in_dim(yb, lax.dynamic_index_in_dim(pre, chip, 0, keepdims=False), chip, 0)
            fin = _sum_slots(yb, F32, f'rs_{n}_chipsum')
            item['stage'] = 'back'
            return ('pair_bcast', fin, (2,) + fin.shape)
        fin, both = _wait(hd, f'rs_{n}_{l}_back_wait', after)
        both = lax.dynamic_update_index_in_dim(both, fin, ic, 0)
        big_grads[n][l] = both.reshape(2 * fin.shape[0], fin.shape[1])
        item.update(stage='done', hd=None)
        return None

    def reduce_many(items, name, after):
        jobs = [(item, reduce_next(item, after)) for item in items]
        jobs = [(item, job) for item, job in jobs if job is not None]
        if not jobs:
            return after
        for (item, _), hd in zip(jobs, _start_many([job for _, job in jobs], name, after)):
            item['hd'] = hd
        return [jobs[0][0]['hd']['token']]

    def reduce_scatter(dw, n, l):
        new = dict(n=n, l=l, stage='new', hd=None)
        older = [reducing[-k] for k in (1, 3) if len(reducing) >= k]
        jobs = [(item, reduce_next(item, [dw])) for item in older] + [(new, ('pair_swap', dw, dw.shape[1:]))]
        new['stage'] = 'pair'
        reducing.append(new)
        for (item, _), hd in zip(jobs, _start_many([job for _, job in jobs], f'rs_{n}_{l}_start', [])):
            item['hd'] = hd
        return [new['hd']['token']]

    for i in reversed(range(depth)):
        j = i // 2
        rec = saved[i]
        r_dn, c_dn = mlp_w_down.shape[1:]
        shape, spec = _dw_rows(r_dn, c_dn)
        dw = _mm('mlp_down_dw', 'tn', _plain(rec['act2']), _plain(dh), (hid, dm, seq),
                 [(shape, BF16, spec)], (_tile(r_dn // 2, 1024), _tile(c_dn, 1024), seq))[0]
        behind = reduce_scatter(dw, 'mlp_w_down', i)
        dpre = _mm('mlp_down_dx', 'nt', _plain(dh), _w_rows(g_down, i), (seq, hid, dm),
                   [((seq, hid), BF16, _plain_shape(None))],
                   (tm, _tile(rh(mlp_w_down), 1024), dm),
                   epilogue=lambda acc, av: (acc * (2.0 * av.astype(F32)),), extras=[_plain(rec['act'])],
                   after=behind)[0]
        r_up, c_up = mlp_w_up.shape[1:]
        shape, spec = _dw_cols(r_up, c_up)
        dw = _mm('mlp_up_dw', 'tn', _plain(rec['hn2']), _plain(dpre), (dm, hid, seq),
                 [(shape, BF16, spec)], (_tile(r_up // 2, 1024), _tile(c_up, 1024), seq))[0]
        behind = reduce_scatter(dw, 'mlp_w_up', i)
        dhn = _mm('mlp_up_dx', 'nt', _plain(dpre), _w_cols(g_up, i), (seq, dm, hid),
                  [((seq, dm), F32, _plain_shape(None))],
                  (tm, _tile(rh(mlp_w_up), 1024), _tile(mlp_w_up.shape[2], 2048)), after=behind)[0]
        dh, dg = _norm_bwd(rec['h_mid'], dhn, dh, mlp_norm[i:i + 1], 'mlp_norm_bwd')
        grads['mlp_norm'][i] = dg[0]

        if i % 2 == 0:
            r_o, c_o = gla_w_out.shape[1:]
            shape, spec = _dw_rows(r_o, c_o)
            dw = _mm('gla_out_dw', 'tn', _plain(rec['gated']), _plain(dh), (vw, dm, seq),
                     [(shape, BF16, spec)], (_tile(r_o // 2, 1024), _tile(c_o, 1024), seq))[0]
            behind = reduce_scatter(dw, 'gla_w_out', j)
            dgated = _mm('gla_out_dx', 'nt', _plain(dh), _w_rows(g_gla_out, j), (seq, vw, dm),
                         [((seq, vw), F32, _plain_shape(None))],
                         (tm, _tile(rh(gla_w_out), 1024), dm), after=behind)[0]
            dq, dkk, dvv, dr, dpre_g, db, don = _gla_scan_bwd(
                rec['proj'], rec['wgu_pad'], gla_b_gate[j:j + 1], gla_o_norm[j:j + 1], rec['states'], dgated,
                heads, kw, vw, tb, 'gla_scan_bwd')
            grads['gla_b_gate'][j] = db[0]
            grads['gla_o_norm'][j] = don[0]
            dgl = _mm('gla_gate_dx', 'nt', _plain(dpre_g), _plain(rec['wgu_pad']), (seq, LANES, kw),
                      [((seq, LANES), BF16, _plain_shape(None))], (tm, LANES, kw))[0]
            g_low = rec['proj'][:, pw - LANES:]
            dwgu = _mm('gla_gate_dw', 'tn', _plain(g_low), _plain(dpre_g), (LANES, kw, seq),
                       [((LANES, kw), F32, _plain_shape(None))], (LANES, kw, seq))[0]
            grads['gla_w_gate_up'][j] = dwgu[:rank]
            dproj = jnp.concatenate([dq, dkk, dvv, dr, dgl], axis=1)
            dw_pad = _mm('gla_proj_dw', 'tn', _plain(rec['hn']), _plain(dproj), (dm, pw, seq),
                         [((dm, pw), BF16, _plain_shape(None))], (_tile(dm, 1024), _tile(pw, 1024), seq))[0]
            shard_w = in_w // 4
            dw = dw_pad[:, :in_w].reshape(2, dm // 2, 4, shard_w).transpose(0, 2, 1, 3)
            behind = reduce_scatter(dw, 'gla_w_in', j)
            dhn = _mm('gla_proj_dx', 'nt', _plain(dproj), _plain(rec['w_in_pad']), (seq, dm, pw),
                      [((seq, dm), F32, _plain_shape(None))], (tm, _tile(dm, 1024), _tile(pw, 1024)),
                      after=behind)[0]
            dh, dg = _norm_bwd(rec['h_in'], dhn, dh, gla_norm[j:j + 1], 'gla_norm_bwd')
            grads['gla_norm'][j] = dg[0]
        else:
            dhp = _permute(dh)
            do = _glu_bwd(rec['o'], dhp, 's5_glu_bwd')
            r_o, c_o = s5_w_out.shape[1:]
            shape, spec = _dw_cols(r_o, c_o)
            dw = _mm('s5_out_dw', 'tn', _plain(rec['z']), _plain(do), (s5w, 2 * dm, seq),
                     [(shape, BF16, spec)], (_tile(r_o // 2, 1024), _tile(c_o, 1024), seq))[0]
            behind = reduce_scatter(dw, 's5_w_out', j)
            dz = _mm('s5_out_dx', 'nt', _plain(do), _w_cols(g_s5_out, j), (seq, s5w, 2 * dm),
                     [((seq, s5w), F32, _plain_shape(None))],
                     (tm, _tile(rh(s5_w_out), 1024), _tile(s5_w_out.shape[2], 1024)), after=behind)[0]
            bre, bim, cre, cim, are, aim, dskip = rec['mats']
            du, dcr, dci, dbr, dbi, dar, dai, dd = _s5_bwd(dz, rec['y'], rec['u'], rec['xr'], rec['xi'],
                                                           bre, bim, cre, cim, are, aim, dskip, 's5_scan_bwd')
            grads['s5_c_re'][j] = _from_blocks(dcr, grp, n_state)
            grads['s5_c_im'][j] = _from_blocks(dci, grp, n_state)
            dbb_re = _from_blocks(dbr, grp, n_state).transpose(0, 2, 1)
            dbb_im = _from_blocks(dbi, grp, n_state).transpose(0, 2, 1)
            d_lr, d_li, d_dt, d_bre, d_bim = rec['disc_vjp'](
                (dar.reshape(n_grp, n_state), dai.reshape(n_grp, n_state), dbb_re, dbb_im))
            grads['s5_lam_re'][j] = d_lr
            grads['s5_lam_im'][j] = d_li
            grads['s5_log_dt'][j] = d_dt
            grads['s5_b_re'][j] = d_bre
            grads['s5_b_im'][j] = d_bim
            grads['s5_d'][j] = dd[0]
            r_i, c_i = s5_w_in.shape[1:]
            shape, spec = _dw_rows(r_i, c_i)
            dw = _mm('s5_in_dw', 'tn', _plain(rec['hn']), _plain(du), (dm, s5w, seq),
                     [(shape, BF16, spec)], (_tile(r_i // 2, 1024), _tile(c_i, 1024), seq))[0]
            behind = reduce_scatter(dw, 's5_w_in', j)
            dhn = _mm('s5_in_dx', 'nt', _plain(du), _w_rows(g_s5_in, j), (seq, dm, s5w),
                      [((seq, dm), F32, _plain_shape(None))],
                      (tm, _tile(rh(s5_w_in), 1024), _tile(s5w, 1024)), after=behind)[0]
            dhp, dg = _norm_bwd(rec['h_in'], dhn, dhp, s5_norm_full[j:j + 1], 's5_norm_bwd')
            dh = _unpermute(dhp)
            grads['s5_norm'][j] = dg[0]
    grad_x = dh[None]
    behind = [dh]
    for k, (stage, items) in enumerate((('pair', reducing), ('chips', reducing[:-1]), ('back', reducing[:-1]))):
        behind = reduce_many([item for item in items if item['stage'] == stage], f'rs_tail_{k}_start', behind)

    local_small = [jnp.stack(grads[n]) if n != 'final_norm' else grads[n][0] for n in small]
    full_shapes = [a.shape for a in local_small]
    packed_small = _pack(local_small)
    ar_small = _xy_start(packed_small, 'bcast', 'ar_small_start', [])
    out_g, out_d, out_m, out_v = {}, {}, {}, {}
    behind = [ar_small['token']]
    last_n = reducing[-1]['n']
    for n in [m for m in big if m != last_n] + [last_n]:
        if n == last_n:
            for k in range(2):
                behind = reduce_many([reducing[-1]], f'rs_last_{k}_start', behind)
        if weights[n].shape[2] % LANES:
            to_t, from_t = (lambda a: a.transpose(2, 0, 1)), (lambda a: a.transpose(1, 2, 0))
            g_t = to_t(jnp.stack(big_grads[n]))
            res = _adamw_t(to_t(weights[n]), to_t(mom1[n]), to_t(mom2[n]), g_t, 'adamw_' + n, behind)
            out_g[n], out_d[n], out_m[n], out_v[n] = (from_t(a) for a in (g_t,) + tuple(res))
            behind = [res[0]]
        else:
            out_g[n], out_d[n], out_m[n], out_v[n] = _adamw(weights[n], mom1[n], mom2[n], big_grads[n],
                                                            'adamw_' + n, behind)
            behind = [out_d[n]]
    sent, by_chip = _xy_wait(ar_small, 'ar_small_wait', behind)
    by_chip = lax.dynamic_update_index_in_dim(by_chip, sent, chip, 0)
    gathered = _exchange(by_chip, 'c', 'bcast', 'ar_small_c')
    rows = gathered.shape[2]
    summed = _sum_slots(gathered.reshape(8, rows, LANES), F32, 'ar_small_sum')
    small_full = dict(zip(small, _unpack(summed, full_shapes)))
    small_grad = {}
    for n in small:
        g = small_full[n]
        if g.shape != weights[n].shape:
            ax = [a for a in range(g.ndim) if g.shape[a] != weights[n].shape[a]][0]
            g = lax.dynamic_slice_in_dim(g, chip * weights[n].shape[ax], weights[n].shape[ax], axis=ax)
        small_grad[n] = g

    shapes = [weights[n].shape for n in small]
    pw_, pm_, pv_, pg_ = (_pack([d[n] for n in small]) for d in (weights, mom1, mom2, small_grad))
    _, sd, sm, sv = _adamw(pw_[None], pm_[None], pv_[None], [pg_], 'adamw_small')
    for n, d_, m_, v_ in zip(small, _unpack(sd[0], shapes), _unpack(sm[0], shapes), _unpack(sv[0], shapes)):
        out_g[n], out_d[n], out_m[n], out_v[n] = small_grad[n], d_, m_, v_

    return (loss, grad_x, *[out_g[n] for n in names], *[out_d[n] for n in names],
            *[out_m[n] for n in names], *[out_v[n] for n in names])
```

```python
import functools
import math

import jax
import jax.numpy as jnp
from jax import lax
from jax.experimental import pallas as pl
from jax.experimental.pallas import tpu as pltpu

F32 = jnp.float32
BF16 = jnp.bfloat16

EPS = 1e-6
CHUNK = 64
GLA_GATE_TEMP = 16.0
S5_EIG_CLIP = -1e-4
S5_SEGMENTS = 8
S5_GROUPS_PER_BLOCK = 8
LANES = 128
ADAM_LR = 0.001
ADAM_B1 = 0.9
ADAM_B2 = 0.999
ADAM_EPS = 1e-08
ADAM_WD = 0.01
ADAM_STEP = 10
VMEM_LIMIT_BYTES = 56 * 1024 * 1024
PAIR_PIECE_BYTES = 2 * 1024 * 1024
GATHERS_IN_FLIGHT = 2
PAIR_VMEM_BYTES = 40 * 1024 * 1024

MESH = pl.DeviceIdType.MESH
ANY = pl.BlockSpec(memory_space=pl.ANY)
IN_VMEM = pl.BlockSpec(memory_space=pltpu.VMEM)
IN_HBM = pl.BlockSpec(memory_space=pltpu.HBM)
IN_SEM = pl.BlockSpec(memory_space=pltpu.SEMAPHORE)
DATAFLOW = pltpu.SideEffectType.DATAFLOW_SIDE_EFFECTING


def _pcall(body, **kw):
    return pl.pallas_call(body, **kw)


def _params(*sem):
    return pltpu.CompilerParams(dimension_semantics=sem, vmem_limit_bytes=VMEM_LIMIT_BYTES)


def _tile(dim, target, unit=LANES):
    if dim <= target:
        return dim
    best = None
    for t in range(unit, target + 1, unit):
        if dim % t == 0:
            best = t
    assert best is not None, (dim, target)
    return best


def _exchange(x, group, mode, name):
    n = 2 if group == 'c' else 4
    blk = x.shape if mode == 'bcast' else x.shape[1:]
    if mode == 'a2a':
        assert x.shape[0] == n
    flips = [(0, 0, 1)] if group == 'c' else [(1, 0, 0), (0, 1, 0), (1, 1, 0)]
    itemsize = jnp.dtype(x.dtype).itemsize
    staged = group == 'c' and (x.size + n * math.prod(blk)) * itemsize <= PAIR_VMEM_BYTES
    if group == 'c' and not staged:
        ic = lax.axis_index('c')
        own = x if mode == 'bcast' else lax.dynamic_index_in_dim(x, ic, 0, keepdims=False)
        return lax.dynamic_update_index_in_dim(_pair_exchange_chunked(x, mode, name), own, ic, 0)

    def body(x_ref, y_ref, send_sems, recv_sems, local_sem):
        ix, iy, ic = lax.axis_index('x'), lax.axis_index('y'), lax.axis_index('c')

        def slot(px, py, pc):
            return pc if group == 'c' else 2 * px + py

        def src(px, py, pc):
            if mode == 'a2a':
                return x_ref.at[slot(px, py, pc)]
            if mode == 'bcast_c':
                return x_ref.at[ic]
            return x_ref

        me = (ix, iy, ic)
        local = pltpu.make_async_copy(src(*me), y_ref.at[slot(*me)], local_sem)
        local.start()
        peers = []
        for fx, fy, fc in flips:
            peers.append((1 - ix if fx else ix, 1 - iy if fy else iy, 1 - ic if fc else ic))
        sends = []
        for k, peer in enumerate(peers):
            cp = pltpu.make_async_remote_copy(
                src_ref=src(*peer), dst_ref=y_ref.at[slot(*me)],
                send_sem=send_sems.at[k], recv_sem=recv_sems.at[k],
                device_id=peer, device_id_type=MESH)
            cp.start()
            sends.append(cp)
        for k, peer in enumerate(peers):
            pltpu.make_async_remote_copy(
                src_ref=src(*peer), dst_ref=y_ref.at[slot(*peer)],
                send_sem=send_sems.at[k], recv_sem=recv_sems.at[k],
                device_id=peer, device_id_type=MESH).wait_recv()
        for cp in sends:
            cp.wait_send()
        local.wait()

    return _pcall(
        body, name=name,
        out_shape=jax.ShapeDtypeStruct((n,) + tuple(blk), x.dtype),
        in_specs=[IN_VMEM if staged else ANY], out_specs=IN_VMEM if staged else ANY,
        scratch_shapes=[pltpu.SemaphoreType.DMA((len(flips),)),
                        pltpu.SemaphoreType.DMA((len(flips),)),
                        pltpu.SemaphoreType.DMA(())],
        compiler_params=pltpu.CompilerParams(vmem_limit_bytes=VMEM_LIMIT_BYTES),
    )(x)


def _split_axis(blk, dtype, piece_bytes):
    itemsize = jnp.dtype(dtype).itemsize
    sublanes = 8 * 4 // itemsize
    want = max(1, math.prod(blk) * itemsize // piece_bytes)
    for pieces in [s for s in (64, 32, 16, 8, 4, 2) if s <= want]:
        for ax in range(len(blk) - 1):
            unit = sublanes if ax == len(blk) - 2 else 1
            if blk[ax] % (pieces * unit) == 0:
                return ax, pieces
    return 0, 1


def _pair_exchange_chunked(x, mode, name, reduce=False):
    assert mode == 'a2a' or not reduce
    blk = x.shape if mode == 'bcast' else x.shape[1:]
    ax, pieces = _split_axis(blk, x.dtype, PAIR_PIECE_BYTES)
    step = blk[ax] // pieces
    piece_shape = tuple(blk[:ax]) + (step,) + tuple(blk[ax + 1:])

    def piece(ref, p):
        return ref.at[(slice(None),) * ax + (pl.ds(p * step, step),)]

    def body(x_ref, y_ref, out_buf, in_buf, own_buf, send_sems, recv_sems, stage_sems, drain_sems, own_sems, credit_sem):
        ix, iy, ic = lax.axis_index('x'), lax.axis_index('y'), lax.axis_index('c')
        sibling = (ix, iy, 1 - ic)
        mine = x_ref.at[ic] if mode != 'bcast' else x_ref
        theirs = x_ref.at[1 - ic] if mode == 'a2a' else mine

        def own(p):
            return pltpu.make_async_copy(piece(mine, p), own_buf.at[p % 2], own_sems.at[p % 2])

        def stage(p):
            return pltpu.make_async_copy(piece(theirs, p), out_buf.at[p % 2], stage_sems.at[p % 2])

        def remote(p):
            return pltpu.make_async_remote_copy(
                src_ref=out_buf.at[p % 2], dst_ref=in_buf.at[p % 2],
                send_sem=send_sems.at[p], recv_sem=recv_sems.at[p],
                device_id=sibling, device_id_type=MESH)

        def drain(p):
            dst = y_ref if reduce else y_ref.at[1 - ic]
            return pltpu.make_async_copy(in_buf.at[p % 2], piece(dst, p), drain_sems.at[p % 2])

        stage(0).start()
        if reduce:
            own(0).start()
        for p in range(pieces):
            stage(p).wait()
            if p >= 2:
                pl.semaphore_wait(credit_sem, 1)
            remote(p).start()
            if p + 1 < pieces:
                if p >= 1:
                    remote(p - 1).wait_send()
                stage(p + 1).start()
                if reduce:
                    own(p + 1).start()
            remote(p).wait_recv()
            if reduce:
                own(p).wait()
                in_buf[p % 2] = (in_buf[p % 2].astype(F32) + own_buf[p % 2].astype(F32)).astype(in_buf.dtype)
            drain(p).start()
            drain(p).wait()
            if p + 2 < pieces:
                pl.semaphore_signal(credit_sem, inc=1, device_id=sibling, device_id_type=MESH)
        for p in range(max(0, pieces - 2), pieces):
            remote(p).wait_send()

    return _pcall(
        body, name=name,
        out_shape=jax.ShapeDtypeStruct(tuple(blk) if reduce else (2,) + tuple(blk), x.dtype),
        in_specs=[ANY], out_specs=ANY,
        scratch_shapes=[pltpu.VMEM((2,) + piece_shape, x.dtype), pltpu.VMEM((2,) + piece_shape, x.dtype),
                        pltpu.VMEM((2,) + piece_shape if reduce else (2, 8, LANES), x.dtype),
                        pltpu.SemaphoreType.DMA((pieces,)), pltpu.SemaphoreType.DMA((pieces,)),
                        pltpu.SemaphoreType.DMA((2,)), pltpu.SemaphoreType.DMA((2,)), pltpu.SemaphoreType.DMA((2,)),
                        pltpu.SemaphoreType.REGULAR],
        compiler_params=pltpu.CompilerParams(vmem_limit_bytes=VMEM_LIMIT_BYTES),
    )(x)


_FLIPS = {'xy': [(1, 0, 0), (0, 1, 0), (1, 1, 0)], 'c': [(0, 0, 1)]}


def _split_copies(group, mode, x_ref, land_ref, sems):
    ix, iy, ic = lax.axis_index('x'), lax.axis_index('y'), lax.axis_index('c')
    slot = (lambda px, py, pc: pc) if group == 'c' else (lambda px, py, pc: 2 * px + py)
    n_peers = len(_FLIPS[group])
    out = []
    for k, (fx, fy, fc) in enumerate(_FLIPS[group]):
        peer = (1 - ix if fx else ix, 1 - iy if fy else iy, 1 - ic if fc else ic)
        if mode == 'a2a':
            src = x_ref.at[slot(*peer)]
        elif mode == 'bcast_c':
            src = x_ref.at[ic]
        else:
            src = x_ref
        mk = lambda dst, src=src, k=k, peer=peer: pltpu.make_async_remote_copy(
            src_ref=src, dst_ref=dst, send_sem=sems[k], recv_sem=sems[n_peers + k],
            device_id=peer, device_id_type=MESH)
        out.append((mk(land_ref.at[slot(ix, iy, ic)]), mk(land_ref.at[slot(*peer)])))
    return out


def _xy_start(x, mode, name, after, group='xy'):
    blk = x.shape if mode == 'bcast' else x.shape[1:]
    land_shape = (2 if group == 'c' else 4,) + tuple(blk)
    n_after = len(after)
    n_sems = 2 * len(_FLIPS[group])

    def body(*refs):
        x_ref, land_ref = refs[0], refs[1]
        sems = refs[2 + n_after:2 + n_sems + n_after]
        for send, _ in _split_copies(group, mode, x_ref, land_ref, sems):
            send.start()
        refs[-1][...] = jnp.zeros_like(refs[-1])

    outs = _pcall(
        body, name=name,
        out_shape=(pltpu.SemaphoreType.DMA(()),) * n_sems
        + (pltpu.HBM(x.shape, x.dtype), pltpu.HBM(land_shape, x.dtype), jax.ShapeDtypeStruct((8, LANES), F32)),
        in_specs=(IN_HBM, IN_HBM) + (ANY,) * n_after,
        out_specs=(IN_SEM,) * n_sems + (IN_HBM, IN_HBM, IN_VMEM),
        input_output_aliases={0: n_sems, 1: n_sems + 1},
        compiler_params=pltpu.CompilerParams(has_side_effects=DATAFLOW),
    )(pltpu.with_memory_space_constraint(x, pltpu.HBM),
      pltpu.with_memory_space_constraint(lax.empty(land_shape, x.dtype), pltpu.HBM), *after)
    return dict(sems=outs[:n_sems], sent=outs[n_sems], land=outs[n_sems + 1], token=outs[n_sems + 2], mode=mode,
                group=group)


def _xy_wait(handle, name, after):
    mode, group = handle['mode'], handle['group']
    n_after = len(after)
    n_sems = len(handle['sems'])

    def body(*refs):
        x_ref, land_ref = refs[0], refs[1]
        for _, recv in _split_copies(group, mode, x_ref, land_ref, refs[2:2 + n_sems]):
            recv.wait_send()
            recv.wait_recv()

    sent, land = handle['sent'], handle['land']
    return _pcall(
        body, name=name,
        out_shape=(pltpu.HBM(sent.shape, sent.dtype), pltpu.HBM(land.shape, land.dtype)),
        in_specs=(IN_HBM, IN_HBM) + (IN_SEM,) * n_sems + (ANY,) * n_after,
        out_specs=(IN_HBM, IN_HBM),
        input_output_aliases={0: 0, 1: 1},
        compiler_params=pltpu.CompilerParams(has_side_effects=DATAFLOW),
    )(sent, land, *handle['sems'], *after)


_KIND_GROUP = {'chips_a2a': 'xy', 'chips_gather': 'xy', 'pair_swap': 'c', 'pair_bcast': 'c', 'pair_inplace': 'c'}


def _plan(kind, x_ref, land_ref, sems):
    ix, iy, ic = lax.axis_index('x'), lax.axis_index('y'), lax.axis_index('c')
    flips = _FLIPS[_KIND_GROUP[kind]]
    me_chip = 2 * ix + iy
    out = []
    for k, (fx, fy, fc) in enumerate(flips):
        peer = (1 - ix if fx else ix, 1 - iy if fy else iy, 1 - ic if fc else ic)
        peer_chip = 2 * peer[0] + peer[1]
        if kind == 'chips_a2a':
            src, dst, got = x_ref.at[peer_chip], land_ref.at[me_chip], land_ref.at[peer_chip]
        elif kind == 'chips_gather':
            src, dst, got = x_ref.at[ic], land_ref.at[ic, me_chip], land_ref.at[ic, peer_chip]
        elif kind == 'pair_swap':
            src, dst, got = x_ref.at[1 - ic], land_ref, land_ref
        elif kind == 'pair_bcast':
            src, dst, got = x_ref, land_ref.at[ic], land_ref.at[1 - ic]
        else:
            src, dst, got = land_ref.at[ic], land_ref.at[ic], land_ref.at[1 - ic]
        mk = lambda d, src=src, k=k, peer=peer: pltpu.make_async_remote_copy(
            src_ref=src, dst_ref=d, send_sem=sems[k], recv_sem=sems[len(flips) + k],
            device_id=peer, device_id_type=MESH)
        out.append((mk(dst), mk(got)))
    return out


def _start_many(jobs, name, after):
    per_job = []
    for kind, x, land in jobs:
        if not hasattr(land, 'dtype'):
            land = lax.empty(tuple(land), x.dtype)
        per_job.append((kind, ([] if x is None else [x]) + [land], 2 * len(_FLIPS[_KIND_GROUP[kind]])))
    arrays = [a for _, arrs, _ in per_job for a in arrs]
    n_arr, n_after = len(arrays), len(after)
    n_sems = sum(ns for _, _, ns in per_job)

    def body(*refs):
        a0, s0 = 0, n_arr + n_after
        for kind, arrs, ns in per_job:
            x_ref = refs[a0] if len(arrs) == 2 else None
            for send, _ in _plan(kind, x_ref, refs[a0 + len(arrs) - 1], refs[s0:s0 + ns]):
                send.start()
            a0, s0 = a0 + len(arrs), s0 + ns
        refs[-1][...] = jnp.zeros_like(refs[-1])

    outs = _pcall(
        body, name=name,
        out_shape=(pltpu.SemaphoreType.DMA(()),) * n_sems + tuple(pltpu.HBM(a.shape, a.dtype) for a in arrays)
        + (jax.ShapeDtypeStruct((8, LANES), F32),),
        in_specs=(IN_HBM,) * n_arr + (ANY,) * n_after,
        out_specs=(IN_SEM,) * n_sems + (IN_HBM,) * n_arr + (IN_VMEM,),
        input_output_aliases={i: n_sems + i for i in range(n_arr)},
        compiler_params=pltpu.CompilerParams(has_side_effects=DATAFLOW),
    )(*[pltpu.with_memory_space_constraint(a, pltpu.HBM) for a in arrays], *after)
    handles, a0, s0 = [], n_sems, 0
    for kind, arrs, ns in per_job:
        handles.append(dict(kind=kind, sems=outs[s0:s0 + ns], arrays=outs[a0:a0 + len(arrs)], token=outs[-1]))
        a0, s0 = a0 + len(arrs), s0 + ns
    return handles


def _start(kind, x, land, name, after):
    return _start_many([(kind, x, land)], name, after)[0]


def _wait(handle, name, after):
    kind, arrays, sems = handle['kind'], handle['arrays'], handle['sems']
    n_arr, n_sems = len(arrays), len(sems)

    def body(*refs):
        for _, got in _plan(kind, refs[0] if n_arr == 2 else None, refs[n_arr - 1], refs[n_arr:n_arr + n_sems]):
            got.wait_send()
            got.wait_recv()

    return _pcall(
        body, name=name,
        out_shape=tuple(pltpu.HBM(a.shape, a.dtype) for a in arrays),
        in_specs=(IN_HBM,) * n_arr + (IN_SEM,) * n_sems + (ANY,) * len(after),
        out_specs=(IN_HBM,) * n_arr,
        input_output_aliases={i: i for i in range(n_arr)},
        compiler_params=pltpu.CompilerParams(has_side_effects=DATAFLOW),
    )(*arrays, *sems, *after)


def _sum_pair(x, recv, ic, out_dtype, name):
    _, rows, cols = x.shape
    tm = _tile(rows, max(8, (1 << 20) // cols // 8 * 8), unit=8)

    def body(c_ref, x_ref, r_ref, o_ref):
        o_ref[...] = (x_ref[...].astype(F32) + r_ref[...].astype(F32)).astype(o_ref.dtype)

    return _pcall(
        body, name=name,
        grid_spec=pltpu.PrefetchScalarGridSpec(
            num_scalar_prefetch=1, grid=(rows // tm,),
            in_specs=[pl.BlockSpec((None, tm, cols), lambda i, c: (c[0], i, 0)),
                      pl.BlockSpec((tm, cols), lambda i, c: (i, 0))],
            out_specs=pl.BlockSpec((tm, cols), lambda i, c: (i, 0))),
        out_shape=jax.ShapeDtypeStruct((rows, cols), out_dtype),
        compiler_params=_params('parallel'),
    )(jnp.reshape(ic, (1,)).astype(jnp.int32), x, recv)


def _sum_slots(y, out_dtype, name):
    n, rows, cols = y.shape
    tm = _tile(rows, max(8, (1 << 20) // (n * cols) // 8 * 8), unit=8)

    def body(y_ref, o_ref):
        acc = y_ref[0].astype(F32)
        for k in range(1, n):
            acc = acc + y_ref[k].astype(F32)
        o_ref[...] = acc.astype(o_ref.dtype)

    return _pcall(
        body, name=name, grid=(rows // tm,),
        out_shape=jax.ShapeDtypeStruct((rows, cols), out_dtype),
        in_specs=[pl.BlockSpec((n, tm, cols), lambda i: (0, i, 0))],
        out_specs=pl.BlockSpec((tm, cols), lambda i: (i, 0)),
        compiler_params=_params('parallel'),
    )(y)


class _Op:
    def __init__(self, arr, spec):
        self.arr = arr
        self.spec = spec


def _plain(arr):
    return _Op(arr, lambda t0, t1: ((t0, t1), lambda b0, b1: (b0, b1)))


def _plain_shape(shape):
    return lambda t0, t1: ((t0, t1), lambda b0, b1: (b0, b1))


def _dw_cols(rows, cols):
    rh = rows // 2

    def spec(t0, t1):
        assert (rh % t0 == 0 or t0 == rows) and cols % t1 == 0, (rh, cols, t0, t1)
        qr, qc = max(1, rh // t0), cols // t1
        if t0 == rows:
            return (2, None, rh, t1), lambda b0, b1: (0, b1 // qc, 0, b1 % qc)
        return (None, None, t0, t1), lambda b0, b1: (b0 // qr, b1 // qc, b0 % qr, b1 % qc)
    return (2, 4, rh, cols), spec


def _dw_rows(rows, cols):
    rh = rows // 2

    def spec(t0, t1):
        assert (rh % t0 == 0 or t0 == rows) and cols % t1 == 0, (rh, cols, t0, t1)
        qr = max(1, rh // t0)
        if t0 == rows:
            return (2, None, rh, t1), lambda b0, b1: (0, b0, 0, b1)
        return (None, None, t0, t1), lambda b0, b1: ((b0 // qr) % 2, b0 // (2 * qr), b0 % qr, b1)
    return (2, 4, rh, cols), spec


def _w_cols(g, j):
    _, _, rh, cols = g[j].shape
    return _Op(g[j], _dw_cols(2 * rh, cols)[1])


def _w_rows(g, j):
    _, _, rh, cols = g[j].shape
    return _Op(g[j], _dw_rows(2 * rh, cols)[1])


def _mm(name, mode, a, b, dims, outs, tiles, epilogue=None, extras=(), after=()):
    m, n, k = dims
    tm, tn, tk = tiles
    assert m % tm == 0 and n % tn == 0 and k % tk == 0, (name, dims, tiles)
    nk = k // tk
    if mode == 'nn':
        a_t, a_ix, b_t, b_ix, ca, cb = (tm, tk), (lambda i, j, kk: (i, kk)), (tk, tn), (lambda i, j, kk: (kk, j)), 1, 0
    elif mode == 'nt':
        a_t, a_ix, b_t, b_ix, ca, cb = (tm, tk), (lambda i, j, kk: (i, kk)), (tn, tk), (lambda i, j, kk: (j, kk)), 1, 1
    else:
        a_t, a_ix, b_t, b_ix, ca, cb = (tk, tm), (lambda i, j, kk: (kk, i)), (tk, tn), (lambda i, j, kk: (kk, j)), 0, 0
    a_blk, a_fn = a.spec(*a_t)
    b_blk, b_fn = b.spec(*b_t)
    in_specs = [pl.BlockSpec(a_blk, lambda i, j, kk: a_fn(*a_ix(i, j, kk))),
                pl.BlockSpec(b_blk, lambda i, j, kk: b_fn(*b_ix(i, j, kk)))]
    operands = [a.arr, b.arr]
    for e in extras:
        e_blk, e_fn = e.spec(tm, tn)
        in_specs.append(pl.BlockSpec(e_blk, functools.partial(lambda i, j, kk, f: f(i, j), f=e_fn)))
        operands.append(e.arr)
    out_shapes, out_specs = [], []
    for shape, dtype, spec in outs:
        o_blk, o_fn = spec(tm, tn)
        out_shapes.append(jax.ShapeDtypeStruct(shape, dtype))
        out_specs.append(pl.BlockSpec(o_blk, functools.partial(lambda i, j, kk, f: f(i, j), f=o_fn)))
    n_ex, n_out = len(extras), len(outs)
    in_specs += [ANY] * len(after)
    operands += list(after)
    if epilogue is None:
        epilogue = lambda acc: (acc,)

    def body(a_ref, b_ref, *rest):
        ex_refs = rest[:n_ex]
        rest = rest[:n_ex] + rest[n_ex + len(after):]
        out_refs = rest[n_ex:n_ex + n_out]
        bv = b_ref[...]
        if bv.ndim == 3:
            bv = bv.reshape(bv.shape[0] * bv.shape[1], bv.shape[2])
        p = lax.dot_general(a_ref[...].astype(BF16), bv.astype(BF16),
                            (((ca,), (cb,)), ((), ())), preferred_element_type=F32)

        def finish(acc):
            res = epilogue(acc, *[r[...] for r in ex_refs])
            for o_ref, val in zip(out_refs, res):
                o_ref[...] = val.astype(o_ref.dtype)

        if nk == 1:
            finish(p)
        else:
            acc_ref = rest[n_ex + n_out]
            kk = pl.program_id(2)

            @pl.when(kk == 0)
            def _():
                acc_ref[...] = p

            @pl.when(kk > 0)
            def _():
                acc_ref[...] += p

            @pl.when(kk == nk - 1)
            def _():
                finish(acc_ref[...])

    res = _pcall(
        body, name=name, grid=(m // tm, n // tn, nk),
        out_shape=out_shapes, in_specs=in_specs, out_specs=out_specs,
        scratch_shapes=[pltpu.VMEM((tm, tn), F32)] if nk > 1 else [],
        compiler_params=_params('parallel', 'parallel', 'arbitrary'),
    )(*operands)
    return res


def _rowwise(name, fn, row_ins, vec_ins, outs, reds, tm, after=()):
    rows = row_ins[0].shape[0]
    assert rows % tm == 0
    n_in = len(row_ins) + len(vec_ins)
    n_out = len(outs)

    def body(*refs):
        vals = [r[...] for r in refs[:n_in]]
        refs = refs[:n_in] + refs[n_in + len(after):]
        res = fn(*vals)
        for o_ref, val in zip(refs[n_in:n_in + n_out], res[:n_out]):
            o_ref[...] = val.astype(o_ref.dtype)
        first = pl.program_id(0) == 0
        for r_ref, val in zip(refs[n_in + n_out:], res[n_out:]):
            @pl.when(first)
            def _(r_ref=r_ref, val=val):
                r_ref[...] = val

            @pl.when(jnp.logical_not(first))
            def _(r_ref=r_ref, val=val):
                r_ref[...] += val

    in_specs = [pl.BlockSpec((tm, a.shape[1]), lambda i: (i, 0)) for a in row_ins]
    in_specs += [pl.BlockSpec((1, v.shape[1]), lambda i: (0, 0)) for v in vec_ins]
    in_specs += [ANY] * len(after)
    out_shapes = [jax.ShapeDtypeStruct((rows, w), dt) for w, dt in outs]
    out_shapes += [jax.ShapeDtypeStruct((1, w), F32) for w in reds]
    out_specs = [pl.BlockSpec((tm, w), lambda i: (i, 0)) for w, _ in outs]
    out_specs += [pl.BlockSpec((1, w), lambda i: (0, 0)) for w in reds]
    return _pcall(
        body, name=name, grid=(rows // tm,),
        out_shape=out_shapes, in_specs=in_specs, out_specs=out_specs,
        compiler_params=_params('arbitrary'),
    )(*row_ins, *vec_ins, *after)


def _norm_fwd(h, g, name, after=()):
    def fn(hv, gv):
        rstd = lax.rsqrt(jnp.mean(hv * hv, axis=-1, keepdims=True) + EPS)
        return (hv * rstd * gv,)
    return _rowwise(name, fn, [h], [g], [(h.shape[1], BF16)], [], 256, after)[0]


def _norm_bwd(h, dhn, dres, g, name):
    def fn(hv, dv, rv, gv):
        rstd = lax.rsqrt(jnp.mean(hv * hv, axis=-1, keepdims=True) + EPS)
        xhat = hv * rstd
        dxhat = dv * gv
        dh = rv + rstd * (dxhat - xhat * jnp.mean(dxhat * xhat, axis=-1, keepdims=True))
        return dh, jnp.sum(dv * xhat, axis=0, keepdims=True)
    w = h.shape[1]
    return _rowwise(name, fn, [h, dhn, dres], [g], [(w, F32)], [w], 256)


def _loss_head(h, target, g, name):
    w = h.shape[1]

    def fn(hv, tv, gv):
        rstd = lax.rsqrt(jnp.mean(hv * hv, axis=-1, keepdims=True) + EPS)
        xhat = hv * rstd
        diff = xhat * gv - tv
        dy = diff * (1.0 / w)
        dxhat = dy * gv
        dh = rstd * (dxhat - xhat * jnp.mean(dxhat * xhat, axis=-1, keepdims=True))
        return (dh, jnp.sum(0.5 * dy * diff, axis=0, keepdims=True),
                jnp.sum(dy * xhat, axis=0, keepdims=True))
    return _rowwise(name, fn, [h, target], [g], [(w, F32)], [w, w], 256)


def _split3(x):
    hi = x.astype(BF16)
    r1 = x - hi.astype(F32)
    mid = r1.astype(BF16)
    lo = (r1 - mid.astype(F32)).astype(BF16)
    return hi, mid, lo


def _tri_dot(tri, x):
    hi, mid, lo = _split3(x)
    d = lambda p: jnp.dot(tri, p, preferred_element_type=F32)
    return d(hi) + d(mid) + d(lo)


def _log_sigmoid(x):
    return jnp.minimum(x, 0.0) - jnp.log(1.0 + jnp.exp(-jnp.abs(x)))


def _gla_dims(proj_w, kw, vw, dk, dv):
    assert kw % dk == 0 and (2 * kw) % dv == 0 and (2 * kw + vw) % dv == 0 and (2 * kw + 2 * vw) % LANES == 0
    return dict(q0=0, k0=kw // dk, v0=2 * kw // dv, r0=(2 * kw + vw) // dv, g0=(2 * kw + 2 * vw) // LANES)


def _gla_gates(gl, wgu, bias):
    pre = jnp.dot(gl.astype(BF16), wgu, preferred_element_type=F32) + bias
    la = _log_sigmoid(pre) * (1.0 / GLA_GATE_TEMP)
    r_i = lax.broadcasted_iota(jnp.int32, (CHUNK, CHUNK), 0)
    c_i = lax.broadcasted_iota(jnp.int32, (CHUNK, CHUNK), 1)
    cum = _tri_dot((c_i <= r_i).astype(BF16), la)
    total = cum[CHUNK - 1:CHUNK, :]
    return pre, cum, total


def _gla_scan_fwd(proj, wgu_pad, b_gate, o_norm, heads, kw, vw, tb, name):
    seq, pw = proj.shape
    dk, dv = kw // heads, vw // heads
    cb = tb // CHUNK
    nt = seq // tb
    o = _gla_dims(pw, kw, vw, dk, dv)
    scale = dk ** -0.5

    def body(q_ref, k_ref, v_ref, r_ref, gl_ref, wgu_ref, b_ref, on_ref, out_ref, st_ref, s_scr):
        @pl.when(pl.program_id(1) == 0)
        def _():
            s_scr[...] = jnp.zeros_like(s_scr)

        wgu = wgu_ref[...].astype(BF16)
        bias = b_ref[...]
        onorm = on_ref[...]
        st = s_scr[...]
        for ci in range(cb):
            rows = pl.ds(ci * CHUNK, CHUNK)
            _, cum, total = _gla_gates(gl_ref[rows, :], wgu, bias)
            kdec = k_ref[rows, :] * jnp.exp(total - cum)
            st = st * jnp.exp(total) + lax.dot_general(
                v_ref[rows, :].astype(BF16), kdec.astype(BF16), (((0,), (0,)), ((), ())),
                preferred_element_type=F32)
            st_ref[ci] = st
            qs = (q_ref[rows, :] * scale).astype(BF16)
            ov = lax.dot_general(qs, st.astype(BF16), (((1,), (1,)), ((), ())), preferred_element_type=F32)
            rstd = lax.rsqrt(jnp.mean(ov * ov, axis=-1, keepdims=True) + EPS)
            rv = r_ref[rows, :]
            out_ref[rows, :] = (ov * rstd * onorm * (rv * jax.nn.sigmoid(rv))).astype(out_ref.dtype)
        s_scr[...] = st

    in_specs = [
        pl.BlockSpec((tb, dk), lambda h, t: (t, o['q0'] + h)),
        pl.BlockSpec((tb, dk), lambda h, t: (t, o['k0'] + h)),
        pl.BlockSpec((tb, dv), lambda h, t: (t, o['v0'] + h)),
        pl.BlockSpec((tb, dv), lambda h, t: (t, o['r0'] + h)),
        pl.BlockSpec((tb, LANES), lambda h, t: (t, o['g0'])),
        pl.BlockSpec((LANES, dk), lambda h, t: (0, h)),
        pl.BlockSpec((1, dk), lambda h, t: (0, h)),
        pl.BlockSpec((1, dv), lambda h, t: (0, 0)),
    ]
    return _pcall(
        body, name=name, grid=(heads, nt),
        out_shape=[jax.ShapeDtypeStruct((seq, vw), BF16),
                   jax.ShapeDtypeStruct((heads, seq // CHUNK, dv, dk), F32)],
        in_specs=in_specs,
        out_specs=[pl.BlockSpec((tb, dv), lambda h, t: (t, h)),
                   pl.BlockSpec((None, cb, dv, dk), lambda h, t: (h, t, 0, 0))],
        scratch_shapes=[pltpu.VMEM((dv, dk), F32)],
        compiler_params=_params('parallel', 'arbitrary'),
    )(proj, proj, proj, proj, proj, wgu_pad, b_gate, o_norm)


def _gla_scan_bwd(proj, wgu_pad, b_gate, o_norm, states, dgated, heads, kw, vw, tb, name):
    seq, pw = proj.shape
    dk, dv = kw // heads, vw // heads
    cb = tb // CHUNK
    nt = seq // tb
    o = _gla_dims(pw, kw, vw, dk, dv)
    scale = dk ** -0.5

    def body(q_ref, k_ref, v_ref, r_ref, gl_ref, wgu_ref, b_ref, on_ref, st_ref, stp_ref, dg_ref,
             dq_ref, dk_ref, dv_ref, dr_ref, dpre_ref, db_ref, don_ref, ds_scr):
        hh = pl.program_id(0)
        t = pl.program_id(1)

        @pl.when(t == 0)
        def _():
            ds_scr[...] = jnp.zeros_like(ds_scr)
            db_ref[...] = jnp.zeros_like(db_ref)

        @pl.when(jnp.logical_and(hh == 0, t == 0))
        def _():
            don_ref[...] = jnp.zeros_like(don_ref)

        wgu = wgu_ref[...].astype(BF16)
        bias = b_ref[...]
        onorm = on_ref[...]
        has_prev = (t < nt - 1).astype(F32)
        r_i = lax.broadcasted_iota(jnp.int32, (CHUNK, CHUNK), 0)
        c_i = lax.broadcasted_iota(jnp.int32, (CHUNK, CHUNK), 1)
        strict = (c_i < r_i).astype(BF16)
        carry = ds_scr[...]
        db_acc = jnp.zeros((1, dk), F32)
        don_acc = jnp.zeros((1, dv), F32)
        for ci in reversed(range(cb)):
            rows = pl.ds(ci * CHUNK, CHUNK)
            pre, cum, total = _gla_gates(gl_ref[rows, :], wgu, bias)
            edec = jnp.exp(total - cum)
            decay = jnp.exp(total)
            kdec = k_ref[rows, :] * edec
            st = st_ref[ci]
            st_prev = st_ref[ci - 1] if ci > 0 else stp_ref[0] * has_prev
            stb = st.astype(BF16)
            qs = (q_ref[rows, :] * scale).astype(BF16)
            vb = v_ref[rows, :].astype(BF16)
            ov = lax.dot_general(qs, stb, (((1,), (1,)), ((), ())), preferred_element_type=F32)
            rstd = lax.rsqrt(jnp.mean(ov * ov, axis=-1, keepdims=True) + EPS)
            ohat = ov * rstd
            rv = r_ref[rows, :]
            sr = jax.nn.sigmoid(rv)
            dgv = dg_ref[rows, :]
            dy = dgv * (rv * sr)
            dr_ref[rows, :] = (dgv * (ohat * onorm) * (sr * (1.0 + rv * (1.0 - sr)))).astype(dr_ref.dtype)
            don_acc = don_acc + jnp.sum(dy * ohat, axis=0, keepdims=True)
            dohat = dy * onorm
            do = (rstd * (dohat - ohat * jnp.mean(dohat * ohat, axis=-1, keepdims=True))).astype(BF16)
            dq_ref[rows, :] = (jnp.dot(do, stb, preferred_element_type=F32) * scale).astype(dq_ref.dtype)
            dst = carry + lax.dot_general(do, qs, (((0,), (0,)), ((), ())), preferred_element_type=F32)
            dstb = dst.astype(BF16)
            dkdec = jnp.dot(vb, dstb, preferred_element_type=F32)
            dv_ref[rows, :] = lax.dot_general(kdec.astype(BF16), dstb, (((1,), (1,)), ((), ())),
                                              preferred_element_type=F32).astype(dv_ref.dtype)
            ddecay = jnp.sum(dst * st_prev, axis=0, keepdims=True)
            dk_ref[rows, :] = (dkdec * edec).astype(dk_ref.dtype)
            da = ddecay * decay + _tri_dot(strict, dkdec * kdec)
            dpre = da * (1.0 / GLA_GATE_TEMP) * (1.0 - jax.nn.sigmoid(pre))
            dpre_ref[rows, :] = dpre.astype(dpre_ref.dtype)
            db_acc = db_acc + jnp.sum(dpre, axis=0, keepdims=True)
            carry = dst * decay
        ds_scr[...] = carry
        db_ref[...] += db_acc
        don_ref[...] += don_acc

    rt = lambda t: nt - 1 - t
    in_specs = [
        pl.BlockSpec((tb, dk), lambda h, t: (rt(t), o['q0'] + h)),
        pl.BlockSpec((tb, dk), lambda h, t: (rt(t), o['k0'] + h)),
        pl.BlockSpec((tb, dv), lambda h, t: (rt(t), o['v0'] + h)),
        pl.BlockSpec((tb, dv), lambda h, t: (rt(t), o['r0'] + h)),
        pl.BlockSpec((tb, LANES), lambda h, t: (rt(t), o['g0'])),
        pl.BlockSpec((LANES, dk), lambda h, t: (0, h)),
        pl.BlockSpec((1, dk), lambda h, t: (0, h)),
        pl.BlockSpec((1, dv), lambda h, t: (0, 0)),
        pl.BlockSpec((None, cb, dv, dk), lambda h, t: (h, rt(t), 0, 0)),
        pl.BlockSpec((None, 1, dv, dk), lambda h, t: (h, jnp.maximum(rt(t) * cb - 1, 0), 0, 0)),
        pl.BlockSpec((tb, dv), lambda h, t: (rt(t), h)),
    ]
    out_shape = [jax.ShapeDtypeStruct((seq, kw), BF16), jax.ShapeDtypeStruct((seq, kw), BF16),
                 jax.ShapeDtypeStruct((seq, vw), BF16), jax.ShapeDtypeStruct((seq, vw), BF16),
                 jax.ShapeDtypeStruct((seq, kw), BF16),
                 jax.ShapeDtypeStruct((1, kw), F32), jax.ShapeDtypeStruct((1, dv), F32)]
    out_specs = [pl.BlockSpec((tb, dk), lambda h, t: (rt(t), h)),
                 pl.BlockSpec((tb, dk), lambda h, t: (rt(t), h)),
                 pl.BlockSpec((tb, dv), lambda h, t: (rt(t), h)),
                 pl.BlockSpec((tb, dv), lambda h, t: (rt(t), h)),
                 pl.BlockSpec((tb, dk), lambda h, t: (rt(t), h)),
                 pl.BlockSpec((1, dk), lambda h, t: (0, h)),
                 pl.BlockSpec((1, dv), lambda h, t: (0, 0))]
    return _pcall(
        body, name=name, grid=(heads, nt),
        out_shape=out_shape, in_specs=in_specs, out_specs=out_specs,
        scratch_shapes=[pltpu.VMEM((dv, dk), F32)],
        compiler_params=_params('arbitrary', 'arbitrary'),
    )(proj, proj, proj, proj, proj, wgu_pad, b_gate, o_norm, states, states, dgated)


def _cmul(ar, ai, br, bi):
    return ar * br - ai * bi, ar * bi + ai * br


def _gelu(y):
    c = math.sqrt(2.0 / math.pi)
    return 0.5 * y * (1.0 + jnp.tanh(c * (y + 0.044715 * y * y * y)))


def _gelu_grad(y):
    c = math.sqrt(2.0 / math.pi)
    th = jnp.tanh(c * (y + 0.044715 * y * y * y))
    return 0.5 * (1.0 + th) + 0.5 * y * (1.0 - th * th) * (c * (1.0 + 3.0 * 0.044715 * y * y))


def _power_pow2(ar, ai, n):
    assert n & (n - 1) == 0
    for _ in range(n.bit_length() - 1):
        ar, ai = _cmul(ar, ai, ar, ai)
    return ar, ai


def _block_mask(rows, cols, rb, cb):
    r_i = lax.broadcasted_iota(jnp.int32, (rows, cols), 0)
    c_i = lax.broadcasted_iota(jnp.int32, (rows, cols), 1)
    return r_i // rb == c_i // cb


def _expand_blocks(m, gb):
    rows, b = m.shape
    r_i = lax.broadcasted_iota(jnp.int32, (b, gb * b), 0)
    c_i = lax.broadcasted_iota(jnp.int32, (b, gb * b), 1)
    repeat = (c_i % b == r_i).astype(BF16)
    full = jnp.dot(m.astype(BF16), repeat, preferred_element_type=F32)
    return jnp.where(_block_mask(rows, gb * b, rows // gb, b), full, 0.0).astype(BF16)


def _diagonal_blocks(m, gb):
    rows, cols = m.shape
    b = cols // gb
    masked = jnp.where(_block_mask(rows, cols, rows // gb, b), m, 0.0)
    r_i = lax.broadcasted_iota(jnp.int32, (cols, b), 0)
    c_i = lax.broadcasted_iota(jnp.int32, (cols, b), 1)
    pick = (r_i % b == c_i).astype(BF16)
    return sum(jnp.dot(p, pick, preferred_element_type=F32) for p in _split3(masked))


def _s5_fwd(u, bre, bim, cre, cim, are, aim, dskip, name):
    seq, width = u.shape
    nb, ub, n_state = bre.shape
    gb = S5_GROUPS_PER_BLOCK
    sb = gb * n_state
    ls = seq // S5_SEGMENTS
    seg = S5_SEGMENTS

    def body(u_ref, bre_ref, bim_ref, cre_ref, cim_ref, are_ref, aim_ref, d_ref, y_ref, z_ref, xr_ref, xi_ref):
        uv = u_ref[...]
        ub16 = uv.astype(BF16)
        xr_ref[...] = jnp.dot(ub16, _expand_blocks(bre_ref[...], gb), preferred_element_type=F32)
        xi_ref[...] = jnp.dot(ub16, _expand_blocks(bim_ref[...], gb), preferred_element_type=F32)
        ar = jnp.broadcast_to(are_ref[...], (seg, sb))
        ai = jnp.broadcast_to(aim_ref[...], (seg, sb))

        def step(i, c):
            rows = pl.ds(pl.multiple_of(i * seg, seg), seg)
            pr, pi = _cmul(ar, ai, c[0], c[1])
            nr = pr + xr_ref[rows, :]
            ni = pi + xi_ref[rows, :]
            xr_ref[rows, :] = nr
            xi_ref[rows, :] = ni
            return nr, ni

        zero = jnp.zeros((seg, sb), F32)
        er, ei = lax.fori_loop(0, ls, step, (zero, zero), unroll=8)
        pr, pi = _power_pow2(ar, ai, ls)
        row = lax.broadcasted_iota(jnp.int32, (seg, sb), 0)
        sr, si = zero, zero
        for _ in range(seg - 1):
            tr, ti = _cmul(pr, pi, sr, si)
            sr = jnp.where(row == 0, 0.0, pltpu.roll(tr + er, 1, 0))
            si = jnp.where(row == 0, 0.0, pltpu.roll(ti + ei, 1, 0))

        def fix(i, c):
            rows = pl.ds(pl.multiple_of(i * seg, seg), seg)
            fr, fi = _cmul(c[0], c[1], sr, si)
            xr_ref[rows, :] += fr
            xi_ref[rows, :] += fi
            return _cmul(c[0], c[1], ar, ai)

        lax.fori_loop(0, ls, fix, (ar, ai), unroll=8)
        y = (jnp.dot(xr_ref[...].astype(BF16), _expand_blocks(cre_ref[...], gb), preferred_element_type=F32)
             - jnp.dot(xi_ref[...].astype(BF16), _expand_blocks(cim_ref[...], gb), preferred_element_type=F32)
             + d_ref[...] * uv)
        y_ref[...] = y
        z_ref[...] = _gelu(y).astype(z_ref.dtype)

    mat = lambda r, c: pl.BlockSpec((None, r, c), lambda b: (b, 0, 0))
    return _pcall(
        body, name=name, grid=(nb,),
        out_shape=[jax.ShapeDtypeStruct((seq, width), F32), jax.ShapeDtypeStruct((seq, width), BF16),
                   jax.ShapeDtypeStruct((seq, nb * sb), F32), jax.ShapeDtypeStruct((seq, nb * sb), F32)],
        in_specs=[pl.BlockSpec((seq, ub), lambda b: (0, b)), mat(ub, n_state), mat(ub, n_state),
                  mat(sb, ub // gb), mat(sb, ub // gb),
                  mat(1, sb), mat(1, sb), pl.BlockSpec((1, ub), lambda b: (0, b))],
        out_specs=[pl.BlockSpec((seq, ub), lambda b: (0, b)), pl.BlockSpec((seq, ub), lambda b: (0, b)),
                   pl.BlockSpec((seq, sb), lambda b: (0, b)), pl.BlockSpec((seq, sb), lambda b: (0, b))],
        compiler_params=_params('parallel'),
    )(u, bre, bim, cre, cim, are, aim, dskip)


def _s5_bwd(dz, y, u, xr, xi, bre, bim, cre, cim, are, aim, dskip, name):
    seq, width = u.shape
    nb, ub, n_state = bre.shape
    gb = S5_GROUPS_PER_BLOCK
    sb = gb * n_state
    ls = seq // S5_SEGMENTS
    seg = S5_SEGMENTS

    def body(dz_ref, y_ref, u_ref, xr_ref, xi_ref, bre_ref, bim_ref, cre_ref, cim_ref, are_ref, aim_ref, d_ref,
             du_ref, dcr_ref, dci_ref, dbr_ref, dbi_ref, dar_ref, dai_ref, dd_ref, lr_ref, li_ref):
        uv = u_ref[...]
        dy = dz_ref[...] * _gelu_grad(y_ref[...])
        dd_ref[...] = jnp.sum(dy * uv, axis=0, keepdims=True)
        dyb = dy.astype(BF16)
        nt = (((1,), (1,)), ((), ()))
        tn = (((0,), (0,)), ((), ()))
        lr_ref[...] = lax.dot_general(dyb, _expand_blocks(cre_ref[...], gb), nt, preferred_element_type=F32)
        li_ref[...] = -lax.dot_general(dyb, _expand_blocks(cim_ref[...], gb), nt, preferred_element_type=F32)
        dcr_ref[...] = _diagonal_blocks(
            lax.dot_general(dyb, xr_ref[...].astype(BF16), tn, preferred_element_type=F32), gb)
        dci_ref[...] = -_diagonal_blocks(
            lax.dot_general(dyb, xi_ref[...].astype(BF16), tn, preferred_element_type=F32), gb)
        ar = jnp.broadcast_to(are_ref[...], (seg, sb))
        ai = jnp.broadcast_to(aim_ref[...], (seg, sb))
        nai = -ai

        def step(ii, c):
            rows = pl.ds(pl.multiple_of((ls - 1 - ii) * seg, seg), seg)
            pr, pi = _cmul(ar, nai, c[0], c[1])
            nr = pr + lr_ref[rows, :]
            ni = pi + li_ref[rows, :]
            lr_ref[rows, :] = nr
            li_ref[rows, :] = ni
            return nr, ni

        zero = jnp.zeros((seg, sb), F32)
        er, ei = lax.fori_loop(0, ls, step, (zero, zero), unroll=8)
        pr, pi = _power_pow2(ar, nai, ls)
        row = lax.broadcasted_iota(jnp.int32, (seg, sb), 0)
        rr, ri = zero, zero
        for _ in range(seg - 1):
            tr, ti = _cmul(pr, pi, rr, ri)
            rr = jnp.where(row == seg - 1, 0.0, pltpu.roll(tr + er, seg - 1, 0))
            ri = jnp.where(row == seg - 1, 0.0, pltpu.roll(ti + ei, seg - 1, 0))

        def corrected(rows, qr, qi):
            fr, fi = _cmul(qr, qi, rr, ri)
            nr = lr_ref[rows, :] + fr
            ni = li_ref[rows, :] + fi
            lr_ref[rows, :] = nr
            li_ref[rows, :] = ni
            return nr, ni

        def grad_a(nr, ni, xpr, xpi, accr, acci):
            return accr + nr * xpr + ni * xpi, acci + ni * xpr - nr * xpi

        def fix(ii, c):
            qr, qi, accr, acci = c
            i = ls - 1 - ii
            rows = pl.ds(pl.multiple_of(i * seg, seg), seg)
            prev = pl.ds(pl.multiple_of((i - 1) * seg, seg), seg)
            nr, ni = corrected(rows, qr, qi)
            accr, acci = grad_a(nr, ni, xr_ref[prev, :], xi_ref[prev, :], accr, acci)
            qr, qi = _cmul(qr, qi, ar, nai)
            return qr, qi, accr, acci

        qr, qi, accr, acci = lax.fori_loop(0, ls - 1, fix, (ar, nai, zero, zero), unroll=8)
        nr, ni = corrected(pl.ds(0, seg), qr, qi)
        last = pl.ds((ls - 1) * seg, seg)
        xpr = jnp.where(row == 0, 0.0, pltpu.roll(xr_ref[last, :], 1, 0))
        xpi = jnp.where(row == 0, 0.0, pltpu.roll(xi_ref[last, :], 1, 0))
        accr, acci = grad_a(nr, ni, xpr, xpi, accr, acci)
        dar_ref[...] = jnp.sum(accr, axis=0, keepdims=True)
        dai_ref[...] = jnp.sum(acci, axis=0, keepdims=True)
        lrb = lr_ref[...].astype(BF16)
        lib = li_ref[...].astype(BF16)
        ub16 = uv.astype(BF16)
        dbr_ref[...] = _diagonal_blocks(lax.dot_general(ub16, lrb, tn, preferred_element_type=F32), gb)
        dbi_ref[...] = _diagonal_blocks(lax.dot_general(ub16, lib, tn, preferred_element_type=F32), gb)
        du_ref[...] = (d_ref[...] * dy
                       + lax.dot_general(lrb, _expand_blocks(bre_ref[...], gb), nt, preferred_element_type=F32)
                       + lax.dot_general(lib, _expand_blocks(bim_ref[...], gb), nt, preferred_element_type=F32))

    mat = lambda r, c: pl.BlockSpec((None, r, c), lambda b: (b, 0, 0))
    col = lambda w: pl.BlockSpec((seq, w), lambda b: (0, b))
    return _pcall(
        body, name=name, grid=(nb,),
        out_shape=[jax.ShapeDtypeStruct((seq, width), F32)]
        + [jax.ShapeDtypeStruct((nb, ub, n_state), F32)] * 4
        + [jax.ShapeDtypeStruct((nb, 1, sb), F32)] * 2
        + [jax.ShapeDtypeStruct((1, width), F32)],
        in_specs=[col(ub), col(ub), col(ub), col(sb), col(sb), mat(ub, n_state), mat(ub, n_state),
                  mat(sb, ub // gb), mat(sb, ub // gb),
                  mat(1, sb), mat(1, sb), pl.BlockSpec((1, ub), lambda b: (0, b))],
        out_specs=[col(ub), mat(ub, n_state), mat(ub, n_state), mat(ub, n_state), mat(ub, n_state),
                   mat(1, sb), mat(1, sb), pl.BlockSpec((1, ub), lambda b: (0, b))],
        scratch_shapes=[pltpu.VMEM((seq, sb), F32), pltpu.VMEM((seq, sb), F32)],
        compiler_params=_params('parallel'),
    )(dz, y, u, xr, xi, bre, bim, cre, cim, are, aim, dskip)


def _s5_discretise(lam_re, lam_im, log_dt, b_re, b_im):
    lr = jnp.minimum(lam_re, S5_EIG_CLIP)
    li = lam_im
    dt = jnp.exp(log_dt)[:, None]
    mag = jnp.exp(lr * dt)
    ang = li * dt
    ab_re = mag * jnp.cos(ang)
    ab_im = mag * jnp.sin(ang)
    den = lr * lr + li * li
    nr = ab_re - 1.0
    f_re = (nr * lr + ab_im * li) / den
    f_im = (ab_im * lr - nr * li) / den
    bb_re = f_re[..., None] * b_re - f_im[..., None] * b_im
    bb_im = f_re[..., None] * b_im + f_im[..., None] * b_re
    return ab_re, ab_im, bb_re, bb_im


def _glu_fwd(o, h, name):
    half = o.shape[1] // 2

    def fn(ov, hv):
        return (hv + ov[:, :half] * jax.nn.sigmoid(ov[:, half:]),)
    return _rowwise(name, fn, [o, h], [], [(half, F32)], [], 256)[0]


def _glu_bwd(o, dout, name):
    half = o.shape[1] // 2

    def fn(ov, dv):
        val, gate = ov[:, :half], ov[:, half:]
        sg = jax.nn.sigmoid(gate)
        return (jnp.concatenate([dv * sg, dv * val * sg * (1.0 - sg)], axis=1),)
    return _rowwise(name, fn, [o, dout], [], [(2 * half, BF16)], [], 256)[0]


def _adam_math(w, g, m, v):
    m = ADAM_B1 * m + (1.0 - ADAM_B1) * g
    v = ADAM_B2 * v + (1.0 - ADAM_B2) * (g * g)
    m_hat = m / (1.0 - ADAM_B1 ** ADAM_STEP)
    v_hat = v / (1.0 - ADAM_B2 ** ADAM_STEP)
    delta = -ADAM_LR * (m_hat / (jnp.sqrt(v_hat) + ADAM_EPS) + ADAM_WD * w)
    return delta, m, v


def _adamw(w, m, v, grads, name, after=()):
    nl, rows, cols = w.shape
    tm = _tile(rows, max(8, (1 << 18) // cols // 8 * 8), unit=8)
    nbk = rows // tm

    def body(*refs):
        w_ref, m_ref, v_ref = refs[:3]
        g_refs = refs[3:3 + nl]
        go_ref, d_ref, mo_ref, vo_ref = refs[3 + nl + len(after):]
        layer = pl.program_id(0)
        g = g_refs[0][...]
        for l in range(1, nl):
            g = jnp.where(layer == l, g_refs[l][...], g)
        delta, mn, vn = _adam_math(w_ref[...], g, m_ref[...], v_ref[...])
        go_ref[...] = g
        d_ref[...] = delta
        mo_ref[...] = mn
        vo_ref[...] = vn

    stacked = pl.BlockSpec((None, tm, cols), lambda l, i: (l, i, 0))

    def g_spec(layer):
        return pl.BlockSpec((tm, cols), lambda l, i: (jnp.where(l == layer, i, jnp.where(l < layer, 0, nbk - 1)), 0))

    return _pcall(
        body, name=name, grid=(nl, nbk),
        out_shape=[jax.ShapeDtypeStruct(w.shape, F32)] * 4,
        in_specs=[stacked] * 3 + [g_spec(l) for l in range(nl)] + [ANY] * len(after),
        out_specs=[stacked] * 4,
        compiler_params=_params('arbitrary', 'arbitrary'),
    )(w, m, v, *grads, *after)


def _adamw_t(w, m, v, g, name, after=()):
    cols, nl, rows = w.shape
    budget = max(1, (1 << 21) // (8 * rows * 4))
    tc = max(t for t in range(1, min(cols, budget) + 1) if cols % t == 0)

    def body(*refs):
        w_ref, m_ref, v_ref, g_ref = refs[:4]
        d_ref, mo_ref, vo_ref = refs[4 + len(after):]
        delta, mn, vn = _adam_math(w_ref[...], g_ref[...], m_ref[...], v_ref[...])
        d_ref[...] = delta
        mo_ref[...] = mn
        vo_ref[...] = vn

    blk = pl.BlockSpec((tc, nl, rows), lambda i: (i, 0, 0))
    return _pcall(
        body, name=name, grid=(cols // tc,),
        out_shape=[jax.ShapeDtypeStruct(w.shape, F32)] * 3,
        in_specs=[blk] * 4 + [ANY] * len(after), out_specs=[blk] * 3,
        compiler_params=_params('parallel'),
    )(w, m, v, g, *after)


def _pack(arrs, rows_mult=512):
    flat = jnp.concatenate([a.reshape(-1) for a in arrs])
    total = flat.shape[0]
    rows = -(-total // LANES)
    rows = -(-rows // rows_mult) * rows_mult
    flat = jnp.pad(flat, (0, rows * LANES - total))
    return flat.reshape(rows, LANES)


def _unpack(packed, shapes):
    flat = packed.reshape(-1)
    out, off = [], 0
    for s in shapes:
        size = math.prod(s)
        out.append(flat[off:off + size].reshape(s))
        off += size
    return out


def _permute(a):
    seq, w = a.shape
    return a.reshape(S5_SEGMENTS, seq // S5_SEGMENTS, w).transpose(1, 0, 2).reshape(seq, w)


def _unpermute(a):
    seq, w = a.shape
    return a.reshape(seq // S5_SEGMENTS, S5_SEGMENTS, w).transpose(1, 0, 2).reshape(seq, w)


def kernel(x, gla_norm, gla_w_in, gla_w_gate_up, gla_b_gate, gla_o_norm, gla_w_out, s5_norm, s5_w_in, s5_lam_re, s5_lam_im, s5_log_dt, s5_b_re, s5_b_im, s5_c_re, s5_c_im, s5_d, s5_w_out, mlp_norm, mlp_w_up, mlp_w_down, final_norm, loss_target, m_gla_norm, m_gla_w_in, m_gla_w_gate_up, m_gla_b_gate, m_gla_o_norm, m_gla_w_out, m_s5_norm, m_s5_w_in, m_s5_lam_re, m_s5_lam_im, m_s5_log_dt, m_s5_b_re, m_s5_b_im, m_s5_c_re, m_s5_c_im, m_s5_d, m_s5_w_out, m_mlp_norm, m_mlp_w_up, m_mlp_w_down, m_final_norm, v_gla_norm, v_gla_w_in, v_gla_w_gate_up, v_gla_b_gate, v_gla_o_norm, v_gla_w_out, v_s5_norm, v_s5_w_in, v_s5_lam_re, v_s5_lam_im, v_s5_log_dt, v_s5_b_re, v_s5_b_im, v_s5_c_re, v_s5_c_im, v_s5_d, v_s5_w_out, v_mlp_norm, v_mlp_w_up, v_mlp_w_down, v_final_norm):
    weights = dict(gla_norm=gla_norm, gla_w_in=gla_w_in, gla_w_gate_up=gla_w_gate_up, gla_b_gate=gla_b_gate, gla_o_norm=gla_o_norm, gla_w_out=gla_w_out, s5_norm=s5_norm, s5_w_in=s5_w_in, s5_lam_re=s5_lam_re, s5_lam_im=s5_lam_im, s5_log_dt=s5_log_dt, s5_b_re=s5_b_re, s5_b_im=s5_b_im, s5_c_re=s5_c_re, s5_c_im=s5_c_im, s5_d=s5_d, s5_w_out=s5_w_out, mlp_norm=mlp_norm, mlp_w_up=mlp_w_up, mlp_w_down=mlp_w_down, final_norm=final_norm)
    mom1 = dict(gla_norm=m_gla_norm, gla_w_in=m_gla_w_in, gla_w_gate_up=m_gla_w_gate_up, gla_b_gate=m_gla_b_gate, gla_o_norm=m_gla_o_norm, gla_w_out=m_gla_w_out, s5_norm=m_s5_norm, s5_w_in=m_s5_w_in, s5_lam_re=m_s5_lam_re, s5_lam_im=m_s5_lam_im, s5_log_dt=m_s5_log_dt, s5_b_re=m_s5_b_re, s5_b_im=m_s5_b_im, s5_c_re=m_s5_c_re, s5_c_im=m_s5_c_im, s5_d=m_s5_d, s5_w_out=m_s5_w_out, mlp_norm=m_mlp_norm, mlp_w_up=m_mlp_w_up, mlp_w_down=m_mlp_w_down, final_norm=m_final_norm)
    mom2 = dict(gla_norm=v_gla_norm, gla_w_in=v_gla_w_in, gla_w_gate_up=v_gla_w_gate_up, gla_b_gate=v_gla_b_gate, gla_o_norm=v_gla_o_norm, gla_w_out=v_gla_w_out, s5_norm=v_s5_norm, s5_w_in=v_s5_w_in, s5_lam_re=v_s5_lam_re, s5_lam_im=v_s5_lam_im, s5_log_dt=v_s5_log_dt, s5_b_re=v_s5_b_re, s5_b_im=v_s5_b_im, s5_c_re=v_s5_c_re, s5_c_im=v_s5_c_im, s5_d=v_s5_d, s5_w_out=v_s5_w_out, mlp_norm=v_mlp_norm, mlp_w_up=v_mlp_w_up, mlp_w_down=v_mlp_w_down, final_norm=v_final_norm)
    names = list(weights)
    big = ['gla_w_in', 'gla_w_out', 's5_w_in', 's5_w_out', 'mlp_w_up', 'mlp_w_down']
    small = [n for n in names if n not in big]

    chip = 2 * lax.axis_index('x') + lax.axis_index('y')
    h0 = x[0]
    target = loss_target[0]
    seq, dm = h0.shape
    depth = mlp_norm.shape[0]
    n_gla = gla_norm.shape[0]
    n_s5 = s5_lam_re.shape[0]
    rank = gla_w_gate_up.shape[1]
    kw = gla_b_gate.shape[1]
    dv = gla_o_norm.shape[1]
    in_w = 4 * gla_w_in.shape[2]
    vw = (in_w - rank - 2 * kw) // 2
    heads = vw // dv
    dk = kw // heads
    pw = -(-in_w // LANES) * LANES
    s5w = s5_w_in.shape[2]
    n_grp, n_state, grp = s5_b_re.shape[1:]
    hid = 4 * mlp_w_up.shape[2]
    tb = min(seq, 8 * CHUNK)
    tm = _tile(seq, 1024)

    ic = lax.axis_index('c')
    rh = lambda w: w.shape[1] // 2
    wb16 = {n: weights[n].astype(BF16) for n in big}
    gathered = {n: [None] * weights[n].shape[0] for n in big}
    g_w_in, g_gla_out, g_s5_in, g_s5_out, g_up, g_down = (gathered[n] for n in big)
    in_flight = {}

    to_start = []
    for i in range(depth):
        mix = ['gla_w_in', 'gla_w_out'] if i % 2 == 0 else ['s5_w_in', 's5_w_out']
        to_start += [(m, i // 2) for m in mix] + [('mlp_w_up', i), ('mlp_w_down', i)]

    def next_gathers(room):
        keys, jobs = [], []
        while to_start and room > 0:
            n, l = to_start.pop(0)
            rows, cols = weights[n].shape[1:]
            keys.append((n, l))
            jobs.append(('chips_gather', wb16[n][l].reshape(2, rows // 2, cols), (2, 4, rows // 2, cols)))
            room -= 1
        return keys, jobs

    def finish_gather(n, l, after):
        halves, land = _wait(in_flight.pop((n, l)), f'ag_{n}_{l}_wait', after)
        own = lax.dynamic_index_in_dim(halves, ic, 0, keepdims=True)
        land = lax.dynamic_update_slice(land, own[:, None], (ic, chip, 0, 0))
        keys, jobs = next_gathers(GATHERS_IN_FLIGHT - len(in_flight))
        handles = _start_many([('pair_inplace', None, land)] + jobs, f'ag_{n}_{l}_pair_start', [])
        in_flight.update(zip(keys, handles[1:]))
        gathered[n][l] = _wait(handles[0], f'ag_{n}_{l}_pair_wait', [])[0]
        return [gathered[n][l]]

    sharded_small = [gla_w_gate_up, s5_norm, s5_d]
    gathered_small = _exchange(_pack(sharded_small), 'xy', 'bcast', 'ag_small')
    keys, jobs = next_gathers(GATHERS_IN_FLIGHT)
    in_flight.update(zip(keys, _start_many(jobs, 'ag_first_start', [gathered_small])))
    parts = [_unpack(gathered_small[k], [a.shape for a in sharded_small]) for k in range(4)]
    wgu_full = jnp.concatenate([p[0] for p in parts], axis=2)
    s5_norm_full = jnp.concatenate([p[1] for p in parts], axis=1)
    s5_d_full = jnp.concatenate([p[2] for p in parts], axis=1)

    def gla_w_in_padded(j):
        wj = g_w_in[j].transpose(0, 2, 1, 3).reshape(dm, in_w)
        return jnp.pad(wj, ((0, 0), (0, pw - in_w)))

    grads = {n: [None] * weights[n].shape[0] for n in names if n != 'final_norm'}

    saved = []
    h = h0
    for i in range(depth):
        j = i // 2
        rec = {}
        if i % 2 == 0:
            rec['h_in'] = h
            behind = finish_gather('gla_w_in', j, [h])
            hn = _norm_fwd(h, gla_norm[j:j + 1], 'gla_norm_fwd', behind)
            w_in_pad = gla_w_in_padded(j)
            proj = _mm('gla_proj', 'nn', _plain(hn), _plain(w_in_pad), (seq, pw, dm),
                       [((seq, pw), F32, _plain_shape(None))], (tm, _tile(pw, 1024), dm))[0]
            wgu_pad = jnp.pad(wgu_full[j], ((0, LANES - rank), (0, 0)))
            gated, states = _gla_scan_fwd(proj, wgu_pad, gla_b_gate[j:j + 1], gla_o_norm[j:j + 1],
                                          heads, kw, vw, tb, 'gla_scan_fwd')
            behind = finish_gather('gla_w_out', j, [gated])
            h = _mm('gla_out', 'nn', _plain(gated), _w_rows(g_gla_out, j), (seq, dm, vw),
                    [((seq, dm), F32, _plain_shape(None))],
                    (tm, _tile(dm, 1024), gla_w_out.shape[1]),
                    epilogue=lambda acc, hv: (acc + hv,), extras=[_plain(h)], after=behind)[0]
            rec.update(hn=hn, w_in_pad=w_in_pad, proj=proj, wgu_pad=wgu_pad, gated=gated, states=states)
        else:
            hp = _permute(h)
            rec['h_in'] = hp
            hn = _norm_fwd(hp, s5_norm_full[j:j + 1], 's5_norm_fwd')
            behind = finish_gather('s5_w_in', j, [hn])
            u = _mm('s5_in', 'nn', _plain(hn), _w_rows(g_s5_in, j), (seq, s5w, dm),
                    [((seq, s5w), F32, _plain_shape(None))],
                    (tm, _tile(s5w, 1024), s5_w_in.shape[1]), after=behind)[0]
            disc, disc_vjp = jax.vjp(_s5_discretise, s5_lam_re[j], s5_lam_im[j], s5_log_dt[j], s5_b_re[j], s5_b_im[j])
            ab_re, ab_im, bb_re, bb_im = disc
            nb = n_grp // S5_GROUPS_PER_BLOCK
            per_block = lambda m: m.transpose(0, 2, 1).reshape(nb, -1, m.shape[1])
            bre, bim = per_block(bb_re), per_block(bb_im)
            cre, cim = per_block(s5_c_re[j]), per_block(s5_c_im[j])
            are = ab_re.reshape(nb, 1, S5_GROUPS_PER_BLOCK * n_state)
            aim = ab_im.reshape(nb, 1, S5_GROUPS_PER_BLOCK * n_state)
            dskip = s5_d_full[j:j + 1]
            y, z, xr, xi = _s5_fwd(u, bre, bim, cre, cim, are, aim, dskip, 's5_scan_fwd')
            behind = finish_gather('s5_w_out', j, [z])
            o = _mm('s5_out', 'nn', _plain(z), _w_cols(g_s5_out, j), (seq, 2 * dm, s5w),
                    [((seq, 2 * dm), F32, _plain_shape(None))],
                    (tm, _tile(s5_w_out.shape[2], 1024), s5_w_out.shape[1]), after=behind)[0]
            h = _unpermute(_glu_fwd(o, hp, 's5_glu_fwd'))
            rec.update(hn=hn, u=u, y=y, z=z, xr=xr, xi=xi, o=o, mats=(bre, bim, cre, cim, are, aim, dskip),
                       disc_vjp=disc_vjp)
        rec['h_mid'] = h
        hn2 = _norm_fwd(h, mlp_norm[i:i + 1], 'mlp_norm_fwd')
        behind = finish_gather('mlp_w_up', i, [hn2])
        act, act2 = _mm('mlp_up', 'nn', _plain(hn2), _w_cols(g_up, i), (seq, hid, dm),
                        [((seq, hid), BF16, _plain_shape(None))] * 2,
                        (tm, _tile(mlp_w_up.shape[2], 1024), mlp_w_up.shape[1]),
                        epilogue=lambda acc: (jnp.maximum(acc, 0.0), jnp.square(jnp.maximum(acc, 0.0))),
                        after=behind)
        behind = finish_gather('mlp_w_down', i, [act2])
        h = _mm('mlp_down', 'nn', _plain(act2), _w_rows(g_down, i), (seq, dm, hid),
                [((seq, dm), F32, _plain_shape(None))],
                (tm, _tile(dm, 1024), mlp_w_down.shape[1]),
                epilogue=lambda acc, hv: (acc + hv,), extras=[_plain(h)], after=behind)[0]
        rec.update(hn2=hn2, act=act, act2=act2)
        saved.append(rec)

    dh, loss_cols, d_final = _loss_head(h, target, final_norm.reshape(1, dm), 'loss_head')
    loss = lax.psum(jnp.sum(loss_cols), ('x', 'y', 'c'))
    grads['final_norm'] = [d_final.reshape(dm)]

    big_grads = {n: [None] * weights[n].shape[0] for n in big}
    reducing = []

    def reduce_next(item, after):
        n, l, hd = item['n'], item['l'], item['hd']
        if item['stage'] == 'pair':
            dw, got = _wait(hd, f'rs_{n}_{l}_pair_wait', after)
            _, _, rows_h, cols = dw.shape
            pre = _sum_pair(dw.reshape(2, 4 * rows_h, cols), got.reshape(4 * rows_h, cols), ic, BF16,
                            f'rs_{n}_pairsum').reshape(4, rows_h, cols)
            item['stage'] = 'chips'
            return ('chips_a2a', pre, pre.shape)
        if item['stage'] == 'chips':
            pre, yb = _wait(hd, f'rs_{n}_{l}_wait', after)
            yb = lax.dynamic_update_index_in_dim(yb, lax.dynamic_index_in_dim(pre, chip, 0, keepdims=False), chip, 0)
            fin = _sum_slots(yb, F32, f'rs_{n}_chipsum')
            item['stage'] = 'back'
            return ('pair_bcast', fin, (2,) + fin.shape)
        fin, both = _wait(hd, f'rs_{n}_{l}_back_wait', after)
        both = lax.dynamic_update_index_in_dim(both, fin, ic, 0)
        big_grads[n][l] = both.reshape(2 * fin.shape[0], fin.shape[1])
        item.update(stage='done', hd=None)
        return None

    def reduce_many(items, name, after):
        jobs = [(item, reduce_next(item, after)) for item in items]
        jobs = [(item, job) for item, job in jobs if job is not None]
        if not jobs:
            return after
        for (item, _), hd in zip(jobs, _start_many([job for _, job in jobs], name, after)):
            item['hd'] = hd
        return [jobs[0][0]['hd']['token']]

    def reduce_scatter(dw, n, l):
        new = dict(n=n, l=l, stage='new', hd=None)
        older = [reducing[-k] for k in (1, 3) if len(reducing) >= k]
        jobs = [(item, reduce_next(item, [dw])) for item in older] + [(new, ('pair_swap', dw, dw.shape[1:]))]
        new['stage'] = 'pair'
        reducing.append(new)
        for (item, _), hd in zip(jobs, _start_many([job for _, job in jobs], f'rs_{n}_{l}_start', [])):
            item['hd'] = hd
        return [new['hd']['token']]

    for i in reversed(range(depth)):
        j = i // 2
        rec = saved[i]
        r_dn, c_dn = mlp_w_down.shape[1:]
        shape, spec = _dw_rows(r_dn, c_dn)
        dw = _mm('mlp_down_dw', 'tn', _plain(rec['act2']), _plain(dh), (hid, dm, seq),
                 [(shape, BF16, spec)], (_tile(r_dn // 2, 1024), _tile(c_dn, 1024), seq))[0]
        behind = reduce_scatter(dw, 'mlp_w_down', i)
        dpre = _mm('mlp_down_dx', 'nt', _plain(dh), _w_rows(g_down, i), (seq, hid, dm),
                   [((seq, hid), BF16, _plain_shape(None))],
                   (tm, _tile(rh(mlp_w_down), 1024), dm),
                   epilogue=lambda acc, av: (acc * (2.0 * av.astype(F32)),), extras=[_plain(rec['act'])],
                   after=behind)[0]
        r_up, c_up = mlp_w_up.shape[1:]
        shape, spec = _dw_cols(r_up, c_up)
        dw = _mm('mlp_up_dw', 'tn', _plain(rec['hn2']), _plain(dpre), (dm, hid, seq),
                 [(shape, BF16, spec)], (_tile(r_up // 2, 1024), _tile(c_up, 1024), seq))[0]
        behind = reduce_scatter(dw, 'mlp_w_up', i)
        dhn = _mm('mlp_up_dx', 'nt', _plain(dpre), _w_cols(g_up, i), (seq, dm, hid),
                  [((seq, dm), F32, _plain_shape(None))],
                  (tm, _tile(rh(mlp_w_up), 1024), _tile(mlp_w_up.shape[2], 2048)), after=behind)[0]
        dh, dg = _norm_bwd(rec['h_mid'], dhn, dh, mlp_norm[i:i + 1], 'mlp_norm_bwd')
        grads['mlp_norm'][i] = dg[0]

        if i % 2 == 0:
            r_o, c_o = gla_w_out.shape[1:]
            shape, spec = _dw_rows(r_o, c_o)
            dw = _mm('gla_out_dw', 'tn', _plain(rec['gated']), _plain(dh), (vw, dm, seq),
                     [(shape, BF16, spec)], (_tile(r_o // 2, 1024), _tile(c_o, 1024), seq))[0]
            behind = reduce_scatter(dw, 'gla_w_out', j)
            dgated = _mm('gla_out_dx', 'nt', _plain(dh), _w_rows(g_gla_out, j), (seq, vw, dm),
                         [((seq, vw), F32, _plain_shape(None))],
                         (tm, _tile(rh(gla_w_out), 1024), dm), after=behind)[0]
            dq, dkk, dvv, dr, dpre_g, db, don = _gla_scan_bwd(
                rec['proj'], rec['wgu_pad'], gla_b_gate[j:j + 1], gla_o_norm[j:j + 1], rec['states'], dgated,
                heads, kw, vw, tb, 'gla_scan_bwd')
            grads['gla_b_gate'][j] = db[0]
            grads['gla_o_norm'][j] = don[0]
            dgl = _mm('gla_gate_dx', 'nt', _plain(dpre_g), _plain(rec['wgu_pad']), (seq, LANES, kw),
                      [((seq, LANES), BF16, _plain_shape(None))], (tm, LANES, kw))[0]
            g_low = rec['proj'][:, pw - LANES:]
            dwgu = _mm('gla_gate_dw', 'tn', _plain(g_low), _plain(dpre_g), (LANES, kw, seq),
                       [((LANES, kw), F32, _plain_shape(None))], (LANES, kw, seq))[0]
            grads['gla_w_gate_up'][j] = dwgu[:rank]
            dproj = jnp.concatenate([dq, dkk, dvv, dr, dgl], axis=1)
            dw_pad = _mm('gla_proj_dw', 'tn', _plain(rec['hn']), _plain(dproj), (dm, pw, seq),
                         [((dm, pw), BF16, _plain_shape(None))], (_tile(dm, 1024), _tile(pw, 1024), seq))[0]
            shard_w = in_w // 4
            dw = dw_pad[:, :in_w].reshape(2, dm // 2, 4, shard_w).transpose(0, 2, 1, 3)
            behind = reduce_scatter(dw, 'gla_w_in', j)
            dhn = _mm('gla_proj_dx', 'nt', _plain(dproj), _plain(rec['w_in_pad']), (seq, dm, pw),
                      [((seq, dm), F32, _plain_shape(None))], (tm, _tile(dm, 1024), _tile(pw, 1024)),
                      after=behind)[0]
            dh, dg = _norm_bwd(rec['h_in'], dhn, dh, gla_norm[j:j + 1], 'gla_norm_bwd')
            grads['gla_norm'][j] = dg[0]
        else:
            dhp = _permute(dh)
            do = _glu_bwd(rec['o'], dhp, 's5_glu_bwd')
            r_o, c_o = s5_w_out.shape[1:]
            shape, spec = _dw_cols(r_o, c_o)
            dw = _mm('s5_out_dw', 'tn', _plain(rec['z']), _plain(do), (s5w, 2 * dm, seq),
                     [(shape, BF16, spec)], (_tile(r_o // 2, 1024), _tile(c_o, 1024), seq))[0]
            behind = reduce_scatter(dw, 's5_w_out', j)
            dz = _mm('s5_out_dx', 'nt', _plain(do), _w_cols(g_s5_out, j), (seq, s5w, 2 * dm),
                     [((seq, s5w), F32, _plain_shape(None))],
                     (tm, _tile(rh(s5_w_out), 1024), _tile(s5_w_out.shape[2], 1024)), after=behind)[0]
            bre, bim, cre, cim, are, aim, dskip = rec['mats']
            du, dcr, dci, dbr, dbi, dar, dai, dd = _s5_bwd(dz, rec['y'], rec['u'], rec['xr'], rec['xi'],
                                                           bre, bim, cre, cim, are, aim, dskip, 's5_scan_bwd')
            grads['s5_c_re'][j] = dcr.reshape(n_grp, grp, n_state)
            grads['s5_c_im'][j] = dci.reshape(n_grp, grp, n_state)
            dbb_re = dbr.reshape(n_grp, grp, n_state).transpose(0, 2, 1)
            dbb_im = dbi.reshape(n_grp, grp, n_state).transpose(0, 2, 1)
            d_lr, d_li, d_dt, d_bre, d_bim = rec['disc_vjp'](
                (dar.reshape(n_grp, n_state), dai.reshape(n_grp, n_state), dbb_re, dbb_im))
            grads['s5_lam_re'][j] = d_lr
            grads['s5_lam_im'][j] = d_li
            grads['s5_log_dt'][j] = d_dt
            grads['s5_b_re'][j] = d_bre
            grads['s5_b_im'][j] = d_bim
            grads['s5_d'][j] = dd[0]
            r_i, c_i = s5_w_in.shape[1:]
            shape, spec = _dw_rows(r_i, c_i)
            dw = _mm('s5_in_dw', 'tn', _plain(rec['hn']), _plain(du), (dm, s5w, seq),
                     [(shape, BF16, spec)], (_tile(r_i // 2, 1024), _tile(c_i, 1024), seq))[0]
            behind = reduce_scatter(dw, 's5_w_in', j)
            dhn = _mm('s5_in_dx', 'nt', _plain(du), _w_rows(g_s5_in, j), (seq, dm, s5w),
                      [((seq, dm), F32, _plain_shape(None))],
                      (tm, _tile(rh(s5_w_in), 1024), _tile(s5w, 1024)), after=behind)[0]
            dhp, dg = _norm_bwd(rec['h_in'], dhn, dhp, s5_norm_full[j:j + 1], 's5_norm_bwd')
            dh = _unpermute(dhp)
            grads['s5_norm'][j] = dg[0]
    grad_x = dh[None]
    behind = [dh]
    for k, (stage, items) in enumerate((('pair', reducing), ('chips', reducing[:-1]), ('back', reducing[:-1]))):
        behind = reduce_many([item for item in items if item['stage'] == stage], f'rs_tail_{k}_start', behind)

    local_small = [jnp.stack(grads[n]) if n != 'final_norm' else grads[n][0] for n in small]
    full_shapes = [a.shape for a in local_small]
    packed_small = _pack(local_small)
    ar_small = _xy_start(packed_small, 'bcast', 'ar_small_start', [])
    out_g, out_d, out_m, out_v = {}, {}, {}, {}
    behind = [ar_small['token']]
    last_n = reducing[-1]['n']
    for n in [m for m in big if m != last_n] + [last_n]:
        if n == last_n:
            for k in range(2):
                behind = reduce_many([reducing[-1]], f'rs_last_{k}_start', behind)
        if weights[n].shape[2] % LANES:
            to_t, from_t = (lambda a: a.transpose(2, 0, 1)), (lambda a: a.transpose(1, 2, 0))
            g_t = to_t(jnp.stack(big_grads[n]))
            res = _adamw_t(to_t(weights[n]), to_t(mom1[n]), to_t(mom2[n]), g_t, 'adamw_' + n, behind)
            out_g[n], out_d[n], out_m[n], out_v[n] = (from_t(a) for a in (g_t,) + tuple(res))
            behind = [res[0]]
        else:
            out_g[n], out_d[n], out_m[n], out_v[n] = _adamw(weights[n], mom1[n], mom2[n], big_grads[n],
                                                            'adamw_' + n, behind)
            behind = [out_d[n]]
    sent, by_chip = _xy_wait(ar_small, 'ar_small_wait', behind)
    by_chip = lax.dynamic_update_index_in_dim(by_chip, sent, chip, 0)
    gathered = _exchange(by_chip, 'c', 'bcast', 'ar_small_c')
    rows = gathered.shape[2]
    summed = _sum_slots(gathered.reshape(8, rows, LANES), F32, 'ar_small_sum')
    small_full = dict(zip(small, _unpack(summed, full_shapes)))
    small_grad = {}
    for n in small:
        g = small_full[n]
        if g.shape != weights[n].shape:
            ax = [a for a in range(g.ndim) if g.shape[a] != weights[n].shape[a]][0]
            g = lax.dynamic_slice_in_dim(g, chip * weights[n].shape[ax], weights[n].shape[ax], axis=ax)
        small_grad[n] = g

    shapes = [weights[n].shape for n in small]
    pw_, pm_, pv_, pg_ = (_pack([d[n] for n in small]) for d in (weights, mom1, mom2, small_grad))
    _, sd, sm, sv = _adamw(pw_[None], pm_[None], pv_[None], [pg_], 'adamw_small')
    for n, d_, m_, v_ in zip(small, _unpack(sd[0], shapes), _unpack(sm[0], shapes), _unpack(sv[0], shapes)):
        out_g[n], out_d[n], out_m[n], out_v[n] = small_grad[n], d_, m_, v_

    return (loss, grad_x, *[out_g[n] for n in names], *[out_d[n] for n in names],
            *[out_m[n] for n in names], *[out_v[n] for n in names])
```

```python
import functools
import math

import jax
import jax.numpy as jnp
from jax import lax
from jax.experimental import pallas as pl
from jax.experimental.pallas import tpu as pltpu

F32 = jnp.float32
BF16 = jnp.bfloat16

EPS = 1e-6
CHUNK = 64
GLA_GATE_TEMP = 16.0
S5_EIG_CLIP = -1e-4
S5_SEGMENTS = 8
S5_GROUPS_PER_BLOCK = 8
LANES = 128
ADAM_LR = 0.001
ADAM_B1 = 0.9
ADAM_B2 = 0.999
ADAM_EPS = 1e-08
ADAM_WD = 0.01
ADAM_STEP = 10
VMEM_LIMIT_BYTES = 56 * 1024 * 1024
GATHERS_IN_FLIGHT = 2

MESH = pl.DeviceIdType.MESH
ANY = pl.BlockSpec(memory_space=pl.ANY)
IN_VMEM = pl.BlockSpec(memory_space=pltpu.VMEM)
IN_HBM = pl.BlockSpec(memory_space=pltpu.HBM)
IN_SEM = pl.BlockSpec(memory_space=pltpu.SEMAPHORE)
DATAFLOW = pltpu.SideEffectType.DATAFLOW_SIDE_EFFECTING


def _pcall(body, **kw):
    return pl.pallas_call(body, **kw)


def _params(*sem):
    return pltpu.CompilerParams(dimension_semantics=sem, vmem_limit_bytes=VMEM_LIMIT_BYTES)


def _tile(dim, target, unit=LANES):
    if dim <= target:
        return dim
    best = None
    for t in range(unit, target + 1, unit):
        if dim % t == 0:
            best = t
    assert best is not None, (dim, target)
    return best


_FLIPS = {'xy': [(1, 0, 0), (0, 1, 0), (1, 1, 0)], 'c': [(0, 0, 1)]}


def _exchange(x, group, name):
    n = 2 if group == 'c' else 4
    flips = _FLIPS[group]

    def body(x_ref, y_ref, send_sems, recv_sems, local_sem):
        ix, iy, ic = lax.axis_index('x'), lax.axis_index('y'), lax.axis_index('c')

        def slot(px, py, pc):
            return pc if group == 'c' else 2 * px + py

        me = (ix, iy, ic)
        local = pltpu.make_async_copy(x_ref, y_ref.at[slot(*me)], local_sem)
        local.start()
        peers = []
        for fx, fy, fc in flips:
            peers.append((1 - ix if fx else ix, 1 - iy if fy else iy, 1 - ic if fc else ic))
        sends = []
        for k, peer in enumerate(peers):
            cp = pltpu.make_async_remote_copy(
                src_ref=x_ref, dst_ref=y_ref.at[slot(*me)],
                send_sem=send_sems.at[k], recv_sem=recv_sems.at[k],
                device_id=peer, device_id_type=MESH)
            cp.start()
            sends.append(cp)
        for k, peer in enumerate(peers):
            pltpu.make_async_remote_copy(
                src_ref=x_ref, dst_ref=y_ref.at[slot(*peer)],
                send_sem=send_sems.at[k], recv_sem=recv_sems.at[k],
                device_id=peer, device_id_type=MESH).wait_recv()
        for cp in sends:
            cp.wait_send()
        local.wait()

    return _pcall(
        body, name=name,
        out_shape=jax.ShapeDtypeStruct((n,) + tuple(x.shape), x.dtype),
        in_specs=[IN_VMEM], out_specs=IN_VMEM,
        scratch_shapes=[pltpu.SemaphoreType.DMA((len(flips),)),
                        pltpu.SemaphoreType.DMA((len(flips),)),
                        pltpu.SemaphoreType.DMA(())],
        compiler_params=pltpu.CompilerParams(vmem_limit_bytes=VMEM_LIMIT_BYTES),
    )(x)


_KIND_GROUP = {'chips_a2a': 'xy', 'chips_bcast': 'xy', 'chips_gather': 'xy',
               'pair_swap': 'c', 'pair_bcast': 'c', 'pair_inplace': 'c'}


def _plan(kind, x_ref, land_ref, sems):
    ix, iy, ic = lax.axis_index('x'), lax.axis_index('y'), lax.axis_index('c')
    flips = _FLIPS[_KIND_GROUP[kind]]
    me_chip = 2 * ix + iy
    out = []
    for k, (fx, fy, fc) in enumerate(flips):
        peer = (1 - ix if fx else ix, 1 - iy if fy else iy, 1 - ic if fc else ic)
        peer_chip = 2 * peer[0] + peer[1]
        if kind == 'chips_a2a':
            src, dst, got = x_ref.at[peer_chip], land_ref.at[me_chip], land_ref.at[peer_chip]
        elif kind == 'chips_bcast':
            src, dst, got = x_ref, land_ref.at[me_chip], land_ref.at[peer_chip]
        elif kind == 'chips_gather':
            src, dst, got = x_ref.at[ic], land_ref.at[ic, me_chip], land_ref.at[ic, peer_chip]
        elif kind == 'pair_swap':
            src, dst, got = x_ref.at[1 - ic], land_ref, land_ref
        elif kind == 'pair_bcast':
            src, dst, got = x_ref, land_ref.at[ic], land_ref.at[1 - ic]
        else:
            src, dst, got = land_ref.at[ic], land_ref.at[ic], land_ref.at[1 - ic]
        mk = lambda d, src=src, k=k, peer=peer: pltpu.make_async_remote_copy(
            src_ref=src, dst_ref=d, send_sem=sems[k], recv_sem=sems[len(flips) + k],
            device_id=peer, device_id_type=MESH)
        out.append((mk(dst), mk(got)))
    return out


def _start_many(jobs, name, after):
    per_job = []
    for kind, x, land in jobs:
        if not hasattr(land, 'dtype'):
            land = lax.empty(tuple(land), x.dtype)
        per_job.append((kind, ([] if x is None else [x]) + [land], 2 * len(_FLIPS[_KIND_GROUP[kind]])))
    arrays = [a for _, arrs, _ in per_job for a in arrs]
    n_arr, n_after = len(arrays), len(after)
    n_sems = sum(ns for _, _, ns in per_job)

    def body(*refs):
        a0, s0 = 0, n_arr + n_after
        for kind, arrs, ns in per_job:
            x_ref = refs[a0] if len(arrs) == 2 else None
            for send, _ in _plan(kind, x_ref, refs[a0 + len(arrs) - 1], refs[s0:s0 + ns]):
                send.start()
            a0, s0 = a0 + len(arrs), s0 + ns
        refs[-1][...] = jnp.zeros_like(refs[-1])

    outs = _pcall(
        body, name=name,
        out_shape=(pltpu.SemaphoreType.DMA(()),) * n_sems + tuple(pltpu.HBM(a.shape, a.dtype) for a in arrays)
        + (jax.ShapeDtypeStruct((8, LANES), F32),),
        in_specs=(IN_HBM,) * n_arr + (ANY,) * n_after,
        out_specs=(IN_SEM,) * n_sems + (IN_HBM,) * n_arr + (IN_VMEM,),
        input_output_aliases={i: n_sems + i for i in range(n_arr)},
        compiler_params=pltpu.CompilerParams(has_side_effects=DATAFLOW),
    )(*[pltpu.with_memory_space_constraint(a, pltpu.HBM) for a in arrays], *after)
    handles, a0, s0 = [], n_sems, 0
    for kind, arrs, ns in per_job:
        handles.append(dict(kind=kind, sems=outs[s0:s0 + ns], arrays=outs[a0:a0 + len(arrs)], token=outs[-1]))
        a0, s0 = a0 + len(arrs), s0 + ns
    return handles


def _start(kind, x, land, name, after):
    return _start_many([(kind, x, land)], name, after)[0]


def _wait(handle, name, after):
    kind, arrays, sems = handle['kind'], handle['arrays'], handle['sems']
    n_arr, n_sems = len(arrays), len(sems)

    def body(*refs):
        for _, got in _plan(kind, refs[0] if n_arr == 2 else None, refs[n_arr - 1], refs[n_arr:n_arr + n_sems]):
            got.wait_send()
            got.wait_recv()

    return _pcall(
        body, name=name,
        out_shape=tuple(pltpu.HBM(a.shape, a.dtype) for a in arrays),
        in_specs=(IN_HBM,) * n_arr + (IN_SEM,) * n_sems + (ANY,) * len(after),
        out_specs=(IN_HBM,) * n_arr,
        input_output_aliases={i: i for i in range(n_arr)},
        compiler_params=pltpu.CompilerParams(has_side_effects=DATAFLOW),
    )(*arrays, *sems, *after)


def _sum_pair(x, recv, ic, out_dtype, name):
    _, rows, cols = x.shape
    tm = _tile(rows, max(8, (1 << 20) // cols // 8 * 8), unit=8)

    def body(c_ref, x_ref, r_ref, o_ref):
        o_ref[...] = (x_ref[...].astype(F32) + r_ref[...].astype(F32)).astype(o_ref.dtype)

    return _pcall(
        body, name=name,
        grid_spec=pltpu.PrefetchScalarGridSpec(
            num_scalar_prefetch=1, grid=(rows // tm,),
            in_specs=[pl.BlockSpec((None, tm, cols), lambda i, c: (c[0], i, 0)),
                      pl.BlockSpec((tm, cols), lambda i, c: (i, 0))],
            out_specs=pl.BlockSpec((tm, cols), lambda i, c: (i, 0))),
        out_shape=jax.ShapeDtypeStruct((rows, cols), out_dtype),
        compiler_params=_params('parallel'),
    )(jnp.reshape(ic, (1,)).astype(jnp.int32), x, recv)


def _sum_slots(y, out_dtype, name):
    n, rows, cols = y.shape
    tm = _tile(rows, max(8, (1 << 20) // (n * cols) // 8 * 8), unit=8)

    def body(y_ref, o_ref):
        acc = y_ref[0].astype(F32)
        for k in range(1, n):
            acc = acc + y_ref[k].astype(F32)
        o_ref[...] = acc.astype(o_ref.dtype)

    return _pcall(
        body, name=name, grid=(rows // tm,),
        out_shape=jax.ShapeDtypeStruct((rows, cols), out_dtype),
        in_specs=[pl.BlockSpec((n, tm, cols), lambda i: (0, i, 0))],
        out_specs=pl.BlockSpec((tm, cols), lambda i: (i, 0)),
        compiler_params=_params('parallel'),
    )(y)


class _Op:
    def __init__(self, arr, spec):
        self.arr = arr
        self.spec = spec


def _plain(arr):
    return _Op(arr, lambda t0, t1: ((t0, t1), lambda b0, b1: (b0, b1)))


def _plain_shape(shape):
    return lambda t0, t1: ((t0, t1), lambda b0, b1: (b0, b1))


def _dw_cols(rows, cols):
    rh = rows // 2

    def spec(t0, t1):
        assert (rh % t0 == 0 or t0 == rows) and cols % t1 == 0, (rh, cols, t0, t1)
        qr, qc = max(1, rh // t0), cols // t1
        if t0 == rows:
            return (2, None, rh, t1), lambda b0, b1: (0, b1 // qc, 0, b1 % qc)
        return (None, None, t0, t1), lambda b0, b1: (b0 // qr, b1 // qc, b0 % qr, b1 % qc)
    return (2, 4, rh, cols), spec


def _dw_rows(rows, cols):
    rh = rows // 2

    def spec(t0, t1):
        assert (rh % t0 == 0 or t0 == rows) and cols % t1 == 0, (rh, cols, t0, t1)
        qr = max(1, rh // t0)
        if t0 == rows:
            return (2, None, rh, t1), lambda b0, b1: (0, b0, 0, b1)
        return (None, None, t0, t1), lambda b0, b1: ((b0 // qr) % 2, b0 // (2 * qr), b0 % qr, b1)
    return (2, 4, rh, cols), spec


def _w_cols(g, j):
    _, _, rh, cols = g[j].shape
    return _Op(g[j], _dw_cols(2 * rh, cols)[1])


def _w_rows(g, j):
    _, _, rh, cols = g[j].shape
    return _Op(g[j], _dw_rows(2 * rh, cols)[1])


def _mm(name, mode, a, b, dims, outs, tiles, epilogue=None, extras=(), after=()):
    m, n, k = dims
    tm, tn, tk = tiles
    assert m % tm == 0 and n % tn == 0 and k % tk == 0, (name, dims, tiles)
    nk = k // tk
    if mode == 'nn':
        a_t, a_ix, b_t, b_ix, ca, cb = (tm, tk), (lambda i, j, kk: (i, kk)), (tk, tn), (lambda i, j, kk: (kk, j)), 1, 0
    elif mode == 'nt':
        a_t, a_ix, b_t, b_ix, ca, cb = (tm, tk), (lambda i, j, kk: (i, kk)), (tn, tk), (lambda i, j, kk: (j, kk)), 1, 1
    else:
        a_t, a_ix, b_t, b_ix, ca, cb = (tk, tm), (lambda i, j, kk: (kk, i)), (tk, tn), (lambda i, j, kk: (kk, j)), 0, 0
    a_blk, a_fn = a.spec(*a_t)
    b_blk, b_fn = b.spec(*b_t)
    in_specs = [pl.BlockSpec(a_blk, lambda i, j, kk: a_fn(*a_ix(i, j, kk))),
                pl.BlockSpec(b_blk, lambda i, j, kk: b_fn(*b_ix(i, j, kk)))]
    operands = [a.arr, b.arr]
    for e in extras:
        e_blk, e_fn = e.spec(tm, tn)
        in_specs.append(pl.BlockSpec(e_blk, functools.partial(lambda i, j, kk, f: f(i, j), f=e_fn)))
        operands.append(e.arr)
    out_shapes, out_specs = [], []
    for shape, dtype, spec in outs:
        o_blk, o_fn = spec(tm, tn)
        out_shapes.append(jax.ShapeDtypeStruct(shape, dtype))
        out_specs.append(pl.BlockSpec(o_blk, functools.partial(lambda i, j, kk, f: f(i, j), f=o_fn)))
    n_ex, n_out = len(extras), len(outs)
    in_specs += [ANY] * len(after)
    operands += list(after)
    if epilogue is None:
        epilogue = lambda acc: (acc,)

    def body(a_ref, b_ref, *rest):
        ex_refs = rest[:n_ex]
        rest = rest[:n_ex] + rest[n_ex + len(after):]
        out_refs = rest[n_ex:n_ex + n_out]
        bv = b_ref[...]
        if bv.ndim == 3:
            bv = bv.reshape(bv.shape[0] * bv.shape[1], bv.shape[2])
        p = lax.dot_general(a_ref[...].astype(BF16), bv.astype(BF16),
                            (((ca,), (cb,)), ((), ())), preferred_element_type=F32)

        def finish(acc):
            res = epilogue(acc, *[r[...] for r in ex_refs])
            for o_ref, val in zip(out_refs, res):
                o_ref[...] = val.astype(o_ref.dtype)

        if nk == 1:
            finish(p)
        else:
            acc_ref = rest[n_ex + n_out]
            kk = pl.program_id(2)

            @pl.when(kk == 0)
            def _():
                acc_ref[...] = p

            @pl.when(kk > 0)
            def _():
                acc_ref[...] += p

            @pl.when(kk == nk - 1)
            def _():
                finish(acc_ref[...])

    res = _pcall(
        body, name=name, grid=(m // tm, n // tn, nk),
        out_shape=out_shapes, in_specs=in_specs, out_specs=out_specs,
        scratch_shapes=[pltpu.VMEM((tm, tn), F32)] if nk > 1 else [],
        compiler_params=_params('parallel', 'parallel', 'arbitrary'),
    )(*operands)
    return res


def _rowwise(name, fn, row_ins, vec_ins, outs, reds, tm, after=()):
    rows = row_ins[0].shape[0]
    assert rows % tm == 0
    n_in = len(row_ins) + len(vec_ins)
    n_out = len(outs)

    def body(*refs):
        vals = [r[...] for r in refs[:n_in]]
        refs = refs[:n_in] + refs[n_in + len(after):]
        res = fn(*vals)
        for o_ref, val in zip(refs[n_in:n_in + n_out], res[:n_out]):
            o_ref[...] = val.astype(o_ref.dtype)
        first = pl.program_id(0) == 0
        for r_ref, val in zip(refs[n_in + n_out:], res[n_out:]):
            @pl.when(first)
            def _(r_ref=r_ref, val=val):
                r_ref[...] = val

            @pl.when(jnp.logical_not(first))
            def _(r_ref=r_ref, val=val):
                r_ref[...] += val

    in_specs = [pl.BlockSpec((tm, a.shape[1]), lambda i: (i, 0)) for a in row_ins]
    in_specs += [pl.BlockSpec((1, v.shape[1]), lambda i: (0, 0)) for v in vec_ins]
    in_specs += [ANY] * len(after)
    out_shapes = [jax.ShapeDtypeStruct((rows, w), dt) for w, dt in outs]
    out_shapes += [jax.ShapeDtypeStruct((1, w), F32) for w in reds]
    out_specs = [pl.BlockSpec((tm, w), lambda i: (i, 0)) for w, _ in outs]
    out_specs += [pl.BlockSpec((1, w), lambda i: (0, 0)) for w in reds]
    return _pcall(
        body, name=name, grid=(rows // tm,),
        out_shape=out_shapes, in_specs=in_specs, out_specs=out_specs,
        compiler_params=_params('arbitrary'),
    )(*row_ins, *vec_ins, *after)


def _norm_fwd(h, g, name, after=()):
    def fn(hv, gv):
        rstd = lax.rsqrt(jnp.mean(hv * hv, axis=-1, keepdims=True) + EPS)
        return (hv * rstd * gv,)
    return _rowwise(name, fn, [h], [g], [(h.shape[1], BF16)], [], 256, after)[0]


def _norm_bwd(h, dhn, dres, g, name):
    def fn(hv, dv, rv, gv):
        rstd = lax.rsqrt(jnp.mean(hv * hv, axis=-1, keepdims=True) + EPS)
        xhat = hv * rstd
        dxhat = dv * gv
        dh = rv + rstd * (dxhat - xhat * jnp.mean(dxhat * xhat, axis=-1, keepdims=True))
        return dh, jnp.sum(dv * xhat, axis=0, keepdims=True)
    w = h.shape[1]
    return _rowwise(name, fn, [h, dhn, dres], [g], [(w, F32)], [w], 256)


def _loss_head(h, target, g, name):
    w = h.shape[1]

    def fn(hv, tv, gv):
        rstd = lax.rsqrt(jnp.mean(hv * hv, axis=-1, keepdims=True) + EPS)
        xhat = hv * rstd
        diff = xhat * gv - tv
        dy = diff * (1.0 / w)
        dxhat = dy * gv
        dh = rstd * (dxhat - xhat * jnp.mean(dxhat * xhat, axis=-1, keepdims=True))
        return (dh, jnp.sum(0.5 * dy * diff, axis=0, keepdims=True),
                jnp.sum(dy * xhat, axis=0, keepdims=True))
    return _rowwise(name, fn, [h, target], [g], [(w, F32)], [w, w], 256)


def _split3(x):
    hi = x.astype(BF16)
    r1 = x - hi.astype(F32)
    mid = r1.astype(BF16)
    lo = (r1 - mid.astype(F32)).astype(BF16)
    return hi, mid, lo


def _tri_dot(tri, x):
    hi, mid, lo = _split3(x)
    d = lambda p: jnp.dot(tri, p, preferred_element_type=F32)
    return d(hi) + d(mid) + d(lo)


def _log_sigmoid(x):
    return jnp.minimum(x, 0.0) - jnp.log(1.0 + jnp.exp(-jnp.abs(x)))


def _gla_dims(proj_w, kw, vw, dk, dv):
    assert kw % dk == 0 and (2 * kw) % dv == 0 and (2 * kw + vw) % dv == 0 and (2 * kw + 2 * vw) % LANES == 0
    return dict(q0=0, k0=kw // dk, v0=2 * kw // dv, r0=(2 * kw + vw) // dv, g0=(2 * kw + 2 * vw) // LANES)


def _gla_gates(gl, wgu, bias):
    pre = jnp.dot(gl.astype(BF16), wgu, preferred_element_type=F32) + bias
    la = _log_sigmoid(pre) * (1.0 / GLA_GATE_TEMP)
    r_i = lax.broadcasted_iota(jnp.int32, (CHUNK, CHUNK), 0)
    c_i = lax.broadcasted_iota(jnp.int32, (CHUNK, CHUNK), 1)
    cum = _tri_dot((c_i <= r_i).astype(BF16), la)
    total = cum[CHUNK - 1:CHUNK, :]
    return pre, cum, total


def _gla_scan_fwd(proj, wgu_pad, b_gate, o_norm, heads, kw, vw, tb, name):
    seq, pw = proj.shape
    dk, dv = kw // heads, vw // heads
    cb = tb // CHUNK
    nt = seq // tb
    o = _gla_dims(pw, kw, vw, dk, dv)
    scale = dk ** -0.5

    def body(q_ref, k_ref, v_ref, r_ref, gl_ref, wgu_ref, b_ref, on_ref, out_ref, st_ref, s_scr):
        @pl.when(pl.program_id(1) == 0)
        def _():
            s_scr[...] = jnp.zeros_like(s_scr)

        wgu = wgu_ref[...].astype(BF16)
        bias = b_ref[...]
        onorm = on_ref[...]
        st = s_scr[...]
        for ci in range(cb):
            rows = pl.ds(ci * CHUNK, CHUNK)
            _, cum, total = _gla_gates(gl_ref[rows, :], wgu, bias)
            kdec = k_ref[rows, :] * jnp.exp(total - cum)
            st = st * jnp.exp(total) + lax.dot_general(
                v_ref[rows, :].astype(BF16), kdec.astype(BF16), (((0,), (0,)), ((), ())),
                preferred_element_type=F32)
            st_ref[ci] = st
            qs = (q_ref[rows, :] * scale).astype(BF16)
            ov = lax.dot_general(qs, st.astype(BF16), (((1,), (1,)), ((), ())), preferred_element_type=F32)
            rstd = lax.rsqrt(jnp.mean(ov * ov, axis=-1, keepdims=True) + EPS)
            rv = r_ref[rows, :]
            out_ref[rows, :] = (ov * rstd * onorm * (rv * jax.nn.sigmoid(rv))).astype(out_ref.dtype)
        s_scr[...] = st

    in_specs = [
        pl.BlockSpec((tb, dk), lambda h, t: (t, o['q0'] + h)),
        pl.BlockSpec((tb, dk), lambda h, t: (t, o['k0'] + h)),
        pl.BlockSpec((tb, dv), lambda h, t: (t, o['v0'] + h)),
        pl.BlockSpec((tb, dv), lambda h, t: (t, o['r0'] + h)),
        pl.BlockSpec((tb, LANES), lambda h, t: (t, o['g0'])),
        pl.BlockSpec((LANES, dk), lambda h, t: (0, h)),
        pl.BlockSpec((1, dk), lambda h, t: (0, h)),
        pl.BlockSpec((1, dv), lambda h, t: (0, 0)),
    ]
    return _pcall(
        body, name=name, grid=(heads, nt),
        out_shape=[jax.ShapeDtypeStruct((seq, vw), BF16),
                   jax.ShapeDtypeStruct((heads, seq // CHUNK, dv, dk), F32)],
        in_specs=in_specs,
        out_specs=[pl.BlockSpec((tb, dv), lambda h, t: (t, h)),
                   pl.BlockSpec((None, cb, dv, dk), lambda h, t: (h, t, 0, 0))],
        scratch_shapes=[pltpu.VMEM((dv, dk), F32)],
        compiler_params=_params('parallel', 'arbitrary'),
    )(proj, proj, proj, proj, proj, wgu_pad, b_gate, o_norm)


def _gla_scan_bwd(proj, wgu_pad, b_gate, o_norm, states, dgated, heads, kw, vw, tb, name):
    seq, pw = proj.shape
    dk, dv = kw // heads, vw // heads
    cb = tb // CHUNK
    nt = seq // tb
    o = _gla_dims(pw, kw, vw, dk, dv)
    scale = dk ** -0.5

    def body(q_ref, k_ref, v_ref, r_ref, gl_ref, wgu_ref, b_ref, on_ref, st_ref, stp_ref, dg_ref,
             dq_ref, dk_ref, dv_ref, dr_ref, dpre_ref, db_ref, don_ref, ds_scr):
        hh = pl.program_id(0)
        t = pl.program_id(1)

        @pl.when(t == 0)
        def _():
            ds_scr[...] = jnp.zeros_like(ds_scr)
            db_ref[...] = jnp.zeros_like(db_ref)

        @pl.when(jnp.logical_and(hh == 0, t == 0))
        def _():
            don_ref[...] = jnp.zeros_like(don_ref)

        wgu = wgu_ref[...].astype(BF16)
        bias = b_ref[...]
        onorm = on_ref[...]
        has_prev = (t < nt - 1).astype(F32)
        r_i = lax.broadcasted_iota(jnp.int32, (CHUNK, CHUNK), 0)
        c_i = lax.broadcasted_iota(jnp.int32, (CHUNK, CHUNK), 1)
        strict = (c_i < r_i).astype(BF16)
        carry = ds_scr[...]
        db_acc = jnp.zeros((1, dk), F32)
        don_acc = jnp.zeros((1, dv), F32)
        for ci in reversed(range(cb)):
            rows = pl.ds(ci * CHUNK, CHUNK)
            pre, cum, total = _gla_gates(gl_ref[rows, :], wgu, bias)
            edec = jnp.exp(total - cum)
            decay = jnp.exp(total)
            kdec = k_ref[rows, :] * edec
            st = st_ref[ci]
            st_prev = st_ref[ci - 1] if ci > 0 else stp_ref[0] * has_prev
            stb = st.astype(BF16)
            qs = (q_ref[rows, :] * scale).astype(BF16)
            vb = v_ref[rows, :].astype(BF16)
            ov = lax.dot_general(qs, stb, (((1,), (1,)), ((), ())), preferred_element_type=F32)
            rstd = lax.rsqrt(jnp.mean(ov * ov, axis=-1, keepdims=True) + EPS)
            ohat = ov * rstd
            rv = r_ref[rows, :]
            sr = jax.nn.sigmoid(rv)
            dgv = dg_ref[rows, :]
            dy = dgv * (rv * sr)
            dr_ref[rows, :] = (dgv * (ohat * onorm) * (sr * (1.0 + rv * (1.0 - sr)))).astype(dr_ref.dtype)
            don_acc = don_acc + jnp.sum(dy * ohat, axis=0, keepdims=True)
            dohat = dy * onorm
            do = (rstd * (dohat - ohat * jnp.mean(dohat * ohat, axis=-1, keepdims=True))).astype(BF16)
            dq_ref[rows, :] = (jnp.dot(do, stb, preferred_element_type=F32) * scale).astype(dq_ref.dtype)
            dst = carry + lax.dot_general(do, qs, (((0,), (0,)), ((), ())), preferred_element_type=F32)
            dstb = dst.astype(BF16)
            dkdec = jnp.dot(vb, dstb, preferred_element_type=F32)
            dv_ref[rows, :] = lax.dot_general(kdec.astype(BF16), dstb, (((1,), (1,)), ((), ())),
                                              preferred_element_type=F32).astype(dv_ref.dtype)
            ddecay = jnp.sum(dst * st_prev, axis=0, keepdims=True)
            dk_ref[rows, :] = (dkdec * edec).astype(dk_ref.dtype)
            da = ddecay * decay + _tri_dot(strict, dkdec * kdec)
            dpre = da * (1.0 / GLA_GATE_TEMP) * (1.0 - jax.nn.sigmoid(pre))
            dpre_ref[rows, :] = dpre.astype(dpre_ref.dtype)
            db_acc = db_acc + jnp.sum(dpre, axis=0, keepdims=True)
            carry = dst * decay
        ds_scr[...] = carry
        db_ref[...] += db_acc
        don_ref[...] += don_acc

    rt = lambda t: nt - 1 - t
    in_specs = [
        pl.BlockSpec((tb, dk), lambda h, t: (rt(t), o['q0'] + h)),
        pl.BlockSpec((tb, dk), lambda h, t: (rt(t), o['k0'] + h)),
        pl.BlockSpec((tb, dv), lambda h, t: (rt(t), o['v0'] + h)),
        pl.BlockSpec((tb, dv), lambda h, t: (rt(t), o['r0'] + h)),
        pl.BlockSpec((tb, LANES), lambda h, t: (rt(t), o['g0'])),
        pl.BlockSpec((LANES, dk), lambda h, t: (0, h)),
        pl.BlockSpec((1, dk), lambda h, t: (0, h)),
        pl.BlockSpec((1, dv), lambda h, t: (0, 0)),
        pl.BlockSpec((None, cb, dv, dk), lambda h, t: (h, rt(t), 0, 0)),
        pl.BlockSpec((None, 1, dv, dk), lambda h, t: (h, jnp.maximum(rt(t) * cb - 1, 0), 0, 0)),
        pl.BlockSpec((tb, dv), lambda h, t: (rt(t), h)),
    ]
    out_shape = [jax.ShapeDtypeStruct((seq, kw), BF16), jax.ShapeDtypeStruct((seq, kw), BF16),
                 jax.ShapeDtypeStruct((seq, vw), BF16), jax.ShapeDtypeStruct((seq, vw), BF16),
                 jax.ShapeDtypeStruct((seq, kw), BF16),
                 jax.ShapeDtypeStruct((1, kw), F32), jax.ShapeDtypeStruct((1, dv), F32)]
    out_specs = [pl.BlockSpec((tb, dk), lambda h, t: (rt(t), h)),
                 pl.BlockSpec((tb, dk), lambda h, t: (rt(t), h)),
                 pl.BlockSpec((tb, dv), lambda h, t: (rt(t), h)),
                 pl.BlockSpec((tb, dv), lambda h, t: (rt(t), h)),
                 pl.BlockSpec((tb, dk), lambda h, t: (rt(t), h)),
                 pl.BlockSpec((1, dk), lambda h, t: (0, h)),
                 pl.BlockSpec((1, dv), lambda h, t: (0, 0))]
    return _pcall(
        body, name=name, grid=(heads, nt),
        out_shape=out_shape, in_specs=in_specs, out_specs=out_specs,
        scratch_shapes=[pltpu.VMEM((dv, dk), F32)],
        compiler_params=_params('arbitrary', 'arbitrary'),
    )(proj, proj, proj, proj, proj, wgu_pad, b_gate, o_norm, states, states, dgated)


def _cmul(ar, ai, br, bi):
    return ar * br - ai * bi, ar * bi + ai * br


def _gelu(y):
    c = math.sqrt(2.0 / math.pi)
    return 0.5 * y * (1.0 + jnp.tanh(c * (y + 0.044715 * y * y * y)))


def _gelu_grad(y):
    c = math.sqrt(2.0 / math.pi)
    th = jnp.tanh(c * (y + 0.044715 * y * y * y))
    return 0.5 * (1.0 + th) + 0.5 * y * (1.0 - th * th) * (c * (1.0 + 3.0 * 0.044715 * y * y))


def _power_pow2(ar, ai, n):
    assert n & (n - 1) == 0
    for _ in range(n.bit_length() - 1):
        ar, ai = _cmul(ar, ai, ar, ai)
    return ar, ai


def _block_mask(rows, cols, rb, cb):
    r_i = lax.broadcasted_iota(jnp.int32, (rows, cols), 0)
    c_i = lax.broadcasted_iota(jnp.int32, (rows, cols), 1)
    return r_i // rb == c_i // cb


def _expand_blocks(m, gb):
    rows, b = m.shape
    r_i = lax.broadcasted_iota(jnp.int32, (b, gb * b), 0)
    c_i = lax.broadcasted_iota(jnp.int32, (b, gb * b), 1)
    repeat = (c_i % b == r_i).astype(BF16)
    full = jnp.dot(m.astype(BF16), repeat, preferred_element_type=F32)
    return jnp.where(_block_mask(rows, gb * b, rows // gb, b), full, 0.0).astype(BF16)


def _diagonal_blocks(m, gb):
    rows, cols = m.shape
    b = cols // gb
    masked = jnp.where(_block_mask(rows, cols, rows // gb, b), m, 0.0)
    r_i = lax.broadcasted_iota(jnp.int32, (cols, b), 0)
    c_i = lax.broadcasted_iota(jnp.int32, (cols, b), 1)
    pick = (r_i % b == c_i).astype(BF16)
    return sum(jnp.dot(p, pick, preferred_element_type=F32) for p in _split3(masked))


def _s5_fwd(u, bre, bim, cre, cim, are, aim, dskip, name):
    seq, width = u.shape
    nb, ub, n_state = bre.shape
    gb = S5_GROUPS_PER_BLOCK
    sb = gb * n_state
    ls = seq // S5_SEGMENTS
    seg = S5_SEGMENTS

    def body(u_ref, bre_ref, bim_ref, cre_ref, cim_ref, are_ref, aim_ref, d_ref, y_ref, z_ref, xr_ref, xi_ref):
        uv = u_ref[...]
        ub16 = uv.astype(BF16)
        xr_ref[...] = jnp.dot(ub16, _expand_blocks(bre_ref[...], gb), preferred_element_type=F32)
        xi_ref[...] = jnp.dot(ub16, _expand_blocks(bim_ref[...], gb), preferred_element_type=F32)
        ar = jnp.broadcast_to(are_ref[...], (seg, sb))
        ai = jnp.broadcast_to(aim_ref[...], (seg, sb))

        def step(i, c):
            rows = pl.ds(pl.multiple_of(i * seg, seg), seg)
            pr, pi = _cmul(ar, ai, c[0], c[1])
            nr = pr + xr_ref[rows, :]
            ni = pi + xi_ref[rows, :]
            xr_ref[rows, :] = nr
            xi_ref[rows, :] = ni
            return nr, ni

        zero = jnp.zeros((seg, sb), F32)
        er, ei = lax.fori_loop(0, ls, step, (zero, zero), unroll=8)
        pr, pi = _power_pow2(ar, ai, ls)
        row = lax.broadcasted_iota(jnp.int32, (seg, sb), 0)
        sr, si = zero, zero
        for _ in range(seg - 1):
            tr, ti = _cmul(pr, pi, sr, si)
            sr = jnp.where(row == 0, 0.0, pltpu.roll(tr + er, 1, 0))
            si = jnp.where(row == 0, 0.0, pltpu.roll(ti + ei, 1, 0))

        def fix(i, c):
            rows = pl.ds(pl.multiple_of(i * seg, seg), seg)
            fr, fi = _cmul(c[0], c[1], sr, si)
            xr_ref[rows, :] += fr
            xi_ref[rows, :] += fi
            return _cmul(c[0], c[1], ar, ai)

        lax.fori_loop(0, ls, fix, (ar, ai), unroll=8)
        y = (jnp.dot(xr_ref[...].astype(BF16), _expand_blocks(cre_ref[...], gb), preferred_element_type=F32)
             - jnp.dot(xi_ref[...].astype(BF16), _expand_blocks(cim_ref[...], gb), preferred_element_type=F32)
             + d_ref[...] * uv)
        y_ref[...] = y
        z_ref[...] = _gelu(y).astype(z_ref.dtype)

    mat = lambda r, c: pl.BlockSpec((None, r, c), lambda b: (b, 0, 0))
    return _pcall(
        body, name=name, grid=(nb,),
        out_shape=[jax.ShapeDtypeStruct((seq, width), F32), jax.ShapeDtypeStruct((seq, width), BF16),
                   jax.ShapeDtypeStruct((seq, nb * sb), F32), jax.ShapeDtypeStruct((seq, nb * sb), F32)],
        in_specs=[pl.BlockSpec((seq, ub), lambda b: (0, b)), mat(ub, n_state), mat(ub, n_state),
                  mat(sb, ub // gb), mat(sb, ub // gb),
                  mat(1, sb), mat(1, sb), pl.BlockSpec((1, ub), lambda b: (0, b))],
        out_specs=[pl.BlockSpec((seq, ub), lambda b: (0, b)), pl.BlockSpec((seq, ub), lambda b: (0, b)),
                   pl.BlockSpec((seq, sb), lambda b: (0, b)), pl.BlockSpec((seq, sb), lambda b: (0, b))],
        compiler_params=_params('parallel'),
    )(u, bre, bim, cre, cim, are, aim, dskip)


def _s5_bwd(dz, y, u, xr, xi, bre, bim, cre, cim, are, aim, dskip, name):
    seq, width = u.shape
    nb, ub, n_state = bre.shape
    gb = S5_GROUPS_PER_BLOCK
    sb = gb * n_state
    ls = seq // S5_SEGMENTS
    seg = S5_SEGMENTS

    def body(dz_ref, y_ref, u_ref, xr_ref, xi_ref, bre_ref, bim_ref, cre_ref, cim_ref, are_ref, aim_ref, d_ref,
             du_ref, dcr_ref, dci_ref, dbr_ref, dbi_ref, dar_ref, dai_ref, dd_ref, lr_ref, li_ref):
        uv = u_ref[...]
        dy = dz_ref[...] * _gelu_grad(y_ref[...])
        dd_ref[...] = jnp.sum(dy * uv, axis=0, keepdims=True)
        dyb = dy.astype(BF16)
        nt = (((1,), (1,)), ((), ()))
        tn = (((0,), (0,)), ((), ()))
        lr_ref[...] = lax.dot_general(dyb, _expand_blocks(cre_ref[...], gb), nt, preferred_element_type=F32)
        li_ref[...] = -lax.dot_general(dyb, _expand_blocks(cim_ref[...], gb), nt, preferred_element_type=F32)
        dcr_ref[...] = _diagonal_blocks(
            lax.dot_general(dyb, xr_ref[...].astype(BF16), tn, preferred_element_type=F32), gb)
        dci_ref[...] = -_diagonal_blocks(
            lax.dot_general(dyb, xi_ref[...].astype(BF16), tn, preferred_element_type=F32), gb)
        ar = jnp.broadcast_to(are_ref[...], (seg, sb))
        ai = jnp.broadcast_to(aim_ref[...], (seg, sb))
        nai = -ai

        def step(ii, c):
            rows = pl.ds(pl.multiple_of((ls - 1 - ii) * seg, seg), seg)
            pr, pi = _cmul(ar, nai, c[0], c[1])
            nr = pr + lr_ref[rows, :]
            ni = pi + li_ref[rows, :]
            lr_ref[rows, :] = nr
            li_ref[rows, :] = ni
            return nr, ni

        zero = jnp.zeros((seg, sb), F32)
        er, ei = lax.fori_loop(0, ls, step, (zero, zero), unroll=8)
        pr, pi = _power_pow2(ar, nai, ls)
        row = lax.broadcasted_iota(jnp.int32, (seg, sb), 0)
        rr, ri = zero, zero
        for _ in range(seg - 1):
            tr, ti = _cmul(pr, pi, rr, ri)
            rr = jnp.where(row == seg - 1, 0.0, pltpu.roll(tr + er, seg - 1, 0))
            ri = jnp.where(row == seg - 1, 0.0, pltpu.roll(ti + ei, seg - 1, 0))

        def corrected(rows, qr, qi):
            fr, fi = _cmul(qr, qi, rr, ri)
            nr = lr_ref[rows, :] + fr
            ni = li_ref[rows, :] + fi
            lr_ref[rows, :] = nr
            li_ref[rows, :] = ni
            return nr, ni

        def grad_a(nr, ni, xpr, xpi, accr, acci):
            return accr + nr * xpr + ni * xpi, acci + ni * xpr - nr * xpi

        def fix(ii, c):
            qr, qi, accr, acci = c
            i = ls - 1 - ii
            rows = pl.ds(pl.multiple_of(i * seg, seg), seg)
            prev = pl.ds(pl.multiple_of((i - 1) * seg, seg), seg)
            nr, ni = corrected(rows, qr, qi)
            accr, acci = grad_a(nr, ni, xr_ref[prev, :], xi_ref[prev, :], accr, acci)
            qr, qi = _cmul(qr, qi, ar, nai)
            return qr, qi, accr, acci

        qr, qi, accr, acci = lax.fori_loop(0, ls - 1, fix, (ar, nai, zero, zero), unroll=8)
        nr, ni = corrected(pl.ds(0, seg), qr, qi)
        last = pl.ds((ls - 1) * seg, seg)
        xpr = jnp.where(row == 0, 0.0, pltpu.roll(xr_ref[last, :], 1, 0))
        xpi = jnp.where(row == 0, 0.0, pltpu.roll(xi_ref[last, :], 1, 0))
        accr, acci = grad_a(nr, ni, xpr, xpi, accr, acci)
        dar_ref[...] = jnp.sum(accr, axis=0, keepdims=True)
        dai_ref[...] = jnp.sum(acci, axis=0, keepdims=True)
        lrb = lr_ref[...].astype(BF16)
        lib = li_ref[...].astype(BF16)
        ub16 = uv.astype(BF16)
        dbr_ref[...] = _diagonal_blocks(lax.dot_general(ub16, lrb, tn, preferred_element_type=F32), gb)
        dbi_ref[...] = _diagonal_blocks(lax.dot_general(ub16, lib, tn, preferred_element_type=F32), gb)
        du_ref[...] = (d_ref[...] * dy
                       + lax.dot_general(lrb, _expand_blocks(bre_ref[...], gb), nt, preferred_element_type=F32)
                       + lax.dot_general(lib, _expand_blocks(bim_ref[...], gb), nt, preferred_element_type=F32))

    mat = lambda r, c: pl.BlockSpec((None, r, c), lambda b: (b, 0, 0))
    col = lambda w: pl.BlockSpec((seq, w), lambda b: (0, b))
    return _pcall(
        body, name=name, grid=(nb,),
        out_shape=[jax.ShapeDtypeStruct((seq, width), F32)]
        + [jax.ShapeDtypeStruct((nb, ub, n_state), F32)] * 4
        + [jax.ShapeDtypeStruct((nb, 1, sb), F32)] * 2
        + [jax.ShapeDtypeStruct((1, width), F32)],
        in_specs=[col(ub), col(ub), col(ub), col(sb), col(sb), mat(ub, n_state), mat(ub, n_state),
                  mat(sb, ub // gb), mat(sb, ub // gb),
                  mat(1, sb), mat(1, sb), pl.BlockSpec((1, ub), lambda b: (0, b))],
        out_specs=[col(ub), mat(ub, n_state), mat(ub, n_state), mat(ub, n_state), mat(ub, n_state),
                   mat(1, sb), mat(1, sb), pl.BlockSpec((1, ub), lambda b: (0, b))],
        scratch_shapes=[pltpu.VMEM((seq, sb), F32), pltpu.VMEM((seq, sb), F32)],
        compiler_params=_params('parallel'),
    )(dz, y, u, xr, xi, bre, bim, cre, cim, are, aim, dskip)


def _s5_discretise(lam_re, lam_im, log_dt, b_re, b_im):
    lr = jnp.minimum(lam_re, S5_EIG_CLIP)
    li = lam_im
    dt = jnp.exp(log_dt)[:, None]
    mag = jnp.exp(lr * dt)
    ang = li * dt
    ab_re = mag * jnp.cos(ang)
    ab_im = mag * jnp.sin(ang)
    den = lr * lr + li * li
    nr = ab_re - 1.0
    f_re = (nr * lr + ab_im * li) / den
    f_im = (ab_im * lr - nr * li) / den
    bb_re = f_re[..., None] * b_re - f_im[..., None] * b_im
    bb_im = f_re[..., None] * b_im + f_im[..., None] * b_re
    return ab_re, ab_im, bb_re, bb_im


def _glu_fwd(o, h, name):
    half = o.shape[1] // 2

    def fn(ov, hv):
        return (hv + ov[:, :half] * jax.nn.sigmoid(ov[:, half:]),)
    return _rowwise(name, fn, [o, h], [], [(half, F32)], [], 256)[0]


def _glu_bwd(o, dout, name):
    half = o.shape[1] // 2

    def fn(ov, dv):
        val, gate = ov[:, :half], ov[:, half:]
        sg = jax.nn.sigmoid(gate)
        return (jnp.concatenate([dv * sg, dv * val * sg * (1.0 - sg)], axis=1),)
    return _rowwise(name, fn, [o, dout], [], [(2 * half, BF16)], [], 256)[0]


def _adam_math(w, g, m, v):
    m = ADAM_B1 * m + (1.0 - ADAM_B1) * g
    v = ADAM_B2 * v + (1.0 - ADAM_B2) * (g * g)
    m_hat = m / (1.0 - ADAM_B1 ** ADAM_STEP)
    v_hat = v / (1.0 - ADAM_B2 ** ADAM_STEP)
    delta = -ADAM_LR * (m_hat / (jnp.sqrt(v_hat) + ADAM_EPS) + ADAM_WD * w)
    return delta, m, v


def _adamw(w, m, v, grads, name, after=()):
    nl, rows, cols = w.shape
    tm = _tile(rows, max(8, (1 << 18) // cols // 8 * 8), unit=8)
    nbk = rows // tm

    def body(*refs):
        w_ref, m_ref, v_ref = refs[:3]
        g_refs = refs[3:3 + nl]
        go_ref, d_ref, mo_ref, vo_ref = refs[3 + nl + len(after):]
        layer = pl.program_id(0)
        g = g_refs[0][...]
        for l in range(1, nl):
            g = jnp.where(layer == l, g_refs[l][...], g)
        delta, mn, vn = _adam_math(w_ref[...], g, m_ref[...], v_ref[...])
        go_ref[...] = g
        d_ref[...] = delta
        mo_ref[...] = mn
        vo_ref[...] = vn

    stacked = pl.BlockSpec((None, tm, cols), lambda l, i: (l, i, 0))

    def g_spec(layer):
        return pl.BlockSpec((tm, cols), lambda l, i: (jnp.where(l == layer, i, jnp.where(l < layer, 0, nbk - 1)), 0))

    return _pcall(
        body, name=name, grid=(nl, nbk),
        out_shape=[jax.ShapeDtypeStruct(w.shape, F32)] * 4,
        in_specs=[stacked] * 3 + [g_spec(l) for l in range(nl)] + [ANY] * len(after),
        out_specs=[stacked] * 4,
        compiler_params=_params('arbitrary', 'arbitrary'),
    )(w, m, v, *grads, *after)


def _adamw_t(w, m, v, g, name, after=()):
    cols, nl, rows = w.shape
    budget = max(1, (1 << 21) // (8 * rows * 4))
    tc = max(t for t in range(1, min(cols, budget) + 1) if cols % t == 0)

    def body(*refs):
        w_ref, m_ref, v_ref, g_ref = refs[:4]
        d_ref, mo_ref, vo_ref = refs[4 + len(after):]
        delta, mn, vn = _adam_math(w_ref[...], g_ref[...], m_ref[...], v_ref[...])
        d_ref[...] = delta
        mo_ref[...] = mn
        vo_ref[...] = vn

    blk = pl.BlockSpec((tc, nl, rows), lambda i: (i, 0, 0))
    return _pcall(
        body, name=name, grid=(cols // tc,),
        out_shape=[jax.ShapeDtypeStruct(w.shape, F32)] * 3,
        in_specs=[blk] * 4 + [ANY] * len(after), out_specs=[blk] * 3,
        compiler_params=_params('parallel'),
    )(w, m, v, g, *after)


def _pack(arrs, rows_mult=512):
    flat = jnp.concatenate([a.reshape(-1) for a in arrs])
    total = flat.shape[0]
    rows = -(-total // LANES)
    rows = -(-rows // rows_mult) * rows_mult
    flat = jnp.pad(flat, (0, rows * LANES - total))
    return flat.reshape(rows, LANES)


def _unpack(packed, shapes):
    flat = packed.reshape(-1)
    out, off = [], 0
    for s in shapes:
        size = math.prod(s)
        out.append(flat[off:off + size].reshape(s))
        off += size
    return out


def _permute(a):
    seq, w = a.shape
    return a.reshape(S5_SEGMENTS, seq // S5_SEGMENTS, w).transpose(1, 0, 2).reshape(seq, w)


def _unpermute(a):
    seq, w = a.shape
    return a.reshape(seq // S5_SEGMENTS, S5_SEGMENTS, w).transpose(1, 0, 2).reshape(seq, w)


def kernel(x, gla_norm, gla_w_in, gla_w_gate_up, gla_b_gate, gla_o_norm, gla_w_out, s5_norm, s5_w_in, s5_lam_re, s5_lam_im, s5_log_dt, s5_b_re, s5_b_im, s5_c_re, s5_c_im, s5_d, s5_w_out, mlp_norm, mlp_w_up, mlp_w_down, final_norm, loss_target, m_gla_norm, m_gla_w_in, m_gla_w_gate_up, m_gla_b_gate, m_gla_o_norm, m_gla_w_out, m_s5_norm, m_s5_w_in, m_s5_lam_re, m_s5_lam_im, m_s5_log_dt, m_s5_b_re, m_s5_b_im, m_s5_c_re, m_s5_c_im, m_s5_d, m_s5_w_out, m_mlp_norm, m_mlp_w_up, m_mlp_w_down, m_final_norm, v_gla_norm, v_gla_w_in, v_gla_w_gate_up, v_gla_b_gate, v_gla_o_norm, v_gla_w_out, v_s5_norm, v_s5_w_in, v_s5_lam_re, v_s5_lam_im, v_s5_log_dt, v_s5_b_re, v_s5_b_im, v_s5_c_re, v_s5_c_im, v_s5_d, v_s5_w_out, v_mlp_norm, v_mlp_w_up, v_mlp_w_down, v_final_norm):
    weights = dict(gla_norm=gla_norm, gla_w_in=gla_w_in, gla_w_gate_up=gla_w_gate_up, gla_b_gate=gla_b_gate, gla_o_norm=gla_o_norm, gla_w_out=gla_w_out, s5_norm=s5_norm, s5_w_in=s5_w_in, s5_lam_re=s5_lam_re, s5_lam_im=s5_lam_im, s5_log_dt=s5_log_dt, s5_b_re=s5_b_re, s5_b_im=s5_b_im, s5_c_re=s5_c_re, s5_c_im=s5_c_im, s5_d=s5_d, s5_w_out=s5_w_out, mlp_norm=mlp_norm, mlp_w_up=mlp_w_up, mlp_w_down=mlp_w_down, final_norm=final_norm)
    mom1 = dict(gla_norm=m_gla_norm, gla_w_in=m_gla_w_in, gla_w_gate_up=m_gla_w_gate_up, gla_b_gate=m_gla_b_gate, gla_o_norm=m_gla_o_norm, gla_w_out=m_gla_w_out, s5_norm=m_s5_norm, s5_w_in=m_s5_w_in, s5_lam_re=m_s5_lam_re, s5_lam_im=m_s5_lam_im, s5_log_dt=m_s5_log_dt, s5_b_re=m_s5_b_re, s5_b_im=m_s5_b_im, s5_c_re=m_s5_c_re, s5_c_im=m_s5_c_im, s5_d=m_s5_d, s5_w_out=m_s5_w_out, mlp_norm=m_mlp_norm, mlp_w_up=m_mlp_w_up, mlp_w_down=m_mlp_w_down, final_norm=m_final_norm)
    mom2 = dict(gla_norm=v_gla_norm, gla_w_in=v_gla_w_in, gla_w_gate_up=v_gla_w_gate_up, gla_b_gate=v_gla_b_gate, gla_o_norm=v_gla_o_norm, gla_w_out=v_gla_w_out, s5_norm=v_s5_norm, s5_w_in=v_s5_w_in, s5_lam_re=v_s5_lam_re, s5_lam_im=v_s5_lam_im, s5_log_dt=v_s5_log_dt, s5_b_re=v_s5_b_re, s5_b_im=v_s5_b_im, s5_c_re=v_s5_c_re, s5_c_im=v_s5_c_im, s5_d=v_s5_d, s5_w_out=v_s5_w_out, mlp_norm=v_mlp_norm, mlp_w_up=v_mlp_w_up, mlp_w_down=v_mlp_w_down, final_norm=v_final_norm)
    names = list(weights)
    big = ['gla_w_in', 'gla_w_out', 's5_w_in', 's5_w_out', 'mlp_w_up', 'mlp_w_down']
    small = [n for n in names if n not in big]

    chip = 2 * lax.axis_index('x') + lax.axis_index('y')
    h0 = x[0]
    target = loss_target[0]
    seq, dm = h0.shape
    depth = mlp_norm.shape[0]
    n_gla = gla_norm.shape[0]
    n_s5 = s5_lam_re.shape[0]
    rank = gla_w_gate_up.shape[1]
    kw = gla_b_gate.shape[1]
    dv = gla_o_norm.shape[1]
    in_w = 4 * gla_w_in.shape[2]
    vw = (in_w - rank - 2 * kw) // 2
    heads = vw // dv
    dk = kw // heads
    pw = -(-in_w // LANES) * LANES
    s5w = s5_w_in.shape[2]
    n_grp, n_state, grp = s5_b_re.shape[1:]
    hid = 4 * mlp_w_up.shape[2]
    tb = min(seq, 8 * CHUNK)
    tm = _tile(seq, 1024)

    ic = lax.axis_index('c')
    rh = lambda w: w.shape[1] // 2
    wb16 = {n: weights[n].astype(BF16) for n in big}
    gathered = {n: [None] * weights[n].shape[0] for n in big}
    g_w_in, g_gla_out, g_s5_in, g_s5_out, g_up, g_down = (gathered[n] for n in big)
    in_flight = {}

    to_start = []
    for i in range(depth):
        mix = ['gla_w_in', 'gla_w_out'] if i % 2 == 0 else ['s5_w_in', 's5_w_out']
        to_start += [(m, i // 2) for m in mix] + [('mlp_w_up', i), ('mlp_w_down', i)]

    def next_gathers(room):
        keys, jobs = [], []
        while to_start and room > 0:
            n, l = to_start.pop(0)
            rows, cols = weights[n].shape[1:]
            keys.append((n, l))
            jobs.append(('chips_gather', wb16[n][l].reshape(2, rows // 2, cols), (2, 4, rows // 2, cols)))
            room -= 1
        return keys, jobs

    swapping = {}

    def gather_arrived(n, l, after):
        halves, land = _wait(in_flight.pop((n, l)), f'ag_{n}_{l}_wait', after)
        own = lax.dynamic_index_in_dim(halves, ic, 0, keepdims=True)
        land = lax.dynamic_update_slice(land, own[:, None], (ic, chip, 0, 0))
        keys, jobs = next_gathers(GATHERS_IN_FLIGHT - len(in_flight))
        handles = _start_many([('pair_inplace', None, land)] + jobs, f'ag_{n}_{l}_pair_start', [])
        in_flight.update(zip(keys, handles[1:]))
        swapping[n, l] = handles[0]

    def gather_ready(n, l):
        gathered[n][l] = _wait(swapping.pop((n, l)), f'ag_{n}_{l}_pair_wait', [])[0]
        return [gathered[n][l]]

    def finish_gather(n, l, after):
        gather_arrived(n, l, after)
        return gather_ready(n, l)

    sharded_small = [gla_w_gate_up, s5_norm, s5_d]
    gathered_small = _exchange(_pack(sharded_small), 'xy', 'ag_small')
    keys, jobs = next_gathers(GATHERS_IN_FLIGHT)
    in_flight.update(zip(keys, _start_many(jobs, 'ag_first_start', [gathered_small])))
    parts = [_unpack(gathered_small[k], [a.shape for a in sharded_small]) for k in range(4)]
    wgu_full = jnp.concatenate([p[0] for p in parts], axis=2)
    s5_norm_full = jnp.concatenate([p[1] for p in parts], axis=1)
    s5_d_full = jnp.concatenate([p[2] for p in parts], axis=1)

    def gla_w_in_padded(j):
        wj = g_w_in[j].transpose(0, 2, 1, 3).reshape(dm, in_w)
        return jnp.pad(wj, ((0, 0), (0, pw - in_w)))

    grads = {n: [None] * weights[n].shape[0] for n in names if n != 'final_norm'}

    saved = []
    h = h0
    for i in range(depth):
        j = i // 2
        rec = {}
        if i % 2 == 0:
            rec['h_in'] = h
            behind = finish_gather('gla_w_in', j, [h])
            hn = _norm_fwd(h, gla_norm[j:j + 1], 'gla_norm_fwd', behind)
            w_in_pad = gla_w_in_padded(j)
            proj = _mm('gla_proj', 'nn', _plain(hn), _plain(w_in_pad), (seq, pw, dm),
                       [((seq, pw), F32, _plain_shape(None))], (tm, _tile(pw, 1024), dm))[0]
            wgu_pad = jnp.pad(wgu_full[j], ((0, LANES - rank), (0, 0)))
            gather_arrived('gla_w_out', j, [proj])
            gated, states = _gla_scan_fwd(proj, wgu_pad, gla_b_gate[j:j + 1], gla_o_norm[j:j + 1],
                                          heads, kw, vw, tb, 'gla_scan_fwd')
            behind = gather_ready('gla_w_out', j)
            h = _mm('gla_out', 'nn', _plain(gated), _w_rows(g_gla_out, j), (seq, dm, vw),
                    [((seq, dm), F32, _plain_shape(None))],
                    (tm, _tile(dm, 1024), gla_w_out.shape[1]),
                    epilogue=lambda acc, hv: (acc + hv,), extras=[_plain(h)], after=behind)[0]
            rec.update(hn=hn, w_in_pad=w_in_pad, proj=proj, wgu_pad=wgu_pad, gated=gated, states=states)
        else:
            hp = _permute(h)
            rec['h_in'] = hp
            hn = _norm_fwd(hp, s5_norm_full[j:j + 1], 's5_norm_fwd')
            behind = finish_gather('s5_w_in', j, [hn])
            u = _mm('s5_in', 'nn', _plain(hn), _w_rows(g_s5_in, j), (seq, s5w, dm),
                    [((seq, s5w), F32, _plain_shape(None))],
                    (tm, _tile(s5w, 1024), s5_w_in.shape[1]), after=behind)[0]
            disc, disc_vjp = jax.vjp(_s5_discretise, s5_lam_re[j], s5_lam_im[j], s5_log_dt[j], s5_b_re[j], s5_b_im[j])
            ab_re, ab_im, bb_re, bb_im = disc
            nb = n_grp // S5_GROUPS_PER_BLOCK
            per_block = lambda m: m.transpose(0, 2, 1).reshape(nb, -1, m.shape[1])
            bre, bim = per_block(bb_re), per_block(bb_im)
            cre, cim = per_block(s5_c_re[j]), per_block(s5_c_im[j])
            are = ab_re.reshape(nb, 1, S5_GROUPS_PER_BLOCK * n_state)
            aim = ab_im.reshape(nb, 1, S5_GROUPS_PER_BLOCK * n_state)
            dskip = s5_d_full[j:j + 1]
            gather_arrived('s5_w_out', j, [u])
            y, z, xr, xi = _s5_fwd(u, bre, bim, cre, cim, are, aim, dskip, 's5_scan_fwd')
            behind = gather_ready('s5_w_out', j)
            o = _mm('s5_out', 'nn', _plain(z), _w_cols(g_s5_out, j), (seq, 2 * dm, s5w),
                    [((seq, 2 * dm), F32, _plain_shape(None))],
                    (tm, _tile(s5_w_out.shape[2], 1024), s5_w_out.shape[1]), after=behind)[0]
            h = _unpermute(_glu_fwd(o, hp, 's5_glu_fwd'))
            rec.update(hn=hn, u=u, y=y, z=z, xr=xr, xi=xi, o=o, mats=(bre, bim, cre, cim, are, aim, dskip),
                       disc_vjp=disc_vjp)
        rec['h_mid'] = h
        hn2 = _norm_fwd(h, mlp_norm[i:i + 1], 'mlp_norm_fwd')
        behind = finish_gather('mlp_w_up', i, [hn2])
        act, act2 = _mm('mlp_up', 'nn', _plain(hn2), _w_cols(g_up, i), (seq, hid, dm),
                        [((seq, hid), BF16, _plain_shape(None))] * 2,
                        (tm, _tile(mlp_w_up.shape[2], 1024), mlp_w_up.shape[1]),
                        epilogue=lambda acc: (jnp.maximum(acc, 0.0), jnp.square(jnp.maximum(acc, 0.0))),
                        after=behind)
        behind = finish_gather('mlp_w_down', i, [act2])
        h = _mm('mlp_down', 'nn', _plain(act2), _w_rows(g_down, i), (seq, dm, hid),
                [((seq, dm), F32, _plain_shape(None))],
                (tm, _tile(dm, 1024), mlp_w_down.shape[1]),
                epilogue=lambda acc, hv: (acc + hv,), extras=[_plain(h)], after=behind)[0]
        rec.update(hn2=hn2, act=act, act2=act2)
        saved.append(rec)

    dh, loss_cols, d_final = _loss_head(h, target, final_norm.reshape(1, dm), 'loss_head')
    loss = lax.psum(jnp.sum(loss_cols), ('x', 'y', 'c'))
    grads['final_norm'] = [d_final.reshape(dm)]

    big_grads = {n: [None] * weights[n].shape[0] for n in big}
    reducing = []

    def reduce_next(item, after):
        n, l, hd = item['n'], item['l'], item['hd']
        if item['stage'] == 'pair':
            dw, got = _wait(hd, f'rs_{n}_{l}_pair_wait', after)
            _, _, rows_h, cols = dw.shape
            pre = _sum_pair(dw.reshape(2, 4 * rows_h, cols), got.reshape(4 * rows_h, cols), ic, BF16,
                            f'rs_{n}_pairsum').reshape(4, rows_h, cols)
            item['stage'] = 'chips'
            return ('chips_a2a', pre, pre.shape)
        if item['stage'] == 'chips':
            pre, yb = _wait(hd, f'rs_{n}_{l}_wait', after)
            yb = lax.dynamic_update_index_in_dim(yb, lax.dynamic_index_in_dim(pre, chip, 0, keepdims=False), chip, 0)
            fin = _sum_slots(yb, F32, f'rs_{n}_chipsum')
            item['stage'] = 'back'
            return ('pair_bcast', fin, (2,) + fin.shape)
        fin, both = _wait(hd, f'rs_{n}_{l}_back_wait', after)
        both = lax.dynamic_update_index_in_dim(both, fin, ic, 0)
        big_grads[n][l] = both.reshape(2 * fin.shape[0], fin.shape[1])
        item.update(stage='done', hd=None)
        return None

    def reduce_many(items, name, after):
        jobs = [(item, reduce_next(item, after)) for item in items]
        jobs = [(item, job) for item, job in jobs if job is not None]
        if not jobs:
            return after
        for (item, _), hd in zip(jobs, _start_many([job for _, job in jobs], name, after)):
            item['hd'] = hd
        return [jobs[0][0]['hd']['token']]

    def reduce_scatter(dw, n, l):
        new = dict(n=n, l=l, stage='new', hd=None)
        older = [reducing[-k] for k in (1, 3) if len(reducing) >= k]
        jobs = [(item, reduce_next(item, [dw])) for item in older] + [(new, ('pair_swap', dw, dw.shape[1:]))]
        new['stage'] = 'pair'
        reducing.append(new)
        for (item, _), hd in zip(jobs, _start_many([job for _, job in jobs], f'rs_{n}_{l}_start', [])):
            item['hd'] = hd
        return [new['hd']['token']]

    for i in reversed(range(depth)):
        j = i // 2
        rec = saved[i]
        r_dn, c_dn = mlp_w_down.shape[1:]
        shape, spec = _dw_rows(r_dn, c_dn)
        dw = _mm('mlp_down_dw', 'tn', _plain(rec['act2']), _plain(dh), (hid, dm, seq),
                 [(shape, BF16, spec)], (_tile(r_dn // 2, 1024), _tile(c_dn, 1024), seq))[0]
        behind = reduce_scatter(dw, 'mlp_w_down', i)
        dpre = _mm('mlp_down_dx', 'nt', _plain(dh), _w_rows(g_down, i), (seq, hid, dm),
                   [((seq, hid), BF16, _plain_shape(None))],
                   (tm, _tile(rh(mlp_w_down), 1024), dm),
                   epilogue=lambda acc, av: (acc * (2.0 * av.astype(F32)),), extras=[_plain(rec['act'])],
                   after=behind)[0]
        r_up, c_up = mlp_w_up.shape[1:]
        shape, spec = _dw_cols(r_up, c_up)
        dw = _mm('mlp_up_dw', 'tn', _plain(rec['hn2']), _plain(dpre), (dm, hid, seq),
                 [(shape, BF16, spec)], (_tile(r_up // 2, 1024), _tile(c_up, 1024), seq))[0]
        behind = reduce_scatter(dw, 'mlp_w_up', i)
        dhn = _mm('mlp_up_dx', 'nt', _plain(dpre), _w_cols(g_up, i), (seq, dm, hid),
                  [((seq, dm), F32, _plain_shape(None))],
                  (tm, _tile(rh(mlp_w_up), 1024), _tile(mlp_w_up.shape[2], 2048)), after=behind)[0]
        dh, dg = _norm_bwd(rec['h_mid'], dhn, dh, mlp_norm[i:i + 1], 'mlp_norm_bwd')
        grads['mlp_norm'][i] = dg[0]

        if i % 2 == 0:
            r_o, c_o = gla_w_out.shape[1:]
            shape, spec = _dw_rows(r_o, c_o)
            dw = _mm('gla_out_dw', 'tn', _plain(rec['gated']), _plain(dh), (vw, dm, seq),
                     [(shape, BF16, spec)], (_tile(r_o // 2, 1024), _tile(c_o, 1024), seq))[0]
            behind = reduce_scatter(dw, 'gla_w_out', j)
            dgated = _mm('gla_out_dx', 'nt', _plain(dh), _w_rows(g_gla_out, j), (seq, vw, dm),
                         [((seq, vw), F32, _plain_shape(None))],
                         (tm, _tile(rh(gla_w_out), 1024), dm), after=behind)[0]
            dq, dkk, dvv, dr, dpre_g, db, don = _gla_scan_bwd(
                rec['proj'], rec['wgu_pad'], gla_b_gate[j:j + 1], gla_o_norm[j:j + 1], rec['states'], dgated,
                heads, kw, vw, tb, 'gla_scan_bwd')
            grads['gla_b_gate'][j] = db[0]
            grads['gla_o_norm'][j] = don[0]
            dgl = _mm('gla_gate_dx', 'nt', _plain(dpre_g), _plain(rec['wgu_pad']), (seq, LANES, kw),
                      [((seq, LANES), BF16, _plain_shape(None))], (tm, LANES, kw))[0]
            g_low = rec['proj'][:, pw - LANES:]
            dwgu = _mm('gla_gate_dw', 'tn', _plain(g_low), _plain(dpre_g), (LANES, kw, seq),
                       [((LANES, kw), F32, _plain_shape(None))], (LANES, kw, seq))[0]
            grads['gla_w_gate_up'][j] = dwgu[:rank]
            dproj = jnp.concatenate([dq, dkk, dvv, dr, dgl], axis=1)
            dw_pad = _mm('gla_proj_dw', 'tn', _plain(rec['hn']), _plain(dproj), (dm, pw, seq),
                         [((dm, pw), BF16, _plain_shape(None))], (_tile(dm, 1024), _tile(pw, 1024), seq))[0]
            shard_w = in_w // 4
            dw = dw_pad[:, :in_w].reshape(2, dm // 2, 4, shard_w).transpose(0, 2, 1, 3)
            behind = reduce_scatter(dw, 'gla_w_in', j)
            dhn = _mm('gla_proj_dx', 'nt', _plain(dproj), _plain(rec['w_in_pad']), (seq, dm, pw),
                      [((seq, dm), F32, _plain_shape(None))], (tm, _tile(dm, 1024), _tile(pw, 1024)),
                      after=behind)[0]
            dh, dg = _norm_bwd(rec['h_in'], dhn, dh, gla_norm[j:j + 1], 'gla_norm_bwd')
            grads['gla_norm'][j] = dg[0]
        else:
            dhp = _permute(dh)
            do = _glu_bwd(rec['o'], dhp, 's5_glu_bwd')
            r_o, c_o = s5_w_out.shape[1:]
            shape, spec = _dw_cols(r_o, c_o)
            dw = _mm('s5_out_dw', 'tn', _plain(rec['z']), _plain(do), (s5w, 2 * dm, seq),
                     [(shape, BF16, spec)], (_tile(r_o // 2, 1024), _tile(c_o, 1024), seq))[0]
            behind = reduce_scatter(dw, 's5_w_out', j)
            dz = _mm('s5_out_dx', 'nt', _plain(do), _w_cols(g_s5_out, j), (seq, s5w, 2 * dm),
                     [((seq, s5w), F32, _plain_shape(None))],
                     (tm, _tile(rh(s5_w_out), 1024), _tile(s5_w_out.shape[2], 1024)), after=behind)[0]
            bre, bim, cre, cim, are, aim, dskip = rec['mats']
            du, dcr, dci, dbr, dbi, dar, dai, dd = _s5_bwd(dz, rec['y'], rec['u'], rec['xr'], rec['xi'],
                                                           bre, bim, cre, cim, are, aim, dskip, 's5_scan_bwd')
            grads['s5_c_re'][j] = dcr.reshape(n_grp, grp, n_state)
            grads['s5_c_im'][j] = dci.reshape(n_grp, grp, n_state)
            dbb_re = dbr.reshape(n_grp, grp, n_state).transpose(0, 2, 1)
            dbb_im = dbi.reshape(n_grp, grp, n_state).transpose(0, 2, 1)
            d_lr, d_li, d_dt, d_bre, d_bim = rec['disc_vjp'](
                (dar.reshape(n_grp, n_state), dai.reshape(n_grp, n_state), dbb_re, dbb_im))
            grads['s5_lam_re'][j] = d_lr
            grads['s5_lam_im'][j] = d_li
            grads['s5_log_dt'][j] = d_dt
            grads['s5_b_re'][j] = d_bre
            grads['s5_b_im'][j] = d_bim
            grads['s5_d'][j] = dd[0]
            r_i, c_i = s5_w_in.shape[1:]
            shape, spec = _dw_rows(r_i, c_i)
            dw = _mm('s5_in_dw', 'tn', _plain(rec['hn']), _plain(du), (dm, s5w, seq),
                     [(shape, BF16, spec)], (_tile(r_i // 2, 1024), _tile(c_i, 1024), seq))[0]
            behind = reduce_scatter(dw, 's5_w_in', j)
            dhn = _mm('s5_in_dx', 'nt', _plain(du), _w_rows(g_s5_in, j), (seq, dm, s5w),
                      [((seq, dm), F32, _plain_shape(None))],
                      (tm, _tile(rh(s5_w_in), 1024), _tile(s5w, 1024)), after=behind)[0]
            dhp, dg = _norm_bwd(rec['h_in'], dhn, dhp, s5_norm_full[j:j + 1], 's5_norm_bwd')
            dh = _unpermute(dhp)
            grads['s5_norm'][j] = dg[0]
    grad_x = dh[None]
    behind = [dh]
    for k, (stage, items) in enumerate((('pair', reducing), ('chips', reducing[:-1]), ('back', reducing[:-1]))):
        behind = reduce_many([item for item in items if item['stage'] == stage], f'rs_tail_{k}_start', behind)

    local_small = [jnp.stack(grads[n]) if n != 'final_norm' else grads[n][0] for n in small]
    full_shapes = [a.shape for a in local_small]
    packed_small = _pack(local_small)
    ar_small = _start('chips_bcast', packed_small, (4,) + packed_small.shape, 'ar_small_start', [])
    out_g, out_d, out_m, out_v = {}, {}, {}, {}
    behind = [ar_small['token']]
    last_n = reducing[-1]['n']
    for n in [m for m in big if m != last_n] + [last_n]:
        if n == last_n:
            for k in range(2):
                behind = reduce_many([reducing[-1]], f'rs_last_{k}_start', behind)
        if weights[n].shape[2] % LANES:
            to_t, from_t = (lambda a: a.transpose(2, 0, 1)), (lambda a: a.transpose(1, 2, 0))
            g_t = to_t(jnp.stack(big_grads[n]))
            res = _adamw_t(to_t(weights[n]), to_t(mom1[n]), to_t(mom2[n]), g_t, 'adamw_' + n, behind)
            out_g[n], out_d[n], out_m[n], out_v[n] = (from_t(a) for a in (g_t,) + tuple(res))
            behind = [res[0]]
        else:
            out_g[n], out_d[n], out_m[n], out_v[n] = _adamw(weights[n], mom1[n], mom2[n], big_grads[n],
                                                            'adamw_' + n, behind)
            behind = [out_d[n]]
    sent, by_chip = _wait(ar_small, 'ar_small_wait', behind)
    by_chip = lax.dynamic_update_index_in_dim(by_chip, sent, chip, 0)
    gathered = _exchange(by_chip, 'c', 'ar_small_c')
    rows = gathered.shape[2]
    summed = _sum_slots(gathered.reshape(8, rows, LANES), F32, 'ar_small_sum')
    small_full = dict(zip(small, _unpack(summed, full_shapes)))
    small_grad = {}
    for n in small:
        g = small_full[n]
        if g.shape != weights[n].shape:
            ax = [a for a in range(g.ndim) if g.shape[a] != weights[n].shape[a]][0]
            g = lax.dynamic_slice_in_dim(g, chip * weights[n].shape[ax], weights[n].shape[ax], axis=ax)
        small_grad[n] = g

    shapes = [weights[n].shape for n in small]
    pw_, pm_, pv_, pg_ = (_pack([d[n] for n in small]) for d in (weights, mom1, mom2, small_grad))
    _, sd, sm, sv = _adamw(pw_[None], pm_[None], pv_[None], [pg_], 'adamw_small')
    for n, d_, m_, v_ in zip(small, _unpack(sd[0], shapes), _unpack(sm[0], shapes), _unpack(sv[0], shapes)):
        out_g[n], out_d[n], out_m[n], out_v[n] = small_grad[n], d_, m_, v_

    return (loss, grad_x, *[out_g[n] for n in names], *[out_d[n] for n in names],
            *[out_m[n] for n in names], *[out_v[n] for n in names])
```

```python
import functools
import math

import jax
import jax.numpy as jnp
from jax import lax
from jax.experimental import pallas as pl
from jax.experimental.pallas import tpu as pltpu

F32 = jnp.float32
BF16 = jnp.bfloat16

EPS = 1e-6
CHUNK = 64
GLA_GATE_TEMP = 16.0
S5_EIG_CLIP = -1e-4
S5_SEGMENTS = 8
S5_GROUPS_PER_BLOCK = 8
LANES = 128
ADAM_LR = 0.001
ADAM_B1 = 0.9
ADAM_B2 = 0.999
ADAM_EPS = 1e-08
ADAM_WD = 0.01
ADAM_STEP = 10
VMEM_LIMIT_BYTES = 56 * 1024 * 1024
GATHERS_IN_FLIGHT = 2

MESH = pl.DeviceIdType.MESH
ANY = pl.BlockSpec(memory_space=pl.ANY)
IN_VMEM = pl.BlockSpec(memory_space=pltpu.VMEM)
IN_HBM = pl.BlockSpec(memory_space=pltpu.HBM)
IN_SEM = pl.BlockSpec(memory_space=pltpu.SEMAPHORE)
DATAFLOW = pltpu.SideEffectType.DATAFLOW_SIDE_EFFECTING


def _pcall(body, **kw):
    return pl.pallas_call(body, **kw)


def _params(*sem):
    return pltpu.CompilerParams(dimension_semantics=sem, vmem_limit_bytes=VMEM_LIMIT_BYTES)


def _tile(dim, target, unit=LANES):
    if dim <= target:
        return dim
    best = None
    for t in range(unit, target + 1, unit):
        if dim % t == 0:
            best = t
    assert best is not None, (dim, target)
    return best


_FLIPS = {'xy': [(1, 0, 0), (0, 1, 0), (1, 1, 0)], 'c': [(0, 0, 1)]}


def _exchange(x, group, name):
    n = 2 if group == 'c' else 4
    flips = _FLIPS[group]

    def body(x_ref, y_ref, send_sems, recv_sems, local_sem):
        ix, iy, ic = lax.axis_index('x'), lax.axis_index('y'), lax.axis_index('c')

        def slot(px, py, pc):
            return pc if group == 'c' else 2 * px + py

        me = (ix, iy, ic)
        local = pltpu.make_async_copy(x_ref, y_ref.at[slot(*me)], local_sem)
        local.start()
        peers = []
        for fx, fy, fc in flips:
            peers.append((1 - ix if fx else ix, 1 - iy if fy else iy, 1 - ic if fc else ic))
        sends = []
        for k, peer in enumerate(peers):
            cp = pltpu.make_async_remote_copy(
                src_ref=x_ref, dst_ref=y_ref.at[slot(*me)],
                send_sem=send_sems.at[k], recv_sem=recv_sems.at[k],
                device_id=peer, device_id_type=MESH)
            cp.start()
            sends.append(cp)
        for k, peer in enumerate(peers):
            pltpu.make_async_remote_copy(
                src_ref=x_ref, dst_ref=y_ref.at[slot(*peer)],
                send_sem=send_sems.at[k], recv_sem=recv_sems.at[k],
                device_id=peer, device_id_type=MESH).wait_recv()
        for cp in sends:
            cp.wait_send()
        local.wait()

    return _pcall(
        body, name=name,
        out_shape=jax.ShapeDtypeStruct((n,) + tuple(x.shape), x.dtype),
        in_specs=[IN_VMEM], out_specs=IN_VMEM,
        scratch_shapes=[pltpu.SemaphoreType.DMA((len(flips),)),
                        pltpu.SemaphoreType.DMA((len(flips),)),
                        pltpu.SemaphoreType.DMA(())],
        compiler_params=pltpu.CompilerParams(vmem_limit_bytes=VMEM_LIMIT_BYTES),
    )(x)


_KIND_GROUP = {'chips_a2a': 'xy', 'chips_bcast': 'xy', 'chips_gather': 'xy',
               'pair_swap': 'c', 'pair_bcast': 'c', 'pair_inplace': 'c'}


def _plan(kind, x_ref, land_ref, sems):
    ix, iy, ic = lax.axis_index('x'), lax.axis_index('y'), lax.axis_index('c')
    flips = _FLIPS[_KIND_GROUP[kind]]
    me_chip = 2 * ix + iy
    out = []
    for k, (fx, fy, fc) in enumerate(flips):
        peer = (1 - ix if fx else ix, 1 - iy if fy else iy, 1 - ic if fc else ic)
        peer_chip = 2 * peer[0] + peer[1]
        if kind == 'chips_a2a':
            src, dst, got = x_ref.at[peer_chip], land_ref.at[me_chip], land_ref.at[peer_chip]
        elif kind == 'chips_bcast':
            src, dst, got = x_ref, land_ref.at[me_chip], land_ref.at[peer_chip]
        elif kind == 'chips_gather':
            src, dst, got = x_ref.at[ic], land_ref.at[ic, me_chip], land_ref.at[ic, peer_chip]
        elif kind == 'pair_swap':
            src, dst, got = x_ref.at[1 - ic], land_ref, land_ref
        elif kind == 'pair_bcast':
            src, dst, got = x_ref, land_ref.at[ic], land_ref.at[1 - ic]
        else:
            src, dst, got = land_ref.at[ic], land_ref.at[ic], land_ref.at[1 - ic]
        mk = lambda d, src=src, k=k, peer=peer: pltpu.make_async_remote_copy(
            src_ref=src, dst_ref=d, send_sem=sems[k], recv_sem=sems[len(flips) + k],
            device_id=peer, device_id_type=MESH)
        out.append((mk(dst), mk(got)))
    return out


def _start_many(jobs, name, after):
    per_job = []
    for kind, x, land in jobs:
        if not hasattr(land, 'dtype'):
            land = lax.empty(tuple(land), x.dtype)
        per_job.append((kind, ([] if x is None else [x]) + [land], 2 * len(_FLIPS[_KIND_GROUP[kind]])))
    arrays = [a for _, arrs, _ in per_job for a in arrs]
    n_arr, n_after = len(arrays), len(after)
    n_sems = sum(ns for _, _, ns in per_job)

    def body(*refs):
        a0, s0 = 0, n_arr + n_after
        for kind, arrs, ns in per_job:
            x_ref = refs[a0] if len(arrs) == 2 else None
            for send, _ in _plan(kind, x_ref, refs[a0 + len(arrs) - 1], refs[s0:s0 + ns]):
                send.start()
            a0, s0 = a0 + len(arrs), s0 + ns
        refs[-1][...] = jnp.zeros_like(refs[-1])

    outs = _pcall(
        body, name=name,
        out_shape=(pltpu.SemaphoreType.DMA(()),) * n_sems + tuple(pltpu.HBM(a.shape, a.dtype) for a in arrays)
        + (jax.ShapeDtypeStruct((8, LANES), F32),),
        in_specs=(IN_HBM,) * n_arr + (ANY,) * n_after,
        out_specs=(IN_SEM,) * n_sems + (IN_HBM,) * n_arr + (IN_VMEM,),
        input_output_aliases={i: n_sems + i for i in range(n_arr)},
        compiler_params=pltpu.CompilerParams(has_side_effects=DATAFLOW),
    )(*[pltpu.with_memory_space_constraint(a, pltpu.HBM) for a in arrays], *after)
    handles, a0, s0 = [], n_sems, 0
    for kind, arrs, ns in per_job:
        handles.append(dict(kind=kind, sems=outs[s0:s0 + ns], arrays=outs[a0:a0 + len(arrs)], token=outs[-1]))
        a0, s0 = a0 + len(arrs), s0 + ns
    return handles


def _start(kind, x, land, name, after):
    return _start_many([(kind, x, land)], name, after)[0]


def _wait(handle, name, after):
    kind, arrays, sems = handle['kind'], handle['arrays'], handle['sems']
    n_arr, n_sems = len(arrays), len(sems)

    def body(*refs):
        for _, got in _plan(kind, refs[0] if n_arr == 2 else None, refs[n_arr - 1], refs[n_arr:n_arr + n_sems]):
            got.wait_send()
            got.wait_recv()

    return _pcall(
        body, name=name,
        out_shape=tuple(pltpu.HBM(a.shape, a.dtype) for a in arrays),
        in_specs=(IN_HBM,) * n_arr + (IN_SEM,) * n_sems + (ANY,) * len(after),
        out_specs=(IN_HBM,) * n_arr,
        input_output_aliases={i: i for i in range(n_arr)},
        compiler_params=pltpu.CompilerParams(has_side_effects=DATAFLOW),
    )(*arrays, *sems, *after)


def _sum_pair(x, recv, ic, out_dtype, name):
    _, rows, cols = x.shape
    tm = _tile(rows, max(8, (1 << 20) // cols // 8 * 8), unit=8)

    def body(c_ref, x_ref, r_ref, o_ref):
        o_ref[...] = (x_ref[...].astype(F32) + r_ref[...].astype(F32)).astype(o_ref.dtype)

    return _pcall(
        body, name=name,
        grid_spec=pltpu.PrefetchScalarGridSpec(
            num_scalar_prefetch=1, grid=(rows // tm,),
            in_specs=[pl.BlockSpec((None, tm, cols), lambda i, c: (c[0], i, 0)),
                      pl.BlockSpec((tm, cols), lambda i, c: (i, 0))],
            out_specs=pl.BlockSpec((tm, cols), lambda i, c: (i, 0))),
        out_shape=jax.ShapeDtypeStruct((rows, cols), out_dtype),
        compiler_params=_params('parallel'),
    )(jnp.reshape(ic, (1,)).astype(jnp.int32), x, recv)


def _sum_slots(y, out_dtype, name):
    n, rows, cols = y.shape
    tm = rows if rows % 8 else _tile(rows, max(8, (1 << 20) // (n * cols) // 8 * 8), unit=8)

    def body(y_ref, o_ref):
        acc = y_ref[0].astype(F32)
        for k in range(1, n):
            acc = acc + y_ref[k].astype(F32)
        o_ref[...] = acc.astype(o_ref.dtype)

    return _pcall(
        body, name=name, grid=(rows // tm,),
        out_shape=jax.ShapeDtypeStruct((rows, cols), out_dtype),
        in_specs=[pl.BlockSpec((n, tm, cols), lambda i: (0, i, 0))],
        out_specs=pl.BlockSpec((tm, cols), lambda i: (i, 0)),
        compiler_params=_params('parallel'),
    )(y)


class _Op:
    def __init__(self, arr, spec):
        self.arr = arr
        self.spec = spec


def _plain(arr):
    return _Op(arr, lambda t0, t1: ((t0, t1), lambda b0, b1: (b0, b1)))


def _plain_shape(shape):
    return lambda t0, t1: ((t0, t1), lambda b0, b1: (b0, b1))


def _dw_cols(rows, cols):
    rh = rows // 2

    def spec(t0, t1):
        assert (rh % t0 == 0 or t0 == rows) and cols % t1 == 0, (rh, cols, t0, t1)
        qr, qc = max(1, rh // t0), cols // t1
        if t0 == rows:
            return (2, None, rh, t1), lambda b0, b1: (0, b1 // qc, 0, b1 % qc)
        return (None, None, t0, t1), lambda b0, b1: (b0 // qr, b1 // qc, b0 % qr, b1 % qc)
    return (2, 4, rh, cols), spec


def _dw_rows(rows, cols):
    rh = rows // 2

    def spec(t0, t1):
        assert (rh % t0 == 0 or t0 == rows) and cols % t1 == 0, (rh, cols, t0, t1)
        qr = max(1, rh // t0)
        if t0 == rows:
            return (2, None, rh, t1), lambda b0, b1: (0, b0, 0, b1)
        return (None, None, t0, t1), lambda b0, b1: ((b0 // qr) % 2, b0 // (2 * qr), b0 % qr, b1)
    return (2, 4, rh, cols), spec


def _w_cols(g, j):
    _, _, rh, cols = g[j].shape
    return _Op(g[j], _dw_cols(2 * rh, cols)[1])


def _w_rows(g, j):
    _, _, rh, cols = g[j].shape
    return _Op(g[j], _dw_rows(2 * rh, cols)[1])


def _mm(name, mode, a, b, dims, outs, tiles, epilogue=None, extras=(), after=()):
    m, n, k = dims
    tm, tn, tk = tiles
    assert m % tm == 0 and n % tn == 0 and k % tk == 0, (name, dims, tiles)
    nk = k // tk
    if mode == 'nn':
        a_t, a_ix, b_t, b_ix, ca, cb = (tm, tk), (lambda i, j, kk: (i, kk)), (tk, tn), (lambda i, j, kk: (kk, j)), 1, 0
    elif mode == 'nt':
        a_t, a_ix, b_t, b_ix, ca, cb = (tm, tk), (lambda i, j, kk: (i, kk)), (tn, tk), (lambda i, j, kk: (j, kk)), 1, 1
    else:
        a_t, a_ix, b_t, b_ix, ca, cb = (tk, tm), (lambda i, j, kk: (kk, i)), (tk, tn), (lambda i, j, kk: (kk, j)), 0, 0
    a_blk, a_fn = a.spec(*a_t)
    b_blk, b_fn = b.spec(*b_t)
    in_specs = [pl.BlockSpec(a_blk, lambda i, j, kk: a_fn(*a_ix(i, j, kk))),
                pl.BlockSpec(b_blk, lambda i, j, kk: b_fn(*b_ix(i, j, kk)))]
    operands = [a.arr, b.arr]
    for e in extras:
        e_blk, e_fn = e.spec(tm, tn)
        in_specs.append(pl.BlockSpec(e_blk, functools.partial(lambda i, j, kk, f: f(i, j), f=e_fn)))
        operands.append(e.arr)
    out_shapes, out_specs = [], []
    for shape, dtype, spec in outs:
        o_blk, o_fn = spec(tm, tn)
        out_shapes.append(jax.ShapeDtypeStruct(shape, dtype))
        out_specs.append(pl.BlockSpec(o_blk, functools.partial(lambda i, j, kk, f: f(i, j), f=o_fn)))
    n_ex, n_out = len(extras), len(outs)
    in_specs += [ANY] * len(after)
    operands += list(after)
    if epilogue is None:
        epilogue = lambda acc: (acc,)

    def body(a_ref, b_ref, *rest):
        ex_refs = rest[:n_ex]
        rest = rest[:n_ex] + rest[n_ex + len(after):]
        out_refs = rest[n_ex:n_ex + n_out]
        bv = b_ref[...]
        if bv.ndim == 3:
            bv = bv.reshape(bv.shape[0] * bv.shape[1], bv.shape[2])
        p = lax.dot_general(a_ref[...].astype(BF16), bv.astype(BF16),
                            (((ca,), (cb,)), ((), ())), preferred_element_type=F32)

        def finish(acc):
            res = epilogue(acc, *[r[...] for r in ex_refs])
            for o_ref, val in zip(out_refs, res):
                o_ref[...] = val.astype(o_ref.dtype)

        if nk == 1:
            finish(p)
        else:
            acc_ref = rest[n_ex + n_out]
            kk = pl.program_id(2)

            @pl.when(kk == 0)
            def _():
                acc_ref[...] = p

            @pl.when(kk > 0)
            def _():
                acc_ref[...] += p

            @pl.when(kk == nk - 1)
            def _():
                finish(acc_ref[...])

    res = _pcall(
        body, name=name, grid=(m // tm, n // tn, nk),
        out_shape=out_shapes, in_specs=in_specs, out_specs=out_specs,
        scratch_shapes=[pltpu.VMEM((tm, tn), F32)] if nk > 1 else [],
        compiler_params=_params('parallel', 'parallel', 'arbitrary'),
    )(*operands)
    return res


def _rowwise(name, fn, row_ins, vec_ins, outs, reds, tm, after=()):
    rows = row_ins[0].shape[0]
    assert rows % tm == 0
    n_in = len(row_ins) + len(vec_ins)
    n_out = len(outs)

    def body(*refs):
        vals = [r[...] for r in refs[:n_in]]
        refs = refs[:n_in] + refs[n_in + len(after):]
        res = fn(*vals)
        for o_ref, val in zip(refs[n_in:n_in + n_out], res[:n_out]):
            o_ref[...] = val.astype(o_ref.dtype)
        first = pl.program_id(0) == 0
        for r_ref, val in zip(refs[n_in + n_out:], res[n_out:]):
            @pl.when(first)
            def _(r_ref=r_ref, val=val):
                r_ref[...] = val

            @pl.when(jnp.logical_not(first))
            def _(r_ref=r_ref, val=val):
                r_ref[...] += val

    in_specs = [pl.BlockSpec((tm, a.shape[1]), lambda i: (i, 0)) for a in row_ins]
    in_specs += [pl.BlockSpec((1, v.shape[1]), lambda i: (0, 0)) for v in vec_ins]
    in_specs += [ANY] * len(after)
    out_shapes = [jax.ShapeDtypeStruct((rows, w), dt) for w, dt in outs]
    out_shapes += [jax.ShapeDtypeStruct((1, w), F32) for w in reds]
    out_specs = [pl.BlockSpec((tm, w), lambda i: (i, 0)) for w, _ in outs]
    out_specs += [pl.BlockSpec((1, w), lambda i: (0, 0)) for w in reds]
    return _pcall(
        body, name=name, grid=(rows // tm,),
        out_shape=out_shapes, in_specs=in_specs, out_specs=out_specs,
        compiler_params=_params('arbitrary'),
    )(*row_ins, *vec_ins, *after)


def _norm_fwd(h, g, name, after=()):
    def fn(hv, gv):
        rstd = lax.rsqrt(jnp.mean(hv * hv, axis=-1, keepdims=True) + EPS)
        return (hv * rstd * gv,)
    return _rowwise(name, fn, [h], [g], [(h.shape[1], BF16)], [], 256, after)[0]


def _norm_bwd(h, dhn, dres, g, name):
    def fn(hv, dv, rv, gv):
        rstd = lax.rsqrt(jnp.mean(hv * hv, axis=-1, keepdims=True) + EPS)
        xhat = hv * rstd
        dxhat = dv * gv
        dh = rv + rstd * (dxhat - xhat * jnp.mean(dxhat * xhat, axis=-1, keepdims=True))
        return dh, jnp.sum(dv * xhat, axis=0, keepdims=True)
    w = h.shape[1]
    return _rowwise(name, fn, [h, dhn, dres], [g], [(w, F32)], [w], 256)


def _loss_head(h, target, g, name):
    w = h.shape[1]

    def fn(hv, tv, gv):
        rstd = lax.rsqrt(jnp.mean(hv * hv, axis=-1, keepdims=True) + EPS)
        xhat = hv * rstd
        diff = xhat * gv - tv
        dy = diff * (1.0 / w)
        dxhat = dy * gv
        dh = rstd * (dxhat - xhat * jnp.mean(dxhat * xhat, axis=-1, keepdims=True))
        return (dh, jnp.sum(0.5 * dy * diff, axis=0, keepdims=True),
                jnp.sum(dy * xhat, axis=0, keepdims=True))
    return _rowwise(name, fn, [h, target], [g], [(w, F32)], [w, w], 256)


def _split3(x):
    hi = x.astype(BF16)
    r1 = x - hi.astype(F32)
    mid = r1.astype(BF16)
    lo = (r1 - mid.astype(F32)).astype(BF16)
    return hi, mid, lo


def _tri_dot(tri, x):
    hi, mid, lo = _split3(x)
    d = lambda p: jnp.dot(tri, p, preferred_element_type=F32)
    return d(hi) + d(mid) + d(lo)


def _log_sigmoid(x):
    return jnp.minimum(x, 0.0) - jnp.log(1.0 + jnp.exp(-jnp.abs(x)))


def _gla_dims(proj_w, kw, vw, dk, dv):
    assert kw % dk == 0 and (2 * kw) % dv == 0 and (2 * kw + vw) % dv == 0 and (2 * kw + 2 * vw) % LANES == 0
    return dict(q0=0, k0=kw // dk, v0=2 * kw // dv, r0=(2 * kw + vw) // dv, g0=(2 * kw + 2 * vw) // LANES)


def _gla_gates(gl, wgu, bias):
    pre = jnp.dot(gl.astype(BF16), wgu, preferred_element_type=F32) + bias
    la = _log_sigmoid(pre) * (1.0 / GLA_GATE_TEMP)
    r_i = lax.broadcasted_iota(jnp.int32, (CHUNK, CHUNK), 0)
    c_i = lax.broadcasted_iota(jnp.int32, (CHUNK, CHUNK), 1)
    cum = _tri_dot((c_i <= r_i).astype(BF16), la)
    total = cum[CHUNK - 1:CHUNK, :]
    return pre, cum, total


def _gla_scan_fwd(proj, wgu_pad, b_gate, o_norm, heads, kw, vw, tb, name):
    seq, pw = proj.shape
    dk, dv = kw // heads, vw // heads
    cb = tb // CHUNK
    nt = seq // tb
    o = _gla_dims(pw, kw, vw, dk, dv)
    scale = dk ** -0.5

    def body(q_ref, k_ref, v_ref, r_ref, gl_ref, wgu_ref, b_ref, on_ref, out_ref, st_ref, s_scr):
        @pl.when(pl.program_id(1) == 0)
        def _():
            s_scr[...] = jnp.zeros_like(s_scr)

        wgu = wgu_ref[...].astype(BF16)
        bias = b_ref[...]
        onorm = on_ref[...]
        st = s_scr[...]
        for ci in range(cb):
            rows = pl.ds(ci * CHUNK, CHUNK)
            _, cum, total = _gla_gates(gl_ref[rows, :], wgu, bias)
            kdec = k_ref[rows, :] * jnp.exp(total - cum)
            st = st * jnp.exp(total) + lax.dot_general(
                v_ref[rows, :].astype(BF16), kdec.astype(BF16), (((0,), (0,)), ((), ())),
                preferred_element_type=F32)
            st_ref[ci] = st
            qs = (q_ref[rows, :] * scale).astype(BF16)
            ov = lax.dot_general(qs, st.astype(BF16), (((1,), (1,)), ((), ())), preferred_element_type=F32)
            rstd = lax.rsqrt(jnp.mean(ov * ov, axis=-1, keepdims=True) + EPS)
            rv = r_ref[rows, :]
            out_ref[rows, :] = (ov * rstd * onorm * (rv * jax.nn.sigmoid(rv))).astype(out_ref.dtype)
        s_scr[...] = st

    in_specs = [
        pl.BlockSpec((tb, dk), lambda h, t: (t, o['q0'] + h)),
        pl.BlockSpec((tb, dk), lambda h, t: (t, o['k0'] + h)),
        pl.BlockSpec((tb, dv), lambda h, t: (t, o['v0'] + h)),
        pl.BlockSpec((tb, dv), lambda h, t: (t, o['r0'] + h)),
        pl.BlockSpec((tb, LANES), lambda h, t: (t, o['g0'])),
        pl.BlockSpec((LANES, dk), lambda h, t: (0, h)),
        pl.BlockSpec((1, dk), lambda h, t: (0, h)),
        pl.BlockSpec((1, dv), lambda h, t: (0, 0)),
    ]
    return _pcall(
        body, name=name, grid=(heads, nt),
        out_shape=[jax.ShapeDtypeStruct((seq, vw), BF16),
                   jax.ShapeDtypeStruct((heads, seq // CHUNK, dv, dk), F32)],
        in_specs=in_specs,
        out_specs=[pl.BlockSpec((tb, dv), lambda h, t: (t, h)),
                   pl.BlockSpec((None, cb, dv, dk), lambda h, t: (h, t, 0, 0))],
        scratch_shapes=[pltpu.VMEM((dv, dk), F32)],
        compiler_params=_params('parallel', 'arbitrary'),
    )(proj, proj, proj, proj, proj, wgu_pad, b_gate, o_norm)


def _gla_scan_bwd(proj, wgu_pad, b_gate, o_norm, states, dgated, heads, kw, vw, tb, name):
    seq, pw = proj.shape
    dk, dv = kw // heads, vw // heads
    cb = tb // CHUNK
    nt = seq // tb
    o = _gla_dims(pw, kw, vw, dk, dv)
    scale = dk ** -0.5

    def body(q_ref, k_ref, v_ref, r_ref, gl_ref, wgu_ref, b_ref, on_ref, st_ref, stp_ref, dg_ref,
             dq_ref, dk_ref, dv_ref, dr_ref, dpre_ref, db_ref, don_ref, ds_scr):
        hh = pl.program_id(0)
        t = pl.program_id(1)

        @pl.when(t == 0)
        def _():
            ds_scr[...] = jnp.zeros_like(ds_scr)
            db_ref[...] = jnp.zeros_like(db_ref)

        @pl.when(jnp.logical_and(hh == 0, t == 0))
        def _():
            don_ref[...] = jnp.zeros_like(don_ref)

        wgu = wgu_ref[...].astype(BF16)
        bias = b_ref[...]
        onorm = on_ref[...]
        has_prev = (t < nt - 1).astype(F32)
        r_i = lax.broadcasted_iota(jnp.int32, (CHUNK, CHUNK), 0)
        c_i = lax.broadcasted_iota(jnp.int32, (CHUNK, CHUNK), 1)
        strict = (c_i < r_i).astype(BF16)
        carry = ds_scr[...]
        db_acc = jnp.zeros((1, dk), F32)
        don_acc = jnp.zeros((1, dv), F32)
        for ci in reversed(range(cb)):
            rows = pl.ds(ci * CHUNK, CHUNK)
            pre, cum, total = _gla_gates(gl_ref[rows, :], wgu, bias)
            edec = jnp.exp(total - cum)
            decay = jnp.exp(total)
            kdec = k_ref[rows, :] * edec
            st = st_ref[ci]
            st_prev = st_ref[ci - 1] if ci > 0 else stp_ref[0] * has_prev
            stb = st.astype(BF16)
            qs = (q_ref[rows, :] * scale).astype(BF16)
            vb = v_ref[rows, :].astype(BF16)
            ov = lax.dot_general(qs, stb, (((1,), (1,)), ((), ())), preferred_element_type=F32)
            rstd = lax.rsqrt(jnp.mean(ov * ov, axis=-1, keepdims=True) + EPS)
            ohat = ov * rstd
            rv = r_ref[rows, :]
            sr = jax.nn.sigmoid(rv)
            dgv = dg_ref[rows, :]
            dy = dgv * (rv * sr)
            dr_ref[rows, :] = (dgv * (ohat * onorm) * (sr * (1.0 + rv * (1.0 - sr)))).astype(dr_ref.dtype)
            don_acc = don_acc + jnp.sum(dy * ohat, axis=0, keepdims=True)
            dohat = dy * onorm
            do = (rstd * (dohat - ohat * jnp.mean(dohat * ohat, axis=-1, keepdims=True))).astype(BF16)
            dq_ref[rows, :] = (jnp.dot(do, stb, preferred_element_type=F32) * scale).astype(dq_ref.dtype)
            dst = carry + lax.dot_general(do, qs, (((0,), (0,)), ((), ())), preferred_element_type=F32)
            dstb = dst.astype(BF16)
            dkdec = jnp.dot(vb, dstb, preferred_element_type=F32)
            dv_ref[rows, :] = lax.dot_general(kdec.astype(BF16), dstb, (((1,), (1,)), ((), ())),
                                              preferred_element_type=F32).astype(dv_ref.dtype)
            ddecay = jnp.sum(dst * st_prev, axis=0, keepdims=True)
            dk_ref[rows, :] = (dkdec * edec).astype(dk_ref.dtype)
            da = ddecay * decay + _tri_dot(strict, dkdec * kdec)
            dpre = da * (1.0 / GLA_GATE_TEMP) * (1.0 - jax.nn.sigmoid(pre))
            dpre_ref[rows, :] = dpre.astype(dpre_ref.dtype)
            db_acc = db_acc + jnp.sum(dpre, axis=0, keepdims=True)
            carry = dst * decay
        ds_scr[...] = carry
        db_ref[...] += db_acc
        don_ref[...] += don_acc

    rt = lambda t: nt - 1 - t
    in_specs = [
        pl.BlockSpec((tb, dk), lambda h, t: (rt(t), o['q0'] + h)),
        pl.BlockSpec((tb, dk), lambda h, t: (rt(t), o['k0'] + h)),
        pl.BlockSpec((tb, dv), lambda h, t: (rt(t), o['v0'] + h)),
        pl.BlockSpec((tb, dv), lambda h, t: (rt(t), o['r0'] + h)),
        pl.BlockSpec((tb, LANES), lambda h, t: (rt(t), o['g0'])),
        pl.BlockSpec((LANES, dk), lambda h, t: (0, h)),
        pl.BlockSpec((1, dk), lambda h, t: (0, h)),
        pl.BlockSpec((1, dv), lambda h, t: (0, 0)),
        pl.BlockSpec((None, cb, dv, dk), lambda h, t: (h, rt(t), 0, 0)),
        pl.BlockSpec((None, 1, dv, dk), lambda h, t: (h, jnp.maximum(rt(t) * cb - 1, 0), 0, 0)),
        pl.BlockSpec((tb, dv), lambda h, t: (rt(t), h)),
    ]
    out_shape = [jax.ShapeDtypeStruct((seq, kw), BF16), jax.ShapeDtypeStruct((seq, kw), BF16),
                 jax.ShapeDtypeStruct((seq, vw), BF16), jax.ShapeDtypeStruct((seq, vw), BF16),
                 jax.ShapeDtypeStruct((seq, kw), BF16),
                 jax.ShapeDtypeStruct((1, kw), F32), jax.ShapeDtypeStruct((1, dv), F32)]
    out_specs = [pl.BlockSpec((tb, dk), lambda h, t: (rt(t), h)),
                 pl.BlockSpec((tb, dk), lambda h, t: (rt(t), h)),
                 pl.BlockSpec((tb, dv), lambda h, t: (rt(t), h)),
                 pl.BlockSpec((tb, dv), lambda h, t: (rt(t), h)),
                 pl.BlockSpec((tb, dk), lambda h, t: (rt(t), h)),
                 pl.BlockSpec((1, dk), lambda h, t: (0, h)),
                 pl.BlockSpec((1, dv), lambda h, t: (0, 0))]
    return _pcall(
        body, name=name, grid=(heads, nt),
        out_shape=out_shape, in_specs=in_specs, out_specs=out_specs,
        scratch_shapes=[pltpu.VMEM((dv, dk), F32)],
        compiler_params=_params('arbitrary', 'arbitrary'),
    )(proj, proj, proj, proj, proj, wgu_pad, b_gate, o_norm, states, states, dgated)


def _cmul(ar, ai, br, bi):
    return ar * br - ai * bi, ar * bi + ai * br


def _gelu(y):
    c = math.sqrt(2.0 / math.pi)
    return 0.5 * y * (1.0 + jnp.tanh(c * (y + 0.044715 * y * y * y)))


def _gelu_grad(y):
    c = math.sqrt(2.0 / math.pi)
    th = jnp.tanh(c * (y + 0.044715 * y * y * y))
    return 0.5 * (1.0 + th) + 0.5 * y * (1.0 - th * th) * (c * (1.0 + 3.0 * 0.044715 * y * y))


def _power_pow2(ar, ai, n):
    assert n & (n - 1) == 0
    for _ in range(n.bit_length() - 1):
        ar, ai = _cmul(ar, ai, ar, ai)
    return ar, ai


def _block_mask(rows, cols, rb, cb):
    r_i = lax.broadcasted_iota(jnp.int32, (rows, cols), 0)
    c_i = lax.broadcasted_iota(jnp.int32, (rows, cols), 1)
    return r_i // rb == c_i // cb


def _expand_blocks(m, gb):
    rows, b = m.shape
    r_i = lax.broadcasted_iota(jnp.int32, (b, gb * b), 0)
    c_i = lax.broadcasted_iota(jnp.int32, (b, gb * b), 1)
    repeat = (c_i % b == r_i).astype(BF16)
    full = jnp.dot(m.astype(BF16), repeat, preferred_element_type=F32)
    return jnp.where(_block_mask(rows, gb * b, rows // gb, b), full, 0.0).astype(BF16)


def _diagonal_blocks(m, gb):
    rows, cols = m.shape
    b = cols // gb
    masked = jnp.where(_block_mask(rows, cols, rows // gb, b), m, 0.0)
    r_i = lax.broadcasted_iota(jnp.int32, (cols, b), 0)
    c_i = lax.broadcasted_iota(jnp.int32, (cols, b), 1)
    pick = (r_i % b == c_i).astype(BF16)
    return sum(jnp.dot(p, pick, preferred_element_type=F32) for p in _split3(masked))


def _s5_fwd(u, bre, bim, cre, cim, are, aim, dskip, name):
    seq, width = u.shape
    nb, ub, n_state = bre.shape
    gb = S5_GROUPS_PER_BLOCK
    sb = gb * n_state
    ls = seq // S5_SEGMENTS
    seg = S5_SEGMENTS

    def body(u_ref, bre_ref, bim_ref, cre_ref, cim_ref, are_ref, aim_ref, d_ref, y_ref, z_ref, xr_ref, xi_ref):
        uv = u_ref[...]
        ub16 = uv.astype(BF16)
        xr_ref[...] = jnp.dot(ub16, _expand_blocks(bre_ref[...], gb), preferred_element_type=F32)
        xi_ref[...] = jnp.dot(ub16, _expand_blocks(bim_ref[...], gb), preferred_element_type=F32)
        ar = jnp.broadcast_to(are_ref[...], (seg, sb))
        ai = jnp.broadcast_to(aim_ref[...], (seg, sb))

        def step(i, c):
            rows = pl.ds(pl.multiple_of(i * seg, seg), seg)
            pr, pi = _cmul(ar, ai, c[0], c[1])
            nr = pr + xr_ref[rows, :]
            ni = pi + xi_ref[rows, :]
            xr_ref[rows, :] = nr
            xi_ref[rows, :] = ni
            return nr, ni

        zero = jnp.zeros((seg, sb), F32)
        er, ei = lax.fori_loop(0, ls, step, (zero, zero), unroll=8)
        pr, pi = _power_pow2(ar, ai, ls)
        row = lax.broadcasted_iota(jnp.int32, (seg, sb), 0)
        sr, si = zero, zero
        for _ in range(seg - 1):
            tr, ti = _cmul(pr, pi, sr, si)
            sr = jnp.where(row == 0, 0.0, pltpu.roll(tr + er, 1, 0))
            si = jnp.where(row == 0, 0.0, pltpu.roll(ti + ei, 1, 0))

        def fix(i, c):
            rows = pl.ds(pl.multiple_of(i * seg, seg), seg)
            fr, fi = _cmul(c[0], c[1], sr, si)
            xr_ref[rows, :] += fr
            xi_ref[rows, :] += fi
            return _cmul(c[0], c[1], ar, ai)

        lax.fori_loop(0, ls, fix, (ar, ai), unroll=8)
        y = (jnp.dot(xr_ref[...].astype(BF16), _expand_blocks(cre_ref[...], gb), preferred_element_type=F32)
             - jnp.dot(xi_ref[...].astype(BF16), _expand_blocks(cim_ref[...], gb), preferred_element_type=F32)
             + d_ref[...] * uv)
        y_ref[...] = y
        z_ref[...] = _gelu(y).astype(z_ref.dtype)

    mat = lambda r, c: pl.BlockSpec((None, r, c), lambda b: (b, 0, 0))
    return _pcall(
        body, name=name, grid=(nb,),
        out_shape=[jax.ShapeDtypeStruct((seq, width), F32), jax.ShapeDtypeStruct((seq, width), BF16),
                   jax.ShapeDtypeStruct((seq, nb * sb), F32), jax.ShapeDtypeStruct((seq, nb * sb), F32)],
        in_specs=[pl.BlockSpec((seq, ub), lambda b: (0, b)), mat(ub, n_state), mat(ub, n_state),
                  mat(sb, ub // gb), mat(sb, ub // gb),
                  mat(1, sb), mat(1, sb), pl.BlockSpec((1, ub), lambda b: (0, b))],
        out_specs=[pl.BlockSpec((seq, ub), lambda b: (0, b)), pl.BlockSpec((seq, ub), lambda b: (0, b)),
                   pl.BlockSpec((seq, sb), lambda b: (0, b)), pl.BlockSpec((seq, sb), lambda b: (0, b))],
        compiler_params=_params('parallel'),
    )(u, bre, bim, cre, cim, are, aim, dskip)


def _s5_bwd(dz, y, u, xr, xi, bre, bim, cre, cim, are, aim, dskip, name):
    seq, width = u.shape
    nb, ub, n_state = bre.shape
    gb = S5_GROUPS_PER_BLOCK
    sb = gb * n_state
    ls = seq // S5_SEGMENTS
    seg = S5_SEGMENTS

    def body(dz_ref, y_ref, u_ref, xr_ref, xi_ref, bre_ref, bim_ref, cre_ref, cim_ref, are_ref, aim_ref, d_ref,
             du_ref, dcr_ref, dci_ref, dbr_ref, dbi_ref, dar_ref, dai_ref, dd_ref, lr_ref, li_ref):
        uv = u_ref[...]
        dy = dz_ref[...] * _gelu_grad(y_ref[...])
        dd_ref[...] = jnp.sum(dy * uv, axis=0, keepdims=True)
        dyb = dy.astype(BF16)
        nt = (((1,), (1,)), ((), ()))
        tn = (((0,), (0,)), ((), ()))
        lr_ref[...] = lax.dot_general(dyb, _expand_blocks(cre_ref[...], gb), nt, preferred_element_type=F32)
        li_ref[...] = -lax.dot_general(dyb, _expand_blocks(cim_ref[...], gb), nt, preferred_element_type=F32)
        dcr_ref[...] = _diagonal_blocks(
            lax.dot_general(dyb, xr_ref[...].astype(BF16), tn, preferred_element_type=F32), gb)
        dci_ref[...] = -_diagonal_blocks(
            lax.dot_general(dyb, xi_ref[...].astype(BF16), tn, preferred_element_type=F32), gb)
        ar = jnp.broadcast_to(are_ref[...], (seg, sb))
        ai = jnp.broadcast_to(aim_ref[...], (seg, sb))
        nai = -ai

        def step(ii, c):
            rows = pl.ds(pl.multiple_of((ls - 1 - ii) * seg, seg), seg)
            pr, pi = _cmul(ar, nai, c[0], c[1])
            nr = pr + lr_ref[rows, :]
            ni = pi + li_ref[rows, :]
            lr_ref[rows, :] = nr
            li_ref[rows, :] = ni
            return nr, ni

        zero = jnp.zeros((seg, sb), F32)
        er, ei = lax.fori_loop(0, ls, step, (zero, zero), unroll=8)
        pr, pi = _power_pow2(ar, nai, ls)
        row = lax.broadcasted_iota(jnp.int32, (seg, sb), 0)
        rr, ri = zero, zero
        for _ in range(seg - 1):
            tr, ti = _cmul(pr, pi, rr, ri)
            rr = jnp.where(row == seg - 1, 0.0, pltpu.roll(tr + er, seg - 1, 0))
            ri = jnp.where(row == seg - 1, 0.0, pltpu.roll(ti + ei, seg - 1, 0))

        def corrected(rows, qr, qi):
            fr, fi = _cmul(qr, qi, rr, ri)
            nr = lr_ref[rows, :] + fr
            ni = li_ref[rows, :] + fi
            lr_ref[rows, :] = nr
            li_ref[rows, :] = ni
            return nr, ni

        def grad_a(nr, ni, xpr, xpi, accr, acci):
            return accr + nr * xpr + ni * xpi, acci + ni * xpr - nr * xpi

        def fix(ii, c):
            qr, qi, accr, acci = c
            i = ls - 1 - ii
            rows = pl.ds(pl.multiple_of(i * seg, seg), seg)
            prev = pl.ds(pl.multiple_of((i - 1) * seg, seg), seg)
            nr, ni = corrected(rows, qr, qi)
            accr, acci = grad_a(nr, ni, xr_ref[prev, :], xi_ref[prev, :], accr, acci)
            qr, qi = _cmul(qr, qi, ar, nai)
            return qr, qi, accr, acci

        qr, qi, accr, acci = lax.fori_loop(0, ls - 1, fix, (ar, nai, zero, zero), unroll=8)
        nr, ni = corrected(pl.ds(0, seg), qr, qi)
        last = pl.ds((ls - 1) * seg, seg)
        xpr = jnp.where(row == 0, 0.0, pltpu.roll(xr_ref[last, :], 1, 0))
        xpi = jnp.where(row == 0, 0.0, pltpu.roll(xi_ref[last, :], 1, 0))
        accr, acci = grad_a(nr, ni, xpr, xpi, accr, acci)
        dar_ref[...] = jnp.sum(accr, axis=0, keepdims=True)
        dai_ref[...] = jnp.sum(acci, axis=0, keepdims=True)
        lrb = lr_ref[...].astype(BF16)
        lib = li_ref[...].astype(BF16)
        ub16 = uv.astype(BF16)
        dbr_ref[...] = _diagonal_blocks(lax.dot_general(ub16, lrb, tn, preferred_element_type=F32), gb)
        dbi_ref[...] = _diagonal_blocks(lax.dot_general(ub16, lib, tn, preferred_element_type=F32), gb)
        du_ref[...] = (d_ref[...] * dy
                       + lax.dot_general(lrb, _expand_blocks(bre_ref[...], gb), nt, preferred_element_type=F32)
                       + lax.dot_general(lib, _expand_blocks(bim_ref[...], gb), nt, preferred_element_type=F32))

    mat = lambda r, c: pl.BlockSpec((None, r, c), lambda b: (b, 0, 0))
    col = lambda w: pl.BlockSpec((seq, w), lambda b: (0, b))
    return _pcall(
        body, name=name, grid=(nb,),
        out_shape=[jax.ShapeDtypeStruct((seq, width), F32)]
        + [jax.ShapeDtypeStruct((nb, ub, n_state), F32)] * 4
        + [jax.ShapeDtypeStruct((nb, 1, sb), F32)] * 2
        + [jax.ShapeDtypeStruct((1, width), F32)],
        in_specs=[col(ub), col(ub), col(ub), col(sb), col(sb), mat(ub, n_state), mat(ub, n_state),
                  mat(sb, ub // gb), mat(sb, ub // gb),
                  mat(1, sb), mat(1, sb), pl.BlockSpec((1, ub), lambda b: (0, b))],
        out_specs=[col(ub), mat(ub, n_state), mat(ub, n_state), mat(ub, n_state), mat(ub, n_state),
                   mat(1, sb), mat(1, sb), pl.BlockSpec((1, ub), lambda b: (0, b))],
        scratch_shapes=[pltpu.VMEM((seq, sb), F32), pltpu.VMEM((seq, sb), F32)],
        compiler_params=_params('parallel'),
    )(dz, y, u, xr, xi, bre, bim, cre, cim, are, aim, dskip)


def _s5_discretise(lam_re, lam_im, log_dt, b_re, b_im):
    lr = jnp.minimum(lam_re, S5_EIG_CLIP)
    li = lam_im
    dt = jnp.exp(log_dt)[:, None]
    mag = jnp.exp(lr * dt)
    ang = li * dt
    ab_re = mag * jnp.cos(ang)
    ab_im = mag * jnp.sin(ang)
    den = lr * lr + li * li
    nr = ab_re - 1.0
    f_re = (nr * lr + ab_im * li) / den
    f_im = (ab_im * lr - nr * li) / den
    bb_re = f_re[..., None] * b_re - f_im[..., None] * b_im
    bb_im = f_re[..., None] * b_im + f_im[..., None] * b_re
    return ab_re, ab_im, bb_re, bb_im


def _glu_fwd(o, h, name):
    half = o.shape[1] // 2

    def fn(ov, hv):
        return (hv + ov[:, :half] * jax.nn.sigmoid(ov[:, half:]),)
    return _rowwise(name, fn, [o, h], [], [(half, F32)], [], 256)[0]


def _glu_bwd(o, dout, name):
    half = o.shape[1] // 2

    def fn(ov, dv):
        val, gate = ov[:, :half], ov[:, half:]
        sg = jax.nn.sigmoid(gate)
        return (jnp.concatenate([dv * sg, dv * val * sg * (1.0 - sg)], axis=1),)
    return _rowwise(name, fn, [o, dout], [], [(2 * half, BF16)], [], 256)[0]


def _adam_math(w, g, m, v):
    m = ADAM_B1 * m + (1.0 - ADAM_B1) * g
    v = ADAM_B2 * v + (1.0 - ADAM_B2) * (g * g)
    m_hat = m / (1.0 - ADAM_B1 ** ADAM_STEP)
    v_hat = v / (1.0 - ADAM_B2 ** ADAM_STEP)
    delta = -ADAM_LR * (m_hat / (jnp.sqrt(v_hat) + ADAM_EPS) + ADAM_WD * w)
    return delta, m, v


def _adamw(w, m, v, grads, name, after=()):
    nl, rows, cols = w.shape
    tm = _tile(rows, max(8, (1 << 18) // cols // 8 * 8), unit=8)
    nbk = rows // tm

    def body(*refs):
        w_ref, m_ref, v_ref = refs[:3]
        g_refs = refs[3:3 + nl]
        go_ref, d_ref, mo_ref, vo_ref = refs[3 + nl + len(after):]
        layer = pl.program_id(0)
        g = g_refs[0][...]
        for l in range(1, nl):
            g = jnp.where(layer == l, g_refs[l][...], g)
        delta, mn, vn = _adam_math(w_ref[...], g, m_ref[...], v_ref[...])
        go_ref[...] = g
        d_ref[...] = delta
        mo_ref[...] = mn
        vo_ref[...] = vn

    stacked = pl.BlockSpec((None, tm, cols), lambda l, i: (l, i, 0))

    def g_spec(layer):
        return pl.BlockSpec((tm, cols), lambda l, i: (jnp.where(l == layer, i, jnp.where(l < layer, 0, nbk - 1)), 0))

    return _pcall(
        body, name=name, grid=(nl, nbk),
        out_shape=[jax.ShapeDtypeStruct(w.shape, F32)] * 4,
        in_specs=[stacked] * 3 + [g_spec(l) for l in range(nl)] + [ANY] * len(after),
        out_specs=[stacked] * 4,
        compiler_params=_params('arbitrary', 'arbitrary'),
    )(w, m, v, *grads, *after)


def _adamw_t(w, m, v, g, name, after=()):
    cols, nl, rows = w.shape
    budget = max(1, (1 << 21) // (8 * rows * 4))
    tc = max(t for t in range(1, min(cols, budget) + 1) if cols % t == 0)

    def body(*refs):
        w_ref, m_ref, v_ref, g_ref = refs[:4]
        d_ref, mo_ref, vo_ref = refs[4 + len(after):]
        delta, mn, vn = _adam_math(w_ref[...], g_ref[...], m_ref[...], v_ref[...])
        d_ref[...] = delta
        mo_ref[...] = mn
        vo_ref[...] = vn

    blk = pl.BlockSpec((tc, nl, rows), lambda i: (i, 0, 0))
    return _pcall(
        body, name=name, grid=(cols // tc,),
        out_shape=[jax.ShapeDtypeStruct(w.shape, F32)] * 3,
        in_specs=[blk] * 4 + [ANY] * len(after), out_specs=[blk] * 3,
        compiler_params=_params('parallel'),
    )(w, m, v, g, *after)


def _pack(arrs, rows_mult=512):
    flat = jnp.concatenate([a.reshape(-1) for a in arrs])
    total = flat.shape[0]
    rows = -(-total // LANES)
    rows = -(-rows // rows_mult) * rows_mult
    flat = jnp.pad(flat, (0, rows * LANES - total))
    return flat.reshape(rows, LANES)


def _unpack(packed, shapes):
    flat = packed.reshape(-1)
    out, off = [], 0
    for s in shapes:
        size = math.prod(s)
        out.append(flat[off:off + size].reshape(s))
        off += size
    return out


def _permute(a):
    seq, w = a.shape
    return a.reshape(S5_SEGMENTS, seq // S5_SEGMENTS, w).transpose(1, 0, 2).reshape(seq, w)


def _unpermute(a):
    seq, w = a.shape
    return a.reshape(seq // S5_SEGMENTS, S5_SEGMENTS, w).transpose(1, 0, 2).reshape(seq, w)


def kernel(x, gla_norm, gla_w_in, gla_w_gate_up, gla_b_gate, gla_o_norm, gla_w_out, s5_norm, s5_w_in, s5_lam_re, s5_lam_im, s5_log_dt, s5_b_re, s5_b_im, s5_c_re, s5_c_im, s5_d, s5_w_out, mlp_norm, mlp_w_up, mlp_w_down, final_norm, loss_target, m_gla_norm, m_gla_w_in, m_gla_w_gate_up, m_gla_b_gate, m_gla_o_norm, m_gla_w_out, m_s5_norm, m_s5_w_in, m_s5_lam_re, m_s5_lam_im, m_s5_log_dt, m_s5_b_re, m_s5_b_im, m_s5_c_re, m_s5_c_im, m_s5_d, m_s5_w_out, m_mlp_norm, m_mlp_w_up, m_mlp_w_down, m_final_norm, v_gla_norm, v_gla_w_in, v_gla_w_gate_up, v_gla_b_gate, v_gla_o_norm, v_gla_w_out, v_s5_norm, v_s5_w_in, v_s5_lam_re, v_s5_lam_im, v_s5_log_dt, v_s5_b_re, v_s5_b_im, v_s5_c_re, v_s5_c_im, v_s5_d, v_s5_w_out, v_mlp_norm, v_mlp_w_up, v_mlp_w_down, v_final_norm):
    weights = dict(gla_norm=gla_norm, gla_w_in=gla_w_in, gla_w_gate_up=gla_w_gate_up, gla_b_gate=gla_b_gate, gla_o_norm=gla_o_norm, gla_w_out=gla_w_out, s5_norm=s5_norm, s5_w_in=s5_w_in, s5_lam_re=s5_lam_re, s5_lam_im=s5_lam_im, s5_log_dt=s5_log_dt, s5_b_re=s5_b_re, s5_b_im=s5_b_im, s5_c_re=s5_c_re, s5_c_im=s5_c_im, s5_d=s5_d, s5_w_out=s5_w_out, mlp_norm=mlp_norm, mlp_w_up=mlp_w_up, mlp_w_down=mlp_w_down, final_norm=final_norm)
    mom1 = dict(gla_norm=m_gla_norm, gla_w_in=m_gla_w_in, gla_w_gate_up=m_gla_w_gate_up, gla_b_gate=m_gla_b_gate, gla_o_norm=m_gla_o_norm, gla_w_out=m_gla_w_out, s5_norm=m_s5_norm, s5_w_in=m_s5_w_in, s5_lam_re=m_s5_lam_re, s5_lam_im=m_s5_lam_im, s5_log_dt=m_s5_log_dt, s5_b_re=m_s5_b_re, s5_b_im=m_s5_b_im, s5_c_re=m_s5_c_re, s5_c_im=m_s5_c_im, s5_d=m_s5_d, s5_w_out=m_s5_w_out, mlp_norm=m_mlp_norm, mlp_w_up=m_mlp_w_up, mlp_w_down=m_mlp_w_down, final_norm=m_final_norm)
    mom2 = dict(gla_norm=v_gla_norm, gla_w_in=v_gla_w_in, gla_w_gate_up=v_gla_w_gate_up, gla_b_gate=v_gla_b_gate, gla_o_norm=v_gla_o_norm, gla_w_out=v_gla_w_out, s5_norm=v_s5_norm, s5_w_in=v_s5_w_in, s5_lam_re=v_s5_lam_re, s5_lam_im=v_s5_lam_im, s5_log_dt=v_s5_log_dt, s5_b_re=v_s5_b_re, s5_b_im=v_s5_b_im, s5_c_re=v_s5_c_re, s5_c_im=v_s5_c_im, s5_d=v_s5_d, s5_w_out=v_s5_w_out, mlp_norm=v_mlp_norm, mlp_w_up=v_mlp_w_up, mlp_w_down=v_mlp_w_down, final_norm=v_final_norm)
    names = list(weights)
    big = ['gla_w_in', 'gla_w_out', 's5_w_in', 's5_w_out', 'mlp_w_up', 'mlp_w_down']
    small = [n for n in names if n not in big]

    chip = 2 * lax.axis_index('x') + lax.axis_index('y')
    h0 = x[0]
    target = loss_target[0]
    seq, dm = h0.shape
    depth = mlp_norm.shape[0]
    n_gla = gla_norm.shape[0]
    n_s5 = s5_lam_re.shape[0]
    rank = gla_w_gate_up.shape[1]
    kw = gla_b_gate.shape[1]
    dv = gla_o_norm.shape[1]
    in_w = 4 * gla_w_in.shape[2]
    vw = (in_w - rank - 2 * kw) // 2
    heads = vw // dv
    dk = kw // heads
    pw = -(-in_w // LANES) * LANES
    s5w = s5_w_in.shape[2]
    n_grp, n_state, grp = s5_b_re.shape[1:]
    hid = 4 * mlp_w_up.shape[2]
    tb = min(seq, 8 * CHUNK)
    tm = _tile(seq, 1024)

    ic = lax.axis_index('c')
    rh = lambda w: w.shape[1] // 2
    wb16 = {n: weights[n].astype(BF16) for n in big}
    gathered = {n: [None] * weights[n].shape[0] for n in big}
    g_w_in, g_gla_out, g_s5_in, g_s5_out, g_up, g_down = (gathered[n] for n in big)
    in_flight = {}

    to_start = []
    for i in range(depth):
        mix = ['gla_w_in', 'gla_w_out'] if i % 2 == 0 else ['s5_w_in', 's5_w_out']
        to_start += [(m, i // 2) for m in mix] + [('mlp_w_up', i), ('mlp_w_down', i)]

    def next_gathers(room):
        keys, jobs = [], []
        while to_start and room > 0:
            n, l = to_start.pop(0)
            rows, cols = weights[n].shape[1:]
            keys.append((n, l))
            if cols % LANES:
                halves = wb16[n].transpose(2, 0, 1)[:, l, :].reshape(cols, 2, rows // 2).transpose(1, 0, 2)
                jobs.append(('chips_gather', halves, (2, 4, cols, rows // 2)))
            else:
                jobs.append(('chips_gather', wb16[n][l].reshape(2, rows // 2, cols), (2, 4, rows // 2, cols)))
            room -= 1
        return keys, jobs

    swapping = {}

    def gather_arrived(n, l, after):
        halves, land = _wait(in_flight.pop((n, l)), f'ag_{n}_{l}_wait', after)
        own = lax.dynamic_index_in_dim(halves, ic, 0, keepdims=True)
        land = lax.dynamic_update_slice(land, own[:, None], (ic, chip, 0, 0))
        keys, jobs = next_gathers(GATHERS_IN_FLIGHT - len(in_flight))
        handles = _start_many([('pair_inplace', None, land)] + jobs, f'ag_{n}_{l}_pair_start', [])
        in_flight.update(zip(keys, handles[1:]))
        swapping[n, l] = handles[0]

    def gather_ready(n, l):
        gathered[n][l] = _wait(swapping.pop((n, l)), f'ag_{n}_{l}_pair_wait', [])[0]
        return [gathered[n][l]]

    def finish_gather(n, l, after):
        gather_arrived(n, l, after)
        return gather_ready(n, l)

    sharded_small = [gla_w_gate_up, s5_norm, s5_d]
    gathered_small = _exchange(_pack(sharded_small), 'xy', 'ag_small')
    keys, jobs = next_gathers(GATHERS_IN_FLIGHT)
    in_flight.update(zip(keys, _start_many(jobs, 'ag_first_start', [gathered_small])))
    parts = [_unpack(gathered_small[k], [a.shape for a in sharded_small]) for k in range(4)]
    wgu_full = jnp.concatenate([p[0] for p in parts], axis=2)
    s5_norm_full = jnp.concatenate([p[1] for p in parts], axis=1)
    s5_d_full = jnp.concatenate([p[2] for p in parts], axis=1)

    def gla_w_in_padded(j):
        wj = g_w_in[j].transpose(1, 2, 0, 3).reshape(in_w, dm)
        return jnp.pad(wj, ((0, pw - in_w), (0, 0)))

    grads = {n: [None] * weights[n].shape[0] for n in names if n != 'final_norm'}

    saved = []
    h = h0
    for i in range(depth):
        j = i // 2
        rec = {}
        if i % 2 == 0:
            rec['h_in'] = h
            behind = finish_gather('gla_w_in', j, [h])
            hn = _norm_fwd(h, gla_norm[j:j + 1], 'gla_norm_fwd', behind)
            w_in_pad = gla_w_in_padded(j)
            proj = _mm('gla_proj', 'nt', _plain(hn), _plain(w_in_pad), (seq, pw, dm),
                       [((seq, pw), F32, _plain_shape(None))], (tm, _tile(pw, 1024), dm))[0]
            wgu_pad = jnp.pad(wgu_full[j], ((0, LANES - rank), (0, 0)))
            gather_arrived('gla_w_out', j, [proj])
            gated, states = _gla_scan_fwd(proj, wgu_pad, gla_b_gate[j:j + 1], gla_o_norm[j:j + 1],
                                          heads, kw, vw, tb, 'gla_scan_fwd')
            behind = gather_ready('gla_w_out', j)
            h = _mm('gla_out', 'nn', _plain(gated), _w_rows(g_gla_out, j), (seq, dm, vw),
                    [((seq, dm), F32, _plain_shape(None))],
                    (tm, _tile(dm, 1024), gla_w_out.shape[1]),
                    epilogue=lambda acc, hv: (acc + hv,), extras=[_plain(h)], after=behind)[0]
            rec.update(hn=hn, w_in_pad=w_in_pad, proj=proj, wgu_pad=wgu_pad, gated=gated, states=states)
        else:
            hp = _permute(h)
            rec['h_in'] = hp
            hn = _norm_fwd(hp, s5_norm_full[j:j + 1], 's5_norm_fwd')
            behind = finish_gather('s5_w_in', j, [hn])
            u = _mm('s5_in', 'nn', _plain(hn), _w_rows(g_s5_in, j), (seq, s5w, dm),
                    [((seq, s5w), F32, _plain_shape(None))],
                    (tm, _tile(s5w, 1024), s5_w_in.shape[1]), after=behind)[0]
            disc, disc_vjp = jax.vjp(_s5_discretise, s5_lam_re[j], s5_lam_im[j], s5_log_dt[j], s5_b_re[j], s5_b_im[j])
            ab_re, ab_im, bb_re, bb_im = disc
            nb = n_grp // S5_GROUPS_PER_BLOCK
            per_block = lambda m: m.transpose(0, 2, 1).reshape(nb, -1, m.shape[1])
            bre, bim = per_block(bb_re), per_block(bb_im)
            cre, cim = per_block(s5_c_re[j]), per_block(s5_c_im[j])
            are = ab_re.reshape(nb, 1, S5_GROUPS_PER_BLOCK * n_state)
            aim = ab_im.reshape(nb, 1, S5_GROUPS_PER_BLOCK * n_state)
            dskip = s5_d_full[j:j + 1]
            gather_arrived('s5_w_out', j, [u])
            y, z, xr, xi = _s5_fwd(u, bre, bim, cre, cim, are, aim, dskip, 's5_scan_fwd')
            behind = gather_ready('s5_w_out', j)
            o = _mm('s5_out', 'nn', _plain(z), _w_cols(g_s5_out, j), (seq, 2 * dm, s5w),
                    [((seq, 2 * dm), F32, _plain_shape(None))],
                    (tm, _tile(s5_w_out.shape[2], 1024), s5_w_out.shape[1]), after=behind)[0]
            h = _unpermute(_glu_fwd(o, hp, 's5_glu_fwd'))
            rec.update(hn=hn, u=u, y=y, z=z, xr=xr, xi=xi, o=o, mats=(bre, bim, cre, cim, are, aim, dskip),
                       disc_vjp=disc_vjp)
        rec['h_mid'] = h
        hn2 = _norm_fwd(h, mlp_norm[i:i + 1], 'mlp_norm_fwd')
        behind = finish_gather('mlp_w_up', i, [hn2])
        act, act2 = _mm('mlp_up', 'nn', _plain(hn2), _w_cols(g_up, i), (seq, hid, dm),
                        [((seq, hid), BF16, _plain_shape(None))] * 2,
                        (tm, _tile(mlp_w_up.shape[2], 1024), mlp_w_up.shape[1]),
                        epilogue=lambda acc: (jnp.maximum(acc, 0.0), jnp.square(jnp.maximum(acc, 0.0))),
                        after=behind)
        behind = finish_gather('mlp_w_down', i, [act2])
        h = _mm('mlp_down', 'nn', _plain(act2), _w_rows(g_down, i), (seq, dm, hid),
                [((seq, dm), F32, _plain_shape(None))],
                (tm, _tile(dm, 1024), mlp_w_down.shape[1]),
                epilogue=lambda acc, hv: (acc + hv,), extras=[_plain(h)], after=behind)[0]
        rec.update(hn2=hn2, act=act, act2=act2)
        saved.append(rec)

    dh, loss_cols, d_final = _loss_head(h, target, final_norm.reshape(1, dm), 'loss_head')
    loss = lax.psum(jnp.sum(loss_cols), ('x', 'y', 'c'))
    grads['final_norm'] = [d_final.reshape(dm)]

    big_grads = {n: [None] * weights[n].shape[0] for n in big}
    reducing = []

    def reduce_next(item, after):
        n, l, hd = item['n'], item['l'], item['hd']
        if item['stage'] == 'pair':
            dw, got = _wait(hd, f'rs_{n}_{l}_pair_wait', after)
            _, _, rows_h, cols = dw.shape
            pre = _sum_pair(dw.reshape(2, 4 * rows_h, cols), got.reshape(4 * rows_h, cols), ic, BF16,
                            f'rs_{n}_pairsum').reshape(4, rows_h, cols)
            item['stage'] = 'chips'
            return ('chips_a2a', pre, pre.shape)
        if item['stage'] == 'chips':
            pre, yb = _wait(hd, f'rs_{n}_{l}_wait', after)
            yb = lax.dynamic_update_index_in_dim(yb, lax.dynamic_index_in_dim(pre, chip, 0, keepdims=False), chip, 0)
            fin = _sum_slots(yb, F32, f'rs_{n}_chipsum')
            item['stage'] = 'back'
            return ('pair_bcast', fin, (2,) + fin.shape)
        fin, both = _wait(hd, f'rs_{n}_{l}_back_wait', after)
        both = lax.dynamic_update_index_in_dim(both, fin, ic, 0)
        big_grads[n][l] = both
        item.update(stage='done', hd=None)
        return None

    def reduce_many(items, name, after):
        jobs = [(item, reduce_next(item, after)) for item in items]
        jobs = [(item, job) for item, job in jobs if job is not None]
        if not jobs:
            return after
        for (item, _), hd in zip(jobs, _start_many([job for _, job in jobs], name, after)):
            item['hd'] = hd
        return [jobs[0][0]['hd']['token']]

    def reduce_scatter(dw, n, l):
        new = dict(n=n, l=l, stage='new', hd=None)
        older = [reducing[-k] for k in (1, 3) if len(reducing) >= k]
        jobs = [(item, reduce_next(item, [dw])) for item in older] + [(new, ('pair_swap', dw, dw.shape[1:]))]
        new['stage'] = 'pair'
        reducing.append(new)
        for (item, _), hd in zip(jobs, _start_many([job for _, job in jobs], f'rs_{n}_{l}_start', [])):
            item['hd'] = hd
        return [new['hd']['token']]

    for i in reversed(range(depth)):
        j = i // 2
        rec = saved[i]
        r_dn, c_dn = mlp_w_down.shape[1:]
        shape, spec = _dw_rows(r_dn, c_dn)
        dw = _mm('mlp_down_dw', 'tn', _plain(rec['act2']), _plain(dh), (hid, dm, seq),
                 [(shape, BF16, spec)], (_tile(r_dn // 2, 1024), _tile(c_dn, 1024), seq))[0]
        behind = reduce_scatter(dw, 'mlp_w_down', i)
        dpre = _mm('mlp_down_dx', 'nt', _plain(dh), _w_rows(g_down, i), (seq, hid, dm),
                   [((seq, hid), BF16, _plain_shape(None))],
                   (tm, _tile(rh(mlp_w_down), 1024), dm),
                   epilogue=lambda acc, av: (acc * (2.0 * av.astype(F32)),), extras=[_plain(rec['act'])],
                   after=behind)[0]
        r_up, c_up = mlp_w_up.shape[1:]
        shape, spec = _dw_cols(r_up, c_up)
        dw = _mm('mlp_up_dw', 'tn', _plain(rec['hn2']), _plain(dpre), (dm, hid, seq),
                 [(shape, BF16, spec)], (_tile(r_up // 2, 1024), _tile(c_up, 1024), seq))[0]
        behind = reduce_scatter(dw, 'mlp_w_up', i)
        dhn = _mm('mlp_up_dx', 'nt', _plain(dpre), _w_cols(g_up, i), (seq, dm, hid),
                  [((seq, dm), F32, _plain_shape(None))],
                  (tm, _tile(rh(mlp_w_up), 1024), _tile(mlp_w_up.shape[2], 2048)), after=behind)[0]
        dh, dg = _norm_bwd(rec['h_mid'], dhn, dh, mlp_norm[i:i + 1], 'mlp_norm_bwd')
        grads['mlp_norm'][i] = dg[0]

        if i % 2 == 0:
            r_o, c_o = gla_w_out.shape[1:]
            shape, spec = _dw_rows(r_o, c_o)
            dw = _mm('gla_out_dw', 'tn', _plain(rec['gated']), _plain(dh), (vw, dm, seq),
                     [(shape, BF16, spec)], (_tile(r_o // 2, 1024), _tile(c_o, 1024), seq))[0]
            behind = reduce_scatter(dw, 'gla_w_out', j)
            dgated = _mm('gla_out_dx', 'nt', _plain(dh), _w_rows(g_gla_out, j), (seq, vw, dm),
                         [((seq, vw), F32, _plain_shape(None))],
                         (tm, _tile(rh(gla_w_out), 1024), dm), after=behind)[0]
            dq, dkk, dvv, dr, dpre_g, db, don = _gla_scan_bwd(
                rec['proj'], rec['wgu_pad'], gla_b_gate[j:j + 1], gla_o_norm[j:j + 1], rec['states'], dgated,
                heads, kw, vw, tb, 'gla_scan_bwd')
            grads['gla_b_gate'][j] = db[0]
            grads['gla_o_norm'][j] = don[0]
            dgl = _mm('gla_gate_dx', 'nt', _plain(dpre_g), _plain(rec['wgu_pad']), (seq, LANES, kw),
                      [((seq, LANES), BF16, _plain_shape(None))], (tm, LANES, kw))[0]
            g_low = rec['proj'][:, pw - LANES:]
            dwgu = _mm('gla_gate_dw', 'tn', _plain(g_low), _plain(dpre_g), (LANES, kw, seq),
                       [((LANES, kw), F32, _plain_shape(None))], (LANES, kw, seq))[0]
            grads['gla_w_gate_up'][j] = dwgu[:rank]
            dproj = jnp.concatenate([dq, dkk, dvv, dr, dgl], axis=1)
            dw_pad = _mm('gla_proj_dw', 'tn', _plain(dproj), _plain(rec['hn']), (pw, dm, seq),
                         [((pw, dm), BF16, _plain_shape(None))], (_tile(pw, 1024), _tile(dm, 1024), seq))[0]
            shard_w = in_w // 4
            dw = dw_pad[:in_w].reshape(4, shard_w, 2, dm // 2).transpose(2, 0, 1, 3)
            behind = reduce_scatter(dw, 'gla_w_in', j)
            dhn = _mm('gla_proj_dx', 'nn', _plain(dproj), _plain(rec['w_in_pad']), (seq, dm, pw),
                      [((seq, dm), F32, _plain_shape(None))], (tm, _tile(dm, 1024), _tile(pw, 1024)),
                      after=behind)[0]
            dh, dg = _norm_bwd(rec['h_in'], dhn, dh, gla_norm[j:j + 1], 'gla_norm_bwd')
            grads['gla_norm'][j] = dg[0]
        else:
            dhp = _permute(dh)
            do = _glu_bwd(rec['o'], dhp, 's5_glu_bwd')
            r_o, c_o = s5_w_out.shape[1:]
            shape, spec = _dw_cols(r_o, c_o)
            dw = _mm('s5_out_dw', 'tn', _plain(rec['z']), _plain(do), (s5w, 2 * dm, seq),
                     [(shape, BF16, spec)], (_tile(r_o // 2, 1024), _tile(c_o, 1024), seq))[0]
            behind = reduce_scatter(dw, 's5_w_out', j)
            dz = _mm('s5_out_dx', 'nt', _plain(do), _w_cols(g_s5_out, j), (seq, s5w, 2 * dm),
                     [((seq, s5w), F32, _plain_shape(None))],
                     (tm, _tile(rh(s5_w_out), 1024), _tile(s5_w_out.shape[2], 1024)), after=behind)[0]
            bre, bim, cre, cim, are, aim, dskip = rec['mats']
            du, dcr, dci, dbr, dbi, dar, dai, dd = _s5_bwd(dz, rec['y'], rec['u'], rec['xr'], rec['xi'],
                                                           bre, bim, cre, cim, are, aim, dskip, 's5_scan_bwd')
            grads['s5_c_re'][j] = dcr.reshape(n_grp, grp, n_state)
            grads['s5_c_im'][j] = dci.reshape(n_grp, grp, n_state)
            dbb_re = dbr.reshape(n_grp, grp, n_state).transpose(0, 2, 1)
            dbb_im = dbi.reshape(n_grp, grp, n_state).transpose(0, 2, 1)
            d_lr, d_li, d_dt, d_bre, d_bim = rec['disc_vjp'](
                (dar.reshape(n_grp, n_state), dai.reshape(n_grp, n_state), dbb_re, dbb_im))
            grads['s5_lam_re'][j] = d_lr
            grads['s5_lam_im'][j] = d_li
            grads['s5_log_dt'][j] = d_dt
            grads['s5_b_re'][j] = d_bre
            grads['s5_b_im'][j] = d_bim
            grads['s5_d'][j] = dd[0]
            r_i, c_i = s5_w_in.shape[1:]
            shape, spec = _dw_rows(r_i, c_i)
            dw = _mm('s5_in_dw', 'tn', _plain(rec['hn']), _plain(du), (dm, s5w, seq),
                     [(shape, BF16, spec)], (_tile(r_i // 2, 1024), _tile(c_i, 1024), seq))[0]
            behind = reduce_scatter(dw, 's5_w_in', j)
            dhn = _mm('s5_in_dx', 'nt', _plain(du), _w_rows(g_s5_in, j), (seq, dm, s5w),
                      [((seq, dm), F32, _plain_shape(None))],
                      (tm, _tile(rh(s5_w_in), 1024), _tile(s5w, 1024)), after=behind)[0]
            dhp, dg = _norm_bwd(rec['h_in'], dhn, dhp, s5_norm_full[j:j + 1], 's5_norm_bwd')
            dh = _unpermute(dhp)
            grads['s5_norm'][j] = dg[0]
    grad_x = dh[None]
    behind = [dh]
    for k, (stage, items) in enumerate((('pair', reducing), ('chips', reducing[:-1]), ('back', reducing[:-1]))):
        behind = reduce_many([item for item in items if item['stage'] == stage], f'rs_tail_{k}_start', behind)

    local_small = [jnp.stack(grads[n]) if n != 'final_norm' else grads[n][0] for n in small]
    full_shapes = [a.shape for a in local_small]
    packed_small = _pack(local_small)
    ar_small = _start('chips_bcast', packed_small, (4,) + packed_small.shape, 'ar_small_start', [])
    out_g, out_d, out_m, out_v = {}, {}, {}, {}
    behind = [ar_small['token']]
    last_n = reducing[-1]['n']
    for n in [m for m in big if m != last_n] + [last_n]:
        if n == last_n:
            for k in range(2):
                behind = reduce_many([reducing[-1]], f'rs_last_{k}_start', behind)
        if weights[n].shape[2] % LANES:
            to_t, from_t = (lambda a: a.transpose(2, 0, 1)), (lambda a: a.transpose(1, 2, 0))
            cols, rows_h = big_grads[n][0].shape[1:]
            g_t = jnp.stack(big_grads[n]).transpose(2, 0, 1, 3).reshape(cols, len(big_grads[n]), 2 * rows_h)
            res = _adamw_t(to_t(weights[n]), to_t(mom1[n]), to_t(mom2[n]), g_t, 'adamw_' + n, behind)
            out_g[n], out_d[n], out_m[n], out_v[n] = (from_t(a) for a in (g_t,) + tuple(res))
            behind = [res[0]]
        else:
            per_layer = [g.reshape(2 * g.shape[1], g.shape[2]) for g in big_grads[n]]
            out_g[n], out_d[n], out_m[n], out_v[n] = _adamw(weights[n], mom1[n], mom2[n], per_layer,
                                                            'adamw_' + n, behind)
            behind = [out_d[n]]
    sent, by_chip = _wait(ar_small, 'ar_small_wait', behind)
    by_chip = lax.dynamic_update_index_in_dim(by_chip, sent, chip, 0)
    gathered = _exchange(by_chip, 'c', 'ar_small_c')
    rows = gathered.shape[2]
    summed = _sum_slots(gathered.reshape(8, rows, LANES), F32, 'ar_small_sum')
    small_full = dict(zip(small, _unpack(summed, full_shapes)))
    small_grad = {}
    for n in small:
        g = small_full[n]
        if g.shape != weights[n].shape:
            ax = [a for a in range(g.ndim) if g.shape[a] != weights[n].shape[a]][0]
            g = lax.dynamic_slice_in_dim(g, chip * weights[n].shape[ax], weights[n].shape[ax], axis=ax)
        small_grad[n] = g

    shapes = [weights[n].shape for n in small]
    pw_, pm_, pv_, pg_ = (_pack([d[n] for n in small]) for d in (weights, mom1, mom2, small_grad))
    _, sd, sm, sv = _adamw(pw_[None], pm_[None], pv_[None], [pg_], 'adamw_small')
    for n, d_, m_, v_ in zip(small, _unpack(sd[0], shapes), _unpack(sm[0], shapes), _unpack(sv[0], shapes)):
        out_g[n], out_d[n], out_m[n], out_v[n] = small_grad[n], d_, m_, v_

    return (loss, grad_x, *[out_g[n] for n in names], *[out_d[n] for n in names],
            *[out_m[n] for n in names], *[out_v[n] for n in names])
```
